```python
import math
import jax
import jax.numpy as jnp
from jax import lax
import numpy as np

D_MODEL = 1024
BATCH = 2
SEQ = 8192
DEPTH = 2

GRID_W = 64
CTX_LEN = 256
EPS = 1e-6
NEG_INF = -1e30

HEAD_DIM = 64
NA_HEADS = D_MODEL // (2 * HEAD_DIM)
NA_KH = 8
NA_KW = 16
WA_HEADS = D_MODEL // (2 * HEAD_DIM)
WA_KV_HEADS = 2
WA_WINDOW = 128
WA_BLOCK = 128
ROPE_BASE = 10000.0
ATT_SIZES = (NA_HEADS * HEAD_DIM,) * 3 + (WA_HEADS * HEAD_DIM, WA_KV_HEADS * HEAD_DIM, WA_KV_HEADS * HEAD_DIM)
ATT_IN = sum(ATT_SIZES)
ATT_WIDTH = (NA_HEADS + WA_HEADS) * HEAD_DIM

SSD_INNER = D_MODEL
SSD_HEAD_DIM = 64
SSD_HEADS = SSD_INNER // SSD_HEAD_DIM
SSD_GROUPS = 2
SSD_STATE = 128
SSD_CHUNK = 128
CONV_K = 3
SSD_CONV_CH = SSD_INNER + 2 * SSD_GROUPS * SSD_STATE
S5_WIDTH = D_MODEL // 2
S5_GROUP = 16
S5_GROUPS = S5_WIDTH // S5_GROUP
S5_STATE = 64
SSM_SIZES = (SSD_INNER, SSD_CONV_CH, 2 * SSD_HEADS, S5_WIDTH)
SSM_IN = sum(SSM_SIZES)
SSM_WIDTH = SSD_INNER + S5_WIDTH

MOE_GROUPS = 4
MOE_EPG = 8
MOE_EXPERTS = MOE_GROUPS * MOE_EPG
MOE_TOPK = 2
MOE_FF = 512

kernel_name = 'hybrid_diffusion_trunk'


def _points(sizes):
    return tuple(int(s) for s in np.cumsum(sizes)[:-1])


def rmsnorm(x, g):
    xf = x.astype(jnp.float32)
    y = xf * lax.rsqrt(jnp.mean(xf * xf, axis=-1, keepdims=True) + EPS)
    return (y * g.astype(jnp.float32)).astype(x.dtype)


def ada_norm(x, g, shift, scale):
    return rmsnorm(x, g) * (1 + scale) + shift


def rope_2d(x):
    T = x.shape[1]
    t = jnp.arange(T)
    nf = HEAD_DIM // 4
    inv = ROPE_BASE ** (-jnp.arange(nf, dtype=jnp.float32) / nf)

    def rot(xa, pos):
        ang = pos.astype(jnp.float32)[:, None] * inv
        cos = jnp.cos(ang)[:, None, :].astype(x.dtype)
        sin = jnp.sin(ang)[:, None, :].astype(x.dtype)
        x1, x2 = xa[..., :nf], xa[..., nf:]
        return jnp.concatenate([x1 * cos - x2 * sin, x1 * sin + x2 * cos], axis=-1)

    half = HEAD_DIM // 2
    return jnp.concatenate([rot(x[..., :half], t // GRID_W), rot(x[..., half:], t % GRID_W)], axis=-1)


def _sink_column(sink, kvh, grp, lead_shape):
    return jnp.broadcast_to(sink.astype(jnp.float32).reshape(kvh, grp, 1, 1), lead_shape + (1,))


def context_attention(q, k, v, sink):
    Bn, S, H, d = q.shape
    kvh = k.shape[2]
    grp = H // kvh
    qg = q.reshape(Bn, S, kvh, grp, d)
    s = jnp.einsum('bqkgd,bskd->bkgqs', qg, k).astype(jnp.float32)
    if sink is not None:
        s = jnp.concatenate([s, _sink_column(sink, kvh, grp, s.shape[:-1])], axis=-1)
    p = jax.nn.softmax(s, axis=-1)[..., :S].astype(v.dtype)
    o = jnp.einsum('bkgqs,bskd->bqkgd', p, v)
    return o.reshape(Bn, S, H, d)


def neighborhood_attention(q, k, v, kc, vc, rel_bias):
    Bn, T, H, d = q.shape
    rows = T // GRID_W
    kh = min(NA_KH, rows)
    qg = q.reshape(Bn, rows, GRID_W, H, d)
    kg = k.reshape(Bn, rows, GRID_W, H, d)
    vg = v.reshape(Bn, rows, GRID_W, H, d)
    col = jnp.arange(GRID_W)
    col_idx = jnp.clip(col - NA_KW // 2, 0, GRID_W - NA_KW)[:, None] + jnp.arange(NA_KW)
    dcol = col_idx - col[:, None] + NA_KW - 1
    bias_cols = rel_bias.astype(jnp.float32)[:, :, dcol]
    nl = kh * NA_KW

    def row_block(r):
        r0 = jnp.clip(r - kh // 2, 0, rows - kh)
        q_r = lax.dynamic_index_in_dim(qg, r, axis=1, keepdims=False)
        k_r = lax.dynamic_slice_in_dim(kg, r0, kh, axis=1)[:, :, col_idx]
        v_r = lax.dynamic_slice_in_dim(vg, r0, kh, axis=1)[:, :, col_idx]
        drow = r0 + jnp.arange(kh) - r + NA_KH - 1
        bias = jnp.transpose(bias_cols[:, drow], (0, 2, 1, 3))
        s_loc = jnp.einsum('bihd,briwhd->bhirw', q_r, k_r).astype(jnp.float32) + bias
        s_ctx = jnp.einsum('bihd,bchd->bhic', q_r, kc).astype(jnp.float32)
        s = jnp.concatenate([s_loc.reshape(Bn, H, GRID_W, nl), s_ctx], axis=-1)
        p = jax.nn.softmax(s, axis=-1).astype(v.dtype)
        p_loc = p[..., :nl].reshape(Bn, H, GRID_W, kh, NA_KW)
        return (jnp.einsum('bhirw,briwhd->bihd', p_loc, v_r)
                + jnp.einsum('bhic,bchd->bihd', p[..., nl:], vc))

    out = lax.map(row_block, jnp.arange(rows))
    return jnp.moveaxis(out, 0, 1).reshape(Bn, T, H, d)


def window_attention(q, k, v, kc, vc, sink):
    Bn, T, H, d = q.shape
    kvh = k.shape[2]
    grp = H // kvh
    nb = T // WA_BLOCK
    qb = q.reshape(Bn, nb, WA_BLOCK, kvh, grp, d)

    def band(t):
        tp = jnp.pad(t, ((0, 0), (WA_BLOCK, WA_BLOCK), (0, 0), (0, 0))).reshape(Bn, nb + 2, WA_BLOCK, kvh, d)
        return jnp.concatenate([tp[:, :-2], tp[:, 1:-1], tp[:, 2:]], axis=2)

    kb, vb = band(k), band(v)
    qpos = jnp.arange(T).reshape(nb, WA_BLOCK)
    kpos = (jnp.arange(nb)[:, None] - 1) * WA_BLOCK + jnp.arange(3 * WA_BLOCK)
    valid = ((jnp.abs(qpos[:, :, None] - kpos[:, None, :]) <= WA_WINDOW)
             & (kpos >= 0)[:, None, :] & (kpos < T)[:, None, :])
    s_loc = jnp.einsum('bnqkgd,bnskd->bnkgqs', qb, kb).astype(jnp.float32)
    s_loc = jnp.where(valid[None, :, None, None], s_loc, NEG_INF)
    s_ctx = jnp.einsum('bnqkgd,bckd->bnkgqc', qb, kc).astype(jnp.float32)
    s = jnp.concatenate([s_loc, s_ctx, _sink_column(sink, kvh, grp, s_loc.shape[:-1])], axis=-1)
    p = jax.nn.softmax(s, axis=-1)
    nl = 3 * WA_BLOCK
    nc = kc.shape[1]
    p_loc = p[..., :nl].astype(v.dtype)
    p_ctx = p[..., nl:nl + nc].astype(v.dtype)
    o = (jnp.einsum('bnkgqs,bnskd->bnqkgd', p_loc, vb)
         + jnp.einsum('bnkgqc,bckd->bnqkgd', p_ctx, vc))
    return o.reshape(Bn, T, H, d)


def attention_mixer(hl, hc, w_in, w_out, na_qn, na_kn, na_rpb, wa_qn, wa_kn, wa_sink, need_ctx):
    scale = HEAD_DIM ** -0.5

    def project(h, rotary):
        Bn, T, _ = h.shape
        qa, ka, va, qb, kb, vb = jnp.split(h @ w_in, _points(ATT_SIZES), axis=-1)
        heads = lambda t, n: t.reshape(Bn, T, n, HEAD_DIM)
        qa = rmsnorm(heads(qa, NA_HEADS), na_qn) * scale
        ka = rmsnorm(heads(ka, NA_HEADS), na_kn)
        va = heads(va, NA_HEADS)
        qb = rmsnorm(heads(qb, WA_HEADS), wa_qn)
        kb = rmsnorm(heads(kb, WA_KV_HEADS), wa_kn)
        vb = heads(vb, WA_KV_HEADS)
        if rotary:
            qb, kb = rope_2d(qb), rope_2d(kb)
        return qa, ka, va, qb * scale, kb, vb

    def merge(oa, ob):
        Bn, T = oa.shape[:2]
        return jnp.concatenate([oa.reshape(Bn, T, -1), ob.reshape(Bn, T, -1)], axis=-1) @ w_out

    qa_l, ka_l, va_l, qb_l, kb_l, vb_l = project(hl, True)
    qa_c, ka_c, va_c, qb_c, kb_c, vb_c = project(hc, False)
    y_l = merge(neighborhood_attention(qa_l, ka_l, va_l, ka_c, va_c, na_rpb),
                window_attention(qb_l, kb_l, vb_l, kb_c, vb_c, wa_sink))
    y_c = None
    if need_ctx:
        y_c = merge(context_attention(qa_c, ka_c, va_c, None),
                    context_attention(qb_c, kb_c, vb_c, wa_sink))
    return y_l, y_c


def dwconv_centred(x, w, b):
    K = w.shape[0]
    out = lax.conv_general_dilated(x, w[:, None, :], window_strides=(1,), padding=[((K - 1) // 2, K // 2)],
                                   dimension_numbers=('NWC', 'WIO', 'NWC'), feature_group_count=x.shape[-1])
    return out + b


def ssd_scan(x, dt, A, bm, cm, h0, want_y):
    Bn, T, H, P = x.shape
    G, N = bm.shape[2], bm.shape[3]
    hg = H // G
    Q = SSD_CHUNK
    nc = T // Q
    xq = x.reshape(Bn, nc, Q, G, hg, P)
    dtq = dt.reshape(Bn, nc, Q, G, hg)
    bq = bm.reshape(Bn, nc, Q, G, N)
    cq = cm.reshape(Bn, nc, Q, G, N)
    acs = jnp.cumsum(dtq * A.reshape(G, hg), axis=2)
    w_end = jnp.exp(acs[:, :, -1:] - acs) * dtq
    states = jnp.einsum('bcqgn,bcqgh,bcqghp->bcghpn', bq, w_end, xq)
    decay = jnp.exp(acs[:, :, -1])

    def step(h, inp):
        s, dc = inp
        return dc[..., None, None] * h + s, h

    h_fin, h_in = lax.scan(step, h0.reshape(Bn, G, hg, P, N),
                           (jnp.moveaxis(states, 1, 0), jnp.moveaxis(decay, 1, 0)))
    h_fin = h_fin.reshape(Bn, H, P, N)
    if not want_y:
        return None, h_fin
    h_in = jnp.moveaxis(h_in, 0, 1)
    lower = jnp.tril(jnp.ones((Q, Q), dtype=bool))[:, :, None, None]
    seg = acs[:, :, :, None] - acs[:, :, None, :]
    lmat = jnp.exp(jnp.where(lower, seg, NEG_INF))
    cb = jnp.einsum('bcign,bcjgn->bcijg', cq, bq)
    w = cb[..., None] * lmat * dtq[:, :, None]
    y_diag = jnp.einsum('bcijgh,bcjghp->bcighp', w, xq)
    y_off = jnp.einsum('bcign,bcghpn->bcighp', cq, h_in) * jnp.exp(acs)[..., None]
    return (y_diag + y_off).reshape(Bn, T, H, P), h_fin


def s5_discretise(lam_re, lam_im, log_step, b_re, b_im):
    lam = lax.complex(lam_re.astype(jnp.float32), lam_im.astype(jnp.float32))
    step = jnp.exp(log_step.astype(jnp.float32))[:, None]
    a_bar = jnp.exp(lam * step)
    b = lax.complex(b_re.astype(jnp.float32), b_im.astype(jnp.float32))
    b_bar = ((a_bar - 1) / lam)[..., None] * b
    return a_bar, b_bar


def _lin_combine(left, right):
    a_l, b_l = left
    a_r, b_r = right
    return a_r * a_l, a_r * b_l + b_r


def s5_scan(u, a_bar, b_bar, cmat, h0, want_y):
    T = u.shape[1]
    bu = jnp.einsum('gnc,btgc->tbgn', b_bar, u.astype(jnp.complex64))
    bu = bu.at[0].add(a_bar * h0)
    a = jnp.broadcast_to(a_bar, (T, 1) + a_bar.shape)
    _, h = lax.associative_scan(_lin_combine, (a, bu), axis=0)
    y = jnp.einsum('gcn,tbgn->btgc', cmat, h).real if want_y else None
    return y, h[-1]


def ssm_mixer(hl, hc, w_in, w_out, conv_w, conv_b, dt_bias, a_log, d_skip, norm_w,
              lam_re, lam_im, log_step, b_re, b_im, c_re, c_im, s5_dskip, glu_w, glu_b, need_ctx):
    def project(h):
        Bn, T, _ = h.shape
        z, xbc, dtr, u = jnp.split(h @ w_in, _points(SSM_SIZES), axis=-1)
        xbc = jax.nn.silu(dwconv_centred(xbc, conv_w, conv_b))
        xs, bm, cm = jnp.split(xbc, _points((SSD_INNER, SSD_GROUPS * SSD_STATE, SSD_GROUPS * SSD_STATE)), axis=-1)
        xs = xs.reshape(Bn, T, SSD_HEADS, SSD_HEAD_DIM).astype(jnp.float32)
        bm = bm.reshape(Bn, T, SSD_GROUPS, SSD_STATE).astype(jnp.float32)
        cm = cm.reshape(Bn, T, SSD_GROUPS, SSD_STATE).astype(jnp.float32)
        dt = jax.nn.softplus(dtr.reshape(Bn, T, 2, SSD_HEADS).astype(jnp.float32) + dt_bias.astype(jnp.float32))
        u = u.reshape(Bn, T, S5_GROUPS, S5_GROUP).astype(jnp.float32)
        return z, xs, bm, cm, dt, u

    z_l, xs_l, b_l, c_l, dt_l, u_l = project(hl)
    z_c, xs_c, b_c, c_c, dt_c, u_c = project(hc)
    A = -jnp.exp(a_log.astype(jnp.float32))
    Bn = hl.shape[0]
    ssd_l, ssd_c, s5_l, s5_c = [], [], [], []
    for dr in range(2):
        fl = (lambda t: jnp.flip(t, axis=1)) if dr else (lambda t: t)
        h0 = jnp.zeros((Bn, SSD_HEADS, SSD_HEAD_DIM, SSD_STATE), jnp.float32)
        y_c, h_ctx = ssd_scan(fl(xs_c), fl(dt_c[:, :, dr]), A[dr], fl(b_c), fl(c_c), h0, need_ctx)
        y_l, _ = ssd_scan(fl(xs_l), fl(dt_l[:, :, dr]), A[dr], fl(b_l), fl(c_l), h_ctx, True)
        ssd_l.append(fl(y_l))
        a_bar, b_bar = s5_discretise(lam_re[dr], lam_im[dr], log_step[dr], b_re[dr], b_im[dr])
        cmat = lax.complex(c_re[dr].astype(jnp.float32), c_im[dr].astype(jnp.float32))
        s0 = jnp.zeros((Bn, S5_GROUPS, S5_STATE), jnp.complex64)
        v_c, s_ctx = s5_scan(fl(u_c), a_bar, b_bar, cmat, s0, need_ctx)
        v_l, _ = s5_scan(fl(u_l), a_bar, b_bar, cmat, s_ctx, True)
        s5_l.append(fl(v_l))
        if need_ctx:
            ssd_c.append(fl(y_c))
            s5_c.append(fl(v_c))

    def finish(z, xs, ssd_y, u, s5_y):
        Bn_, T = z.shape[:2]
        y = (ssd_y + d_skip.astype(jnp.float32)[:, None] * xs).reshape(Bn_, T, SSD_INNER)
        y = rmsnorm(y * jax.nn.silu(z.astype(jnp.float32)), norm_w)
        v = (s5_y + s5_dskip.astype(jnp.float32).reshape(S5_GROUPS, S5_GROUP) * u).reshape(Bn_, T, S5_WIDTH)
        v = jax.nn.gelu(v)
        v = v * jax.nn.sigmoid(v @ glu_w.astype(jnp.float32) + glu_b.astype(jnp.float32))
        return jnp.concatenate([y, v], axis=-1).astype(w_out.dtype) @ w_out

    out_l = finish(z_l, xs_l, ssd_l[0] + ssd_l[1], u_l, s5_l[0] + s5_l[1])
    out_c = finish(z_c, xs_c, ssd_c[0] + ssd_c[1], u_c, s5_c[0] + s5_c[1]) if need_ctx else None
    return out_l, out_c


def hier_moe(h, w_group, b_group, w_expert, b_expert, w13, w2):
    N = h.shape[0]
    hf = h.astype(jnp.float32)
    pg = jax.nn.softmax(hf @ w_group.astype(jnp.float32) + b_group.astype(jnp.float32), axis=-1)
    g_idx = jnp.argmax(pg, axis=-1)
    p_top = jnp.max(pg, axis=-1)
    le = (hf @ w_expert.astype(jnp.float32) + b_expert.astype(jnp.float32)).reshape(N, MOE_GROUPS, MOE_EPG)
    le_sel = jnp.take_along_axis(le, g_idx[:, None, None], axis=1)[:, 0]
    top_v, top_i = lax.top_k(jax.nn.softmax(le_sel, axis=-1), MOE_TOPK)
    top_v = top_v / jnp.sum(top_v, axis=-1, keepdims=True)
    within = jnp.sum(jax.nn.one_hot(top_i, MOE_EPG, dtype=jnp.float32) * top_v[..., None], axis=1)
    combine = (jax.nn.one_hot(g_idx, MOE_GROUPS, dtype=jnp.float32)[:, :, None]
               * within[:, None, :] * p_top[:, None, None])
    out = jnp.zeros((N, h.shape[1]), jnp.float32)
    for g in range(MOE_GROUPS):
        sl = slice(g * MOE_EPG, (g + 1) * MOE_EPG)
        a, b = jnp.split(jnp.einsum('nd,edf->nef', h, w13[sl]), 2, axis=-1)
        act = jax.nn.silu(a) * b * combine[:, g, :, None].astype(h.dtype)
        out = out + jnp.einsum('nef,efd->nd', act, w2[sl]).astype(jnp.float32)
    return out.astype(h.dtype)


def setup_inputs(seed: int = 0) -> dict:
    key = jax.random.key(seed)
    ks = iter(jax.random.split(key, 64))
    nrm = lambda shape, s: jax.random.normal(next(ks), shape, jnp.float32) * s
    D = D_MODEL
    NE = (DEPTH + 1) // 2
    NO = DEPTH // 2
    dt0 = jnp.exp(jax.random.uniform(next(ks), (NO, 2, SSD_HEADS), minval=math.log(1e-3), maxval=math.log(1e-1)))
    n_idx = jnp.arange(S5_STATE, dtype=jnp.float32)
    return {
        'x': nrm((BATCH, SEQ, D), 1.0),
        'c': nrm((BATCH, D), 1.0),
        'ctx': nrm((BATCH, CTX_LEN, D), 1.0),
        'c_ctx': nrm((D,), 1.0),
        'mod_w': nrm((DEPTH, D, 6 * D), 0.5 * D ** -0.5),
        'mod_b': nrm((DEPTH, 6 * D), 0.02),
        'norm_mix': 1.0 + nrm((DEPTH, D), 0.1),
        'norm_ffn': 1.0 + nrm((DEPTH, D), 0.1),
        'att_w_in': nrm((NE, D, ATT_IN), D ** -0.5),
        'att_w_out': nrm((NE, ATT_WIDTH, D), ATT_WIDTH ** -0.5),
        'na_q_norm': 1.0 + nrm((NE, HEAD_DIM), 0.1),
        'na_k_norm': 1.0 + nrm((NE, HEAD_DIM), 0.1),
        'na_rel_bias': nrm((NE, NA_HEADS, 2 * NA_KH - 1, 2 * NA_KW - 1), 0.5),
        'wa_q_norm': 1.0 + nrm((NE, HEAD_DIM), 0.1),
        'wa_k_norm': 1.0 + nrm((NE, HEAD_DIM), 0.1),
        'wa_sink': nrm((NE, WA_HEADS), 0.5),
        'ssm_w_in': nrm((NO, D, SSM_IN), D ** -0.5),
        'ssm_w_out': nrm((NO, SSM_WIDTH, D), SSM_WIDTH ** -0.5),
        'ssd_conv_w': nrm((NO, CONV_K, SSD_CONV_CH), CONV_K ** -0.5),
        'ssd_conv_b': nrm((NO, SSD_CONV_CH), 0.01),
        'ssd_dt_bias': dt0 + jnp.log(-jnp.expm1(-dt0)),
        'ssd_a_log': jnp.log(jax.random.uniform(next(ks), (NO, 2, SSD_HEADS), minval=1.0, maxval=16.0)),
        'ssd_d': 1.0 + nrm((NO, SSD_HEADS), 0.1),
        'ssd_norm': 1.0 + nrm((NO, SSD_INNER), 0.1),
        's5_lambda_re': -0.5 + nrm((NO, 2, S5_GROUPS, S5_STATE), 0.01),
        's5_lambda_im': jnp.pi * n_idx + nrm((NO, 2, S5_GROUPS, S5_STATE), 0.01),
        's5_log_step': jax.random.uniform(next(ks), (NO, 2, S5_GROUPS), minval=math.log(1e-3), maxval=math.log(1e-1)),
        's5_b_re': nrm((NO, 2, S5_GROUPS, S5_STATE, S5_GROUP), (2 * S5_GROUP) ** -0.5),
        's5_b_im': nrm((NO, 2, S5_GROUPS, S5_STATE, S5_GROUP), (2 * S5_GROUP) ** -0.5),
        's5_c_re': nrm((NO, 2, S5_GROUPS, S5_GROUP, S5_STATE), (2 * S5_STATE) ** -0.5),
        's5_c_im': nrm((NO, 2, S5_GROUPS, S5_GROUP, S5_STATE), (2 * S5_STATE) ** -0.5),
        's5_d': nrm((NO, S5_WIDTH), 1.0),
        's5_glu_w': nrm((NO, S5_WIDTH, S5_WIDTH), S5_WIDTH ** -0.5),
        's5_glu_b': nrm((NO, S5_WIDTH), 0.01),
        'moe_w_group': nrm((DEPTH, D, MOE_GROUPS), D ** -0.5),
        'moe_b_group': nrm((DEPTH, MOE_GROUPS), 0.01),
        'moe_w_expert': nrm((DEPTH, D, MOE_EXPERTS), D ** -0.5),
        'moe_b_expert': nrm((DEPTH, MOE_EXPERTS), 0.01),
        'moe_w13': nrm((DEPTH, MOE_EXPERTS, D, 2 * MOE_FF), D ** -0.5),
        'moe_w2': nrm((DEPTH, MOE_EXPERTS, MOE_FF, D), MOE_FF ** -0.5),
    }


def reference(x, c, ctx, c_ctx, mod_w, mod_b, norm_mix, norm_ffn, att_w_in, att_w_out, na_q_norm, na_k_norm,
              na_rel_bias, wa_q_norm, wa_k_norm, wa_sink, ssm_w_in, ssm_w_out, ssd_conv_w, ssd_conv_b, ssd_dt_bias,
              ssd_a_log, ssd_d, ssd_norm, s5_lambda_re, s5_lambda_im, s5_log_step, s5_b_re, s5_b_im, s5_c_re, s5_c_im,
              s5_d, s5_glu_w, s5_glu_b, moe_w_group, moe_b_group, moe_w_expert, moe_b_expert, moe_w13, moe_w2):
    Bn, L, D = x.shape
    xl, xc = x, ctx
    for i in range(DEPTH):
        last = i == DEPTH - 1
        j = i // 2
        sh1_l, sc1_l, g1_l, sh2_l, sc2_l, g2_l = jnp.split(
            (jax.nn.silu(c) @ mod_w[i] + mod_b[i])[:, None, :], 6, axis=-1)
        sh1_c, sc1_c, g1_c, sh2_c, sc2_c, g2_c = jnp.split(
            (jax.nn.silu(c_ctx) @ mod_w[i] + mod_b[i])[None, None, :], 6, axis=-1)
        hl = ada_norm(xl, norm_mix[i], sh1_l, sc1_l)
        hc = ada_norm(xc, norm_mix[i], sh1_c, sc1_c)
        if i % 2 == 0:
            yl, yc = attention_mixer(hl, hc, att_w_in[j], att_w_out[j], na_q_norm[j], na_k_norm[j], na_rel_bias[j],
                                     wa_q_norm[j], wa_k_norm[j], wa_sink[j], not last)
        else:
            yl, yc = ssm_mixer(hl, hc, ssm_w_in[j], ssm_w_out[j], ssd_conv_w[j], ssd_conv_b[j], ssd_dt_bias[j],
                               ssd_a_log[j], ssd_d[j], ssd_norm[j], s5_lambda_re[j], s5_lambda_im[j],
                               s5_log_step[j], s5_b_re[j], s5_b_im[j], s5_c_re[j], s5_c_im[j], s5_d[j],
                               s5_glu_w[j], s5_glu_b[j], not last)
        xl = xl + g1_l * yl
        hl = ada_norm(xl, norm_ffn[i], sh2_l, sc2_l).reshape(Bn * L, D)
        moe_args = (moe_w_group[i], moe_b_group[i], moe_w_expert[i], moe_b_expert[i], moe_w13[i], moe_w2[i])
        if last:
            fl = hier_moe(hl, *moe_args)
        else:
            xc = xc + g1_c * yc
            hc = ada_norm(xc, norm_ffn[i], sh2_c, sc2_c).reshape(-1, D)
            f = hier_moe(jnp.concatenate([hl, hc], axis=0), *moe_args)
            fl = f[:Bn * L]
            xc = xc + g2_c * f[Bn * L:].reshape(xc.shape)
        xl = xl + g2_l * fl.reshape(Bn, L, D)
    return xl
```

```python
import functools
import math

import jax
import jax.numpy as jnp
import numpy as np
from jax import lax
from jax.experimental import pallas as pl
from jax.experimental.pallas import tpu as pltpu

F32 = jnp.float32
BF16 = jnp.bfloat16

EPS = 1e-6
NEG_INF = -1e30
GRID_W = 64
HEAD_DIM = 64
NA_KH = 8
NA_KW = 16
WA_BLOCK = 128
ROPE_BASE = 10000.0
SSD_CHUNK = 128
S5_GROUP = 16
S5_STATE = 64
MOE_GROUPS = 4
MOE_EPG = 8
MOE_EXPERTS = MOE_GROUPS * MOE_EPG

LANES = 128
ROW_TILE = 512
MOE_TILE = 256
VMEM_LIMIT = 56 * 1024 * 1024


def _cparams(n_axes, vmem=VMEM_LIMIT):
    return pltpu.CompilerParams(dimension_semantics=("arbitrary",) * n_axes, vmem_limit_bytes=vmem)


def _sigmoid(x):
    return 1.0 / (1.0 + jnp.exp(-x))


def _silu(x):
    return x * _sigmoid(x)


def _rms(x, eps=EPS):
    return x * lax.rsqrt(jnp.mean(x * x, axis=-1, keepdims=True) + eps)


def _ada_norm(x, g, shift, scale):
    return (_rms(x) * g) * (1.0 + scale) + shift


def _mod_kernel(c_ref, w_ref, b_ref, o_ref):
    a = _silu(c_ref[...])
    o_ref[...] = jnp.dot(a, w_ref[...], preferred_element_type=F32, precision=lax.Precision.HIGHEST) + b_ref[...]


def _modulation(cm, mod_w, mod_b):
    depth, d, n6 = mod_w.shape
    tn = 1024
    return pl.pallas_call(
        _mod_kernel,
        grid=(depth, n6 // tn),
        in_specs=[pl.BlockSpec((8, d), lambda l, j: (0, 0)),
                  pl.BlockSpec((None, d, tn), lambda l, j: (l, 0, j)),
                  pl.BlockSpec((None, 1, tn), lambda l, j: (l, 0, j))],
        out_specs=pl.BlockSpec((None, 8, tn), lambda l, j: (l, 0, j)),
        out_shape=jax.ShapeDtypeStruct((depth, 8, n6), F32),
        compiler_params=_cparams(2),
        name="modulation",
    )(cm, mod_w, mod_b.reshape(depth, 1, n6))


class _Rows:
    def __init__(self, B, T, C, D, tm):
        assert T % tm == 0 and (B * C) % tm == 0
        self.B, self.T, self.C, self.D, self.tm = B, T, C, D, tm
        self.tpb = T // tm
        self.nlat = B * self.tpb
        self.nctx = (B * C) // tm
        self.ntot = self.nlat + self.nctx
        self.rows = B * (T + C)

    def group(self, i):
        return jnp.where(i < self.nlat, i // self.tpb, self.B)


def _mod_spec(rw, col):
    return pl.BlockSpec((None, 1, rw.D), lambda i, *_: (rw.group(i), 0, col))


def _seg_norm(y, seg, gcol):
    ss = jnp.dot((y * y).astype(BF16), seg, preferred_element_type=F32)
    return y * lax.rsqrt(ss + EPS) * gcol


def _rope(y, cos, sin):
    w = y.shape[-1]
    lane = lax.broadcasted_iota(jnp.int32, y.shape, 1)
    first = (lane % 32) < 16
    partner = jnp.where(first, pltpu.roll(y, w - 16, 1), pltpu.roll(y, 16, 1))
    return y * cos + partner * sin


def _dup_halves(k):
    lane = lax.broadcasted_iota(jnp.int32, k.shape, 1)
    sw = pltpu.roll(k, 64, 1)
    return jnp.where(lane < 64, k, sw), jnp.where(lane < 64, sw, k)


def _att_inproj_kernel(nlat, xl_ref, xc_ref, g_ref, sh_ref, sc_ref, w_ref, gcol_ref, cos_ref, sin_ref, seg_ref,
                       o_ref, h_scr):
    i = pl.program_id(0)
    x = jnp.where(i < nlat, xl_ref[...], xc_ref[...])
    h_scr[...] = _ada_norm(x, g_ref[...], sh_ref[...], sc_ref[...]).astype(BF16)
    seg = seg_ref[...]
    cos2 = jnp.concatenate([cos_ref[...], cos_ref[...]], axis=1)
    sin2 = jnp.concatenate([sin_ref[...], sin_ref[...]], axis=1)
    for c in range(9):
        c0 = c * 256
        y = jnp.dot(h_scr[...], w_ref[:, c0:c0 + 256], preferred_element_type=F32)
        gcol = gcol_ref[:, c0:c0 + 256]
        if c in (0, 1, 2, 3):
            o_ref[:, c0:c0 + 256] = _seg_norm(y, seg, gcol).astype(BF16)
        elif c in (4, 5):
            o_ref[:, c0:c0 + 256] = y.astype(BF16)
        elif c in (6, 7):
            o_ref[:, c0:c0 + 256] = _rope(_seg_norm(y, seg, gcol), cos2, sin2).astype(BF16)
        else:
            lane = lax.broadcasted_iota(jnp.int32, y.shape, 1)
            yk = jnp.where(lane < 128, _seg_norm(y, seg, gcol), y)
            yr = jnp.where(lane < 128, _rope(yk, cos2, sin2), yk)
            k0, k1 = _dup_halves(yr[:, :128])
            v0, v1 = _dup_halves(yr[:, 128:])
            o_ref[:, 2048:2176] = k0.astype(BF16)
            o_ref[:, 2176:2304] = k1.astype(BF16)
            o_ref[:, 2304:2432] = v0.astype(BF16)
            o_ref[:, 2432:2560] = v1.astype(BF16)


def _rope_tables(T, tm):
    t = np.arange(T)
    d = np.arange(HEAD_DIM)
    nf = HEAD_DIM // 4
    inv = jnp.asarray(ROPE_BASE, F32) ** (-jnp.arange(nf, dtype=F32) / nf)
    pos = np.where((d // 32 == 0)[None, :], (t // GRID_W)[:, None], (t % GRID_W)[:, None])
    ang = jnp.asarray(pos, F32) * inv[d % nf][None, :]
    sign = np.where((d % 32) < 16, -1.0, 1.0).astype(np.float32)
    cos = jnp.cos(ang)
    sin = jnp.sin(ang) * sign[None, :]
    cos = jnp.concatenate([cos, jnp.ones((tm, HEAD_DIM), F32)], axis=0)
    sin = jnp.concatenate([sin, jnp.zeros((tm, HEAD_DIM), F32)], axis=0)
    return jnp.tile(cos, (1, 2)), jnp.tile(sin, (1, 2))


def _att_inproj(rw, xl, xc, mods, norm_g, w_in, na_qn, na_kn, wa_qn, wa_kn):
    D, tm = rw.D, rw.tm
    scale = HEAD_DIM ** -0.5
    gcol = jnp.concatenate([jnp.tile(na_qn * scale, 8), jnp.tile(na_kn, 8), jnp.ones((512,), F32),
                            jnp.tile(wa_qn * scale, 8), jnp.tile(wa_kn, 2), jnp.ones((128,), F32)])[None, :]
    cos, sin = _rope_tables(rw.T, tm)
    segn = np.arange(256) // 64
    seg = jnp.asarray((segn[:, None] == segn[None, :]).astype(np.float32) / 64.0, BF16)
    nlat, tpb = rw.nlat, rw.tpb
    return pl.pallas_call(
        functools.partial(_att_inproj_kernel, nlat),
        grid=(rw.ntot,),
        in_specs=[pl.BlockSpec((tm, D), lambda i: (jnp.minimum(i, nlat - 1), 0)),
                  pl.BlockSpec((tm, D), lambda i: (jnp.maximum(i - nlat, 0), 0)),
                  pl.BlockSpec((1, D), lambda i: (0, 0)),
                  _mod_spec(rw, 0), _mod_spec(rw, 1),
                  pl.BlockSpec((D, 2304), lambda i: (0, 0)),
                  pl.BlockSpec((1, 2304), lambda i: (0, 0)),
                  pl.BlockSpec((tm, 128), lambda i: (jnp.where(i < nlat, i % tpb, tpb), 0)),
                  pl.BlockSpec((tm, 128), lambda i: (jnp.where(i < nlat, i % tpb, tpb), 0)),
                  pl.BlockSpec((256, 256), lambda i: (0, 0))],
        out_specs=pl.BlockSpec((tm, 2560), lambda i: (i, 0)),
        out_shape=jax.ShapeDtypeStruct((rw.rows, 2560), BF16),
        scratch_shapes=[pltpu.VMEM((tm, D), BF16)],
        compiler_params=_cparams(1),
        name="att_inproj",
    )(xl, xc, norm_g[None, :], mods, mods, w_in.astype(BF16), gcol, cos, sin, seg)


def _route(lg, lt, carry):
    lane = lax.broadcasted_iota(jnp.int32, lg.shape, 1).astype(F32)
    gm = lane < MOE_GROUPS
    mg = jnp.max(jnp.where(gm, lg, NEG_INF), axis=-1, keepdims=True)
    eg = jnp.where(gm, jnp.exp(jnp.where(gm, lg, NEG_INF) - mg), 0.0)
    pg = eg / jnp.sum(eg, axis=-1, keepdims=True)
    ptop = jnp.max(pg, axis=-1, keepdims=True)
    gidx = jnp.min(jnp.where(gm & (pg == ptop), lane, 1e9), axis=-1, keepdims=True)
    lo = MOE_GROUPS + MOE_EPG * gidx
    em = (lane >= lo) & (lane < lo + MOE_EPG)
    le = jnp.where(em, lg, NEG_INF)
    ee = jnp.where(em, jnp.exp(le - jnp.max(le, axis=-1, keepdims=True)), 0.0)
    pe = ee / jnp.sum(ee, axis=-1, keepdims=True)
    v1 = jnp.max(jnp.where(em, pe, -1.0), axis=-1, keepdims=True)
    i1 = jnp.min(jnp.where(em & (pe == v1), lane, 1e9), axis=-1, keepdims=True)
    em2 = em & (lane != i1)
    v2 = jnp.max(jnp.where(em2, pe, -1.0), axis=-1, keepdims=True)
    i2 = jnp.min(jnp.where(em2 & (pe == v2), lane, 1e9), axis=-1, keepdims=True)
    den = v1 + v2
    w1 = v1 / den * ptop
    w2 = v2 / den * ptop
    e1 = i1 - MOE_GROUPS
    e2 = i2 - MOE_GROUPS
    m1 = lane == e1
    m2 = lane == e2
    oh = jnp.where(m1 | m2, 1.0, 0.0)
    cnt = jnp.dot(lt, oh.astype(BF16), preferred_element_type=F32) + carry
    r1 = jnp.sum(jnp.where(m1, cnt, 0.0), axis=-1, keepdims=True)
    r2 = jnp.sum(jnp.where(m2, cnt, 0.0), axis=-1, keepdims=True)
    route = jnp.where(lane == 0, e1, jnp.where(lane == 1, e2, jnp.where(lane == 2, w1, jnp.where(
        lane == 3, w2, jnp.where(lane == 4, r1, jnp.where(lane == 5, r2, 0.0))))))
    return route, carry + jnp.sum(oh, axis=0, keepdims=True)


def _post_mixer(i, x, y, g1, gn, sh2, sc2, wr_ref, br_ref, lt_ref, xo_ref, h2_ref, rt_ref, cnt_ref, carry):
    xn = x + g1 * y
    xo_ref[...] = xn
    h2 = _ada_norm(xn, gn, sh2, sc2)
    h2_ref[...] = h2.astype(BF16)
    lg = jnp.dot(h2, wr_ref[...], preferred_element_type=F32, precision=lax.Precision.HIGHEST) + br_ref[...]

    @pl.when(i == 0)
    def _():
        carry[...] = jnp.zeros_like(carry)

    route, newc = _route(lg, lt_ref[...], carry[...])
    rt_ref[...] = route
    carry[...] = newc
    cnt_ref[...] = newc


def _att_outproj_kernel(nlat, na_ref, wa_ref, cx_ref, xl_ref, xc_ref, w_ref, g1_ref, gn_ref, sh2_ref, sc2_ref,
                        wr_ref, br_ref, lt_ref, xo_ref, h2_ref, rt_ref, cnt_ref, carry):
    i = pl.program_id(0)
    lat = i < nlat
    mix = jnp.where(lat, jnp.concatenate([na_ref[...], wa_ref[...]], axis=1), cx_ref[...])
    y = jnp.dot(mix, w_ref[...], preferred_element_type=F32)
    x = jnp.where(lat, xl_ref[...], xc_ref[...])
    _post_mixer(i, x, y, g1_ref[...], gn_ref[...], sh2_ref[...], sc2_ref[...], wr_ref, br_ref, lt_ref,
                xo_ref, h2_ref, rt_ref, cnt_ref, carry)


def _router_weights(w_group, b_group, w_expert, b_expert):
    D = w_group.shape[0]
    pad = LANES - MOE_GROUPS - MOE_EXPERTS
    wr = jnp.concatenate([w_group, w_expert, jnp.zeros((D, pad), F32)], axis=1)
    br = jnp.concatenate([b_group, b_expert, jnp.zeros((pad,), F32)])[None, :]
    return wr, br


def _lower_tri(tm):
    r = np.arange(tm)
    return jnp.asarray((r[None, :] < r[:, None]).astype(np.float32), BF16)


def _post_specs(rw):
    D, tm = rw.D, rw.tm
    return ([pl.BlockSpec((1, D), lambda i: (0, 0)), _mod_spec(rw, 3), _mod_spec(rw, 4),
             pl.BlockSpec((D, LANES), lambda i: (0, 0)), pl.BlockSpec((1, LANES), lambda i: (0, 0)),
             pl.BlockSpec((tm, tm), lambda i: (0, 0))],
            [pl.BlockSpec((tm, D), lambda i: (i, 0)), pl.BlockSpec((tm, D), lambda i: (i, 0)),
             pl.BlockSpec((tm, LANES), lambda i: (i, 0)), pl.BlockSpec((1, LANES), lambda i: (0, 0))])


def _post_shapes(nrows, D):
    return [jax.ShapeDtypeStruct((nrows, D), F32), jax.ShapeDtypeStruct((nrows, D), BF16),
            jax.ShapeDtypeStruct((nrows, LANES), F32), jax.ShapeDtypeStruct((1, LANES), F32)]


def _att_outproj(rw, na, wa, cx, xl, xc, w_out, mods, norm_ffn, wr, br):
    D, tm, nlat = rw.D, rw.tm, rw.nlat
    post_in, post_out = _post_specs(rw)
    latmap = lambda i: (jnp.minimum(i, nlat - 1), 0)
    ctxmap = lambda i: (jnp.maximum(i - nlat, 0), 0)
    return pl.pallas_call(
        functools.partial(_att_outproj_kernel, nlat),
        grid=(rw.ntot,),
        in_specs=[pl.BlockSpec((tm, 512), latmap), pl.BlockSpec((tm, 512), latmap), pl.BlockSpec((tm, D), ctxmap),
                  pl.BlockSpec((tm, D), latmap), pl.BlockSpec((tm, D), ctxmap),
                  pl.BlockSpec((D, D), lambda i: (0, 0)), _mod_spec(rw, 2)] + post_in,
        out_specs=post_out,
        out_shape=_post_shapes(rw.rows, D),
        scratch_shapes=[pltpu.VMEM((1, LANES), F32)],
        compiler_params=_cparams(1),
        name="att_outproj_router",
    )(na, wa, cx, xl, xc, w_out.astype(BF16), mods, norm_ffn[None, :], mods, mods, wr, br, _lower_tri(tm))


def _moe_kernel(te_ref, nu_ref, xs_ref, w13_ref, w2_ref, o_ref, w13b, w2b):
    i = pl.program_id(0)
    prev = te_ref[jnp.maximum(i - 1, 0)]
    changed = (i == 0) | (te_ref[i] != prev)

    @pl.when(changed)
    def _():
        w13b[...] = w13_ref[...].astype(BF16)
        w2b[...] = w2_ref[...].astype(BF16)

    @pl.when(i < nu_ref[0])
    def _():
        ff = w2b.shape[0]
        a13 = jnp.dot(xs_ref[...], w13b[...], preferred_element_type=F32)
        act = _silu(a13[:, :ff]) * a13[:, ff:]
        o_ref[...] = jnp.dot(act.astype(BF16), w2b[...], preferred_element_type=F32).astype(BF16)

    @pl.when(i >= nu_ref[0])
    def _():
        o_ref[...] = jnp.zeros_like(o_ref)


def _moe(h2, route, counts, w13, w2):
    N, D = h2.shape
    E, _, F2 = w13.shape
    tg = MOE_TILE
    nt = (2 * N) // tg + E
    e = route[:, 0:2].astype(jnp.int32)
    rank = route[:, 4:6].astype(jnp.int32)
    cnt = counts[0, :E].astype(jnp.int32)
    ntile_e = (cnt + tg - 1) // tg
    tile_end = jnp.cumsum(ntile_e)
    offs = (tile_end - ntile_e) * tg
    dest = offs[e] + rank
    src = jnp.zeros((nt * tg,), jnp.int32).at[dest.reshape(-1)].set(jnp.repeat(jnp.arange(N, dtype=jnp.int32), 2))
    te = jnp.minimum(jnp.searchsorted(tile_end, jnp.arange(nt, dtype=jnp.int32), side="right"), E - 1).astype(jnp.int32)
    nu = tile_end[-1:].astype(jnp.int32)
    te = jnp.where(jnp.arange(nt) < nu[0], te, te[jnp.maximum(nu[0] - 1, 0)])
    xs = jnp.take(h2, src, axis=0)
    ys = pl.pallas_call(
        _moe_kernel,
        grid_spec=pltpu.PrefetchScalarGridSpec(
            num_scalar_prefetch=2,
            grid=(nt,),
            in_specs=[pl.BlockSpec((tg, D), lambda i, te, nu: (i, 0)),
                      pl.BlockSpec((None, D, F2), lambda i, te, nu: (te[i], 0, 0)),
                      pl.BlockSpec((None, F2 // 2, D), lambda i, te, nu: (te[i], 0, 0))],
            out_specs=pl.BlockSpec((tg, D), lambda i, te, nu: (i, 0)),
            scratch_shapes=[pltpu.VMEM((D, F2), BF16), pltpu.VMEM((F2 // 2, D), BF16)]),
        out_shape=jax.ShapeDtypeStruct((nt * tg, D), BF16),
        compiler_params=_cparams(1),
        name="moe_experts",
    )(te, nu, xs, w13, w2)
    return jnp.take(ys, dest[:, 0], axis=0), jnp.take(ys, dest[:, 1], axis=0)


def _combine_kernel(x_ref, y1_ref, y2_ref, rt_ref, g2_ref, o_ref):
    rt = rt_ref[...]
    f = rt[:, 2:3] * y1_ref[...].astype(F32) + rt[:, 3:4] * y2_ref[...].astype(F32)
    o_ref[...] = x_ref[...] + g2_ref[...] * f


def _combine(rw, ntiles, xall, y1, y2, route, mods):
    D, tm = rw.D, rw.tm
    row = lambda i: (i, 0)
    return pl.pallas_call(
        _combine_kernel,
        grid=(ntiles,),
        in_specs=[pl.BlockSpec((tm, D), row), pl.BlockSpec((tm, D), row), pl.BlockSpec((tm, D), row),
                  pl.BlockSpec((tm, LANES), row), _mod_spec(rw, 5)],
        out_specs=pl.BlockSpec((tm, D), row),
        out_shape=jax.ShapeDtypeStruct((ntiles * tm, D), F32),
        compiler_params=_cparams(1),
        name="moe_combine",
    )(xall, y1, y2, route, mods)


def _plain_outproj_kernel(mix_ref, x_ref, w_ref, g1_ref, gn_ref, sh2_ref, sc2_ref, wr_ref, br_ref, lt_ref,
                          xo_ref, h2_ref, rt_ref, cnt_ref, carry):
    i = pl.program_id(0)
    y = jnp.dot(mix_ref[...], w_ref[...], preferred_element_type=F32)
    _post_mixer(i, x_ref[...], y, g1_ref[...], gn_ref[...], sh2_ref[...], sc2_ref[...], wr_ref, br_ref, lt_ref,
                xo_ref, h2_ref, rt_ref, cnt_ref, carry)


def _plain_outproj(rw, ntiles, mix, xall, w_out, mods, norm_ffn, wr, br):
    D, tm = rw.D, rw.tm
    K = mix.shape[1]
    post_in, post_out = _post_specs(rw)
    row = lambda i: (i, 0)
    return pl.pallas_call(
        _plain_outproj_kernel,
        grid=(ntiles,),
        in_specs=[pl.BlockSpec((tm, K), row), pl.BlockSpec((tm, D), row),
                  pl.BlockSpec((K, D), lambda i: (0, 0)), _mod_spec(rw, 2)] + post_in,
        out_specs=post_out,
        out_shape=_post_shapes(ntiles * tm, D),
        scratch_shapes=[pltpu.VMEM((1, LANES), F32)],
        compiler_params=_cparams(1),
        name="ssm_outproj_router",
    )(mix, xall, w_out.astype(BF16), mods, norm_ffn[None, :], mods, mods, wr, br, _lower_tri(tm))


NA_HEADS = 8
WA_HEADS = 8
WA_KV_HEADS = 2
WA_WINDOW = 128


def _sink_column(sink, kvh, grp, lead_shape):
    return jnp.broadcast_to(sink.astype(jnp.float32).reshape(kvh, grp, 1, 1), lead_shape + (1,))


def _j_context_attention(q, k, v, sink):
    Bn, S, H, d = q.shape
    kvh = k.shape[2]
    grp = H // kvh
    qg = q.reshape(Bn, S, kvh, grp, d)
    s = jnp.einsum('bqkgd,bskd->bkgqs', qg, k).astype(jnp.float32)
    if sink is not None:
        s = jnp.concatenate([s, _sink_column(sink, kvh, grp, s.shape[:-1])], axis=-1)
    p = jax.nn.softmax(s, axis=-1)[..., :S].astype(v.dtype)
    o = jnp.einsum('bkgqs,bskd->bqkgd', p, v)
    return o.reshape(Bn, S, H, d)


def _j_neighborhood_attention(q, k, v, kc, vc, rel_bias):
    Bn, T, H, d = q.shape
    rows = T // GRID_W
    kh = min(NA_KH, rows)
    qg = q.reshape(Bn, rows, GRID_W, H, d)
    kg = k.reshape(Bn, rows, GRID_W, H, d)
    vg = v.reshape(Bn, rows, GRID_W, H, d)
    col = jnp.arange(GRID_W)
    col_idx = jnp.clip(col - NA_KW // 2, 0, GRID_W - NA_KW)[:, None] + jnp.arange(NA_KW)
    dcol = col_idx - col[:, None] + NA_KW - 1
    bias_cols = rel_bias.astype(jnp.float32)[:, :, dcol]
    nl = kh * NA_KW

    def row_block(r):
        r0 = jnp.clip(r - kh // 2, 0, rows - kh)
        q_r = lax.dynamic_index_in_dim(qg, r, axis=1, keepdims=False)
        k_r = lax.dynamic_slice_in_dim(kg, r0, kh, axis=1)[:, :, col_idx]
        v_r = lax.dynamic_slice_in_dim(vg, r0, kh, axis=1)[:, :, col_idx]
        drow = r0 + jnp.arange(kh) - r + NA_KH - 1
        bias = jnp.transpose(bias_cols[:, drow], (0, 2, 1, 3))
        s_loc = jnp.einsum('bihd,briwhd->bhirw', q_r, k_r).astype(jnp.float32) + bias
        s_ctx = jnp.einsum('bihd,bchd->bhic', q_r, kc).astype(jnp.float32)
        s = jnp.concatenate([s_loc.reshape(Bn, H, GRID_W, nl), s_ctx], axis=-1)
        p = jax.nn.softmax(s, axis=-1).astype(v.dtype)
        p_loc = p[..., :nl].reshape(Bn, H, GRID_W, kh, NA_KW)
        return (jnp.einsum('bhirw,briwhd->bihd', p_loc, v_r)
                + jnp.einsum('bhic,bchd->bihd', p[..., nl:], vc))

    out = lax.map(row_block, jnp.arange(rows))
    return jnp.moveaxis(out, 0, 1).reshape(Bn, T, H, d)


def _j_window_attention(q, k, v, kc, vc, sink):
    Bn, T, H, d = q.shape
    kvh = k.shape[2]
    grp = H // kvh
    nb = T // WA_BLOCK
    qb = q.reshape(Bn, nb, WA_BLOCK, kvh, grp, d)

    def band(t):
        tp = jnp.pad(t, ((0, 0), (WA_BLOCK, WA_BLOCK), (0, 0), (0, 0))).reshape(Bn, nb + 2, WA_BLOCK, kvh, d)
        return jnp.concatenate([tp[:, :-2], tp[:, 1:-1], tp[:, 2:]], axis=2)

    kb, vb = band(k), band(v)
    qpos = jnp.arange(T).reshape(nb, WA_BLOCK)
    kpos = (jnp.arange(nb)[:, None] - 1) * WA_BLOCK + jnp.arange(3 * WA_BLOCK)
    valid = ((jnp.abs(qpos[:, :, None] - kpos[:, None, :]) <= WA_WINDOW)
             & (kpos >= 0)[:, None, :] & (kpos < T)[:, None, :])
    s_loc = jnp.einsum('bnqkgd,bnskd->bnkgqs', qb, kb).astype(jnp.float32)
    s_loc = jnp.where(valid[None, :, None, None], s_loc, NEG_INF)
    s_ctx = jnp.einsum('bnqkgd,bckd->bnkgqc', qb, kc).astype(jnp.float32)
    s = jnp.concatenate([s_loc, s_ctx, _sink_column(sink, kvh, grp, s_loc.shape[:-1])], axis=-1)
    p = jax.nn.softmax(s, axis=-1)
    nl = 3 * WA_BLOCK
    nc = kc.shape[1]
    p_loc = p[..., :nl].astype(v.dtype)
    p_ctx = p[..., nl:nl + nc].astype(v.dtype)
    o = (jnp.einsum('bnkgqs,bnskd->bnqkgd', p_loc, vb)
         + jnp.einsum('bnkgqc,bckd->bnqkgd', p_ctx, vc))
    return o.reshape(Bn, T, H, d)


def _j_attention(rw, qkv, na_rpb, wa_sink):
    B, T, C = rw.B, rw.T, rw.C
    q = qkv.astype(F32)
    lat, ctx = q[:B * T].reshape(B, T, -1), q[B * T:].reshape(B, C, -1)

    def parts(t):
        n = t.shape[1]
        hd = lambda a, h: a.reshape(B, n, h, HEAD_DIM)
        kb = jnp.concatenate([t[..., 2048:2112], t[..., 2176:2240]], axis=-1)
        vb = jnp.concatenate([t[..., 2304:2368], t[..., 2432:2496]], axis=-1)
        return (hd(t[..., 0:512], 8), hd(t[..., 512:1024], 8), hd(t[..., 1024:1536], 8),
                hd(t[..., 1536:2048], 8), hd(kb, 2), hd(vb, 2))

    qa_l, ka_l, va_l, qb_l, kb_l, vb_l = parts(lat)
    qa_c, ka_c, va_c, qb_c, kb_c, vb_c = parts(ctx)
    na = _j_neighborhood_attention(qa_l, ka_l, va_l, ka_c, va_c, na_rpb).reshape(B * T, 512)
    wa = _j_window_attention(qb_l, kb_l, vb_l, kb_c, vb_c, wa_sink).reshape(B * T, 512)
    cx = jnp.concatenate([_j_context_attention(qa_c, ka_c, va_c, None).reshape(B * C, 512),
                          _j_context_attention(qb_c, kb_c, vb_c, wa_sink).reshape(B * C, 512)], axis=-1)
    return na.astype(BF16), wa.astype(BF16), cx.astype(BF16)


SSD_INNER = 1024
SSD_HEAD_DIM = 64
SSD_HEADS = 16
SSD_GROUPS = 2
SSD_STATE = 128
S5_WIDTH = 512
S5_GROUPS = 32
SSM_SIZES = (1024, 1536, 32, 512)


def _points(sizes):
    return tuple(int(s) for s in np.cumsum(sizes)[:-1])


def _j_dwconv(x, w, b):
    K = w.shape[0]
    out = lax.conv_general_dilated(x, w[:, None, :], window_strides=(1,), padding=[((K - 1) // 2, K // 2)],
                                   dimension_numbers=('NWC', 'WIO', 'NWC'), feature_group_count=x.shape[-1])
    return out + b


def _j_ssd_scan(x, dt, A, bm, cm, h0, want_y):
    Bn, T, H, P = x.shape
    G, N = bm.shape[2], bm.shape[3]
    hg = H // G
    Q = SSD_CHUNK
    nc = T // Q
    xq = x.reshape(Bn, nc, Q, G, hg, P)
    dtq = dt.reshape(Bn, nc, Q, G, hg)
    bq = bm.reshape(Bn, nc, Q, G, N)
    cq = cm.reshape(Bn, nc, Q, G, N)
    acs = jnp.cumsum(dtq * A.reshape(G, hg), axis=2)
    w_end = jnp.exp(acs[:, :, -1:] - acs) * dtq
    states = jnp.einsum('bcqgn,bcqgh,bcqghp->bcghpn', bq, w_end, xq)
    decay = jnp.exp(acs[:, :, -1])

    def step(h, inp):
        s, dc = inp
        return dc[..., None, None] * h + s, h

    h_fin, h_in = lax.scan(step, h0.reshape(Bn, G, hg, P, N),
                           (jnp.moveaxis(states, 1, 0), jnp.moveaxis(decay, 1, 0)))
    h_fin = h_fin.reshape(Bn, H, P, N)
    if not want_y:
        return None, h_fin
    h_in = jnp.moveaxis(h_in, 0, 1)
    lower = jnp.tril(jnp.ones((Q, Q), dtype=bool))[:, :, None, None]
    seg = acs[:, :, :, None] - acs[:, :, None, :]
    lmat = jnp.exp(jnp.where(lower, seg, NEG_INF))
    cb = jnp.einsum('bcign,bcjgn->bcijg', cq, bq)
    w = cb[..., None] * lmat * dtq[:, :, None]
    y_diag = jnp.einsum('bcijgh,bcjghp->bcighp', w, xq)
    y_off = jnp.einsum('bcign,bcghpn->bcighp', cq, h_in) * jnp.exp(acs)[..., None]
    return (y_diag + y_off).reshape(Bn, T, H, P), h_fin


def _j_s5_discretise(lam_re, lam_im, log_step, b_re, b_im):
    lam = lax.complex(lam_re.astype(jnp.float32), lam_im.astype(jnp.float32))
    step = jnp.exp(log_step.astype(jnp.float32))[:, None]
    a_bar = jnp.exp(lam * step)
    b = lax.complex(b_re.astype(jnp.float32), b_im.astype(jnp.float32))
    b_bar = ((a_bar - 1) / lam)[..., None] * b
    return a_bar, b_bar


def _lin_combine(left, right):
    a_l, b_l = left
    a_r, b_r = right
    return a_r * a_l, a_r * b_l + b_r


def _j_s5_scan(u, a_bar, b_bar, cmat, h0, want_y):
    T = u.shape[1]
    bu = jnp.einsum('gnc,btgc->tbgn', b_bar, u.astype(jnp.complex64))
    bu = bu.at[0].add(a_bar * h0)
    a = jnp.broadcast_to(a_bar, (T, 1) + a_bar.shape)
    _, h = lax.associative_scan(_lin_combine, (a, bu), axis=0)
    y = jnp.einsum('gcn,tbgn->btgc', cmat, h).real if want_y else None
    return y, h[-1]


def _j_ssm_mixer(hl, hc, w_in, conv_w, conv_b, dt_bias, a_log, d_skip, norm_w,
                 lam_re, lam_im, log_step, b_re, b_im, c_re, c_im, s5_dskip, glu_w, glu_b):
    def project(h):
        Bn, T, _ = h.shape
        z, xbc, dtr, u = jnp.split(h @ w_in, _points(SSM_SIZES), axis=-1)
        xbc = jax.nn.silu(_j_dwconv(xbc, conv_w, conv_b))
        xs, bm, cm = jnp.split(xbc, _points((SSD_INNER, SSD_GROUPS * SSD_STATE, SSD_GROUPS * SSD_STATE)), axis=-1)
        xs = xs.reshape(Bn, T, SSD_HEADS, SSD_HEAD_DIM).astype(jnp.float32)
        bm = bm.reshape(Bn, T, SSD_GROUPS, SSD_STATE).astype(jnp.float32)
        cm = cm.reshape(Bn, T, SSD_GROUPS, SSD_STATE).astype(jnp.float32)
        dt = jax.nn.softplus(dtr.reshape(Bn, T, 2, SSD_HEADS).astype(jnp.float32) + dt_bias.astype(jnp.float32))
        u = u.reshape(Bn, T, S5_GROUPS, S5_GROUP).astype(jnp.float32)
        return z, xs, bm, cm, dt, u

    z_l, xs_l, b_l, c_l, dt_l, u_l = project(hl)
    z_c, xs_c, b_c, c_c, dt_c, u_c = project(hc)
    A = -jnp.exp(a_log.astype(jnp.float32))
    Bn = hl.shape[0]
    ssd_l, s5_l = [], []
    for dr in range(2):
        fl = (lambda t: jnp.flip(t, axis=1)) if dr else (lambda t: t)
        h0 = jnp.zeros((Bn, SSD_HEADS, SSD_HEAD_DIM, SSD_STATE), jnp.float32)
        _, h_ctx = _j_ssd_scan(fl(xs_c), fl(dt_c[:, :, dr]), A[dr], fl(b_c), fl(c_c), h0, False)
        y_l, _ = _j_ssd_scan(fl(xs_l), fl(dt_l[:, :, dr]), A[dr], fl(b_l), fl(c_l), h_ctx, True)
        ssd_l.append(fl(y_l))
        a_bar, b_bar = _j_s5_discretise(lam_re[dr], lam_im[dr], log_step[dr], b_re[dr], b_im[dr])
        cmat = lax.complex(c_re[dr].astype(jnp.float32), c_im[dr].astype(jnp.float32))
        s0 = jnp.zeros((Bn, S5_GROUPS, S5_STATE), jnp.complex64)
        _, s_ctx = _j_s5_scan(fl(u_c), a_bar, b_bar, cmat, s0, False)
        v_l, _ = _j_s5_scan(fl(u_l), a_bar, b_bar, cmat, s_ctx, True)
        s5_l.append(fl(v_l))

    def finish(z, xs, ssd_y, u, s5_y):
        Bn_, T = z.shape[:2]
        y = (ssd_y + d_skip.astype(jnp.float32)[:, None] * xs).reshape(Bn_, T, SSD_INNER)
        y = _rms(y * jax.nn.silu(z.astype(jnp.float32))) * norm_w
        v = (s5_y + s5_dskip.astype(jnp.float32).reshape(S5_GROUPS, S5_GROUP) * u).reshape(Bn_, T, S5_WIDTH)
        v = jax.nn.gelu(v)
        v = v * jax.nn.sigmoid(v @ glu_w.astype(jnp.float32) + glu_b.astype(jnp.float32))
        return jnp.concatenate([y, v], axis=-1)

    return finish(z_l, xs_l, ssd_l[0] + ssd_l[1], u_l, s5_l[0] + s5_l[1])


def kernel(x, c, ctx, c_ctx, mod_w, mod_b, norm_mix, norm_ffn, att_w_in, att_w_out, na_q_norm, na_k_norm, na_rel_bias, wa_q_norm, wa_k_norm, wa_sink, ssm_w_in, ssm_w_out, ssd_conv_w, ssd_conv_b, ssd_dt_bias, ssd_a_log, ssd_d, ssd_norm, s5_lambda_re, s5_lambda_im, s5_log_step, s5_b_re, s5_b_im, s5_c_re, s5_c_im, s5_d, s5_glu_w, s5_glu_b, moe_w_group, moe_b_group, moe_w_expert, moe_b_expert, moe_w13, moe_w2):
    B, T, D = x.shape
    C = ctx.shape[1]
    rw = _Rows(B, T, C, D, ROW_TILE)
    xl = x.reshape(B * T, D)
    xc = ctx.reshape(B * C, D)
    cm = jnp.concatenate([c, c_ctx[None, :], jnp.zeros((8 - B - 1, D), F32)], axis=0)
    mods = _modulation(cm, mod_w, mod_b)
    mods = mods.reshape(mods.shape[0], 8, 1, 6 * D)

    m0 = mods[0]
    qkv = _att_inproj(rw, xl, xc, m0, norm_mix[0], att_w_in[0], na_q_norm[0], na_k_norm[0], wa_q_norm[0],
                      wa_k_norm[0])
    na, wa, cx = _j_attention(rw, qkv, na_rel_bias[0], wa_sink[0])
    wr, br = _router_weights(moe_w_group[0], moe_b_group[0], moe_w_expert[0], moe_b_expert[0])
    xall, h2, route, counts = _att_outproj(rw, na, wa, cx, xl, xc, att_w_out[0], m0, norm_ffn[0], wr, br)
    y1, y2 = _moe(h2, route, counts, moe_w13[0], moe_w2[0])
    xall = _combine(rw, rw.ntot, xall, y1, y2, route, m0)

    m1 = mods[1]
    hl = _ada_norm(xall[:B * T].reshape(B, T, D), norm_mix[1], m1[:B, :, 0:D], m1[:B, :, D:2 * D])
    hc = _ada_norm(xall[B * T:].reshape(B, C, D), norm_mix[1], m1[B:B + 1, :, 0:D], m1[B:B + 1, :, D:2 * D])
    mix = _j_ssm_mixer(hl, hc, ssm_w_in[0], ssd_conv_w[0], ssd_conv_b[0], ssd_dt_bias[0], ssd_a_log[0], ssd_d[0],
                       ssd_norm[0], s5_lambda_re[0], s5_lambda_im[0], s5_log_step[0], s5_b_re[0], s5_b_im[0],
                       s5_c_re[0], s5_c_im[0], s5_d[0], s5_glu_w[0], s5_glu_b[0])
    mix = mix.reshape(B * T, -1).astype(BF16)
    wr, br = _router_weights(moe_w_group[1], moe_b_group[1], moe_w_expert[1], moe_b_expert[1])
    xlat, h2, route, counts = _plain_outproj(rw, rw.nlat, mix, xall, ssm_w_out[0], m1, norm_ffn[1], wr, br)
    y1, y2 = _moe(h2, route, counts, moe_w13[1], moe_w2[1])
    out = _combine(rw, rw.nlat, xlat, y1, y2, route, m1)
    return out.reshape(B, T, D)
```

```python
import functools
import math

import jax
import jax.numpy as jnp
import numpy as np
from jax import lax
from jax.experimental import pallas as pl
from jax.experimental.pallas import tpu as pltpu

F32 = jnp.float32
BF16 = jnp.bfloat16

EPS = 1e-6
NEG_INF = -1e30
GRID_W = 64
HEAD_DIM = 64
NA_KH = 8
NA_KW = 16
WA_BLOCK = 128
ROPE_BASE = 10000.0
SSD_CHUNK = 128
S5_GROUP = 16
S5_STATE = 64
MOE_GROUPS = 4
MOE_EPG = 8
MOE_EXPERTS = MOE_GROUPS * MOE_EPG

LANES = 128
ROW_TILE = 512
MOE_TILE = 256
VMEM_LIMIT = 56 * 1024 * 1024


def _cparams(n_axes, vmem=VMEM_LIMIT):
    return pltpu.CompilerParams(dimension_semantics=("arbitrary",) * n_axes, vmem_limit_bytes=vmem)


def _sigmoid(x):
    return 1.0 / (1.0 + jnp.exp(-x))


def _silu(x):
    return x * _sigmoid(x)


def _rms(x, eps=EPS):
    return x * lax.rsqrt(jnp.mean(x * x, axis=-1, keepdims=True) + eps)


def _ada_norm(x, g, shift, scale):
    return (_rms(x) * g) * (1.0 + scale) + shift


def _mod_kernel(c_ref, w_ref, b_ref, o_ref):
    a = _silu(c_ref[...])
    o_ref[...] = jnp.dot(a, w_ref[...], preferred_element_type=F32, precision=lax.Precision.HIGHEST) + b_ref[...]


def _modulation(cm, mod_w, mod_b):
    depth, d, n6 = mod_w.shape
    tn = 1024
    return pl.pallas_call(
        _mod_kernel,
        grid=(depth, n6 // tn),
        in_specs=[pl.BlockSpec((8, d), lambda l, j: (0, 0)),
                  pl.BlockSpec((None, d, tn), lambda l, j: (l, 0, j)),
                  pl.BlockSpec((None, 1, tn), lambda l, j: (l, 0, j))],
        out_specs=pl.BlockSpec((None, 8, tn), lambda l, j: (l, 0, j)),
        out_shape=jax.ShapeDtypeStruct((depth, 8, n6), F32),
        compiler_params=_cparams(2),
        name="modulation",
    )(cm, mod_w, mod_b.reshape(depth, 1, n6))


class _Rows:
    def __init__(self, B, T, C, D, tm):
        assert T % tm == 0 and (B * C) % tm == 0
        self.B, self.T, self.C, self.D, self.tm = B, T, C, D, tm
        self.tpb = T // tm
        self.nlat = B * self.tpb
        self.nctx = (B * C) // tm
        self.ntot = self.nlat + self.nctx
        self.rows = B * (T + C)

    def group(self, i):
        return jnp.where(i < self.nlat, i // self.tpb, self.B)


def _mod_spec(rw, col):
    return pl.BlockSpec((None, 1, rw.D), lambda i, *_: (rw.group(i), 0, col))


def _seg_norm(y, seg, gcol):
    ss = jnp.dot((y * y).astype(BF16), seg, preferred_element_type=F32)
    return y * lax.rsqrt(ss + EPS) * gcol


def _rope(y, cos, sin):
    w = y.shape[-1]
    lane = lax.broadcasted_iota(jnp.int32, y.shape, 1)
    first = (lane % 32) < 16
    partner = jnp.where(first, pltpu.roll(y, w - 16, 1), pltpu.roll(y, 16, 1))
    return y * cos + partner * sin


def _dup_halves(k):
    lane = lax.broadcasted_iota(jnp.int32, k.shape, 1)
    sw = pltpu.roll(k, 64, 1)
    return jnp.where(lane < 64, k, sw), jnp.where(lane < 64, sw, k)


def _att_inproj_kernel(nlat, xl_ref, xc_ref, g_ref, sh_ref, sc_ref, w_ref, gcol_ref, cos_ref, sin_ref, seg_ref,
                       o_ref, h_scr):
    i = pl.program_id(0)
    x = jnp.where(i < nlat, xl_ref[...], xc_ref[...])
    h_scr[...] = _ada_norm(x, g_ref[...], sh_ref[...], sc_ref[...]).astype(BF16)
    seg = seg_ref[...]
    cos2 = jnp.concatenate([cos_ref[...], cos_ref[...]], axis=1)
    sin2 = jnp.concatenate([sin_ref[...], sin_ref[...]], axis=1)
    for c in range(9):
        c0 = c * 256
        y = jnp.dot(h_scr[...], w_ref[:, c0:c0 + 256], preferred_element_type=F32)
        gcol = gcol_ref[:, c0:c0 + 256]
        if c in (0, 1, 2, 3):
            o_ref[:, c0:c0 + 256] = _seg_norm(y, seg, gcol).astype(BF16)
        elif c in (4, 5):
            o_ref[:, c0:c0 + 256] = y.astype(BF16)
        elif c in (6, 7):
            o_ref[:, c0:c0 + 256] = _rope(_seg_norm(y, seg, gcol), cos2, sin2).astype(BF16)
        else:
            lane = lax.broadcasted_iota(jnp.int32, y.shape, 1)
            yk = jnp.where(lane < 128, _seg_norm(y, seg, gcol), y)
            yr = jnp.where(lane < 128, _rope(yk, cos2, sin2), yk)
            k0, k1 = _dup_halves(yr[:, :128])
            v0, v1 = _dup_halves(yr[:, 128:])
            o_ref[:, 2048:2176] = k0.astype(BF16)
            o_ref[:, 2176:2304] = k1.astype(BF16)
            o_ref[:, 2304:2432] = v0.astype(BF16)
            o_ref[:, 2432:2560] = v1.astype(BF16)


def _rope_tables(T, tm):
    t = np.arange(T)
    d = np.arange(HEAD_DIM)
    nf = HEAD_DIM // 4
    inv = jnp.asarray(ROPE_BASE, F32) ** (-jnp.arange(nf, dtype=F32) / nf)
    pos = np.where((d // 32 == 0)[None, :], (t // GRID_W)[:, None], (t % GRID_W)[:, None])
    ang = jnp.asarray(pos, F32) * inv[d % nf][None, :]
    sign = np.where((d % 32) < 16, -1.0, 1.0).astype(np.float32)
    cos = jnp.cos(ang)
    sin = jnp.sin(ang) * sign[None, :]
    cos = jnp.concatenate([cos, jnp.ones((tm, HEAD_DIM), F32)], axis=0)
    sin = jnp.concatenate([sin, jnp.zeros((tm, HEAD_DIM), F32)], axis=0)
    return jnp.tile(cos, (1, 2)), jnp.tile(sin, (1, 2))


def _att_inproj(rw, xl, xc, mods, norm_g, w_in, na_qn, na_kn, wa_qn, wa_kn):
    D, tm = rw.D, rw.tm
    scale = HEAD_DIM ** -0.5
    gcol = jnp.concatenate([jnp.tile(na_qn * scale, 8), jnp.tile(na_kn, 8), jnp.ones((512,), F32),
                            jnp.tile(wa_qn * scale, 8), jnp.tile(wa_kn, 2), jnp.ones((128,), F32)])[None, :]
    cos, sin = _rope_tables(rw.T, tm)
    segn = np.arange(256) // 64
    seg = jnp.asarray((segn[:, None] == segn[None, :]).astype(np.float32) / 64.0, BF16)
    nlat, tpb = rw.nlat, rw.tpb
    return pl.pallas_call(
        functools.partial(_att_inproj_kernel, nlat),
        grid=(rw.ntot,),
        in_specs=[pl.BlockSpec((tm, D), lambda i: (jnp.minimum(i, nlat - 1), 0)),
                  pl.BlockSpec((tm, D), lambda i: (jnp.maximum(i - nlat, 0), 0)),
                  pl.BlockSpec((1, D), lambda i: (0, 0)),
                  _mod_spec(rw, 0), _mod_spec(rw, 1),
                  pl.BlockSpec((D, 2304), lambda i: (0, 0)),
                  pl.BlockSpec((1, 2304), lambda i: (0, 0)),
                  pl.BlockSpec((tm, 128), lambda i: (jnp.where(i < nlat, i % tpb, tpb), 0)),
                  pl.BlockSpec((tm, 128), lambda i: (jnp.where(i < nlat, i % tpb, tpb), 0)),
                  pl.BlockSpec((256, 256), lambda i: (0, 0))],
        out_specs=pl.BlockSpec((tm, 2560), lambda i: (i, 0)),
        out_shape=jax.ShapeDtypeStruct((rw.rows, 2560), BF16),
        scratch_shapes=[pltpu.VMEM((tm, D), BF16)],
        compiler_params=_cparams(1),
        name="att_inproj",
    )(xl, xc, norm_g[None, :], mods, mods, w_in.astype(BF16), gcol, cos, sin, seg)


def _route(lg, lt, carry):
    lane = lax.broadcasted_iota(jnp.int32, lg.shape, 1).astype(F32)
    gm = lane < MOE_GROUPS
    mg = jnp.max(jnp.where(gm, lg, NEG_INF), axis=-1, keepdims=True)
    eg = jnp.where(gm, jnp.exp(jnp.where(gm, lg, NEG_INF) - mg), 0.0)
    pg = eg / jnp.sum(eg, axis=-1, keepdims=True)
    ptop = jnp.max(pg, axis=-1, keepdims=True)
    gidx = jnp.min(jnp.where(gm & (pg == ptop), lane, 1e9), axis=-1, keepdims=True)
    lo = MOE_GROUPS + MOE_EPG * gidx
    em = (lane >= lo) & (lane < lo + MOE_EPG)
    le = jnp.where(em, lg, NEG_INF)
    ee = jnp.where(em, jnp.exp(le - jnp.max(le, axis=-1, keepdims=True)), 0.0)
    pe = ee / jnp.sum(ee, axis=-1, keepdims=True)
    v1 = jnp.max(jnp.where(em, pe, -1.0), axis=-1, keepdims=True)
    i1 = jnp.min(jnp.where(em & (pe == v1), lane, 1e9), axis=-1, keepdims=True)
    em2 = em & (lane != i1)
    v2 = jnp.max(jnp.where(em2, pe, -1.0), axis=-1, keepdims=True)
    i2 = jnp.min(jnp.where(em2 & (pe == v2), lane, 1e9), axis=-1, keepdims=True)
    den = v1 + v2
    w1 = v1 / den * ptop
    w2 = v2 / den * ptop
    e1 = i1 - MOE_GROUPS
    e2 = i2 - MOE_GROUPS
    m1 = lane == e1
    m2 = lane == e2
    oh = jnp.where(m1 | m2, 1.0, 0.0)
    cnt = jnp.dot(lt, oh.astype(BF16), preferred_element_type=F32) + carry
    r1 = jnp.sum(jnp.where(m1, cnt, 0.0), axis=-1, keepdims=True)
    r2 = jnp.sum(jnp.where(m2, cnt, 0.0), axis=-1, keepdims=True)
    route = jnp.where(lane == 0, e1, jnp.where(lane == 1, e2, jnp.where(lane == 2, w1, jnp.where(
        lane == 3, w2, jnp.where(lane == 4, r1, jnp.where(lane == 5, r2, 0.0))))))
    return route, carry + jnp.sum(oh, axis=0, keepdims=True)


def _post_mixer(i, x, y, g1, gn, sh2, sc2, wr_ref, br_ref, lt_ref, xo_ref, h2_ref, rt_ref, cnt_ref, carry):
    xn = x + g1 * y
    xo_ref[...] = xn
    h2 = _ada_norm(xn, gn, sh2, sc2)
    h2_ref[...] = h2.astype(BF16)
    lg = jnp.dot(h2, wr_ref[...], preferred_element_type=F32, precision=lax.Precision.HIGHEST) + br_ref[...]

    @pl.when(i == 0)
    def _():
        carry[...] = jnp.zeros_like(carry)

    route, newc = _route(lg, lt_ref[...], carry[...])
    rt_ref[...] = route
    carry[...] = newc
    cnt_ref[...] = newc


def _att_outproj_kernel(nlat, na_ref, wa_ref, cx_ref, xl_ref, xc_ref, w_ref, g1_ref, gn_ref, sh2_ref, sc2_ref,
                        wr_ref, br_ref, lt_ref, xo_ref, h2_ref, rt_ref, cnt_ref, carry):
    i = pl.program_id(0)
    lat = i < nlat
    mix = jnp.where(lat, jnp.concatenate([na_ref[...], wa_ref[...]], axis=1), cx_ref[...])
    y = jnp.dot(mix, w_ref[...], preferred_element_type=F32)
    x = jnp.where(lat, xl_ref[...], xc_ref[...])
    _post_mixer(i, x, y, g1_ref[...], gn_ref[...], sh2_ref[...], sc2_ref[...], wr_ref, br_ref, lt_ref,
                xo_ref, h2_ref, rt_ref, cnt_ref, carry)


def _router_weights(w_group, b_group, w_expert, b_expert):
    D = w_group.shape[0]
    pad = LANES - MOE_GROUPS - MOE_EXPERTS
    wr = jnp.concatenate([w_group, w_expert, jnp.zeros((D, pad), F32)], axis=1)
    br = jnp.concatenate([b_group, b_expert, jnp.zeros((pad,), F32)])[None, :]
    return wr, br


def _lower_tri(tm):
    r = np.arange(tm)
    return jnp.asarray((r[None, :] < r[:, None]).astype(np.float32), BF16)


def _post_specs(rw):
    D, tm = rw.D, rw.tm
    return ([pl.BlockSpec((1, D), lambda i: (0, 0)), _mod_spec(rw, 3), _mod_spec(rw, 4),
             pl.BlockSpec((D, LANES), lambda i: (0, 0)), pl.BlockSpec((1, LANES), lambda i: (0, 0)),
             pl.BlockSpec((tm, tm), lambda i: (0, 0))],
            [pl.BlockSpec((tm, D), lambda i: (i, 0)), pl.BlockSpec((tm, D), lambda i: (i, 0)),
             pl.BlockSpec((tm, LANES), lambda i: (i, 0)), pl.BlockSpec((1, LANES), lambda i: (0, 0))])


def _post_shapes(nrows, D):
    return [jax.ShapeDtypeStruct((nrows, D), F32), jax.ShapeDtypeStruct((nrows, D), BF16),
            jax.ShapeDtypeStruct((nrows, LANES), F32), jax.ShapeDtypeStruct((1, LANES), F32)]


def _att_outproj(rw, na, wa, cx, xl, xc, w_out, mods, norm_ffn, wr, br):
    D, tm, nlat = rw.D, rw.tm, rw.nlat
    post_in, post_out = _post_specs(rw)
    latmap = lambda i: (jnp.minimum(i, nlat - 1), 0)
    ctxmap = lambda i: (jnp.maximum(i - nlat, 0), 0)
    return pl.pallas_call(
        functools.partial(_att_outproj_kernel, nlat),
        grid=(rw.ntot,),
        in_specs=[pl.BlockSpec((tm, 512), latmap), pl.BlockSpec((tm, 512), latmap), pl.BlockSpec((tm, D), ctxmap),
                  pl.BlockSpec((tm, D), latmap), pl.BlockSpec((tm, D), ctxmap),
                  pl.BlockSpec((D, D), lambda i: (0, 0)), _mod_spec(rw, 2)] + post_in,
        out_specs=post_out,
        out_shape=_post_shapes(rw.rows, D),
        scratch_shapes=[pltpu.VMEM((1, LANES), F32)],
        compiler_params=_cparams(1),
        name="att_outproj_router",
    )(na, wa, cx, xl, xc, w_out.astype(BF16), mods, norm_ffn[None, :], mods, mods, wr, br, _lower_tri(tm))


def _moe_kernel(te_ref, nu_ref, xs_ref, w13_ref, w2_ref, o_ref, w13b, w2b):
    i = pl.program_id(0)
    prev = te_ref[jnp.maximum(i - 1, 0)]
    changed = (i == 0) | (te_ref[i] != prev)

    @pl.when(changed)
    def _():
        w13b[...] = w13_ref[...].astype(BF16)
        w2b[...] = w2_ref[...].astype(BF16)

    @pl.when(i < nu_ref[0])
    def _():
        ff = w2b.shape[0]
        a13 = jnp.dot(xs_ref[...], w13b[...], preferred_element_type=F32)
        act = _silu(a13[:, :ff]) * a13[:, ff:]
        o_ref[...] = jnp.dot(act.astype(BF16), w2b[...], preferred_element_type=F32).astype(BF16)

    @pl.when(i >= nu_ref[0])
    def _():
        o_ref[...] = jnp.zeros_like(o_ref)


def _moe(h2, route, counts, w13, w2):
    N, D = h2.shape
    E, _, F2 = w13.shape
    tg = MOE_TILE
    nt = (2 * N) // tg + E
    e = route[:, 0:2].astype(jnp.int32)
    rank = route[:, 4:6].astype(jnp.int32)
    cnt = counts[0, :E].astype(jnp.int32)
    ntile_e = (cnt + tg - 1) // tg
    tile_end = jnp.cumsum(ntile_e)
    offs = (tile_end - ntile_e) * tg
    dest = offs[e] + rank
    src = jnp.zeros((nt * tg,), jnp.int32).at[dest.reshape(-1)].set(jnp.repeat(jnp.arange(N, dtype=jnp.int32), 2))
    te = jnp.minimum(jnp.searchsorted(tile_end, jnp.arange(nt, dtype=jnp.int32), side="right"), E - 1).astype(jnp.int32)
    nu = tile_end[-1:].astype(jnp.int32)
    te = jnp.where(jnp.arange(nt) < nu[0], te, te[jnp.maximum(nu[0] - 1, 0)])
    xs = jnp.take(h2, src, axis=0)
    ys = pl.pallas_call(
        _moe_kernel,
        grid_spec=pltpu.PrefetchScalarGridSpec(
            num_scalar_prefetch=2,
            grid=(nt,),
            in_specs=[pl.BlockSpec((tg, D), lambda i, te, nu: (i, 0)),
                      pl.BlockSpec((None, D, F2), lambda i, te, nu: (te[i], 0, 0)),
                      pl.BlockSpec((None, F2 // 2, D), lambda i, te, nu: (te[i], 0, 0))],
            out_specs=pl.BlockSpec((tg, D), lambda i, te, nu: (i, 0)),
            scratch_shapes=[pltpu.VMEM((D, F2), BF16), pltpu.VMEM((F2 // 2, D), BF16)]),
        out_shape=jax.ShapeDtypeStruct((nt * tg, D), BF16),
        compiler_params=_cparams(1),
        name="moe_experts",
    )(te, nu, xs, w13, w2)
    return jnp.take(ys, dest[:, 0], axis=0), jnp.take(ys, dest[:, 1], axis=0)


def _combine_kernel(x_ref, y1_ref, y2_ref, rt_ref, g2_ref, o_ref):
    rt = rt_ref[...]
    f = rt[:, 2:3] * y1_ref[...].astype(F32) + rt[:, 3:4] * y2_ref[...].astype(F32)
    o_ref[...] = x_ref[...] + g2_ref[...] * f


def _combine(rw, ntiles, xall, y1, y2, route, mods):
    D, tm = rw.D, rw.tm
    row = lambda i: (i, 0)
    return pl.pallas_call(
        _combine_kernel,
        grid=(ntiles,),
        in_specs=[pl.BlockSpec((tm, D), row), pl.BlockSpec((tm, D), row), pl.BlockSpec((tm, D), row),
                  pl.BlockSpec((tm, LANES), row), _mod_spec(rw, 5)],
        out_specs=pl.BlockSpec((tm, D), row),
        out_shape=jax.ShapeDtypeStruct((ntiles * tm, D), F32),
        compiler_params=_cparams(1),
        name="moe_combine",
    )(xall, y1, y2, route, mods)


NA_QROWS = 8
NA_KROWS = 16


def _na_bias(rpb, n_rb):
    H = rpb.shape[0]
    i = np.arange(GRID_W)
    c0 = np.clip(i - NA_KW // 2, 0, GRID_W - NA_KW)
    j = np.arange(GRID_W)
    colvalid = (j[None, :] >= c0[:, None]) & (j[None, :] < c0[:, None] + NA_KW)
    dc = np.clip(j[None, :] - i[:, None] + NA_KW - 1, 0, 2 * NA_KW - 2)
    onehot = ((dc[None] == np.arange(2 * NA_KW - 1)[:, None, None]) & colvalid[None]).astype(np.float32)
    tiles = jnp.einsum('hrc,cij->hrij', rpb.astype(F32), jnp.asarray(onehot), precision=lax.Precision.HIGHEST)
    tiles = tiles + jnp.asarray(np.where(colvalid, 0.0, NEG_INF).astype(np.float32))
    a = np.arange(NA_QROWS)[:, None]
    kr = np.arange(NA_KROWS)[None, :]
    dr = kr - a + 3
    start = np.stack([np.maximum(a - 4, 0) + 4 + 0 * kr, a + 0 * kr, np.minimum(a, 4) + 0 * kr])
    valid = (kr[None] >= start) & (kr[None] < start + NA_KH) & (dr[None] >= 0) & (dr[None] <= 2 * NA_KH - 2)
    drc = np.broadcast_to(np.clip(dr, 0, 2 * NA_KH - 2), (3, NA_QROWS, NA_KROWS))
    full = jnp.take(tiles, jnp.asarray(drc.reshape(-1)), axis=1).reshape(H, 3, NA_QROWS, NA_KROWS, GRID_W, GRID_W)
    full = jnp.where(jnp.asarray(valid)[None, :, :, :, None, None], full, NEG_INF)
    return full.transpose(0, 1, 2, 4, 3, 5).reshape(H, 3, NA_QROWS * GRID_W, NA_KROWS * GRID_W)


def _softmax_pv(parts, extra=None):
    mx = functools.reduce(jnp.maximum, [jnp.max(s, axis=-1, keepdims=True) for s, _ in parts])
    if extra is not None:
        mx = jnp.maximum(mx, extra)
    l = jnp.zeros_like(mx) if extra is None else jnp.exp(extra - mx)
    o = None
    for s, v in parts:
        p = jnp.exp(s - mx)
        l = l + jnp.sum(p, axis=-1, keepdims=True)
        pv = jnp.dot(p.astype(BF16), v, preferred_element_type=F32)
        o = pv if o is None else o + pv
    return o / l


def _nt(a, b):
    return lax.dot_general(a, b, (((1,), (1,)), ((), ())), preferred_element_type=F32)


def _na_kernel(q_ref, k0, k1, k2, k3, v0, v1, v2, v3, kc_ref, vc_ref, bias_ref, o_ref):
    q2 = q_ref[...]
    kw = jnp.concatenate([k0[...], k1[...], k2[...], k3[...]], axis=0)
    vw = jnp.concatenate([v0[...], v1[...], v2[...], v3[...]], axis=0)
    kc = kc_ref[...]
    vc = vc_ref[...]
    lane = lax.broadcasted_iota(jnp.int32, q2.shape, 1)
    out = jnp.zeros(q2.shape, F32)
    for hh in range(2):
        m = (lane < HEAD_DIM) if hh == 0 else (lane >= HEAD_DIM)
        qm = jnp.where(m, q2, jnp.zeros_like(q2))
        o = _softmax_pv([(_nt(qm, kw) + bias_ref[hh], vw), (_nt(qm, kc), vc)])
        out = jnp.where(m, o, out)
    o_ref[...] = out.astype(BF16)


def _na_attention(rw, qkv, rpb):
    B, T, C = rw.B, rw.T, rw.C
    tq = NA_QROWS * GRID_W
    tk = tq // 2
    n_rb = T // tq
    nkb = T // tk
    assert T % tq == 0 and n_rb >= 2 and (B * T) % C == 0
    bias = _na_bias(rpb, n_rb)
    ctxrow = (B * T) // C

    def kvspec(j, col):
        return pl.BlockSpec((tk, LANES), lambda p, rb, b: (b * nkb + jnp.clip(2 * rb - 1 + j, 0, nkb - 1), col + p))

    return pl.pallas_call(
        _na_kernel,
        grid=(4, n_rb, B),
        in_specs=[pl.BlockSpec((tq, LANES), lambda p, rb, b: (b * n_rb + rb, p))]
        + [kvspec(j, 4) for j in range(4)] + [kvspec(j, 8) for j in range(4)]
        + [pl.BlockSpec((C, LANES), lambda p, rb, b: (ctxrow + b, 4 + p)),
           pl.BlockSpec((C, LANES), lambda p, rb, b: (ctxrow + b, 8 + p)),
           pl.BlockSpec((2, None, tq, 2 * tq),
                        lambda p, rb, b: (p, jnp.where(rb == 0, 0, jnp.where(rb == n_rb - 1, 2, 1)), 0, 0))],
        out_specs=pl.BlockSpec((tq, LANES), lambda p, rb, b: (b * n_rb + rb, p)),
        out_shape=jax.ShapeDtypeStruct((B * T, 4 * LANES), BF16),
        compiler_params=_cparams(3),
        name="neighbourhood_attention",
    )(qkv, *([qkv] * 10), bias)


def _wa_kernel(nb, sink_ref, q_ref, kp, kc_, kn, vp, vc_, vn, kx_ref, vx_ref, o_ref):
    kv = pl.program_id(1)
    n = pl.program_id(2)
    q4 = q_ref[...]
    blk = q4.shape[0]
    lane = lax.broadcasted_iota(jnp.int32, (blk, LANES), 1)
    zero = jnp.zeros((blk, LANES), BF16)
    parts = []
    for pr in range(2):
        qp = q4[:, pr * LANES:(pr + 1) * LANES]
        parts += [jnp.where(lane < HEAD_DIM, qp, zero), jnp.where(lane >= HEAD_DIM, qp, zero)]
    qs = jnp.concatenate(parts, axis=0)
    kb = jnp.concatenate([kp[...], kc_[...], kn[...]], axis=0)
    vb = jnp.concatenate([vp[...], vc_[...], vn[...]], axis=0)
    qi = lax.broadcasted_iota(jnp.int32, (blk, 3 * blk), 0)
    ks = lax.broadcasted_iota(jnp.int32, (blk, 3 * blk), 1)
    lo = jnp.where(n > 0, 0, blk)
    hi = jnp.where(n < nb - 1, 3 * blk, 2 * blk)
    valid = (ks >= qi) & (ks <= qi + 2 * blk) & (ks >= lo) & (ks < hi)
    valid4 = jnp.concatenate([valid] * 4, axis=0)
    s = jnp.where(valid4, _nt(qs, kb), NEG_INF)
    sink = jnp.concatenate([jnp.full((blk, 1), sink_ref[kv * 4 + g], F32) for g in range(4)], axis=0)
    o = _softmax_pv([(s, vb), (_nt(qs, kx_ref[...]), vx_ref[...])], extra=sink)
    out0 = jnp.where(lane < HEAD_DIM, o[0:blk], o[blk:2 * blk])
    out1 = jnp.where(lane < HEAD_DIM, o[2 * blk:3 * blk], o[3 * blk:4 * blk])
    o_ref[...] = jnp.concatenate([out0, out1], axis=1).astype(BF16)


def _wa_attention(rw, qkv, sink):
    B, T, C = rw.B, rw.T, rw.C
    blk = WA_BLOCK
    nb = T // blk
    ctxrow = (B * T) // C

    def kvspec(j, col):
        return pl.BlockSpec((blk, LANES), lambda b, kv, n: (b * nb + jnp.clip(n + j, 0, nb - 1), col + kv))

    return pl.pallas_call(
        functools.partial(_wa_kernel, nb),
        grid=(B, 2, nb),
        in_specs=[pl.BlockSpec(memory_space=pltpu.SMEM),
                  pl.BlockSpec((blk, 2 * LANES), lambda b, kv, n: (b * nb + n, 6 + kv))]
        + [kvspec(j, 16) for j in (-1, 0, 1)] + [kvspec(j, 18) for j in (-1, 0, 1)]
        + [pl.BlockSpec((C, LANES), lambda b, kv, n: (ctxrow + b, 16 + kv)),
           pl.BlockSpec((C, LANES), lambda b, kv, n: (ctxrow + b, 18 + kv))],
        out_specs=pl.BlockSpec((blk, 2 * LANES), lambda b, kv, n: (b * nb + n, kv)),
        out_shape=jax.ShapeDtypeStruct((B * T, 4 * LANES), BF16),
        compiler_params=_cparams(3),
        name="window_attention",
    )(sink.astype(F32), qkv, *([qkv] * 8))


def _ctx_attn_kernel(sink_ref, t_ref, o_ref):
    C = t_ref.shape[0]
    lane = lax.broadcasted_iota(jnp.int32, (C, LANES), 1)
    zero = jnp.zeros((C, LANES), BF16)

    def pair(q2, k2, v2, sinks):
        out = jnp.zeros((C, LANES), F32)
        for hh in range(2):
            m = (lane < HEAD_DIM) if hh == 0 else (lane >= HEAD_DIM)
            extra = None if sinks is None else jnp.full((C, 1), sinks[hh], F32)
            o = _softmax_pv([(_nt(jnp.where(m, q2, zero), k2), v2)], extra=extra)
            out = jnp.where(m, o, out)
        return out.astype(BF16)

    for p in range(4):
        c = p * LANES
        o_ref[:, c:c + LANES] = pair(t_ref[:, c:c + LANES], t_ref[:, 512 + c:640 + c], t_ref[:, 1024 + c:1152 + c], None)
    for kv in range(2):
        kd = t_ref[:, 2048 + kv * LANES:2176 + kv * LANES]
        vd = t_ref[:, 2304 + kv * LANES:2432 + kv * LANES]
        for pr in range(2):
            c = kv * 256 + pr * LANES
            h0 = kv * 4 + pr * 2
            o_ref[:, 512 + c:640 + c] = pair(t_ref[:, 1536 + c:1664 + c], kd, vd, (sink_ref[h0], sink_ref[h0 + 1]))


def _ctx_attention(rw, qkv, sink):
    B, T, C = rw.B, rw.T, rw.C
    ctxrow = (B * T) // C
    return pl.pallas_call(
        _ctx_attn_kernel,
        grid=(B,),
        in_specs=[pl.BlockSpec(memory_space=pltpu.SMEM),
                  pl.BlockSpec((C, qkv.shape[1]), lambda b: (ctxrow + b, 0))],
        out_specs=pl.BlockSpec((C, 8 * LANES), lambda b: (b, 0)),
        out_shape=jax.ShapeDtypeStruct((B * C, 8 * LANES), BF16),
        compiler_params=_cparams(1),
        name="context_attention",
    )(sink.astype(F32), qkv)


S5_Q = 16
CONV_TILE = 256
CONV_HALO = 16


def _ssm_inproj_kernel(x_ref, g_ref, sh_ref, sc_ref, w_ref, z_ref, xbc_ref, u_ref, dt_ref, h_scr):
    h_scr[...] = _ada_norm(x_ref[...], g_ref[...], sh_ref[...], sc_ref[...]).astype(BF16)

    def mm(c0, n):
        return jnp.dot(h_scr[...], w_ref[:, c0:c0 + n], preferred_element_type=F32)

    for c in range(4):
        z_ref[:, c * 256:(c + 1) * 256] = mm(c * 256, 256).astype(BF16)
    for c in range(6):
        xbc_ref[:, c * 256:(c + 1) * 256] = mm(1024 + c * 256, 256).astype(BF16)
    for c in range(2):
        u_ref[:, c * 256:(c + 1) * 256] = mm(2560 + c * 256, 256).astype(BF16)
    dt_ref[...] = mm(3072, LANES)


def _ssm_inproj(rw, xall, mods, norm_g, w_in):
    D, tm = rw.D, rw.tm
    w = jnp.concatenate([w_in[:, 0:2560], w_in[:, 2592:3104], w_in[:, 2560:2592], jnp.zeros((D, LANES - 32), F32)],
                        axis=1).astype(BF16)
    row = lambda i: (i, 0)
    return pl.pallas_call(
        _ssm_inproj_kernel,
        grid=(rw.ntot,),
        in_specs=[pl.BlockSpec((tm, D), row), pl.BlockSpec((1, D), lambda i: (0, 0)),
                  _mod_spec(rw, 0), _mod_spec(rw, 1), pl.BlockSpec((D, 3200), lambda i: (0, 0))],
        out_specs=[pl.BlockSpec((tm, 1024), row), pl.BlockSpec((tm, 1536), row), pl.BlockSpec((tm, 512), row),
                   pl.BlockSpec((tm, LANES), row)],
        out_shape=[jax.ShapeDtypeStruct((rw.rows, 1024), BF16), jax.ShapeDtypeStruct((rw.rows, 1536), BF16),
                   jax.ShapeDtypeStruct((rw.rows, 512), BF16), jax.ShapeDtypeStruct((rw.rows, LANES), F32)],
        scratch_shapes=[pltpu.VMEM((tm, D), BF16)],
        compiler_params=_cparams(1),
        name="ssm_inproj",
    )(xall, norm_g[None, :], mods, mods, w)


def _softplus(x):
    return jnp.maximum(x, 0.0) + jnp.log(1.0 + jnp.exp(-jnp.abs(x)))


def _conv_kernel(lat_tiles, tpb, cpb, x_ref, pv_ref, nx_ref, w_ref, b_ref, dtr_ref, dtb_ref, act_ref, dt_ref):
    i = pl.program_id(0)
    is_lat = i < lat_tiles
    pos = jnp.where(is_lat, i % tpb, (i - lat_tiles) % cpb)
    last_pos = jnp.where(is_lat, tpb - 1, cpb - 1)
    x = x_ref[...].astype(F32)
    tc = x.shape[0]
    prev_row = jnp.where(pos == 0, 0.0, pv_ref[...].astype(F32)[CONV_HALO - 1:CONV_HALO, :])
    next_row = jnp.where(pos == last_pos, 0.0, nx_ref[...].astype(F32)[0:1, :])
    row = lax.broadcasted_iota(jnp.int32, x.shape, 0)
    xm1 = jnp.where(row == 0, prev_row, pltpu.roll(x, 1, 0))
    xp1 = jnp.where(row == tc - 1, next_row, pltpu.roll(x, tc - 1, 0))
    y = w_ref[0:1, :] * xm1 + w_ref[1:2, :] * x + w_ref[2:3, :] * xp1 + b_ref[...]
    act_ref[...] = _silu(y).astype(BF16)
    sp = _softplus(dtr_ref[...] + dtb_ref[...])
    dt_ref[0] = sp
    dt_ref[1] = pltpu.roll(sp, LANES - 16, 1)


def _ssm_conv(rw, xbc, dtr, conv_w, conv_b, dt_bias):
    B, T, C = rw.B, rw.T, rw.C
    tc = CONV_TILE
    assert T % tc == 0 and C % tc == 0
    lat_tiles, tpb, cpb = (B * T) // tc, T // tc, C // tc
    ntiles = rw.rows // tc
    hpt = tc // CONV_HALO
    nhalo = rw.rows // CONV_HALO
    W = xbc.shape[1]
    dtb = jnp.concatenate([dt_bias.reshape(-1), jnp.zeros((LANES - 32,), F32)])[None, :]
    row = lambda i: (i, 0)
    return pl.pallas_call(
        functools.partial(_conv_kernel, lat_tiles, tpb, cpb),
        grid=(ntiles,),
        in_specs=[pl.BlockSpec((tc, W), row),
                  pl.BlockSpec((CONV_HALO, W), lambda i: (jnp.maximum(i * hpt - 1, 0), 0)),
                  pl.BlockSpec((CONV_HALO, W), lambda i: (jnp.minimum((i + 1) * hpt, nhalo - 1), 0)),
                  pl.BlockSpec((3, W), lambda i: (0, 0)), pl.BlockSpec((1, W), lambda i: (0, 0)),
                  pl.BlockSpec((tc, LANES), row), pl.BlockSpec((1, LANES), lambda i: (0, 0))],
        out_specs=[pl.BlockSpec((tc, W), row), pl.BlockSpec((2, tc, LANES), lambda i: (0, i, 0))],
        out_shape=[jax.ShapeDtypeStruct((rw.rows, W), BF16), jax.ShapeDtypeStruct((2, rw.rows, LANES), F32)],
        compiler_params=_cparams(1),
        name="ssm_conv",
    )(xbc, xbc, xbc, conv_w, conv_b[None, :], dtr, dtb)


def _ssd_kernel(act_ref, dt_ref, tri_ref, a_ref, y_ref, hst):
    @pl.when(pl.program_id(2) == 0)
    def _():
        hst[...] = jnp.zeros_like(hst)

    q = SSD_CHUNK
    tri = tri_ref[...]
    dt = dt_ref[...]
    da = dt * a_ref[...]
    acs = jnp.dot(tri, da, preferred_element_type=F32, precision=lax.Precision.HIGHEST)
    tot = jnp.sum(da, axis=0, keepdims=True)
    acs_t = acs.T
    dt_t = dt.T
    eacs = jnp.exp(acs)
    wend = jnp.exp(tot - acs) * dt
    etot = jnp.exp(tot)
    mask = tri > 0.5
    left = lax.broadcasted_iota(jnp.int32, (q, LANES), 1) < HEAD_DIM
    left1 = lax.broadcasted_iota(jnp.int32, (1, LANES), 1) < HEAD_DIM
    for g in range(2):
        bg = act_ref[:, 1024 + g * 128:1152 + g * 128]
        cg = act_ref[:, 1280 + g * 128:1408 + g * 128]
        cb = _nt(cg, bg)
        hin = hst[:, g * 512:(g + 1) * 512]
        yoff = jnp.dot(cg, hin.astype(BF16), preferred_element_type=F32)
        xw, dec = [], []
        for pr in range(4):
            h_a = g * 8 + pr * 2
            h_b = h_a + 1
            c0 = h_a * HEAD_DIM
            x2 = act_ref[:, c0:c0 + LANES]
            outs = []
            for h in (h_a, h_b):
                seg = acs[:, h:h + 1] - acs_t[h:h + 1, :]
                w = cb * jnp.exp(jnp.where(mask, seg, NEG_INF)) * dt_t[h:h + 1, :]
                outs.append(jnp.dot(w.astype(BF16), x2, preferred_element_type=F32))
            yd = jnp.where(left, outs[0], outs[1])
            sc = jnp.where(left, eacs[:, h_a:h_a + 1], eacs[:, h_b:h_b + 1])
            y_ref[:, c0:c0 + LANES] = (yd + yoff[:, pr * LANES:(pr + 1) * LANES] * sc).astype(BF16)
            wsc = jnp.where(left, wend[:, h_a:h_a + 1], wend[:, h_b:h_b + 1])
            xw.append((x2.astype(F32) * wsc).astype(BF16))
            dec.append(jnp.where(left1, etot[:, h_a:h_a + 1], etot[:, h_b:h_b + 1]))
        bg_t = bg.astype(F32).T.astype(BF16)
        snew = jnp.dot(bg_t, jnp.concatenate(xw, axis=1), preferred_element_type=F32)
        hst[:, g * 512:(g + 1) * 512] = hin * jnp.concatenate(dec, axis=1) + snew


def _ssd(rw, act, dt2, a_log):
    B, T, C = rw.B, rw.T, rw.C
    q = SSD_CHUNK
    nct, nlt = C // q, T // q
    ctx0 = (B * T) // q
    r = np.arange(q)
    tri = jnp.asarray(np.stack([r[None, :] <= r[:, None], r[None, :] >= r[:, None]]).astype(np.float32))
    avec = jnp.concatenate([-jnp.exp(a_log.astype(F32)), jnp.zeros((2, LANES - a_log.shape[1]), F32)], axis=1)[:, None, :]

    def blk(d, b, s):
        kc = jnp.where(d == 0, s, nct - 1 - s)
        kl = jnp.where(d == 0, s - nct, nlt - 1 - (s - nct))
        return jnp.where(s < nct, ctx0 + b * nct + kc, b * nlt + kl)

    return pl.pallas_call(
        _ssd_kernel,
        grid=(2, B, nct + nlt),
        in_specs=[pl.BlockSpec((q, act.shape[1]), lambda d, b, s: (blk(d, b, s), 0)),
                  pl.BlockSpec((None, q, LANES), lambda d, b, s: (d, blk(d, b, s), 0)),
                  pl.BlockSpec((None, q, q), lambda d, b, s: (d, 0, 0)),
                  pl.BlockSpec((None, 1, LANES), lambda d, b, s: (d, 0, 0))],
        out_specs=pl.BlockSpec((None, q, 1024), lambda d, b, s: (d, blk(d, b, s), 0)),
        out_shape=jax.ShapeDtypeStruct((2, rw.rows, 1024), BF16),
        scratch_shapes=[pltpu.VMEM((q, 1024), F32)],
        compiler_params=_cparams(3),
        name="ssd_scan",
    )(act, dt2, tri, avec)


def _cmul(ar, ai, br, bi):
    return ar * br - ai * bi, ar * bi + ai * br


def _s5_weight_kernel(lre_ref, lim_ref, ls_ref, bre_ref, bim_ref, cre_ref, cim_ref,
                      wsr_ref, wsi_ref, wor_ref, woi_ref, kt_ref, are_ref, aim_ref):
    lre, lim = lre_ref[...], lim_ref[...]
    step = jnp.exp(ls_ref[...])
    er, ei = lre * step, lim * step
    npow = 24
    p = lax.broadcasted_iota(jnp.int32, (1, npow, 1), 1).astype(F32)
    mag = jnp.exp(p * er)
    pre, pim = mag * jnp.cos(p * ei), mag * jnp.sin(p * ei)
    a_re, a_im = pre[:, 1:2, :], pim[:, 1:2, :]
    den = lre * lre + lim * lim
    q_re = ((a_re - 1.0) * lre + a_im * lim) / den
    q_im = (a_im * lre - (a_re - 1.0) * lim) / den
    bb_re, bb_im = _cmul(q_re, q_im, bre_ref[...], bim_ref[...])
    c_re, c_im = cre_ref[...], cim_ref[...]
    ws_r, ws_i, wo_r, wo_i, ca_r, ca_i = [], [], [], [], [], []
    for t in range(S5_Q):
        r, i = _cmul(bb_re, bb_im, pre[:, t:t + 1, :], pim[:, t:t + 1, :])
        ws_r.append(r)
        ws_i.append(i)
        r, i = _cmul(c_re, c_im, pre[:, t:t + 1, :], pim[:, t:t + 1, :])
        ca_r.append(r)
        ca_i.append(i)
        r, i = _cmul(c_re, c_im, pre[:, t + 1:t + 2, :], pim[:, t + 1:t + 2, :])
        wo_r.append(r)
        wo_i.append(-i)
    cat = lambda xs: jnp.concatenate(xs, axis=1)
    wsr_ref[...] = cat(ws_r)
    wsi_ref[...] = cat(ws_i)
    wor_ref[...] = cat(wo_r)
    woi_ref[...] = cat(wo_i)
    bdot = lambda a, b: lax.dot_general(a, b, (((2,), (2,)), ((0,), (0,))), preferred_element_type=F32,
                                        precision=lax.Precision.HIGHEST)
    kt_ref[...] = bdot(cat(ca_r), bb_re) - bdot(cat(ca_i), bb_im)
    are_ref[...] = pre[:, S5_Q:S5_Q + 1, :]
    aim_ref[...] = pim[:, S5_Q:S5_Q + 1, :]


def _s5_weights(lam_re, lam_im, log_step, b_re, b_im, c_re, c_im):
    nd, ng, ns = lam_re.shape
    G = nd * ng
    ch = S5_GROUP
    gb = 8
    qc = S5_Q * ch
    f = lambda a: a.astype(F32)
    args = (f(lam_re).reshape(G, 1, ns), f(lam_im).reshape(G, 1, ns), f(log_step).reshape(G, 1, 1),
            f(b_re).reshape(G, ns, ch).transpose(0, 2, 1), f(b_im).reshape(G, ns, ch).transpose(0, 2, 1),
            f(c_re).reshape(G, ch, ns), f(c_im).reshape(G, ch, ns))
    spec = lambda a: pl.BlockSpec((gb,) + a.shape[1:], lambda i: (i, 0, 0))
    oshape = [jax.ShapeDtypeStruct((G, qc, ns), F32)] * 4 + [jax.ShapeDtypeStruct((G, qc, ch), F32)] \
        + [jax.ShapeDtypeStruct((G, 1, ns), F32)] * 2
    wsr, wsi, wor, woi, kt, a_re, a_im = pl.pallas_call(
        _s5_weight_kernel,
        grid=(G // gb,),
        in_specs=[spec(a) for a in args],
        out_specs=[pl.BlockSpec((gb,) + s.shape[1:], lambda i: (i, 0, 0)) for s in oshape],
        out_shape=oshape,
        compiler_params=_cparams(1),
        name="s5_weights",
    )(*args)

    def by_dir(w, flip_dir):
        w = w.reshape(nd, ng, S5_Q, ch, ns)
        w = jnp.stack([jnp.flip(w[d], axis=1) if d == flip_dir else w[d] for d in range(nd)])
        return w.reshape(nd, ng, qc, ns)

    def pack(w):
        z = jnp.zeros_like(w)
        even = (np.arange(ng) % 2 == 0)[None, :, None, None]
        return jnp.where(even, jnp.concatenate([w, z], axis=-1), jnp.concatenate([z, w], axis=-1)).astype(BF16)

    ws_r, ws_i = pack(by_dir(wsr, 0)), pack(by_dir(wsi, 0))
    wo_r, wo_i = pack(by_dir(wor, 1)), pack(by_dir(woi, 1))
    k = kt.reshape(nd, ng, S5_Q, ch, ch).transpose(0, 1, 2, 4, 3)
    j = np.arange(S5_Q)[:, None]
    i = np.arange(S5_Q)[None, :]
    tau = jnp.asarray(np.abs(i - j).reshape(-1))
    bt = jnp.take(k, tau, axis=2).reshape(nd, ng, S5_Q, S5_Q, ch, ch)
    causal = np.stack([i >= j, i <= j])[:, None, :, :, None, None]
    bt = jnp.where(jnp.asarray(causal), bt, 0.0).transpose(0, 1, 2, 4, 3, 5).reshape(nd, ng, qc, qc).astype(BF16)
    pair = lambda a: a.reshape(nd, ng // 2, 1, 2 * ns)
    return bt, ws_r, ws_i, wo_r, wo_i, pair(a_re), pair(a_im)


def _s5_kernel(B, nct, nlt, u_ref, bt_ref, wsr_ref, wsi_ref, wor_ref, woi_ref, are_ref, aim_ref, y_ref, s_re, s_im):
    gb = u_ref.shape[0]
    npair = gb // 2
    for d in range(2):
        for pr in range(npair):
            for dst, w_ref in ((s_re, wsr_ref), (s_im, wsi_ref)):
                dst[d, pr] = (jnp.dot(u_ref[2 * pr], w_ref[d, 2 * pr], preferred_element_type=F32)
                              + jnp.dot(u_ref[2 * pr + 1], w_ref[d, 2 * pr + 1], preferred_element_type=F32))
    chains = [(d, pr) for d in range(2) for pr in range(npair)]
    coef = [(are_ref[d, pr], aim_ref[d, pr]) for d, pr in chains]

    def body(s, carry):
        c_bwd = jnp.where(s < nct, nct - 1 - s, 2 * nct + nlt - 1 - s)
        new = []
        for (d, pr), (ar, ai), (hr, hi) in zip(chains, coef, carry):
            rows = pl.ds((s if d == 0 else c_bwd) * B, B)
            sr = s_re[d, pr, rows, :]
            si = s_im[d, pr, rows, :]
            s_re[d, pr, rows, :] = hr
            s_im[d, pr, rows, :] = hi
            new.append((ar * hr - ai * hi + sr, ar * hi + ai * hr + si))
        return tuple(new)

    zero = jnp.zeros((B, LANES), F32)
    lax.fori_loop(0, nct + nlt, body, tuple((zero, zero) for _ in chains))
    for g in range(gb):
        acc = None
        for d in range(2):
            t = (jnp.dot(u_ref[g], bt_ref[d, g], preferred_element_type=F32)
                 + _nt(s_re[d, g // 2].astype(BF16), wor_ref[d, g])
                 + _nt(s_im[d, g // 2].astype(BF16), woi_ref[d, g]))
            acc = t if acc is None else acc + t
        y_ref[g] = acc.astype(BF16)


def _s5(rw, u, weights):
    B, T, C = rw.B, rw.T, rw.C
    bt, ws_r, ws_i, wo_r, wo_i, a_re, a_im = weights
    ng = bt.shape[1]
    ch = S5_GROUP
    q = S5_Q
    nct, nlt = C // q, T // q
    nch = nct + nlt
    qc = q * ch
    gb = 4
    ul = u[:B * T].reshape(B, nlt, q, ng, ch)
    uc = u[B * T:].reshape(B, nct, q, ng, ch)
    ug = jnp.concatenate([uc, ul], axis=1).transpose(3, 1, 0, 2, 4).reshape(ng, nch * B, qc)
    wspec = lambda n: pl.BlockSpec((2, gb, qc, n), lambda i: (0, i, 0, 0))
    aspec = pl.BlockSpec((2, gb // 2, 1, LANES), lambda i: (0, i, 0, 0))
    yg = pl.pallas_call(
        functools.partial(_s5_kernel, B, nct, nlt),
        grid=(ng // gb,),
        in_specs=[pl.BlockSpec((gb, nch * B, qc), lambda i: (i, 0, 0)), wspec(qc), wspec(LANES), wspec(LANES),
                  wspec(LANES), wspec(LANES), aspec, aspec],
        out_specs=pl.BlockSpec((gb, nch * B, qc), lambda i: (i, 0, 0)),
        out_shape=jax.ShapeDtypeStruct((ng, nch * B, qc), BF16),
        scratch_shapes=[pltpu.VMEM((2, gb // 2, nch * B, LANES), F32), pltpu.VMEM((2, gb // 2, nch * B, LANES), F32)],
        compiler_params=_cparams(1),
        name="s5_scan",
    )(ug, bt, ws_r, ws_i, wo_r, wo_i, a_re, a_im)
    yl = yg.reshape(ng, nch, B, q, ch)[:, nct:]
    return yl.transpose(2, 1, 3, 0, 4).reshape(B * T, ng * ch)


def _gelu_tanh(x):
    return 0.5 * x * (1.0 + jnp.tanh(math.sqrt(2.0 / math.pi) * (x + 0.044715 * (x * x * x))))


def _ssm_outproj_kernel(y0_ref, y1_ref, xs_ref, z_ref, v_ref, u_ref, dsk_ref, nw_ref, s5d_ref, gw_ref, gb_ref,
                        x_ref, w_ref, g1_ref, gn_ref, sh2_ref, sc2_ref, wr_ref, br_ref, lt_ref,
                        xo_ref, h2_ref, rt_ref, cnt_ref, carry):
    i = pl.program_id(0)
    y = y0_ref[...].astype(F32) + y1_ref[...].astype(F32) + dsk_ref[...] * xs_ref[...].astype(F32)
    y = _rms(y * _silu(z_ref[...].astype(F32))) * nw_ref[...]
    v = _gelu_tanh(v_ref[...].astype(F32) + s5d_ref[...] * u_ref[...].astype(F32))
    v = v * _sigmoid(jnp.dot(v.astype(BF16), gw_ref[...], preferred_element_type=F32) + gb_ref[...])
    mix = jnp.concatenate([y, v], axis=1).astype(BF16)
    yo = jnp.dot(mix, w_ref[...], preferred_element_type=F32)
    _post_mixer(i, x_ref[...], yo, g1_ref[...], gn_ref[...], sh2_ref[...], sc2_ref[...], wr_ref, br_ref, lt_ref,
                xo_ref, h2_ref, rt_ref, cnt_ref, carry)


def _ssm_outproj(rw, ssd_y, act, z, s5_y, u, d_skip, norm_w, s5_d, glu_w, glu_b, xall, w_out, mods, norm_ffn, wr, br):
    D, tm = rw.D, rw.tm
    ntiles = rw.nlat
    post_in, post_out = _post_specs(rw)
    row = lambda i: (i, 0)
    vec = lambda n: pl.BlockSpec((1, n), lambda i: (0, 0))
    dsk = jnp.repeat(d_skip.astype(F32), HEAD_DIM)[None, :]
    return pl.pallas_call(
        _ssm_outproj_kernel,
        grid=(ntiles,),
        in_specs=[pl.BlockSpec((None, tm, 1024), lambda i: (0, i, 0)), pl.BlockSpec((None, tm, 1024), lambda i: (1, i, 0)),
                  pl.BlockSpec((tm, 1024), row), pl.BlockSpec((tm, 1024), row), pl.BlockSpec((tm, 512), row),
                  pl.BlockSpec((tm, 512), row), vec(1024), vec(1024), vec(512),
                  pl.BlockSpec((512, 512), lambda i: (0, 0)), vec(512),
                  pl.BlockSpec((tm, D), row), pl.BlockSpec((1536, D), lambda i: (0, 0)), _mod_spec(rw, 2)] + post_in,
        out_specs=post_out,
        out_shape=_post_shapes(ntiles * tm, D),
        scratch_shapes=[pltpu.VMEM((1, LANES), F32)],
        compiler_params=_cparams(1),
        name="ssm_outproj_router",
    )(ssd_y, ssd_y, act, z, s5_y, u, dsk, norm_w[None, :], s5_d[None, :], glu_w.astype(BF16), glu_b[None, :],
      xall, w_out.astype(BF16), mods, norm_ffn[None, :], mods, mods, wr, br, _lower_tri(tm))


def kernel(x, c, ctx, c_ctx, mod_w, mod_b, norm_mix, norm_ffn, att_w_in, att_w_out, na_q_norm, na_k_norm, na_rel_bias, wa_q_norm, wa_k_norm, wa_sink, ssm_w_in, ssm_w_out, ssd_conv_w, ssd_conv_b, ssd_dt_bias, ssd_a_log, ssd_d, ssd_norm, s5_lambda_re, s5_lambda_im, s5_log_step, s5_b_re, s5_b_im, s5_c_re, s5_c_im, s5_d, s5_glu_w, s5_glu_b, moe_w_group, moe_b_group, moe_w_expert, moe_b_expert, moe_w13, moe_w2):
    B, T, D = x.shape
    C = ctx.shape[1]
    rw = _Rows(B, T, C, D, ROW_TILE)
    xl = x.reshape(B * T, D)
    xc = ctx.reshape(B * C, D)
    cm = jnp.concatenate([c, c_ctx[None, :], jnp.zeros((8 - B - 1, D), F32)], axis=0)
    mods = _modulation(cm, mod_w, mod_b)
    mods = mods.reshape(mods.shape[0], 8, 1, 6 * D)

    m0 = mods[0]
    qkv = _att_inproj(rw, xl, xc, m0, norm_mix[0], att_w_in[0], na_q_norm[0], na_k_norm[0], wa_q_norm[0],
                      wa_k_norm[0])
    na = _na_attention(rw, qkv, na_rel_bias[0])
    wa = _wa_attention(rw, qkv, wa_sink[0])
    cx = _ctx_attention(rw, qkv, wa_sink[0])
    wr, br = _router_weights(moe_w_group[0], moe_b_group[0], moe_w_expert[0], moe_b_expert[0])
    xall, h2, route, counts = _att_outproj(rw, na, wa, cx, xl, xc, att_w_out[0], m0, norm_ffn[0], wr, br)
    y1, y2 = _moe(h2, route, counts, moe_w13[0], moe_w2[0])
    xall = _combine(rw, rw.ntot, xall, y1, y2, route, m0)

    m1 = mods[1]
    z, xbc, u, dtr = _ssm_inproj(rw, xall, m1, norm_mix[1], ssm_w_in[0])
    act, dt2 = _ssm_conv(rw, xbc, dtr, ssd_conv_w[0], ssd_conv_b[0], ssd_dt_bias[0])
    ssd_y = _ssd(rw, act, dt2, ssd_a_log[0])
    s5_w = _s5_weights(s5_lambda_re[0], s5_lambda_im[0], s5_log_step[0], s5_b_re[0], s5_b_im[0], s5_c_re[0],
                       s5_c_im[0])
    s5_y = _s5(rw, u, s5_w)
    wr, br = _router_weights(moe_w_group[1], moe_b_group[1], moe_w_expert[1], moe_b_expert[1])
    xlat, h2, route, counts = _ssm_outproj(rw, ssd_y, act, z, s5_y, u, ssd_d[0], ssd_norm[0], s5_d[0], s5_glu_w[0],
                                           s5_glu_b[0], xall, ssm_w_out[0], m1, norm_ffn[1], wr, br)
    y1, y2 = _moe(h2, route, counts, moe_w13[1], moe_w2[1])
    out = _combine(rw, rw.nlat, xlat, y1, y2, route, m1)
    return out.reshape(B, T, D)
```

```python
import functools
import math

import jax
import jax.numpy as jnp
import numpy as np
from jax import lax
from jax.experimental import pallas as pl
from jax.experimental.pallas import tpu as pltpu

F32 = jnp.float32
BF16 = jnp.bfloat16

EPS = 1e-6
NEG_INF = -1e30
GRID_W = 64
HEAD_DIM = 64
NA_KH = 8
NA_KW = 16
WA_BLOCK = 128
ROPE_BASE = 10000.0
SSD_CHUNK = 128
S5_GROUP = 16
S5_STATE = 64
MOE_GROUPS = 4
MOE_EPG = 8
MOE_EXPERTS = MOE_GROUPS * MOE_EPG

LANES = 128
ROW_TILE = 512
MOE_TILE = 256
VMEM_LIMIT = 56 * 1024 * 1024


def _cparams(n_axes, vmem=VMEM_LIMIT):
    return pltpu.CompilerParams(dimension_semantics=("arbitrary",) * n_axes, vmem_limit_bytes=vmem)


def _sigmoid(x):
    return 1.0 / (1.0 + jnp.exp(-x))


def _silu(x):
    return x * _sigmoid(x)


def _rms(x, eps=EPS):
    return x * lax.rsqrt(jnp.mean(x * x, axis=-1, keepdims=True) + eps)


def _ada_norm(x, g, shift, scale):
    return (_rms(x) * g) * (1.0 + scale) + shift


def _mod_kernel(c_ref, w_ref, b_ref, o_ref):
    a = _silu(c_ref[...])
    o_ref[...] = jnp.dot(a, w_ref[...], preferred_element_type=F32, precision=lax.Precision.HIGHEST) + b_ref[...]


def _modulation(cm, mod_w, mod_b):
    depth, d, n6 = mod_w.shape
    tn = 1024
    return pl.pallas_call(
        _mod_kernel,
        grid=(depth, n6 // tn),
        in_specs=[pl.BlockSpec((8, d), lambda l, j: (0, 0)),
                  pl.BlockSpec((None, d, tn), lambda l, j: (l, 0, j)),
                  pl.BlockSpec((None, 1, tn), lambda l, j: (l, 0, j))],
        out_specs=pl.BlockSpec((None, 8, tn), lambda l, j: (l, 0, j)),
        out_shape=jax.ShapeDtypeStruct((depth, 8, n6), F32),
        compiler_params=_cparams(2),
        name="modulation",
    )(cm, mod_w, mod_b.reshape(depth, 1, n6))


class _Rows:
    def __init__(self, B, T, C, D, tm):
        assert T % tm == 0 and (B * C) % tm == 0
        self.B, self.T, self.C, self.D, self.tm = B, T, C, D, tm
        self.tpb = T // tm
        self.nlat = B * self.tpb
        self.nctx = (B * C) // tm
        self.ntot = self.nlat + self.nctx
        self.rows = B * (T + C)

    def group(self, i):
        return jnp.where(i < self.nlat, i // self.tpb, self.B)


def _mod_spec(rw, col):
    return pl.BlockSpec((None, 1, rw.D), lambda i, *_: (rw.group(i), 0, col))


def _seg_norm(y, seg, gcol):
    ss = jnp.dot((y * y).astype(BF16), seg, preferred_element_type=F32)
    return y * lax.rsqrt(ss + EPS) * gcol


def _rope(y, cos, sin):
    w = y.shape[-1]
    lane = lax.broadcasted_iota(jnp.int32, y.shape, 1)
    first = (lane % 32) < 16
    partner = jnp.where(first, pltpu.roll(y, w - 16, 1), pltpu.roll(y, 16, 1))
    return y * cos + partner * sin


def _dup_halves(k):
    lane = lax.broadcasted_iota(jnp.int32, k.shape, 1)
    sw = pltpu.roll(k, 64, 1)
    return jnp.where(lane < 64, k, sw), jnp.where(lane < 64, sw, k)


def _att_inproj_kernel(nlat, xl_ref, xc_ref, g_ref, sh_ref, sc_ref, w_ref, gcol_ref, cos_ref, sin_ref, seg_ref,
                       o_ref, h_scr):
    i = pl.program_id(0)
    x = jnp.where(i < nlat, xl_ref[...], xc_ref[...])
    h_scr[...] = _ada_norm(x, g_ref[...], sh_ref[...], sc_ref[...]).astype(BF16)
    seg = seg_ref[...]
    cos2 = jnp.concatenate([cos_ref[...], cos_ref[...]], axis=1)
    sin2 = jnp.concatenate([sin_ref[...], sin_ref[...]], axis=1)
    for c in range(9):
        c0 = c * 256
        y = jnp.dot(h_scr[...], w_ref[:, c0:c0 + 256], preferred_element_type=F32)
        gcol = gcol_ref[:, c0:c0 + 256]
        if c in (0, 1, 2, 3):
            o_ref[:, c0:c0 + 256] = _seg_norm(y, seg, gcol).astype(BF16)
        elif c in (4, 5):
            o_ref[:, c0:c0 + 256] = y.astype(BF16)
        elif c in (6, 7):
            o_ref[:, c0:c0 + 256] = _rope(_seg_norm(y, seg, gcol), cos2, sin2).astype(BF16)
        else:
            lane = lax.broadcasted_iota(jnp.int32, y.shape, 1)
            yk = jnp.where(lane < 128, _seg_norm(y, seg, gcol), y)
            yr = jnp.where(lane < 128, _rope(yk, cos2, sin2), yk)
            k0, k1 = _dup_halves(yr[:, :128])
            v0, v1 = _dup_halves(yr[:, 128:])
            o_ref[:, 2048:2176] = k0.astype(BF16)
            o_ref[:, 2176:2304] = k1.astype(BF16)
            o_ref[:, 2304:2432] = v0.astype(BF16)
            o_ref[:, 2432:2560] = v1.astype(BF16)


def _rope_tables(T, tm):
    t = np.arange(T)
    d = np.arange(HEAD_DIM)
    nf = HEAD_DIM // 4
    inv = jnp.asarray(ROPE_BASE, F32) ** (-jnp.arange(nf, dtype=F32) / nf)
    pos = np.where((d // 32 == 0)[None, :], (t // GRID_W)[:, None], (t % GRID_W)[:, None])
    ang = jnp.asarray(pos, F32) * inv[d % nf][None, :]
    sign = np.where((d % 32) < 16, -1.0, 1.0).astype(np.float32)
    cos = jnp.cos(ang)
    sin = jnp.sin(ang) * sign[None, :]
    cos = jnp.concatenate([cos, jnp.ones((tm, HEAD_DIM), F32)], axis=0)
    sin = jnp.concatenate([sin, jnp.zeros((tm, HEAD_DIM), F32)], axis=0)
    return jnp.tile(cos, (1, 2)), jnp.tile(sin, (1, 2))


def _att_inproj(rw, xl, xc, mods, norm_g, w_in, na_qn, na_kn, wa_qn, wa_kn):
    D, tm = rw.D, rw.tm
    scale = HEAD_DIM ** -0.5
    gcol = jnp.concatenate([jnp.tile(na_qn * scale, 8), jnp.tile(na_kn, 8), jnp.ones((512,), F32),
                            jnp.tile(wa_qn * scale, 8), jnp.tile(wa_kn, 2), jnp.ones((128,), F32)])[None, :]
    cos, sin = _rope_tables(rw.T, tm)
    segn = np.arange(256) // 64
    seg = jnp.asarray((segn[:, None] == segn[None, :]).astype(np.float32) / 64.0, BF16)
    nlat, tpb = rw.nlat, rw.tpb
    return pl.pallas_call(
        functools.partial(_att_inproj_kernel, nlat),
        grid=(rw.ntot,),
        in_specs=[pl.BlockSpec((tm, D), lambda i: (jnp.minimum(i, nlat - 1), 0)),
                  pl.BlockSpec((tm, D), lambda i: (jnp.maximum(i - nlat, 0), 0)),
                  pl.BlockSpec((1, D), lambda i: (0, 0)),
                  _mod_spec(rw, 0), _mod_spec(rw, 1),
                  pl.BlockSpec((D, 2304), lambda i: (0, 0)),
                  pl.BlockSpec((1, 2304), lambda i: (0, 0)),
                  pl.BlockSpec((tm, 128), lambda i: (jnp.where(i < nlat, i % tpb, tpb), 0)),
                  pl.BlockSpec((tm, 128), lambda i: (jnp.where(i < nlat, i % tpb, tpb), 0)),
                  pl.BlockSpec((256, 256), lambda i: (0, 0))],
        out_specs=pl.BlockSpec((tm, 2560), lambda i: (i, 0)),
        out_shape=jax.ShapeDtypeStruct((rw.rows, 2560), BF16),
        scratch_shapes=[pltpu.VMEM((tm, D), BF16)],
        compiler_params=_cparams(1),
        name="att_inproj",
    )(xl, xc, norm_g[None, :], mods, mods, w_in.astype(BF16), gcol, cos, sin, seg)


def _route(lg, lt, carry):
    lane = lax.broadcasted_iota(jnp.int32, lg.shape, 1).astype(F32)
    gm = lane < MOE_GROUPS
    mg = jnp.max(jnp.where(gm, lg, NEG_INF), axis=-1, keepdims=True)
    eg = jnp.where(gm, jnp.exp(jnp.where(gm, lg, NEG_INF) - mg), 0.0)
    pg = eg / jnp.sum(eg, axis=-1, keepdims=True)
    ptop = jnp.max(pg, axis=-1, keepdims=True)
    gidx = jnp.min(jnp.where(gm & (pg == ptop), lane, 1e9), axis=-1, keepdims=True)
    lo = MOE_GROUPS + MOE_EPG * gidx
    em = (lane >= lo) & (lane < lo + MOE_EPG)
    le = jnp.where(em, lg, NEG_INF)
    ee = jnp.where(em, jnp.exp(le - jnp.max(le, axis=-1, keepdims=True)), 0.0)
    pe = ee / jnp.sum(ee, axis=-1, keepdims=True)
    v1 = jnp.max(jnp.where(em, pe, -1.0), axis=-1, keepdims=True)
    i1 = jnp.min(jnp.where(em & (pe == v1), lane, 1e9), axis=-1, keepdims=True)
    em2 = em & (lane != i1)
    v2 = jnp.max(jnp.where(em2, pe, -1.0), axis=-1, keepdims=True)
    i2 = jnp.min(jnp.where(em2 & (pe == v2), lane, 1e9), axis=-1, keepdims=True)
    den = v1 + v2
    w1 = v1 / den * ptop
    w2 = v2 / den * ptop
    e1 = i1 - MOE_GROUPS
    e2 = i2 - MOE_GROUPS
    m1 = lane == e1
    m2 = lane == e2
    oh = jnp.where(m1 | m2, 1.0, 0.0)
    cnt = jnp.dot(lt, oh.astype(BF16), preferred_element_type=F32) + carry
    r1 = jnp.sum(jnp.where(m1, cnt, 0.0), axis=-1, keepdims=True)
    r2 = jnp.sum(jnp.where(m2, cnt, 0.0), axis=-1, keepdims=True)
    route = jnp.where(lane == 0, e1, jnp.where(lane == 1, e2, jnp.where(lane == 2, w1, jnp.where(
        lane == 3, w2, jnp.where(lane == 4, r1, jnp.where(lane == 5, r2, 0.0))))))
    return route, carry + jnp.sum(oh, axis=0, keepdims=True)


def _post_mixer(i, x, y, g1, gn, sh2, sc2, wr_ref, br_ref, lt_ref, xo_ref, h2_ref, rt_ref, cnt_ref, carry):
    xn = x + g1 * y
    xo_ref[...] = xn
    h2 = _ada_norm(xn, gn, sh2, sc2)
    hb = h2.astype(BF16)
    h2_ref[...] = hb
    hl = (h2 - hb.astype(F32)).astype(BF16)
    lg = (jnp.dot(hb, wr_ref[0], preferred_element_type=F32)
          + (jnp.dot(hb, wr_ref[1], preferred_element_type=F32) + jnp.dot(hl, wr_ref[0], preferred_element_type=F32))
          + br_ref[...])

    @pl.when(i == 0)
    def _():
        carry[...] = jnp.zeros_like(carry)

    route, newc = _route(lg, lt_ref[...], carry[...])
    rt_ref[...] = route
    carry[...] = newc
    cnt_ref[...] = newc


def _att_outproj_kernel(nlat, na_ref, wa_ref, cx_ref, xl_ref, xc_ref, w_ref, g1_ref, gn_ref, sh2_ref, sc2_ref,
                        wr_ref, br_ref, lt_ref, xo_ref, h2_ref, rt_ref, cnt_ref, carry):
    i = pl.program_id(0)
    lat = i < nlat
    mix = jnp.where(lat, jnp.concatenate([na_ref[...], wa_ref[...]], axis=1), cx_ref[...])
    y = jnp.dot(mix, w_ref[...], preferred_element_type=F32)
    x = jnp.where(lat, xl_ref[...], xc_ref[...])
    _post_mixer(i, x, y, g1_ref[...], gn_ref[...], sh2_ref[...], sc2_ref[...], wr_ref, br_ref, lt_ref,
                xo_ref, h2_ref, rt_ref, cnt_ref, carry)


def _router_weights(w_group, b_group, w_expert, b_expert):
    D = w_group.shape[0]
    pad = LANES - MOE_GROUPS - MOE_EXPERTS
    wr = jnp.concatenate([w_group, w_expert, jnp.zeros((D, pad), F32)], axis=1)
    br = jnp.concatenate([b_group, b_expert, jnp.zeros((pad,), F32)])[None, :]
    hi = wr.astype(BF16)
    lo = (wr - hi.astype(F32)).astype(BF16)
    return jnp.stack([hi, lo]), br


def _lower_tri(tm):
    r = np.arange(tm)
    return jnp.asarray((r[None, :] < r[:, None]).astype(np.float32), BF16)


def _post_specs(rw):
    D, tm = rw.D, rw.tm
    return ([pl.BlockSpec((1, D), lambda i: (0, 0)), _mod_spec(rw, 3), _mod_spec(rw, 4),
             pl.BlockSpec((2, D, LANES), lambda i: (0, 0, 0)), pl.BlockSpec((1, LANES), lambda i: (0, 0)),
             pl.BlockSpec((tm, tm), lambda i: (0, 0))],
            [pl.BlockSpec((tm, D), lambda i: (i, 0)), pl.BlockSpec((tm, D), lambda i: (i, 0)),
             pl.BlockSpec((tm, LANES), lambda i: (i, 0)), pl.BlockSpec((1, LANES), lambda i: (0, 0))])


def _post_shapes(nrows, D):
    return [jax.ShapeDtypeStruct((nrows, D), F32), jax.ShapeDtypeStruct((nrows, D), BF16),
            jax.ShapeDtypeStruct((nrows, LANES), F32), jax.ShapeDtypeStruct((1, LANES), F32)]


def _att_outproj(rw, na, wa, cx, xl, xc, w_out, mods, norm_ffn, wr, br):
    D, tm, nlat = rw.D, rw.tm, rw.nlat
    post_in, post_out = _post_specs(rw)
    latmap = lambda i: (jnp.minimum(i, nlat - 1), 0)
    ctxmap = lambda i: (jnp.maximum(i - nlat, 0), 0)
    return pl.pallas_call(
        functools.partial(_att_outproj_kernel, nlat),
        grid=(rw.ntot,),
        in_specs=[pl.BlockSpec((tm, 512), latmap), pl.BlockSpec((tm, 512), latmap), pl.BlockSpec((tm, D), ctxmap),
                  pl.BlockSpec((tm, D), latmap), pl.BlockSpec((tm, D), ctxmap),
                  pl.BlockSpec((D, D), lambda i: (0, 0)), _mod_spec(rw, 2)] + post_in,
        out_specs=post_out,
        out_shape=_post_shapes(rw.rows, D),
        scratch_shapes=[pltpu.VMEM((1, LANES), F32)],
        compiler_params=_cparams(1),
        name="att_outproj_router",
    )(na, wa, cx, xl, xc, w_out.astype(BF16), mods, norm_ffn[None, :], mods, mods, wr, br, _lower_tri(tm))


def _moe_kernel(te_ref, nu_ref, xs_ref, w13_ref, w2_ref, o_ref, w13b, w2b):
    i = pl.program_id(0)
    prev = te_ref[jnp.maximum(i - 1, 0)]
    changed = (i == 0) | (te_ref[i] != prev)

    @pl.when(changed)
    def _():
        w13b[...] = w13_ref[...].astype(BF16)
        w2b[...] = w2_ref[...].astype(BF16)

    @pl.when(i < nu_ref[0])
    def _():
        ff = w2b.shape[0]
        a13 = jnp.dot(xs_ref[...], w13b[...], preferred_element_type=F32)
        act = _silu(a13[:, :ff]) * a13[:, ff:]
        o_ref[...] = jnp.dot(act.astype(BF16), w2b[...], preferred_element_type=F32).astype(BF16)

    @pl.when(i >= nu_ref[0])
    def _():
        o_ref[...] = jnp.zeros_like(o_ref)


def _moe(h2, route, counts, w13, w2):
    N, D = h2.shape
    E, _, F2 = w13.shape
    tg = MOE_TILE
    nt = (2 * N) // tg + E
    e = route[:, 0:2].astype(jnp.int32)
    rank = route[:, 4:6].astype(jnp.int32)
    cnt = counts[0, :E].astype(jnp.int32)
    ntile_e = (cnt + tg - 1) // tg
    tile_end = jnp.cumsum(ntile_e)
    offs = (tile_end - ntile_e) * tg
    dest = offs[e] + rank
    src = jnp.zeros((nt * tg,), jnp.int32).at[dest.reshape(-1)].set(jnp.repeat(jnp.arange(N, dtype=jnp.int32), 2))
    tile_id = jnp.arange(nt, dtype=jnp.int32)
    nu = tile_end[-1:].astype(jnp.int32)
    te = jnp.sum((tile_end[None, :] <= jnp.minimum(tile_id, nu[0] - 1)[:, None]).astype(jnp.int32), axis=1)
    te = jnp.minimum(te, E - 1)
    xs = jnp.take(h2, src, axis=0)
    ys = pl.pallas_call(
        _moe_kernel,
        grid_spec=pltpu.PrefetchScalarGridSpec(
            num_scalar_prefetch=2,
            grid=(nt,),
            in_specs=[pl.BlockSpec((tg, D), lambda i, te, nu: (i, 0)),
                      pl.BlockSpec((None, D, F2), lambda i, te, nu: (te[i], 0, 0)),
                      pl.BlockSpec((None, F2 // 2, D), lambda i, te, nu: (te[i], 0, 0))],
            out_specs=pl.BlockSpec((tg, D), lambda i, te, nu: (i, 0)),
            scratch_shapes=[pltpu.VMEM((D, F2), BF16), pltpu.VMEM((F2 // 2, D), BF16)]),
        out_shape=jax.ShapeDtypeStruct((nt * tg, D), BF16),
        compiler_params=_cparams(1),
        name="moe_experts",
    )(te, nu, xs, w13, w2)
    return jnp.take(ys, dest[:, 0], axis=0), jnp.take(ys, dest[:, 1], axis=0)


def _combine_kernel(x_ref, y1_ref, y2_ref, rt_ref, g2_ref, o_ref):
    rt = rt_ref[...]
    f = rt[:, 2:3] * y1_ref[...].astype(F32) + rt[:, 3:4] * y2_ref[...].astype(F32)
    o_ref[...] = x_ref[...] + g2_ref[...] * f


def _combine(rw, ntiles, xall, y1, y2, route, mods):
    D, tm = rw.D, rw.tm
    row = lambda i: (i, 0)
    return pl.pallas_call(
        _combine_kernel,
        grid=(ntiles,),
        in_specs=[pl.BlockSpec((tm, D), row), pl.BlockSpec((tm, D), row), pl.BlockSpec((tm, D), row),
                  pl.BlockSpec((tm, LANES), row), _mod_spec(rw, 5)],
        out_specs=pl.BlockSpec((tm, D), row),
        out_shape=jax.ShapeDtypeStruct((ntiles * tm, D), F32),
        compiler_params=_cparams(1),
        name="moe_combine",
    )(xall, y1, y2, route, mods)


NA_QROWS = 8
NA_KROWS = 16


def _na_bias(rpb, n_rb):
    H = rpb.shape[0]
    i = np.arange(GRID_W)
    c0 = np.clip(i - NA_KW // 2, 0, GRID_W - NA_KW)
    j = np.arange(GRID_W)
    colvalid = (j[None, :] >= c0[:, None]) & (j[None, :] < c0[:, None] + NA_KW)
    dc = np.clip(j[None, :] - i[:, None] + NA_KW - 1, 0, 2 * NA_KW - 2)
    onehot = ((dc[None] == np.arange(2 * NA_KW - 1)[:, None, None]) & colvalid[None]).astype(np.float32)
    tiles = jnp.einsum('hrc,cij->hrij', rpb.astype(F32), jnp.asarray(onehot), precision=lax.Precision.HIGHEST)
    tiles = tiles + jnp.asarray(np.where(colvalid, 0.0, NEG_INF).astype(np.float32))
    flat = tiles.transpose(0, 2, 1, 3).reshape(H, GRID_W, (2 * NA_KH - 1) * GRID_W)
    blocks = []
    for variant in range(3):
        for a in range(NA_QROWS):
            start = (max(a - 4, 0) + 4, a, min(a, 4))[variant]
            dr0 = start - a + 3
            neg = lambda n: jnp.full((H, GRID_W, n * GRID_W), NEG_INF, F32)
            blocks.append(jnp.concatenate([neg(start), flat[:, :, dr0 * GRID_W:(dr0 + NA_KH) * GRID_W],
                                           neg(NA_KROWS - NA_KH - start)], axis=-1))
    return jnp.stack(blocks, axis=1).reshape(H, 3, NA_QROWS * GRID_W, NA_KROWS * GRID_W)


def _softmax_pv(parts, extra=None):
    mx = functools.reduce(jnp.maximum, [jnp.max(s, axis=-1, keepdims=True) for s, _ in parts])
    if extra is not None:
        mx = jnp.maximum(mx, extra)
    l = jnp.zeros_like(mx) if extra is None else jnp.exp(extra - mx)
    o = None
    for s, v in parts:
        p = jnp.exp(s - mx)
        l = l + jnp.sum(p, axis=-1, keepdims=True)
        pv = jnp.dot(p.astype(BF16), v, preferred_element_type=F32)
        o = pv if o is None else o + pv
    return o / l


def _nt(a, b):
    return lax.dot_general(a, b, (((1,), (1,)), ((), ())), preferred_element_type=F32)


def _na_kernel(q_ref, k0, k1, k2, k3, v0, v1, v2, v3, kc_ref, vc_ref, bias_ref, o_ref):
    q2 = q_ref[...]
    kw = jnp.concatenate([k0[...], k1[...], k2[...], k3[...]], axis=0)
    vw = jnp.concatenate([v0[...], v1[...], v2[...], v3[...]], axis=0)
    kc = kc_ref[...]
    vc = vc_ref[...]
    lane = lax.broadcasted_iota(jnp.int32, q2.shape, 1)
    out = jnp.zeros(q2.shape, F32)
    for hh in range(2):
        m = (lane < HEAD_DIM) if hh == 0 else (lane >= HEAD_DIM)
        qm = jnp.where(m, q2, jnp.zeros_like(q2))
        o = _softmax_pv([(_nt(qm, kw) + bias_ref[hh], vw), (_nt(qm, kc), vc)])
        out = jnp.where(m, o, out)
    o_ref[...] = out.astype(BF16)


def _na_attention(rw, qkv, rpb):
    B, T, C = rw.B, rw.T, rw.C
    tq = NA_QROWS * GRID_W
    tk = tq // 2
    n_rb = T // tq
    nkb = T // tk
    assert T % tq == 0 and n_rb >= 2 and (B * T) % C == 0
    bias = _na_bias(rpb, n_rb)
    ctxrow = (B * T) // C

    def kvspec(j, col):
        return pl.BlockSpec((tk, LANES), lambda p, rb, b: (b * nkb + jnp.clip(2 * rb - 1 + j, 0, nkb - 1), col + p))

    return pl.pallas_call(
        _na_kernel,
        grid=(4, n_rb, B),
        in_specs=[pl.BlockSpec((tq, LANES), lambda p, rb, b: (b * n_rb + rb, p))]
        + [kvspec(j, 4) for j in range(4)] + [kvspec(j, 8) for j in range(4)]
        + [pl.BlockSpec((C, LANES), lambda p, rb, b: (ctxrow + b, 4 + p)),
           pl.BlockSpec((C, LANES), lambda p, rb, b: (ctxrow + b, 8 + p)),
           pl.BlockSpec((2, None, tq, 2 * tq),
                        lambda p, rb, b: (p, jnp.where(rb == 0, 0, jnp.where(rb == n_rb - 1, 2, 1)), 0, 0))],
        out_specs=pl.BlockSpec((tq, LANES), lambda p, rb, b: (b * n_rb + rb, p)),
        out_shape=jax.ShapeDtypeStruct((B * T, 4 * LANES), BF16),
        compiler_params=_cparams(3),
        name="neighbourhood_attention",
    )(qkv, *([qkv] * 10), bias)


def _wa_kernel(nb, sink_ref, q_ref, kp, kc_, kn, vp, vc_, vn, kx_ref, vx_ref, o_ref):
    n = pl.program_id(1)
    blk = q_ref.shape[0]
    lane = lax.broadcasted_iota(jnp.int32, (blk, LANES), 1)
    zero = jnp.zeros((blk, LANES), BF16)
    qi = lax.broadcasted_iota(jnp.int32, (blk, 3 * blk), 0)
    ks = lax.broadcasted_iota(jnp.int32, (blk, 3 * blk), 1)
    lo = jnp.where(n > 0, 0, blk)
    hi = jnp.where(n < nb - 1, 3 * blk, 2 * blk)
    valid = (ks >= qi) & (ks <= qi + 2 * blk) & (ks >= lo) & (ks < hi)
    valid4 = jnp.concatenate([valid] * 4, axis=0)
    for kv in range(2):
        parts = []
        for pr in range(2):
            c0 = kv * 2 * LANES + pr * LANES
            qp = q_ref[:, c0:c0 + LANES]
            parts += [jnp.where(lane < HEAD_DIM, qp, zero), jnp.where(lane >= HEAD_DIM, qp, zero)]
        qs = jnp.concatenate(parts, axis=0)
        cs = slice(kv * LANES, (kv + 1) * LANES)
        kb = jnp.concatenate([kp[:, cs], kc_[:, cs], kn[:, cs]], axis=0)
        vb = jnp.concatenate([vp[:, cs], vc_[:, cs], vn[:, cs]], axis=0)
        s = jnp.where(valid4, _nt(qs, kb), NEG_INF)
        sink = jnp.concatenate([jnp.full((blk, 1), sink_ref[kv * 4 + g], F32) for g in range(4)], axis=0)
        o = _softmax_pv([(s, vb), (_nt(qs, kx_ref[:, cs]), vx_ref[:, cs])], extra=sink)
        o_ref[:, kv * 2 * LANES:kv * 2 * LANES + LANES] = jnp.where(lane < HEAD_DIM, o[0:blk], o[blk:2 * blk]).astype(BF16)
        o_ref[:, kv * 2 * LANES + LANES:(kv + 1) * 2 * LANES] = jnp.where(
            lane < HEAD_DIM, o[2 * blk:3 * blk], o[3 * blk:4 * blk]).astype(BF16)


def _wa_attention(rw, qkv, sink):
    B, T, C = rw.B, rw.T, rw.C
    blk = WA_BLOCK
    nb = T // blk
    ctxrow = (B * T) // C

    def kvspec(j, col):
        return pl.BlockSpec((blk, 2 * LANES), lambda b, n: (b * nb + jnp.clip(n + j, 0, nb - 1), col))

    return pl.pallas_call(
        functools.partial(_wa_kernel, nb),
        grid=(B, nb),
        in_specs=[pl.BlockSpec(memory_space=pltpu.SMEM),
                  pl.BlockSpec((blk, 4 * LANES), lambda b, n: (b * nb + n, 3))]
        + [kvspec(j, 8) for j in (-1, 0, 1)] + [kvspec(j, 9) for j in (-1, 0, 1)]
        + [pl.BlockSpec((C, 2 * LANES), lambda b, n: (ctxrow + b, 8)),
           pl.BlockSpec((C, 2 * LANES), lambda b, n: (ctxrow + b, 9))],
        out_specs=pl.BlockSpec((blk, 4 * LANES), lambda b, n: (b * nb + n, 0)),
        out_shape=jax.ShapeDtypeStruct((B * T, 4 * LANES), BF16),
        compiler_params=_cparams(2),
        name="window_attention",
    )(sink.astype(F32), qkv, *([qkv] * 8))


def _ctx_attn_kernel(sink_ref, t_ref, o_ref):
    C = t_ref.shape[0]
    lane = lax.broadcasted_iota(jnp.int32, (C, LANES), 1)
    zero = jnp.zeros((C, LANES), BF16)

    def pair(q2, k2, v2, sinks):
        out = jnp.zeros((C, LANES), F32)
        for hh in range(2):
            m = (lane < HEAD_DIM) if hh == 0 else (lane >= HEAD_DIM)
            extra = None if sinks is None else jnp.full((C, 1), sinks[hh], F32)
            o = _softmax_pv([(_nt(jnp.where(m, q2, zero), k2), v2)], extra=extra)
            out = jnp.where(m, o, out)
        return out.astype(BF16)

    for p in range(4):
        c = p * LANES
        o_ref[:, c:c + LANES] = pair(t_ref[:, c:c + LANES], t_ref[:, 512 + c:640 + c], t_ref[:, 1024 + c:1152 + c], None)
    for kv in range(2):
        kd = t_ref[:, 2048 + kv * LANES:2176 + kv * LANES]
        vd = t_ref[:, 2304 + kv * LANES:2432 + kv * LANES]
        for pr in range(2):
            c = kv * 256 + pr * LANES
            h0 = kv * 4 + pr * 2
            o_ref[:, 512 + c:640 + c] = pair(t_ref[:, 1536 + c:1664 + c], kd, vd, (sink_ref[h0], sink_ref[h0 + 1]))


def _ctx_attention(rw, qkv, sink):
    B, T, C = rw.B, rw.T, rw.C
    ctxrow = (B * T) // C
    return pl.pallas_call(
        _ctx_attn_kernel,
        grid=(B,),
        in_specs=[pl.BlockSpec(memory_space=pltpu.SMEM),
                  pl.BlockSpec((C, qkv.shape[1]), lambda b: (ctxrow + b, 0))],
        out_specs=pl.BlockSpec((C, 8 * LANES), lambda b: (b, 0)),
        out_shape=jax.ShapeDtypeStruct((B * C, 8 * LANES), BF16),
        compiler_params=_cparams(1),
        name="context_attention",
    )(sink.astype(F32), qkv)


S5_Q = 16
CONV_TILE = 256
CONV_HALO = 16


def _ssm_inproj_kernel(x_ref, g_ref, sh_ref, sc_ref, w_ref, z_ref, xbc_ref, u_ref, dt_ref, h_scr):
    h_scr[...] = _ada_norm(x_ref[...], g_ref[...], sh_ref[...], sc_ref[...]).astype(BF16)

    def mm(c0, n):
        return jnp.dot(h_scr[...], w_ref[:, c0:c0 + n], preferred_element_type=F32)

    for c in range(4):
        z_ref[:, c * 256:(c + 1) * 256] = mm(c * 256, 256).astype(BF16)
    for c in range(6):
        xbc_ref[:, c * 256:(c + 1) * 256] = mm(1024 + c * 256, 256).astype(BF16)
    for c in range(2):
        u_ref[:, c * 256:(c + 1) * 256] = mm(2560 + c * 256, 256).astype(BF16)
    dt_ref[...] = mm(3072, LANES)


def _ssm_inproj(rw, xall, mods, norm_g, w_in):
    D, tm = rw.D, rw.tm
    w = jnp.concatenate([w_in[:, 0:2560], w_in[:, 2592:3104], w_in[:, 2560:2592], jnp.zeros((D, LANES - 32), F32)],
                        axis=1).astype(BF16)
    row = lambda i: (i, 0)
    return pl.pallas_call(
        _ssm_inproj_kernel,
        grid=(rw.ntot,),
        in_specs=[pl.BlockSpec((tm, D), row), pl.BlockSpec((1, D), lambda i: (0, 0)),
                  _mod_spec(rw, 0), _mod_spec(rw, 1), pl.BlockSpec((D, 3200), lambda i: (0, 0))],
        out_specs=[pl.BlockSpec((tm, 1024), row), pl.BlockSpec((tm, 1536), row), pl.BlockSpec((tm, 512), row),
                   pl.BlockSpec((tm, LANES), row)],
        out_shape=[jax.ShapeDtypeStruct((rw.rows, 1024), BF16), jax.ShapeDtypeStruct((rw.rows, 1536), BF16),
                   jax.ShapeDtypeStruct((rw.rows, 512), BF16), jax.ShapeDtypeStruct((rw.rows, LANES), F32)],
        scratch_shapes=[pltpu.VMEM((tm, D), BF16)],
        compiler_params=_cparams(1),
        name="ssm_inproj",
    )(xall, norm_g[None, :], mods, mods, w)


def _softplus(x):
    return jnp.maximum(x, 0.0) + jnp.log(1.0 + jnp.exp(-jnp.abs(x)))


def _conv_kernel(lat_tiles, tpb, cpb, x_ref, pv_ref, nx_ref, w_ref, b_ref, dtr_ref, dtb_ref, act_ref, dt_ref):
    i = pl.program_id(0)
    is_lat = i < lat_tiles
    pos = jnp.where(is_lat, i % tpb, (i - lat_tiles) % cpb)
    last_pos = jnp.where(is_lat, tpb - 1, cpb - 1)
    x = x_ref[...].astype(F32)
    tc = x.shape[0]
    prev_row = jnp.where(pos == 0, 0.0, pv_ref[...].astype(F32)[CONV_HALO - 1:CONV_HALO, :])
    next_row = jnp.where(pos == last_pos, 0.0, nx_ref[...].astype(F32)[0:1, :])
    row = lax.broadcasted_iota(jnp.int32, x.shape, 0)
    xm1 = jnp.where(row == 0, prev_row, pltpu.roll(x, 1, 0))
    xp1 = jnp.where(row == tc - 1, next_row, pltpu.roll(x, tc - 1, 0))
    y = w_ref[0:1, :] * xm1 + w_ref[1:2, :] * x + w_ref[2:3, :] * xp1 + b_ref[...]
    act_ref[...] = _silu(y).astype(BF16)
    sp = _softplus(dtr_ref[...] + dtb_ref[...])
    dt_ref[0] = sp
    dt_ref[1] = pltpu.roll(sp, LANES - 16, 1)


def _ssm_conv(rw, xbc, dtr, conv_w, conv_b, dt_bias):
    B, T, C = rw.B, rw.T, rw.C
    tc = CONV_TILE
    assert T % tc == 0 and C % tc == 0
    lat_tiles, tpb, cpb = (B * T) // tc, T // tc, C // tc
    ntiles = rw.rows // tc
    hpt = tc // CONV_HALO
    nhalo = rw.rows // CONV_HALO
    W = xbc.shape[1]
    dtb = jnp.concatenate([dt_bias.reshape(-1), jnp.zeros((LANES - 32,), F32)])[None, :]
    row = lambda i: (i, 0)
    return pl.pallas_call(
        functools.partial(_conv_kernel, lat_tiles, tpb, cpb),
        grid=(ntiles,),
        in_specs=[pl.BlockSpec((tc, W), row),
                  pl.BlockSpec((CONV_HALO, W), lambda i: (jnp.maximum(i * hpt - 1, 0), 0)),
                  pl.BlockSpec((CONV_HALO, W), lambda i: (jnp.minimum((i + 1) * hpt, nhalo - 1), 0)),
                  pl.BlockSpec((3, W), lambda i: (0, 0)), pl.BlockSpec((1, W), lambda i: (0, 0)),
                  pl.BlockSpec((tc, LANES), row), pl.BlockSpec((1, LANES), lambda i: (0, 0))],
        out_specs=[pl.BlockSpec((tc, W), row), pl.BlockSpec((2, tc, LANES), lambda i: (0, i, 0))],
        out_shape=[jax.ShapeDtypeStruct((rw.rows, W), BF16), jax.ShapeDtypeStruct((2, rw.rows, LANES), F32)],
        compiler_params=_cparams(1),
        name="ssm_conv",
    )(xbc, xbc, xbc, conv_w, conv_b[None, :], dtr, dtb)


def _ssd_kernel(actf_ref, actb_ref, dtf_ref, dtb_ref, tri_ref, a_ref, yf_ref, yb_ref, hst):
    @pl.when(pl.program_id(1) == 0)
    def _():
        hst[...] = jnp.zeros_like(hst)

    _ssd_chunk(actf_ref, dtf_ref, tri_ref[0], a_ref[0], yf_ref, hst.at[0])
    _ssd_chunk(actb_ref, dtb_ref, tri_ref[1], a_ref[1], yb_ref, hst.at[1])


def _ssd_chunk(act_ref, dt_ref, tri, avec, y_ref, hst):
    q = SSD_CHUNK
    dt = dt_ref[...]
    da = dt * avec
    acs = jnp.dot(tri, da, preferred_element_type=F32, precision=lax.Precision.HIGHEST)
    tot = jnp.sum(da, axis=0, keepdims=True)
    acs_t = acs.T
    dt_t = dt.T
    eacs = jnp.exp(acs)
    wend = jnp.exp(tot - acs) * dt
    etot = jnp.exp(tot)
    mask = tri > 0.5
    left = lax.broadcasted_iota(jnp.int32, (q, LANES), 1) < HEAD_DIM
    left1 = lax.broadcasted_iota(jnp.int32, (1, LANES), 1) < HEAD_DIM
    for g in range(2):
        bg = act_ref[:, 1024 + g * 128:1152 + g * 128]
        cg = act_ref[:, 1280 + g * 128:1408 + g * 128]
        cb = _nt(cg, bg)
        hin = hst[:, g * 512:(g + 1) * 512]
        yoff = jnp.dot(cg, hin.astype(BF16), preferred_element_type=F32)
        xw, dec = [], []
        for pr in range(4):
            h_a = g * 8 + pr * 2
            h_b = h_a + 1
            c0 = h_a * HEAD_DIM
            x2 = act_ref[:, c0:c0 + LANES]
            outs = []
            for h in (h_a, h_b):
                seg = acs[:, h:h + 1] - acs_t[h:h + 1, :]
                w = cb * jnp.exp(jnp.where(mask, seg, NEG_INF)) * dt_t[h:h + 1, :]
                outs.append(jnp.dot(w.astype(BF16), x2, preferred_element_type=F32))
            yd = jnp.where(left, outs[0], outs[1])
            sc = jnp.where(left, eacs[:, h_a:h_a + 1], eacs[:, h_b:h_b + 1])
            y_ref[:, c0:c0 + LANES] = (yd + yoff[:, pr * LANES:(pr + 1) * LANES] * sc).astype(BF16)
            wsc = jnp.where(left, wend[:, h_a:h_a + 1], wend[:, h_b:h_b + 1])
            xw.append((x2.astype(F32) * wsc).astype(BF16))
            dec.append(jnp.where(left1, etot[:, h_a:h_a + 1], etot[:, h_b:h_b + 1]))
        bg_t = bg.astype(F32).T.astype(BF16)
        snew = jnp.dot(bg_t, jnp.concatenate(xw, axis=1), preferred_element_type=F32)
        hst[:, g * 512:(g + 1) * 512] = hin * jnp.concatenate(dec, axis=1) + snew


def _ssd(rw, act, dt2, a_log):
    B, T, C = rw.B, rw.T, rw.C
    q = SSD_CHUNK
    nct, nlt = C // q, T // q
    ctx0 = (B * T) // q
    r = np.arange(q)
    tri = jnp.asarray(np.stack([r[None, :] <= r[:, None], r[None, :] >= r[:, None]]).astype(np.float32))
    avec = jnp.concatenate([-jnp.exp(a_log.astype(F32)), jnp.zeros((2, LANES - a_log.shape[1]), F32)], axis=1)[:, None, :]

    def blk(d, b, s):
        kc = s if d == 0 else nct - 1 - s
        kl = s - nct if d == 0 else nlt - 1 - (s - nct)
        return jnp.where(s < nct, ctx0 + b * nct + kc, b * nlt + kl)

    aspec = lambda d: pl.BlockSpec((q, act.shape[1]), lambda b, s: (blk(d, b, s), 0))
    dspec = lambda d: pl.BlockSpec((None, q, LANES), lambda b, s: (d, blk(d, b, s), 0))
    yspec = lambda d: pl.BlockSpec((q, 1024), lambda b, s: (blk(d, b, s), 0))
    return pl.pallas_call(
        _ssd_kernel,
        grid=(B, nct + nlt),
        in_specs=[aspec(0), aspec(1), dspec(0), dspec(1),
                  pl.BlockSpec((2, q, q), lambda b, s: (0, 0, 0)),
                  pl.BlockSpec((2, 1, LANES), lambda b, s: (0, 0, 0))],
        out_specs=[yspec(0), yspec(1)],
        out_shape=[jax.ShapeDtypeStruct((rw.rows, 1024), BF16)] * 2,
        scratch_shapes=[pltpu.VMEM((2, q, 1024), F32)],
        compiler_params=_cparams(2),
        name="ssd_scan",
    )(act, act, dt2, dt2, tri, avec)


def _cmul(ar, ai, br, bi):
    return ar * br - ai * bi, ar * bi + ai * br


def _s5_weight_kernel(lre_ref, lim_ref, ls_ref, bre_ref, bim_ref, cre_ref, cim_ref,
                      wsr_ref, wsi_ref, wor_ref, woi_ref, kt_ref, are_ref, aim_ref):
    lre, lim = lre_ref[...], lim_ref[...]
    step = jnp.exp(ls_ref[...])
    er, ei = lre * step, lim * step
    npow = 24
    p = lax.broadcasted_iota(jnp.int32, (1, npow, 1), 1).astype(F32)
    mag = jnp.exp(p * er)
    pre, pim = mag * jnp.cos(p * ei), mag * jnp.sin(p * ei)
    a_re, a_im = pre[:, 1:2, :], pim[:, 1:2, :]
    den = lre * lre + lim * lim
    q_re = ((a_re - 1.0) * lre + a_im * lim) / den
    q_im = (a_im * lre - (a_re - 1.0) * lim) / den
    bb_re, bb_im = _cmul(q_re, q_im, bre_ref[...], bim_ref[...])
    c_re, c_im = cre_ref[...], cim_ref[...]
    ws_r, ws_i, wo_r, wo_i, ca_r, ca_i = [], [], [], [], [], []
    for t in range(S5_Q):
        r, i = _cmul(bb_re, bb_im, pre[:, t:t + 1, :], pim[:, t:t + 1, :])
        ws_r.append(r)
        ws_i.append(i)
        r, i = _cmul(c_re, c_im, pre[:, t:t + 1, :], pim[:, t:t + 1, :])
        ca_r.append(r)
        ca_i.append(i)
        r, i = _cmul(c_re, c_im, pre[:, t + 1:t + 2, :], pim[:, t + 1:t + 2, :])
        wo_r.append(r)
        wo_i.append(-i)
    cat = lambda xs: jnp.concatenate(xs, axis=1)
    wsr_ref[...] = cat(ws_r)
    wsi_ref[...] = cat(ws_i)
    wor_ref[...] = cat(wo_r)
    woi_ref[...] = cat(wo_i)
    bdot = lambda a, b: lax.dot_general(a, b, (((2,), (2,)), ((0,), (0,))), preferred_element_type=F32,
                                        precision=lax.Precision.HIGHEST)
    kt_ref[...] = bdot(cat(ca_r), bb_re) - bdot(cat(ca_i), bb_im)
    are_ref[...] = pre[:, S5_Q:S5_Q + 1, :]
    aim_ref[...] = pim[:, S5_Q:S5_Q + 1, :]


def _s5_weights(lam_re, lam_im, log_step, b_re, b_im, c_re, c_im):
    nd, ng, ns = lam_re.shape
    G = nd * ng
    ch = S5_GROUP
    gb = 8
    qc = S5_Q * ch
    f = lambda a: a.astype(F32)
    args = (f(lam_re).reshape(G, 1, ns), f(lam_im).reshape(G, 1, ns), f(log_step).reshape(G, 1, 1),
            f(b_re).reshape(G, ns, ch).transpose(0, 2, 1), f(b_im).reshape(G, ns, ch).transpose(0, 2, 1),
            f(c_re).reshape(G, ch, ns), f(c_im).reshape(G, ch, ns))
    spec = lambda a: pl.BlockSpec((gb,) + a.shape[1:], lambda i: (i, 0, 0))
    oshape = [jax.ShapeDtypeStruct((G, qc, ns), F32)] * 4 + [jax.ShapeDtypeStruct((G, qc, ch), F32)] \
        + [jax.ShapeDtypeStruct((G, 1, ns), F32)] * 2
    wsr, wsi, wor, woi, kt, a_re, a_im = pl.pallas_call(
        _s5_weight_kernel,
        grid=(G // gb,),
        in_specs=[spec(a) for a in args],
        out_specs=[pl.BlockSpec((gb,) + s.shape[1:], lambda i: (i, 0, 0)) for s in oshape],
        out_shape=oshape,
        compiler_params=_cparams(1),
        name="s5_weights",
    )(*args)

    def by_dir(w, flip_dir):
        w = w.reshape(nd, ng, S5_Q, ch, ns)
        w = jnp.stack([jnp.flip(w[d], axis=1) if d == flip_dir else w[d] for d in range(nd)])
        return w.reshape(nd, ng, qc, ns)

    def pack(w):
        z = jnp.zeros_like(w)
        even = (np.arange(ng) % 2 == 0)[None, :, None, None]
        return jnp.where(even, jnp.concatenate([w, z], axis=-1), jnp.concatenate([z, w], axis=-1)).astype(BF16)

    ws_r, ws_i = pack(by_dir(wsr, 0)), pack(by_dir(wsi, 0))
    wo_r, wo_i = pack(by_dir(wor, 1)), pack(by_dir(woi, 1))
    k = kt.astype(BF16).reshape(nd, ng, S5_Q, ch, ch).transpose(0, 1, 4, 2, 3)
    kf = k.reshape(nd, ng, ch, qc)
    kb = jnp.flip(k, axis=3).reshape(nd, ng, ch, qc)
    rows_f, rows_b = [], []
    for j in range(S5_Q):
        z_f = jnp.zeros((ng, ch, j * ch), BF16)
        z_b = jnp.zeros((ng, ch, (S5_Q - 1 - j) * ch), BF16)
        rows_f.append(jnp.concatenate([z_f, kf[0, :, :, :(S5_Q - j) * ch]], axis=-1))
        rows_b.append(jnp.concatenate([kb[1, :, :, (S5_Q - 1 - j) * ch:], z_b], axis=-1))
    bt = jnp.stack([jnp.stack(rows_f, axis=1), jnp.stack(rows_b, axis=1)]).reshape(nd, ng, qc, qc)
    pair = lambda a: a.reshape(nd, ng // 2, 1, 2 * ns)
    return bt, ws_r, ws_i, wo_r, wo_i, pair(a_re), pair(a_im)


def _s5_kernel(B, nct, nlt, u_ref, bt_ref, wsr_ref, wsi_ref, wor_ref, woi_ref, are_ref, aim_ref, y_ref, s_re, s_im):
    gb = u_ref.shape[0]
    npair = gb // 2
    for d in range(2):
        for pr in range(npair):
            for dst, w_ref in ((s_re, wsr_ref), (s_im, wsi_ref)):
                dst[d, pr] = (jnp.dot(u_ref[2 * pr], w_ref[d, 2 * pr], preferred_element_type=F32)
                              + jnp.dot(u_ref[2 * pr + 1], w_ref[d, 2 * pr + 1], preferred_element_type=F32))
    chains = [(d, pr) for d in range(2) for pr in range(npair)]
    coef = [(are_ref[d, pr], aim_ref[d, pr]) for d, pr in chains]

    def body(s, carry):
        c_bwd = jnp.where(s < nct, nct - 1 - s, 2 * nct + nlt - 1 - s)
        new = []
        for (d, pr), (ar, ai), (hr, hi) in zip(chains, coef, carry):
            rows = pl.ds((s if d == 0 else c_bwd) * B, B)
            sr = s_re[d, pr, rows, :]
            si = s_im[d, pr, rows, :]
            s_re[d, pr, rows, :] = hr
            s_im[d, pr, rows, :] = hi
            new.append((ar * hr - ai * hi + sr, ar * hi + ai * hr + si))
        return tuple(new)

    zero = jnp.zeros((B, LANES), F32)
    lax.fori_loop(0, nct + nlt, body, tuple((zero, zero) for _ in chains))
    for g in range(gb):
        acc = None
        for d in range(2):
            t = (jnp.dot(u_ref[g], bt_ref[d, g], preferred_element_type=F32)
                 + _nt(s_re[d, g // 2].astype(BF16), wor_ref[d, g])
                 + _nt(s_im[d, g // 2].astype(BF16), woi_ref[d, g]))
            acc = t if acc is None else acc + t
        y_ref[g] = acc.astype(BF16)


def _s5(rw, u, weights):
    B, T, C = rw.B, rw.T, rw.C
    bt, ws_r, ws_i, wo_r, wo_i, a_re, a_im = weights
    ng = bt.shape[1]
    ch = S5_GROUP
    q = S5_Q
    nct, nlt = C // q, T // q
    nch = nct + nlt
    qc = q * ch
    gb = 4
    ul = u[:B * T].reshape(B, nlt, q, ng, ch)
    uc = u[B * T:].reshape(B, nct, q, ng, ch)
    ug = jnp.concatenate([uc, ul], axis=1).transpose(3, 1, 0, 2, 4).reshape(ng, nch * B, qc)
    wspec = lambda n: pl.BlockSpec((2, gb, qc, n), lambda i: (0, i, 0, 0))
    aspec = pl.BlockSpec((2, gb // 2, 1, LANES), lambda i: (0, i, 0, 0))
    yg = pl.pallas_call(
        functools.partial(_s5_kernel, B, nct, nlt),
        grid=(ng // gb,),
        in_specs=[pl.BlockSpec((gb, nch * B, qc), lambda i: (i, 0, 0)), wspec(qc), wspec(LANES), wspec(LANES),
                  wspec(LANES), wspec(LANES), aspec, aspec],
        out_specs=pl.BlockSpec((gb, nch * B, qc), lambda i: (i, 0, 0)),
        out_shape=jax.ShapeDtypeStruct((ng, nch * B, qc), BF16),
        scratch_shapes=[pltpu.VMEM((2, gb // 2, nch * B, LANES), F32), pltpu.VMEM((2, gb // 2, nch * B, LANES), F32)],
        compiler_params=_cparams(1),
        name="s5_scan",
    )(ug, bt, ws_r, ws_i, wo_r, wo_i, a_re, a_im)
    yl = yg.reshape(ng, nch, B, q, ch)[:, nct:]
    return yl.transpose(2, 1, 3, 0, 4).reshape(B * T, ng * ch)


def _gelu_tanh(x):
    return 0.5 * x * (1.0 + jnp.tanh(math.sqrt(2.0 / math.pi) * (x + 0.044715 * (x * x * x))))


def _ssm_outproj_kernel(y0_ref, y1_ref, xs_ref, z_ref, v_ref, u_ref, dsk_ref, nw_ref, s5d_ref, gw_ref, gb_ref,
                        x_ref, w_ref, g1_ref, gn_ref, sh2_ref, sc2_ref, wr_ref, br_ref, lt_ref,
                        xo_ref, h2_ref, rt_ref, cnt_ref, carry):
    i = pl.program_id(0)
    y = y0_ref[...].astype(F32) + y1_ref[...].astype(F32) + dsk_ref[...] * xs_ref[...].astype(F32)
    y = _rms(y * _silu(z_ref[...].astype(F32))) * nw_ref[...]
    v = _gelu_tanh(v_ref[...].astype(F32) + s5d_ref[...] * u_ref[...].astype(F32))
    v = v * _sigmoid(jnp.dot(v.astype(BF16), gw_ref[...], preferred_element_type=F32) + gb_ref[...])
    mix = jnp.concatenate([y, v], axis=1).astype(BF16)
    yo = jnp.dot(mix, w_ref[...], preferred_element_type=F32)
    _post_mixer(i, x_ref[...], yo, g1_ref[...], gn_ref[...], sh2_ref[...], sc2_ref[...], wr_ref, br_ref, lt_ref,
                xo_ref, h2_ref, rt_ref, cnt_ref, carry)


def _ssm_outproj(rw, ssd_y, act, z, s5_y, u, d_skip, norm_w, s5_d, glu_w, glu_b, xall, w_out, mods, norm_ffn, wr, br):
    D, tm = rw.D, rw.tm
    ntiles = rw.nlat
    post_in, post_out = _post_specs(rw)
    row = lambda i: (i, 0)
    vec = lambda n: pl.BlockSpec((1, n), lambda i: (0, 0))
    dsk = jnp.repeat(d_skip.astype(F32), HEAD_DIM)[None, :]
    return pl.pallas_call(
        _ssm_outproj_kernel,
        grid=(ntiles,),
        in_specs=[pl.BlockSpec((tm, 1024), row), pl.BlockSpec((tm, 1024), row),
                  pl.BlockSpec((tm, 1024), row), pl.BlockSpec((tm, 1024), row), pl.BlockSpec((tm, 512), row),
                  pl.BlockSpec((tm, 512), row), vec(1024), vec(1024), vec(512),
                  pl.BlockSpec((512, 512), lambda i: (0, 0)), vec(512),
                  pl.BlockSpec((tm, D), row), pl.BlockSpec((1536, D), lambda i: (0, 0)), _mod_spec(rw, 2)] + post_in,
        out_specs=post_out,
        out_shape=_post_shapes(ntiles * tm, D),
        scratch_shapes=[pltpu.VMEM((1, LANES), F32)],
        compiler_params=_cparams(1),
        name="ssm_outproj_router",
    )(ssd_y[0], ssd_y[1], act, z, s5_y, u, dsk, norm_w[None, :], s5_d[None, :], glu_w.astype(BF16), glu_b[None, :],
      xall, w_out.astype(BF16), mods, norm_ffn[None, :], mods, mods, wr, br, _lower_tri(tm))


def kernel(x, c, ctx, c_ctx, mod_w, mod_b, norm_mix, norm_ffn, att_w_in, att_w_out, na_q_norm, na_k_norm, na_rel_bias, wa_q_norm, wa_k_norm, wa_sink, ssm_w_in, ssm_w_out, ssd_conv_w, ssd_conv_b, ssd_dt_bias, ssd_a_log, ssd_d, ssd_norm, s5_lambda_re, s5_lambda_im, s5_log_step, s5_b_re, s5_b_im, s5_c_re, s5_c_im, s5_d, s5_glu_w, s5_glu_b, moe_w_group, moe_b_group, moe_w_expert, moe_b_expert, moe_w13, moe_w2):
    B, T, D = x.shape
    C = ctx.shape[1]
    rw = _Rows(B, T, C, D, ROW_TILE)
    xl = x.reshape(B * T, D)
    xc = ctx.reshape(B * C, D)
    cm = jnp.concatenate([c, c_ctx[None, :], jnp.zeros((8 - B - 1, D), F32)], axis=0)
    mods = _modulation(cm, mod_w, mod_b)
    mods = mods.reshape(mods.shape[0], 8, 1, 6 * D)

    m0 = mods[0]
    qkv = _att_inproj(rw, xl, xc, m0, norm_mix[0], att_w_in[0], na_q_norm[0], na_k_norm[0], wa_q_norm[0],
                      wa_k_norm[0])
    na = _na_attention(rw, qkv, na_rel_bias[0])
    wa = _wa_attention(rw, qkv, wa_sink[0])
    cx = _ctx_attention(rw, qkv, wa_sink[0])
    wr, br = _router_weights(moe_w_group[0], moe_b_group[0], moe_w_expert[0], moe_b_expert[0])
    xall, h2, route, counts = _att_outproj(rw, na, wa, cx, xl, xc, att_w_out[0], m0, norm_ffn[0], wr, br)
    y1, y2 = _moe(h2, route, counts, moe_w13[0], moe_w2[0])
    xall = _combine(rw, rw.ntot, xall, y1, y2, route, m0)

    m1 = mods[1]
    z, xbc, u, dtr = _ssm_inproj(rw, xall, m1, norm_mix[1], ssm_w_in[0])
    act, dt2 = _ssm_conv(rw, xbc, dtr, ssd_conv_w[0], ssd_conv_b[0], ssd_dt_bias[0])
    ssd_y = _ssd(rw, act, dt2, ssd_a_log[0])
    s5_w = _s5_weights(s5_lambda_re[0], s5_lambda_im[0], s5_log_step[0], s5_b_re[0], s5_b_im[0], s5_c_re[0],
                       s5_c_im[0])
    s5_y = _s5(rw, u, s5_w)
    wr, br = _router_weights(moe_w_group[1], moe_b_group[1], moe_w_expert[1], moe_b_expert[1])
    xlat, h2, route, counts = _ssm_outproj(rw, ssd_y, act, z, s5_y, u, ssd_d[0], ssd_norm[0], s5_d[0], s5_glu_w[0],
                                           s5_glu_b[0], xall, ssm_w_out[0], m1, norm_ffn[1], wr, br)
    y1, y2 = _moe(h2, route, counts, moe_w13[1], moe_w2[1])
    out = _combine(rw, rw.nlat, xlat, y1, y2, route, m1)
    return out.reshape(B, T, D)
```

```python
import functools
import math

import jax
import jax.numpy as jnp
import numpy as np
from jax import lax
from jax.experimental import pallas as pl
from jax.experimental.pallas import tpu as pltpu

F32 = jnp.float32
BF16 = jnp.bfloat16

EPS = 1e-6
NEG_INF = -1e30
GRID_W = 64
HEAD_DIM = 64
NA_KH = 8
NA_KW = 16
WA_BLOCK = 128
ROPE_BASE = 10000.0
SSD_CHUNK = 128
S5_GROUP = 16
S5_STATE = 64
MOE_GROUPS = 4
MOE_EPG = 8
MOE_EXPERTS = MOE_GROUPS * MOE_EPG

LANES = 128
ROW_TILE = 512
MOE_TILE = 256
VMEM_LIMIT = 56 * 1024 * 1024
MOE_VMEM_LIMIT = 60 * 1024 * 1024


def _cparams(n_axes, vmem=VMEM_LIMIT):
    return pltpu.CompilerParams(dimension_semantics=("arbitrary",) * n_axes, vmem_limit_bytes=vmem)


def _sigmoid(x):
    return 1.0 / (1.0 + jnp.exp(-x))


def _silu(x):
    return x * _sigmoid(x)


def _rms(x, eps=EPS):
    return x * lax.rsqrt(jnp.mean(x * x, axis=-1, keepdims=True) + eps)


def _ada_norm(x, g, shift, scale):
    return (_rms(x) * g) * (1.0 + scale) + shift


def _mod_kernel(c_ref, w_ref, b_ref, o_ref):
    a = _silu(c_ref[...])
    o_ref[...] = jnp.dot(a, w_ref[...], preferred_element_type=F32, precision=lax.Precision.HIGHEST) + b_ref[...]


def _modulation(cm, mod_w, mod_b):
    depth, d, n6 = mod_w.shape
    tn = 1024
    return pl.pallas_call(
        _mod_kernel,
        grid=(depth, n6 // tn),
        in_specs=[pl.BlockSpec((8, d), lambda l, j: (0, 0)),
                  pl.BlockSpec((None, d, tn), lambda l, j: (l, 0, j)),
                  pl.BlockSpec((None, 1, tn), lambda l, j: (l, 0, j))],
        out_specs=pl.BlockSpec((None, 8, tn), lambda l, j: (l, 0, j)),
        out_shape=jax.ShapeDtypeStruct((depth, 8, n6), F32),
        compiler_params=_cparams(2),
        name="modulation",
    )(cm, mod_w, mod_b.reshape(depth, 1, n6))


class _Rows:
    def __init__(self, B, T, C, D, tm):
        assert T % tm == 0 and (B * C) % tm == 0
        self.B, self.T, self.C, self.D, self.tm = B, T, C, D, tm
        self.tpb = T // tm
        self.nlat = B * self.tpb
        self.nctx = (B * C) // tm
        self.ntot = self.nlat + self.nctx
        self.rows = B * (T + C)

    def group(self, i):
        return jnp.where(i < self.nlat, i // self.tpb, self.B)


def _mod_spec(rw, col):
    return pl.BlockSpec((None, 1, rw.D), lambda i, *_: (rw.group(i), 0, col))


def _seg_norm(y, seg, gcol):
    ss = jnp.dot((y * y).astype(BF16), seg, preferred_element_type=F32)
    return y * lax.rsqrt(ss + EPS) * gcol


def _rope(y, cos, sin):
    w = y.shape[-1]
    lane = lax.broadcasted_iota(jnp.int32, y.shape, 1)
    first = (lane % 32) < 16
    partner = jnp.where(first, pltpu.roll(y, w - 16, 1), pltpu.roll(y, 16, 1))
    return y * cos + partner * sin


def _dup_halves(k):
    lane = lax.broadcasted_iota(jnp.int32, k.shape, 1)
    sw = pltpu.roll(k, 64, 1)
    return jnp.where(lane < 64, k, sw), jnp.where(lane < 64, sw, k)


def _att_inproj_kernel(nlat, xl_ref, xc_ref, g_ref, sh_ref, sc_ref, w_ref, gcol_ref, cos_ref, sin_ref, seg_ref,
                       o_ref, h_scr):
    i = pl.program_id(0)
    x = jnp.where(i < nlat, xl_ref[...], xc_ref[...])
    h_scr[...] = _ada_norm(x, g_ref[...], sh_ref[...], sc_ref[...]).astype(BF16)
    seg = seg_ref[...]
    cos2 = jnp.concatenate([cos_ref[...], cos_ref[...]], axis=1)
    sin2 = jnp.concatenate([sin_ref[...], sin_ref[...]], axis=1)
    for c in range(9):
        c0 = c * 256
        y = jnp.dot(h_scr[...], w_ref[:, c0:c0 + 256], preferred_element_type=F32)
        gcol = gcol_ref[:, c0:c0 + 256]
        if c in (0, 1, 2, 3):
            o_ref[:, c0:c0 + 256] = _seg_norm(y, seg, gcol).astype(BF16)
        elif c in (4, 5):
            o_ref[:, c0:c0 + 256] = y.astype(BF16)
        elif c in (6, 7):
            o_ref[:, c0:c0 + 256] = _rope(_seg_norm(y, seg, gcol), cos2, sin2).astype(BF16)
        else:
            lane = lax.broadcasted_iota(jnp.int32, y.shape, 1)
            yk = jnp.where(lane < 128, _seg_norm(y, seg, gcol), y)
            yr = jnp.where(lane < 128, _rope(yk, cos2, sin2), yk)
            k0, k1 = _dup_halves(yr[:, :128])
            v0, v1 = _dup_halves(yr[:, 128:])
            o_ref[:, 2048:2176] = k0.astype(BF16)
            o_ref[:, 2176:2304] = k1.astype(BF16)
            o_ref[:, 2304:2432] = v0.astype(BF16)
            o_ref[:, 2432:2560] = v1.astype(BF16)


def _rope_tables(T, tm):
    t = np.arange(T)
    d = np.arange(HEAD_DIM)
    nf = HEAD_DIM // 4
    inv = jnp.asarray(ROPE_BASE, F32) ** (-jnp.arange(nf, dtype=F32) / nf)
    pos = np.where((d // 32 == 0)[None, :], (t // GRID_W)[:, None], (t % GRID_W)[:, None])
    ang = jnp.asarray(pos, F32) * inv[d % nf][None, :]
    sign = np.where((d % 32) < 16, -1.0, 1.0).astype(np.float32)
    cos = jnp.cos(ang)
    sin = jnp.sin(ang) * sign[None, :]
    cos = jnp.concatenate([cos, jnp.ones((tm, HEAD_DIM), F32)], axis=0)
    sin = jnp.concatenate([sin, jnp.zeros((tm, HEAD_DIM), F32)], axis=0)
    return jnp.tile(cos, (1, 2)), jnp.tile(sin, (1, 2))


def _att_inproj(rw, xl, xc, mods, norm_g, w_in, na_qn, na_kn, wa_qn, wa_kn):
    D, tm = rw.D, rw.tm
    scale = HEAD_DIM ** -0.5
    gcol = jnp.concatenate([jnp.tile(na_qn * scale, 8), jnp.tile(na_kn, 8), jnp.ones((512,), F32),
                            jnp.tile(wa_qn * scale, 8), jnp.tile(wa_kn, 2), jnp.ones((128,), F32)])[None, :]
    cos, sin = _rope_tables(rw.T, tm)
    segn = np.arange(256) // 64
    seg = jnp.asarray((segn[:, None] == segn[None, :]).astype(np.float32) / 64.0, BF16)
    nlat, tpb = rw.nlat, rw.tpb
    return pl.pallas_call(
        functools.partial(_att_inproj_kernel, nlat),
        grid=(rw.ntot,),
        in_specs=[pl.BlockSpec((tm, D), lambda i: (jnp.minimum(i, nlat - 1), 0)),
                  pl.BlockSpec((tm, D), lambda i: (jnp.maximum(i - nlat, 0), 0)),
                  pl.BlockSpec((1, D), lambda i: (0, 0)),
                  _mod_spec(rw, 0), _mod_spec(rw, 1),
                  pl.BlockSpec((D, 2304), lambda i: (0, 0)),
                  pl.BlockSpec((1, 2304), lambda i: (0, 0)),
                  pl.BlockSpec((tm, 128), lambda i: (jnp.where(i < nlat, i % tpb, tpb), 0)),
                  pl.BlockSpec((tm, 128), lambda i: (jnp.where(i < nlat, i % tpb, tpb), 0)),
                  pl.BlockSpec((256, 256), lambda i: (0, 0))],
        out_specs=pl.BlockSpec((tm, 2560), lambda i: (i, 0)),
        out_shape=jax.ShapeDtypeStruct((rw.rows, 2560), BF16),
        scratch_shapes=[pltpu.VMEM((tm, D), BF16)],
        compiler_params=_cparams(1),
        name="att_inproj",
    )(xl, xc, norm_g[None, :], mods, mods, w_in.astype(BF16), gcol, cos, sin, seg)


def _route(lg, lt, carry):
    lane = lax.broadcasted_iota(jnp.int32, lg.shape, 1).astype(F32)
    gm = lane < MOE_GROUPS
    mg = jnp.max(jnp.where(gm, lg, NEG_INF), axis=-1, keepdims=True)
    eg = jnp.where(gm, jnp.exp(jnp.where(gm, lg, NEG_INF) - mg), 0.0)
    pg = eg / jnp.sum(eg, axis=-1, keepdims=True)
    ptop = jnp.max(pg, axis=-1, keepdims=True)
    gidx = jnp.min(jnp.where(gm & (pg == ptop), lane, 1e9), axis=-1, keepdims=True)
    lo = MOE_GROUPS + MOE_EPG * gidx
    em = (lane >= lo) & (lane < lo + MOE_EPG)
    le = jnp.where(em, lg, NEG_INF)
    ee = jnp.where(em, jnp.exp(le - jnp.max(le, axis=-1, keepdims=True)), 0.0)
    pe = ee / jnp.sum(ee, axis=-1, keepdims=True)
    v1 = jnp.max(jnp.where(em, pe, -1.0), axis=-1, keepdims=True)
    i1 = jnp.min(jnp.where(em & (pe == v1), lane, 1e9), axis=-1, keepdims=True)
    em2 = em & (lane != i1)
    v2 = jnp.max(jnp.where(em2, pe, -1.0), axis=-1, keepdims=True)
    i2 = jnp.min(jnp.where(em2 & (pe == v2), lane, 1e9), axis=-1, keepdims=True)
    den = v1 + v2
    w1 = v1 / den * ptop
    w2 = v2 / den * ptop
    e1 = i1 - MOE_GROUPS
    e2 = i2 - MOE_GROUPS
    m1 = lane == e1
    m2 = lane == e2
    oh = jnp.where(m1 | m2, 1.0, 0.0)
    cnt = jnp.dot(lt, oh.astype(BF16), preferred_element_type=F32) + carry
    r1 = jnp.sum(jnp.where(m1, cnt, 0.0), axis=-1, keepdims=True)
    r2 = jnp.sum(jnp.where(m2, cnt, 0.0), axis=-1, keepdims=True)
    route = jnp.where(lane == 0, e1, jnp.where(lane == 1, e2, jnp.where(lane == 2, w1, jnp.where(
        lane == 3, w2, jnp.where(lane == 4, r1, jnp.where(lane == 5, r2, 0.0))))))
    return route, carry + jnp.sum(oh, axis=0, keepdims=True)


def _post_mixer(i, x, y, g1, gn, sh2, sc2, wr_ref, br_ref, lt_ref, xo_ref, h2_ref, rt_ref, cnt_ref, carry):
    xn = x + g1 * y
    xo_ref[...] = xn
    h2 = _ada_norm(xn, gn, sh2, sc2)
    hb = h2.astype(BF16)
    hbf = hb.astype(F32)
    half = h2.shape[1] // 2
    h2_ref[...] = pltpu.pack_elementwise([h2[:, :half], h2[:, half:]], packed_dtype=BF16)
    hl = (h2 - hbf).astype(BF16)
    lg = (jnp.dot(hb, wr_ref[0], preferred_element_type=F32)
          + (jnp.dot(hb, wr_ref[1], preferred_element_type=F32) + jnp.dot(hl, wr_ref[0], preferred_element_type=F32))
          + br_ref[...])

    @pl.when(i == 0)
    def _():
        carry[...] = jnp.zeros_like(carry)

    route, newc = _route(lg, lt_ref[...], carry[...])
    rt_ref[...] = route
    carry[...] = newc
    cnt_ref[...] = newc


def _att_outproj_kernel(nlat, na_ref, wa_ref, cx_ref, xl_ref, xc_ref, w_ref, g1_ref, gn_ref, sh2_ref, sc2_ref,
                        wr_ref, br_ref, lt_ref, xo_ref, h2_ref, rt_ref, cnt_ref, carry):
    i = pl.program_id(0)
    lat = i < nlat
    mix = jnp.where(lat, jnp.concatenate([na_ref[...], wa_ref[...]], axis=1), cx_ref[...])
    y = jnp.dot(mix, w_ref[...], preferred_element_type=F32)
    x = jnp.where(lat, xl_ref[...], xc_ref[...])
    _post_mixer(i, x, y, g1_ref[...], gn_ref[...], sh2_ref[...], sc2_ref[...], wr_ref, br_ref, lt_ref,
                xo_ref, h2_ref, rt_ref, cnt_ref, carry)


def _router_weights(w_group, b_group, w_expert, b_expert):
    D = w_group.shape[0]
    pad = LANES - MOE_GROUPS - MOE_EXPERTS
    wr = jnp.concatenate([w_group, w_expert, jnp.zeros((D, pad), F32)], axis=1)
    br = jnp.concatenate([b_group, b_expert, jnp.zeros((pad,), F32)])[None, :]
    hi = wr.astype(BF16)
    lo = (wr - hi.astype(F32)).astype(BF16)
    return jnp.stack([hi, lo]), br


def _lower_tri(tm):
    r = np.arange(tm)
    return jnp.asarray((r[None, :] < r[:, None]).astype(np.float32), BF16)


def _post_specs(rw):
    D, tm = rw.D, rw.tm
    return ([pl.BlockSpec((1, D), lambda i: (0, 0)), _mod_spec(rw, 3), _mod_spec(rw, 4),
             pl.BlockSpec((2, D, LANES), lambda i: (0, 0, 0)), pl.BlockSpec((1, LANES), lambda i: (0, 0)),
             pl.BlockSpec((tm, tm), lambda i: (0, 0))],
            [pl.BlockSpec((tm, D), lambda i: (i, 0)), pl.BlockSpec((tm, D // 2), lambda i: (i, 0)),
             pl.BlockSpec((tm, LANES), lambda i: (i, 0)), pl.BlockSpec((1, LANES), lambda i: (0, 0))])


def _post_shapes(nrows, D):
    return [jax.ShapeDtypeStruct((nrows, D), F32), jax.ShapeDtypeStruct((nrows, D // 2), jnp.uint32),
            jax.ShapeDtypeStruct((nrows, LANES), F32), jax.ShapeDtypeStruct((1, LANES), F32)]


def _att_outproj(rw, na, wa, cx, xl, xc, w_out, mods, norm_ffn, wr, br):
    D, tm, nlat = rw.D, rw.tm, rw.nlat
    post_in, post_out = _post_specs(rw)
    latmap = lambda i: (jnp.minimum(i, nlat - 1), 0)
    ctxmap = lambda i: (jnp.maximum(i - nlat, 0), 0)
    return pl.pallas_call(
        functools.partial(_att_outproj_kernel, nlat),
        grid=(rw.ntot,),
        in_specs=[pl.BlockSpec((tm, 512), latmap), pl.BlockSpec((tm, 512), latmap), pl.BlockSpec((tm, D), ctxmap),
                  pl.BlockSpec((tm, D), latmap), pl.BlockSpec((tm, D), ctxmap),
                  pl.BlockSpec((D, D), lambda i: (0, 0)), _mod_spec(rw, 2)] + post_in,
        out_specs=post_out,
        out_shape=_post_shapes(rw.rows, D),
        scratch_shapes=[pltpu.VMEM((1, LANES), F32)],
        compiler_params=_cparams(1),
        name="att_outproj_router",
    )(na, wa, cx, xl, xc, w_out.astype(BF16), mods, norm_ffn[None, :], mods, mods, wr, br, _lower_tri(tm))


def _moe_kernel(te_ref, nu_ref, src_ref, hp_ref, w13_ref, w2_ref, o_ref, w13b, w2b, xbuf):
    i = pl.program_id(0)
    prev = te_ref[jnp.maximum(i - 1, 0)]
    changed = (i == 0) | (te_ref[i] != prev)

    @pl.when(changed)
    def _():
        w13b[...] = w13_ref[...].astype(BF16)
        w2b[...] = w2_ref[...].astype(BF16)

    @pl.when(i < nu_ref[0])
    def _():
        def fetch(j, carry):
            xbuf[pl.ds(j, 1), :] = hp_ref[pl.ds(src_ref[0, j], 1), :]
            return carry

        lax.fori_loop(0, xbuf.shape[0], fetch, 0, unroll=8)
        ff = w2b.shape[0]
        half = xbuf.shape[1]
        w = xbuf[...]
        unpack = functools.partial(pltpu.unpack_elementwise, packed_dtype=BF16, unpacked_dtype=F32)
        x_lo = unpack(w, index=0).astype(BF16)
        x_hi = unpack(w, index=1).astype(BF16)
        a13 = (jnp.dot(x_lo, w13b[:half, :], preferred_element_type=F32)
               + jnp.dot(x_hi, w13b[half:, :], preferred_element_type=F32))
        act = _silu(a13[:, :ff]) * a13[:, ff:]
        o_ref[...] = jnp.dot(act.astype(BF16), w2b[...], preferred_element_type=F32).astype(BF16)

    @pl.when(i >= nu_ref[0])
    def _():
        o_ref[...] = jnp.zeros_like(o_ref)


def _moe(h2p, route, counts, w13, w2, layer):
    N = h2p.shape[0]
    D = 2 * h2p.shape[1]
    _, E, _, F2 = w13.shape
    tg = MOE_TILE
    nt = (2 * N) // tg + E
    e = route[:, 0:2].astype(jnp.int32)
    rank = route[:, 4:6].astype(jnp.int32)
    cnt = counts[0, :E].astype(jnp.int32)
    ntile_e = (cnt + tg - 1) // tg
    tile_end = jnp.cumsum(ntile_e)
    offs = (tile_end - ntile_e) * tg
    onehot = (e[:, :, None] == jnp.arange(E, dtype=jnp.int32)).astype(jnp.int32)
    dest = jnp.sum(onehot * offs, axis=-1) + rank
    src = jnp.zeros((nt * tg,), jnp.int32).at[dest.reshape(-1)].set(jnp.repeat(jnp.arange(N, dtype=jnp.int32), 2))
    tile_id = jnp.arange(nt, dtype=jnp.int32)
    nu = tile_end[-1:].astype(jnp.int32)
    te = jnp.sum((tile_end[None, :] <= jnp.minimum(tile_id, nu[0] - 1)[:, None]).astype(jnp.int32), axis=1)
    te = jnp.minimum(te, E - 1)
    ys = pl.pallas_call(
        _moe_kernel,
        grid_spec=pltpu.PrefetchScalarGridSpec(
            num_scalar_prefetch=2,
            grid=(nt,),
            in_specs=[pl.BlockSpec((None, 1, tg), lambda i, te, nu: (i, 0, 0), memory_space=pltpu.SMEM),
                      pl.BlockSpec((N, D // 2), lambda i, te, nu: (0, 0), pipeline_mode=pl.Buffered(1)),
                      pl.BlockSpec((None, None, D, F2), lambda i, te, nu: (layer, te[i], 0, 0)),
                      pl.BlockSpec((None, None, F2 // 2, D), lambda i, te, nu: (layer, te[i], 0, 0))],
            out_specs=pl.BlockSpec((tg, D), lambda i, te, nu: (i, 0)),
            scratch_shapes=[pltpu.VMEM((D, F2), BF16), pltpu.VMEM((F2 // 2, D), BF16),
                            pltpu.VMEM((tg, D // 2), jnp.uint32)]),
        out_shape=jax.ShapeDtypeStruct((nt * tg, D), BF16),
        compiler_params=_cparams(1, vmem=MOE_VMEM_LIMIT),
        name="moe_experts",
    )(te, nu, src.reshape(nt, 1, tg), h2p, w13, w2)
    pick = lambda k: ys.at[dest[:, k]].get(mode="promise_in_bounds")
    return pick(0), pick(1)


def _combine_kernel(x_ref, y1_ref, y2_ref, rt_ref, g2_ref, o_ref):
    rt = rt_ref[...]
    f = rt[:, 2:3] * y1_ref[...].astype(F32) + rt[:, 3:4] * y2_ref[...].astype(F32)
    o_ref[...] = x_ref[...] + g2_ref[...] * f


def _combine(rw, ntiles, xall, y1, y2, route, mods):
    D, tm = rw.D, rw.tm
    row = lambda i: (i, 0)
    return pl.pallas_call(
        _combine_kernel,
        grid=(ntiles,),
        in_specs=[pl.BlockSpec((tm, D), row), pl.BlockSpec((tm, D), row), pl.BlockSpec((tm, D), row),
                  pl.BlockSpec((tm, LANES), row), _mod_spec(rw, 5)],
        out_specs=pl.BlockSpec((tm, D), row),
        out_shape=jax.ShapeDtypeStruct((ntiles * tm, D), F32),
        compiler_params=_cparams(1),
        name="moe_combine",
    )(xall, y1, y2, route, mods)


NA_QROWS = 8
NA_KROWS = 16


def _na_bias(rpb, n_rb):
    H = rpb.shape[0]
    i = np.arange(GRID_W)
    c0 = np.clip(i - NA_KW // 2, 0, GRID_W - NA_KW)
    j = np.arange(GRID_W)
    colvalid = (j[None, :] >= c0[:, None]) & (j[None, :] < c0[:, None] + NA_KW)
    dc = np.clip(j[None, :] - i[:, None] + NA_KW - 1, 0, 2 * NA_KW - 2)
    onehot = ((dc[None] == np.arange(2 * NA_KW - 1)[:, None, None]) & colvalid[None]).astype(np.float32)
    tiles = jnp.einsum('hrc,cij->hrij', rpb.astype(F32), jnp.asarray(onehot), precision=lax.Precision.HIGHEST)
    tiles = tiles + jnp.asarray(np.where(colvalid, 0.0, NEG_INF).astype(np.float32))
    flat = tiles.transpose(0, 2, 1, 3).reshape(H, GRID_W, (2 * NA_KH - 1) * GRID_W)
    blocks = []
    for variant in range(3):
        for a in range(NA_QROWS):
            start = (max(a - 4, 0) + 4, a, min(a, 4))[variant]
            dr0 = start - a + 3
            neg = lambda n: jnp.full((H, GRID_W, n * GRID_W), NEG_INF, F32)
            blocks.append(jnp.concatenate([neg(start), flat[:, :, dr0 * GRID_W:(dr0 + NA_KH) * GRID_W],
                                           neg(NA_KROWS - NA_KH - start)], axis=-1))
    return jnp.stack(blocks, axis=1).reshape(H, 3, NA_QROWS * GRID_W, NA_KROWS * GRID_W)


def _softmax_pv(parts, extra=None):
    mx = functools.reduce(jnp.maximum, [jnp.max(s, axis=-1, keepdims=True) for s, _ in parts])
    if extra is not None:
        mx = jnp.maximum(mx, extra)
    l = jnp.zeros_like(mx) if extra is None else jnp.exp(extra - mx)
    o = None
    for s, v in parts:
        p = jnp.exp(s - mx)
        l = l + jnp.sum(p, axis=-1, keepdims=True)
        pv = jnp.dot(p.astype(BF16), v, preferred_element_type=F32)
        o = pv if o is None else o + pv
    return o / l


def _nt(a, b):
    return lax.dot_general(a, b, (((1,), (1,)), ((), ())), preferred_element_type=F32)


def _na_kernel(q_ref, k0, k1, k2, k3, v0, v1, v2, v3, kc_ref, vc_ref, bias_ref, o_ref):
    q2 = q_ref[...]
    kw = jnp.concatenate([k0[...], k1[...], k2[...], k3[...]], axis=0)
    vw = jnp.concatenate([v0[...], v1[...], v2[...], v3[...]], axis=0)
    kc = kc_ref[...]
    vc = vc_ref[...]
    lane = lax.broadcasted_iota(jnp.int32, q2.shape, 1)
    out = jnp.zeros(q2.shape, F32)
    for hh in range(2):
        m = (lane < HEAD_DIM) if hh == 0 else (lane >= HEAD_DIM)
        qm = jnp.where(m, q2, jnp.zeros_like(q2))
        o = _softmax_pv([(_nt(qm, kw) + bias_ref[hh], vw), (_nt(qm, kc), vc)])
        out = jnp.where(m, o, out)
    o_ref[...] = out.astype(BF16)


def _na_attention(rw, qkv, rpb):
    B, T, C = rw.B, rw.T, rw.C
    tq = NA_QROWS * GRID_W
    tk = tq // 2
    n_rb = T // tq
    nkb = T // tk
    assert T % tq == 0 and n_rb >= 2 and (B * T) % C == 0
    bias = _na_bias(rpb, n_rb)
    ctxrow = (B * T) // C

    def kvspec(j, col):
        return pl.BlockSpec((tk, LANES), lambda p, rb, b: (b * nkb + jnp.clip(2 * rb - 1 + j, 0, nkb - 1), col + p))

    return pl.pallas_call(
        _na_kernel,
        grid=(4, n_rb, B),
        in_specs=[pl.BlockSpec((tq, LANES), lambda p, rb, b: (b * n_rb + rb, p))]
        + [kvspec(j, 4) for j in range(4)] + [kvspec(j, 8) for j in range(4)]
        + [pl.BlockSpec((C, LANES), lambda p, rb, b: (ctxrow + b, 4 + p)),
           pl.BlockSpec((C, LANES), lambda p, rb, b: (ctxrow + b, 8 + p)),
           pl.BlockSpec((2, None, tq, 2 * tq),
                        lambda p, rb, b: (p, jnp.where(rb == 0, 0, jnp.where(rb == n_rb - 1, 2, 1)), 0, 0))],
        out_specs=pl.BlockSpec((tq, LANES), lambda p, rb, b: (b * n_rb + rb, p)),
        out_shape=jax.ShapeDtypeStruct((B * T, 4 * LANES), BF16),
        compiler_params=_cparams(3),
        name="neighbourhood_attention",
    )(qkv, *([qkv] * 10), bias)


def _wa_kernel(nb, sink_ref, q_ref, kp, kc_, kn, vp, vc_, vn, kx_ref, vx_ref, o_ref):
    n = pl.program_id(1)
    blk = q_ref.shape[0]
    lane = lax.broadcasted_iota(jnp.int32, (blk, LANES), 1)
    zero = jnp.zeros((blk, LANES), BF16)
    qi = lax.broadcasted_iota(jnp.int32, (blk, 3 * blk), 0)
    ks = lax.broadcasted_iota(jnp.int32, (blk, 3 * blk), 1)
    lo = jnp.where(n > 0, 0, blk)
    hi = jnp.where(n < nb - 1, 3 * blk, 2 * blk)
    valid = (ks >= qi) & (ks <= qi + 2 * blk) & (ks >= lo) & (ks < hi)
    valid4 = jnp.concatenate([valid] * 4, axis=0)
    for kv in range(2):
        parts = []
        for pr in range(2):
            c0 = kv * 2 * LANES + pr * LANES
            qp = q_ref[:, c0:c0 + LANES]
            parts += [jnp.where(lane < HEAD_DIM, qp, zero), jnp.where(lane >= HEAD_DIM, qp, zero)]
        qs = jnp.concatenate(parts, axis=0)
        cs = slice(kv * LANES, (kv + 1) * LANES)
        kb = jnp.concatenate([kp[:, cs], kc_[:, cs], kn[:, cs]], axis=0)
        vb = jnp.concatenate([vp[:, cs], vc_[:, cs], vn[:, cs]], axis=0)
        s = jnp.where(valid4, _nt(qs, kb), NEG_INF)
        sink = jnp.concatenate([jnp.full((blk, 1), sink_ref[kv * 4 + g], F32) for g in range(4)], axis=0)
        o = _softmax_pv([(s, vb), (_nt(qs, kx_ref[:, cs]), vx_ref[:, cs])], extra=sink)
        o_ref[:, kv * 2 * LANES:kv * 2 * LANES + LANES] = jnp.where(lane < HEAD_DIM, o[0:blk], o[blk:2 * blk]).astype(BF16)
        o_ref[:, kv * 2 * LANES + LANES:(kv + 1) * 2 * LANES] = jnp.where(
            lane < HEAD_DIM, o[2 * blk:3 * blk], o[3 * blk:4 * blk]).astype(BF16)


def _wa_attention(rw, qkv, sink):
    B, T, C = rw.B, rw.T, rw.C
    blk = WA_BLOCK
    nb = T // blk
    ctxrow = (B * T) // C

    def kvspec(j, col):
        return pl.BlockSpec((blk, 2 * LANES), lambda b, n: (b * nb + jnp.clip(n + j, 0, nb - 1), col))

    return pl.pallas_call(
        functools.partial(_wa_kernel, nb),
        grid=(B, nb),
        in_specs=[pl.BlockSpec(memory_space=pltpu.SMEM),
                  pl.BlockSpec((blk, 4 * LANES), lambda b, n: (b * nb + n, 3))]
        + [kvspec(j, 8) for j in (-1, 0, 1)] + [kvspec(j, 9) for j in (-1, 0, 1)]
        + [pl.BlockSpec((C, 2 * LANES), lambda b, n: (ctxrow + b, 8)),
           pl.BlockSpec((C, 2 * LANES), lambda b, n: (ctxrow + b, 9))],
        out_specs=pl.BlockSpec((blk, 4 * LANES), lambda b, n: (b * nb + n, 0)),
        out_shape=jax.ShapeDtypeStruct((B * T, 4 * LANES), BF16),
        compiler_params=_cparams(2),
        name="window_attention",
    )(sink.astype(F32), qkv, *([qkv] * 8))


def _ctx_attn_kernel(sink_ref, t_ref, o_ref):
    C = t_ref.shape[0]
    lane = lax.broadcasted_iota(jnp.int32, (C, LANES), 1)
    zero = jnp.zeros((C, LANES), BF16)

    def pair(q2, k2, v2, sinks):
        out = jnp.zeros((C, LANES), F32)
        for hh in range(2):
            m = (lane < HEAD_DIM) if hh == 0 else (lane >= HEAD_DIM)
            extra = None if sinks is None else jnp.full((C, 1), sinks[hh], F32)
            o = _softmax_pv([(_nt(jnp.where(m, q2, zero), k2), v2)], extra=extra)
            out = jnp.where(m, o, out)
        return out.astype(BF16)

    for p in range(4):
        c = p * LANES
        o_ref[:, c:c + LANES] = pair(t_ref[:, c:c + LANES], t_ref[:, 512 + c:640 + c], t_ref[:, 1024 + c:1152 + c], None)
    for kv in range(2):
        kd = t_ref[:, 2048 + kv * LANES:2176 + kv * LANES]
        vd = t_ref[:, 2304 + kv * LANES:2432 + kv * LANES]
        for pr in range(2):
            c = kv * 256 + pr * LANES
            h0 = kv * 4 + pr * 2
            o_ref[:, 512 + c:640 + c] = pair(t_ref[:, 1536 + c:1664 + c], kd, vd, (sink_ref[h0], sink_ref[h0 + 1]))


def _ctx_attention(rw, qkv, sink):
    B, T, C = rw.B, rw.T, rw.C
    ctxrow = (B * T) // C
    return pl.pallas_call(
        _ctx_attn_kernel,
        grid=(B,),
        in_specs=[pl.BlockSpec(memory_space=pltpu.SMEM),
                  pl.BlockSpec((C, qkv.shape[1]), lambda b: (ctxrow + b, 0))],
        out_specs=pl.BlockSpec((C, 8 * LANES), lambda b: (b, 0)),
        out_shape=jax.ShapeDtypeStruct((B * C, 8 * LANES), BF16),
        compiler_params=_cparams(1),
        name="context_attention",
    )(sink.astype(F32), qkv)


S5_Q = 16
CONV_TILE = 256
CONV_HALO = 16


def _ssm_inproj_kernel(x_ref, g_ref, sh_ref, sc_ref, w_ref, z_ref, xbc_ref, u_ref, dt_ref, h_scr):
    h_scr[...] = _ada_norm(x_ref[...], g_ref[...], sh_ref[...], sc_ref[...]).astype(BF16)

    def mm(c0, n):
        return jnp.dot(h_scr[...], w_ref[:, c0:c0 + n], preferred_element_type=F32)

    for c in range(4):
        z_ref[:, c * 256:(c + 1) * 256] = mm(c * 256, 256).astype(BF16)
    for c in range(6):
        xbc_ref[:, c * 256:(c + 1) * 256] = mm(1024 + c * 256, 256).astype(BF16)
    for c in range(2):
        u_ref[:, c * 256:(c + 1) * 256] = mm(2560 + c * 256, 256).astype(BF16)
    dt_ref[...] = mm(3072, LANES)


def _ssm_inproj(rw, xall, mods, norm_g, w_in):
    D, tm = rw.D, rw.tm
    w = jnp.concatenate([w_in[:, 0:2560], w_in[:, 2592:3104], w_in[:, 2560:2592], jnp.zeros((D, LANES - 32), F32)],
                        axis=1).astype(BF16)
    row = lambda i: (i, 0)
    return pl.pallas_call(
        _ssm_inproj_kernel,
        grid=(rw.ntot,),
        in_specs=[pl.BlockSpec((tm, D), row), pl.BlockSpec((1, D), lambda i: (0, 0)),
                  _mod_spec(rw, 0), _mod_spec(rw, 1), pl.BlockSpec((D, 3200), lambda i: (0, 0))],
        out_specs=[pl.BlockSpec((tm, 1024), row), pl.BlockSpec((tm, 1536), row), pl.BlockSpec((tm, 512), row),
                   pl.BlockSpec((tm, LANES), row)],
        out_shape=[jax.ShapeDtypeStruct((rw.rows, 1024), BF16), jax.ShapeDtypeStruct((rw.rows, 1536), BF16),
                   jax.ShapeDtypeStruct((rw.rows, 512), BF16), jax.ShapeDtypeStruct((rw.rows, LANES), F32)],
        scratch_shapes=[pltpu.VMEM((tm, D), BF16)],
        compiler_params=_cparams(1),
        name="ssm_inproj",
    )(xall, norm_g[None, :], mods, mods, w)


def _softplus(x):
    return jnp.maximum(x, 0.0) + jnp.log(1.0 + jnp.exp(-jnp.abs(x)))


def _conv_kernel(lat_tiles, tpb, cpb, x_ref, pv_ref, nx_ref, w_ref, b_ref, dtr_ref, dtb_ref, act_ref, dt_ref):
    i = pl.program_id(0)
    is_lat = i < lat_tiles
    pos = jnp.where(is_lat, i % tpb, (i - lat_tiles) % cpb)
    last_pos = jnp.where(is_lat, tpb - 1, cpb - 1)
    x = x_ref[...].astype(F32)
    tc = x.shape[0]
    prev_row = jnp.where(pos == 0, 0.0, pv_ref[...].astype(F32)[CONV_HALO - 1:CONV_HALO, :])
    next_row = jnp.where(pos == last_pos, 0.0, nx_ref[...].astype(F32)[0:1, :])
    row = lax.broadcasted_iota(jnp.int32, x.shape, 0)
    xm1 = jnp.where(row == 0, prev_row, pltpu.roll(x, 1, 0))
    xp1 = jnp.where(row == tc - 1, next_row, pltpu.roll(x, tc - 1, 0))
    y = w_ref[0:1, :] * xm1 + w_ref[1:2, :] * x + w_ref[2:3, :] * xp1 + b_ref[...]
    act_ref[...] = _silu(y).astype(BF16)
    sp = _softplus(dtr_ref[...] + dtb_ref[...])
    dt_ref[0] = sp
    dt_ref[1] = pltpu.roll(sp, LANES - 16, 1)


def _ssm_conv(rw, xbc, dtr, conv_w, conv_b, dt_bias):
    B, T, C = rw.B, rw.T, rw.C
    tc = CONV_TILE
    assert T % tc == 0 and C % tc == 0
    lat_tiles, tpb, cpb = (B * T) // tc, T // tc, C // tc
    ntiles = rw.rows // tc
    hpt = tc // CONV_HALO
    nhalo = rw.rows // CONV_HALO
    W = xbc.shape[1]
    dtb = jnp.concatenate([dt_bias.reshape(-1), jnp.zeros((LANES - 32,), F32)])[None, :]
    row = lambda i: (i, 0)
    return pl.pallas_call(
        functools.partial(_conv_kernel, lat_tiles, tpb, cpb),
        grid=(ntiles,),
        in_specs=[pl.BlockSpec((tc, W), row),
                  pl.BlockSpec((CONV_HALO, W), lambda i: (jnp.maximum(i * hpt - 1, 0), 0)),
                  pl.BlockSpec((CONV_HALO, W), lambda i: (jnp.minimum((i + 1) * hpt, nhalo - 1), 0)),
                  pl.BlockSpec((3, W), lambda i: (0, 0)), pl.BlockSpec((1, W), lambda i: (0, 0)),
                  pl.BlockSpec((tc, LANES), row), pl.BlockSpec((1, LANES), lambda i: (0, 0))],
        out_specs=[pl.BlockSpec((tc, W), row), pl.BlockSpec((2, tc, LANES), lambda i: (0, i, 0))],
        out_shape=[jax.ShapeDtypeStruct((rw.rows, W), BF16), jax.ShapeDtypeStruct((2, rw.rows, LANES), F32)],
        compiler_params=_cparams(1),
        name="ssm_conv",
    )(xbc, xbc, xbc, conv_w, conv_b[None, :], dtr, dtb)


def _ssd_kernel(actf_ref, actb_ref, dtf_ref, dtb_ref, tri_ref, a_ref, yf_ref, yb_ref, hst):
    @pl.when(pl.program_id(1) == 0)
    def _():
        hst[...] = jnp.zeros_like(hst)

    _ssd_chunk(actf_ref, dtf_ref, tri_ref[0], a_ref[0], yf_ref, hst.at[0])
    _ssd_chunk(actb_ref, dtb_ref, tri_ref[1], a_ref[1], yb_ref, hst.at[1])


def _ssd_chunk(act_ref, dt_ref, tri, avec, y_ref, hst):
    q = SSD_CHUNK
    dt = dt_ref[...]
    da = dt * avec
    acs = jnp.dot(tri, da, preferred_element_type=F32, precision=lax.Precision.HIGHEST)
    tot = jnp.sum(da, axis=0, keepdims=True)
    acs_t = acs.T
    dt_t = dt.T
    eacs = jnp.exp(acs)
    wend = jnp.exp(tot - acs) * dt
    etot = jnp.exp(tot)
    mask = tri > 0.5
    left = lax.broadcasted_iota(jnp.int32, (q, LANES), 1) < HEAD_DIM
    left1 = lax.broadcasted_iota(jnp.int32, (1, LANES), 1) < HEAD_DIM
    for g in range(2):
        bg = act_ref[:, 1024 + g * 128:1152 + g * 128]
        cg = act_ref[:, 1280 + g * 128:1408 + g * 128]
        cb = _nt(cg, bg)
        hin = hst[:, g * 512:(g + 1) * 512]
        yoff = jnp.dot(cg, hin.astype(BF16), preferred_element_type=F32)
        xw, dec = [], []
        for pr in range(4):
            h_a = g * 8 + pr * 2
            h_b = h_a + 1
            c0 = h_a * HEAD_DIM
            x2 = act_ref[:, c0:c0 + LANES]
            outs = []
            for h in (h_a, h_b):
                seg = acs[:, h:h + 1] - acs_t[h:h + 1, :]
                w = cb * jnp.exp(jnp.where(mask, seg, NEG_INF)) * dt_t[h:h + 1, :]
                outs.append(jnp.dot(w.astype(BF16), x2, preferred_element_type=F32))
            yd = jnp.where(left, outs[0], outs[1])
            sc = jnp.where(left, eacs[:, h_a:h_a + 1], eacs[:, h_b:h_b + 1])
            y_ref[:, c0:c0 + LANES] = (yd + yoff[:, pr * LANES:(pr + 1) * LANES] * sc).astype(BF16)
            wsc = jnp.where(left, wend[:, h_a:h_a + 1], wend[:, h_b:h_b + 1])
            xw.append((x2.astype(F32) * wsc).astype(BF16))
            dec.append(jnp.where(left1, etot[:, h_a:h_a + 1], etot[:, h_b:h_b + 1]))
        bg_t = bg.astype(F32).T.astype(BF16)
        snew = jnp.dot(bg_t, jnp.concatenate(xw, axis=1), preferred_element_type=F32)
        hst[:, g * 512:(g + 1) * 512] = hin * jnp.concatenate(dec, axis=1) + snew


def _ssd(rw, act, dt2, a_log):
    B, T, C = rw.B, rw.T, rw.C
    q = SSD_CHUNK
    nct, nlt = C // q, T // q
    ctx0 = (B * T) // q
    r = np.arange(q)
    tri = jnp.asarray(np.stack([r[None, :] <= r[:, None], r[None, :] >= r[:, None]]).astype(np.float32))
    avec = jnp.concatenate([-jnp.exp(a_log.astype(F32)), jnp.zeros((2, LANES - a_log.shape[1]), F32)], axis=1)[:, None, :]

    def blk(d, b, s):
        kc = s if d == 0 else nct - 1 - s
        kl = s - nct if d == 0 else nlt - 1 - (s - nct)
        return jnp.where(s < nct, ctx0 + b * nct + kc, b * nlt + kl)

    aspec = lambda d: pl.BlockSpec((q, act.shape[1]), lambda b, s: (blk(d, b, s), 0))
    dspec = lambda d: pl.BlockSpec((None, q, LANES), lambda b, s: (d, blk(d, b, s), 0))
    yspec = lambda d: pl.BlockSpec((q, 1024), lambda b, s: (blk(d, b, s), 0))
    return pl.pallas_call(
        _ssd_kernel,
        grid=(B, nct + nlt),
        in_specs=[aspec(0), aspec(1), dspec(0), dspec(1),
                  pl.BlockSpec((2, q, q), lambda b, s: (0, 0, 0)),
                  pl.BlockSpec((2, 1, LANES), lambda b, s: (0, 0, 0))],
        out_specs=[yspec(0), yspec(1)],
        out_shape=[jax.ShapeDtypeStruct((rw.rows, 1024), BF16)] * 2,
        scratch_shapes=[pltpu.VMEM((2, q, 1024), F32)],
        compiler_params=_cparams(2),
        name="ssd_scan",
    )(act, act, dt2, dt2, tri, avec)


def _cmul(ar, ai, br, bi):
    return ar * br - ai * bi, ar * bi + ai * br


def _s5_weight_kernel(lre_ref, lim_ref, ls_ref, bre_ref, bim_ref, cre_ref, cim_ref,
                      wsr_ref, wsi_ref, wor_ref, woi_ref, kt_ref, are_ref, aim_ref):
    lre, lim = lre_ref[...], lim_ref[...]
    step = jnp.exp(ls_ref[...])
    er, ei = lre * step, lim * step
    npow = 24
    p = lax.broadcasted_iota(jnp.int32, (1, npow, 1), 1).astype(F32)
    mag = jnp.exp(p * er)
    pre, pim = mag * jnp.cos(p * ei), mag * jnp.sin(p * ei)
    a_re, a_im = pre[:, 1:2, :], pim[:, 1:2, :]
    den = lre * lre + lim * lim
    q_re = ((a_re - 1.0) * lre + a_im * lim) / den
    q_im = (a_im * lre - (a_re - 1.0) * lim) / den
    bb_re, bb_im = _cmul(q_re, q_im, bre_ref[...], bim_ref[...])
    c_re, c_im = cre_ref[...], cim_ref[...]
    ws_r, ws_i, wo_r, wo_i, ca_r, ca_i = [], [], [], [], [], []
    for t in range(S5_Q):
        r, i = _cmul(bb_re, bb_im, pre[:, t:t + 1, :], pim[:, t:t + 1, :])
        ws_r.append(r)
        ws_i.append(i)
        r, i = _cmul(c_re, c_im, pre[:, t:t + 1, :], pim[:, t:t + 1, :])
        ca_r.append(r)
        ca_i.append(i)
        r, i = _cmul(c_re, c_im, pre[:, t + 1:t + 2, :], pim[:, t + 1:t + 2, :])
        wo_r.append(r)
        wo_i.append(-i)
    cat = lambda xs: jnp.concatenate(xs, axis=1)
    wsr_ref[...] = cat(ws_r)
    wsi_ref[...] = cat(ws_i)
    wor_ref[...] = cat(wo_r)
    woi_ref[...] = cat(wo_i)
    bdot = lambda a, b: lax.dot_general(a, b, (((2,), (2,)), ((0,), (0,))), preferred_element_type=F32,
                                        precision=lax.Precision.HIGHEST)
    kt_ref[...] = bdot(cat(ca_r), bb_re) - bdot(cat(ca_i), bb_im)
    are_ref[...] = pre[:, S5_Q:S5_Q + 1, :]
    aim_ref[...] = pim[:, S5_Q:S5_Q + 1, :]


def _s5_weights(lam_re, lam_im, log_step, b_re, b_im, c_re, c_im):
    nd, ng, ns = lam_re.shape
    G = nd * ng
    ch = S5_GROUP
    gb = 8
    qc = S5_Q * ch
    f = lambda a: a.astype(F32)
    args = (f(lam_re).reshape(G, 1, ns), f(lam_im).reshape(G, 1, ns), f(log_step).reshape(G, 1, 1),
            f(b_re).reshape(G, ns, ch).transpose(0, 2, 1), f(b_im).reshape(G, ns, ch).transpose(0, 2, 1),
            f(c_re).reshape(G, ch, ns), f(c_im).reshape(G, ch, ns))
    spec = lambda a: pl.BlockSpec((gb,) + a.shape[1:], lambda i: (i, 0, 0))
    oshape = [jax.ShapeDtypeStruct((G, qc, ns), F32)] * 4 + [jax.ShapeDtypeStruct((G, qc, ch), F32)] \
        + [jax.ShapeDtypeStruct((G, 1, ns), F32)] * 2
    wsr, wsi, wor, woi, kt, a_re, a_im = pl.pallas_call(
        _s5_weight_kernel,
        grid=(G // gb,),
        in_specs=[spec(a) for a in args],
        out_specs=[pl.BlockSpec((gb,) + s.shape[1:], lambda i: (i, 0, 0)) for s in oshape],
        out_shape=oshape,
        compiler_params=_cparams(1),
        name="s5_weights",
    )(*args)

    def by_dir(w, flip_dir):
        w = w.reshape(nd, ng, S5_Q, ch, ns)
        w = jnp.stack([jnp.flip(w[d], axis=1) if d == flip_dir else w[d] for d in range(nd)])
        return w.reshape(nd, ng, qc, ns)

    def pack(w):
        z = jnp.zeros_like(w)
        even = (np.arange(ng) % 2 == 0)[None, :, None, None]
        return jnp.where(even, jnp.concatenate([w, z], axis=-1), jnp.concatenate([z, w], axis=-1)).astype(BF16)

    ws_r, ws_i = pack(by_dir(wsr, 0)), pack(by_dir(wsi, 0))
    wo_r, wo_i = pack(by_dir(wor, 1)), pack(by_dir(woi, 1))
    k = kt.astype(BF16).reshape(nd, ng, S5_Q, ch, ch).transpose(0, 1, 4, 2, 3)
    kf = k.reshape(nd, ng, ch, qc)
    kb = jnp.flip(k, axis=3).reshape(nd, ng, ch, qc)
    rows_f, rows_b = [], []
    for j in range(S5_Q):
        z_f = jnp.zeros((ng, ch, j * ch), BF16)
        z_b = jnp.zeros((ng, ch, (S5_Q - 1 - j) * ch), BF16)
        rows_f.append(jnp.concatenate([z_f, kf[0, :, :, :(S5_Q - j) * ch]], axis=-1))
        rows_b.append(jnp.concatenate([kb[1, :, :, (S5_Q - 1 - j) * ch:], z_b], axis=-1))
    bt = jnp.stack([jnp.stack(rows_f, axis=1), jnp.stack(rows_b, axis=1)]).reshape(nd, ng, qc, qc)
    pair = lambda a: a.reshape(nd, ng // 2, 1, 2 * ns)
    return bt, ws_r, ws_i, wo_r, wo_i, pair(a_re), pair(a_im)


def _s5_kernel(B, nct, nlt, u_ref, bt_ref, wsr_ref, wsi_ref, wor_ref, woi_ref, are_ref, aim_ref, y_ref, s_re, s_im):
    gb = u_ref.shape[0]
    npair = gb // 2
    for d in range(2):
        for pr in range(npair):
            for dst, w_ref in ((s_re, wsr_ref), (s_im, wsi_ref)):
                dst[d, pr] = (jnp.dot(u_ref[2 * pr], w_ref[d, 2 * pr], preferred_element_type=F32)
                              + jnp.dot(u_ref[2 * pr + 1], w_ref[d, 2 * pr + 1], preferred_element_type=F32))
    chains = [(d, pr) for d in range(2) for pr in range(npair)]
    coef = [(are_ref[d, pr], aim_ref[d, pr]) for d, pr in chains]

    def body(s, carry):
        c_bwd = jnp.where(s < nct, nct - 1 - s, 2 * nct + nlt - 1 - s)
        new = []
        for (d, pr), (ar, ai), (hr, hi) in zip(chains, coef, carry):
            rows = pl.ds((s if d == 0 else c_bwd) * B, B)
            sr = s_re[d, pr, rows, :]
            si = s_im[d, pr, rows, :]
            s_re[d, pr, rows, :] = hr
            s_im[d, pr, rows, :] = hi
            new.append((ar * hr - ai * hi + sr, ar * hi + ai * hr + si))
        return tuple(new)

    zero = jnp.zeros((B, LANES), F32)
    lax.fori_loop(0, nct + nlt, body, tuple((zero, zero) for _ in chains))
    for g in range(gb):
        acc = None
        for d in range(2):
            t = (jnp.dot(u_ref[g], bt_ref[d, g], preferred_element_type=F32)
                 + _nt(s_re[d, g // 2].astype(BF16), wor_ref[d, g])
                 + _nt(s_im[d, g // 2].astype(BF16), woi_ref[d, g]))
            acc = t if acc is None else acc + t
        y_ref[g] = acc.astype(BF16)


def _s5(rw, u, weights):
    B, T, C = rw.B, rw.T, rw.C
    bt, ws_r, ws_i, wo_r, wo_i, a_re, a_im = weights
    ng = bt.shape[1]
    ch = S5_GROUP
    q = S5_Q
    nct, nlt = C // q, T // q
    nch = nct + nlt
    qc = q * ch
    gb = 4
    ul = u[:B * T].reshape(B, nlt, q, ng, ch)
    uc = u[B * T:].reshape(B, nct, q, ng, ch)
    ug = jnp.concatenate([uc, ul], axis=1).transpose(3, 1, 0, 2, 4).reshape(ng, nch * B, qc)
    wspec = lambda n: pl.BlockSpec((2, gb, qc, n), lambda i: (0, i, 0, 0))
    aspec = pl.BlockSpec((2, gb // 2, 1, LANES), lambda i: (0, i, 0, 0))
    yg = pl.pallas_call(
        functools.partial(_s5_kernel, B, nct, nlt),
        grid=(ng // gb,),
        in_specs=[pl.BlockSpec((gb, nch * B, qc), lambda i: (i, 0, 0)), wspec(qc), wspec(LANES), wspec(LANES),
                  wspec(LANES), wspec(LANES), aspec, aspec],
        out_specs=pl.BlockSpec((gb, nch * B, qc), lambda i: (i, 0, 0)),
        out_shape=jax.ShapeDtypeStruct((ng, nch * B, qc), BF16),
        scratch_shapes=[pltpu.VMEM((2, gb // 2, nch * B, LANES), F32), pltpu.VMEM((2, gb // 2, nch * B, LANES), F32)],
        compiler_params=_cparams(1),
        name="s5_scan",
    )(ug, bt, ws_r, ws_i, wo_r, wo_i, a_re, a_im)
    yl = yg.reshape(ng, nch, B, q, ch)[:, nct:]
    return yl.transpose(2, 1, 3, 0, 4).reshape(B * T, ng * ch)


def _gelu_tanh(x):
    return 0.5 * x * (1.0 + jnp.tanh(math.sqrt(2.0 / math.pi) * (x + 0.044715 * (x * x * x))))


def _ssm_outproj_kernel(y0_ref, y1_ref, xs_ref, z_ref, v_ref, u_ref, dsk_ref, nw_ref, s5d_ref, gw_ref, gb_ref,
                        x_ref, w_ref, g1_ref, gn_ref, sh2_ref, sc2_ref, wr_ref, br_ref, lt_ref,
                        xo_ref, h2_ref, rt_ref, cnt_ref, carry):
    i = pl.program_id(0)
    y = y0_ref[...].astype(F32) + y1_ref[...].astype(F32) + dsk_ref[...] * xs_ref[...].astype(F32)
    y = _rms(y * _silu(z_ref[...].astype(F32))) * nw_ref[...]
    v = _gelu_tanh(v_ref[...].astype(F32) + s5d_ref[...] * u_ref[...].astype(F32))
    v = v * _sigmoid(jnp.dot(v.astype(BF16), gw_ref[...], preferred_element_type=F32) + gb_ref[...])
    mix = jnp.concatenate([y, v], axis=1).astype(BF16)
    yo = jnp.dot(mix, w_ref[...], preferred_element_type=F32)
    _post_mixer(i, x_ref[...], yo, g1_ref[...], gn_ref[...], sh2_ref[...], sc2_ref[...], wr_ref, br_ref, lt_ref,
                xo_ref, h2_ref, rt_ref, cnt_ref, carry)


def _ssm_outproj(rw, ssd_y, act, z, s5_y, u, d_skip, norm_w, s5_d, glu_w, glu_b, xall, w_out, mods, norm_ffn, wr, br):
    D, tm = rw.D, rw.tm
    ntiles = rw.nlat
    post_in, post_out = _post_specs(rw)
    row = lambda i: (i, 0)
    vec = lambda n: pl.BlockSpec((1, n), lambda i: (0, 0))
    dsk = jnp.repeat(d_skip.astype(F32), HEAD_DIM)[None, :]
    return pl.pallas_call(
        _ssm_outproj_kernel,
        grid=(ntiles,),
        in_specs=[pl.BlockSpec((tm, 1024), row), pl.BlockSpec((tm, 1024), row),
                  pl.BlockSpec((tm, 1024), row), pl.BlockSpec((tm, 1024), row), pl.BlockSpec((tm, 512), row),
                  pl.BlockSpec((tm, 512), row), vec(1024), vec(1024), vec(512),
                  pl.BlockSpec((512, 512), lambda i: (0, 0)), vec(512),
                  pl.BlockSpec((tm, D), row), pl.BlockSpec((1536, D), lambda i: (0, 0)), _mod_spec(rw, 2)] + post_in,
        out_specs=post_out,
        out_shape=_post_shapes(ntiles * tm, D),
        scratch_shapes=[pltpu.VMEM((1, LANES), F32)],
        compiler_params=_cparams(1),
        name="ssm_outproj_router",
    )(ssd_y[0], ssd_y[1], act, z, s5_y, u, dsk, norm_w[None, :], s5_d[None, :], glu_w.astype(BF16), glu_b[None, :],
      xall, w_out.astype(BF16), mods, norm_ffn[None, :], mods, mods, wr, br, _lower_tri(tm))


def kernel(x, c, ctx, c_ctx, mod_w, mod_b, norm_mix, norm_ffn, att_w_in, att_w_out, na_q_norm, na_k_norm, na_rel_bias, wa_q_norm, wa_k_norm, wa_sink, ssm_w_in, ssm_w_out, ssd_conv_w, ssd_conv_b, ssd_dt_bias, ssd_a_log, ssd_d, ssd_norm, s5_lambda_re, s5_lambda_im, s5_log_step, s5_b_re, s5_b_im, s5_c_re, s5_c_im, s5_d, s5_glu_w, s5_glu_b, moe_w_group, moe_b_group, moe_w_expert, moe_b_expert, moe_w13, moe_w2):
    B, T, D = x.shape
    C = ctx.shape[1]
    rw = _Rows(B, T, C, D, ROW_TILE)
    xl = x.reshape(B * T, D)
    xc = ctx.reshape(B * C, D)
    cm = jnp.concatenate([c, c_ctx[None, :], jnp.zeros((8 - B - 1, D), F32)], axis=0)
    mods = _modulation(cm, mod_w, mod_b)
    mods = mods.reshape(mods.shape[0], 8, 1, 6 * D)

    m0 = mods[0]
    qkv = _att_inproj(rw, xl, xc, m0, norm_mix[0], att_w_in[0], na_q_norm[0], na_k_norm[0], wa_q_norm[0],
                      wa_k_norm[0])
    na = _na_attention(rw, qkv, na_rel_bias[0])
    wa = _wa_attention(rw, qkv, wa_sink[0])
    cx = _ctx_attention(rw, qkv, wa_sink[0])
    wr, br = _router_weights(moe_w_group[0], moe_b_group[0], moe_w_expert[0], moe_b_expert[0])
    xall, h2, route, counts = _att_outproj(rw, na, wa, cx, xl, xc, att_w_out[0], m0, norm_ffn[0], wr, br)
    y1, y2 = _moe(h2, route, counts, moe_w13, moe_w2, 0)
    xall = _combine(rw, rw.ntot, xall, y1, y2, route, m0)

    m1 = mods[1]
    z, xbc, u, dtr = _ssm_inproj(rw, xall, m1, norm_mix[1], ssm_w_in[0])
    act, dt2 = _ssm_conv(rw, xbc, dtr, ssd_conv_w[0], ssd_conv_b[0], ssd_dt_bias[0])
    ssd_y = _ssd(rw, act, dt2, ssd_a_log[0])
    s5_w = _s5_weights(s5_lambda_re[0], s5_lambda_im[0], s5_log_step[0], s5_b_re[0], s5_b_im[0], s5_c_re[0],
                       s5_c_im[0])
    s5_y = _s5(rw, u, s5_w)
    wr, br = _router_weights(moe_w_group[1], moe_b_group[1], moe_w_expert[1], moe_b_expert[1])
    xlat, h2, route, counts = _ssm_outproj(rw, ssd_y, act, z, s5_y, u, ssd_d[0], ssd_norm[0], s5_d[0], s5_glu_w[0],
                                           s5_glu_b[0], xall, ssm_w_out[0], m1, norm_ffn[1], wr, br)
    y1, y2 = _moe(h2, route, counts, moe_w13, moe_w2, 1)
    out = _combine(rw, rw.nlat, xlat, y1, y2, route, m1)
    return out.reshape(B, T, D)
```

```python
import functools
import math

import jax
import jax.numpy as jnp
import numpy as np
from jax import lax
from jax.experimental import pallas as pl
from jax.experimental.pallas import tpu as pltpu

F32 = jnp.float32
BF16 = jnp.bfloat16

EPS = 1e-6
NEG_INF = -1e30
GRID_W = 64
HEAD_DIM = 64
NA_KH = 8
NA_KW = 16
WA_BLOCK = 128
ROPE_BASE = 10000.0
SSD_CHUNK = 128
S5_GROUP = 16
S5_STATE = 64
MOE_GROUPS = 4
MOE_EPG = 8
MOE_EXPERTS = MOE_GROUPS * MOE_EPG

LANES = 128
ROW_TILE = 512
MOE_TILE = 256
VMEM_LIMIT = 56 * 1024 * 1024
MOE_VMEM_LIMIT = 60 * 1024 * 1024


def _cparams(n_axes, vmem=VMEM_LIMIT):
    return pltpu.CompilerParams(dimension_semantics=("arbitrary",) * n_axes, vmem_limit_bytes=vmem)


def _sigmoid(x):
    return 1.0 / (1.0 + jnp.exp(-x))


def _silu(x):
    return x * _sigmoid(x)


def _rms(x, eps=EPS):
    return x * lax.rsqrt(jnp.mean(x * x, axis=-1, keepdims=True) + eps)


def _ada_norm(x, g, shift, scale):
    return (_rms(x) * g) * (1.0 + scale) + shift


def _mod_kernel(c_ref, w_ref, b_ref, o_ref):
    a = _silu(c_ref[...])
    o_ref[...] = jnp.dot(a, w_ref[...], preferred_element_type=F32, precision=lax.Precision.HIGHEST) + b_ref[...]


def _modulation(cm, mod_w, mod_b):
    depth, d, n6 = mod_w.shape
    tn = 1024
    return pl.pallas_call(
        _mod_kernel,
        grid=(depth, n6 // tn),
        in_specs=[pl.BlockSpec((8, d), lambda l, j: (0, 0)),
                  pl.BlockSpec((None, d, tn), lambda l, j: (l, 0, j)),
                  pl.BlockSpec((None, 1, tn), lambda l, j: (l, 0, j))],
        out_specs=pl.BlockSpec((None, 8, tn), lambda l, j: (l, 0, j)),
        out_shape=jax.ShapeDtypeStruct((depth, 8, n6), F32),
        compiler_params=_cparams(2),
        name="modulation",
    )(cm, mod_w, mod_b.reshape(depth, 1, n6))


class _Rows:
    def __init__(self, B, T, C, D, tm):
        assert T % tm == 0 and (B * C) % tm == 0
        self.B, self.T, self.C, self.D, self.tm = B, T, C, D, tm
        self.tpb = T // tm
        self.nlat = B * self.tpb
        self.nctx = (B * C) // tm
        self.ntot = self.nlat + self.nctx
        self.rows = B * (T + C)

    def group(self, i):
        return jnp.where(i < self.nlat, i // self.tpb, self.B)


def _mod_spec(rw, col):
    return pl.BlockSpec((None, 1, rw.D), lambda i, *_: (rw.group(i), 0, col))


def _seg_norm(y, seg, gcol):
    ss = jnp.dot((y * y).astype(BF16), seg, preferred_element_type=F32)
    return y * lax.rsqrt(ss + EPS) * gcol


def _rope(y, cos, sin):
    w = y.shape[-1]
    lane = lax.broadcasted_iota(jnp.int32, y.shape, 1)
    first = (lane % 32) < 16
    partner = jnp.where(first, pltpu.roll(y, w - 16, 1), pltpu.roll(y, 16, 1))
    return y * cos + partner * sin


def _dup_halves(k):
    lane = lax.broadcasted_iota(jnp.int32, k.shape, 1)
    sw = pltpu.roll(k, 64, 1)
    return jnp.where(lane < 64, k, sw), jnp.where(lane < 64, sw, k)


def _att_inproj_kernel(nlat, xl_ref, xc_ref, g_ref, sh_ref, sc_ref, w_ref, gcol_ref, cos_ref, sin_ref, seg_ref,
                       o_ref, h_scr):
    i = pl.program_id(0)
    x = jnp.where(i < nlat, xl_ref[...], xc_ref[...])
    h_scr[...] = _ada_norm(x, g_ref[...], sh_ref[...], sc_ref[...]).astype(BF16)
    seg = seg_ref[...]
    cos2 = jnp.concatenate([cos_ref[...], cos_ref[...]], axis=1)
    sin2 = jnp.concatenate([sin_ref[...], sin_ref[...]], axis=1)
    for c in range(9):
        c0 = c * 256
        y = jnp.dot(h_scr[...], w_ref[:, c0:c0 + 256], preferred_element_type=F32)
        gcol = gcol_ref[:, c0:c0 + 256]
        if c in (0, 1, 2, 3):
            o_ref[:, c0:c0 + 256] = _seg_norm(y, seg, gcol).astype(BF16)
        elif c in (4, 5):
            o_ref[:, c0:c0 + 256] = y.astype(BF16)
        elif c in (6, 7):
            o_ref[:, c0:c0 + 256] = _rope(_seg_norm(y, seg, gcol), cos2, sin2).astype(BF16)
        else:
            lane = lax.broadcasted_iota(jnp.int32, y.shape, 1)
            yk = jnp.where(lane < 128, _seg_norm(y, seg, gcol), y)
            yr = jnp.where(lane < 128, _rope(yk, cos2, sin2), yk)
            k0, k1 = _dup_halves(yr[:, :128])
            v0, v1 = _dup_halves(yr[:, 128:])
            o_ref[:, 2048:2176] = k0.astype(BF16)
            o_ref[:, 2176:2304] = k1.astype(BF16)
            o_ref[:, 2304:2432] = v0.astype(BF16)
            o_ref[:, 2432:2560] = v1.astype(BF16)


def _rope_tables(T, tm):
    t = np.arange(T)
    d = np.arange(HEAD_DIM)
    nf = HEAD_DIM // 4
    inv = jnp.asarray(ROPE_BASE, F32) ** (-jnp.arange(nf, dtype=F32) / nf)
    pos = np.where((d // 32 == 0)[None, :], (t // GRID_W)[:, None], (t % GRID_W)[:, None])
    ang = jnp.asarray(pos, F32) * inv[d % nf][None, :]
    sign = np.where((d % 32) < 16, -1.0, 1.0).astype(np.float32)
    cos = jnp.cos(ang)
    sin = jnp.sin(ang) * sign[None, :]
    cos = jnp.concatenate([cos, jnp.ones((tm, HEAD_DIM), F32)], axis=0)
    sin = jnp.concatenate([sin, jnp.zeros((tm, HEAD_DIM), F32)], axis=0)
    return jnp.tile(cos, (1, 2)), jnp.tile(sin, (1, 2))


def _att_inproj(rw, xl, xc, mods, norm_g, w_in, na_qn, na_kn, wa_qn, wa_kn):
    D, tm = rw.D, rw.tm
    scale = HEAD_DIM ** -0.5
    gcol = jnp.concatenate([jnp.tile(na_qn * scale, 8), jnp.tile(na_kn, 8), jnp.ones((512,), F32),
                            jnp.tile(wa_qn * scale, 8), jnp.tile(wa_kn, 2), jnp.ones((128,), F32)])[None, :]
    cos, sin = _rope_tables(rw.T, tm)
    segn = np.arange(256) // 64
    seg = jnp.asarray((segn[:, None] == segn[None, :]).astype(np.float32) / 64.0, BF16)
    nlat, tpb = rw.nlat, rw.tpb
    return pl.pallas_call(
        functools.partial(_att_inproj_kernel, nlat),
        grid=(rw.ntot,),
        in_specs=[pl.BlockSpec((tm, D), lambda i: (jnp.minimum(i, nlat - 1), 0)),
                  pl.BlockSpec((tm, D), lambda i: (jnp.maximum(i - nlat, 0), 0)),
                  pl.BlockSpec((1, D), lambda i: (0, 0)),
                  _mod_spec(rw, 0), _mod_spec(rw, 1),
                  pl.BlockSpec((D, 2304), lambda i: (0, 0)),
                  pl.BlockSpec((1, 2304), lambda i: (0, 0)),
                  pl.BlockSpec((tm, 128), lambda i: (jnp.where(i < nlat, i % tpb, tpb), 0)),
                  pl.BlockSpec((tm, 128), lambda i: (jnp.where(i < nlat, i % tpb, tpb), 0)),
                  pl.BlockSpec((256, 256), lambda i: (0, 0))],
        out_specs=pl.BlockSpec((tm, 2560), lambda i: (i, 0)),
        out_shape=jax.ShapeDtypeStruct((rw.rows, 2560), BF16),
        scratch_shapes=[pltpu.VMEM((tm, D), BF16)],
        compiler_params=_cparams(1),
        name="att_inproj",
    )(xl, xc, norm_g[None, :], mods, mods, w_in.astype(BF16), gcol, cos, sin, seg)


def _route(lg, lt, carry):
    lane = lax.broadcasted_iota(jnp.int32, lg.shape, 1).astype(F32)
    gm = lane < MOE_GROUPS
    mg = jnp.max(jnp.where(gm, lg, NEG_INF), axis=-1, keepdims=True)
    eg = jnp.where(gm, jnp.exp(jnp.where(gm, lg, NEG_INF) - mg), 0.0)
    pg = eg / jnp.sum(eg, axis=-1, keepdims=True)
    ptop = jnp.max(pg, axis=-1, keepdims=True)
    gidx = jnp.min(jnp.where(gm & (pg == ptop), lane, 1e9), axis=-1, keepdims=True)
    lo = MOE_GROUPS + MOE_EPG * gidx
    em = (lane >= lo) & (lane < lo + MOE_EPG)
    le = jnp.where(em, lg, NEG_INF)
    ee = jnp.where(em, jnp.exp(le - jnp.max(le, axis=-1, keepdims=True)), 0.0)
    pe = ee / jnp.sum(ee, axis=-1, keepdims=True)
    v1 = jnp.max(jnp.where(em, pe, -1.0), axis=-1, keepdims=True)
    i1 = jnp.min(jnp.where(em & (pe == v1), lane, 1e9), axis=-1, keepdims=True)
    em2 = em & (lane != i1)
    v2 = jnp.max(jnp.where(em2, pe, -1.0), axis=-1, keepdims=True)
    i2 = jnp.min(jnp.where(em2 & (pe == v2), lane, 1e9), axis=-1, keepdims=True)
    den = v1 + v2
    w1 = v1 / den * ptop
    w2 = v2 / den * ptop
    e1 = i1 - MOE_GROUPS
    e2 = i2 - MOE_GROUPS
    m1 = lane == e1
    m2 = lane == e2
    oh = jnp.where(m1 | m2, 1.0, 0.0)
    cnt = jnp.dot(lt, oh.astype(BF16), preferred_element_type=F32) + carry
    r1 = jnp.sum(jnp.where(m1, cnt, 0.0), axis=-1, keepdims=True)
    r2 = jnp.sum(jnp.where(m2, cnt, 0.0), axis=-1, keepdims=True)
    route = jnp.where(lane == 0, e1, jnp.where(lane == 1, e2, jnp.where(lane == 2, w1, jnp.where(
        lane == 3, w2, jnp.where(lane == 4, r1, jnp.where(lane == 5, r2, 0.0))))))
    return route, carry + jnp.sum(oh, axis=0, keepdims=True)


def _post_mixer(i, x, y, g1, gn, sh2, sc2, wr_ref, br_ref, lt_ref, xo_ref, h2_ref, rt_ref, cnt_ref, carry):
    xn = x + g1 * y
    xo_ref[...] = xn
    h2 = _ada_norm(xn, gn, sh2, sc2)
    hb = h2.astype(BF16)
    hbf = hb.astype(F32)
    half = h2.shape[1] // 2
    h2_ref[...] = pltpu.pack_elementwise([h2[:, :half], h2[:, half:]], packed_dtype=BF16)
    hl = (h2 - hbf).astype(BF16)
    lg = (jnp.dot(hb, wr_ref[0], preferred_element_type=F32)
          + (jnp.dot(hb, wr_ref[1], preferred_element_type=F32) + jnp.dot(hl, wr_ref[0], preferred_element_type=F32))
          + br_ref[...])

    @pl.when(i == 0)
    def _():
        carry[...] = jnp.zeros_like(carry)

    route, newc = _route(lg, lt_ref[...], carry[...])
    rt_ref[...] = route
    carry[...] = newc
    cnt_ref[...] = newc


def _att_outproj_kernel(nlat, na_ref, wa_ref, cx_ref, xl_ref, xc_ref, w_ref, g1_ref, gn_ref, sh2_ref, sc2_ref,
                        wr_ref, br_ref, lt_ref, xo_ref, h2_ref, rt_ref, cnt_ref, carry):
    i = pl.program_id(0)
    lat = i < nlat
    mix = jnp.where(lat, jnp.concatenate([na_ref[...], wa_ref[...]], axis=1), cx_ref[...])
    y = jnp.dot(mix, w_ref[...], preferred_element_type=F32)
    x = jnp.where(lat, xl_ref[...], xc_ref[...])
    _post_mixer(i, x, y, g1_ref[...], gn_ref[...], sh2_ref[...], sc2_ref[...], wr_ref, br_ref, lt_ref,
                xo_ref, h2_ref, rt_ref, cnt_ref, carry)


def _router_weights(w_group, b_group, w_expert, b_expert):
    D = w_group.shape[0]
    pad = LANES - MOE_GROUPS - MOE_EXPERTS
    wr = jnp.concatenate([w_group, w_expert, jnp.zeros((D, pad), F32)], axis=1)
    br = jnp.concatenate([b_group, b_expert, jnp.zeros((pad,), F32)])[None, :]
    hi = wr.astype(BF16)
    lo = (wr - hi.astype(F32)).astype(BF16)
    return jnp.stack([hi, lo]), br


def _lower_tri(tm):
    r = np.arange(tm)
    return jnp.asarray((r[None, :] < r[:, None]).astype(np.float32), BF16)


def _post_specs(rw):
    D, tm = rw.D, rw.tm
    return ([pl.BlockSpec((1, D), lambda i: (0, 0)), _mod_spec(rw, 3), _mod_spec(rw, 4),
             pl.BlockSpec((2, D, LANES), lambda i: (0, 0, 0)), pl.BlockSpec((1, LANES), lambda i: (0, 0)),
             pl.BlockSpec((tm, tm), lambda i: (0, 0))],
            [pl.BlockSpec((tm, D), lambda i: (i, 0)), pl.BlockSpec((tm, D // 2), lambda i: (i, 0)),
             pl.BlockSpec((tm, LANES), lambda i: (i, 0)), pl.BlockSpec((1, LANES), lambda i: (0, 0))])


def _post_shapes(nrows, D):
    return [jax.ShapeDtypeStruct((nrows, D), F32), jax.ShapeDtypeStruct((nrows, D // 2), jnp.uint32),
            jax.ShapeDtypeStruct((nrows, LANES), F32), jax.ShapeDtypeStruct((1, LANES), F32)]


def _att_outproj(rw, na, wa, cx, xl, xc, w_out, mods, norm_ffn, wr, br):
    D, tm, nlat = rw.D, rw.tm, rw.nlat
    post_in, post_out = _post_specs(rw)
    latmap = lambda i: (jnp.minimum(i, nlat - 1), 0)
    ctxmap = lambda i: (jnp.maximum(i - nlat, 0), 0)
    return pl.pallas_call(
        functools.partial(_att_outproj_kernel, nlat),
        grid=(rw.ntot,),
        in_specs=[pl.BlockSpec((tm, 512), latmap), pl.BlockSpec((tm, 512), latmap), pl.BlockSpec((tm, D), ctxmap),
                  pl.BlockSpec((tm, D), latmap), pl.BlockSpec((tm, D), ctxmap),
                  pl.BlockSpec((D, D), lambda i: (0, 0)), _mod_spec(rw, 2)] + post_in,
        out_specs=post_out,
        out_shape=_post_shapes(rw.rows, D),
        scratch_shapes=[pltpu.VMEM((1, LANES), F32)],
        compiler_params=_cparams(1),
        name="att_outproj_router",
    )(na, wa, cx, xl, xc, w_out.astype(BF16), mods, norm_ffn[None, :], mods, mods, wr, br, _lower_tri(tm))


def _moe_kernel(te_ref, nu_ref, src_ref, hp_ref, w13_ref, w2_ref, o_ref, w13b, w2b, xbuf):
    i = pl.program_id(0)
    prev = te_ref[jnp.maximum(i - 1, 0)]
    changed = (i == 0) | (te_ref[i] != prev)

    @pl.when(changed)
    def _():
        w13b[...] = w13_ref[...].astype(BF16)
        w2b[...] = w2_ref[...].astype(BF16)

    @pl.when(i < nu_ref[0])
    def _():
        def fetch(j, carry):
            xbuf[pl.ds(j, 1), :] = hp_ref[pl.ds(src_ref[0, j], 1), :]
            return carry

        lax.fori_loop(0, xbuf.shape[0], fetch, 0, unroll=8)
        ff = w2b.shape[0]
        half = xbuf.shape[1]
        w = xbuf[...]
        unpack = functools.partial(pltpu.unpack_elementwise, packed_dtype=BF16, unpacked_dtype=F32)
        x_lo = unpack(w, index=0).astype(BF16)
        x_hi = unpack(w, index=1).astype(BF16)
        a13 = (jnp.dot(x_lo, w13b[:half, :], preferred_element_type=F32)
               + jnp.dot(x_hi, w13b[half:, :], preferred_element_type=F32))
        act = _silu(a13[:, :ff]) * a13[:, ff:]
        o_ref[...] = jnp.dot(act.astype(BF16), w2b[...], preferred_element_type=F32).astype(BF16)

    @pl.when(i >= nu_ref[0])
    def _():
        o_ref[...] = jnp.zeros_like(o_ref)


def _moe(h2p, route, counts, w13, w2, layer):
    N = h2p.shape[0]
    D = 2 * h2p.shape[1]
    _, E, _, F2 = w13.shape
    tg = MOE_TILE
    nt = (2 * N) // tg + E
    e = route[:, 0:2].astype(jnp.int32)
    rank = route[:, 4:6].astype(jnp.int32)
    cnt = counts[0, :E].astype(jnp.int32)
    ntile_e = (cnt + tg - 1) // tg
    tile_end = jnp.cumsum(ntile_e)
    offs = (tile_end - ntile_e) * tg
    onehot = (e[:, :, None] == jnp.arange(E, dtype=jnp.int32)).astype(jnp.int32)
    dest = jnp.sum(onehot * offs, axis=-1) + rank
    src = jnp.zeros((nt * tg,), jnp.int32).at[dest.reshape(-1)].set(jnp.repeat(jnp.arange(N, dtype=jnp.int32), 2))
    tile_id = jnp.arange(nt, dtype=jnp.int32)
    nu = tile_end[-1:].astype(jnp.int32)
    te = jnp.sum((tile_end[None, :] <= jnp.minimum(tile_id, nu[0] - 1)[:, None]).astype(jnp.int32), axis=1)
    te = jnp.minimum(te, E - 1)
    ys = pl.pallas_call(
        _moe_kernel,
        grid_spec=pltpu.PrefetchScalarGridSpec(
            num_scalar_prefetch=2,
            grid=(nt,),
            in_specs=[pl.BlockSpec((None, 1, tg), lambda i, te, nu: (i, 0, 0), memory_space=pltpu.SMEM),
                      pl.BlockSpec((N, D // 2), lambda i, te, nu: (0, 0), pipeline_mode=pl.Buffered(1)),
                      pl.BlockSpec((None, None, D, F2), lambda i, te, nu: (layer, te[i], 0, 0)),
                      pl.BlockSpec((None, None, F2 // 2, D), lambda i, te, nu: (layer, te[i], 0, 0))],
            out_specs=pl.BlockSpec((tg, D), lambda i, te, nu: (i, 0)),
            scratch_shapes=[pltpu.VMEM((D, F2), BF16), pltpu.VMEM((F2 // 2, D), BF16),
                            pltpu.VMEM((tg, D // 2), jnp.uint32)]),
        out_shape=jax.ShapeDtypeStruct((nt * tg, D), BF16),
        compiler_params=_cparams(1, vmem=MOE_VMEM_LIMIT),
        name="moe_experts",
    )(te, nu, src.reshape(nt, 1, tg), h2p, w13, w2)
    pick = lambda k: ys.at[dest[:, k]].get(mode="promise_in_bounds")
    return pick(0), pick(1)


def _combine_kernel(x_ref, y1_ref, y2_ref, rt_ref, g2_ref, o_ref):
    rt = rt_ref[...]
    f = rt[:, 2:3] * y1_ref[...].astype(F32) + rt[:, 3:4] * y2_ref[...].astype(F32)
    o_ref[...] = x_ref[...] + g2_ref[...] * f


def _combine(rw, ntiles, xall, y1, y2, route, mods):
    D, tm = rw.D, rw.tm
    row = lambda i: (i, 0)
    return pl.pallas_call(
        _combine_kernel,
        grid=(ntiles,),
        in_specs=[pl.BlockSpec((tm, D), row), pl.BlockSpec((tm, D), row), pl.BlockSpec((tm, D), row),
                  pl.BlockSpec((tm, LANES), row), _mod_spec(rw, 5)],
        out_specs=pl.BlockSpec((tm, D), row),
        out_shape=jax.ShapeDtypeStruct((ntiles * tm, D), F32),
        compiler_params=_cparams(1),
        name="moe_combine",
    )(xall, y1, y2, route, mods)


NA_QROWS = 8
NA_KROWS = 16


def _na_bias(rpb, n_rb):
    H = rpb.shape[0]
    i = np.arange(GRID_W)
    c0 = np.clip(i - NA_KW // 2, 0, GRID_W - NA_KW)
    j = np.arange(GRID_W)
    colvalid = (j[None, :] >= c0[:, None]) & (j[None, :] < c0[:, None] + NA_KW)
    dc = np.clip(j[None, :] - i[:, None] + NA_KW - 1, 0, 2 * NA_KW - 2)
    onehot = ((dc[None] == np.arange(2 * NA_KW - 1)[:, None, None]) & colvalid[None]).astype(np.float32)
    tiles = jnp.einsum('hrc,cij->hrij', rpb.astype(F32), jnp.asarray(onehot), precision=lax.Precision.HIGHEST)
    tiles = tiles + jnp.asarray(np.where(colvalid, 0.0, NEG_INF).astype(np.float32))
    flat = tiles.transpose(0, 2, 1, 3).reshape(H, GRID_W, (2 * NA_KH - 1) * GRID_W)
    blocks = []
    for variant in range(3):
        for a in range(NA_QROWS):
            start = (max(a - 4, 0) + 4, a, min(a, 4))[variant]
            dr0 = start - a + 3
            neg = lambda n: jnp.full((H, GRID_W, n * GRID_W), NEG_INF, F32)
            blocks.append(jnp.concatenate([neg(start), flat[:, :, dr0 * GRID_W:(dr0 + NA_KH) * GRID_W],
                                           neg(NA_KROWS - NA_KH - start)], axis=-1))
    return jnp.stack(blocks, axis=1).reshape(H, 3, NA_QROWS * GRID_W, NA_KROWS * GRID_W)


def _softmax_pv(parts, extra=None, rc=64):
    m_rows = parts[0][0].shape[0]
    probs = [[] for _ in parts]
    inv_l = []
    for r0 in range(0, m_rows, rc):
        sc = []
        for s, _, bias_fn, valid_fn in parts:
            c = s[r0:r0 + rc]
            if bias_fn is not None:
                c = c + bias_fn(r0, rc)
            if valid_fn is not None:
                c = jnp.where(valid_fn(r0, rc), c, NEG_INF)
            sc.append(c)
        mx = functools.reduce(jnp.maximum, [jnp.max(c, axis=-1, keepdims=True) for c in sc])
        if extra is not None:
            mx = jnp.maximum(mx, extra[r0:r0 + rc])
        l = jnp.zeros_like(mx) if extra is None else jnp.exp(extra[r0:r0 + rc] - mx)
        for k, c in enumerate(sc):
            p = jnp.exp(c - mx)
            l = l + jnp.sum(p, axis=-1, keepdims=True)
            probs[k].append(p.astype(BF16))
        inv_l.append(1.0 / l)
    o = None
    for k, (_, v, _, _) in enumerate(parts):
        pv = jnp.dot(jnp.concatenate(probs[k], axis=0), v, preferred_element_type=F32)
        o = pv if o is None else o + pv
    return o * jnp.concatenate(inv_l, axis=0)


def _nt(a, b):
    return lax.dot_general(a, b, (((1,), (1,)), ((), ())), preferred_element_type=F32)


def _na_kernel(q_ref, k0, k1, k2, k3, v0, v1, v2, v3, kc_ref, vc_ref, bias_ref, o_ref):
    q2 = q_ref[...]
    kw = jnp.concatenate([k0[...], k1[...], k2[...], k3[...]], axis=0)
    vw = jnp.concatenate([v0[...], v1[...], v2[...], v3[...]], axis=0)
    kc = kc_ref[...]
    vc = vc_ref[...]
    lane = lax.broadcasted_iota(jnp.int32, q2.shape, 1)
    out = jnp.zeros(q2.shape, F32)
    for hh in range(2):
        m = (lane < HEAD_DIM) if hh == 0 else (lane >= HEAD_DIM)
        qm = jnp.where(m, q2, jnp.zeros_like(q2))
        o = _softmax_pv([(_nt(qm, kw), vw, lambda r0, rc, hh=hh: bias_ref[hh, r0:r0 + rc, :], None),
                         (_nt(qm, kc), vc, None, None)], rc=32)
        out = jnp.where(m, o, out)
    o_ref[...] = out.astype(BF16)


def _na_attention(rw, qkv, rpb):
    B, T, C = rw.B, rw.T, rw.C
    tq = NA_QROWS * GRID_W
    tk = tq // 2
    n_rb = T // tq
    nkb = T // tk
    assert T % tq == 0 and n_rb >= 2 and (B * T) % C == 0
    bias = _na_bias(rpb, n_rb)
    ctxrow = (B * T) // C

    def kvspec(j, col):
        return pl.BlockSpec((tk, LANES), lambda p, rb, b: (b * nkb + jnp.clip(2 * rb - 1 + j, 0, nkb - 1), col + p))

    return pl.pallas_call(
        _na_kernel,
        grid=(4, n_rb, B),
        in_specs=[pl.BlockSpec((tq, LANES), lambda p, rb, b: (b * n_rb + rb, p))]
        + [kvspec(j, 4) for j in range(4)] + [kvspec(j, 8) for j in range(4)]
        + [pl.BlockSpec((C, LANES), lambda p, rb, b: (ctxrow + b, 4 + p)),
           pl.BlockSpec((C, LANES), lambda p, rb, b: (ctxrow + b, 8 + p)),
           pl.BlockSpec((2, None, tq, 2 * tq),
                        lambda p, rb, b: (p, jnp.where(rb == 0, 0, jnp.where(rb == n_rb - 1, 2, 1)), 0, 0))],
        out_specs=pl.BlockSpec((tq, LANES), lambda p, rb, b: (b * n_rb + rb, p)),
        out_shape=jax.ShapeDtypeStruct((B * T, 4 * LANES), BF16),
        compiler_params=_cparams(3),
        name="neighbourhood_attention",
    )(qkv, *([qkv] * 10), bias)


def _wa_kernel(nb, sink_ref, q_ref, kp, kc_, kn, vp, vc_, vn, kx_ref, vx_ref, o_ref):
    n = pl.program_id(1)
    blk = q_ref.shape[0]
    lane = lax.broadcasted_iota(jnp.int32, (blk, LANES), 1)
    zero = jnp.zeros((blk, LANES), BF16)
    qi = lax.broadcasted_iota(jnp.int32, (blk, 3 * blk), 0)
    ks = lax.broadcasted_iota(jnp.int32, (blk, 3 * blk), 1)
    lo = jnp.where(n > 0, 0, blk)
    hi = jnp.where(n < nb - 1, 3 * blk, 2 * blk)
    valid = (ks >= qi) & (ks <= qi + 2 * blk) & (ks >= lo) & (ks < hi)
    band_mask = jnp.where(valid, 0.0, NEG_INF)
    for kv in range(2):
        parts = []
        for pr in range(2):
            c0 = kv * 2 * LANES + pr * LANES
            qp = q_ref[:, c0:c0 + LANES]
            parts += [jnp.where(lane < HEAD_DIM, qp, zero), jnp.where(lane >= HEAD_DIM, qp, zero)]
        qs = jnp.concatenate(parts, axis=0)
        cs = slice(kv * LANES, (kv + 1) * LANES)
        kb = jnp.concatenate([kp[:, cs], kc_[:, cs], kn[:, cs]], axis=0)
        vb = jnp.concatenate([vp[:, cs], vc_[:, cs], vn[:, cs]], axis=0)
        sink = jnp.concatenate([jnp.full((blk, 1), sink_ref[kv * 4 + g], F32) for g in range(4)], axis=0)
        o = _softmax_pv([(_nt(qs, kb), vb, lambda r0, rc: band_mask[r0 % blk:r0 % blk + rc], None),
                         (_nt(qs, kx_ref[:, cs]), vx_ref[:, cs], None, None)], extra=sink, rc=64)
        o_ref[:, kv * 2 * LANES:kv * 2 * LANES + LANES] = jnp.where(lane < HEAD_DIM, o[0:blk], o[blk:2 * blk]).astype(BF16)
        o_ref[:, kv * 2 * LANES + LANES:(kv + 1) * 2 * LANES] = jnp.where(
            lane < HEAD_DIM, o[2 * blk:3 * blk], o[3 * blk:4 * blk]).astype(BF16)


def _wa_attention(rw, qkv, sink):
    B, T, C = rw.B, rw.T, rw.C
    blk = WA_BLOCK
    nb = T // blk
    ctxrow = (B * T) // C

    def kvspec(j, col):
        return pl.BlockSpec((blk, 2 * LANES), lambda b, n: (b * nb + jnp.clip(n + j, 0, nb - 1), col))

    return pl.pallas_call(
        functools.partial(_wa_kernel, nb),
        grid=(B, nb),
        in_specs=[pl.BlockSpec(memory_space=pltpu.SMEM),
                  pl.BlockSpec((blk, 4 * LANES), lambda b, n: (b * nb + n, 3))]
        + [kvspec(j, 8) for j in (-1, 0, 1)] + [kvspec(j, 9) for j in (-1, 0, 1)]
        + [pl.BlockSpec((C, 2 * LANES), lambda b, n: (ctxrow + b, 8)),
           pl.BlockSpec((C, 2 * LANES), lambda b, n: (ctxrow + b, 9))],
        out_specs=pl.BlockSpec((blk, 4 * LANES), lambda b, n: (b * nb + n, 0)),
        out_shape=jax.ShapeDtypeStruct((B * T, 4 * LANES), BF16),
        compiler_params=_cparams(2),
        name="window_attention",
    )(sink.astype(F32), qkv, *([qkv] * 8))


def _ctx_attn_kernel(sink_ref, t_ref, o_ref):
    C = t_ref.shape[0]
    lane = lax.broadcasted_iota(jnp.int32, (C, LANES), 1)
    zero = jnp.zeros((C, LANES), BF16)

    def pair(q2, k2, v2, sinks):
        out = jnp.zeros((C, LANES), F32)
        for hh in range(2):
            m = (lane < HEAD_DIM) if hh == 0 else (lane >= HEAD_DIM)
            extra = None if sinks is None else jnp.full((C, 1), sinks[hh], F32)
            o = _softmax_pv([(_nt(jnp.where(m, q2, zero), k2), v2, None, None)], extra=extra, rc=64)
            out = jnp.where(m, o, out)
        return out.astype(BF16)

    for p in range(4):
        c = p * LANES
        o_ref[:, c:c + LANES] = pair(t_ref[:, c:c + LANES], t_ref[:, 512 + c:640 + c], t_ref[:, 1024 + c:1152 + c], None)
    for kv in range(2):
        kd = t_ref[:, 2048 + kv * LANES:2176 + kv * LANES]
        vd = t_ref[:, 2304 + kv * LANES:2432 + kv * LANES]
        for pr in range(2):
            c = kv * 256 + pr * LANES
            h0 = kv * 4 + pr * 2
            o_ref[:, 512 + c:640 + c] = pair(t_ref[:, 1536 + c:1664 + c], kd, vd, (sink_ref[h0], sink_ref[h0 + 1]))


def _ctx_attention(rw, qkv, sink):
    B, T, C = rw.B, rw.T, rw.C
    ctxrow = (B * T) // C
    return pl.pallas_call(
        _ctx_attn_kernel,
        grid=(B,),
        in_specs=[pl.BlockSpec(memory_space=pltpu.SMEM),
                  pl.BlockSpec((C, qkv.shape[1]), lambda b: (ctxrow + b, 0))],
        out_specs=pl.BlockSpec((C, 8 * LANES), lambda b: (b, 0)),
        out_shape=jax.ShapeDtypeStruct((B * C, 8 * LANES), BF16),
        compiler_params=_cparams(1),
        name="context_attention",
    )(sink.astype(F32), qkv)


S5_Q = 16
CONV_TILE = 256
CONV_HALO = 16


def _ssm_inproj_kernel(x_ref, g_ref, sh_ref, sc_ref, w_ref, z_ref, xbc_ref, u_ref, uj_ref, dt_ref, h_scr, u_scr):
    h_scr[...] = _ada_norm(x_ref[...], g_ref[...], sh_ref[...], sc_ref[...]).astype(BF16)

    def mm(c0, n):
        return jnp.dot(h_scr[...], w_ref[:, c0:c0 + n], preferred_element_type=F32)

    for c in range(4):
        z_ref[:, c * 256:(c + 1) * 256] = mm(c * 256, 256).astype(BF16)
    for c in range(6):
        xbc_ref[:, c * 256:(c + 1) * 256] = mm(1024 + c * 256, 256).astype(BF16)
    for c in range(2):
        y = mm(2560 + c * 256, 256)
        u_ref[:, c * 256:(c + 1) * 256] = y.astype(BF16)
        u_scr[2 * c] = y[:, :LANES]
        u_scr[2 * c + 1] = y[:, LANES:]
    dt_ref[...] = mm(3072, LANES)
    nchunk = u_scr.shape[1] // S5_Q
    for j in range(S5_Q):
        for t in range(u_scr.shape[0]):
            uj_ref[j, :, t * LANES:(t + 1) * LANES] = u_scr[t, pl.ds(j, nchunk, stride=S5_Q), :].astype(BF16)


def _ssm_inproj(rw, xall, mods, norm_g, w_in):
    D, tm = rw.D, rw.tm
    w = jnp.concatenate([w_in[:, 0:2560], w_in[:, 2592:3104], w_in[:, 2560:2592], jnp.zeros((D, LANES - 32), F32)],
                        axis=1).astype(BF16)
    row = lambda i: (i, 0)
    return pl.pallas_call(
        _ssm_inproj_kernel,
        grid=(rw.ntot,),
        in_specs=[pl.BlockSpec((tm, D), row), pl.BlockSpec((1, D), lambda i: (0, 0)),
                  _mod_spec(rw, 0), _mod_spec(rw, 1), pl.BlockSpec((D, 3200), lambda i: (0, 0))],
        out_specs=[pl.BlockSpec((tm, 1024), row), pl.BlockSpec((tm, 1536), row), pl.BlockSpec((tm, 512), row),
                   pl.BlockSpec((S5_Q, tm // S5_Q, 512), lambda i: (0, i, 0)), pl.BlockSpec((tm, LANES), row)],
        out_shape=[jax.ShapeDtypeStruct((rw.rows, 1024), BF16), jax.ShapeDtypeStruct((rw.rows, 1536), BF16),
                   jax.ShapeDtypeStruct((rw.rows, 512), BF16),
                   jax.ShapeDtypeStruct((S5_Q, rw.rows // S5_Q, 512), BF16),
                   jax.ShapeDtypeStruct((rw.rows, LANES), F32)],
        scratch_shapes=[pltpu.VMEM((tm, D), BF16), pltpu.VMEM((512 // LANES, tm, LANES), F32)],
        compiler_params=_cparams(1),
        name="ssm_inproj",
    )(xall, norm_g[None, :], mods, mods, w)


def _softplus(x):
    return jnp.maximum(x, 0.0) + jnp.log(1.0 + jnp.exp(-jnp.abs(x)))


def _conv_kernel(lat_tiles, tpb, cpb, x_ref, pv_ref, nx_ref, w_ref, b_ref, dtr_ref, dtb_ref, act_ref, dt_ref):
    i = pl.program_id(0)
    is_lat = i < lat_tiles
    pos = jnp.where(is_lat, i % tpb, (i - lat_tiles) % cpb)
    last_pos = jnp.where(is_lat, tpb - 1, cpb - 1)
    x = x_ref[...].astype(F32)
    tc = x.shape[0]
    prev_row = jnp.where(pos == 0, 0.0, pv_ref[...].astype(F32)[CONV_HALO - 1:CONV_HALO, :])
    next_row = jnp.where(pos == last_pos, 0.0, nx_ref[...].astype(F32)[0:1, :])
    row = lax.broadcasted_iota(jnp.int32, x.shape, 0)
    xm1 = jnp.where(row == 0, prev_row, pltpu.roll(x, 1, 0))
    xp1 = jnp.where(row == tc - 1, next_row, pltpu.roll(x, tc - 1, 0))
    y = w_ref[0:1, :] * xm1 + w_ref[1:2, :] * x + w_ref[2:3, :] * xp1 + b_ref[...]
    act_ref[...] = _silu(y).astype(BF16)
    sp = _softplus(dtr_ref[...] + dtb_ref[...])
    dt_ref[0] = sp
    dt_ref[1] = pltpu.roll(sp, LANES - 16, 1)


def _ssm_conv(rw, xbc, dtr, conv_w, conv_b, dt_bias):
    B, T, C = rw.B, rw.T, rw.C
    tc = CONV_TILE
    assert T % tc == 0 and C % tc == 0
    lat_tiles, tpb, cpb = (B * T) // tc, T // tc, C // tc
    ntiles = rw.rows // tc
    hpt = tc // CONV_HALO
    nhalo = rw.rows // CONV_HALO
    W = xbc.shape[1]
    dtb = jnp.concatenate([dt_bias.reshape(-1), jnp.zeros((LANES - 32,), F32)])[None, :]
    row = lambda i: (i, 0)
    return pl.pallas_call(
        functools.partial(_conv_kernel, lat_tiles, tpb, cpb),
        grid=(ntiles,),
        in_specs=[pl.BlockSpec((tc, W), row),
                  pl.BlockSpec((CONV_HALO, W), lambda i: (jnp.maximum(i * hpt - 1, 0), 0)),
                  pl.BlockSpec((CONV_HALO, W), lambda i: (jnp.minimum((i + 1) * hpt, nhalo - 1), 0)),
                  pl.BlockSpec((3, W), lambda i: (0, 0)), pl.BlockSpec((1, W), lambda i: (0, 0)),
                  pl.BlockSpec((tc, LANES), row), pl.BlockSpec((1, LANES), lambda i: (0, 0))],
        out_specs=[pl.BlockSpec((tc, W), row), pl.BlockSpec((2, tc, LANES), lambda i: (0, i, 0))],
        out_shape=[jax.ShapeDtypeStruct((rw.rows, W), BF16), jax.ShapeDtypeStruct((2, rw.rows, LANES), F32)],
        compiler_params=_cparams(1),
        name="ssm_conv",
    )(xbc, xbc, xbc, conv_w, conv_b[None, :], dtr, dtb)


def _ssd_kernel(actf_ref, actb_ref, dtf_ref, dtb_ref, tri_ref, a_ref, yf_ref, yb_ref, hst):
    @pl.when(pl.program_id(1) == 0)
    def _():
        hst[...] = jnp.zeros_like(hst)

    _ssd_chunk(actf_ref, dtf_ref, tri_ref[0], a_ref[0], yf_ref, hst.at[0])
    _ssd_chunk(actb_ref, dtb_ref, tri_ref[1], a_ref[1], yb_ref, hst.at[1])


def _ssd_chunk(act_ref, dt_ref, tri, avec, y_ref, hst):
    q = SSD_CHUNK
    dt = dt_ref[...]
    da = dt * avec
    acs = jnp.dot(tri, da, preferred_element_type=F32, precision=lax.Precision.HIGHEST)
    tot = jnp.sum(da, axis=0, keepdims=True)
    acs_t = acs.T
    dt_t = dt.T
    eacs = jnp.exp(acs)
    wend = jnp.exp(tot - acs) * dt
    etot = jnp.exp(tot)
    mask = tri > 0.5
    left = lax.broadcasted_iota(jnp.int32, (q, LANES), 1) < HEAD_DIM
    left1 = lax.broadcasted_iota(jnp.int32, (1, LANES), 1) < HEAD_DIM
    for g in range(2):
        bg = act_ref[:, 1024 + g * 128:1152 + g * 128]
        cg = act_ref[:, 1280 + g * 128:1408 + g * 128]
        cb = _nt(cg, bg)
        hin = hst[:, g * 512:(g + 1) * 512]
        yoff = jnp.dot(cg, hin.astype(BF16), preferred_element_type=F32)
        xw, dec = [], []
        for pr in range(4):
            h_a = g * 8 + pr * 2
            h_b = h_a + 1
            c0 = h_a * HEAD_DIM
            x2 = act_ref[:, c0:c0 + LANES]
            outs = []
            for h in (h_a, h_b):
                seg = acs[:, h:h + 1] - acs_t[h:h + 1, :]
                w = cb * jnp.exp(jnp.where(mask, seg, NEG_INF)) * dt_t[h:h + 1, :]
                outs.append(jnp.dot(w.astype(BF16), x2, preferred_element_type=F32))
            yd = jnp.where(left, outs[0], outs[1])
            sc = jnp.where(left, eacs[:, h_a:h_a + 1], eacs[:, h_b:h_b + 1])
            y_ref[:, c0:c0 + LANES] = (yd + yoff[:, pr * LANES:(pr + 1) * LANES] * sc).astype(BF16)
            wsc = jnp.where(left, wend[:, h_a:h_a + 1], wend[:, h_b:h_b + 1])
            xw.append((x2.astype(F32) * wsc).astype(BF16))
            dec.append(jnp.where(left1, etot[:, h_a:h_a + 1], etot[:, h_b:h_b + 1]))
        bg_t = bg.astype(F32).T.astype(BF16)
        snew = jnp.dot(bg_t, jnp.concatenate(xw, axis=1), preferred_element_type=F32)
        hst[:, g * 512:(g + 1) * 512] = hin * jnp.concatenate(dec, axis=1) + snew


def _ssd(rw, act, dt2, a_log):
    B, T, C = rw.B, rw.T, rw.C
    q = SSD_CHUNK
    nct, nlt = C // q, T // q
    ctx0 = (B * T) // q
    r = np.arange(q)
    tri = jnp.asarray(np.stack([r[None, :] <= r[:, None], r[None, :] >= r[:, None]]).astype(np.float32))
    avec = jnp.concatenate([-jnp.exp(a_log.astype(F32)), jnp.zeros((2, LANES - a_log.shape[1]), F32)], axis=1)[:, None, :]

    def blk(d, b, s):
        kc = s if d == 0 else nct - 1 - s
        kl = s - nct if d == 0 else nlt - 1 - (s - nct)
        return jnp.where(s < nct, ctx0 + b * nct + kc, b * nlt + kl)

    aspec = lambda d: pl.BlockSpec((q, act.shape[1]), lambda b, s: (blk(d, b, s), 0))
    dspec = lambda d: pl.BlockSpec((None, q, LANES), lambda b, s: (d, blk(d, b, s), 0))
    yspec = lambda d: pl.BlockSpec((q, 1024), lambda b, s: (blk(d, b, s), 0))
    return pl.pallas_call(
        _ssd_kernel,
        grid=(B, nct + nlt),
        in_specs=[aspec(0), aspec(1), dspec(0), dspec(1),
                  pl.BlockSpec((2, q, q), lambda b, s: (0, 0, 0)),
                  pl.BlockSpec((2, 1, LANES), lambda b, s: (0, 0, 0))],
        out_specs=[yspec(0), yspec(1)],
        out_shape=[jax.ShapeDtypeStruct((rw.rows, 1024), BF16)] * 2,
        scratch_shapes=[pltpu.VMEM((2, q, 1024), F32)],
        compiler_params=_cparams(2),
        name="ssd_scan",
    )(act, act, dt2, dt2, tri, avec)


def _cmul(ar, ai, br, bi):
    return ar * br - ai * bi, ar * bi + ai * br


def _s5_weight_kernel(lre_ref, lim_ref, ls_ref, bre_ref, bim_ref, cre_ref, cim_ref,
                      wsr_ref, wsi_ref, wor_ref, woi_ref, kt_ref, are_ref, aim_ref):
    lre, lim = lre_ref[...], lim_ref[...]
    step = jnp.exp(ls_ref[...])
    er, ei = lre * step, lim * step
    npow = 24
    p = lax.broadcasted_iota(jnp.int32, (1, npow, 1), 1).astype(F32)
    mag = jnp.exp(p * er)
    pre, pim = mag * jnp.cos(p * ei), mag * jnp.sin(p * ei)
    a_re, a_im = pre[:, 1:2, :], pim[:, 1:2, :]
    den = lre * lre + lim * lim
    q_re = ((a_re - 1.0) * lre + a_im * lim) / den
    q_im = (a_im * lre - (a_re - 1.0) * lim) / den
    bb_re, bb_im = _cmul(q_re, q_im, bre_ref[...], bim_ref[...])
    c_re, c_im = cre_ref[...], cim_ref[...]
    ws_r, ws_i, wo_r, wo_i, ca_r, ca_i = [], [], [], [], [], []
    for t in range(S5_Q):
        r, i = _cmul(bb_re, bb_im, pre[:, t:t + 1, :], pim[:, t:t + 1, :])
        ws_r.append(r)
        ws_i.append(i)
        r, i = _cmul(c_re, c_im, pre[:, t:t + 1, :], pim[:, t:t + 1, :])
        ca_r.append(r)
        ca_i.append(i)
        r, i = _cmul(c_re, c_im, pre[:, t + 1:t + 2, :], pim[:, t + 1:t + 2, :])
        wo_r.append(r)
        wo_i.append(-i)
    cat = lambda xs: jnp.concatenate(xs, axis=1)
    wsr_ref[...] = cat(ws_r)
    wsi_ref[...] = cat(ws_i)
    wor_ref[...] = cat(wo_r)
    woi_ref[...] = cat(wo_i)
    bdot = lambda a, b: lax.dot_general(a, b, (((2,), (2,)), ((0,), (0,))), preferred_element_type=F32,
                                        precision=lax.Precision.HIGHEST)
    kt_ref[...] = bdot(cat(ca_r), bb_re) - bdot(cat(ca_i), bb_im)
    are_ref[...] = pre[:, S5_Q:S5_Q + 1, :]
    aim_ref[...] = pim[:, S5_Q:S5_Q + 1, :]


def _s5_weights(lam_re, lam_im, log_step, b_re, b_im, c_re, c_im):
    nd, ng, ns = lam_re.shape
    G = nd * ng
    ch = S5_GROUP
    gb = 8
    qc = S5_Q * ch
    f = lambda a: a.astype(F32)
    args = (f(lam_re).reshape(G, 1, ns), f(lam_im).reshape(G, 1, ns), f(log_step).reshape(G, 1, 1),
            f(b_re).reshape(G, ns, ch).transpose(0, 2, 1), f(b_im).reshape(G, ns, ch).transpose(0, 2, 1),
            f(c_re).reshape(G, ch, ns), f(c_im).reshape(G, ch, ns))
    spec = lambda a: pl.BlockSpec((gb,) + a.shape[1:], lambda i: (i, 0, 0))
    oshape = [jax.ShapeDtypeStruct((G, qc, ns), F32)] * 4 + [jax.ShapeDtypeStruct((G, qc, ch), F32)] \
        + [jax.ShapeDtypeStruct((G, 1, ns), F32)] * 2
    wsr, wsi, wor, woi, kt, a_re, a_im = pl.pallas_call(
        _s5_weight_kernel,
        grid=(G // gb,),
        in_specs=[spec(a) for a in args],
        out_specs=[pl.BlockSpec((gb,) + s.shape[1:], lambda i: (i, 0, 0)) for s in oshape],
        out_shape=oshape,
        compiler_params=_cparams(1),
        name="s5_weights",
    )(*args)

    def by_dir(w, flip_dir):
        w = w.reshape(nd, ng, S5_Q, ch, ns)
        w = jnp.stack([jnp.flip(w[d], axis=1) if d == flip_dir else w[d] for d in range(nd)])
        return w.reshape(nd, ng, qc, ns)

    def pack(w):
        z = jnp.zeros_like(w)
        even = (np.arange(ng) % 2 == 0)[None, :, None, None]
        return jnp.where(even, jnp.concatenate([w, z], axis=-1), jnp.concatenate([z, w], axis=-1)).astype(BF16)

    ws_r, ws_i = pack(by_dir(wsr, 0)), pack(by_dir(wsi, 0))
    wo_r, wo_i = pack(by_dir(wor, 1)), pack(by_dir(woi, 1))
    k = kt.astype(BF16).reshape(nd, ng, S5_Q, ch, ch).transpose(0, 1, 4, 2, 3)
    kf = k.reshape(nd, ng, ch, qc)
    kb = jnp.flip(k, axis=3).reshape(nd, ng, ch, qc)
    rows_f, rows_b = [], []
    for j in range(S5_Q):
        z_f = jnp.zeros((ng, ch, j * ch), BF16)
        z_b = jnp.zeros((ng, ch, (S5_Q - 1 - j) * ch), BF16)
        rows_f.append(jnp.concatenate([z_f, kf[0, :, :, :(S5_Q - j) * ch]], axis=-1))
        rows_b.append(jnp.concatenate([kb[1, :, :, (S5_Q - 1 - j) * ch:], z_b], axis=-1))
    bt = jnp.stack([jnp.stack(rows_f, axis=1), jnp.stack(rows_b, axis=1)]).reshape(nd, ng, qc, qc)
    pair = lambda a: a.reshape(nd, ng // 2, 1, 2 * ns)
    return bt, ws_r, ws_i, wo_r, wo_i, pair(a_re), pair(a_im)


S5_GB = LANES // S5_GROUP


def _s5_kernel(B, nct, nlt, uj_ref, perm_ref, bt_ref, wsr_ref, wsi_ref, wor_ref, woi_ref, are_ref, aim_ref, yj_ref,
               x_scr, y_scr, s_re, s_im):
    gb, npair, qc = S5_GB, S5_GB // 2, S5_Q * S5_GROUP
    lhs = jnp.concatenate([uj_ref[j] for j in range(S5_Q)], axis=1)
    for m in range(gb):
        x_scr[:, m * qc:(m + 1) * qc] = jnp.dot(lhs, perm_ref[:, m * qc:(m + 1) * qc],
                                                preferred_element_type=F32).astype(BF16)
    xg = lambda g: x_scr[:, g * qc:(g + 1) * qc]
    for d in range(2):
        for pr in range(npair):
            for dst, w_ref in ((s_re, wsr_ref), (s_im, wsi_ref)):
                dst[d, pr] = (jnp.dot(xg(2 * pr), w_ref[d, 2 * pr], preferred_element_type=F32)
                              + jnp.dot(xg(2 * pr + 1), w_ref[d, 2 * pr + 1], preferred_element_type=F32))
    chains = [(d, pr, b) for d in range(2) for pr in range(npair) for b in range(B)]
    coef = {(d, pr): (are_ref[d, pr], aim_ref[d, pr]) for d in range(2) for pr in range(npair)}
    ctx0 = B * nlt

    def body(s, carry):
        in_ctx = s < nct
        rows = {}
        for d in range(2):
            kc = s if d == 0 else nct - 1 - s
            kl = s - nct if d == 0 else nlt - 1 - (s - nct)
            for b in range(B):
                rows[(d, b)] = pl.ds(jnp.where(in_ctx, ctx0 + b * nct + kc, b * nlt + kl), 1)
        contrib = [(s_re[d, pr, rows[(d, b)], :], s_im[d, pr, rows[(d, b)], :]) for d, pr, b in chains]
        new = []
        for (d, pr, b), (hr, hi), (sr, si) in zip(chains, carry, contrib):
            ar, ai = coef[(d, pr)]
            s_re[d, pr, rows[(d, b)], :] = hr
            s_im[d, pr, rows[(d, b)], :] = hi
            new.append((ar * hr - ai * hi + sr, ar * hi + ai * hr + si))
        return tuple(new)

    zero = jnp.zeros((1, LANES), F32)
    lax.fori_loop(0, nct + nlt, body, tuple((zero, zero) for _ in chains))
    for g in range(gb):
        acc = None
        for d in range(2):
            t = (jnp.dot(xg(g), bt_ref[d, g], preferred_element_type=F32)
                 + _nt(s_re[d, g // 2].astype(BF16), wor_ref[d, g])
                 + _nt(s_im[d, g // 2].astype(BF16), woi_ref[d, g]))
            acc = t if acc is None else acc + t
        y_scr[:, g * qc:(g + 1) * qc] = acc.astype(BF16)
    for i in range(S5_Q):
        yj_ref[i] = _nt(y_scr[...], perm_ref[i * LANES:(i + 1) * LANES, :]).astype(BF16)


def _s5(rw, uj, weights):
    B, T, C = rw.B, rw.T, rw.C
    bt, ws_r, ws_i, wo_r, wo_i, a_re, a_im = weights
    ng = bt.shape[1]
    q, gb = S5_Q, S5_GB
    nct, nlt = C // q, T // q
    nrow = uj.shape[1]
    qc = q * S5_GROUP
    k = gb * qc
    idx = np.arange(k)
    j, m, c = idx // LANES, (idx % LANES) // S5_GROUP, idx % S5_GROUP
    perm = np.zeros((k, k), np.float32)
    perm[idx, m * qc + j * S5_GROUP + c] = 1.0
    once = dict(pipeline_mode=pl.Buffered(1))
    wspec = lambda n: pl.BlockSpec((2, gb, qc, n), lambda i: (0, i, 0, 0), **once)
    aspec = pl.BlockSpec((2, gb // 2, 1, LANES), lambda i: (0, i, 0, 0))
    return pl.pallas_call(
        functools.partial(_s5_kernel, B, nct, nlt),
        grid=(ng // gb,),
        in_specs=[pl.BlockSpec((q, nrow, LANES), lambda i: (0, 0, i), **once),
                  pl.BlockSpec((k, k), lambda i: (0, 0), **once),
                  wspec(qc), wspec(LANES), wspec(LANES), wspec(LANES), wspec(LANES), aspec, aspec],
        out_specs=pl.BlockSpec((q, nrow, LANES), lambda i: (0, 0, i)),
        out_shape=jax.ShapeDtypeStruct(uj.shape, BF16),
        scratch_shapes=[pltpu.VMEM((nrow, k), BF16), pltpu.VMEM((nrow, k), BF16),
                        pltpu.VMEM((2, gb // 2, nrow, LANES), F32), pltpu.VMEM((2, gb // 2, nrow, LANES), F32)],
        compiler_params=_cparams(1),
        name="s5_scan",
    )(uj, jnp.asarray(perm, BF16), bt, ws_r, ws_i, wo_r, wo_i, a_re, a_im)


def _gelu_tanh(x):
    return 0.5 * x * (1.0 + jnp.tanh(math.sqrt(2.0 / math.pi) * (x + 0.044715 * (x * x * x))))


def _ssm_outproj_kernel(y0_ref, y1_ref, xs_ref, z_ref, v_ref, u_ref, dsk_ref, nw_ref, s5d_ref, gw_ref, gb_ref,
                        x_ref, w_ref, g1_ref, gn_ref, sh2_ref, sc2_ref, wr_ref, br_ref, lt_ref,
                        xo_ref, h2_ref, rt_ref, cnt_ref, carry, v_scr):
    i = pl.program_id(0)
    y = y0_ref[...].astype(F32) + y1_ref[...].astype(F32) + dsk_ref[...] * xs_ref[...].astype(F32)
    y = _rms(y * _silu(z_ref[...].astype(F32))) * nw_ref[...]
    ntile = v_scr.shape[0]
    nchunk = v_scr.shape[1] // S5_Q
    for j in range(S5_Q):
        for t in range(ntile):
            v_scr[t, pl.ds(j, nchunk, stride=S5_Q), :] = v_ref[j, :, t * LANES:(t + 1) * LANES].astype(F32)
    s5_y = jnp.concatenate([v_scr[t] for t in range(ntile)], axis=1)
    v = _gelu_tanh(s5_y + s5d_ref[...] * u_ref[...].astype(F32))
    v = v * _sigmoid(jnp.dot(v.astype(BF16), gw_ref[...], preferred_element_type=F32) + gb_ref[...])
    mix = jnp.concatenate([y, v], axis=1).astype(BF16)
    yo = jnp.dot(mix, w_ref[...], preferred_element_type=F32)
    _post_mixer(i, x_ref[...], yo, g1_ref[...], gn_ref[...], sh2_ref[...], sc2_ref[...], wr_ref, br_ref, lt_ref,
                xo_ref, h2_ref, rt_ref, cnt_ref, carry)


def _ssm_outproj(rw, ssd_y, act, z, s5_y, u, d_skip, norm_w, s5_d, glu_w, glu_b, xall, w_out, mods, norm_ffn, wr, br):
    D, tm = rw.D, rw.tm
    ntiles = rw.nlat
    post_in, post_out = _post_specs(rw)
    row = lambda i: (i, 0)
    vec = lambda n: pl.BlockSpec((1, n), lambda i: (0, 0))
    dsk = jnp.repeat(d_skip.astype(F32), HEAD_DIM)[None, :]
    return pl.pallas_call(
        _ssm_outproj_kernel,
        grid=(ntiles,),
        in_specs=[pl.BlockSpec((tm, 1024), row), pl.BlockSpec((tm, 1024), row),
                  pl.BlockSpec((tm, 1024), row), pl.BlockSpec((tm, 1024), row),
                  pl.BlockSpec((S5_Q, tm // S5_Q, 512), lambda i: (0, i, 0)),
                  pl.BlockSpec((tm, 512), row), vec(1024), vec(1024), vec(512),
                  pl.BlockSpec((512, 512), lambda i: (0, 0)), vec(512),
                  pl.BlockSpec((tm, D), row), pl.BlockSpec((1536, D), lambda i: (0, 0)), _mod_spec(rw, 2)] + post_in,
        out_specs=post_out,
        out_shape=_post_shapes(ntiles * tm, D),
        scratch_shapes=[pltpu.VMEM((1, LANES), F32), pltpu.VMEM((512 // LANES, tm, LANES), F32)],
        compiler_params=_cparams(1),
        name="ssm_outproj_router",
    )(ssd_y[0], ssd_y[1], act, z, s5_y, u, dsk, norm_w[None, :], s5_d[None, :], glu_w.astype(BF16), glu_b[None, :],
      xall, w_out.astype(BF16), mods, norm_ffn[None, :], mods, mods, wr, br, _lower_tri(tm))


def kernel(x, c, ctx, c_ctx, mod_w, mod_b, norm_mix, norm_ffn, att_w_in, att_w_out, na_q_norm, na_k_norm, na_rel_bias, wa_q_norm, wa_k_norm, wa_sink, ssm_w_in, ssm_w_out, ssd_conv_w, ssd_conv_b, ssd_dt_bias, ssd_a_log, ssd_d, ssd_norm, s5_lambda_re, s5_lambda_im, s5_log_step, s5_b_re, s5_b_im, s5_c_re, s5_c_im, s5_d, s5_glu_w, s5_glu_b, moe_w_group, moe_b_group, moe_w_expert, moe_b_expert, moe_w13, moe_w2):
    B, T, D = x.shape
    C = ctx.shape[1]
    rw = _Rows(B, T, C, D, ROW_TILE)
    xl = x.reshape(B * T, D)
    xc = ctx.reshape(B * C, D)
    cm = jnp.concatenate([c, c_ctx[None, :], jnp.zeros((8 - B - 1, D), F32)], axis=0)
    mods = _modulation(cm, mod_w, mod_b)
    mods = mods.reshape(mods.shape[0], 8, 1, 6 * D)

    m0 = mods[0]
    qkv = _att_inproj(rw, xl, xc, m0, norm_mix[0], att_w_in[0], na_q_norm[0], na_k_norm[0], wa_q_norm[0],
                      wa_k_norm[0])
    na = _na_attention(rw, qkv, na_rel_bias[0])
    wa = _wa_attention(rw, qkv, wa_sink[0])
    cx = _ctx_attention(rw, qkv, wa_sink[0])
    wr, br = _router_weights(moe_w_group[0], moe_b_group[0], moe_w_expert[0], moe_b_expert[0])
    xall, h2, route, counts = _att_outproj(rw, na, wa, cx, xl, xc, att_w_out[0], m0, norm_ffn[0], wr, br)
    y1, y2 = _moe(h2, route, counts, moe_w13, moe_w2, 0)
    xall = _combine(rw, rw.ntot, xall, y1, y2, route, m0)

    m1 = mods[1]
    z, xbc, u, uj, dtr = _ssm_inproj(rw, xall, m1, norm_mix[1], ssm_w_in[0])
    act, dt2 = _ssm_conv(rw, xbc, dtr, ssd_conv_w[0], ssd_conv_b[0], ssd_dt_bias[0])
    ssd_y = _ssd(rw, act, dt2, ssd_a_log[0])
    s5_w = _s5_weights(s5_lambda_re[0], s5_lambda_im[0], s5_log_step[0], s5_b_re[0], s5_b_im[0], s5_c_re[0],
                       s5_c_im[0])
    s5_y = _s5(rw, uj, s5_w)
    wr, br = _router_weights(moe_w_group[1], moe_b_group[1], moe_w_expert[1], moe_b_expert[1])
    xlat, h2, route, counts = _ssm_outproj(rw, ssd_y, act, z, s5_y, u, ssd_d[0], ssd_norm[0], s5_d[0], s5_glu_w[0],
                                           s5_glu_b[0], xall, ssm_w_out[0], m1, norm_ffn[1], wr, br)
    y1, y2 = _moe(h2, route, counts, moe_w13, moe_w2, 1)
    out = _combine(rw, rw.nlat, xlat, y1, y2, route, m1)
    return out.reshape(B, T, D)
```

```python
import functools
import math

import jax
import jax.numpy as jnp
import numpy as np
from jax import lax
from jax.experimental import pallas as pl
from jax.experimental.pallas import tpu as pltpu

F32 = jnp.float32
BF16 = jnp.bfloat16

EPS = 1e-6
NEG_INF = -1e30
GRID_W = 64
HEAD_DIM = 64
NA_KH = 8
NA_KW = 16
WA_BLOCK = 128
ROPE_BASE = 10000.0
SSD_CHUNK = 128
S5_GROUP = 16
S5_STATE = 64
MOE_GROUPS = 4
MOE_EPG = 8
MOE_EXPERTS = MOE_GROUPS * MOE_EPG

LANES = 128
ROW_TILE = 512
MOE_TILE = 256
VMEM_LIMIT = 56 * 1024 * 1024
MOE_VMEM_LIMIT = 60 * 1024 * 1024


def _cparams(n_axes, vmem=VMEM_LIMIT):
    return pltpu.CompilerParams(dimension_semantics=("arbitrary",) * n_axes, vmem_limit_bytes=vmem)


def _sigmoid(x):
    return 1.0 / (1.0 + jnp.exp(-x))


def _silu(x):
    return x * _sigmoid(x)


def _rms(x, eps=EPS):
    return x * lax.rsqrt(jnp.mean(x * x, axis=-1, keepdims=True) + eps)


def _ada_norm(x, g, shift, scale):
    return (_rms(x) * g) * (1.0 + scale) + shift


def _mod_kernel(c_ref, w_ref, b_ref, o_ref):
    a = _silu(c_ref[...])
    o_ref[...] = jnp.dot(a, w_ref[...], preferred_element_type=F32, precision=lax.Precision.HIGHEST) + b_ref[...]


def _modulation(cm, mod_w, mod_b):
    depth, d, n6 = mod_w.shape
    tn = 1024
    return pl.pallas_call(
        _mod_kernel,
        grid=(depth, n6 // tn),
        in_specs=[pl.BlockSpec((8, d), lambda l, j: (0, 0)),
                  pl.BlockSpec((None, d, tn), lambda l, j: (l, 0, j)),
                  pl.BlockSpec((None, 1, tn), lambda l, j: (l, 0, j))],
        out_specs=pl.BlockSpec((None, 8, tn), lambda l, j: (l, 0, j)),
        out_shape=jax.ShapeDtypeStruct((depth, 8, n6), F32),
        compiler_params=_cparams(2),
        name="modulation",
    )(cm, mod_w, mod_b.reshape(depth, 1, n6))


class _Rows:
    def __init__(self, B, T, C, D, tm):
        assert T % tm == 0 and (B * C) % tm == 0
        self.B, self.T, self.C, self.D, self.tm = B, T, C, D, tm
        self.tpb = T // tm
        self.nlat = B * self.tpb
        self.nctx = (B * C) // tm
        self.ntot = self.nlat + self.nctx
        self.rows = B * (T + C)

    def group(self, i):
        return jnp.where(i < self.nlat, i // self.tpb, self.B)


def _mod_spec(rw, col):
    return pl.BlockSpec((None, 1, rw.D), lambda i, *_: (rw.group(i), 0, col))


def _seg_norm(y, seg, gcol):
    ss = jnp.dot((y * y).astype(BF16), seg, preferred_element_type=F32)
    return y * lax.rsqrt(ss + EPS) * gcol


def _rope(y, cos, sin):
    w = y.shape[-1]
    lane = lax.broadcasted_iota(jnp.int32, y.shape, 1)
    first = (lane % 32) < 16
    partner = jnp.where(first, pltpu.roll(y, w - 16, 1), pltpu.roll(y, 16, 1))
    return y * cos + partner * sin


def _dup_halves(k):
    lane = lax.broadcasted_iota(jnp.int32, k.shape, 1)
    sw = pltpu.roll(k, 64, 1)
    return jnp.where(lane < 64, k, sw), jnp.where(lane < 64, sw, k)


def _att_inproj_kernel(nlat, xl_ref, xc_ref, g_ref, sh_ref, sc_ref, w_ref, gcol_ref, cos_ref, sin_ref, seg_ref,
                       o_ref, h_scr):
    i = pl.program_id(0)
    x = jnp.where(i < nlat, xl_ref[...], xc_ref[...])
    h_scr[...] = _ada_norm(x, g_ref[...], sh_ref[...], sc_ref[...]).astype(BF16)
    seg = seg_ref[...]
    cos2 = jnp.concatenate([cos_ref[...], cos_ref[...]], axis=1)
    sin2 = jnp.concatenate([sin_ref[...], sin_ref[...]], axis=1)
    for c in range(9):
        c0 = c * 256
        y = jnp.dot(h_scr[...], w_ref[:, c0:c0 + 256], preferred_element_type=F32)
        gcol = gcol_ref[:, c0:c0 + 256]
        if c in (0, 1, 2, 3):
            o_ref[:, c0:c0 + 256] = _seg_norm(y, seg, gcol).astype(BF16)
        elif c in (4, 5):
            o_ref[:, c0:c0 + 256] = y.astype(BF16)
        elif c in (6, 7):
            o_ref[:, c0:c0 + 256] = _rope(_seg_norm(y, seg, gcol), cos2, sin2).astype(BF16)
        else:
            lane = lax.broadcasted_iota(jnp.int32, y.shape, 1)
            yk = jnp.where(lane < 128, _seg_norm(y, seg, gcol), y)
            yr = jnp.where(lane < 128, _rope(yk, cos2, sin2), yk)
            k0, k1 = _dup_halves(yr[:, :128])
            v0, v1 = _dup_halves(yr[:, 128:])
            o_ref[:, 2048:2176] = k0.astype(BF16)
            o_ref[:, 2176:2304] = k1.astype(BF16)
            o_ref[:, 2304:2432] = v0.astype(BF16)
            o_ref[:, 2432:2560] = v1.astype(BF16)


def _rope_tables(T, tm):
    t = np.arange(T)
    d = np.arange(HEAD_DIM)
    nf = HEAD_DIM // 4
    inv = jnp.asarray(ROPE_BASE, F32) ** (-jnp.arange(nf, dtype=F32) / nf)
    pos = np.where((d // 32 == 0)[None, :], (t // GRID_W)[:, None], (t % GRID_W)[:, None])
    ang = jnp.asarray(pos, F32) * inv[d % nf][None, :]
    sign = np.where((d % 32) < 16, -1.0, 1.0).astype(np.float32)
    cos = jnp.cos(ang)
    sin = jnp.sin(ang) * sign[None, :]
    cos = jnp.concatenate([cos, jnp.ones((tm, HEAD_DIM), F32)], axis=0)
    sin = jnp.concatenate([sin, jnp.zeros((tm, HEAD_DIM), F32)], axis=0)
    return jnp.tile(cos, (1, 2)), jnp.tile(sin, (1, 2))


def _att_inproj(rw, xl, xc, mods, norm_g, w_in, na_qn, na_kn, wa_qn, wa_kn):
    D, tm = rw.D, rw.tm
    scale = HEAD_DIM ** -0.5
    gcol = jnp.concatenate([jnp.tile(na_qn * scale, 8), jnp.tile(na_kn, 8), jnp.ones((512,), F32),
                            jnp.tile(wa_qn * scale, 8), jnp.tile(wa_kn, 2), jnp.ones((128,), F32)])[None, :]
    cos, sin = _rope_tables(rw.T, tm)
    segn = np.arange(256) // 64
    seg = jnp.asarray((segn[:, None] == segn[None, :]).astype(np.float32) / 64.0, BF16)
    nlat, tpb = rw.nlat, rw.tpb
    return pl.pallas_call(
        functools.partial(_att_inproj_kernel, nlat),
        grid=(rw.ntot,),
        in_specs=[pl.BlockSpec((tm, D), lambda i: (jnp.minimum(i, nlat - 1), 0)),
                  pl.BlockSpec((tm, D), lambda i: (jnp.maximum(i - nlat, 0), 0)),
                  pl.BlockSpec((1, D), lambda i: (0, 0)),
                  _mod_spec(rw, 0), _mod_spec(rw, 1),
                  pl.BlockSpec((D, 2304), lambda i: (0, 0)),
                  pl.BlockSpec((1, 2304), lambda i: (0, 0)),
                  pl.BlockSpec((tm, 128), lambda i: (jnp.where(i < nlat, i % tpb, tpb), 0)),
                  pl.BlockSpec((tm, 128), lambda i: (jnp.where(i < nlat, i % tpb, tpb), 0)),
                  pl.BlockSpec((256, 256), lambda i: (0, 0))],
        out_specs=pl.BlockSpec((tm, 2560), lambda i: (i, 0)),
        out_shape=jax.ShapeDtypeStruct((rw.rows, 2560), BF16),
        scratch_shapes=[pltpu.VMEM((tm, D), BF16)],
        compiler_params=_cparams(1),
        name="att_inproj",
    )(xl, xc, norm_g[None, :], mods, mods, w_in.astype(BF16), gcol, cos, sin, seg)


def _route(lg, lt, carry):
    lane = lax.broadcasted_iota(jnp.int32, lg.shape, 1).astype(F32)
    gm = lane < MOE_GROUPS
    mg = jnp.max(jnp.where(gm, lg, NEG_INF), axis=-1, keepdims=True)
    eg = jnp.where(gm, jnp.exp(jnp.where(gm, lg, NEG_INF) - mg), 0.0)
    pg = eg / jnp.sum(eg, axis=-1, keepdims=True)
    ptop = jnp.max(pg, axis=-1, keepdims=True)
    gidx = jnp.min(jnp.where(gm & (pg == ptop), lane, 1e9), axis=-1, keepdims=True)
    lo = MOE_GROUPS + MOE_EPG * gidx
    em = (lane >= lo) & (lane < lo + MOE_EPG)
    le = jnp.where(em, lg, NEG_INF)
    ee = jnp.where(em, jnp.exp(le - jnp.max(le, axis=-1, keepdims=True)), 0.0)
    pe = ee / jnp.sum(ee, axis=-1, keepdims=True)
    v1 = jnp.max(jnp.where(em, pe, -1.0), axis=-1, keepdims=True)
    i1 = jnp.min(jnp.where(em & (pe == v1), lane, 1e9), axis=-1, keepdims=True)
    em2 = em & (lane != i1)
    v2 = jnp.max(jnp.where(em2, pe, -1.0), axis=-1, keepdims=True)
    i2 = jnp.min(jnp.where(em2 & (pe == v2), lane, 1e9), axis=-1, keepdims=True)
    den = v1 + v2
    w1 = v1 / den * ptop
    w2 = v2 / den * ptop
    e1 = i1 - MOE_GROUPS
    e2 = i2 - MOE_GROUPS
    m1 = lane == e1
    m2 = lane == e2
    oh = jnp.where(m1 | m2, 1.0, 0.0)
    cnt = jnp.dot(lt, oh.astype(BF16), preferred_element_type=F32) + carry
    r1 = jnp.sum(jnp.where(m1, cnt, 0.0), axis=-1, keepdims=True)
    r2 = jnp.sum(jnp.where(m2, cnt, 0.0), axis=-1, keepdims=True)
    route = jnp.where(lane == 0, e1, jnp.where(lane == 1, e2, jnp.where(lane == 2, w1, jnp.where(
        lane == 3, w2, jnp.where(lane == 4, r1, jnp.where(lane == 5, r2, 0.0))))))
    return route, carry + jnp.sum(oh, axis=0, keepdims=True)


def _post_mixer(i, x, y, g1, gn, sh2, sc2, wr_ref, br_ref, lt_ref, xo_ref, h2_ref, rt_ref, cnt_ref, carry):
    xn = x + g1 * y
    xo_ref[...] = xn
    h2 = _ada_norm(xn, gn, sh2, sc2)
    hb = h2.astype(BF16)
    hbf = hb.astype(F32)
    half = h2.shape[1] // 2
    h2_ref[...] = pltpu.pack_elementwise([h2[:, :half], h2[:, half:]], packed_dtype=BF16)
    hl = (h2 - hbf).astype(BF16)
    lg = (jnp.dot(hb, wr_ref[0], preferred_element_type=F32)
          + (jnp.dot(hb, wr_ref[1], preferred_element_type=F32) + jnp.dot(hl, wr_ref[0], preferred_element_type=F32))
          + br_ref[...])

    @pl.when(i == 0)
    def _():
        carry[...] = jnp.zeros_like(carry)

    route, newc = _route(lg, lt_ref[...], carry[...])
    rt_ref[...] = route
    carry[...] = newc
    cnt_ref[...] = newc


def _att_outproj_kernel(nlat, na_ref, wa_ref, cx_ref, xl_ref, xc_ref, w_ref, g1_ref, gn_ref, sh2_ref, sc2_ref,
                        wr_ref, br_ref, lt_ref, xo_ref, h2_ref, rt_ref, cnt_ref, carry):
    i = pl.program_id(0)
    lat = i < nlat
    mix = jnp.where(lat, jnp.concatenate([na_ref[...], wa_ref[...]], axis=1), cx_ref[...])
    y = jnp.dot(mix, w_ref[...], preferred_element_type=F32)
    x = jnp.where(lat, xl_ref[...], xc_ref[...])
    _post_mixer(i, x, y, g1_ref[...], gn_ref[...], sh2_ref[...], sc2_ref[...], wr_ref, br_ref, lt_ref,
                xo_ref, h2_ref, rt_ref, cnt_ref, carry)


def _router_weights(w_group, b_group, w_expert, b_expert):
    D = w_group.shape[0]
    pad = LANES - MOE_GROUPS - MOE_EXPERTS
    wr = jnp.concatenate([w_group, w_expert, jnp.zeros((D, pad), F32)], axis=1)
    br = jnp.concatenate([b_group, b_expert, jnp.zeros((pad,), F32)])[None, :]
    hi = wr.astype(BF16)
    lo = (wr - hi.astype(F32)).astype(BF16)
    return jnp.stack([hi, lo]), br


def _lower_tri(tm):
    r = np.arange(tm)
    return jnp.asarray((r[None, :] < r[:, None]).astype(np.float32), BF16)


def _post_specs(rw):
    D, tm = rw.D, rw.tm
    return ([pl.BlockSpec((1, D), lambda i: (0, 0)), _mod_spec(rw, 3), _mod_spec(rw, 4),
             pl.BlockSpec((2, D, LANES), lambda i: (0, 0, 0)), pl.BlockSpec((1, LANES), lambda i: (0, 0)),
             pl.BlockSpec((tm, tm), lambda i: (0, 0))],
            [pl.BlockSpec((tm, D), lambda i: (i, 0)), pl.BlockSpec((tm, D // 2), lambda i: (i, 0)),
             pl.BlockSpec((tm, LANES), lambda i: (i, 0)), pl.BlockSpec((1, LANES), lambda i: (0, 0))])


def _post_shapes(nrows, D):
    return [jax.ShapeDtypeStruct((nrows, D), F32), jax.ShapeDtypeStruct((nrows, D // 2), jnp.uint32),
            jax.ShapeDtypeStruct((nrows, LANES), F32), jax.ShapeDtypeStruct((1, LANES), F32)]


def _att_outproj(rw, na, wa, cx, xl, xc, w_out, mods, norm_ffn, wr, br):
    D, tm, nlat = rw.D, rw.tm, rw.nlat
    post_in, post_out = _post_specs(rw)
    latmap = lambda i: (jnp.minimum(i, nlat - 1), 0)
    ctxmap = lambda i: (jnp.maximum(i - nlat, 0), 0)
    return pl.pallas_call(
        functools.partial(_att_outproj_kernel, nlat),
        grid=(rw.ntot,),
        in_specs=[pl.BlockSpec((tm, 512), latmap), pl.BlockSpec((tm, 512), latmap), pl.BlockSpec((tm, D), ctxmap),
                  pl.BlockSpec((tm, D), latmap), pl.BlockSpec((tm, D), ctxmap),
                  pl.BlockSpec((D, D), lambda i: (0, 0)), _mod_spec(rw, 2)] + post_in,
        out_specs=post_out,
        out_shape=_post_shapes(rw.rows, D),
        scratch_shapes=[pltpu.VMEM((1, LANES), F32)],
        compiler_params=_cparams(1),
        name="att_outproj_router",
    )(na, wa, cx, xl, xc, w_out.astype(BF16), mods, norm_ffn[None, :], mods, mods, wr, br, _lower_tri(tm))


def _moe_kernel(te_ref, nu_ref, src_ref, nsrc_ref, hp_ref, w13_ref, w2_ref, o_ref, w13b, w2b, xa, xb):
    i = pl.program_id(0)
    prev = te_ref[jnp.maximum(i - 1, 0)]
    changed = (i == 0) | (te_ref[i] != prev)
    tg = xa.shape[0]

    @pl.when(changed)
    def _():
        w13b[...] = w13_ref[...].astype(BF16)
        w2b[...] = w2_ref[...].astype(BF16)

    @pl.when(i == 0)
    def _():
        def fetch(j, carry):
            xa[pl.ds(j, 1), :] = hp_ref[pl.ds(src_ref[0, j], 1), :]
            return carry

        lax.fori_loop(0, tg, fetch, 0, unroll=8)

    def step(cur, nxt):
        for j in range(tg):
            nxt[pl.ds(j, 1), :] = hp_ref[pl.ds(nsrc_ref[0, j], 1), :]
        ff = w2b.shape[0]
        half = cur.shape[1]
        w = cur[...]
        unpack = functools.partial(pltpu.unpack_elementwise, packed_dtype=BF16, unpacked_dtype=F32)
        x_lo = unpack(w, index=0).astype(BF16)
        x_hi = unpack(w, index=1).astype(BF16)
        a13 = (jnp.dot(x_lo, w13b[:half, :], preferred_element_type=F32)
               + jnp.dot(x_hi, w13b[half:, :], preferred_element_type=F32))
        act = _silu(a13[:, :ff]) * a13[:, ff:]
        o_ref[...] = jnp.dot(act.astype(BF16), w2b[...], preferred_element_type=F32).astype(BF16)

    used = i < nu_ref[0]

    @pl.when(used & (i % 2 == 0))
    def _():
        step(xa, xb)

    @pl.when(used & (i % 2 == 1))
    def _():
        step(xb, xa)

    @pl.when(i >= nu_ref[0])
    def _():
        o_ref[...] = jnp.zeros_like(o_ref)


def _moe(h2p, route, counts, w13, w2, layer):
    N = h2p.shape[0]
    D = 2 * h2p.shape[1]
    _, E, _, F2 = w13.shape
    tg = MOE_TILE
    nt = (2 * N) // tg + E
    e = route[:, 0:2].astype(jnp.int32)
    rank = route[:, 4:6].astype(jnp.int32)
    cnt = counts[0, :E].astype(jnp.int32)
    ntile_e = (cnt + tg - 1) // tg
    tile_end = jnp.cumsum(ntile_e)
    offs = (tile_end - ntile_e) * tg
    onehot = (e[:, :, None] == jnp.arange(E, dtype=jnp.int32)).astype(jnp.int32)
    dest = jnp.sum(onehot * offs, axis=-1) + rank
    src = jnp.zeros((nt * tg,), jnp.int32).at[dest.reshape(-1)].set(jnp.repeat(jnp.arange(N, dtype=jnp.int32), 2))
    tile_id = jnp.arange(nt, dtype=jnp.int32)
    nu = tile_end[-1:].astype(jnp.int32)
    te = jnp.sum((tile_end[None, :] <= jnp.minimum(tile_id, nu[0] - 1)[:, None]).astype(jnp.int32), axis=1)
    te = jnp.minimum(te, E - 1)
    ys = pl.pallas_call(
        _moe_kernel,
        grid_spec=pltpu.PrefetchScalarGridSpec(
            num_scalar_prefetch=2,
            grid=(nt,),
            in_specs=[pl.BlockSpec((None, 1, tg), lambda i, te, nu: (i, 0, 0), memory_space=pltpu.SMEM),
                      pl.BlockSpec((None, 1, tg), lambda i, te, nu: (jnp.minimum(i + 1, nt - 1), 0, 0),
                                   memory_space=pltpu.SMEM),
                      pl.BlockSpec((N, D // 2), lambda i, te, nu: (0, 0), pipeline_mode=pl.Buffered(1)),
                      pl.BlockSpec((None, None, D, F2), lambda i, te, nu: (layer, te[i], 0, 0)),
                      pl.BlockSpec((None, None, F2 // 2, D), lambda i, te, nu: (layer, te[i], 0, 0))],
            out_specs=pl.BlockSpec((tg, D), lambda i, te, nu: (i, 0)),
            scratch_shapes=[pltpu.VMEM((D, F2), BF16), pltpu.VMEM((F2 // 2, D), BF16),
                            pltpu.VMEM((tg, D // 2), jnp.uint32), pltpu.VMEM((tg, D // 2), jnp.uint32)]),
        out_shape=jax.ShapeDtypeStruct((nt * tg, D), BF16),
        compiler_params=_cparams(1, vmem=MOE_VMEM_LIMIT),
        name="moe_experts",
    )(te, nu, src.reshape(nt, 1, tg), src.reshape(nt, 1, tg), h2p, w13, w2)
    pick = lambda k: ys.at[dest[:, k]].get(mode="promise_in_bounds")
    return pick(0), pick(1)


def _combine_kernel(x_ref, y1_ref, y2_ref, rt_ref, g2_ref, o_ref):
    rt = rt_ref[...]
    f = rt[:, 2:3] * y1_ref[...].astype(F32) + rt[:, 3:4] * y2_ref[...].astype(F32)
    o_ref[...] = x_ref[...] + g2_ref[...] * f


def _combine(rw, ntiles, xall, y1, y2, route, mods):
    D, tm = rw.D, rw.tm
    row = lambda i: (i, 0)
    return pl.pallas_call(
        _combine_kernel,
        grid=(ntiles,),
        in_specs=[pl.BlockSpec((tm, D), row), pl.BlockSpec((tm, D), row), pl.BlockSpec((tm, D), row),
                  pl.BlockSpec((tm, LANES), row), _mod_spec(rw, 5)],
        out_specs=pl.BlockSpec((tm, D), row),
        out_shape=jax.ShapeDtypeStruct((ntiles * tm, D), F32),
        compiler_params=_cparams(1),
        name="moe_combine",
    )(xall, y1, y2, route, mods)


NA_QROWS = 8
NA_KROWS = 16


def _na_bias(rpb, n_rb):
    H = rpb.shape[0]
    i = np.arange(GRID_W)
    c0 = np.clip(i - NA_KW // 2, 0, GRID_W - NA_KW)
    j = np.arange(GRID_W)
    colvalid = (j[None, :] >= c0[:, None]) & (j[None, :] < c0[:, None] + NA_KW)
    dc = np.clip(j[None, :] - i[:, None] + NA_KW - 1, 0, 2 * NA_KW - 2)
    onehot = ((dc[None] == np.arange(2 * NA_KW - 1)[:, None, None]) & colvalid[None]).astype(np.float32)
    tiles = jnp.einsum('hrc,cij->hrij', rpb.astype(F32), jnp.asarray(onehot), precision=lax.Precision.HIGHEST)
    tiles = tiles + jnp.asarray(np.where(colvalid, 0.0, NEG_INF).astype(np.float32))
    flat = tiles.transpose(0, 2, 1, 3).reshape(H, GRID_W, (2 * NA_KH - 1) * GRID_W)
    blocks = []
    for variant in range(3):
        for a in range(NA_QROWS):
            start = (max(a - 4, 0) + 4, a, min(a, 4))[variant]
            dr0 = start - a + 3
            neg = lambda n: jnp.full((H, GRID_W, n * GRID_W), NEG_INF, F32)
            blocks.append(jnp.concatenate([neg(start), flat[:, :, dr0 * GRID_W:(dr0 + NA_KH) * GRID_W],
                                           neg(NA_KROWS - NA_KH - start)], axis=-1))
    return jnp.stack(blocks, axis=1).reshape(H, 3, NA_QROWS * GRID_W, NA_KROWS * GRID_W)


def _softmax_pv(parts, extra=None, rc=64):
    m_rows = parts[0][0].shape[0]
    probs = [[] for _ in parts]
    inv_l = []
    for r0 in range(0, m_rows, rc):
        sc = []
        for s, _, bias_fn, valid_fn in parts:
            c = s[r0:r0 + rc]
            if bias_fn is not None:
                c = c + bias_fn(r0, rc)
            if valid_fn is not None:
                c = jnp.where(valid_fn(r0, rc), c, NEG_INF)
            sc.append(c)
        mx = functools.reduce(jnp.maximum, [jnp.max(c, axis=-1, keepdims=True) for c in sc])
        if extra is not None:
            mx = jnp.maximum(mx, extra[r0:r0 + rc])
        l = jnp.zeros_like(mx) if extra is None else jnp.exp(extra[r0:r0 + rc] - mx)
        for k, c in enumerate(sc):
            p = jnp.exp(c - mx)
            l = l + jnp.sum(p, axis=-1, keepdims=True)
            probs[k].append(p.astype(BF16))
        inv_l.append(1.0 / l)
    o = None
    for k, (_, v, _, _) in enumerate(parts):
        pv = jnp.dot(jnp.concatenate(probs[k], axis=0), v, preferred_element_type=F32)
        o = pv if o is None else o + pv
    return o * jnp.concatenate(inv_l, axis=0)


def _nt(a, b):
    return lax.dot_general(a, b, (((1,), (1,)), ((), ())), preferred_element_type=F32)


def _na_kernel(q_ref, k0, k1, k2, k3, v0, v1, v2, v3, kc_ref, vc_ref, bias_ref, o_ref):
    q2 = q_ref[...]
    kw = jnp.concatenate([k0[...], k1[...], k2[...], k3[...]], axis=0)
    vw = jnp.concatenate([v0[...], v1[...], v2[...], v3[...]], axis=0)
    kc = kc_ref[...]
    vc = vc_ref[...]
    lane = lax.broadcasted_iota(jnp.int32, q2.shape, 1)
    out = jnp.zeros(q2.shape, F32)
    for hh in range(2):
        m = (lane < HEAD_DIM) if hh == 0 else (lane >= HEAD_DIM)
        qm = jnp.where(m, q2, jnp.zeros_like(q2))
        o = _softmax_pv([(_nt(qm, kw), vw, lambda r0, rc, hh=hh: bias_ref[hh, r0:r0 + rc, :], None),
                         (_nt(qm, kc), vc, None, None)], rc=32)
        out = jnp.where(m, o, out)
    o_ref[...] = out.astype(BF16)


def _na_attention(rw, qkv, rpb):
    B, T, C = rw.B, rw.T, rw.C
    tq = NA_QROWS * GRID_W
    tk = tq // 2
    n_rb = T // tq
    nkb = T // tk
    assert T % tq == 0 and n_rb >= 2 and (B * T) % C == 0
    bias = _na_bias(rpb, n_rb)
    ctxrow = (B * T) // C

    def kvspec(j, col):
        return pl.BlockSpec((tk, LANES), lambda p, rb, b: (b * nkb + jnp.clip(2 * rb - 1 + j, 0, nkb - 1), col + p))

    return pl.pallas_call(
        _na_kernel,
        grid=(4, n_rb, B),
        in_specs=[pl.BlockSpec((tq, LANES), lambda p, rb, b: (b * n_rb + rb, p))]
        + [kvspec(j, 4) for j in range(4)] + [kvspec(j, 8) for j in range(4)]
        + [pl.BlockSpec((C, LANES), lambda p, rb, b: (ctxrow + b, 4 + p)),
           pl.BlockSpec((C, LANES), lambda p, rb, b: (ctxrow + b, 8 + p)),
           pl.BlockSpec((2, None, tq, 2 * tq),
                        lambda p, rb, b: (p, jnp.where(rb == 0, 0, jnp.where(rb == n_rb - 1, 2, 1)), 0, 0))],
        out_specs=pl.BlockSpec((tq, LANES), lambda p, rb, b: (b * n_rb + rb, p)),
        out_shape=jax.ShapeDtypeStruct((B * T, 4 * LANES), BF16),
        compiler_params=_cparams(3),
        name="neighbourhood_attention",
    )(qkv, *([qkv] * 10), bias)


def _wa_kernel(nb, sink_ref, q_ref, kp, kc_, kn, vp, vc_, vn, kx_ref, vx_ref, o_ref):
    n = pl.program_id(1)
    blk = q_ref.shape[0]
    lane = lax.broadcasted_iota(jnp.int32, (blk, LANES), 1)
    zero = jnp.zeros((blk, LANES), BF16)
    qi = lax.broadcasted_iota(jnp.int32, (blk, 3 * blk), 0)
    ks = lax.broadcasted_iota(jnp.int32, (blk, 3 * blk), 1)
    lo = jnp.where(n > 0, 0, blk)
    hi = jnp.where(n < nb - 1, 3 * blk, 2 * blk)
    valid = (ks >= qi) & (ks <= qi + 2 * blk) & (ks >= lo) & (ks < hi)
    band_mask = jnp.where(valid, 0.0, NEG_INF)
    for kv in range(2):
        parts = []
        for pr in range(2):
            c0 = kv * 2 * LANES + pr * LANES
            qp = q_ref[:, c0:c0 + LANES]
            parts += [jnp.where(lane < HEAD_DIM, qp, zero), jnp.where(lane >= HEAD_DIM, qp, zero)]
        qs = jnp.concatenate(parts, axis=0)
        cs = slice(kv * LANES, (kv + 1) * LANES)
        kb = jnp.concatenate([kp[:, cs], kc_[:, cs], kn[:, cs]], axis=0)
        vb = jnp.concatenate([vp[:, cs], vc_[:, cs], vn[:, cs]], axis=0)
        sink = jnp.concatenate([jnp.full((blk, 1), sink_ref[kv * 4 + g], F32) for g in range(4)], axis=0)
        o = _softmax_pv([(_nt(qs, kb), vb, lambda r0, rc: band_mask[r0 % blk:r0 % blk + rc], None),
                         (_nt(qs, kx_ref[:, cs]), vx_ref[:, cs], None, None)], extra=sink, rc=64)
        o_ref[:, kv * 2 * LANES:kv * 2 * LANES + LANES] = jnp.where(lane < HEAD_DIM, o[0:blk], o[blk:2 * blk]).astype(BF16)
        o_ref[:, kv * 2 * LANES + LANES:(kv + 1) * 2 * LANES] = jnp.where(
            lane < HEAD_DIM, o[2 * blk:3 * blk], o[3 * blk:4 * blk]).astype(BF16)


def _wa_attention(rw, qkv, sink):
    B, T, C = rw.B, rw.T, rw.C
    blk = WA_BLOCK
    nb = T // blk
    ctxrow = (B * T) // C

    def kvspec(j, col):
        return pl.BlockSpec((blk, 2 * LANES), lambda b, n: (b * nb + jnp.clip(n + j, 0, nb - 1), col))

    return pl.pallas_call(
        functools.partial(_wa_kernel, nb),
        grid=(B, nb),
        in_specs=[pl.BlockSpec(memory_space=pltpu.SMEM),
                  pl.BlockSpec((blk, 4 * LANES), lambda b, n: (b * nb + n, 3))]
        + [kvspec(j, 8) for j in (-1, 0, 1)] + [kvspec(j, 9) for j in (-1, 0, 1)]
        + [pl.BlockSpec((C, 2 * LANES), lambda b, n: (ctxrow + b, 8)),
           pl.BlockSpec((C, 2 * LANES), lambda b, n: (ctxrow + b, 9))],
        out_specs=pl.BlockSpec((blk, 4 * LANES), lambda b, n: (b * nb + n, 0)),
        out_shape=jax.ShapeDtypeStruct((B * T, 4 * LANES), BF16),
        compiler_params=_cparams(2),
        name="window_attention",
    )(sink.astype(F32), qkv, *([qkv] * 8))


def _ctx_attn_kernel(sink_ref, t_ref, o_ref):
    C = t_ref.shape[0]
    lane = lax.broadcasted_iota(jnp.int32, (C, LANES), 1)
    zero = jnp.zeros((C, LANES), BF16)

    def pair(q2, k2, v2, sinks):
        out = jnp.zeros((C, LANES), F32)
        for hh in range(2):
            m = (lane < HEAD_DIM) if hh == 0 else (lane >= HEAD_DIM)
            extra = None if sinks is None else jnp.full((C, 1), sinks[hh], F32)
            o = _softmax_pv([(_nt(jnp.where(m, q2, zero), k2), v2, None, None)], extra=extra, rc=64)
            out = jnp.where(m, o, out)
        return out.astype(BF16)

    for p in range(4):
        c = p * LANES
        o_ref[:, c:c + LANES] = pair(t_ref[:, c:c + LANES], t_ref[:, 512 + c:640 + c], t_ref[:, 1024 + c:1152 + c], None)
    for kv in range(2):
        kd = t_ref[:, 2048 + kv * LANES:2176 + kv * LANES]
        vd = t_ref[:, 2304 + kv * LANES:2432 + kv * LANES]
        for pr in range(2):
            c = kv * 256 + pr * LANES
            h0 = kv * 4 + pr * 2
            o_ref[:, 512 + c:640 + c] = pair(t_ref[:, 1536 + c:1664 + c], kd, vd, (sink_ref[h0], sink_ref[h0 + 1]))


def _ctx_attention(rw, qkv, sink):
    B, T, C = rw.B, rw.T, rw.C
    ctxrow = (B * T) // C
    return pl.pallas_call(
        _ctx_attn_kernel,
        grid=(B,),
        in_specs=[pl.BlockSpec(memory_space=pltpu.SMEM),
                  pl.BlockSpec((C, qkv.shape[1]), lambda b: (ctxrow + b, 0))],
        out_specs=pl.BlockSpec((C, 8 * LANES), lambda b: (b, 0)),
        out_shape=jax.ShapeDtypeStruct((B * C, 8 * LANES), BF16),
        compiler_params=_cparams(1),
        name="context_attention",
    )(sink.astype(F32), qkv)


S5_Q = 16
CONV_TILE = 256
CONV_HALO = 16


def _ssm_inproj_kernel(x_ref, g_ref, sh_ref, sc_ref, w_ref, z_ref, xbc_ref, u_ref, uj_ref, dt_ref, h_scr, u_scr):
    h_scr[...] = _ada_norm(x_ref[...], g_ref[...], sh_ref[...], sc_ref[...]).astype(BF16)

    def mm(c0, n):
        return jnp.dot(h_scr[...], w_ref[:, c0:c0 + n], preferred_element_type=F32)

    for c in range(4):
        z_ref[:, c * 256:(c + 1) * 256] = mm(c * 256, 256).astype(BF16)
    for c in range(6):
        xbc_ref[:, c * 256:(c + 1) * 256] = mm(1024 + c * 256, 256).astype(BF16)
    for c in range(2):
        y = mm(2560 + c * 256, 256)
        u_ref[:, c * 256:(c + 1) * 256] = y.astype(BF16)
        u_scr[2 * c] = y[:, :LANES]
        u_scr[2 * c + 1] = y[:, LANES:]
    dt_ref[...] = mm(3072, LANES)
    nchunk = u_scr.shape[1] // S5_Q
    for j in range(S5_Q):
        for t in range(u_scr.shape[0]):
            uj_ref[j, :, t * LANES:(t + 1) * LANES] = u_scr[t, pl.ds(j, nchunk, stride=S5_Q), :].astype(BF16)


def _ssm_inproj(rw, xall, mods, norm_g, w_in):
    D, tm = rw.D, rw.tm
    w = jnp.concatenate([w_in[:, 0:2560], w_in[:, 2592:3104], w_in[:, 2560:2592], jnp.zeros((D, LANES - 32), F32)],
                        axis=1).astype(BF16)
    row = lambda i: (i, 0)
    return pl.pallas_call(
        _ssm_inproj_kernel,
        grid=(rw.ntot,),
        in_specs=[pl.BlockSpec((tm, D), row), pl.BlockSpec((1, D), lambda i: (0, 0)),
                  _mod_spec(rw, 0), _mod_spec(rw, 1), pl.BlockSpec((D, 3200), lambda i: (0, 0))],
        out_specs=[pl.BlockSpec((tm, 1024), row), pl.BlockSpec((tm, 1536), row), pl.BlockSpec((tm, 512), row),
                   pl.BlockSpec((S5_Q, tm // S5_Q, 512), lambda i: (0, i, 0)), pl.BlockSpec((tm, LANES), row)],
        out_shape=[jax.ShapeDtypeStruct((rw.rows, 1024), BF16), jax.ShapeDtypeStruct((rw.rows, 1536), BF16),
                   jax.ShapeDtypeStruct((rw.rows, 512), BF16),
                   jax.ShapeDtypeStruct((S5_Q, rw.rows // S5_Q, 512), BF16),
                   jax.ShapeDtypeStruct((rw.rows, LANES), F32)],
        scratch_shapes=[pltpu.VMEM((tm, D), BF16), pltpu.VMEM((512 // LANES, tm, LANES), F32)],
        compiler_params=_cparams(1),
        name="ssm_inproj",
    )(xall, norm_g[None, :], mods, mods, w)


def _softplus(x):
    return jnp.maximum(x, 0.0) + jnp.log(1.0 + jnp.exp(-jnp.abs(x)))


def _conv_kernel(lat_tiles, tpb, cpb, x_ref, pv_ref, nx_ref, w_ref, b_ref, dtr_ref, dtb_ref, act_ref, dt_ref):
    i = pl.program_id(0)
    is_lat = i < lat_tiles
    pos = jnp.where(is_lat, i % tpb, (i - lat_tiles) % cpb)
    last_pos = jnp.where(is_lat, tpb - 1, cpb - 1)
    x = x_ref[...].astype(F32)
    tc = x.shape[0]
    prev_row = jnp.where(pos == 0, 0.0, pv_ref[...].astype(F32)[CONV_HALO - 1:CONV_HALO, :])
    next_row = jnp.where(pos == last_pos, 0.0, nx_ref[...].astype(F32)[0:1, :])
    row = lax.broadcasted_iota(jnp.int32, x.shape, 0)
    xm1 = jnp.where(row == 0, prev_row, pltpu.roll(x, 1, 0))
    xp1 = jnp.where(row == tc - 1, next_row, pltpu.roll(x, tc - 1, 0))
    y = w_ref[0:1, :] * xm1 + w_ref[1:2, :] * x + w_ref[2:3, :] * xp1 + b_ref[...]
    act_ref[...] = _silu(y).astype(BF16)
    sp = _softplus(dtr_ref[...] + dtb_ref[...])
    dt_ref[0] = sp
    dt_ref[1] = pltpu.roll(sp, LANES - 16, 1)


def _ssm_conv(rw, xbc, dtr, conv_w, conv_b, dt_bias):
    B, T, C = rw.B, rw.T, rw.C
    tc = CONV_TILE
    assert T % tc == 0 and C % tc == 0
    lat_tiles, tpb, cpb = (B * T) // tc, T // tc, C // tc
    ntiles = rw.rows // tc
    hpt = tc // CONV_HALO
    nhalo = rw.rows // CONV_HALO
    W = xbc.shape[1]
    dtb = jnp.concatenate([dt_bias.reshape(-1), jnp.zeros((LANES - 32,), F32)])[None, :]
    row = lambda i: (i, 0)
    return pl.pallas_call(
        functools.partial(_conv_kernel, lat_tiles, tpb, cpb),
        grid=(ntiles,),
        in_specs=[pl.BlockSpec((tc, W), row),
                  pl.BlockSpec((CONV_HALO, W), lambda i: (jnp.maximum(i * hpt - 1, 0), 0)),
                  pl.BlockSpec((CONV_HALO, W), lambda i: (jnp.minimum((i + 1) * hpt, nhalo - 1), 0)),
                  pl.BlockSpec((3, W), lambda i: (0, 0)), pl.BlockSpec((1, W), lambda i: (0, 0)),
                  pl.BlockSpec((tc, LANES), row), pl.BlockSpec((1, LANES), lambda i: (0, 0))],
        out_specs=[pl.BlockSpec((tc, W), row), pl.BlockSpec((2, tc, LANES), lambda i: (0, i, 0))],
        out_shape=[jax.ShapeDtypeStruct((rw.rows, W), BF16), jax.ShapeDtypeStruct((2, rw.rows, LANES), F32)],
        compiler_params=_cparams(1),
        name="ssm_conv",
    )(xbc, xbc, xbc, conv_w, conv_b[None, :], dtr, dtb)


def _ssd_kernel(actf_ref, actb_ref, dtf_ref, dtb_ref, tri_ref, a_ref, yf_ref, yb_ref, hst):
    @pl.when(pl.program_id(1) == 0)
    def _():
        hst[...] = jnp.zeros_like(hst)

    _ssd_chunk(actf_ref, dtf_ref, tri_ref[0], a_ref[0], yf_ref, hst.at[0])
    _ssd_chunk(actb_ref, dtb_ref, tri_ref[1], a_ref[1], yb_ref, hst.at[1])


def _ssd_chunk(act_ref, dt_ref, tri, avec, y_ref, hst):
    q = SSD_CHUNK
    dt = dt_ref[...]
    da = dt * avec
    acs = jnp.dot(tri, da, preferred_element_type=F32, precision=lax.Precision.HIGHEST)
    tot = jnp.sum(da, axis=0, keepdims=True)
    acs_t = acs.T
    dt_t = dt.T
    eacs = jnp.exp(acs)
    wend = jnp.exp(tot - acs) * dt
    etot = jnp.exp(tot)
    mask = tri > 0.5
    left = lax.broadcasted_iota(jnp.int32, (q, LANES), 1) < HEAD_DIM
    left1 = lax.broadcasted_iota(jnp.int32, (1, LANES), 1) < HEAD_DIM
    for g in range(2):
        bg = act_ref[:, 1024 + g * 128:1152 + g * 128]
        cg = act_ref[:, 1280 + g * 128:1408 + g * 128]
        cb = _nt(cg, bg)
        hin = hst[:, g * 512:(g + 1) * 512]
        yoff = jnp.dot(cg, hin.astype(BF16), preferred_element_type=F32)
        xw, dec = [], []
        for pr in range(4):
            h_a = g * 8 + pr * 2
            h_b = h_a + 1
            c0 = h_a * HEAD_DIM
            x2 = act_ref[:, c0:c0 + LANES]
            outs = []
            for h in (h_a, h_b):
                seg = acs[:, h:h + 1] - acs_t[h:h + 1, :]
                w = cb * jnp.exp(jnp.where(mask, seg, NEG_INF)) * dt_t[h:h + 1, :]
                outs.append(jnp.dot(w.astype(BF16), x2, preferred_element_type=F32))
            yd = jnp.where(left, outs[0], outs[1])
            sc = jnp.where(left, eacs[:, h_a:h_a + 1], eacs[:, h_b:h_b + 1])
            y_ref[:, c0:c0 + LANES] = (yd + yoff[:, pr * LANES:(pr + 1) * LANES] * sc).astype(BF16)
            wsc = jnp.where(left, wend[:, h_a:h_a + 1], wend[:, h_b:h_b + 1])
            xw.append((x2.astype(F32) * wsc).astype(BF16))
            dec.append(jnp.where(left1, etot[:, h_a:h_a + 1], etot[:, h_b:h_b + 1]))
        bg_t = bg.astype(F32).T.astype(BF16)
        snew = jnp.dot(bg_t, jnp.concatenate(xw, axis=1), preferred_element_type=F32)
        hst[:, g * 512:(g + 1) * 512] = hin * jnp.concatenate(dec, axis=1) + snew


def _ssd(rw, act, dt2, a_log):
    B, T, C = rw.B, rw.T, rw.C
    q = SSD_CHUNK
    nct, nlt = C // q, T // q
    ctx0 = (B * T) // q
    r = np.arange(q)
    tri = jnp.asarray(np.stack([r[None, :] <= r[:, None], r[None, :] >= r[:, None]]).astype(np.float32))
    avec = jnp.concatenate([-jnp.exp(a_log.astype(F32)), jnp.zeros((2, LANES - a_log.shape[1]), F32)], axis=1)[:, None, :]

    def blk(d, b, s):
        kc = s if d == 0 else nct - 1 - s
        kl = s - nct if d == 0 else nlt - 1 - (s - nct)
        return jnp.where(s < nct, ctx0 + b * nct + kc, b * nlt + kl)

    aspec = lambda d: pl.BlockSpec((q, act.shape[1]), lambda b, s: (blk(d, b, s), 0))
    dspec = lambda d: pl.BlockSpec((None, q, LANES), lambda b, s: (d, blk(d, b, s), 0))
    yspec = lambda d: pl.BlockSpec((q, 1024), lambda b, s: (blk(d, b, s), 0))
    return pl.pallas_call(
        _ssd_kernel,
        grid=(B, nct + nlt),
        in_specs=[aspec(0), aspec(1), dspec(0), dspec(1),
                  pl.BlockSpec((2, q, q), lambda b, s: (0, 0, 0)),
                  pl.BlockSpec((2, 1, LANES), lambda b, s: (0, 0, 0))],
        out_specs=[yspec(0), yspec(1)],
        out_shape=[jax.ShapeDtypeStruct((rw.rows, 1024), BF16)] * 2,
        scratch_shapes=[pltpu.VMEM((2, q, 1024), F32)],
        compiler_params=_cparams(2),
        name="ssd_scan",
    )(act, act, dt2, dt2, tri, avec)


def _cmul(ar, ai, br, bi):
    return ar * br - ai * bi, ar * bi + ai * br


def _s5_weight_kernel(lre_ref, lim_ref, ls_ref, bre_ref, bim_ref, cre_ref, cim_ref,
                      wsr_ref, wsi_ref, wor_ref, woi_ref, kt_ref, are_ref, aim_ref):
    lre, lim = lre_ref[...], lim_ref[...]
    step = jnp.exp(ls_ref[...])
    er, ei = lre * step, lim * step
    npow = 24
    p = lax.broadcasted_iota(jnp.int32, (1, npow, 1), 1).astype(F32)
    mag = jnp.exp(p * er)
    pre, pim = mag * jnp.cos(p * ei), mag * jnp.sin(p * ei)
    a_re, a_im = pre[:, 1:2, :], pim[:, 1:2, :]
    den = lre * lre + lim * lim
    q_re = ((a_re - 1.0) * lre + a_im * lim) / den
    q_im = (a_im * lre - (a_re - 1.0) * lim) / den
    bb_re, bb_im = _cmul(q_re, q_im, bre_ref[...], bim_ref[...])
    c_re, c_im = cre_ref[...], cim_ref[...]
    ws_r, ws_i, wo_r, wo_i, ca_r, ca_i = [], [], [], [], [], []
    for t in range(S5_Q):
        r, i = _cmul(bb_re, bb_im, pre[:, t:t + 1, :], pim[:, t:t + 1, :])
        ws_r.append(r)
        ws_i.append(i)
        r, i = _cmul(c_re, c_im, pre[:, t:t + 1, :], pim[:, t:t + 1, :])
        ca_r.append(r)
        ca_i.append(i)
        r, i = _cmul(c_re, c_im, pre[:, t + 1:t + 2, :], pim[:, t + 1:t + 2, :])
        wo_r.append(r)
        wo_i.append(-i)
    cat = lambda xs: jnp.concatenate(xs, axis=1)
    wsr_ref[...] = cat(ws_r)
    wsi_ref[...] = cat(ws_i)
    wor_ref[...] = cat(wo_r)
    woi_ref[...] = cat(wo_i)
    bdot = lambda a, b: lax.dot_general(a, b, (((2,), (2,)), ((0,), (0,))), preferred_element_type=F32,
                                        precision=lax.Precision.HIGHEST)
    kt_ref[...] = bdot(cat(ca_r), bb_re) - bdot(cat(ca_i), bb_im)
    are_ref[...] = pre[:, S5_Q:S5_Q + 1, :]
    aim_ref[...] = pim[:, S5_Q:S5_Q + 1, :]


def _s5_weights(lam_re, lam_im, log_step, b_re, b_im, c_re, c_im):
    nd, ng, ns = lam_re.shape
    G = nd * ng
    ch = S5_GROUP
    gb = 8
    qc = S5_Q * ch
    f = lambda a: a.astype(F32)
    args = (f(lam_re).reshape(G, 1, ns), f(lam_im).reshape(G, 1, ns), f(log_step).reshape(G, 1, 1),
            f(b_re).reshape(G, ns, ch).transpose(0, 2, 1), f(b_im).reshape(G, ns, ch).transpose(0, 2, 1),
            f(c_re).reshape(G, ch, ns), f(c_im).reshape(G, ch, ns))
    spec = lambda a: pl.BlockSpec((gb,) + a.shape[1:], lambda i: (i, 0, 0))
    oshape = [jax.ShapeDtypeStruct((G, qc, ns), F32)] * 4 + [jax.ShapeDtypeStruct((G, qc, ch), F32)] \
        + [jax.ShapeDtypeStruct((G, 1, ns), F32)] * 2
    wsr, wsi, wor, woi, kt, a_re, a_im = pl.pallas_call(
        _s5_weight_kernel,
        grid=(G // gb,),
        in_specs=[spec(a) for a in args],
        out_specs=[pl.BlockSpec((gb,) + s.shape[1:], lambda i: (i, 0, 0)) for s in oshape],
        out_shape=oshape,
        compiler_params=_cparams(1),
        name="s5_weights",
    )(*args)

    def by_dir(w, flip_dir):
        w = w.reshape(nd, ng, S5_Q, ch, ns)
        w = jnp.stack([jnp.flip(w[d], axis=1) if d == flip_dir else w[d] for d in range(nd)])
        return w.reshape(nd, ng, qc, ns)

    def pack(w):
        z = jnp.zeros_like(w)
        even = (np.arange(ng) % 2 == 0)[None, :, None, None]
        return jnp.where(even, jnp.concatenate([w, z], axis=-1), jnp.concatenate([z, w], axis=-1)).astype(BF16)

    ws_r, ws_i = pack(by_dir(wsr, 0)), pack(by_dir(wsi, 0))
    wo_r, wo_i = pack(by_dir(wor, 1)), pack(by_dir(woi, 1))
    k = kt.astype(BF16).reshape(nd, ng, S5_Q, ch, ch).transpose(0, 1, 4, 2, 3)
    kf = k.reshape(nd, ng, ch, qc)
    kb = jnp.flip(k, axis=3).reshape(nd, ng, ch, qc)
    rows_f, rows_b = [], []
    for j in range(S5_Q):
        z_f = jnp.zeros((ng, ch, j * ch), BF16)
        z_b = jnp.zeros((ng, ch, (S5_Q - 1 - j) * ch), BF16)
        rows_f.append(jnp.concatenate([z_f, kf[0, :, :, :(S5_Q - j) * ch]], axis=-1))
        rows_b.append(jnp.concatenate([kb[1, :, :, (S5_Q - 1 - j) * ch:], z_b], axis=-1))
    bt = jnp.stack([jnp.stack(rows_f, axis=1), jnp.stack(rows_b, axis=1)]).reshape(nd, ng, qc, qc)
    pair = lambda a: a.reshape(nd, ng // 2, 1, 2 * ns)
    return bt, ws_r, ws_i, wo_r, wo_i, pair(a_re), pair(a_im)


S5_GB = LANES // S5_GROUP


def _s5_kernel(B, nct, nlt, uj_ref, perm_ref, bt_ref, wsr_ref, wsi_ref, wor_ref, woi_ref, are_ref, aim_ref, yj_ref,
               x_scr, y_scr, s_re, s_im):
    gb, npair, qc = S5_GB, S5_GB // 2, S5_Q * S5_GROUP
    lhs = jnp.concatenate([uj_ref[j] for j in range(S5_Q)], axis=1)
    for m in range(gb):
        x_scr[:, m * qc:(m + 1) * qc] = jnp.dot(lhs, perm_ref[:, m * qc:(m + 1) * qc],
                                                preferred_element_type=F32).astype(BF16)
    xg = lambda g: x_scr[:, g * qc:(g + 1) * qc]
    for d in range(2):
        for pr in range(npair):
            for dst, w_ref in ((s_re, wsr_ref), (s_im, wsi_ref)):
                dst[d, pr] = (jnp.dot(xg(2 * pr), w_ref[d, 2 * pr], preferred_element_type=F32)
                              + jnp.dot(xg(2 * pr + 1), w_ref[d, 2 * pr + 1], preferred_element_type=F32))
    chains = [(d, pr, b) for d in range(2) for pr in range(npair) for b in range(B)]
    coef = {(d, pr): (are_ref[d, pr], aim_ref[d, pr]) for d in range(2) for pr in range(npair)}
    ctx0 = B * nlt

    def body(s, carry):
        in_ctx = s < nct
        rows = {}
        for d in range(2):
            kc = s if d == 0 else nct - 1 - s
            kl = s - nct if d == 0 else nlt - 1 - (s - nct)
            for b in range(B):
                rows[(d, b)] = pl.ds(jnp.where(in_ctx, ctx0 + b * nct + kc, b * nlt + kl), 1)
        contrib = [(s_re[d, pr, rows[(d, b)], :], s_im[d, pr, rows[(d, b)], :]) for d, pr, b in chains]
        new = []
        for (d, pr, b), (hr, hi), (sr, si) in zip(chains, carry, contrib):
            ar, ai = coef[(d, pr)]
            s_re[d, pr, rows[(d, b)], :] = hr
            s_im[d, pr, rows[(d, b)], :] = hi
            new.append((ar * hr - ai * hi + sr, ar * hi + ai * hr + si))
        return tuple(new)

    zero = jnp.zeros((1, LANES), F32)
    lax.fori_loop(0, nct + nlt, body, tuple((zero, zero) for _ in chains))
    for g in range(gb):
        acc = None
        for d in range(2):
            t = (jnp.dot(xg(g), bt_ref[d, g], preferred_element_type=F32)
                 + _nt(s_re[d, g // 2].astype(BF16), wor_ref[d, g])
                 + _nt(s_im[d, g // 2].astype(BF16), woi_ref[d, g]))
            acc = t if acc is None else acc + t
        y_scr[:, g * qc:(g + 1) * qc] = acc.astype(BF16)
    for i in range(S5_Q):
        yj_ref[i] = _nt(y_scr[...], perm_ref[i * LANES:(i + 1) * LANES, :]).astype(BF16)


def _s5(rw, uj, weights):
    B, T, C = rw.B, rw.T, rw.C
    bt, ws_r, ws_i, wo_r, wo_i, a_re, a_im = weights
    ng = bt.shape[1]
    q, gb = S5_Q, S5_GB
    nct, nlt = C // q, T // q
    nrow = uj.shape[1]
    qc = q * S5_GROUP
    k = gb * qc
    idx = np.arange(k)
    j, m, c = idx // LANES, (idx % LANES) // S5_GROUP, idx % S5_GROUP
    perm = np.zeros((k, k), np.float32)
    perm[idx, m * qc + j * S5_GROUP + c] = 1.0
    once = dict(pipeline_mode=pl.Buffered(1))
    wspec = lambda n: pl.BlockSpec((2, gb, qc, n), lambda i: (0, i, 0, 0), **once)
    aspec = pl.BlockSpec((2, gb // 2, 1, LANES), lambda i: (0, i, 0, 0))
    return pl.pallas_call(
        functools.partial(_s5_kernel, B, nct, nlt),
        grid=(ng // gb,),
        in_specs=[pl.BlockSpec((q, nrow, LANES), lambda i: (0, 0, i), **once),
                  pl.BlockSpec((k, k), lambda i: (0, 0), **once),
                  wspec(qc), wspec(LANES), wspec(LANES), wspec(LANES), wspec(LANES), aspec, aspec],
        out_specs=pl.BlockSpec((q, nrow, LANES), lambda i: (0, 0, i)),
        out_shape=jax.ShapeDtypeStruct(uj.shape, BF16),
        scratch_shapes=[pltpu.VMEM((nrow, k), BF16), pltpu.VMEM((nrow, k), BF16),
                        pltpu.VMEM((2, gb // 2, nrow, LANES), F32), pltpu.VMEM((2, gb // 2, nrow, LANES), F32)],
        compiler_params=_cparams(1),
        name="s5_scan",
    )(uj, jnp.asarray(perm, BF16), bt, ws_r, ws_i, wo_r, wo_i, a_re, a_im)


def _gelu_tanh(x):
    return 0.5 * x * (1.0 + jnp.tanh(math.sqrt(2.0 / math.pi) * (x + 0.044715 * (x * x * x))))


def _ssm_outproj_kernel(y0_ref, y1_ref, xs_ref, z_ref, v_ref, u_ref, dsk_ref, nw_ref, s5d_ref, gw_ref, gb_ref,
                        x_ref, w_ref, g1_ref, gn_ref, sh2_ref, sc2_ref, wr_ref, br_ref, lt_ref,
                        xo_ref, h2_ref, rt_ref, cnt_ref, carry, v_scr):
    i = pl.program_id(0)
    y = y0_ref[...].astype(F32) + y1_ref[...].astype(F32) + dsk_ref[...] * xs_ref[...].astype(F32)
    y = _rms(y * _silu(z_ref[...].astype(F32))) * nw_ref[...]
    ntile = v_scr.shape[0]
    nchunk = v_scr.shape[1] // S5_Q
    for j in range(S5_Q):
        for t in range(ntile):
            v_scr[t, pl.ds(j, nchunk, stride=S5_Q), :] = v_ref[j, :, t * LANES:(t + 1) * LANES].astype(F32)
    s5_y = jnp.concatenate([v_scr[t] for t in range(ntile)], axis=1)
    v = _gelu_tanh(s5_y + s5d_ref[...] * u_ref[...].astype(F32))
    v = v * _sigmoid(jnp.dot(v.astype(BF16), gw_ref[...], preferred_element_type=F32) + gb_ref[...])
    mix = jnp.concatenate([y, v], axis=1).astype(BF16)
    yo = jnp.dot(mix, w_ref[...], preferred_element_type=F32)
    _post_mixer(i, x_ref[...], yo, g1_ref[...], gn_ref[...], sh2_ref[...], sc2_ref[...], wr_ref, br_ref, lt_ref,
                xo_ref, h2_ref, rt_ref, cnt_ref, carry)


def _ssm_outproj(rw, ssd_y, act, z, s5_y, u, d_skip, norm_w, s5_d, glu_w, glu_b, xall, w_out, mods, norm_ffn, wr, br):
    D, tm = rw.D, rw.tm
    ntiles = rw.nlat
    post_in, post_out = _post_specs(rw)
    row = lambda i: (i, 0)
    vec = lambda n: pl.BlockSpec((1, n), lambda i: (0, 0))
    dsk = jnp.repeat(d_skip.astype(F32), HEAD_DIM)[None, :]
    return pl.pallas_call(
        _ssm_outproj_kernel,
        grid=(ntiles,),
        in_specs=[pl.BlockSpec((tm, 1024), row), pl.BlockSpec((tm, 1024), row),
                  pl.BlockSpec((tm, 1024), row), pl.BlockSpec((tm, 1024), row),
                  pl.BlockSpec((S5_Q, tm // S5_Q, 512), lambda i: (0, i, 0)),
                  pl.BlockSpec((tm, 512), row), vec(1024), vec(1024), vec(512),
                  pl.BlockSpec((512, 512), lambda i: (0, 0)), vec(512),
                  pl.BlockSpec((tm, D), row), pl.BlockSpec((1536, D), lambda i: (0, 0)), _mod_spec(rw, 2)] + post_in,
        out_specs=post_out,
        out_shape=_post_shapes(ntiles * tm, D),
        scratch_shapes=[pltpu.VMEM((1, LANES), F32), pltpu.VMEM((512 // LANES, tm, LANES), F32)],
        compiler_params=_cparams(1),
        name="ssm_outproj_router",
    )(ssd_y[0], ssd_y[1], act, z, s5_y, u, dsk, norm_w[None, :], s5_d[None, :], glu_w.astype(BF16), glu_b[None, :],
      xall, w_out.astype(BF16), mods, norm_ffn[None, :], mods, mods, wr, br, _lower_tri(tm))


def kernel(x, c, ctx, c_ctx, mod_w, mod_b, norm_mix, norm_ffn, att_w_in, att_w_out, na_q_norm, na_k_norm, na_rel_bias, wa_q_norm, wa_k_norm, wa_sink, ssm_w_in, ssm_w_out, ssd_conv_w, ssd_conv_b, ssd_dt_bias, ssd_a_log, ssd_d, ssd_norm, s5_lambda_re, s5_lambda_im, s5_log_step, s5_b_re, s5_b_im, s5_c_re, s5_c_im, s5_d, s5_glu_w, s5_glu_b, moe_w_group, moe_b_group, moe_w_expert, moe_b_expert, moe_w13, moe_w2):
    B, T, D = x.shape
    C = ctx.shape[1]
    rw = _Rows(B, T, C, D, ROW_TILE)
    xl = x.reshape(B * T, D)
    xc = ctx.reshape(B * C, D)
    cm = jnp.concatenate([c, c_ctx[None, :], jnp.zeros((8 - B - 1, D), F32)], axis=0)
    mods = _modulation(cm, mod_w, mod_b)
    mods = mods.reshape(mods.shape[0], 8, 1, 6 * D)

    m0 = mods[0]
    qkv = _att_inproj(rw, xl, xc, m0, norm_mix[0], att_w_in[0], na_q_norm[0], na_k_norm[0], wa_q_norm[0],
                      wa_k_norm[0])
    na = _na_attention(rw, qkv, na_rel_bias[0])
    wa = _wa_attention(rw, qkv, wa_sink[0])
    cx = _ctx_attention(rw, qkv, wa_sink[0])
    wr, br = _router_weights(moe_w_group[0], moe_b_group[0], moe_w_expert[0], moe_b_expert[0])
    xall, h2, route, counts = _att_outproj(rw, na, wa, cx, xl, xc, att_w_out[0], m0, norm_ffn[0], wr, br)
    y1, y2 = _moe(h2, route, counts, moe_w13, moe_w2, 0)
    xall = _combine(rw, rw.ntot, xall, y1, y2, route, m0)

    m1 = mods[1]
    z, xbc, u, uj, dtr = _ssm_inproj(rw, xall, m1, norm_mix[1], ssm_w_in[0])
    act, dt2 = _ssm_conv(rw, xbc, dtr, ssd_conv_w[0], ssd_conv_b[0], ssd_dt_bias[0])
    ssd_y = _ssd(rw, act, dt2, ssd_a_log[0])
    s5_w = _s5_weights(s5_lambda_re[0], s5_lambda_im[0], s5_log_step[0], s5_b_re[0], s5_b_im[0], s5_c_re[0],
                       s5_c_im[0])
    s5_y = _s5(rw, uj, s5_w)
    wr, br = _router_weights(moe_w_group[1], moe_b_group[1], moe_w_expert[1], moe_b_expert[1])
    xlat, h2, route, counts = _ssm_outproj(rw, ssd_y, act, z, s5_y, u, ssd_d[0], ssd_norm[0], s5_d[0], s5_glu_w[0],
                                           s5_glu_b[0], xall, ssm_w_out[0], m1, norm_ffn[1], wr, br)
    y1, y2 = _moe(h2, route, counts, moe_w13, moe_w2, 1)
    out = _combine(rw, rw.nlat, xlat, y1, y2, route, m1)
    return out.reshape(B, T, D)
```

```python
import functools
import math

import jax
import jax.numpy as jnp
import numpy as np
from jax import lax
from jax.experimental import pallas as pl
from jax.experimental.pallas import tpu as pltpu

F32 = jnp.float32
BF16 = jnp.bfloat16

EPS = 1e-6
NEG_INF = -1e30
GRID_W = 64
HEAD_DIM = 64
NA_KH = 8
NA_KW = 16
WA_BLOCK = 128
ROPE_BASE = 10000.0
SSD_CHUNK = 128
S5_GROUP = 16
S5_STATE = 64
MOE_GROUPS = 4
MOE_EPG = 8
MOE_EXPERTS = MOE_GROUPS * MOE_EPG

LANES = 128
ROW_TILE = 512
MOE_TILE = 256
VMEM_LIMIT = 56 * 1024 * 1024
MOE_VMEM_LIMIT = 60 * 1024 * 1024


def _cparams(n_axes, vmem=VMEM_LIMIT):
    return pltpu.CompilerParams(dimension_semantics=("arbitrary",) * n_axes, vmem_limit_bytes=vmem)


def _sigmoid(x):
    return 1.0 / (1.0 + jnp.exp(-x))


def _silu(x):
    return x * _sigmoid(x)


def _rms(x, eps=EPS):
    return x * lax.rsqrt(jnp.mean(x * x, axis=-1, keepdims=True) + eps)


def _ada_norm(x, g, shift, scale):
    return (_rms(x) * g) * (1.0 + scale) + shift


def _mod_kernel(c_ref, w_ref, b_ref, o_ref):
    a = _silu(c_ref[...])
    o_ref[...] = jnp.dot(a, w_ref[...], preferred_element_type=F32, precision=lax.Precision.HIGHEST) + b_ref[...]


def _modulation(cm, mod_w, mod_b):
    depth, d, n6 = mod_w.shape
    tn = 1024
    return pl.pallas_call(
        _mod_kernel,
        grid=(depth, n6 // tn),
        in_specs=[pl.BlockSpec((8, d), lambda l, j: (0, 0)),
                  pl.BlockSpec((None, d, tn), lambda l, j: (l, 0, j)),
                  pl.BlockSpec((None, 1, tn), lambda l, j: (l, 0, j))],
        out_specs=pl.BlockSpec((None, 8, tn), lambda l, j: (l, 0, j)),
        out_shape=jax.ShapeDtypeStruct((depth, 8, n6), F32),
        compiler_params=_cparams(2),
        name="modulation",
    )(cm, mod_w, mod_b.reshape(depth, 1, n6))


class _Rows:
    def __init__(self, B, T, C, D, tm):
        assert T % tm == 0 and (B * C) % tm == 0
        self.B, self.T, self.C, self.D, self.tm = B, T, C, D, tm
        self.tpb = T // tm
        self.nlat = B * self.tpb
        self.nctx = (B * C) // tm
        self.ntot = self.nlat + self.nctx
        self.rows = B * (T + C)

    def group(self, i):
        return jnp.where(i < self.nlat, i // self.tpb, self.B)


def _mod_spec(rw, col):
    return pl.BlockSpec((None, 1, rw.D), lambda i, *_: (rw.group(i), 0, col))


def _seg_norm(y, seg, gcol):
    ss = jnp.dot((y * y).astype(BF16), seg, preferred_element_type=F32)
    return y * lax.rsqrt(ss + EPS) * gcol


def _rope(y, cos, sin):
    w = y.shape[-1]
    lane = lax.broadcasted_iota(jnp.int32, y.shape, 1)
    first = (lane % 32) < 16
    partner = jnp.where(first, pltpu.roll(y, w - 16, 1), pltpu.roll(y, 16, 1))
    return y * cos + partner * sin


def _dup_halves(k):
    lane = lax.broadcasted_iota(jnp.int32, k.shape, 1)
    sw = pltpu.roll(k, 64, 1)
    return jnp.where(lane < 64, k, sw), jnp.where(lane < 64, sw, k)


def _att_inproj_kernel(nlat, xl_ref, xc_ref, g_ref, sh_ref, sc_ref, w_ref, gcol_ref, cos_ref, sin_ref, seg_ref,
                       o_ref, h_scr):
    i = pl.program_id(0)
    x = jnp.where(i < nlat, xl_ref[...], xc_ref[...])
    h_scr[...] = _ada_norm(x, g_ref[...], sh_ref[...], sc_ref[...]).astype(BF16)
    seg = seg_ref[...]
    cos2 = jnp.concatenate([cos_ref[...], cos_ref[...]], axis=1)
    sin2 = jnp.concatenate([sin_ref[...], sin_ref[...]], axis=1)
    for c in range(9):
        c0 = c * 256
        y = jnp.dot(h_scr[...], w_ref[:, c0:c0 + 256], preferred_element_type=F32)
        gcol = gcol_ref[:, c0:c0 + 256]
        if c in (0, 1, 2, 3):
            o_ref[:, c0:c0 + 256] = _seg_norm(y, seg, gcol).astype(BF16)
        elif c in (4, 5):
            o_ref[:, c0:c0 + 256] = y.astype(BF16)
        elif c in (6, 7):
            o_ref[:, c0:c0 + 256] = _rope(_seg_norm(y, seg, gcol), cos2, sin2).astype(BF16)
        else:
            lane = lax.broadcasted_iota(jnp.int32, y.shape, 1)
            yk = jnp.where(lane < 128, _seg_norm(y, seg, gcol), y)
            yr = jnp.where(lane < 128, _rope(yk, cos2, sin2), yk)
            k0, k1 = _dup_halves(yr[:, :128])
            v0, v1 = _dup_halves(yr[:, 128:])
            o_ref[:, 2048:2176] = k0.astype(BF16)
            o_ref[:, 2176:2304] = k1.astype(BF16)
            o_ref[:, 2304:2432] = v0.astype(BF16)
            o_ref[:, 2432:2560] = v1.astype(BF16)


def _rope_tables(T, tm):
    t = np.arange(T)
    d = np.arange(HEAD_DIM)
    nf = HEAD_DIM // 4
    inv = jnp.asarray(ROPE_BASE, F32) ** (-jnp.arange(nf, dtype=F32) / nf)
    pos = np.where((d // 32 == 0)[None, :], (t // GRID_W)[:, None], (t % GRID_W)[:, None])
    ang = jnp.asarray(pos, F32) * inv[d % nf][None, :]
    sign = np.where((d % 32) < 16, -1.0, 1.0).astype(np.float32)
    cos = jnp.cos(ang)
    sin = jnp.sin(ang) * sign[None, :]
    cos = jnp.concatenate([cos, jnp.ones((tm, HEAD_DIM), F32)], axis=0)
    sin = jnp.concatenate([sin, jnp.zeros((tm, HEAD_DIM), F32)], axis=0)
    return jnp.tile(cos, (1, 2)), jnp.tile(sin, (1, 2))


def _att_inproj(rw, xl, xc, mods, norm_g, w_in, na_qn, na_kn, wa_qn, wa_kn):
    D, tm = rw.D, rw.tm
    scale = HEAD_DIM ** -0.5
    gcol = jnp.concatenate([jnp.tile(na_qn * scale, 8), jnp.tile(na_kn, 8), jnp.ones((512,), F32),
                            jnp.tile(wa_qn * scale, 8), jnp.tile(wa_kn, 2), jnp.ones((128,), F32)])[None, :]
    cos, sin = _rope_tables(rw.T, tm)
    segn = np.arange(256) // 64
    seg = jnp.asarray((segn[:, None] == segn[None, :]).astype(np.float32) / 64.0, BF16)
    nlat, tpb = rw.nlat, rw.tpb
    return pl.pallas_call(
        functools.partial(_att_inproj_kernel, nlat),
        grid=(rw.ntot,),
        in_specs=[pl.BlockSpec((tm, D), lambda i: (jnp.minimum(i, nlat - 1), 0)),
                  pl.BlockSpec((tm, D), lambda i: (jnp.maximum(i - nlat, 0), 0)),
                  pl.BlockSpec((1, D), lambda i: (0, 0)),
                  _mod_spec(rw, 0), _mod_spec(rw, 1),
                  pl.BlockSpec((D, 2304), lambda i: (0, 0)),
                  pl.BlockSpec((1, 2304), lambda i: (0, 0)),
                  pl.BlockSpec((tm, 128), lambda i: (jnp.where(i < nlat, i % tpb, tpb), 0)),
                  pl.BlockSpec((tm, 128), lambda i: (jnp.where(i < nlat, i % tpb, tpb), 0)),
                  pl.BlockSpec((256, 256), lambda i: (0, 0))],
        out_specs=pl.BlockSpec((tm, 2560), lambda i: (i, 0)),
        out_shape=jax.ShapeDtypeStruct((rw.rows, 2560), BF16),
        scratch_shapes=[pltpu.VMEM((tm, D), BF16)],
        compiler_params=_cparams(1),
        name="att_inproj",
    )(xl, xc, norm_g[None, :], mods, mods, w_in.astype(BF16), gcol, cos, sin, seg)


def _route(lg, lt, carry):
    lane = lax.broadcasted_iota(jnp.int32, lg.shape, 1).astype(F32)
    gm = lane < MOE_GROUPS
    mg = jnp.max(jnp.where(gm, lg, NEG_INF), axis=-1, keepdims=True)
    eg = jnp.where(gm, jnp.exp(jnp.where(gm, lg, NEG_INF) - mg), 0.0)
    pg = eg / jnp.sum(eg, axis=-1, keepdims=True)
    ptop = jnp.max(pg, axis=-1, keepdims=True)
    gidx = jnp.min(jnp.where(gm & (pg == ptop), lane, 1e9), axis=-1, keepdims=True)
    lo = MOE_GROUPS + MOE_EPG * gidx
    em = (lane >= lo) & (lane < lo + MOE_EPG)
    le = jnp.where(em, lg, NEG_INF)
    ee = jnp.where(em, jnp.exp(le - jnp.max(le, axis=-1, keepdims=True)), 0.0)
    pe = ee / jnp.sum(ee, axis=-1, keepdims=True)
    v1 = jnp.max(jnp.where(em, pe, -1.0), axis=-1, keepdims=True)
    i1 = jnp.min(jnp.where(em & (pe == v1), lane, 1e9), axis=-1, keepdims=True)
    em2 = em & (lane != i1)
    v2 = jnp.max(jnp.where(em2, pe, -1.0), axis=-1, keepdims=True)
    i2 = jnp.min(jnp.where(em2 & (pe == v2), lane, 1e9), axis=-1, keepdims=True)
    den = v1 + v2
    w1 = v1 / den * ptop
    w2 = v2 / den * ptop
    e1 = i1 - MOE_GROUPS
    e2 = i2 - MOE_GROUPS
    m1 = lane == e1
    m2 = lane == e2
    oh = jnp.where(m1 | m2, 1.0, 0.0)
    cnt = jnp.dot(lt, oh.astype(BF16), preferred_element_type=F32) + carry
    r1 = jnp.sum(jnp.where(m1, cnt, 0.0), axis=-1, keepdims=True)
    r2 = jnp.sum(jnp.where(m2, cnt, 0.0), axis=-1, keepdims=True)
    route = jnp.where(lane == 0, e1, jnp.where(lane == 1, e2, jnp.where(lane == 2, w1, jnp.where(
        lane == 3, w2, jnp.where(lane == 4, r1, jnp.where(lane == 5, r2, 0.0))))))
    return route, carry + jnp.sum(oh, axis=0, keepdims=True)


def _post_mixer(i, x, y, g1, gn, sh2, sc2, wr_ref, br_ref, lt_ref, xo_ref, h2_ref, rt_ref, cnt_ref, carry):
    xn = x + g1 * y
    xo_ref[...] = xn
    h2 = _ada_norm(xn, gn, sh2, sc2)
    hb = h2.astype(BF16)
    hbf = hb.astype(F32)
    half = h2.shape[1] // 2
    h2_ref[...] = pltpu.pack_elementwise([h2[:, :half], h2[:, half:]], packed_dtype=BF16)
    hl = (h2 - hbf).astype(BF16)
    lg = (jnp.dot(hb, wr_ref[0], preferred_element_type=F32)
          + (jnp.dot(hb, wr_ref[1], preferred_element_type=F32) + jnp.dot(hl, wr_ref[0], preferred_element_type=F32))
          + br_ref[...])

    @pl.when(i == 0)
    def _():
        carry[...] = jnp.zeros_like(carry)

    route, newc = _route(lg, lt_ref[...], carry[...])
    rt_ref[...] = route
    carry[...] = newc
    cnt_ref[...] = newc


def _att_outproj_kernel(nlat, na_ref, wa_ref, cx_ref, xl_ref, xc_ref, w_ref, g1_ref, gn_ref, sh2_ref, sc2_ref,
                        wr_ref, br_ref, lt_ref, xo_ref, h2_ref, rt_ref, cnt_ref, carry):
    i = pl.program_id(0)
    lat = i < nlat
    mix = jnp.where(lat, jnp.concatenate([na_ref[...], wa_ref[...]], axis=1), cx_ref[...])
    y = jnp.dot(mix, w_ref[...], preferred_element_type=F32)
    x = jnp.where(lat, xl_ref[...], xc_ref[...])
    _post_mixer(i, x, y, g1_ref[...], gn_ref[...], sh2_ref[...], sc2_ref[...], wr_ref, br_ref, lt_ref,
                xo_ref, h2_ref, rt_ref, cnt_ref, carry)


def _router_weights(w_group, b_group, w_expert, b_expert):
    D = w_group.shape[0]
    pad = LANES - MOE_GROUPS - MOE_EXPERTS
    wr = jnp.concatenate([w_group, w_expert, jnp.zeros((D, pad), F32)], axis=1)
    br = jnp.concatenate([b_group, b_expert, jnp.zeros((pad,), F32)])[None, :]
    hi = wr.astype(BF16)
    lo = (wr - hi.astype(F32)).astype(BF16)
    return jnp.stack([hi, lo]), br


def _lower_tri(tm):
    r = np.arange(tm)
    return jnp.asarray((r[None, :] < r[:, None]).astype(np.float32), BF16)


def _post_specs(rw):
    D, tm = rw.D, rw.tm
    return ([pl.BlockSpec((1, D), lambda i: (0, 0)), _mod_spec(rw, 3), _mod_spec(rw, 4),
             pl.BlockSpec((2, D, LANES), lambda i: (0, 0, 0)), pl.BlockSpec((1, LANES), lambda i: (0, 0)),
             pl.BlockSpec((tm, tm), lambda i: (0, 0))],
            [pl.BlockSpec((tm, D), lambda i: (i, 0)), pl.BlockSpec((tm, D // 2), lambda i: (i, 0)),
             pl.BlockSpec((tm, LANES), lambda i: (i, 0)), pl.BlockSpec((1, LANES), lambda i: (0, 0))])


def _post_shapes(nrows, D):
    return [jax.ShapeDtypeStruct((nrows, D), F32), jax.ShapeDtypeStruct((nrows, D // 2), jnp.uint32),
            jax.ShapeDtypeStruct((nrows, LANES), F32), jax.ShapeDtypeStruct((1, LANES), F32)]


def _att_outproj(rw, na, wa, cx, xl, xc, w_out, mods, norm_ffn, wr, br):
    D, tm, nlat = rw.D, rw.tm, rw.nlat
    post_in, post_out = _post_specs(rw)
    latmap = lambda i: (jnp.minimum(i, nlat - 1), 0)
    ctxmap = lambda i: (jnp.maximum(i - nlat, 0), 0)
    return pl.pallas_call(
        functools.partial(_att_outproj_kernel, nlat),
        grid=(rw.ntot,),
        in_specs=[pl.BlockSpec((tm, 512), latmap), pl.BlockSpec((tm, 512), latmap), pl.BlockSpec((tm, D), ctxmap),
                  pl.BlockSpec((tm, D), latmap), pl.BlockSpec((tm, D), ctxmap),
                  pl.BlockSpec((D, D), lambda i: (0, 0)), _mod_spec(rw, 2)] + post_in,
        out_specs=post_out,
        out_shape=_post_shapes(rw.rows, D),
        scratch_shapes=[pltpu.VMEM((1, LANES), F32)],
        compiler_params=_cparams(1),
        name="att_outproj_router",
    )(na, wa, cx, xl, xc, w_out.astype(BF16), mods, norm_ffn[None, :], mods, mods, wr, br, _lower_tri(tm))


def _moe_kernel(te_ref, nu_ref, src_ref, nsrc_ref, hp_ref, w13_ref, w2_ref, o_ref, w13b, w2b, xa, xb):
    i = pl.program_id(0)
    prev = te_ref[jnp.maximum(i - 1, 0)]
    changed = (i == 0) | (te_ref[i] != prev)
    tg = xa.shape[0]

    @pl.when(changed)
    def _():
        w13b[...] = w13_ref[...].astype(BF16)
        w2b[...] = w2_ref[...].astype(BF16)

    @pl.when(i == 0)
    def _():
        def fetch(j, carry):
            xa[pl.ds(j, 1), :] = hp_ref[pl.ds(src_ref[0, j], 1), :]
            return carry

        lax.fori_loop(0, tg, fetch, 0, unroll=8)

    def step(cur, nxt):
        for j in range(tg):
            nxt[pl.ds(j, 1), :] = hp_ref[pl.ds(nsrc_ref[0, j], 1), :]
        ff = w2b.shape[0]
        half = cur.shape[1]
        w = cur[...]
        unpack = functools.partial(pltpu.unpack_elementwise, packed_dtype=BF16, unpacked_dtype=F32)
        x_lo = unpack(w, index=0).astype(BF16)
        x_hi = unpack(w, index=1).astype(BF16)
        a13 = (jnp.dot(x_lo, w13b[:half, :], preferred_element_type=F32)
               + jnp.dot(x_hi, w13b[half:, :], preferred_element_type=F32))
        act = _silu(a13[:, :ff]) * a13[:, ff:]
        o_ref[...] = jnp.dot(act.astype(BF16), w2b[...], preferred_element_type=F32).astype(BF16)

    used = i < nu_ref[0]

    @pl.when(used & (i % 2 == 0))
    def _():
        step(xa, xb)

    @pl.when(used & (i % 2 == 1))
    def _():
        step(xb, xa)

    @pl.when(i >= nu_ref[0])
    def _():
        o_ref[...] = jnp.zeros_like(o_ref)


def _moe(h2p, route, counts, w13, w2, layer):
    N = h2p.shape[0]
    D = 2 * h2p.shape[1]
    _, E, _, F2 = w13.shape
    tg = MOE_TILE
    nt = (2 * N) // tg + E
    e = route[:, 0:2].astype(jnp.int32)
    rank = route[:, 4:6].astype(jnp.int32)
    cnt = counts[0, :E].astype(jnp.int32)
    ntile_e = (cnt + tg - 1) // tg
    tile_end = jnp.cumsum(ntile_e)
    offs = (tile_end - ntile_e) * tg
    onehot = (e[:, :, None] == jnp.arange(E, dtype=jnp.int32)).astype(jnp.int32)
    dest = jnp.sum(onehot * offs, axis=-1) + rank
    src = jnp.zeros((nt * tg,), jnp.int32).at[dest.reshape(-1)].set(jnp.repeat(jnp.arange(N, dtype=jnp.int32), 2))
    tile_id = jnp.arange(nt, dtype=jnp.int32)
    nu = tile_end[-1:].astype(jnp.int32)
    te = jnp.sum((tile_end[None, :] <= jnp.minimum(tile_id, nu[0] - 1)[:, None]).astype(jnp.int32), axis=1)
    te = jnp.minimum(te, E - 1)
    ys = pl.pallas_call(
        _moe_kernel,
        grid_spec=pltpu.PrefetchScalarGridSpec(
            num_scalar_prefetch=2,
            grid=(nt,),
            in_specs=[pl.BlockSpec((None, 1, tg), lambda i, te, nu: (i, 0, 0), memory_space=pltpu.SMEM),
                      pl.BlockSpec((None, 1, tg), lambda i, te, nu: (jnp.minimum(i + 1, nt - 1), 0, 0),
                                   memory_space=pltpu.SMEM),
                      pl.BlockSpec((N, D // 2), lambda i, te, nu: (0, 0), pipeline_mode=pl.Buffered(1)),
                      pl.BlockSpec((None, None, D, F2), lambda i, te, nu: (layer, te[i], 0, 0)),
                      pl.BlockSpec((None, None, F2 // 2, D), lambda i, te, nu: (layer, te[i], 0, 0))],
            out_specs=pl.BlockSpec((tg, D), lambda i, te, nu: (i, 0)),
            scratch_shapes=[pltpu.VMEM((D, F2), BF16), pltpu.VMEM((F2 // 2, D), BF16),
                            pltpu.VMEM((tg, D // 2), jnp.uint32), pltpu.VMEM((tg, D // 2), jnp.uint32)]),
        out_shape=jax.ShapeDtypeStruct((nt * tg, D), BF16),
        compiler_params=_cparams(1, vmem=MOE_VMEM_LIMIT),
        name="moe_experts",
    )(te, nu, src.reshape(nt, 1, tg), src.reshape(nt, 1, tg), h2p, w13, w2)
    pick = lambda k: ys.at[dest[:, k]].get(mode="promise_in_bounds")
    return pick(0), pick(1)


def _moe_residual(x_ref, y1_ref, y2_ref, rt_ref, g2_ref):
    rt = rt_ref[...]
    f = rt[:, 2:3] * y1_ref[...].astype(F32) + rt[:, 3:4] * y2_ref[...].astype(F32)
    return x_ref[...] + g2_ref[...] * f


def _combine_kernel(x_ref, y1_ref, y2_ref, rt_ref, g2_ref, o_ref):
    o_ref[...] = _moe_residual(x_ref, y1_ref, y2_ref, rt_ref, g2_ref)


def _combine(rw, ntiles, xall, y1, y2, route, mods):
    D, tm = rw.D, rw.tm
    row = lambda i: (i, 0)
    return pl.pallas_call(
        _combine_kernel,
        grid=(ntiles,),
        in_specs=[pl.BlockSpec((tm, D), row), pl.BlockSpec((tm, D), row), pl.BlockSpec((tm, D), row),
                  pl.BlockSpec((tm, LANES), row), _mod_spec(rw, 5)],
        out_specs=pl.BlockSpec((tm, D), row),
        out_shape=jax.ShapeDtypeStruct((ntiles * tm, D), F32),
        compiler_params=_cparams(1),
        name="moe_combine",
    )(xall, y1, y2, route, mods)


NA_QROWS = 8
NA_KROWS = 16


def _na_bias(rpb, n_rb):
    H = rpb.shape[0]
    i = np.arange(GRID_W)
    c0 = np.clip(i - NA_KW // 2, 0, GRID_W - NA_KW)
    j = np.arange(GRID_W)
    colvalid = (j[None, :] >= c0[:, None]) & (j[None, :] < c0[:, None] + NA_KW)
    dc = np.clip(j[None, :] - i[:, None] + NA_KW - 1, 0, 2 * NA_KW - 2)
    onehot = ((dc[None] == np.arange(2 * NA_KW - 1)[:, None, None]) & colvalid[None]).astype(np.float32)
    tiles = jnp.einsum('hrc,cij->hrij', rpb.astype(F32), jnp.asarray(onehot), precision=lax.Precision.HIGHEST)
    tiles = tiles + jnp.asarray(np.where(colvalid, 0.0, NEG_INF).astype(np.float32))
    flat = tiles.transpose(0, 2, 1, 3).reshape(H, GRID_W, (2 * NA_KH - 1) * GRID_W)
    blocks = []
    for variant in range(3):
        for a in range(NA_QROWS):
            start = (max(a - 4, 0) + 4, a, min(a, 4))[variant]
            dr0 = start - a + 3
            neg = lambda n: jnp.full((H, GRID_W, n * GRID_W), NEG_INF, F32)
            blocks.append(jnp.concatenate([neg(start), flat[:, :, dr0 * GRID_W:(dr0 + NA_KH) * GRID_W],
                                           neg(NA_KROWS - NA_KH - start)], axis=-1))
    return jnp.stack(blocks, axis=1).reshape(H, 3, NA_QROWS * GRID_W, NA_KROWS * GRID_W)


def _softmax_pv(parts, extra=None, rc=64):
    m_rows = parts[0][0].shape[0]
    probs = [[] for _ in parts]
    inv_l = []
    for r0 in range(0, m_rows, rc):
        sc = []
        for s, _, bias_fn, valid_fn in parts:
            c = s[r0:r0 + rc]
            if bias_fn is not None:
                c = c + bias_fn(r0, rc)
            if valid_fn is not None:
                c = jnp.where(valid_fn(r0, rc), c, NEG_INF)
            sc.append(c)
        mx = functools.reduce(jnp.maximum, [jnp.max(c, axis=-1, keepdims=True) for c in sc])
        if extra is not None:
            mx = jnp.maximum(mx, extra[r0:r0 + rc])
        l = jnp.zeros_like(mx) if extra is None else jnp.exp(extra[r0:r0 + rc] - mx)
        for k, c in enumerate(sc):
            p = jnp.exp(c - mx)
            l = l + jnp.sum(p, axis=-1, keepdims=True)
            probs[k].append(p.astype(BF16))
        inv_l.append(1.0 / l)
    o = None
    for k, (_, v, _, _) in enumerate(parts):
        pv = jnp.dot(jnp.concatenate(probs[k], axis=0), v, preferred_element_type=F32)
        o = pv if o is None else o + pv
    return o * jnp.concatenate(inv_l, axis=0)


def _nt(a, b):
    return lax.dot_general(a, b, (((1,), (1,)), ((), ())), preferred_element_type=F32)


def _na_kernel(q_ref, k0, k1, k2, k3, v0, v1, v2, v3, kc_ref, vc_ref, bias_ref, o_ref):
    q2 = q_ref[...]
    kw = jnp.concatenate([k0[...], k1[...], k2[...], k3[...]], axis=0)
    vw = jnp.concatenate([v0[...], v1[...], v2[...], v3[...]], axis=0)
    kc = kc_ref[...]
    vc = vc_ref[...]
    lane = lax.broadcasted_iota(jnp.int32, q2.shape, 1)
    out = jnp.zeros(q2.shape, F32)
    for hh in range(2):
        m = (lane < HEAD_DIM) if hh == 0 else (lane >= HEAD_DIM)
        qm = jnp.where(m, q2, jnp.zeros_like(q2))
        o = _softmax_pv([(_nt(qm, kw), vw, lambda r0, rc, hh=hh: bias_ref[hh, r0:r0 + rc, :], None),
                         (_nt(qm, kc), vc, None, None)], rc=32)
        out = jnp.where(m, o, out)
    o_ref[...] = out.astype(BF16)


def _na_attention(rw, qkv, rpb):
    B, T, C = rw.B, rw.T, rw.C
    tq = NA_QROWS * GRID_W
    tk = tq // 2
    n_rb = T // tq
    nkb = T // tk
    assert T % tq == 0 and n_rb >= 2 and (B * T) % C == 0
    bias = _na_bias(rpb, n_rb)
    ctxrow = (B * T) // C

    def kvspec(j, col):
        return pl.BlockSpec((tk, LANES), lambda p, rb, b: (b * nkb + jnp.clip(2 * rb - 1 + j, 0, nkb - 1), col + p))

    return pl.pallas_call(
        _na_kernel,
        grid=(4, n_rb, B),
        in_specs=[pl.BlockSpec((tq, LANES), lambda p, rb, b: (b * n_rb + rb, p))]
        + [kvspec(j, 4) for j in range(4)] + [kvspec(j, 8) for j in range(4)]
        + [pl.BlockSpec((C, LANES), lambda p, rb, b: (ctxrow + b, 4 + p)),
           pl.BlockSpec((C, LANES), lambda p, rb, b: (ctxrow + b, 8 + p)),
           pl.BlockSpec((2, None, tq, 2 * tq),
                        lambda p, rb, b: (p, jnp.where(rb == 0, 0, jnp.where(rb == n_rb - 1, 2, 1)), 0, 0))],
        out_specs=pl.BlockSpec((tq, LANES), lambda p, rb, b: (b * n_rb + rb, p)),
        out_shape=jax.ShapeDtypeStruct((B * T, 4 * LANES), BF16),
        compiler_params=_cparams(3),
        name="neighbourhood_attention",
    )(qkv, *([qkv] * 10), bias)


WA_QBLOCKS = 2


def _wa_kernel(nb, sink_ref, q_ref, k0, k1, k2, k3, v0, v1, v2, v3, kx_ref, vx_ref, o_ref):
    step = pl.program_id(1)
    blk = WA_BLOCK
    kblocks, vblocks = (k0, k1, k2, k3), (v0, v1, v2, v3)
    lane = lax.broadcasted_iota(jnp.int32, (blk, LANES), 1)
    zero = jnp.zeros((blk, LANES), BF16)
    qi = lax.broadcasted_iota(jnp.int32, (blk, 3 * blk), 0)
    ks = lax.broadcasted_iota(jnp.int32, (blk, 3 * blk), 1)
    for qb in range(WA_QBLOCKS):
        n = step * WA_QBLOCKS + qb
        lo = jnp.where(n > 0, 0, blk)
        hi = jnp.where(n < nb - 1, 3 * blk, 2 * blk)
        valid = (ks >= qi) & (ks <= qi + 2 * blk) & (ks >= lo) & (ks < hi)
        band_mask = jnp.where(valid, 0.0, NEG_INF)
        rows = slice(qb * blk, (qb + 1) * blk)
        for kv in range(2):
            parts = []
            for pr in range(2):
                c0 = kv * 2 * LANES + pr * LANES
                qp = q_ref[rows, c0:c0 + LANES]
                parts += [jnp.where(lane < HEAD_DIM, qp, zero), jnp.where(lane >= HEAD_DIM, qp, zero)]
            qs = jnp.concatenate(parts, axis=0)
            cs = slice(kv * LANES, (kv + 1) * LANES)
            kb = jnp.concatenate([r[:, cs] for r in kblocks[qb:qb + 3]], axis=0)
            vb = jnp.concatenate([r[:, cs] for r in vblocks[qb:qb + 3]], axis=0)
            sink = jnp.concatenate([jnp.full((blk, 1), sink_ref[kv * 4 + g], F32) for g in range(4)], axis=0)
            o = _softmax_pv([(_nt(qs, kb), vb, lambda r0, rc, m=band_mask: m[r0 % blk:r0 % blk + rc], None),
                             (_nt(qs, kx_ref[:, cs]), vx_ref[:, cs], None, None)], extra=sink, rc=64)
            c0 = kv * 2 * LANES
            o_ref[rows, c0:c0 + LANES] = jnp.where(lane < HEAD_DIM, o[0:blk], o[blk:2 * blk]).astype(BF16)
            o_ref[rows, c0 + LANES:c0 + 2 * LANES] = jnp.where(
                lane < HEAD_DIM, o[2 * blk:3 * blk], o[3 * blk:4 * blk]).astype(BF16)


def _wa_attention(rw, qkv, sink):
    B, T, C = rw.B, rw.T, rw.C
    blk = WA_BLOCK
    nb = T // blk
    nq = WA_QBLOCKS
    assert nb % nq == 0
    ctxrow = (B * T) // C

    def kvspec(j, col):
        return pl.BlockSpec((blk, 2 * LANES), lambda b, s: (b * nb + jnp.clip(nq * s - 1 + j, 0, nb - 1), col))

    return pl.pallas_call(
        functools.partial(_wa_kernel, nb),
        grid=(B, nb // nq),
        in_specs=[pl.BlockSpec(memory_space=pltpu.SMEM),
                  pl.BlockSpec((nq * blk, 4 * LANES), lambda b, s: (b * (nb // nq) + s, 3))]
        + [kvspec(j, 8) for j in range(nq + 2)] + [kvspec(j, 9) for j in range(nq + 2)]
        + [pl.BlockSpec((C, 2 * LANES), lambda b, s: (ctxrow + b, 8)),
           pl.BlockSpec((C, 2 * LANES), lambda b, s: (ctxrow + b, 9))],
        out_specs=pl.BlockSpec((nq * blk, 4 * LANES), lambda b, s: (b * (nb // nq) + s, 0)),
        out_shape=jax.ShapeDtypeStruct((B * T, 4 * LANES), BF16),
        compiler_params=_cparams(2),
        name="window_attention",
    )(sink.astype(F32), qkv, *([qkv] * (2 * nq + 6)))


def _ctx_attn_kernel(sink_ref, t_ref, o_ref):
    C = t_ref.shape[0]
    lane = lax.broadcasted_iota(jnp.int32, (C, LANES), 1)
    zero = jnp.zeros((C, LANES), BF16)

    def pair(q2, k2, v2, sinks):
        out = jnp.zeros((C, LANES), F32)
        for hh in range(2):
            m = (lane < HEAD_DIM) if hh == 0 else (lane >= HEAD_DIM)
            extra = None if sinks is None else jnp.full((C, 1), sinks[hh], F32)
            o = _softmax_pv([(_nt(jnp.where(m, q2, zero), k2), v2, None, None)], extra=extra, rc=64)
            out = jnp.where(m, o, out)
        return out.astype(BF16)

    for p in range(4):
        c = p * LANES
        o_ref[:, c:c + LANES] = pair(t_ref[:, c:c + LANES], t_ref[:, 512 + c:640 + c], t_ref[:, 1024 + c:1152 + c], None)
    for kv in range(2):
        kd = t_ref[:, 2048 + kv * LANES:2176 + kv * LANES]
        vd = t_ref[:, 2304 + kv * LANES:2432 + kv * LANES]
        for pr in range(2):
            c = kv * 256 + pr * LANES
            h0 = kv * 4 + pr * 2
            o_ref[:, 512 + c:640 + c] = pair(t_ref[:, 1536 + c:1664 + c], kd, vd, (sink_ref[h0], sink_ref[h0 + 1]))


def _ctx_attention(rw, qkv, sink):
    B, T, C = rw.B, rw.T, rw.C
    ctxrow = (B * T) // C
    return pl.pallas_call(
        _ctx_attn_kernel,
        grid=(B,),
        in_specs=[pl.BlockSpec(memory_space=pltpu.SMEM),
                  pl.BlockSpec((C, qkv.shape[1]), lambda b: (ctxrow + b, 0))],
        out_specs=pl.BlockSpec((C, 8 * LANES), lambda b: (b, 0)),
        out_shape=jax.ShapeDtypeStruct((B * C, 8 * LANES), BF16),
        compiler_params=_cparams(1),
        name="context_attention",
    )(sink.astype(F32), qkv)


S5_Q = 16
CONV_TILE = 256
CONV_HALO = 16


def _ssm_inproj_kernel(x_ref, y1_ref, y2_ref, rt_ref, g2_ref, g_ref, sh_ref, sc_ref, w_ref,
                       xo_ref, z_ref, xbc_ref, u_ref, uj_ref, dt_ref, h_scr, u_scr):
    xn = _moe_residual(x_ref, y1_ref, y2_ref, rt_ref, g2_ref)
    xo_ref[...] = xn
    h_scr[...] = _ada_norm(xn, g_ref[...], sh_ref[...], sc_ref[...]).astype(BF16)

    def mm(c0, n):
        return jnp.dot(h_scr[...], w_ref[:, c0:c0 + n], preferred_element_type=F32)

    for c in range(4):
        z_ref[:, c * 256:(c + 1) * 256] = mm(c * 256, 256).astype(BF16)
    for c in range(6):
        xbc_ref[:, c * 256:(c + 1) * 256] = mm(1024 + c * 256, 256).astype(BF16)
    for c in range(2):
        y = mm(2560 + c * 256, 256)
        u_ref[:, c * 256:(c + 1) * 256] = y.astype(BF16)
        u_scr[2 * c] = y[:, :LANES]
        u_scr[2 * c + 1] = y[:, LANES:]
    dt_ref[...] = mm(3072, LANES)
    nchunk = u_scr.shape[1] // S5_Q
    for j in range(S5_Q):
        for t in range(u_scr.shape[0]):
            uj_ref[j, :, t * LANES:(t + 1) * LANES] = u_scr[t, pl.ds(j, nchunk, stride=S5_Q), :].astype(BF16)


def _ssm_inproj(rw, xall, y1, y2, route, prev_mods, mods, norm_g, w_in):
    D, tm = rw.D, rw.tm
    w = jnp.concatenate([w_in[:, 0:2560], w_in[:, 2592:3104], w_in[:, 2560:2592], jnp.zeros((D, LANES - 32), F32)],
                        axis=1).astype(BF16)
    row = lambda i: (i, 0)
    return pl.pallas_call(
        _ssm_inproj_kernel,
        grid=(rw.ntot,),
        in_specs=[pl.BlockSpec((tm, D), row), pl.BlockSpec((tm, D), row), pl.BlockSpec((tm, D), row),
                  pl.BlockSpec((tm, LANES), row), _mod_spec(rw, 5), pl.BlockSpec((1, D), lambda i: (0, 0)),
                  _mod_spec(rw, 0), _mod_spec(rw, 1), pl.BlockSpec((D, 3200), lambda i: (0, 0))],
        out_specs=[pl.BlockSpec((tm, D), row),
                   pl.BlockSpec((tm, 1024), row), pl.BlockSpec((tm, 1536), row), pl.BlockSpec((tm, 512), row),
                   pl.BlockSpec((S5_Q, tm // S5_Q, 512), lambda i: (0, i, 0)), pl.BlockSpec((tm, LANES), row)],
        out_shape=[jax.ShapeDtypeStruct((rw.rows, D), F32),
                   jax.ShapeDtypeStruct((rw.rows, 1024), BF16), jax.ShapeDtypeStruct((rw.rows, 1536), BF16),
                   jax.ShapeDtypeStruct((rw.rows, 512), BF16),
                   jax.ShapeDtypeStruct((S5_Q, rw.rows // S5_Q, 512), BF16),
                   jax.ShapeDtypeStruct((rw.rows, LANES), F32)],
        scratch_shapes=[pltpu.VMEM((tm, D), BF16), pltpu.VMEM((512 // LANES, tm, LANES), F32)],
        compiler_params=_cparams(1),
        name="ssm_inproj",
    )(xall, y1, y2, route, prev_mods, norm_g[None, :], mods, mods, w)


def _softplus(x):
    return jnp.maximum(x, 0.0) + jnp.log(1.0 + jnp.exp(-jnp.abs(x)))


def _conv_kernel(lat_tiles, tpb, cpb, x_ref, pv_ref, nx_ref, w_ref, b_ref, dtr_ref, dtb_ref, act_ref, dt_ref):
    i = pl.program_id(0)
    is_lat = i < lat_tiles
    pos = jnp.where(is_lat, i % tpb, (i - lat_tiles) % cpb)
    last_pos = jnp.where(is_lat, tpb - 1, cpb - 1)
    x = x_ref[...].astype(F32)
    tc = x.shape[0]
    prev_row = jnp.where(pos == 0, 0.0, pv_ref[...].astype(F32)[CONV_HALO - 1:CONV_HALO, :])
    next_row = jnp.where(pos == last_pos, 0.0, nx_ref[...].astype(F32)[0:1, :])
    row = lax.broadcasted_iota(jnp.int32, x.shape, 0)
    xm1 = jnp.where(row == 0, prev_row, pltpu.roll(x, 1, 0))
    xp1 = jnp.where(row == tc - 1, next_row, pltpu.roll(x, tc - 1, 0))
    y = w_ref[0:1, :] * xm1 + w_ref[1:2, :] * x + w_ref[2:3, :] * xp1 + b_ref[...]
    act_ref[...] = _silu(y).astype(BF16)
    sp = _softplus(dtr_ref[...] + dtb_ref[...])
    dt_ref[0] = sp
    dt_ref[1] = pltpu.roll(sp, LANES - 16, 1)


def _ssm_conv(rw, xbc, dtr, conv_w, conv_b, dt_bias):
    B, T, C = rw.B, rw.T, rw.C
    tc = CONV_TILE
    assert T % tc == 0 and C % tc == 0
    lat_tiles, tpb, cpb = (B * T) // tc, T // tc, C // tc
    ntiles = rw.rows // tc
    hpt = tc // CONV_HALO
    nhalo = rw.rows // CONV_HALO
    W = xbc.shape[1]
    dtb = jnp.concatenate([dt_bias.reshape(-1), jnp.zeros((LANES - 32,), F32)])[None, :]
    row = lambda i: (i, 0)
    return pl.pallas_call(
        functools.partial(_conv_kernel, lat_tiles, tpb, cpb),
        grid=(ntiles,),
        in_specs=[pl.BlockSpec((tc, W), row),
                  pl.BlockSpec((CONV_HALO, W), lambda i: (jnp.maximum(i * hpt - 1, 0), 0)),
                  pl.BlockSpec((CONV_HALO, W), lambda i: (jnp.minimum((i + 1) * hpt, nhalo - 1), 0)),
                  pl.BlockSpec((3, W), lambda i: (0, 0)), pl.BlockSpec((1, W), lambda i: (0, 0)),
                  pl.BlockSpec((tc, LANES), row), pl.BlockSpec((1, LANES), lambda i: (0, 0))],
        out_specs=[pl.BlockSpec((tc, W), row), pl.BlockSpec((2, tc, LANES), lambda i: (0, i, 0))],
        out_shape=[jax.ShapeDtypeStruct((rw.rows, W), BF16), jax.ShapeDtypeStruct((2, rw.rows, LANES), F32)],
        compiler_params=_cparams(1),
        name="ssm_conv",
    )(xbc, xbc, xbc, conv_w, conv_b[None, :], dtr, dtb)


def _ssd_kernel(actf_ref, actb_ref, dtf_ref, dtb_ref, tri_ref, a_ref, yf_ref, yb_ref, hst):
    @pl.when(pl.program_id(1) == 0)
    def _():
        hst[...] = jnp.zeros_like(hst)

    _ssd_chunk(actf_ref, dtf_ref, tri_ref[0], a_ref[0], yf_ref, hst.at[0])
    _ssd_chunk(actb_ref, dtb_ref, tri_ref[1], a_ref[1], yb_ref, hst.at[1])


def _ssd_chunk(act_ref, dt_ref, tri, avec, y_ref, hst):
    q = SSD_CHUNK
    dt = dt_ref[...]
    da = dt * avec
    acs = jnp.dot(tri, da, preferred_element_type=F32, precision=lax.Precision.HIGHEST)
    tot = jnp.sum(da, axis=0, keepdims=True)
    acs_t = acs.T
    dt_t = dt.T
    eacs = jnp.exp(acs)
    wend = jnp.exp(tot - acs) * dt
    etot = jnp.exp(tot)
    mask = tri > 0.5
    left = lax.broadcasted_iota(jnp.int32, (q, LANES), 1) < HEAD_DIM
    left1 = lax.broadcasted_iota(jnp.int32, (1, LANES), 1) < HEAD_DIM
    for g in range(2):
        bg = act_ref[:, 1024 + g * 128:1152 + g * 128]
        cg = act_ref[:, 1280 + g * 128:1408 + g * 128]
        cb = _nt(cg, bg)
        hin = hst[:, g * 512:(g + 1) * 512]
        yoff = jnp.dot(cg, hin.astype(BF16), preferred_element_type=F32)
        xw, dec = [], []
        for pr in range(4):
            h_a = g * 8 + pr * 2
            h_b = h_a + 1
            c0 = h_a * HEAD_DIM
            x2 = act_ref[:, c0:c0 + LANES]
            outs = []
            for h in (h_a, h_b):
                seg = acs[:, h:h + 1] - acs_t[h:h + 1, :]
                w = cb * jnp.exp(jnp.where(mask, seg, NEG_INF)) * dt_t[h:h + 1, :]
                outs.append(jnp.dot(w.astype(BF16), x2, preferred_element_type=F32))
            yd = jnp.where(left, outs[0], outs[1])
            sc = jnp.where(left, eacs[:, h_a:h_a + 1], eacs[:, h_b:h_b + 1])
            y_ref[:, c0:c0 + LANES] = (yd + yoff[:, pr * LANES:(pr + 1) * LANES] * sc).astype(BF16)
            wsc = jnp.where(left, wend[:, h_a:h_a + 1], wend[:, h_b:h_b + 1])
            xw.append((x2.astype(F32) * wsc).astype(BF16))
            dec.append(jnp.where(left1, etot[:, h_a:h_a + 1], etot[:, h_b:h_b + 1]))
        bg_t = bg.astype(F32).T.astype(BF16)
        snew = jnp.dot(bg_t, jnp.concatenate(xw, axis=1), preferred_element_type=F32)
        hst[:, g * 512:(g + 1) * 512] = hin * jnp.concatenate(dec, axis=1) + snew


def _ssd(rw, act, dt2, a_log):
    B, T, C = rw.B, rw.T, rw.C
    q = SSD_CHUNK
    nct, nlt = C // q, T // q
    ctx0 = (B * T) // q
    r = np.arange(q)
    tri = jnp.asarray(np.stack([r[None, :] <= r[:, None], r[None, :] >= r[:, None]]).astype(np.float32))
    avec = jnp.concatenate([-jnp.exp(a_log.astype(F32)), jnp.zeros((2, LANES - a_log.shape[1]), F32)], axis=1)[:, None, :]

    def blk(d, b, s):
        kc = s if d == 0 else nct - 1 - s
        kl = s - nct if d == 0 else nlt - 1 - (s - nct)
        return jnp.where(s < nct, ctx0 + b * nct + kc, b * nlt + kl)

    aspec = lambda d: pl.BlockSpec((q, act.shape[1]), lambda b, s: (blk(d, b, s), 0))
    dspec = lambda d: pl.BlockSpec((None, q, LANES), lambda b, s: (d, blk(d, b, s), 0))
    yspec = lambda d: pl.BlockSpec((q, 1024), lambda b, s: (blk(d, b, s), 0))
    return pl.pallas_call(
        _ssd_kernel,
        grid=(B, nct + nlt),
        in_specs=[aspec(0), aspec(1), dspec(0), dspec(1),
                  pl.BlockSpec((2, q, q), lambda b, s: (0, 0, 0)),
                  pl.BlockSpec((2, 1, LANES), lambda b, s: (0, 0, 0))],
        out_specs=[yspec(0), yspec(1)],
        out_shape=[jax.ShapeDtypeStruct((rw.rows, 1024), BF16)] * 2,
        scratch_shapes=[pltpu.VMEM((2, q, 1024), F32)],
        compiler_params=_cparams(2),
        name="ssd_scan",
    )(act, act, dt2, dt2, tri, avec)


def _cmul(ar, ai, br, bi):
    return ar * br - ai * bi, ar * bi + ai * br


def _s5_weight_kernel(lre_ref, lim_ref, ls_ref, bre_ref, bim_ref, cre_ref, cim_ref,
                      wsr_ref, wsi_ref, wor_ref, woi_ref, kt_ref, are_ref, aim_ref):
    lre, lim = lre_ref[...], lim_ref[...]
    step = jnp.exp(ls_ref[...])
    er, ei = lre * step, lim * step
    npow = 24
    p = lax.broadcasted_iota(jnp.int32, (1, npow, 1), 1).astype(F32)
    mag = jnp.exp(p * er)
    pre, pim = mag * jnp.cos(p * ei), mag * jnp.sin(p * ei)
    a_re, a_im = pre[:, 1:2, :], pim[:, 1:2, :]
    den = lre * lre + lim * lim
    q_re = ((a_re - 1.0) * lre + a_im * lim) / den
    q_im = (a_im * lre - (a_re - 1.0) * lim) / den
    bb_re, bb_im = _cmul(q_re, q_im, bre_ref[...], bim_ref[...])
    c_re, c_im = cre_ref[...], cim_ref[...]
    ws_r, ws_i, wo_r, wo_i, ca_r, ca_i = [], [], [], [], [], []
    for t in range(S5_Q):
        r, i = _cmul(bb_re, bb_im, pre[:, t:t + 1, :], pim[:, t:t + 1, :])
        ws_r.append(r)
        ws_i.append(i)
        r, i = _cmul(c_re, c_im, pre[:, t:t + 1, :], pim[:, t:t + 1, :])
        ca_r.append(r)
        ca_i.append(i)
        r, i = _cmul(c_re, c_im, pre[:, t + 1:t + 2, :], pim[:, t + 1:t + 2, :])
        wo_r.append(r)
        wo_i.append(-i)
    cat = lambda xs: jnp.concatenate(xs, axis=1)
    wsr_ref[...] = cat(ws_r)
    wsi_ref[...] = cat(ws_i)
    wor_ref[...] = cat(wo_r)
    woi_ref[...] = cat(wo_i)
    bdot = lambda a, b: lax.dot_general(a, b, (((2,), (2,)), ((0,), (0,))), preferred_element_type=F32,
                                        precision=lax.Precision.HIGHEST)
    kt_ref[...] = bdot(cat(ca_r), bb_re) - bdot(cat(ca_i), bb_im)
    are_ref[...] = pre[:, S5_Q:S5_Q + 1, :]
    aim_ref[...] = pim[:, S5_Q:S5_Q + 1, :]


def _s5_weights(lam_re, lam_im, log_step, b_re, b_im, c_re, c_im):
    nd, ng, ns = lam_re.shape
    G = nd * ng
    ch = S5_GROUP
    gb = 8
    qc = S5_Q * ch
    f = lambda a: a.astype(F32)
    args = (f(lam_re).reshape(G, 1, ns), f(lam_im).reshape(G, 1, ns), f(log_step).reshape(G, 1, 1),
            f(b_re).reshape(G, ns, ch).transpose(0, 2, 1), f(b_im).reshape(G, ns, ch).transpose(0, 2, 1),
            f(c_re).reshape(G, ch, ns), f(c_im).reshape(G, ch, ns))
    spec = lambda a: pl.BlockSpec((gb,) + a.shape[1:], lambda i: (i, 0, 0))
    oshape = [jax.ShapeDtypeStruct((G, qc, ns), F32)] * 4 + [jax.ShapeDtypeStruct((G, qc, ch), F32)] \
        + [jax.ShapeDtypeStruct((G, 1, ns), F32)] * 2
    wsr, wsi, wor, woi, kt, a_re, a_im = pl.pallas_call(
        _s5_weight_kernel,
        grid=(G // gb,),
        in_specs=[spec(a) for a in args],
        out_specs=[pl.BlockSpec((gb,) + s.shape[1:], lambda i: (i, 0, 0)) for s in oshape],
        out_shape=oshape,
        compiler_params=_cparams(1),
        name="s5_weights",
    )(*args)

    def by_dir(w, flip_dir):
        w = w.reshape(nd, ng, S5_Q, ch, ns)
        w = jnp.stack([jnp.flip(w[d], axis=1) if d == flip_dir else w[d] for d in range(nd)])
        return w.reshape(nd, ng, qc, ns)

    def pack(w):
        z = jnp.zeros_like(w)
        even = (np.arange(ng) % 2 == 0)[None, :, None, None]
        return jnp.where(even, jnp.concatenate([w, z], axis=-1), jnp.concatenate([z, w], axis=-1)).astype(BF16)

    ws_r, ws_i = pack(by_dir(wsr, 0)), pack(by_dir(wsi, 0))
    wo_r, wo_i = pack(by_dir(wor, 1)), pack(by_dir(woi, 1))
    k = kt.astype(BF16).reshape(nd, ng, S5_Q, ch, ch).transpose(0, 1, 4, 2, 3)
    kf = k.reshape(nd, ng, ch, qc)
    kb = jnp.flip(k, axis=3).reshape(nd, ng, ch, qc)
    rows_f, rows_b = [], []
    for j in range(S5_Q):
        z_f = jnp.zeros((ng, ch, j * ch), BF16)
        z_b = jnp.zeros((ng, ch, (S5_Q - 1 - j) * ch), BF16)
        rows_f.append(jnp.concatenate([z_f, kf[0, :, :, :(S5_Q - j) * ch]], axis=-1))
        rows_b.append(jnp.concatenate([kb[1, :, :, (S5_Q - 1 - j) * ch:], z_b], axis=-1))
    bt = jnp.stack([jnp.stack(rows_f, axis=1), jnp.stack(rows_b, axis=1)]).reshape(nd, ng, qc, qc)
    pair = lambda a: a.reshape(nd, ng // 2, 1, 2 * ns)
    return bt, ws_r, ws_i, wo_r, wo_i, pair(a_re), pair(a_im)


S5_GB = LANES // S5_GROUP


def _s5_kernel(B, nct, nlt, uj_ref, perm_ref, bt_ref, wsr_ref, wsi_ref, wor_ref, woi_ref, are_ref, aim_ref, yj_ref,
               x_scr, y_scr, s_re, s_im):
    gb, npair, qc = S5_GB, S5_GB // 2, S5_Q * S5_GROUP
    lhs = jnp.concatenate([uj_ref[j] for j in range(S5_Q)], axis=1)
    for m in range(gb):
        x_scr[:, m * qc:(m + 1) * qc] = jnp.dot(lhs, perm_ref[:, m * qc:(m + 1) * qc],
                                                preferred_element_type=F32).astype(BF16)
    xg = lambda g: x_scr[:, g * qc:(g + 1) * qc]
    for d in range(2):
        for pr in range(npair):
            for dst, w_ref in ((s_re, wsr_ref), (s_im, wsi_ref)):
                dst[d, pr] = (jnp.dot(xg(2 * pr), w_ref[d, 2 * pr], preferred_element_type=F32)
                              + jnp.dot(xg(2 * pr + 1), w_ref[d, 2 * pr + 1], preferred_element_type=F32))
    chains = [(d, pr, b) for d in range(2) for pr in range(npair) for b in range(B)]
    coef = {(d, pr): (are_ref[d, pr], aim_ref[d, pr]) for d in range(2) for pr in range(npair)}
    ctx0 = B * nlt

    def body(s, carry):
        in_ctx = s < nct
        rows = {}
        for d in range(2):
            kc = s if d == 0 else nct - 1 - s
            kl = s - nct if d == 0 else nlt - 1 - (s - nct)
            for b in range(B):
                rows[(d, b)] = pl.ds(jnp.where(in_ctx, ctx0 + b * nct + kc, b * nlt + kl), 1)
        contrib = [(s_re[d, pr, rows[(d, b)], :], s_im[d, pr, rows[(d, b)], :]) for d, pr, b in chains]
        new = []
        for (d, pr, b), (hr, hi), (sr, si) in zip(chains, carry, contrib):
            ar, ai = coef[(d, pr)]
            s_re[d, pr, rows[(d, b)], :] = hr
            s_im[d, pr, rows[(d, b)], :] = hi
            new.append((ar * hr - ai * hi + sr, ar * hi + ai * hr + si))
        return tuple(new)

    zero = jnp.zeros((1, LANES), F32)
    lax.fori_loop(0, nct + nlt, body, tuple((zero, zero) for _ in chains))
    for g in range(gb):
        acc = None
        for d in range(2):
            t = (jnp.dot(xg(g), bt_ref[d, g], preferred_element_type=F32)
                 + _nt(s_re[d, g // 2].astype(BF16), wor_ref[d, g])
                 + _nt(s_im[d, g // 2].astype(BF16), woi_ref[d, g]))
            acc = t if acc is None else acc + t
        y_scr[:, g * qc:(g + 1) * qc] = acc.astype(BF16)
    for i in range(S5_Q):
        yj_ref[i] = _nt(y_scr[...], perm_ref[i * LANES:(i + 1) * LANES, :]).astype(BF16)


def _s5(rw, uj, weights):
    B, T, C = rw.B, rw.T, rw.C
    bt, ws_r, ws_i, wo_r, wo_i, a_re, a_im = weights
    ng = bt.shape[1]
    q, gb = S5_Q, S5_GB
    nct, nlt = C // q, T // q
    nrow = uj.shape[1]
    qc = q * S5_GROUP
    k = gb * qc
    idx = np.arange(k)
    j, m, c = idx // LANES, (idx % LANES) // S5_GROUP, idx % S5_GROUP
    perm = np.zeros((k, k), np.float32)
    perm[idx, m * qc + j * S5_GROUP + c] = 1.0
    once = dict(pipeline_mode=pl.Buffered(1))
    wspec = lambda n: pl.BlockSpec((2, gb, qc, n), lambda i: (0, i, 0, 0), **once)
    aspec = pl.BlockSpec((2, gb // 2, 1, LANES), lambda i: (0, i, 0, 0))
    return pl.pallas_call(
        functools.partial(_s5_kernel, B, nct, nlt),
        grid=(ng // gb,),
        in_specs=[pl.BlockSpec((q, nrow, LANES), lambda i: (0, 0, i), **once),
                  pl.BlockSpec((k, k), lambda i: (0, 0), **once),
                  wspec(qc), wspec(LANES), wspec(LANES), wspec(LANES), wspec(LANES), aspec, aspec],
        out_specs=pl.BlockSpec((q, nrow, LANES), lambda i: (0, 0, i)),
        out_shape=jax.ShapeDtypeStruct(uj.shape, BF16),
        scratch_shapes=[pltpu.VMEM((nrow, k), BF16), pltpu.VMEM((nrow, k), BF16),
                        pltpu.VMEM((2, gb // 2, nrow, LANES), F32), pltpu.VMEM((2, gb // 2, nrow, LANES), F32)],
        compiler_params=_cparams(1),
        name="s5_scan",
    )(uj, jnp.asarray(perm, BF16), bt, ws_r, ws_i, wo_r, wo_i, a_re, a_im)


def _gelu_tanh(x):
    return 0.5 * x * (1.0 + jnp.tanh(math.sqrt(2.0 / math.pi) * (x + 0.044715 * (x * x * x))))


def _ssm_outproj_kernel(y0_ref, y1_ref, xs_ref, z_ref, v_ref, u_ref, dsk_ref, nw_ref, s5d_ref, gw_ref, gb_ref,
                        x_ref, w_ref, g1_ref, gn_ref, sh2_ref, sc2_ref, wr_ref, br_ref, lt_ref,
                        xo_ref, h2_ref, rt_ref, cnt_ref, carry, v_scr):
    i = pl.program_id(0)
    y = y0_ref[...].astype(F32) + y1_ref[...].astype(F32) + dsk_ref[...] * xs_ref[...].astype(F32)
    y = _rms(y * _silu(z_ref[...].astype(F32))) * nw_ref[...]
    ntile = v_scr.shape[0]
    nchunk = v_scr.shape[1] // S5_Q
    for j in range(S5_Q):
        for t in range(ntile):
            v_scr[t, pl.ds(j, nchunk, stride=S5_Q), :] = v_ref[j, :, t * LANES:(t + 1) * LANES].astype(F32)
    s5_y = jnp.concatenate([v_scr[t] for t in range(ntile)], axis=1)
    v = _gelu_tanh(s5_y + s5d_ref[...] * u_ref[...].astype(F32))
    v = v * _sigmoid(jnp.dot(v.astype(BF16), gw_ref[...], preferred_element_type=F32) + gb_ref[...])
    mix = jnp.concatenate([y, v], axis=1).astype(BF16)
    yo = jnp.dot(mix, w_ref[...], preferred_element_type=F32)
    _post_mixer(i, x_ref[...], yo, g1_ref[...], gn_ref[...], sh2_ref[...], sc2_ref[...], wr_ref, br_ref, lt_ref,
                xo_ref, h2_ref, rt_ref, cnt_ref, carry)


def _ssm_outproj(rw, ssd_y, act, z, s5_y, u, d_skip, norm_w, s5_d, glu_w, glu_b, xall, w_out, mods, norm_ffn, wr, br):
    D, tm = rw.D, rw.tm
    ntiles = rw.nlat
    post_in, post_out = _post_specs(rw)
    row = lambda i: (i, 0)
    vec = lambda n: pl.BlockSpec((1, n), lambda i: (0, 0))
    dsk = jnp.repeat(d_skip.astype(F32), HEAD_DIM)[None, :]
    return pl.pallas_call(
        _ssm_outproj_kernel,
        grid=(ntiles,),
        in_specs=[pl.BlockSpec((tm, 1024), row), pl.BlockSpec((tm, 1024), row),
                  pl.BlockSpec((tm, 1024), row), pl.BlockSpec((tm, 1024), row),
                  pl.BlockSpec((S5_Q, tm // S5_Q, 512), lambda i: (0, i, 0)),
                  pl.BlockSpec((tm, 512), row), vec(1024), vec(1024), vec(512),
                  pl.BlockSpec((512, 512), lambda i: (0, 0)), vec(512),
                  pl.BlockSpec((tm, D), row), pl.BlockSpec((1536, D), lambda i: (0, 0)), _mod_spec(rw, 2)] + post_in,
        out_specs=post_out,
        out_shape=_post_shapes(ntiles * tm, D),
        scratch_shapes=[pltpu.VMEM((1, LANES), F32), pltpu.VMEM((512 // LANES, tm, LANES), F32)],
        compiler_params=_cparams(1),
        name="ssm_outproj_router",
    )(ssd_y[0], ssd_y[1], act, z, s5_y, u, dsk, norm_w[None, :], s5_d[None, :], glu_w.astype(BF16), glu_b[None, :],
      xall, w_out.astype(BF16), mods, norm_ffn[None, :], mods, mods, wr, br, _lower_tri(tm))


def kernel(x, c, ctx, c_ctx, mod_w, mod_b, norm_mix, norm_ffn, att_w_in, att_w_out, na_q_norm, na_k_norm, na_rel_bias, wa_q_norm, wa_k_norm, wa_sink, ssm_w_in, ssm_w_out, ssd_conv_w, ssd_conv_b, ssd_dt_bias, ssd_a_log, ssd_d, ssd_norm, s5_lambda_re, s5_lambda_im, s5_log_step, s5_b_re, s5_b_im, s5_c_re, s5_c_im, s5_d, s5_glu_w, s5_glu_b, moe_w_group, moe_b_group, moe_w_expert, moe_b_expert, moe_w13, moe_w2):
    B, T, D = x.shape
    C = ctx.shape[1]
    rw = _Rows(B, T, C, D, ROW_TILE)
    xl = x.reshape(B * T, D)
    xc = ctx.reshape(B * C, D)
    cm = jnp.concatenate([c, c_ctx[None, :], jnp.zeros((8 - B - 1, D), F32)], axis=0)
    mods = _modulation(cm, mod_w, mod_b)
    mods = mods.reshape(mods.shape[0], 8, 1, 6 * D)

    m0 = mods[0]
    qkv = _att_inproj(rw, xl, xc, m0, norm_mix[0], att_w_in[0], na_q_norm[0], na_k_norm[0], wa_q_norm[0],
                      wa_k_norm[0])
    na = _na_attention(rw, qkv, na_rel_bias[0])
    wa = _wa_attention(rw, qkv, wa_sink[0])
    cx = _ctx_attention(rw, qkv, wa_sink[0])
    wr, br = _router_weights(moe_w_group[0], moe_b_group[0], moe_w_expert[0], moe_b_expert[0])
    xall, h2, route, counts = _att_outproj(rw, na, wa, cx, xl, xc, att_w_out[0], m0, norm_ffn[0], wr, br)
    y1, y2 = _moe(h2, route, counts, moe_w13, moe_w2, 0)

    m1 = mods[1]
    xall, z, xbc, u, uj, dtr = _ssm_inproj(rw, xall, y1, y2, route, m0, m1, norm_mix[1], ssm_w_in[0])
    act, dt2 = _ssm_conv(rw, xbc, dtr, ssd_conv_w[0], ssd_conv_b[0], ssd_dt_bias[0])
    ssd_y = _ssd(rw, act, dt2, ssd_a_log[0])
    s5_w = _s5_weights(s5_lambda_re[0], s5_lambda_im[0], s5_log_step[0], s5_b_re[0], s5_b_im[0], s5_c_re[0],
                       s5_c_im[0])
    s5_y = _s5(rw, uj, s5_w)
    wr, br = _router_weights(moe_w_group[1], moe_b_group[1], moe_w_expert[1], moe_b_expert[1])
    xlat, h2, route, counts = _ssm_outproj(rw, ssd_y, act, z, s5_y, u, ssd_d[0], ssd_norm[0], s5_d[0], s5_glu_w[0],
                                           s5_glu_b[0], xall, ssm_w_out[0], m1, norm_ffn[1], wr, br)
    y1, y2 = _moe(h2, route, counts, moe_w13, moe_w2, 1)
    out = _combine(rw, rw.nlat, xlat, y1, y2, route, m1)
    return out.reshape(B, T, D)
```

```python
import functools
import math

import jax
import jax.numpy as jnp
import numpy as np
from jax import lax
from jax.experimental import pallas as pl
from jax.experimental.pallas import tpu as pltpu

F32 = jnp.float32
BF16 = jnp.bfloat16

EPS = 1e-6
NEG_INF = -1e30
GRID_W = 64
HEAD_DIM = 64
NA_KH = 8
NA_KW = 16
WA_BLOCK = 128
ROPE_BASE = 10000.0
SSD_CHUNK = 128
S5_GROUP = 16
S5_STATE = 64
MOE_GROUPS = 4
MOE_EPG = 8
MOE_EXPERTS = MOE_GROUPS * MOE_EPG

LANES = 128
ROW_TILE = 512
MOE_TILE = 256
VMEM_LIMIT = 56 * 1024 * 1024
MOE_VMEM_LIMIT = 60 * 1024 * 1024


def _cparams(n_axes, vmem=VMEM_LIMIT):
    return pltpu.CompilerParams(dimension_semantics=("arbitrary",) * n_axes, vmem_limit_bytes=vmem)


def _sigmoid(x):
    return 1.0 / (1.0 + jnp.exp(-x))


def _silu(x):
    return x * _sigmoid(x)


def _rms(x, eps=EPS):
    return x * lax.rsqrt(jnp.mean(x * x, axis=-1, keepdims=True) + eps)


def _ada_norm(x, g, shift, scale):
    return (_rms(x) * g) * (1.0 + scale) + shift


def _mod_kernel(c_ref, w_ref, b_ref, o_ref):
    a = _silu(c_ref[...])
    o_ref[...] = jnp.dot(a, w_ref[...], preferred_element_type=F32, precision=lax.Precision.HIGHEST) + b_ref[...]


def _modulation(cm, mod_w, mod_b):
    depth, d, n6 = mod_w.shape
    tn = 1024
    return pl.pallas_call(
        _mod_kernel,
        grid=(depth, n6 // tn),
        in_specs=[pl.BlockSpec((8, d), lambda l, j: (0, 0)),
                  pl.BlockSpec((None, d, tn), lambda l, j: (l, 0, j)),
                  pl.BlockSpec((None, 1, tn), lambda l, j: (l, 0, j))],
        out_specs=pl.BlockSpec((None, 8, tn), lambda l, j: (l, 0, j)),
        out_shape=jax.ShapeDtypeStruct((depth, 8, n6), F32),
        compiler_params=_cparams(2),
        name="modulation",
    )(cm, mod_w, mod_b.reshape(depth, 1, n6))


class _Rows:
    def __init__(self, B, T, C, D, tm):
        assert T % tm == 0 and (B * C) % tm == 0
        self.B, self.T, self.C, self.D, self.tm = B, T, C, D, tm
        self.tpb = T // tm
        self.nlat = B * self.tpb
        self.nctx = (B * C) // tm
        self.ntot = self.nlat + self.nctx
        self.rows = B * (T + C)

    def group(self, i):
        return jnp.where(i < self.nlat, i // self.tpb, self.B)


def _mod_spec(rw, col):
    return pl.BlockSpec((None, 1, rw.D), lambda i, *_: (rw.group(i), 0, col))


def _seg_norm(y, seg, gcol):
    ss = jnp.dot((y * y).astype(BF16), seg, preferred_element_type=F32)
    return y * lax.rsqrt(ss + EPS) * gcol


def _rope(y, cos, sin):
    w = y.shape[-1]
    lane = lax.broadcasted_iota(jnp.int32, y.shape, 1)
    first = (lane % 32) < 16
    partner = jnp.where(first, pltpu.roll(y, w - 16, 1), pltpu.roll(y, 16, 1))
    return y * cos + partner * sin


def _dup_halves(k):
    lane = lax.broadcasted_iota(jnp.int32, k.shape, 1)
    sw = pltpu.roll(k, 64, 1)
    return jnp.where(lane < 64, k, sw), jnp.where(lane < 64, sw, k)


def _att_inproj_kernel(nlat, xl_ref, xc_ref, g_ref, sh_ref, sc_ref, w_ref, gcol_ref, cos_ref, sin_ref, seg_ref,
                       o_ref, h_scr):
    i = pl.program_id(0)
    x = jnp.where(i < nlat, xl_ref[...], xc_ref[...])
    h_scr[...] = _ada_norm(x, g_ref[...], sh_ref[...], sc_ref[...]).astype(BF16)
    seg = seg_ref[...]
    cos2 = jnp.concatenate([cos_ref[...], cos_ref[...]], axis=1)
    sin2 = jnp.concatenate([sin_ref[...], sin_ref[...]], axis=1)
    for c in range(9):
        c0 = c * 256
        y = jnp.dot(h_scr[...], w_ref[:, c0:c0 + 256], preferred_element_type=F32)
        gcol = gcol_ref[:, c0:c0 + 256]
        if c in (0, 1, 2, 3):
            o_ref[:, c0:c0 + 256] = _seg_norm(y, seg, gcol).astype(BF16)
        elif c in (4, 5):
            o_ref[:, c0:c0 + 256] = y.astype(BF16)
        elif c in (6, 7):
            o_ref[:, c0:c0 + 256] = _rope(_seg_norm(y, seg, gcol), cos2, sin2).astype(BF16)
        else:
            lane = lax.broadcasted_iota(jnp.int32, y.shape, 1)
            yk = jnp.where(lane < 128, _seg_norm(y, seg, gcol), y)
            yr = jnp.where(lane < 128, _rope(yk, cos2, sin2), yk)
            k0, k1 = _dup_halves(yr[:, :128])
            v0, v1 = _dup_halves(yr[:, 128:])
            o_ref[:, 2048:2176] = k0.astype(BF16)
            o_ref[:, 2176:2304] = k1.astype(BF16)
            o_ref[:, 2304:2432] = v0.astype(BF16)
            o_ref[:, 2432:2560] = v1.astype(BF16)


def _rope_tables(T, tm):
    t = np.arange(T)
    d = np.arange(HEAD_DIM)
    nf = HEAD_DIM // 4
    inv = jnp.asarray(ROPE_BASE, F32) ** (-jnp.arange(nf, dtype=F32) / nf)
    pos = np.where((d // 32 == 0)[None, :], (t // GRID_W)[:, None], (t % GRID_W)[:, None])
    ang = jnp.asarray(pos, F32) * inv[d % nf][None, :]
    sign = np.where((d % 32) < 16, -1.0, 1.0).astype(np.float32)
    cos = jnp.cos(ang)
    sin = jnp.sin(ang) * sign[None, :]
    cos = jnp.concatenate([cos, jnp.ones((tm, HEAD_DIM), F32)], axis=0)
    sin = jnp.concatenate([sin, jnp.zeros((tm, HEAD_DIM), F32)], axis=0)
    return jnp.tile(cos, (1, 2)), jnp.tile(sin, (1, 2))


def _att_inproj(rw, xl, xc, mods, norm_g, w_in, na_qn, na_kn, wa_qn, wa_kn):
    D, tm = rw.D, rw.tm
    scale = HEAD_DIM ** -0.5
    gcol = jnp.concatenate([jnp.tile(na_qn * scale, 8), jnp.tile(na_kn, 8), jnp.ones((512,), F32),
                            jnp.tile(wa_qn * scale, 8), jnp.tile(wa_kn, 2), jnp.ones((128,), F32)])[None, :]
    cos, sin = _rope_tables(rw.T, tm)
    segn = np.arange(256) // 64
    seg = jnp.asarray((segn[:, None] == segn[None, :]).astype(np.float32) / 64.0, BF16)
    nlat, tpb = rw.nlat, rw.tpb
    return pl.pallas_call(
        functools.partial(_att_inproj_kernel, nlat),
        grid=(rw.ntot,),
        in_specs=[pl.BlockSpec((tm, D), lambda i: (jnp.minimum(i, nlat - 1), 0)),
                  pl.BlockSpec((tm, D), lambda i: (jnp.maximum(i - nlat, 0), 0)),
                  pl.BlockSpec((1, D), lambda i: (0, 0)),
                  _mod_spec(rw, 0), _mod_spec(rw, 1),
                  pl.BlockSpec((D, 2304), lambda i: (0, 0)),
                  pl.BlockSpec((1, 2304), lambda i: (0, 0)),
                  pl.BlockSpec((tm, 128), lambda i: (jnp.where(i < nlat, i % tpb, tpb), 0)),
                  pl.BlockSpec((tm, 128), lambda i: (jnp.where(i < nlat, i % tpb, tpb), 0)),
                  pl.BlockSpec((256, 256), lambda i: (0, 0))],
        out_specs=pl.BlockSpec((tm, 2560), lambda i: (i, 0)),
        out_shape=jax.ShapeDtypeStruct((rw.rows, 2560), BF16),
        scratch_shapes=[pltpu.VMEM((tm, D), BF16)],
        compiler_params=_cparams(1),
        name="att_inproj",
    )(xl, xc, norm_g[None, :], mods, mods, w_in.astype(BF16), gcol, cos, sin, seg)


def _route(lg, lt, carry):
    lane = lax.broadcasted_iota(jnp.int32, lg.shape, 1).astype(F32)
    gm = lane < MOE_GROUPS
    mg = jnp.max(jnp.where(gm, lg, NEG_INF), axis=-1, keepdims=True)
    eg = jnp.where(gm, jnp.exp(jnp.where(gm, lg, NEG_INF) - mg), 0.0)
    pg = eg / jnp.sum(eg, axis=-1, keepdims=True)
    ptop = jnp.max(pg, axis=-1, keepdims=True)
    gidx = jnp.min(jnp.where(gm & (pg == ptop), lane, 1e9), axis=-1, keepdims=True)
    lo = MOE_GROUPS + MOE_EPG * gidx
    em = (lane >= lo) & (lane < lo + MOE_EPG)
    le = jnp.where(em, lg, NEG_INF)
    ee = jnp.where(em, jnp.exp(le - jnp.max(le, axis=-1, keepdims=True)), 0.0)
    pe = ee / jnp.sum(ee, axis=-1, keepdims=True)
    v1 = jnp.max(jnp.where(em, pe, -1.0), axis=-1, keepdims=True)
    i1 = jnp.min(jnp.where(em & (pe == v1), lane, 1e9), axis=-1, keepdims=True)
    em2 = em & (lane != i1)
    v2 = jnp.max(jnp.where(em2, pe, -1.0), axis=-1, keepdims=True)
    i2 = jnp.min(jnp.where(em2 & (pe == v2), lane, 1e9), axis=-1, keepdims=True)
    den = v1 + v2
    w1 = v1 / den * ptop
    w2 = v2 / den * ptop
    e1 = i1 - MOE_GROUPS
    e2 = i2 - MOE_GROUPS
    m1 = lane == e1
    m2 = lane == e2
    oh = jnp.where(m1 | m2, 1.0, 0.0)
    cnt = jnp.dot(lt, oh.astype(BF16), preferred_element_type=F32) + carry
    r1 = jnp.sum(jnp.where(m1, cnt, 0.0), axis=-1, keepdims=True)
    r2 = jnp.sum(jnp.where(m2, cnt, 0.0), axis=-1, keepdims=True)
    route = jnp.where(lane == 0, e1, jnp.where(lane == 1, e2, jnp.where(lane == 2, w1, jnp.where(
        lane == 3, w2, jnp.where(lane == 4, r1, jnp.where(lane == 5, r2, 0.0))))))
    return route, carry + jnp.sum(oh, axis=0, keepdims=True)


def _post_mixer(i, x, y, g1, gn, sh2, sc2, wr_ref, br_ref, lt_ref, xo_ref, h2_ref, rt_ref, cnt_ref, carry):
    xn = x + g1 * y
    xo_ref[...] = xn
    h2 = _ada_norm(xn, gn, sh2, sc2)
    hb = h2.astype(BF16)
    hbf = hb.astype(F32)
    half = h2.shape[1] // 2
    h2_ref[...] = pltpu.pack_elementwise([h2[:, :half], h2[:, half:]], packed_dtype=BF16)
    hl = (h2 - hbf).astype(BF16)
    lg = (jnp.dot(hb, wr_ref[0], preferred_element_type=F32)
          + (jnp.dot(hb, wr_ref[1], preferred_element_type=F32) + jnp.dot(hl, wr_ref[0], preferred_element_type=F32))
          + br_ref[...])

    @pl.when(i == 0)
    def _():
        carry[...] = jnp.zeros_like(carry)

    route, newc = _route(lg, lt_ref[...], carry[...])
    rt_ref[...] = route
    carry[...] = newc
    cnt_ref[...] = newc


def _att_outproj_kernel(nlat, na_ref, wa_ref, cx_ref, xl_ref, xc_ref, w_ref, g1_ref, gn_ref, sh2_ref, sc2_ref,
                        wr_ref, br_ref, lt_ref, xo_ref, h2_ref, rt_ref, cnt_ref, carry):
    i = pl.program_id(0)
    lat = i < nlat
    mix = jnp.where(lat, jnp.concatenate([na_ref[...], wa_ref[...]], axis=1), cx_ref[...])
    y = jnp.dot(mix, w_ref[...], preferred_element_type=F32)
    x = jnp.where(lat, xl_ref[...], xc_ref[...])
    _post_mixer(i, x, y, g1_ref[...], gn_ref[...], sh2_ref[...], sc2_ref[...], wr_ref, br_ref, lt_ref,
                xo_ref, h2_ref, rt_ref, cnt_ref, carry)


def _router_weights(w_group, b_group, w_expert, b_expert):
    D = w_group.shape[0]
    pad = LANES - MOE_GROUPS - MOE_EXPERTS
    wr = jnp.concatenate([w_group, w_expert, jnp.zeros((D, pad), F32)], axis=1)
    br = jnp.concatenate([b_group, b_expert, jnp.zeros((pad,), F32)])[None, :]
    hi = wr.astype(BF16)
    lo = (wr - hi.astype(F32)).astype(BF16)
    return jnp.stack([hi, lo]), br


def _lower_tri(tm):
    r = np.arange(tm)
    return jnp.asarray((r[None, :] < r[:, None]).astype(np.float32), BF16)


def _post_specs(rw):
    D, tm = rw.D, rw.tm
    return ([pl.BlockSpec((1, D), lambda i: (0, 0)), _mod_spec(rw, 3), _mod_spec(rw, 4),
             pl.BlockSpec((2, D, LANES), lambda i: (0, 0, 0)), pl.BlockSpec((1, LANES), lambda i: (0, 0)),
             pl.BlockSpec((tm, tm), lambda i: (0, 0))],
            [pl.BlockSpec((tm, D), lambda i: (i, 0)), pl.BlockSpec((tm, D // 2), lambda i: (i, 0)),
             pl.BlockSpec((tm, LANES), lambda i: (i, 0)), pl.BlockSpec((1, LANES), lambda i: (0, 0))])


def _post_shapes(nrows, D):
    return [jax.ShapeDtypeStruct((nrows, D), F32), jax.ShapeDtypeStruct((nrows, D // 2), jnp.uint32),
            jax.ShapeDtypeStruct((nrows, LANES), F32), jax.ShapeDtypeStruct((1, LANES), F32)]


def _att_outproj(rw, na, wa, cx, xl, xc, w_out, mods, norm_ffn, wr, br):
    D, tm, nlat = rw.D, rw.tm, rw.nlat
    post_in, post_out = _post_specs(rw)
    latmap = lambda i: (jnp.minimum(i, nlat - 1), 0)
    ctxmap = lambda i: (jnp.maximum(i - nlat, 0), 0)
    return pl.pallas_call(
        functools.partial(_att_outproj_kernel, nlat),
        grid=(rw.ntot,),
        in_specs=[pl.BlockSpec((tm, 512), latmap), pl.BlockSpec((tm, 512), latmap), pl.BlockSpec((tm, D), ctxmap),
                  pl.BlockSpec((tm, D), latmap), pl.BlockSpec((tm, D), ctxmap),
                  pl.BlockSpec((D, D), lambda i: (0, 0)), _mod_spec(rw, 2)] + post_in,
        out_specs=post_out,
        out_shape=_post_shapes(rw.rows, D),
        scratch_shapes=[pltpu.VMEM((1, LANES), F32)],
        compiler_params=_cparams(1),
        name="att_outproj_router",
    )(na, wa, cx, xl, xc, w_out.astype(BF16), mods, norm_ffn[None, :], mods, mods, wr, br, _lower_tri(tm))


def _moe_kernel(te_ref, nu_ref, src_ref, nsrc_ref, hp_ref, w13_ref, w2_ref, o_ref, w13b, w2b, xa, xb):
    i = pl.program_id(0)
    prev = te_ref[jnp.maximum(i - 1, 0)]
    changed = (i == 0) | (te_ref[i] != prev)
    tg = xa.shape[0]

    @pl.when(changed)
    def _():
        w13b[...] = w13_ref[...].astype(BF16)
        w2b[...] = w2_ref[...].astype(BF16)

    @pl.when(i == 0)
    def _():
        def fetch(j, carry):
            xa[pl.ds(j, 1), :] = hp_ref[pl.ds(src_ref[0, j], 1), :]
            return carry

        lax.fori_loop(0, tg, fetch, 0, unroll=8)

    def step(cur, nxt):
        for j in range(tg):
            nxt[pl.ds(j, 1), :] = hp_ref[pl.ds(nsrc_ref[0, j], 1), :]
        ff = w2b.shape[0]
        half = cur.shape[1]
        w = cur[...]
        unpack = functools.partial(pltpu.unpack_elementwise, packed_dtype=BF16, unpacked_dtype=F32)
        x_lo = unpack(w, index=0).astype(BF16)
        x_hi = unpack(w, index=1).astype(BF16)
        a13 = (jnp.dot(x_lo, w13b[:half, :], preferred_element_type=F32)
               + jnp.dot(x_hi, w13b[half:, :], preferred_element_type=F32))
        act = _silu(a13[:, :ff]) * a13[:, ff:]
        o_ref[...] = jnp.dot(act.astype(BF16), w2b[...], preferred_element_type=F32).astype(BF16)

    used = i < nu_ref[0]

    @pl.when(used & (i % 2 == 0))
    def _():
        step(xa, xb)

    @pl.when(used & (i % 2 == 1))
    def _():
        step(xb, xa)

    @pl.when(i >= nu_ref[0])
    def _():
        o_ref[...] = jnp.zeros_like(o_ref)


def _moe(h2p, route, counts, w13, w2, layer):
    N = h2p.shape[0]
    D = 2 * h2p.shape[1]
    _, E, _, F2 = w13.shape
    tg = MOE_TILE
    nt = (2 * N) // tg + E
    e = route[:, 0:2].astype(jnp.int32)
    rank = route[:, 4:6].astype(jnp.int32)
    cnt = counts[0, :E].astype(jnp.int32)
    ntile_e = (cnt + tg - 1) // tg
    tile_end = jnp.cumsum(ntile_e)
    offs = (tile_end - ntile_e) * tg
    onehot = (e[:, :, None] == jnp.arange(E, dtype=jnp.int32)).astype(jnp.int32)
    dest = jnp.sum(onehot * offs, axis=-1) + rank
    src = jnp.zeros((nt * tg,), jnp.int32).at[dest.reshape(-1)].set(jnp.repeat(jnp.arange(N, dtype=jnp.int32), 2))
    tile_id = jnp.arange(nt, dtype=jnp.int32)
    nu = tile_end[-1:].astype(jnp.int32)
    te = jnp.sum((tile_end[None, :] <= jnp.minimum(tile_id, nu[0] - 1)[:, None]).astype(jnp.int32), axis=1)
    te = jnp.minimum(te, E - 1)
    ys = pl.pallas_call(
        _moe_kernel,
        grid_spec=pltpu.PrefetchScalarGridSpec(
            num_scalar_prefetch=2,
            grid=(nt,),
            in_specs=[pl.BlockSpec((None, 1, tg), lambda i, te, nu: (i, 0, 0), memory_space=pltpu.SMEM),
                      pl.BlockSpec((None, 1, tg), lambda i, te, nu: (jnp.minimum(i + 1, nt - 1), 0, 0),
                                   memory_space=pltpu.SMEM),
                      pl.BlockSpec((N, D // 2), lambda i, te, nu: (0, 0), pipeline_mode=pl.Buffered(1)),
                      pl.BlockSpec((None, None, D, F2), lambda i, te, nu: (layer, te[i], 0, 0)),
                      pl.BlockSpec((None, None, F2 // 2, D), lambda i, te, nu: (layer, te[i], 0, 0))],
            out_specs=pl.BlockSpec((tg, D), lambda i, te, nu: (i, 0)),
            scratch_shapes=[pltpu.VMEM((D, F2), BF16), pltpu.VMEM((F2 // 2, D), BF16),
                            pltpu.VMEM((tg, D // 2), jnp.uint32), pltpu.VMEM((tg, D // 2), jnp.uint32)]),
        out_shape=jax.ShapeDtypeStruct((nt * tg, D), BF16),
        compiler_params=_cparams(1, vmem=MOE_VMEM_LIMIT),
        name="moe_experts",
    )(te, nu, src.reshape(nt, 1, tg), src.reshape(nt, 1, tg), h2p, w13, w2)
    pick = lambda k: ys.at[dest[:, k]].get(mode="promise_in_bounds")
    return pick(0), pick(1)


def _moe_residual(x_ref, y1_ref, y2_ref, rt_ref, g2_ref):
    rt = rt_ref[...]
    f = rt[:, 2:3] * y1_ref[...].astype(F32) + rt[:, 3:4] * y2_ref[...].astype(F32)
    return x_ref[...] + g2_ref[...] * f


def _combine_kernel(x_ref, y1_ref, y2_ref, rt_ref, g2_ref, o_ref):
    o_ref[...] = _moe_residual(x_ref, y1_ref, y2_ref, rt_ref, g2_ref)


def _combine(rw, ntiles, xall, y1, y2, route, mods):
    D, tm = rw.D, rw.tm
    row = lambda i: (i, 0)
    return pl.pallas_call(
        _combine_kernel,
        grid=(ntiles,),
        in_specs=[pl.BlockSpec((tm, D), row), pl.BlockSpec((tm, D), row), pl.BlockSpec((tm, D), row),
                  pl.BlockSpec((tm, LANES), row), _mod_spec(rw, 5)],
        out_specs=pl.BlockSpec((tm, D), row),
        out_shape=jax.ShapeDtypeStruct((ntiles * tm, D), F32),
        compiler_params=_cparams(1),
        name="moe_combine",
    )(xall, y1, y2, route, mods)


NA_QROWS = 8
NA_KROWS = 16


def _na_first_key_row(variant, a):
    return (max(a - 4, 0) + 4, a, min(a, 4))[variant]


def _na_key_lanes(row0):
    a = row0 // GRID_W
    starts = [_na_first_key_row(v, a) for v in range(3)]
    lo = (min(starts) * GRID_W) // LANES * LANES
    hi = -(-((max(starts) + NA_KH) * GRID_W) // LANES) * LANES
    return lo, hi


def _na_bias_tiles(rpb):
    H = rpb.shape[0]
    i = np.arange(GRID_W)
    c0 = np.clip(i - NA_KW // 2, 0, GRID_W - NA_KW)
    j = np.arange(GRID_W)
    colvalid = (j[None, :] >= c0[:, None]) & (j[None, :] < c0[:, None] + NA_KW)
    dc = np.clip(j[None, :] - i[:, None] + NA_KW - 1, 0, 2 * NA_KW - 2)
    onehot = ((dc[None] == np.arange(2 * NA_KW - 1)[:, None, None]) & colvalid[None]).astype(np.float32)
    tiles = jnp.einsum('hrc,cij->hrij', rpb.astype(F32), jnp.asarray(onehot), precision=lax.Precision.HIGHEST)
    tiles = tiles + jnp.asarray(np.where(colvalid, 0.0, NEG_INF).astype(np.float32))
    return tiles.transpose(0, 2, 1, 3).reshape(H, GRID_W, (2 * NA_KH - 1) * GRID_W)


def _na_fill_bias(variant, tiles_ref, bias_scr):
    for hh in range(2):
        for a in range(NA_QROWS):
            start = _na_first_key_row(variant, a)
            dr0 = start - a + 3
            rows = slice(a * GRID_W, (a + 1) * GRID_W)
            w0, w1 = start * GRID_W, (start + NA_KH) * GRID_W
            if w0 > 0:
                bias_scr[hh, rows, 0:w0] = jnp.full((GRID_W, w0), NEG_INF, F32)
            bias_scr[hh, rows, w0:w1] = tiles_ref[hh, :, dr0 * GRID_W:(dr0 + NA_KH) * GRID_W]
            if w1 < NA_KROWS * GRID_W:
                bias_scr[hh, rows, w1:] = jnp.full((GRID_W, NA_KROWS * GRID_W - w1), NEG_INF, F32)


def _softmax_pv(parts, extra=None, rc=64):
    m_rows = parts[0][0].shape[0]
    probs = [[] for _ in parts]
    inv_l = []
    for r0 in range(0, m_rows, rc):
        sc = []
        for s, _, bias_fn, lanes_fn in parts:
            l0, l1 = (0, s.shape[1]) if lanes_fn is None else lanes_fn(r0)
            c = s[r0:r0 + rc, l0:l1]
            if bias_fn is not None:
                c = c + bias_fn(r0, rc, slice(l0, l1))
            sc.append((c, l0, s.shape[1] - l1))
        mx = functools.reduce(jnp.maximum, [jnp.max(c, axis=-1, keepdims=True) for c, _, _ in sc])
        if extra is not None:
            mx = jnp.maximum(mx, extra[r0:r0 + rc])
        l = jnp.zeros_like(mx) if extra is None else jnp.exp(extra[r0:r0 + rc] - mx)
        for k, (c, before, after) in enumerate(sc):
            p = jnp.exp(c - mx)
            l = l + jnp.sum(p, axis=-1, keepdims=True)
            row = [jnp.zeros((rc, before), BF16)] * (before > 0) + [p.astype(BF16)] \
                + [jnp.zeros((rc, after), BF16)] * (after > 0)
            probs[k].append(row[0] if len(row) == 1 else jnp.concatenate(row, axis=1))
        inv_l.append(1.0 / l)
    o = None
    for k, (_, v, _, _) in enumerate(parts):
        pv = jnp.dot(jnp.concatenate(probs[k], axis=0), v, preferred_element_type=F32)
        o = pv if o is None else o + pv
    return o * jnp.concatenate(inv_l, axis=0)


def _nt(a, b):
    return lax.dot_general(a, b, (((1,), (1,)), ((), ())), preferred_element_type=F32)


def _na_kernel(n_rb, q_ref, k0, k1, k2, k3, v0, v1, v2, v3, kc_ref, vc_ref, tiles_ref, o_ref, bias_ref):
    rb = pl.program_id(1)
    first_visit = pl.program_id(2) == 0
    for variant, at_rb in ((0, 0), (1, 1), (2, n_rb - 1)):
        @pl.when(first_visit & (rb == at_rb))
        def _(variant=variant):
            _na_fill_bias(variant, tiles_ref, bias_ref)

    q2 = q_ref[...]
    kw = jnp.concatenate([k0[...], k1[...], k2[...], k3[...]], axis=0)
    vw = jnp.concatenate([v0[...], v1[...], v2[...], v3[...]], axis=0)
    kc = kc_ref[...]
    vc = vc_ref[...]
    lane = lax.broadcasted_iota(jnp.int32, q2.shape, 1)
    out = jnp.zeros(q2.shape, F32)
    for hh in range(2):
        m = (lane < HEAD_DIM) if hh == 0 else (lane >= HEAD_DIM)
        qm = jnp.where(m, q2, jnp.zeros_like(q2))
        o = _softmax_pv([(_nt(qm, kw), vw, lambda r0, rc, lanes, hh=hh: bias_ref[hh, r0:r0 + rc, lanes],
                          _na_key_lanes), (_nt(qm, kc), vc, None, None)], rc=32)
        out = jnp.where(m, o, out)
    o_ref[...] = out.astype(BF16)


def _na_attention(rw, qkv, rpb):
    B, T, C = rw.B, rw.T, rw.C
    tq = NA_QROWS * GRID_W
    tk = tq // 2
    n_rb = T // tq
    nkb = T // tk
    assert T % tq == 0 and n_rb >= 2 and (B * T) % C == 0
    tiles = _na_bias_tiles(rpb)
    ctxrow = (B * T) // C

    def kvspec(j, col):
        return pl.BlockSpec((tk, LANES), lambda p, rb, b: (b * nkb + jnp.clip(2 * rb - 1 + j, 0, nkb - 1), col + p))

    return pl.pallas_call(
        functools.partial(_na_kernel, n_rb),
        grid=(4, n_rb, B),
        in_specs=[pl.BlockSpec((tq, LANES), lambda p, rb, b: (b * n_rb + rb, p))]
        + [kvspec(j, 4) for j in range(4)] + [kvspec(j, 8) for j in range(4)]
        + [pl.BlockSpec((C, LANES), lambda p, rb, b: (ctxrow + b, 4 + p)),
           pl.BlockSpec((C, LANES), lambda p, rb, b: (ctxrow + b, 8 + p)),
           pl.BlockSpec((2,) + tiles.shape[1:], lambda p, rb, b: (p, 0, 0))],
        out_specs=pl.BlockSpec((tq, LANES), lambda p, rb, b: (b * n_rb + rb, p)),
        out_shape=jax.ShapeDtypeStruct((B * T, 4 * LANES), BF16),
        scratch_shapes=[pltpu.VMEM((2, tq, 2 * tq), F32)],
        compiler_params=_cparams(3),
        name="neighbourhood_attention",
    )(qkv, *([qkv] * 10), tiles)


WA_QBLOCKS = 2


def _wa_kernel(nb, sink_ref, q_ref, k0, k1, k2, k3, v0, v1, v2, v3, kx_ref, vx_ref, o_ref):
    step = pl.program_id(1)
    blk = WA_BLOCK
    kblocks, vblocks = (k0, k1, k2, k3), (v0, v1, v2, v3)
    lane = lax.broadcasted_iota(jnp.int32, (blk, LANES), 1)
    zero = jnp.zeros((blk, LANES), BF16)
    qi = lax.broadcasted_iota(jnp.int32, (blk, 3 * blk), 0)
    ks = lax.broadcasted_iota(jnp.int32, (blk, 3 * blk), 1)
    for qb in range(WA_QBLOCKS):
        n = step * WA_QBLOCKS + qb
        lo = jnp.where(n > 0, 0, blk)
        hi = jnp.where(n < nb - 1, 3 * blk, 2 * blk)
        valid = (ks >= qi) & (ks <= qi + 2 * blk) & (ks >= lo) & (ks < hi)
        band_mask = jnp.where(valid, 0.0, NEG_INF)
        rows = slice(qb * blk, (qb + 1) * blk)
        for kv in range(2):
            parts = []
            for pr in range(2):
                c0 = kv * 2 * LANES + pr * LANES
                qp = q_ref[rows, c0:c0 + LANES]
                parts += [jnp.where(lane < HEAD_DIM, qp, zero), jnp.where(lane >= HEAD_DIM, qp, zero)]
            qs = jnp.concatenate(parts, axis=0)
            cs = slice(kv * LANES, (kv + 1) * LANES)
            kb = jnp.concatenate([r[:, cs] for r in kblocks[qb:qb + 3]], axis=0)
            vb = jnp.concatenate([r[:, cs] for r in vblocks[qb:qb + 3]], axis=0)
            sink = jnp.concatenate([jnp.full((blk, 1), sink_ref[kv * 4 + g], F32) for g in range(4)], axis=0)
            o = _softmax_pv([(_nt(qs, kb), vb, lambda r0, rc, lanes, m=band_mask: m[r0 % blk:r0 % blk + rc, lanes], None),
                             (_nt(qs, kx_ref[:, cs]), vx_ref[:, cs], None, None)], extra=sink, rc=64)
            c0 = kv * 2 * LANES
            o_ref[rows, c0:c0 + LANES] = jnp.where(lane < HEAD_DIM, o[0:blk], o[blk:2 * blk]).astype(BF16)
            o_ref[rows, c0 + LANES:c0 + 2 * LANES] = jnp.where(
                lane < HEAD_DIM, o[2 * blk:3 * blk], o[3 * blk:4 * blk]).astype(BF16)


def _wa_attention(rw, qkv, sink):
    B, T, C = rw.B, rw.T, rw.C
    blk = WA_BLOCK
    nb = T // blk
    nq = WA_QBLOCKS
    assert nb % nq == 0
    ctxrow = (B * T) // C

    def kvspec(j, col):
        return pl.BlockSpec((blk, 2 * LANES), lambda b, s: (b * nb + jnp.clip(nq * s - 1 + j, 0, nb - 1), col))

    return pl.pallas_call(
        functools.partial(_wa_kernel, nb),
        grid=(B, nb // nq),
        in_specs=[pl.BlockSpec(memory_space=pltpu.SMEM),
                  pl.BlockSpec((nq * blk, 4 * LANES), lambda b, s: (b * (nb // nq) + s, 3))]
        + [kvspec(j, 8) for j in range(nq + 2)] + [kvspec(j, 9) for j in range(nq + 2)]
        + [pl.BlockSpec((C, 2 * LANES), lambda b, s: (ctxrow + b, 8)),
           pl.BlockSpec((C, 2 * LANES), lambda b, s: (ctxrow + b, 9))],
        out_specs=pl.BlockSpec((nq * blk, 4 * LANES), lambda b, s: (b * (nb // nq) + s, 0)),
        out_shape=jax.ShapeDtypeStruct((B * T, 4 * LANES), BF16),
        compiler_params=_cparams(2),
        name="window_attention",
    )(sink.astype(F32), qkv, *([qkv] * (2 * nq + 6)))


def _ctx_attn_kernel(sink_ref, t_ref, o_ref):
    C = t_ref.shape[0]
    lane = lax.broadcasted_iota(jnp.int32, (C, LANES), 1)
    zero = jnp.zeros((C, LANES), BF16)

    def pair(q2, k2, v2, sinks):
        out = jnp.zeros((C, LANES), F32)
        for hh in range(2):
            m = (lane < HEAD_DIM) if hh == 0 else (lane >= HEAD_DIM)
            extra = None if sinks is None else jnp.full((C, 1), sinks[hh], F32)
            o = _softmax_pv([(_nt(jnp.where(m, q2, zero), k2), v2, None, None)], extra=extra, rc=64)
            out = jnp.where(m, o, out)
        return out.astype(BF16)

    for p in range(4):
        c = p * LANES
        o_ref[:, c:c + LANES] = pair(t_ref[:, c:c + LANES], t_ref[:, 512 + c:640 + c], t_ref[:, 1024 + c:1152 + c], None)
    for kv in range(2):
        kd = t_ref[:, 2048 + kv * LANES:2176 + kv * LANES]
        vd = t_ref[:, 2304 + kv * LANES:2432 + kv * LANES]
        for pr in range(2):
            c = kv * 256 + pr * LANES
            h0 = kv * 4 + pr * 2
            o_ref[:, 512 + c:640 + c] = pair(t_ref[:, 1536 + c:1664 + c], kd, vd, (sink_ref[h0], sink_ref[h0 + 1]))


def _ctx_attention(rw, qkv, sink):
    B, T, C = rw.B, rw.T, rw.C
    ctxrow = (B * T) // C
    return pl.pallas_call(
        _ctx_attn_kernel,
        grid=(B,),
        in_specs=[pl.BlockSpec(memory_space=pltpu.SMEM),
                  pl.BlockSpec((C, qkv.shape[1]), lambda b: (ctxrow + b, 0))],
        out_specs=pl.BlockSpec((C, 8 * LANES), lambda b: (b, 0)),
        out_shape=jax.ShapeDtypeStruct((B * C, 8 * LANES), BF16),
        compiler_params=_cparams(1),
        name="context_attention",
    )(sink.astype(F32), qkv)


S5_Q = 16
CONV_TILE = 256
CONV_HALO = 16


def _ssm_inproj_kernel(x_ref, y1_ref, y2_ref, rt_ref, g2_ref, g_ref, sh_ref, sc_ref, w_ref,
                       xo_ref, z_ref, xbc_ref, u_ref, uj_ref, dt_ref, h_scr, u_scr):
    xn = _moe_residual(x_ref, y1_ref, y2_ref, rt_ref, g2_ref)
    xo_ref[...] = xn
    h_scr[...] = _ada_norm(xn, g_ref[...], sh_ref[...], sc_ref[...]).astype(BF16)

    def mm(c0, n):
        return jnp.dot(h_scr[...], w_ref[:, c0:c0 + n], preferred_element_type=F32)

    for c in range(4):
        z_ref[:, c * 256:(c + 1) * 256] = mm(c * 256, 256).astype(BF16)
    for c in range(6):
        xbc_ref[:, c * 256:(c + 1) * 256] = mm(1024 + c * 256, 256).astype(BF16)
    for c in range(2):
        y = mm(2560 + c * 256, 256)
        u_ref[:, c * 256:(c + 1) * 256] = y.astype(BF16)
        u_scr[2 * c] = y[:, :LANES]
        u_scr[2 * c + 1] = y[:, LANES:]
    dt_ref[...] = mm(3072, LANES)
    nchunk = u_scr.shape[1] // S5_Q
    for j in range(S5_Q):
        for t in range(u_scr.shape[0]):
            uj_ref[j, :, t * LANES:(t + 1) * LANES] = u_scr[t, pl.ds(j, nchunk, stride=S5_Q), :].astype(BF16)


def _ssm_inproj(rw, xall, y1, y2, route, prev_mods, mods, norm_g, w_in):
    D, tm = rw.D, rw.tm
    w = jnp.concatenate([w_in[:, 0:2560], w_in[:, 2592:3104], w_in[:, 2560:2592], jnp.zeros((D, LANES - 32), F32)],
                        axis=1).astype(BF16)
    row = lambda i: (i, 0)
    return pl.pallas_call(
        _ssm_inproj_kernel,
        grid=(rw.ntot,),
        in_specs=[pl.BlockSpec((tm, D), row), pl.BlockSpec((tm, D), row), pl.BlockSpec((tm, D), row),
                  pl.BlockSpec((tm, LANES), row), _mod_spec(rw, 5), pl.BlockSpec((1, D), lambda i: (0, 0)),
                  _mod_spec(rw, 0), _mod_spec(rw, 1), pl.BlockSpec((D, 3200), lambda i: (0, 0))],
        out_specs=[pl.BlockSpec((tm, D), row),
                   pl.BlockSpec((tm, 1024), row), pl.BlockSpec((tm, 1536), row), pl.BlockSpec((tm, 512), row),
                   pl.BlockSpec((S5_Q, tm // S5_Q, 512), lambda i: (0, i, 0)), pl.BlockSpec((tm, LANES), row)],
        out_shape=[jax.ShapeDtypeStruct((rw.rows, D), F32),
                   jax.ShapeDtypeStruct((rw.rows, 1024), BF16), jax.ShapeDtypeStruct((rw.rows, 1536), BF16),
                   jax.ShapeDtypeStruct((rw.rows, 512), BF16),
                   jax.ShapeDtypeStruct((S5_Q, rw.rows // S5_Q, 512), BF16),
                   jax.ShapeDtypeStruct((rw.rows, LANES), F32)],
        scratch_shapes=[pltpu.VMEM((tm, D), BF16), pltpu.VMEM((512 // LANES, tm, LANES), F32)],
        compiler_params=_cparams(1),
        name="ssm_inproj",
    )(xall, y1, y2, route, prev_mods, norm_g[None, :], mods, mods, w)


def _softplus(x):
    return jnp.maximum(x, 0.0) + jnp.log(1.0 + jnp.exp(-jnp.abs(x)))


def _conv_kernel(lat_tiles, tpb, cpb, x_ref, pv_ref, nx_ref, w_ref, b_ref, dtr_ref, dtb_ref, act_ref, dt_ref):
    i = pl.program_id(0)
    is_lat = i < lat_tiles
    pos = jnp.where(is_lat, i % tpb, (i - lat_tiles) % cpb)
    last_pos = jnp.where(is_lat, tpb - 1, cpb - 1)
    x = x_ref[...].astype(F32)
    tc = x.shape[0]
    prev_row = jnp.where(pos == 0, 0.0, pv_ref[...].astype(F32)[CONV_HALO - 1:CONV_HALO, :])
    next_row = jnp.where(pos == last_pos, 0.0, nx_ref[...].astype(F32)[0:1, :])
    row = lax.broadcasted_iota(jnp.int32, x.shape, 0)
    xm1 = jnp.where(row == 0, prev_row, pltpu.roll(x, 1, 0))
    xp1 = jnp.where(row == tc - 1, next_row, pltpu.roll(x, tc - 1, 0))
    y = w_ref[0:1, :] * xm1 + w_ref[1:2, :] * x + w_ref[2:3, :] * xp1 + b_ref[...]
    act_ref[...] = _silu(y).astype(BF16)
    sp = _softplus(dtr_ref[...] + dtb_ref[...])
    dt_ref[0] = sp
    dt_ref[1] = pltpu.roll(sp, LANES - 16, 1)


def _ssm_conv(rw, xbc, dtr, conv_w, conv_b, dt_bias):
    B, T, C = rw.B, rw.T, rw.C
    tc = CONV_TILE
    assert T % tc == 0 and C % tc == 0
    lat_tiles, tpb, cpb = (B * T) // tc, T // tc, C // tc
    ntiles = rw.rows // tc
    hpt = tc // CONV_HALO
    nhalo = rw.rows // CONV_HALO
    W = xbc.shape[1]
    dtb = jnp.concatenate([dt_bias.reshape(-1), jnp.zeros((LANES - 32,), F32)])[None, :]
    row = lambda i: (i, 0)
    return pl.pallas_call(
        functools.partial(_conv_kernel, lat_tiles, tpb, cpb),
        grid=(ntiles,),
        in_specs=[pl.BlockSpec((tc, W), row),
                  pl.BlockSpec((CONV_HALO, W), lambda i: (jnp.maximum(i * hpt - 1, 0), 0)),
                  pl.BlockSpec((CONV_HALO, W), lambda i: (jnp.minimum((i + 1) * hpt, nhalo - 1), 0)),
                  pl.BlockSpec((3, W), lambda i: (0, 0)), pl.BlockSpec((1, W), lambda i: (0, 0)),
                  pl.BlockSpec((tc, LANES), row), pl.BlockSpec((1, LANES), lambda i: (0, 0))],
        out_specs=[pl.BlockSpec((tc, W), row), pl.BlockSpec((2, tc, LANES), lambda i: (0, i, 0))],
        out_shape=[jax.ShapeDtypeStruct((rw.rows, W), BF16), jax.ShapeDtypeStruct((2, rw.rows, LANES), F32)],
        compiler_params=_cparams(1),
        name="ssm_conv",
    )(xbc, xbc, xbc, conv_w, conv_b[None, :], dtr, dtb)


def _ssd_kernel(actf_ref, actb_ref, dtf_ref, dtb_ref, tri_ref, a_ref, yf_ref, yb_ref, hst):
    @pl.when(pl.program_id(1) == 0)
    def _():
        hst[...] = jnp.zeros_like(hst)

    _ssd_chunk(actf_ref, dtf_ref, tri_ref[0], a_ref[0], yf_ref, hst.at[0])
    _ssd_chunk(actb_ref, dtb_ref, tri_ref[1], a_ref[1], yb_ref, hst.at[1])


def _ssd_chunk(act_ref, dt_ref, tri, avec, y_ref, hst):
    q = SSD_CHUNK
    dt = dt_ref[...]
    da = dt * avec
    acs = jnp.dot(tri, da, preferred_element_type=F32, precision=lax.Precision.HIGHEST)
    tot = jnp.sum(da, axis=0, keepdims=True)
    acs_t = acs.T
    dt_t = dt.T
    eacs = jnp.exp(acs)
    wend = jnp.exp(tot - acs) * dt
    etot = jnp.exp(tot)
    mask = tri > 0.5
    left = lax.broadcasted_iota(jnp.int32, (q, LANES), 1) < HEAD_DIM
    left1 = lax.broadcasted_iota(jnp.int32, (1, LANES), 1) < HEAD_DIM
    for g in range(2):
        bg = act_ref[:, 1024 + g * 128:1152 + g * 128]
        cg = act_ref[:, 1280 + g * 128:1408 + g * 128]
        cb = _nt(cg, bg)
        hin = hst[:, g * 512:(g + 1) * 512]
        yoff = jnp.dot(cg, hin.astype(BF16), preferred_element_type=F32)
        xw, dec = [], []
        for pr in range(4):
            h_a = g * 8 + pr * 2
            h_b = h_a + 1
            c0 = h_a * HEAD_DIM
            x2 = act_ref[:, c0:c0 + LANES]
            outs = []
            for h in (h_a, h_b):
                seg = acs[:, h:h + 1] - acs_t[h:h + 1, :]
                w = cb * jnp.exp(jnp.where(mask, seg, NEG_INF)) * dt_t[h:h + 1, :]
                outs.append(jnp.dot(w.astype(BF16), x2, preferred_element_type=F32))
            yd = jnp.where(left, outs[0], outs[1])
            sc = jnp.where(left, eacs[:, h_a:h_a + 1], eacs[:, h_b:h_b + 1])
            y_ref[:, c0:c0 + LANES] = (yd + yoff[:, pr * LANES:(pr + 1) * LANES] * sc).astype(BF16)
            wsc = jnp.where(left, wend[:, h_a:h_a + 1], wend[:, h_b:h_b + 1])
            xw.append((x2.astype(F32) * wsc).astype(BF16))
            dec.append(jnp.where(left1, etot[:, h_a:h_a + 1], etot[:, h_b:h_b + 1]))
        bg_t = bg.astype(F32).T.astype(BF16)
        snew = jnp.dot(bg_t, jnp.concatenate(xw, axis=1), preferred_element_type=F32)
        hst[:, g * 512:(g + 1) * 512] = hin * jnp.concatenate(dec, axis=1) + snew


def _ssd(rw, act, dt2, a_log):
    B, T, C = rw.B, rw.T, rw.C
    q = SSD_CHUNK
    nct, nlt = C // q, T // q
    ctx0 = (B * T) // q
    r = np.arange(q)
    tri = jnp.asarray(np.stack([r[None, :] <= r[:, None], r[None, :] >= r[:, None]]).astype(np.float32))
    avec = jnp.concatenate([-jnp.exp(a_log.astype(F32)), jnp.zeros((2, LANES - a_log.shape[1]), F32)], axis=1)[:, None, :]

    def blk(d, b, s):
        kc = s if d == 0 else nct - 1 - s
        kl = s - nct if d == 0 else nlt - 1 - (s - nct)
        return jnp.where(s < nct, ctx0 + b * nct + kc, b * nlt + kl)

    aspec = lambda d: pl.BlockSpec((q, act.shape[1]), lambda b, s: (blk(d, b, s), 0))
    dspec = lambda d: pl.BlockSpec((None, q, LANES), lambda b, s: (d, blk(d, b, s), 0))
    yspec = lambda d: pl.BlockSpec((q, 1024), lambda b, s: (blk(d, b, s), 0))
    return pl.pallas_call(
        _ssd_kernel,
        grid=(B, nct + nlt),
        in_specs=[aspec(0), aspec(1), dspec(0), dspec(1),
                  pl.BlockSpec((2, q, q), lambda b, s: (0, 0, 0)),
                  pl.BlockSpec((2, 1, LANES), lambda b, s: (0, 0, 0))],
        out_specs=[yspec(0), yspec(1)],
        out_shape=[jax.ShapeDtypeStruct((rw.rows, 1024), BF16)] * 2,
        scratch_shapes=[pltpu.VMEM((2, q, 1024), F32)],
        compiler_params=_cparams(2),
        name="ssd_scan",
    )(act, act, dt2, dt2, tri, avec)


def _cmul(ar, ai, br, bi):
    return ar * br - ai * bi, ar * bi + ai * br


def _s5_weight_kernel(lre_ref, lim_ref, ls_ref, bre_ref, bim_ref, cre_ref, cim_ref,
                      wsr_ref, wsi_ref, wor_ref, woi_ref, kt_ref, are_ref, aim_ref):
    lre, lim = lre_ref[...], lim_ref[...]
    step = jnp.exp(ls_ref[...])
    er, ei = lre * step, lim * step
    npow = 24
    p = lax.broadcasted_iota(jnp.int32, (1, npow, 1), 1).astype(F32)
    mag = jnp.exp(p * er)
    pre, pim = mag * jnp.cos(p * ei), mag * jnp.sin(p * ei)
    a_re, a_im = pre[:, 1:2, :], pim[:, 1:2, :]
    den = lre * lre + lim * lim
    q_re = ((a_re - 1.0) * lre + a_im * lim) / den
    q_im = (a_im * lre - (a_re - 1.0) * lim) / den
    bb_re, bb_im = _cmul(q_re, q_im, bre_ref[...], bim_ref[...])
    c_re, c_im = cre_ref[...], cim_ref[...]
    ws_r, ws_i, wo_r, wo_i, ca_r, ca_i = [], [], [], [], [], []
    for t in range(S5_Q):
        r, i = _cmul(bb_re, bb_im, pre[:, t:t + 1, :], pim[:, t:t + 1, :])
        ws_r.append(r)
        ws_i.append(i)
        r, i = _cmul(c_re, c_im, pre[:, t:t + 1, :], pim[:, t:t + 1, :])
        ca_r.append(r)
        ca_i.append(i)
        r, i = _cmul(c_re, c_im, pre[:, t + 1:t + 2, :], pim[:, t + 1:t + 2, :])
        wo_r.append(r)
        wo_i.append(-i)
    cat = lambda xs: jnp.concatenate(xs, axis=1)
    wsr_ref[...] = cat(ws_r)
    wsi_ref[...] = cat(ws_i)
    wor_ref[...] = cat(wo_r)
    woi_ref[...] = cat(wo_i)
    bdot = lambda a, b: lax.dot_general(a, b, (((2,), (2,)), ((0,), (0,))), preferred_element_type=F32,
                                        precision=lax.Precision.HIGHEST)
    kt_ref[...] = bdot(cat(ca_r), bb_re) - bdot(cat(ca_i), bb_im)
    are_ref[...] = pre[:, S5_Q:S5_Q + 1, :]
    aim_ref[...] = pim[:, S5_Q:S5_Q + 1, :]


def _s5_weights(lam_re, lam_im, log_step, b_re, b_im, c_re, c_im):
    nd, ng, ns = lam_re.shape
    G = nd * ng
    ch = S5_GROUP
    gb = 8
    qc = S5_Q * ch
    f = lambda a: a.astype(F32)
    args = (f(lam_re).reshape(G, 1, ns), f(lam_im).reshape(G, 1, ns), f(log_step).reshape(G, 1, 1),
            f(b_re).reshape(G, ns, ch).transpose(0, 2, 1), f(b_im).reshape(G, ns, ch).transpose(0, 2, 1),
            f(c_re).reshape(G, ch, ns), f(c_im).reshape(G, ch, ns))
    spec = lambda a: pl.BlockSpec((gb,) + a.shape[1:], lambda i: (i, 0, 0))
    oshape = [jax.ShapeDtypeStruct((G, qc, ns), F32)] * 4 + [jax.ShapeDtypeStruct((G, qc, ch), F32)] \
        + [jax.ShapeDtypeStruct((G, 1, ns), F32)] * 2
    wsr, wsi, wor, woi, kt, a_re, a_im = pl.pallas_call(
        _s5_weight_kernel,
        grid=(G // gb,),
        in_specs=[spec(a) for a in args],
        out_specs=[pl.BlockSpec((gb,) + s.shape[1:], lambda i: (i, 0, 0)) for s in oshape],
        out_shape=oshape,
        compiler_params=_cparams(1),
        name="s5_weights",
    )(*args)

    def by_dir(w, flip_dir):
        w = w.reshape(nd, ng, S5_Q, ch, ns)
        w = jnp.stack([jnp.flip(w[d], axis=1) if d == flip_dir else w[d] for d in range(nd)])
        return w.reshape(nd, ng, qc, ns)

    def pack(w):
        z = jnp.zeros_like(w)
        even = (np.arange(ng) % 2 == 0)[None, :, None, None]
        return jnp.where(even, jnp.concatenate([w, z], axis=-1), jnp.concatenate([z, w], axis=-1)).astype(BF16)

    ws_r, ws_i = pack(by_dir(wsr, 0)), pack(by_dir(wsi, 0))
    wo_r, wo_i = pack(by_dir(wor, 1)), pack(by_dir(woi, 1))
    k = kt.astype(BF16).reshape(nd, ng, S5_Q, ch, ch).transpose(0, 1, 4, 2, 3)
    kf = k.reshape(nd, ng, ch, qc)
    kb = jnp.flip(k, axis=3).reshape(nd, ng, ch, qc)
    rows_f, rows_b = [], []
    for j in range(S5_Q):
        z_f = jnp.zeros((ng, ch, j * ch), BF16)
        z_b = jnp.zeros((ng, ch, (S5_Q - 1 - j) * ch), BF16)
        rows_f.append(jnp.concatenate([z_f, kf[0, :, :, :(S5_Q - j) * ch]], axis=-1))
        rows_b.append(jnp.concatenate([kb[1, :, :, (S5_Q - 1 - j) * ch:], z_b], axis=-1))
    bt = jnp.stack([jnp.stack(rows_f, axis=1), jnp.stack(rows_b, axis=1)]).reshape(nd, ng, qc, qc)
    pair = lambda a: a.reshape(nd, ng // 2, 1, 2 * ns)
    return bt, ws_r, ws_i, wo_r, wo_i, pair(a_re), pair(a_im)


S5_GB = LANES // S5_GROUP


def _s5_kernel(B, nct, nlt, uj_ref, perm_ref, bt_ref, wsr_ref, wsi_ref, wor_ref, woi_ref, are_ref, aim_ref, yj_ref,
               x_scr, y_scr, s_re, s_im):
    gb, npair, qc = S5_GB, S5_GB // 2, S5_Q * S5_GROUP
    lhs = jnp.concatenate([uj_ref[j] for j in range(S5_Q)], axis=1)
    for m in range(gb):
        x_scr[:, m * qc:(m + 1) * qc] = jnp.dot(lhs, perm_ref[:, m * qc:(m + 1) * qc],
                                                preferred_element_type=F32).astype(BF16)
    xg = lambda g: x_scr[:, g * qc:(g + 1) * qc]
    for d in range(2):
        for pr in range(npair):
            for dst, w_ref in ((s_re, wsr_ref), (s_im, wsi_ref)):
                dst[d, pr] = (jnp.dot(xg(2 * pr), w_ref[d, 2 * pr], preferred_element_type=F32)
                              + jnp.dot(xg(2 * pr + 1), w_ref[d, 2 * pr + 1], preferred_element_type=F32))
    chains = [(d, pr, b) for d in range(2) for pr in range(npair) for b in range(B)]
    coef = {(d, pr): (are_ref[d, pr], aim_ref[d, pr]) for d in range(2) for pr in range(npair)}
    ctx0 = B * nlt

    def body(s, carry):
        in_ctx = s < nct
        rows = {}
        for d in range(2):
            kc = s if d == 0 else nct - 1 - s
            kl = s - nct if d == 0 else nlt - 1 - (s - nct)
            for b in range(B):
                rows[(d, b)] = pl.ds(jnp.where(in_ctx, ctx0 + b * nct + kc, b * nlt + kl), 1)
        contrib = [(s_re[d, pr, rows[(d, b)], :], s_im[d, pr, rows[(d, b)], :]) for d, pr, b in chains]
        new = []
        for (d, pr, b), (hr, hi), (sr, si) in zip(chains, carry, contrib):
            ar, ai = coef[(d, pr)]
            s_re[d, pr, rows[(d, b)], :] = hr
            s_im[d, pr, rows[(d, b)], :] = hi
            new.append((ar * hr - ai * hi + sr, ar * hi + ai * hr + si))
        return tuple(new)

    zero = jnp.zeros((1, LANES), F32)
    lax.fori_loop(0, nct + nlt, body, tuple((zero, zero) for _ in chains))
    for g in range(gb):
        acc = None
        for d in range(2):
            t = (jnp.dot(xg(g), bt_ref[d, g], preferred_element_type=F32)
                 + _nt(s_re[d, g // 2].astype(BF16), wor_ref[d, g])
                 + _nt(s_im[d, g // 2].astype(BF16), woi_ref[d, g]))
            acc = t if acc is None else acc + t
        y_scr[:, g * qc:(g + 1) * qc] = acc.astype(BF16)
    for i in range(S5_Q):
        yj_ref[i] = _nt(y_scr[...], perm_ref[i * LANES:(i + 1) * LANES, :]).astype(BF16)


def _s5(rw, uj, weights):
    B, T, C = rw.B, rw.T, rw.C
    bt, ws_r, ws_i, wo_r, wo_i, a_re, a_im = weights
    ng = bt.shape[1]
    q, gb = S5_Q, S5_GB
    nct, nlt = C // q, T // q
    nrow = uj.shape[1]
    qc = q * S5_GROUP
    k = gb * qc
    idx = np.arange(k)
    j, m, c = idx // LANES, (idx % LANES) // S5_GROUP, idx % S5_GROUP
    perm = np.zeros((k, k), np.float32)
    perm[idx, m * qc + j * S5_GROUP + c] = 1.0
    once = dict(pipeline_mode=pl.Buffered(1))
    wspec = lambda n: pl.BlockSpec((2, gb, qc, n), lambda i: (0, i, 0, 0), **once)
    aspec = pl.BlockSpec((2, gb // 2, 1, LANES), lambda i: (0, i, 0, 0))
    return pl.pallas_call(
        functools.partial(_s5_kernel, B, nct, nlt),
        grid=(ng // gb,),
        in_specs=[pl.BlockSpec((q, nrow, LANES), lambda i: (0, 0, i), **once),
                  pl.BlockSpec((k, k), lambda i: (0, 0), **once),
                  wspec(qc), wspec(LANES), wspec(LANES), wspec(LANES), wspec(LANES), aspec, aspec],
        out_specs=pl.BlockSpec((q, nrow, LANES), lambda i: (0, 0, i)),
        out_shape=jax.ShapeDtypeStruct(uj.shape, BF16),
        scratch_shapes=[pltpu.VMEM((nrow, k), BF16), pltpu.VMEM((nrow, k), BF16),
                        pltpu.VMEM((2, gb // 2, nrow, LANES), F32), pltpu.VMEM((2, gb // 2, nrow, LANES), F32)],
        compiler_params=_cparams(1),
        name="s5_scan",
    )(uj, jnp.asarray(perm, BF16), bt, ws_r, ws_i, wo_r, wo_i, a_re, a_im)


def _gelu_tanh(x):
    return 0.5 * x * (1.0 + jnp.tanh(math.sqrt(2.0 / math.pi) * (x + 0.044715 * (x * x * x))))


def _ssm_outproj_kernel(y0_ref, y1_ref, xs_ref, z_ref, v_ref, u_ref, dsk_ref, nw_ref, s5d_ref, gw_ref, gb_ref,
                        x_ref, w_ref, g1_ref, gn_ref, sh2_ref, sc2_ref, wr_ref, br_ref, lt_ref,
                        xo_ref, h2_ref, rt_ref, cnt_ref, carry, v_scr):
    i = pl.program_id(0)
    y = y0_ref[...].astype(F32) + y1_ref[...].astype(F32) + dsk_ref[...] * xs_ref[...].astype(F32)
    y = _rms(y * _silu(z_ref[...].astype(F32))) * nw_ref[...]
    ntile = v_scr.shape[0]
    nchunk = v_scr.shape[1] // S5_Q
    for j in range(S5_Q):
        for t in range(ntile):
            v_scr[t, pl.ds(j, nchunk, stride=S5_Q), :] = v_ref[j, :, t * LANES:(t + 1) * LANES].astype(F32)
    s5_y = jnp.concatenate([v_scr[t] for t in range(ntile)], axis=1)
    v = _gelu_tanh(s5_y + s5d_ref[...] * u_ref[...].astype(F32))
    v = v * _sigmoid(jnp.dot(v.astype(BF16), gw_ref[...], preferred_element_type=F32) + gb_ref[...])
    mix = jnp.concatenate([y, v], axis=1).astype(BF16)
    yo = jnp.dot(mix, w_ref[...], preferred_element_type=F32)
    _post_mixer(i, x_ref[...], yo, g1_ref[...], gn_ref[...], sh2_ref[...], sc2_ref[...], wr_ref, br_ref, lt_ref,
                xo_ref, h2_ref, rt_ref, cnt_ref, carry)


def _ssm_outproj(rw, ssd_y, act, z, s5_y, u, d_skip, norm_w, s5_d, glu_w, glu_b, xall, w_out, mods, norm_ffn, wr, br):
    D, tm = rw.D, rw.tm
    ntiles = rw.nlat
    post_in, post_out = _post_specs(rw)
    row = lambda i: (i, 0)
    vec = lambda n: pl.BlockSpec((1, n), lambda i: (0, 0))
    dsk = jnp.repeat(d_skip.astype(F32), HEAD_DIM)[None, :]
    return pl.pallas_call(
        _ssm_outproj_kernel,
        grid=(ntiles,),
        in_specs=[pl.BlockSpec((tm, 1024), row), pl.BlockSpec((tm, 1024), row),
                  pl.BlockSpec((tm, 1024), row), pl.BlockSpec((tm, 1024), row),
                  pl.BlockSpec((S5_Q, tm // S5_Q, 512), lambda i: (0, i, 0)),
                  pl.BlockSpec((tm, 512), row), vec(1024), vec(1024), vec(512),
                  pl.BlockSpec((512, 512), lambda i: (0, 0)), vec(512),
                  pl.BlockSpec((tm, D), row), pl.BlockSpec((1536, D), lambda i: (0, 0)), _mod_spec(rw, 2)] + post_in,
        out_specs=post_out,
        out_shape=_post_shapes(ntiles * tm, D),
        scratch_shapes=[pltpu.VMEM((1, LANES), F32), pltpu.VMEM((512 // LANES, tm, LANES), F32)],
        compiler_params=_cparams(1),
        name="ssm_outproj_router",
    )(ssd_y[0], ssd_y[1], act, z, s5_y, u, dsk, norm_w[None, :], s5_d[None, :], glu_w.astype(BF16), glu_b[None, :],
      xall, w_out.astype(BF16), mods, norm_ffn[None, :], mods, mods, wr, br, _lower_tri(tm))


def kernel(x, c, ctx, c_ctx, mod_w, mod_b, norm_mix, norm_ffn, att_w_in, att_w_out, na_q_norm, na_k_norm, na_rel_bias, wa_q_norm, wa_k_norm, wa_sink, ssm_w_in, ssm_w_out, ssd_conv_w, ssd_conv_b, ssd_dt_bias, ssd_a_log, ssd_d, ssd_norm, s5_lambda_re, s5_lambda_im, s5_log_step, s5_b_re, s5_b_im, s5_c_re, s5_c_im, s5_d, s5_glu_w, s5_glu_b, moe_w_group, moe_b_group, moe_w_expert, moe_b_expert, moe_w13, moe_w2):
    B, T, D = x.shape
    C = ctx.shape[1]
    rw = _Rows(B, T, C, D, ROW_TILE)
    xl = x.reshape(B * T, D)
    xc = ctx.reshape(B * C, D)
    cm = jnp.concatenate([c, c_ctx[None, :], jnp.zeros((8 - B - 1, D), F32)], axis=0)
    mods = _modulation(cm, mod_w, mod_b)
    mods = mods.reshape(mods.shape[0], 8, 1, 6 * D)

    m0 = mods[0]
    qkv = _att_inproj(rw, xl, xc, m0, norm_mix[0], att_w_in[0], na_q_norm[0], na_k_norm[0], wa_q_norm[0],
                      wa_k_norm[0])
    na = _na_attention(rw, qkv, na_rel_bias[0])
    wa = _wa_attention(rw, qkv, wa_sink[0])
    cx = _ctx_attention(rw, qkv, wa_sink[0])
    wr, br = _router_weights(moe_w_group[0], moe_b_group[0], moe_w_expert[0], moe_b_expert[0])
    xall, h2, route, counts = _att_outproj(rw, na, wa, cx, xl, xc, att_w_out[0], m0, norm_ffn[0], wr, br)
    y1, y2 = _moe(h2, route, counts, moe_w13, moe_w2, 0)

    m1 = mods[1]
    xall, z, xbc, u, uj, dtr = _ssm_inproj(rw, xall, y1, y2, route, m0, m1, norm_mix[1], ssm_w_in[0])
    act, dt2 = _ssm_conv(rw, xbc, dtr, ssd_conv_w[0], ssd_conv_b[0], ssd_dt_bias[0])
    ssd_y = _ssd(rw, act, dt2, ssd_a_log[0])
    s5_w = _s5_weights(s5_lambda_re[0], s5_lambda_im[0], s5_log_step[0], s5_b_re[0], s5_b_im[0], s5_c_re[0],
                       s5_c_im[0])
    s5_y = _s5(rw, uj, s5_w)
    wr, br = _router_weights(moe_w_group[1], moe_b_group[1], moe_w_expert[1], moe_b_expert[1])
    xlat, h2, route, counts = _ssm_outproj(rw, ssd_y, act, z, s5_y, u, ssd_d[0], ssd_norm[0], s5_d[0], s5_glu_w[0],
                                           s5_glu_b[0], xall, ssm_w_out[0], m1, norm_ffn[1], wr, br)
    y1, y2 = _moe(h2, route, counts, moe_w13, moe_w2, 1)
    out = _combine(rw, rw.nlat, xlat, y1, y2, route, m1)
    return out.reshape(B, T, D)
```

```python
import functools
import math

import jax
import jax.numpy as jnp
import numpy as np
from jax import lax
from jax.experimental import pallas as pl
from jax.experimental.pallas import tpu as pltpu

F32 = jnp.float32
BF16 = jnp.bfloat16

EPS = 1e-6
NEG_INF = -1e30
GRID_W = 64
HEAD_DIM = 64
NA_KH = 8
NA_KW = 16
WA_BLOCK = 128
ROPE_BASE = 10000.0
SSD_CHUNK = 128
S5_GROUP = 16
S5_STATE = 64
MOE_GROUPS = 4
MOE_EPG = 8
MOE_EXPERTS = MOE_GROUPS * MOE_EPG

LANES = 128
ROW_TILE = 512
MOE_TILE = 256
VMEM_LIMIT = 56 * 1024 * 1024
MOE_VMEM_LIMIT = 60 * 1024 * 1024


def _cparams(n_axes, vmem=VMEM_LIMIT):
    return pltpu.CompilerParams(dimension_semantics=("arbitrary",) * n_axes, vmem_limit_bytes=vmem)


def _sigmoid(x):
    return 1.0 / (1.0 + jnp.exp(-x))


def _silu(x):
    return x * _sigmoid(x)


def _rms(x, eps=EPS):
    return x * lax.rsqrt(jnp.mean(x * x, axis=-1, keepdims=True) + eps)


def _ada_norm(x, g, shift, scale):
    return (_rms(x) * g) * (1.0 + scale) + shift


def _mod_kernel(c_ref, w_ref, b_ref, o_ref):
    a = _silu(c_ref[...])
    o_ref[...] = jnp.dot(a, w_ref[...], preferred_element_type=F32, precision=lax.Precision.HIGHEST) + b_ref[...]


def _modulation(cm, mod_w, mod_b):
    depth, d, n6 = mod_w.shape
    tn = 1024
    return pl.pallas_call(
        _mod_kernel,
        grid=(depth, n6 // tn),
        in_specs=[pl.BlockSpec((8, d), lambda l, j: (0, 0)),
                  pl.BlockSpec((None, d, tn), lambda l, j: (l, 0, j)),
                  pl.BlockSpec((None, 1, tn), lambda l, j: (l, 0, j))],
        out_specs=pl.BlockSpec((None, 8, tn), lambda l, j: (l, 0, j)),
        out_shape=jax.ShapeDtypeStruct((depth, 8, n6), F32),
        compiler_params=_cparams(2),
        name="modulation",
    )(cm, mod_w, mod_b.reshape(depth, 1, n6))


class _Rows:
    def __init__(self, B, T, C, D, tm):
        assert T % tm == 0 and (B * C) % tm == 0
        self.B, self.T, self.C, self.D, self.tm = B, T, C, D, tm
        self.tpb = T // tm
        self.nlat = B * self.tpb
        self.nctx = (B * C) // tm
        self.ntot = self.nlat + self.nctx
        self.rows = B * (T + C)

    def group(self, i):
        return jnp.where(i < self.nlat, i // self.tpb, self.B)


def _mod_spec(rw, col):
    return pl.BlockSpec((None, 1, rw.D), lambda i, *_: (rw.group(i), 0, col))


def _seg_norm(y, seg, gcol):
    ss = jnp.dot((y * y).astype(BF16), seg, preferred_element_type=F32)
    return y * lax.rsqrt(ss + EPS) * gcol


def _rope(y, cos, sin):
    w = y.shape[-1]
    lane = lax.broadcasted_iota(jnp.int32, y.shape, 1)
    first = (lane % 32) < 16
    partner = jnp.where(first, pltpu.roll(y, w - 16, 1), pltpu.roll(y, 16, 1))
    return y * cos + partner * sin


def _dup_halves(k):
    lane = lax.broadcasted_iota(jnp.int32, k.shape, 1)
    sw = pltpu.roll(k, 64, 1)
    return jnp.where(lane < 64, k, sw), jnp.where(lane < 64, sw, k)


def _att_inproj_kernel(nlat, xl_ref, xc_ref, g_ref, sh_ref, sc_ref, w_ref, gcol_ref, cos_ref, sin_ref, seg_ref,
                       o_ref, h_scr):
    i = pl.program_id(0)
    x = jnp.where(i < nlat, xl_ref[...], xc_ref[...])
    h_scr[...] = _ada_norm(x, g_ref[...], sh_ref[...], sc_ref[...]).astype(BF16)
    seg = seg_ref[...]
    cos2 = jnp.concatenate([cos_ref[...], cos_ref[...]], axis=1)
    sin2 = jnp.concatenate([sin_ref[...], sin_ref[...]], axis=1)
    for c in range(9):
        c0 = c * 256
        y = jnp.dot(h_scr[...], w_ref[:, c0:c0 + 256], preferred_element_type=F32)
        gcol = gcol_ref[:, c0:c0 + 256]
        if c in (0, 1, 2, 3):
            o_ref[:, c0:c0 + 256] = _seg_norm(y, seg, gcol).astype(BF16)
        elif c in (4, 5):
            o_ref[:, c0:c0 + 256] = y.astype(BF16)
        elif c in (6, 7):
            o_ref[:, c0:c0 + 256] = _rope(_seg_norm(y, seg, gcol), cos2, sin2).astype(BF16)
        else:
            lane = lax.broadcasted_iota(jnp.int32, y.shape, 1)
            yk = jnp.where(lane < 128, _seg_norm(y, seg, gcol), y)
            yr = jnp.where(lane < 128, _rope(yk, cos2, sin2), yk)
            k0, k1 = _dup_halves(yr[:, :128])
            v0, v1 = _dup_halves(yr[:, 128:])
            o_ref[:, 2048:2176] = k0.astype(BF16)
            o_ref[:, 2176:2304] = k1.astype(BF16)
            o_ref[:, 2304:2432] = v0.astype(BF16)
            o_ref[:, 2432:2560] = v1.astype(BF16)


def _rope_tables(T, tm):
    t = np.arange(T)
    d = np.arange(HEAD_DIM)
    nf = HEAD_DIM // 4
    inv = jnp.asarray(ROPE_BASE, F32) ** (-jnp.arange(nf, dtype=F32) / nf)
    pos = np.where((d // 32 == 0)[None, :], (t // GRID_W)[:, None], (t % GRID_W)[:, None])
    ang = jnp.asarray(pos, F32) * inv[d % nf][None, :]
    sign = np.where((d % 32) < 16, -1.0, 1.0).astype(np.float32)
    cos = jnp.cos(ang)
    sin = jnp.sin(ang) * sign[None, :]
    cos = jnp.concatenate([cos, jnp.ones((tm, HEAD_DIM), F32)], axis=0)
    sin = jnp.concatenate([sin, jnp.zeros((tm, HEAD_DIM), F32)], axis=0)
    return jnp.tile(cos, (1, 2)), jnp.tile(sin, (1, 2))


def _att_inproj(rw, xl, xc, mods, norm_g, w_in, na_qn, na_kn, wa_qn, wa_kn):
    D, tm = rw.D, rw.tm
    scale = HEAD_DIM ** -0.5
    gcol = jnp.concatenate([jnp.tile(na_qn * scale, 8), jnp.tile(na_kn, 8), jnp.ones((512,), F32),
                            jnp.tile(wa_qn * scale, 8), jnp.tile(wa_kn, 2), jnp.ones((128,), F32)])[None, :]
    cos, sin = _rope_tables(rw.T, tm)
    segn = np.arange(256) // 64
    seg = jnp.asarray((segn[:, None] == segn[None, :]).astype(np.float32) / 64.0, BF16)
    nlat, tpb = rw.nlat, rw.tpb
    return pl.pallas_call(
        functools.partial(_att_inproj_kernel, nlat),
        grid=(rw.ntot,),
        in_specs=[pl.BlockSpec((tm, D), lambda i: (jnp.minimum(i, nlat - 1), 0)),
                  pl.BlockSpec((tm, D), lambda i: (jnp.maximum(i - nlat, 0), 0)),
                  pl.BlockSpec((1, D), lambda i: (0, 0)),
                  _mod_spec(rw, 0), _mod_spec(rw, 1),
                  pl.BlockSpec((D, 2304), lambda i: (0, 0)),
                  pl.BlockSpec((1, 2304), lambda i: (0, 0)),
                  pl.BlockSpec((tm, 128), lambda i: (jnp.where(i < nlat, i % tpb, tpb), 0)),
                  pl.BlockSpec((tm, 128), lambda i: (jnp.where(i < nlat, i % tpb, tpb), 0)),
                  pl.BlockSpec((256, 256), lambda i: (0, 0))],
        out_specs=pl.BlockSpec((tm, 2560), lambda i: (i, 0)),
        out_shape=jax.ShapeDtypeStruct((rw.rows, 2560), BF16),
        scratch_shapes=[pltpu.VMEM((tm, D), BF16)],
        compiler_params=_cparams(1),
        name="att_inproj",
    )(xl, xc, norm_g[None, :], mods, mods, w_in.astype(BF16), gcol, cos, sin, seg)


def _route(lg, lt, carry):
    lane = lax.broadcasted_iota(jnp.int32, lg.shape, 1).astype(F32)
    gm = lane < MOE_GROUPS
    mg = jnp.max(jnp.where(gm, lg, NEG_INF), axis=-1, keepdims=True)
    eg = jnp.where(gm, jnp.exp(jnp.where(gm, lg, NEG_INF) - mg), 0.0)
    pg = eg / jnp.sum(eg, axis=-1, keepdims=True)
    ptop = jnp.max(pg, axis=-1, keepdims=True)
    gidx = jnp.min(jnp.where(gm & (pg == ptop), lane, 1e9), axis=-1, keepdims=True)
    lo = MOE_GROUPS + MOE_EPG * gidx
    em = (lane >= lo) & (lane < lo + MOE_EPG)
    le = jnp.where(em, lg, NEG_INF)
    ee = jnp.where(em, jnp.exp(le - jnp.max(le, axis=-1, keepdims=True)), 0.0)
    pe = ee / jnp.sum(ee, axis=-1, keepdims=True)
    v1 = jnp.max(jnp.where(em, pe, -1.0), axis=-1, keepdims=True)
    i1 = jnp.min(jnp.where(em & (pe == v1), lane, 1e9), axis=-1, keepdims=True)
    em2 = em & (lane != i1)
    v2 = jnp.max(jnp.where(em2, pe, -1.0), axis=-1, keepdims=True)
    i2 = jnp.min(jnp.where(em2 & (pe == v2), lane, 1e9), axis=-1, keepdims=True)
    den = v1 + v2
    w1 = v1 / den * ptop
    w2 = v2 / den * ptop
    e1 = i1 - MOE_GROUPS
    e2 = i2 - MOE_GROUPS
    m1 = lane == e1
    m2 = lane == e2
    oh = jnp.where(m1 | m2, 1.0, 0.0)
    cnt = jnp.dot(lt, oh.astype(BF16), preferred_element_type=F32) + carry
    r1 = jnp.sum(jnp.where(m1, cnt, 0.0), axis=-1, keepdims=True)
    r2 = jnp.sum(jnp.where(m2, cnt, 0.0), axis=-1, keepdims=True)
    route = jnp.where(lane == 0, e1, jnp.where(lane == 1, e2, jnp.where(lane == 2, w1, jnp.where(
        lane == 3, w2, jnp.where(lane == 4, r1, jnp.where(lane == 5, r2, 0.0))))))
    return route, carry + jnp.sum(oh, axis=0, keepdims=True)


def _post_mixer(i, x, y, g1, gn, sh2, sc2, wr_ref, br_ref, lt_ref, xo_ref, h2_ref, rt_ref, cnt_ref, carry):
    xn = x + g1 * y
    xo_ref[...] = xn
    h2 = _ada_norm(xn, gn, sh2, sc2)
    hb = h2.astype(BF16)
    hbf = hb.astype(F32)
    half = h2.shape[1] // 2
    h2_ref[...] = pltpu.pack_elementwise([h2[:, :half], h2[:, half:]], packed_dtype=BF16)
    hl = (h2 - hbf).astype(BF16)
    lg = (jnp.dot(hb, wr_ref[0], preferred_element_type=F32)
          + (jnp.dot(hb, wr_ref[1], preferred_element_type=F32) + jnp.dot(hl, wr_ref[0], preferred_element_type=F32))
          + br_ref[...])

    @pl.when(i == 0)
    def _():
        carry[...] = jnp.zeros_like(carry)

    route, newc = _route(lg, lt_ref[...], carry[...])
    rt_ref[...] = route
    carry[...] = newc
    cnt_ref[...] = newc


def _att_outproj_kernel(nlat, na_ref, wa_ref, cx_ref, xl_ref, xc_ref, w_ref, g1_ref, gn_ref, sh2_ref, sc2_ref,
                        wr_ref, br_ref, lt_ref, xo_ref, h2_ref, rt_ref, cnt_ref, carry):
    i = pl.program_id(0)
    lat = i < nlat
    mix = jnp.where(lat, jnp.concatenate([na_ref[...], wa_ref[...]], axis=1), cx_ref[...])
    y = jnp.dot(mix, w_ref[...], preferred_element_type=F32)
    x = jnp.where(lat, xl_ref[...], xc_ref[...])
    _post_mixer(i, x, y, g1_ref[...], gn_ref[...], sh2_ref[...], sc2_ref[...], wr_ref, br_ref, lt_ref,
                xo_ref, h2_ref, rt_ref, cnt_ref, carry)


def _router_weights(w_group, b_group, w_expert, b_expert):
    D = w_group.shape[0]
    pad = LANES - MOE_GROUPS - MOE_EXPERTS
    wr = jnp.concatenate([w_group, w_expert, jnp.zeros((D, pad), F32)], axis=1)
    br = jnp.concatenate([b_group, b_expert, jnp.zeros((pad,), F32)])[None, :]
    hi = wr.astype(BF16)
    lo = (wr - hi.astype(F32)).astype(BF16)
    return jnp.stack([hi, lo]), br


def _lower_tri(tm):
    r = np.arange(tm)
    return jnp.asarray((r[None, :] < r[:, None]).astype(np.float32), BF16)


def _post_specs(rw):
    D, tm = rw.D, rw.tm
    return ([pl.BlockSpec((1, D), lambda i: (0, 0)), _mod_spec(rw, 3), _mod_spec(rw, 4),
             pl.BlockSpec((2, D, LANES), lambda i: (0, 0, 0)), pl.BlockSpec((1, LANES), lambda i: (0, 0)),
             pl.BlockSpec((tm, tm), lambda i: (0, 0))],
            [pl.BlockSpec((tm, D), lambda i: (i, 0)), pl.BlockSpec((tm, D // 2), lambda i: (i, 0)),
             pl.BlockSpec((tm, LANES), lambda i: (i, 0)), pl.BlockSpec((1, LANES), lambda i: (0, 0))])


def _post_shapes(nrows, D):
    return [jax.ShapeDtypeStruct((nrows, D), F32), jax.ShapeDtypeStruct((nrows, D // 2), jnp.uint32),
            jax.ShapeDtypeStruct((nrows, LANES), F32), jax.ShapeDtypeStruct((1, LANES), F32)]


def _att_outproj(rw, na, wa, cx, xl, xc, w_out, mods, norm_ffn, wr, br):
    D, tm, nlat = rw.D, rw.tm, rw.nlat
    post_in, post_out = _post_specs(rw)
    latmap = lambda i: (jnp.minimum(i, nlat - 1), 0)
    ctxmap = lambda i: (jnp.maximum(i - nlat, 0), 0)
    return pl.pallas_call(
        functools.partial(_att_outproj_kernel, nlat),
        grid=(rw.ntot,),
        in_specs=[pl.BlockSpec((tm, 512), latmap), pl.BlockSpec((tm, 512), latmap), pl.BlockSpec((tm, D), ctxmap),
                  pl.BlockSpec((tm, D), latmap), pl.BlockSpec((tm, D), ctxmap),
                  pl.BlockSpec((D, D), lambda i: (0, 0)), _mod_spec(rw, 2)] + post_in,
        out_specs=post_out,
        out_shape=_post_shapes(rw.rows, D),
        scratch_shapes=[pltpu.VMEM((1, LANES), F32)],
        compiler_params=_cparams(1),
        name="att_outproj_router",
    )(na, wa, cx, xl, xc, w_out.astype(BF16), mods, norm_ffn[None, :], mods, mods, wr, br, _lower_tri(tm))


def _moe_kernel(te_ref, nu_ref, src_ref, nsrc_ref, hp_ref, w13_ref, w2_ref, o_ref, w13b, w2b, xa, xb):
    i = pl.program_id(0)
    prev = te_ref[jnp.maximum(i - 1, 0)]
    changed = (i == 0) | (te_ref[i] != prev)
    tg = xa.shape[0]

    @pl.when(changed)
    def _():
        w13b[...] = w13_ref[...].astype(BF16)
        w2b[...] = w2_ref[...].astype(BF16)

    @pl.when(i == 0)
    def _():
        def fetch(j, carry):
            xa[pl.ds(j, 1), :] = hp_ref[pl.ds(src_ref[0, j], 1), :]
            return carry

        lax.fori_loop(0, tg, fetch, 0, unroll=8)

    def step(cur, nxt):
        for j in range(tg):
            nxt[pl.ds(j, 1), :] = hp_ref[pl.ds(nsrc_ref[0, j], 1), :]
        ff = w2b.shape[0]
        half = cur.shape[1]
        w = cur[...]
        unpack = functools.partial(pltpu.unpack_elementwise, packed_dtype=BF16, unpacked_dtype=F32)
        x_lo = unpack(w, index=0).astype(BF16)
        x_hi = unpack(w, index=1).astype(BF16)
        a13 = (jnp.dot(x_lo, w13b[:half, :], preferred_element_type=F32)
               + jnp.dot(x_hi, w13b[half:, :], preferred_element_type=F32))
        act = _silu(a13[:, :ff]) * a13[:, ff:]
        o_ref[...] = jnp.dot(act.astype(BF16), w2b[...], preferred_element_type=F32).astype(BF16)

    used = i < nu_ref[0]

    @pl.when(used & (i % 2 == 0))
    def _():
        step(xa, xb)

    @pl.when(used & (i % 2 == 1))
    def _():
        step(xb, xa)

    @pl.when(i >= nu_ref[0])
    def _():
        o_ref[...] = jnp.zeros_like(o_ref)


def _moe(h2p, route, counts, w13, w2, layer):
    N = h2p.shape[0]
    D = 2 * h2p.shape[1]
    _, E, _, F2 = w13.shape
    tg = MOE_TILE
    nt = (2 * N) // tg + E
    e = route[:, 0:2].astype(jnp.int32)
    rank = route[:, 4:6].astype(jnp.int32)
    cnt = counts[0, :E].astype(jnp.int32)
    ntile_e = (cnt + tg - 1) // tg
    tile_end = jnp.cumsum(ntile_e)
    offs = (tile_end - ntile_e) * tg
    onehot = (e[:, :, None] == jnp.arange(E, dtype=jnp.int32)).astype(jnp.int32)
    dest = jnp.sum(onehot * offs, axis=-1) + rank
    src = jnp.zeros((nt * tg,), jnp.int32).at[dest.reshape(-1)].set(jnp.repeat(jnp.arange(N, dtype=jnp.int32), 2))
    tile_id = jnp.arange(nt, dtype=jnp.int32)
    nu = tile_end[-1:].astype(jnp.int32)
    te = jnp.sum((tile_end[None, :] <= jnp.minimum(tile_id, nu[0] - 1)[:, None]).astype(jnp.int32), axis=1)
    te = jnp.minimum(te, E - 1)
    ys = pl.pallas_call(
        _moe_kernel,
        grid_spec=pltpu.PrefetchScalarGridSpec(
            num_scalar_prefetch=2,
            grid=(nt,),
            in_specs=[pl.BlockSpec((None, 1, tg), lambda i, te, nu: (i, 0, 0), memory_space=pltpu.SMEM),
                      pl.BlockSpec((None, 1, tg), lambda i, te, nu: (jnp.minimum(i + 1, nt - 1), 0, 0),
                                   memory_space=pltpu.SMEM),
                      pl.BlockSpec((N, D // 2), lambda i, te, nu: (0, 0), pipeline_mode=pl.Buffered(1)),
                      pl.BlockSpec((None, None, D, F2), lambda i, te, nu: (layer, te[i], 0, 0)),
                      pl.BlockSpec((None, None, F2 // 2, D), lambda i, te, nu: (layer, te[i], 0, 0))],
            out_specs=pl.BlockSpec((tg, D), lambda i, te, nu: (i, 0)),
            scratch_shapes=[pltpu.VMEM((D, F2), BF16), pltpu.VMEM((F2 // 2, D), BF16),
                            pltpu.VMEM((tg, D // 2), jnp.uint32), pltpu.VMEM((tg, D // 2), jnp.uint32)]),
        out_shape=jax.ShapeDtypeStruct((nt * tg, D), BF16),
        compiler_params=_cparams(1, vmem=MOE_VMEM_LIMIT),
        name="moe_experts",
    )(te, nu, src.reshape(nt, 1, tg), src.reshape(nt, 1, tg), h2p, w13, w2)
    pick = lambda k: ys.at[dest[:, k]].get(mode="promise_in_bounds")
    return pick(0), pick(1)


def _moe_residual(x_ref, y1_ref, y2_ref, rt_ref, g2_ref):
    rt = rt_ref[...]
    f = rt[:, 2:3] * y1_ref[...].astype(F32) + rt[:, 3:4] * y2_ref[...].astype(F32)
    return x_ref[...] + g2_ref[...] * f


def _combine_kernel(x_ref, y1_ref, y2_ref, rt_ref, g2_ref, o_ref):
    o_ref[...] = _moe_residual(x_ref, y1_ref, y2_ref, rt_ref, g2_ref)


def _combine(rw, ntiles, xall, y1, y2, route, mods):
    D, tm = rw.D, rw.tm
    row = lambda i: (i, 0)
    return pl.pallas_call(
        _combine_kernel,
        grid=(ntiles,),
        in_specs=[pl.BlockSpec((tm, D), row), pl.BlockSpec((tm, D), row), pl.BlockSpec((tm, D), row),
                  pl.BlockSpec((tm, LANES), row), _mod_spec(rw, 5)],
        out_specs=pl.BlockSpec((tm, D), row),
        out_shape=jax.ShapeDtypeStruct((ntiles * tm, D), F32),
        compiler_params=_cparams(1),
        name="moe_combine",
    )(xall, y1, y2, route, mods)


NA_QROWS = 8
NA_KROWS = 16


def _na_first_key_row(variant, a):
    return (max(a - 4, 0) + 4, a, min(a, 4))[variant]


def _na_key_lanes(row0):
    a = row0 // GRID_W
    starts = [_na_first_key_row(v, a) for v in range(3)]
    lo = (min(starts) * GRID_W) // LANES * LANES
    hi = -(-((max(starts) + NA_KH) * GRID_W) // LANES) * LANES
    return lo, hi


def _na_bias_tiles(rpb):
    H = rpb.shape[0]
    i = np.arange(GRID_W)
    c0 = np.clip(i - NA_KW // 2, 0, GRID_W - NA_KW)
    j = np.arange(GRID_W)
    colvalid = (j[None, :] >= c0[:, None]) & (j[None, :] < c0[:, None] + NA_KW)
    dc = np.clip(j[None, :] - i[:, None] + NA_KW - 1, 0, 2 * NA_KW - 2)
    onehot = ((dc[None] == np.arange(2 * NA_KW - 1)[:, None, None]) & colvalid[None]).astype(np.float32)
    tiles = jnp.einsum('hrc,cij->hrij', rpb.astype(F32), jnp.asarray(onehot), precision=lax.Precision.HIGHEST)
    tiles = tiles + jnp.asarray(np.where(colvalid, 0.0, NEG_INF).astype(np.float32))
    return tiles.transpose(0, 2, 1, 3).reshape(H, GRID_W, (2 * NA_KH - 1) * GRID_W)


def _na_fill_bias(variant, tiles_ref, bias_scr):
    for hh in range(2):
        for a in range(NA_QROWS):
            start = _na_first_key_row(variant, a)
            dr0 = start - a + 3
            rows = slice(a * GRID_W, (a + 1) * GRID_W)
            w0, w1 = start * GRID_W, (start + NA_KH) * GRID_W
            if w0 > 0:
                bias_scr[hh, rows, 0:w0] = jnp.full((GRID_W, w0), NEG_INF, F32)
            bias_scr[hh, rows, w0:w1] = tiles_ref[hh, :, dr0 * GRID_W:(dr0 + NA_KH) * GRID_W]
            if w1 < NA_KROWS * GRID_W:
                bias_scr[hh, rows, w1:] = jnp.full((GRID_W, NA_KROWS * GRID_W - w1), NEG_INF, F32)


def _softmax_pv(parts, extra=None, rc=64):
    m_rows = parts[0][0].shape[0]
    probs = [[] for _ in parts]
    inv_l = []
    for r0 in range(0, m_rows, rc):
        sc = []
        for s, _, bias_fn, lanes_fn in parts:
            l0, l1 = (0, s.shape[1]) if lanes_fn is None else lanes_fn(r0)
            c = s[r0:r0 + rc, l0:l1]
            if bias_fn is not None:
                c = c + bias_fn(r0, rc, slice(l0, l1))
            sc.append((c, l0, s.shape[1] - l1))
        mx = functools.reduce(jnp.maximum, [jnp.max(c, axis=-1, keepdims=True) for c, _, _ in sc])
        if extra is not None:
            mx = jnp.maximum(mx, extra[r0:r0 + rc])
        l = jnp.zeros_like(mx) if extra is None else jnp.exp(extra[r0:r0 + rc] - mx)
        for k, (c, before, after) in enumerate(sc):
            p = jnp.exp(c - mx)
            l = l + jnp.sum(p, axis=-1, keepdims=True)
            row = [jnp.zeros((rc, before), BF16)] * (before > 0) + [p.astype(BF16)] \
                + [jnp.zeros((rc, after), BF16)] * (after > 0)
            probs[k].append(row[0] if len(row) == 1 else jnp.concatenate(row, axis=1))
        inv_l.append(1.0 / l)
    o = None
    for k, (_, v, _, _) in enumerate(parts):
        pv = jnp.dot(jnp.concatenate(probs[k], axis=0), v, preferred_element_type=F32)
        o = pv if o is None else o + pv
    return o * jnp.concatenate(inv_l, axis=0)


def _nt(a, b):
    return lax.dot_general(a, b, (((1,), (1,)), ((), ())), preferred_element_type=F32)


NA_REFS_PER_BATCH = 11


def _na_kernel(n_rb, nb, *refs):
    tiles_ref, o_ref, bias_ref = refs[nb * NA_REFS_PER_BATCH:]
    rb = pl.program_id(1)
    for variant, at_rb in ((0, 0), (1, 1), (2, n_rb - 1)):
        @pl.when(rb == at_rb)
        def _(variant=variant):
            _na_fill_bias(variant, tiles_ref, bias_ref)

    for b in range(nb):
        q_ref, k0, k1, k2, k3, v0, v1, v2, v3, kc_ref, vc_ref = refs[b * NA_REFS_PER_BATCH:(b + 1) * NA_REFS_PER_BATCH]
        q2 = q_ref[...]
        kw = jnp.concatenate([k0[...], k1[...], k2[...], k3[...]], axis=0)
        vw = jnp.concatenate([v0[...], v1[...], v2[...], v3[...]], axis=0)
        kc = kc_ref[...]
        vc = vc_ref[...]
        lane = lax.broadcasted_iota(jnp.int32, q2.shape, 1)
        out = jnp.zeros(q2.shape, F32)
        for hh in range(2):
            m = (lane < HEAD_DIM) if hh == 0 else (lane >= HEAD_DIM)
            qm = jnp.where(m, q2, jnp.zeros_like(q2))
            o = _softmax_pv([(_nt(qm, kw), vw, lambda r0, rc, lanes, hh=hh: bias_ref[hh, r0:r0 + rc, lanes],
                              _na_key_lanes), (_nt(qm, kc), vc, None, None)], rc=32)
            out = jnp.where(m, o, out)
        o_ref[b] = out.astype(BF16)


def _na_attention(rw, qkv, rpb):
    B, T, C = rw.B, rw.T, rw.C
    tq = NA_QROWS * GRID_W
    tk = tq // 2
    n_rb = T // tq
    nkb = T // tk
    assert T % tq == 0 and n_rb >= 2 and (B * T) % C == 0
    tiles = _na_bias_tiles(rpb)
    ctxrow = (B * T) // C

    def batch_specs(b):
        kv = lambda j, col: pl.BlockSpec(
            (tk, LANES), lambda p, rb: (b * nkb + jnp.clip(2 * rb - 1 + j, 0, nkb - 1), col + p))
        return ([pl.BlockSpec((tq, LANES), lambda p, rb: (b * n_rb + rb, p))]
                + [kv(j, 4) for j in range(4)] + [kv(j, 8) for j in range(4)]
                + [pl.BlockSpec((C, LANES), lambda p, rb: (ctxrow + b, 4 + p)),
                   pl.BlockSpec((C, LANES), lambda p, rb: (ctxrow + b, 8 + p))])

    out = pl.pallas_call(
        functools.partial(_na_kernel, n_rb, B),
        grid=(4, n_rb),
        in_specs=sum([batch_specs(b) for b in range(B)], [])
        + [pl.BlockSpec((2,) + tiles.shape[1:], lambda p, rb: (p, 0, 0))],
        out_specs=pl.BlockSpec((B, tq, LANES), lambda p, rb: (0, rb, p)),
        out_shape=jax.ShapeDtypeStruct((B, T, 4 * LANES), BF16),
        scratch_shapes=[pltpu.VMEM((2, tq, 2 * tq), F32)],
        compiler_params=_cparams(2),
        name="neighbourhood_attention",
    )(*([qkv] * (B * NA_REFS_PER_BATCH)), tiles)
    return out.reshape(B * T, 4 * LANES)


WA_QBLOCKS = 2


def _wa_kernel(nb, sink_ref, q_ref, *refs):
    step = pl.program_id(1)
    blk = WA_BLOCK
    nkb = WA_QBLOCKS + 2
    kblocks, vblocks = refs[:nkb], refs[nkb:2 * nkb]
    kx_ref, vx_ref, o_ref = refs[2 * nkb:]
    lane = lax.broadcasted_iota(jnp.int32, (blk, LANES), 1)
    zero = jnp.zeros((blk, LANES), BF16)
    qi = lax.broadcasted_iota(jnp.int32, (blk, 3 * blk), 0)
    ks = lax.broadcasted_iota(jnp.int32, (blk, 3 * blk), 1)
    for qb in range(WA_QBLOCKS):
        n = step * WA_QBLOCKS + qb
        lo = jnp.where(n > 0, 0, blk)
        hi = jnp.where(n < nb - 1, 3 * blk, 2 * blk)
        valid = (ks >= qi) & (ks <= qi + 2 * blk) & (ks >= lo) & (ks < hi)
        band_mask = jnp.where(valid, 0.0, NEG_INF)
        rows = slice(qb * blk, (qb + 1) * blk)
        for kv in range(2):
            parts = []
            for pr in range(2):
                c0 = kv * 2 * LANES + pr * LANES
                qp = q_ref[rows, c0:c0 + LANES]
                parts += [jnp.where(lane < HEAD_DIM, qp, zero), jnp.where(lane >= HEAD_DIM, qp, zero)]
            qs = jnp.concatenate(parts, axis=0)
            cs = slice(kv * LANES, (kv + 1) * LANES)
            kb = jnp.concatenate([r[:, cs] for r in kblocks[qb:qb + 3]], axis=0)
            vb = jnp.concatenate([r[:, cs] for r in vblocks[qb:qb + 3]], axis=0)
            sink = jnp.concatenate([jnp.full((blk, 1), sink_ref[kv * 4 + g], F32) for g in range(4)], axis=0)
            o = _softmax_pv([(_nt(qs, kb), vb, lambda r0, rc, lanes, m=band_mask: m[r0 % blk:r0 % blk + rc, lanes], None),
                             (_nt(qs, kx_ref[:, cs]), vx_ref[:, cs], None, None)], extra=sink, rc=64)
            c0 = kv * 2 * LANES
            o_ref[rows, c0:c0 + LANES] = jnp.where(lane < HEAD_DIM, o[0:blk], o[blk:2 * blk]).astype(BF16)
            o_ref[rows, c0 + LANES:c0 + 2 * LANES] = jnp.where(
                lane < HEAD_DIM, o[2 * blk:3 * blk], o[3 * blk:4 * blk]).astype(BF16)


def _wa_attention(rw, qkv, sink):
    B, T, C = rw.B, rw.T, rw.C
    blk = WA_BLOCK
    nb = T // blk
    nq = WA_QBLOCKS
    assert nb % nq == 0
    ctxrow = (B * T) // C

    def kvspec(j, col):
        return pl.BlockSpec((blk, 2 * LANES), lambda b, s: (b * nb + jnp.clip(nq * s - 1 + j, 0, nb - 1), col))

    return pl.pallas_call(
        functools.partial(_wa_kernel, nb),
        grid=(B, nb // nq),
        in_specs=[pl.BlockSpec(memory_space=pltpu.SMEM),
                  pl.BlockSpec((nq * blk, 4 * LANES), lambda b, s: (b * (nb // nq) + s, 3))]
        + [kvspec(j, 8) for j in range(nq + 2)] + [kvspec(j, 9) for j in range(nq + 2)]
        + [pl.BlockSpec((C, 2 * LANES), lambda b, s: (ctxrow + b, 8)),
           pl.BlockSpec((C, 2 * LANES), lambda b, s: (ctxrow + b, 9))],
        out_specs=pl.BlockSpec((nq * blk, 4 * LANES), lambda b, s: (b * (nb // nq) + s, 0)),
        out_shape=jax.ShapeDtypeStruct((B * T, 4 * LANES), BF16),
        compiler_params=_cparams(2),
        name="window_attention",
    )(sink.astype(F32), qkv, *([qkv] * (2 * nq + 6)))


def _ctx_attn_kernel(sink_ref, t_ref, o_ref):
    C = t_ref.shape[0]
    lane = lax.broadcasted_iota(jnp.int32, (C, LANES), 1)
    zero = jnp.zeros((C, LANES), BF16)

    def pair(q2, k2, v2, sinks):
        out = jnp.zeros((C, LANES), F32)
        for hh in range(2):
            m = (lane < HEAD_DIM) if hh == 0 else (lane >= HEAD_DIM)
            extra = None if sinks is None else jnp.full((C, 1), sinks[hh], F32)
            o = _softmax_pv([(_nt(jnp.where(m, q2, zero), k2), v2, None, None)], extra=extra, rc=64)
            out = jnp.where(m, o, out)
        return out.astype(BF16)

    for p in range(4):
        c = p * LANES
        o_ref[:, c:c + LANES] = pair(t_ref[:, c:c + LANES], t_ref[:, 512 + c:640 + c], t_ref[:, 1024 + c:1152 + c], None)
    for kv in range(2):
        kd = t_ref[:, 2048 + kv * LANES:2176 + kv * LANES]
        vd = t_ref[:, 2304 + kv * LANES:2432 + kv * LANES]
        for pr in range(2):
            c = kv * 256 + pr * LANES
            h0 = kv * 4 + pr * 2
            o_ref[:, 512 + c:640 + c] = pair(t_ref[:, 1536 + c:1664 + c], kd, vd, (sink_ref[h0], sink_ref[h0 + 1]))


def _ctx_attention(rw, qkv, sink):
    B, T, C = rw.B, rw.T, rw.C
    ctxrow = (B * T) // C
    return pl.pallas_call(
        _ctx_attn_kernel,
        grid=(B,),
        in_specs=[pl.BlockSpec(memory_space=pltpu.SMEM),
                  pl.BlockSpec((C, qkv.shape[1]), lambda b: (ctxrow + b, 0))],
        out_specs=pl.BlockSpec((C, 8 * LANES), lambda b: (b, 0)),
        out_shape=jax.ShapeDtypeStruct((B * C, 8 * LANES), BF16),
        compiler_params=_cparams(1),
        name="context_attention",
    )(sink.astype(F32), qkv)


S5_Q = 16
CONV_TILE = 256
CONV_HALO = 16


def _ssm_inproj_kernel(x_ref, y1_ref, y2_ref, rt_ref, g2_ref, g_ref, sh_ref, sc_ref, w_ref,
                       xo_ref, z_ref, xbc_ref, u_ref, uj_ref, dt_ref, h_scr, u_scr):
    xn = _moe_residual(x_ref, y1_ref, y2_ref, rt_ref, g2_ref)
    xo_ref[...] = xn
    h_scr[...] = _ada_norm(xn, g_ref[...], sh_ref[...], sc_ref[...]).astype(BF16)

    def mm(c0, n):
        return jnp.dot(h_scr[...], w_ref[:, c0:c0 + n], preferred_element_type=F32)

    for c in range(4):
        z_ref[:, c * 256:(c + 1) * 256] = mm(c * 256, 256).astype(BF16)
    for c in range(6):
        xbc_ref[:, c * 256:(c + 1) * 256] = mm(1024 + c * 256, 256).astype(BF16)
    for c in range(2):
        y = mm(2560 + c * 256, 256)
        u_ref[:, c * 256:(c + 1) * 256] = y.astype(BF16)
        u_scr[2 * c] = y[:, :LANES]
        u_scr[2 * c + 1] = y[:, LANES:]
    dt_ref[...] = mm(3072, LANES)
    nchunk = u_scr.shape[1] // S5_Q
    for j in range(S5_Q):
        for t in range(u_scr.shape[0]):
            uj_ref[j, :, t * LANES:(t + 1) * LANES] = u_scr[t, pl.ds(j, nchunk, stride=S5_Q), :].astype(BF16)


def _ssm_inproj(rw, xall, y1, y2, route, prev_mods, mods, norm_g, w_in):
    D, tm = rw.D, rw.tm
    w = jnp.concatenate([w_in[:, 0:2560], w_in[:, 2592:3104], w_in[:, 2560:2592], jnp.zeros((D, LANES - 32), F32)],
                        axis=1).astype(BF16)
    row = lambda i: (i, 0)
    return pl.pallas_call(
        _ssm_inproj_kernel,
        grid=(rw.ntot,),
        in_specs=[pl.BlockSpec((tm, D), row), pl.BlockSpec((tm, D), row), pl.BlockSpec((tm, D), row),
                  pl.BlockSpec((tm, LANES), row), _mod_spec(rw, 5), pl.BlockSpec((1, D), lambda i: (0, 0)),
                  _mod_spec(rw, 0), _mod_spec(rw, 1), pl.BlockSpec((D, 3200), lambda i: (0, 0))],
        out_specs=[pl.BlockSpec((tm, D), row),
                   pl.BlockSpec((tm, 1024), row), pl.BlockSpec((tm, 1536), row), pl.BlockSpec((tm, 512), row),
                   pl.BlockSpec((S5_Q, tm // S5_Q, 512), lambda i: (0, i, 0)), pl.BlockSpec((tm, LANES), row)],
        out_shape=[jax.ShapeDtypeStruct((rw.rows, D), F32),
                   jax.ShapeDtypeStruct((rw.rows, 1024), BF16), jax.ShapeDtypeStruct((rw.rows, 1536), BF16),
                   jax.ShapeDtypeStruct((rw.rows, 512), BF16),
                   jax.ShapeDtypeStruct((S5_Q, rw.rows // S5_Q, 512), BF16),
                   jax.ShapeDtypeStruct((rw.rows, LANES), F32)],
        scratch_shapes=[pltpu.VMEM((tm, D), BF16), pltpu.VMEM((512 // LANES, tm, LANES), F32)],
        compiler_params=_cparams(1),
        name="ssm_inproj",
    )(xall, y1, y2, route, prev_mods, norm_g[None, :], mods, mods, w)


def _softplus(x):
    return jnp.maximum(x, 0.0) + jnp.log(1.0 + jnp.exp(-jnp.abs(x)))


def _conv_kernel(lat_tiles, tpb, cpb, x_ref, pv_ref, nx_ref, w_ref, b_ref, dtr_ref, dtb_ref, act_ref, dt_ref):
    i = pl.program_id(0)
    is_lat = i < lat_tiles
    pos = jnp.where(is_lat, i % tpb, (i - lat_tiles) % cpb)
    last_pos = jnp.where(is_lat, tpb - 1, cpb - 1)
    x = x_ref[...].astype(F32)
    tc = x.shape[0]
    prev_row = jnp.where(pos == 0, 0.0, pv_ref[...].astype(F32)[CONV_HALO - 1:CONV_HALO, :])
    next_row = jnp.where(pos == last_pos, 0.0, nx_ref[...].astype(F32)[0:1, :])
    row = lax.broadcasted_iota(jnp.int32, x.shape, 0)
    xm1 = jnp.where(row == 0, prev_row, pltpu.roll(x, 1, 0))
    xp1 = jnp.where(row == tc - 1, next_row, pltpu.roll(x, tc - 1, 0))
    y = w_ref[0:1, :] * xm1 + w_ref[1:2, :] * x + w_ref[2:3, :] * xp1 + b_ref[...]
    act_ref[...] = _silu(y).astype(BF16)
    sp = _softplus(dtr_ref[...] + dtb_ref[...])
    dt_ref[0] = sp
    dt_ref[1] = pltpu.roll(sp, LANES - 16, 1)


def _ssm_conv(rw, xbc, dtr, conv_w, conv_b, dt_bias):
    B, T, C = rw.B, rw.T, rw.C
    tc = CONV_TILE
    assert T % tc == 0 and C % tc == 0
    lat_tiles, tpb, cpb = (B * T) // tc, T // tc, C // tc
    ntiles = rw.rows // tc
    hpt = tc // CONV_HALO
    nhalo = rw.rows // CONV_HALO
    W = xbc.shape[1]
    dtb = jnp.concatenate([dt_bias.reshape(-1), jnp.zeros((LANES - 32,), F32)])[None, :]
    row = lambda i: (i, 0)
    return pl.pallas_call(
        functools.partial(_conv_kernel, lat_tiles, tpb, cpb),
        grid=(ntiles,),
        in_specs=[pl.BlockSpec((tc, W), row),
                  pl.BlockSpec((CONV_HALO, W), lambda i: (jnp.maximum(i * hpt - 1, 0), 0)),
                  pl.BlockSpec((CONV_HALO, W), lambda i: (jnp.minimum((i + 1) * hpt, nhalo - 1), 0)),
                  pl.BlockSpec((3, W), lambda i: (0, 0)), pl.BlockSpec((1, W), lambda i: (0, 0)),
                  pl.BlockSpec((tc, LANES), row), pl.BlockSpec((1, LANES), lambda i: (0, 0))],
        out_specs=[pl.BlockSpec((tc, W), row), pl.BlockSpec((2, tc, LANES), lambda i: (0, i, 0))],
        out_shape=[jax.ShapeDtypeStruct((rw.rows, W), BF16), jax.ShapeDtypeStruct((2, rw.rows, LANES), F32)],
        compiler_params=_cparams(1),
        name="ssm_conv",
    )(xbc, xbc, xbc, conv_w, conv_b[None, :], dtr, dtb)


def _ssd_kernel(nb, *refs):
    acts, dts = refs[0:2 * nb], refs[2 * nb:4 * nb]
    tri_ref, a_ref, yf_ref, yb_ref, hst = refs[4 * nb:]

    @pl.when(pl.program_id(0) == 0)
    def _():
        hst[...] = jnp.zeros_like(hst)

    for d, y_ref in enumerate((yf_ref, yb_ref)):
        for b in range(nb):
            _ssd_chunk(acts[d * nb + b], dts[d * nb + b], tri_ref[d], a_ref[d], y_ref.at[b], hst.at[d, b])


def _ssd_chunk(act_ref, dt_ref, tri, avec, y_ref, hst):
    q = SSD_CHUNK
    dt = dt_ref[...]
    da = dt * avec
    acs = jnp.dot(tri, da, preferred_element_type=F32, precision=lax.Precision.HIGHEST)
    tot = jnp.sum(da, axis=0, keepdims=True)
    acs_t = acs.T
    dt_t = dt.T
    eacs = jnp.exp(acs)
    wend = jnp.exp(tot - acs) * dt
    etot = jnp.exp(tot)
    mask = tri > 0.5
    left = lax.broadcasted_iota(jnp.int32, (q, LANES), 1) < HEAD_DIM
    left1 = lax.broadcasted_iota(jnp.int32, (1, LANES), 1) < HEAD_DIM
    for g in range(2):
        bg = act_ref[:, 1024 + g * 128:1152 + g * 128]
        cg = act_ref[:, 1280 + g * 128:1408 + g * 128]
        cb = _nt(cg, bg)
        hin = hst[:, g * 512:(g + 1) * 512]
        yoff = jnp.dot(cg, hin.astype(BF16), preferred_element_type=F32)
        xw, dec = [], []
        for pr in range(4):
            h_a = g * 8 + pr * 2
            h_b = h_a + 1
            c0 = h_a * HEAD_DIM
            x2 = act_ref[:, c0:c0 + LANES]
            outs = []
            for h in (h_a, h_b):
                seg = acs[:, h:h + 1] - acs_t[h:h + 1, :]
                w = cb * jnp.exp(jnp.where(mask, seg, NEG_INF)) * dt_t[h:h + 1, :]
                outs.append(jnp.dot(w.astype(BF16), x2, preferred_element_type=F32))
            yd = jnp.where(left, outs[0], outs[1])
            sc = jnp.where(left, eacs[:, h_a:h_a + 1], eacs[:, h_b:h_b + 1])
            y_ref[:, c0:c0 + LANES] = (yd + yoff[:, pr * LANES:(pr + 1) * LANES] * sc).astype(BF16)
            wsc = jnp.where(left, wend[:, h_a:h_a + 1], wend[:, h_b:h_b + 1])
            xw.append((x2.astype(F32) * wsc).astype(BF16))
            dec.append(jnp.where(left1, etot[:, h_a:h_a + 1], etot[:, h_b:h_b + 1]))
        bg_t = bg.astype(F32).T.astype(BF16)
        snew = jnp.dot(bg_t, jnp.concatenate(xw, axis=1), preferred_element_type=F32)
        hst[:, g * 512:(g + 1) * 512] = hin * jnp.concatenate(dec, axis=1) + snew


def _ssd(rw, act, dt2, a_log):
    B, T, C = rw.B, rw.T, rw.C
    q = SSD_CHUNK
    nct, nlt = C // q, T // q
    ctx0 = (B * T) // q
    r = np.arange(q)
    tri = jnp.asarray(np.stack([r[None, :] <= r[:, None], r[None, :] >= r[:, None]]).astype(np.float32))
    avec = jnp.concatenate([-jnp.exp(a_log.astype(F32)), jnp.zeros((2, LANES - a_log.shape[1]), F32)], axis=1)[:, None, :]

    def lat(d, s):
        return jnp.clip(s - nct, 0, nlt - 1) if d == 0 else nlt - 1 - jnp.clip(s - nct, 0, nlt - 1)

    def blk(d, b, s):
        kc = s if d == 0 else nct - 1 - s
        return jnp.where(s < nct, ctx0 + b * nct + kc, b * nlt + lat(d, s))

    pairs = [(d, b) for d in range(2) for b in range(B)]
    aspec = lambda d, b: pl.BlockSpec((q, act.shape[1]), lambda s: (blk(d, b, s), 0))
    dspec = lambda d, b: pl.BlockSpec((None, q, LANES), lambda s: (d, blk(d, b, s), 0))
    yspec = lambda d: pl.BlockSpec((B, q, 1024), lambda s: (0, lat(d, s), 0))
    yf, yb = pl.pallas_call(
        functools.partial(_ssd_kernel, B),
        grid=(nct + nlt,),
        in_specs=[aspec(d, b) for d, b in pairs] + [dspec(d, b) for d, b in pairs]
        + [pl.BlockSpec((2, q, q), lambda s: (0, 0, 0)), pl.BlockSpec((2, 1, LANES), lambda s: (0, 0, 0))],
        out_specs=[yspec(0), yspec(1)],
        out_shape=[jax.ShapeDtypeStruct((B, T, 1024), BF16)] * 2,
        scratch_shapes=[pltpu.VMEM((2, B, q, 1024), F32)],
        compiler_params=_cparams(1),
        name="ssd_scan",
    )(*([act] * (2 * B)), *([dt2] * (2 * B)), tri, avec)
    return yf.reshape(B * T, 1024), yb.reshape(B * T, 1024)


def _cmul(ar, ai, br, bi):
    return ar * br - ai * bi, ar * bi + ai * br


def _s5_weight_kernel(lre_ref, lim_ref, ls_ref, bre_ref, bim_ref, cre_ref, cim_ref,
                      wsr_ref, wsi_ref, wor_ref, woi_ref, kt_ref, are_ref, aim_ref):
    lre, lim = lre_ref[...], lim_ref[...]
    step = jnp.exp(ls_ref[...])
    er, ei = lre * step, lim * step
    npow = 24
    p = lax.broadcasted_iota(jnp.int32, (1, npow, 1), 1).astype(F32)
    mag = jnp.exp(p * er)
    pre, pim = mag * jnp.cos(p * ei), mag * jnp.sin(p * ei)
    a_re, a_im = pre[:, 1:2, :], pim[:, 1:2, :]
    den = lre * lre + lim * lim
    q_re = ((a_re - 1.0) * lre + a_im * lim) / den
    q_im = (a_im * lre - (a_re - 1.0) * lim) / den
    bb_re, bb_im = _cmul(q_re, q_im, bre_ref[...], bim_ref[...])
    c_re, c_im = cre_ref[...], cim_ref[...]
    ws_r, ws_i, wo_r, wo_i, ca_r, ca_i = [], [], [], [], [], []
    for t in range(S5_Q):
        r, i = _cmul(bb_re, bb_im, pre[:, t:t + 1, :], pim[:, t:t + 1, :])
        ws_r.append(r)
        ws_i.append(i)
        r, i = _cmul(c_re, c_im, pre[:, t:t + 1, :], pim[:, t:t + 1, :])
        ca_r.append(r)
        ca_i.append(i)
        r, i = _cmul(c_re, c_im, pre[:, t + 1:t + 2, :], pim[:, t + 1:t + 2, :])
        wo_r.append(r)
        wo_i.append(-i)
    cat = lambda xs: jnp.concatenate(xs, axis=1)
    wsr_ref[...] = cat(ws_r)
    wsi_ref[...] = cat(ws_i)
    wor_ref[...] = cat(wo_r)
    woi_ref[...] = cat(wo_i)
    bdot = lambda a, b: lax.dot_general(a, b, (((2,), (2,)), ((0,), (0,))), preferred_element_type=F32,
                                        precision=lax.Precision.HIGHEST)
    kt_ref[...] = bdot(cat(ca_r), bb_re) - bdot(cat(ca_i), bb_im)
    are_ref[...] = pre[:, S5_Q:S5_Q + 1, :]
    aim_ref[...] = pim[:, S5_Q:S5_Q + 1, :]


def _s5_weights(lam_re, lam_im, log_step, b_re, b_im, c_re, c_im):
    nd, ng, ns = lam_re.shape
    G = nd * ng
    ch = S5_GROUP
    gb = 8
    qc = S5_Q * ch
    f = lambda a: a.astype(F32)
    args = (f(lam_re).reshape(G, 1, ns), f(lam_im).reshape(G, 1, ns), f(log_step).reshape(G, 1, 1),
            f(b_re).reshape(G, ns, ch).transpose(0, 2, 1), f(b_im).reshape(G, ns, ch).transpose(0, 2, 1),
            f(c_re).reshape(G, ch, ns), f(c_im).reshape(G, ch, ns))
    spec = lambda a: pl.BlockSpec((gb,) + a.shape[1:], lambda i: (i, 0, 0))
    oshape = [jax.ShapeDtypeStruct((G, qc, ns), F32)] * 4 + [jax.ShapeDtypeStruct((G, qc, ch), F32)] \
        + [jax.ShapeDtypeStruct((G, 1, ns), F32)] * 2
    wsr, wsi, wor, woi, kt, a_re, a_im = pl.pallas_call(
        _s5_weight_kernel,
        grid=(G // gb,),
        in_specs=[spec(a) for a in args],
        out_specs=[pl.BlockSpec((gb,) + s.shape[1:], lambda i: (i, 0, 0)) for s in oshape],
        out_shape=oshape,
        compiler_params=_cparams(1),
        name="s5_weights",
    )(*args)

    def by_dir(w, flip_dir):
        w = w.reshape(nd, ng, S5_Q, ch, ns)
        w = jnp.stack([jnp.flip(w[d], axis=1) if d == flip_dir else w[d] for d in range(nd)])
        return w.reshape(nd, ng, qc, ns)

    def pack(w):
        z = jnp.zeros_like(w)
        even = (np.arange(ng) % 2 == 0)[None, :, None, None]
        return jnp.where(even, jnp.concatenate([w, z], axis=-1), jnp.concatenate([z, w], axis=-1)).astype(BF16)

    ws_r, ws_i = pack(by_dir(wsr, 0)), pack(by_dir(wsi, 0))
    wo_r, wo_i = pack(by_dir(wor, 1)), pack(by_dir(woi, 1))
    k = kt.astype(BF16).reshape(nd, ng, S5_Q, ch, ch).transpose(0, 1, 4, 2, 3)
    kf = k.reshape(nd, ng, ch, qc)
    kb = jnp.flip(k, axis=3).reshape(nd, ng, ch, qc)
    rows_f, rows_b = [], []
    for j in range(S5_Q):
        z_f = jnp.zeros((ng, ch, j * ch), BF16)
        z_b = jnp.zeros((ng, ch, (S5_Q - 1 - j) * ch), BF16)
        rows_f.append(jnp.concatenate([z_f, kf[0, :, :, :(S5_Q - j) * ch]], axis=-1))
        rows_b.append(jnp.concatenate([kb[1, :, :, (S5_Q - 1 - j) * ch:], z_b], axis=-1))
    bt = jnp.stack([jnp.stack(rows_f, axis=1), jnp.stack(rows_b, axis=1)]).reshape(nd, ng, qc, qc)
    pair = lambda a: a.reshape(nd, ng // 2, 1, 2 * ns)
    return bt, ws_r, ws_i, wo_r, wo_i, pair(a_re), pair(a_im)


S5_GB = LANES // S5_GROUP


def _s5_kernel(B, nct, nlt, uj_ref, perm_ref, bt_ref, wsr_ref, wsi_ref, wor_ref, woi_ref, are_ref, aim_ref, yj_ref,
               x_scr, y_scr, s_re, s_im):
    gb, npair, qc = S5_GB, S5_GB // 2, S5_Q * S5_GROUP
    lhs = jnp.concatenate([uj_ref[j] for j in range(S5_Q)], axis=1)
    for m in range(gb):
        x_scr[:, m * qc:(m + 1) * qc] = jnp.dot(lhs, perm_ref[:, m * qc:(m + 1) * qc],
                                                preferred_element_type=F32).astype(BF16)
    xg = lambda g: x_scr[:, g * qc:(g + 1) * qc]
    for d in range(2):
        for pr in range(npair):
            for dst, w_ref in ((s_re, wsr_ref), (s_im, wsi_ref)):
                dst[d, pr] = (jnp.dot(xg(2 * pr), w_ref[d, 2 * pr], preferred_element_type=F32)
                              + jnp.dot(xg(2 * pr + 1), w_ref[d, 2 * pr + 1], preferred_element_type=F32))
    chains = [(d, pr, b) for d in range(2) for pr in range(npair) for b in range(B)]
    coef = {(d, pr): (are_ref[d, pr], aim_ref[d, pr]) for d in range(2) for pr in range(npair)}
    ctx0 = B * nlt

    def body(s, carry):
        in_ctx = s < nct
        rows = {}
        for d in range(2):
            kc = s if d == 0 else nct - 1 - s
            kl = s - nct if d == 0 else nlt - 1 - (s - nct)
            for b in range(B):
                rows[(d, b)] = pl.ds(jnp.where(in_ctx, ctx0 + b * nct + kc, b * nlt + kl), 1)
        contrib = [(s_re[d, pr, rows[(d, b)], :], s_im[d, pr, rows[(d, b)], :]) for d, pr, b in chains]
        new = []
        for (d, pr, b), (hr, hi), (sr, si) in zip(chains, carry, contrib):
            ar, ai = coef[(d, pr)]
            s_re[d, pr, rows[(d, b)], :] = hr
            s_im[d, pr, rows[(d, b)], :] = hi
            new.append((ar * hr - ai * hi + sr, ar * hi + ai * hr + si))
        return tuple(new)

    zero = jnp.zeros((1, LANES), F32)
    lax.fori_loop(0, nct + nlt, body, tuple((zero, zero) for _ in chains))
    for g in range(gb):
        acc = None
        for d in range(2):
            t = (jnp.dot(xg(g), bt_ref[d, g], preferred_element_type=F32)
                 + _nt(s_re[d, g // 2].astype(BF16), wor_ref[d, g])
                 + _nt(s_im[d, g // 2].astype(BF16), woi_ref[d, g]))
            acc = t if acc is None else acc + t
        y_scr[:, g * qc:(g + 1) * qc] = acc.astype(BF16)
    for i in range(S5_Q):
        yj_ref[i] = _nt(y_scr[...], perm_ref[i * LANES:(i + 1) * LANES, :]).astype(BF16)


def _s5(rw, uj, weights):
    B, T, C = rw.B, rw.T, rw.C
    bt, ws_r, ws_i, wo_r, wo_i, a_re, a_im = weights
    ng = bt.shape[1]
    q, gb = S5_Q, S5_GB
    nct, nlt = C // q, T // q
    nrow = uj.shape[1]
    qc = q * S5_GROUP
    k = gb * qc
    idx = np.arange(k)
    j, m, c = idx // LANES, (idx % LANES) // S5_GROUP, idx % S5_GROUP
    perm = np.zeros((k, k), np.float32)
    perm[idx, m * qc + j * S5_GROUP + c] = 1.0
    once = dict(pipeline_mode=pl.Buffered(1))
    wspec = lambda n: pl.BlockSpec((2, gb, qc, n), lambda i: (0, i, 0, 0), **once)
    aspec = pl.BlockSpec((2, gb // 2, 1, LANES), lambda i: (0, i, 0, 0))
    return pl.pallas_call(
        functools.partial(_s5_kernel, B, nct, nlt),
        grid=(ng // gb,),
        in_specs=[pl.BlockSpec((q, nrow, LANES), lambda i: (0, 0, i), **once),
                  pl.BlockSpec((k, k), lambda i: (0, 0), **once),
                  wspec(qc), wspec(LANES), wspec(LANES), wspec(LANES), wspec(LANES), aspec, aspec],
        out_specs=pl.BlockSpec((q, nrow, LANES), lambda i: (0, 0, i)),
        out_shape=jax.ShapeDtypeStruct(uj.shape, BF16),
        scratch_shapes=[pltpu.VMEM((nrow, k), BF16), pltpu.VMEM((nrow, k), BF16),
                        pltpu.VMEM((2, gb // 2, nrow, LANES), F32), pltpu.VMEM((2, gb // 2, nrow, LANES), F32)],
        compiler_params=_cparams(1),
        name="s5_scan",
    )(uj, jnp.asarray(perm, BF16), bt, ws_r, ws_i, wo_r, wo_i, a_re, a_im)


def _gelu_tanh(x):
    return 0.5 * x * (1.0 + jnp.tanh(math.sqrt(2.0 / math.pi) * (x + 0.044715 * (x * x * x))))


def _ssm_outproj_kernel(y0_ref, y1_ref, xs_ref, z_ref, v_ref, u_ref, dsk_ref, nw_ref, s5d_ref, gw_ref, gb_ref,
                        x_ref, w_ref, g1_ref, gn_ref, sh2_ref, sc2_ref, wr_ref, br_ref, lt_ref,
                        xo_ref, h2_ref, rt_ref, cnt_ref, carry, v_scr):
    i = pl.program_id(0)
    y = y0_ref[...].astype(F32) + y1_ref[...].astype(F32) + dsk_ref[...] * xs_ref[...].astype(F32)
    y = _rms(y * _silu(z_ref[...].astype(F32))) * nw_ref[...]
    ntile = v_scr.shape[0]
    nchunk = v_scr.shape[1] // S5_Q
    for j in range(S5_Q):
        for t in range(ntile):
            v_scr[t, pl.ds(j, nchunk, stride=S5_Q), :] = v_ref[j, :, t * LANES:(t + 1) * LANES].astype(F32)
    s5_y = jnp.concatenate([v_scr[t] for t in range(ntile)], axis=1)
    v = _gelu_tanh(s5_y + s5d_ref[...] * u_ref[...].astype(F32))
    v = v * _sigmoid(jnp.dot(v.astype(BF16), gw_ref[...], preferred_element_type=F32) + gb_ref[...])
    mix = jnp.concatenate([y, v], axis=1).astype(BF16)
    yo = jnp.dot(mix, w_ref[...], preferred_element_type=F32)
    _post_mixer(i, x_ref[...], yo, g1_ref[...], gn_ref[...], sh2_ref[...], sc2_ref[...], wr_ref, br_ref, lt_ref,
                xo_ref, h2_ref, rt_ref, cnt_ref, carry)


def _ssm_outproj(rw, ssd_y, act, z, s5_y, u, d_skip, norm_w, s5_d, glu_w, glu_b, xall, w_out, mods, norm_ffn, wr, br):
    D, tm = rw.D, rw.tm
    ntiles = rw.nlat
    post_in, post_out = _post_specs(rw)
    row = lambda i: (i, 0)
    vec = lambda n: pl.BlockSpec((1, n), lambda i: (0, 0))
    dsk = jnp.repeat(d_skip.astype(F32), HEAD_DIM)[None, :]
    return pl.pallas_call(
        _ssm_outproj_kernel,
        grid=(ntiles,),
        in_specs=[pl.BlockSpec((tm, 1024), row), pl.BlockSpec((tm, 1024), row),
                  pl.BlockSpec((tm, 1024), row), pl.BlockSpec((tm, 1024), row),
                  pl.BlockSpec((S5_Q, tm // S5_Q, 512), lambda i: (0, i, 0)),
                  pl.BlockSpec((tm, 512), row), vec(1024), vec(1024), vec(512),
                  pl.BlockSpec((512, 512), lambda i: (0, 0)), vec(512),
                  pl.BlockSpec((tm, D), row), pl.BlockSpec((1536, D), lambda i: (0, 0)), _mod_spec(rw, 2)] + post_in,
        out_specs=post_out,
        out_shape=_post_shapes(ntiles * tm, D),
        scratch_shapes=[pltpu.VMEM((1, LANES), F32), pltpu.VMEM((512 // LANES, tm, LANES), F32)],
        compiler_params=_cparams(1),
        name="ssm_outproj_router",
    )(ssd_y[0], ssd_y[1], act, z, s5_y, u, dsk, norm_w[None, :], s5_d[None, :], glu_w.astype(BF16), glu_b[None, :],
      xall, w_out.astype(BF16), mods, norm_ffn[None, :], mods, mods, wr, br, _lower_tri(tm))


def kernel(x, c, ctx, c_ctx, mod_w, mod_b, norm_mix, norm_ffn, att_w_in, att_w_out, na_q_norm, na_k_norm, na_rel_bias, wa_q_norm, wa_k_norm, wa_sink, ssm_w_in, ssm_w_out, ssd_conv_w, ssd_conv_b, ssd_dt_bias, ssd_a_log, ssd_d, ssd_norm, s5_lambda_re, s5_lambda_im, s5_log_step, s5_b_re, s5_b_im, s5_c_re, s5_c_im, s5_d, s5_glu_w, s5_glu_b, moe_w_group, moe_b_group, moe_w_expert, moe_b_expert, moe_w13, moe_w2):
    B, T, D = x.shape
    C = ctx.shape[1]
    rw = _Rows(B, T, C, D, ROW_TILE)
    xl = x.reshape(B * T, D)
    xc = ctx.reshape(B * C, D)
    cm = jnp.concatenate([c, c_ctx[None, :], jnp.zeros((8 - B - 1, D), F32)], axis=0)
    mods = _modulation(cm, mod_w, mod_b)
    mods = mods.reshape(mods.shape[0], 8, 1, 6 * D)

    m0 = mods[0]
    qkv = _att_inproj(rw, xl, xc, m0, norm_mix[0], att_w_in[0], na_q_norm[0], na_k_norm[0], wa_q_norm[0],
                      wa_k_norm[0])
    na = _na_attention(rw, qkv, na_rel_bias[0])
    wa = _wa_attention(rw, qkv, wa_sink[0])
    cx = _ctx_attention(rw, qkv, wa_sink[0])
    wr, br = _router_weights(moe_w_group[0], moe_b_group[0], moe_w_expert[0], moe_b_expert[0])
    xall, h2, route, counts = _att_outproj(rw, na, wa, cx, xl, xc, att_w_out[0], m0, norm_ffn[0], wr, br)
    y1, y2 = _moe(h2, route, counts, moe_w13, moe_w2, 0)

    m1 = mods[1]
    xall, z, xbc, u, uj, dtr = _ssm_inproj(rw, xall, y1, y2, route, m0, m1, norm_mix[1], ssm_w_in[0])
    act, dt2 = _ssm_conv(rw, xbc, dtr, ssd_conv_w[0], ssd_conv_b[0], ssd_dt_bias[0])
    ssd_y = _ssd(rw, act, dt2, ssd_a_log[0])
    s5_w = _s5_weights(s5_lambda_re[0], s5_lambda_im[0], s5_log_step[0], s5_b_re[0], s5_b_im[0], s5_c_re[0],
                       s5_c_im[0])
    s5_y = _s5(rw, uj, s5_w)
    wr, br = _router_weights(moe_w_group[1], moe_b_group[1], moe_w_expert[1], moe_b_expert[1])
    xlat, h2, route, counts = _ssm_outproj(rw, ssd_y, act, z, s5_y, u, ssd_d[0], ssd_norm[0], s5_d[0], s5_glu_w[0],
                                           s5_glu_b[0], xall, ssm_w_out[0], m1, norm_ffn[1], wr, br)
    y1, y2 = _moe(h2, route, counts, moe_w13, moe_w2, 1)
    out = _combine(rw, rw.nlat, xlat, y1, y2, route, m1)
    return out.reshape(B, T, D)
```

```python
import functools
import math

import jax
import jax.numpy as jnp
import numpy as np
from jax import lax
from jax.experimental import pallas as pl
from jax.experimental.pallas import tpu as pltpu

F32 = jnp.float32
BF16 = jnp.bfloat16

EPS = 1e-6
NEG_INF = -1e30
GRID_W = 64
HEAD_DIM = 64
NA_KH = 8
NA_KW = 16
WA_BLOCK = 128
ROPE_BASE = 10000.0
SSD_CHUNK = 128
S5_GROUP = 16
S5_STATE = 64
MOE_GROUPS = 4
MOE_EPG = 8
MOE_EXPERTS = MOE_GROUPS * MOE_EPG

LANES = 128
ROW_TILE = 512
MXU_W = 256
MOE_TILE = MXU_W
VMEM_LIMIT = 56 * 1024 * 1024
MOE_VMEM_LIMIT = 60 * 1024 * 1024


def _cparams(n_axes, vmem=VMEM_LIMIT):
    return pltpu.CompilerParams(dimension_semantics=("arbitrary",) * n_axes, vmem_limit_bytes=vmem)


def _sigmoid(x):
    return 1.0 / (1.0 + jnp.exp(-x))


def _silu(x):
    return x * _sigmoid(x)


def _rms(x, eps=EPS):
    return x * lax.rsqrt(jnp.mean(x * x, axis=-1, keepdims=True) + eps)


def _ada_norm(x, g, shift, scale):
    return (_rms(x) * g) * (1.0 + scale) + shift


def _mod_kernel(c_ref, w_ref, b_ref, o_ref):
    a = _silu(c_ref[...])
    o_ref[...] = jnp.dot(a, w_ref[...], preferred_element_type=F32, precision=lax.Precision.HIGHEST) + b_ref[...]


def _modulation(cm, mod_w, mod_b):
    depth, d, n6 = mod_w.shape
    tn = 1024
    return pl.pallas_call(
        _mod_kernel,
        grid=(depth, n6 // tn),
        in_specs=[pl.BlockSpec((8, d), lambda l, j: (0, 0)),
                  pl.BlockSpec((None, d, tn), lambda l, j: (l, 0, j)),
                  pl.BlockSpec((None, 1, tn), lambda l, j: (l, 0, j))],
        out_specs=pl.BlockSpec((None, 8, tn), lambda l, j: (l, 0, j)),
        out_shape=jax.ShapeDtypeStruct((depth, 8, n6), F32),
        compiler_params=_cparams(2),
        name="modulation",
    )(cm, mod_w, mod_b.reshape(depth, 1, n6))


class _Rows:
    def __init__(self, B, T, C, D, tm):
        assert T % tm == 0 and (B * C) % tm == 0
        self.B, self.T, self.C, self.D, self.tm = B, T, C, D, tm
        self.tpb = T // tm
        self.nlat = B * self.tpb
        self.nctx = (B * C) // tm
        self.ntot = self.nlat + self.nctx
        self.rows = B * (T + C)

    def group(self, i):
        return jnp.where(i < self.nlat, i // self.tpb, self.B)


def _mod_spec(rw, col):
    return pl.BlockSpec((None, 1, rw.D), lambda i, *_: (rw.group(i), 0, col))


def _seg_norm(y, seg, gcol):
    ss = jnp.dot((y * y).astype(BF16), seg, preferred_element_type=F32)
    return y * lax.rsqrt(ss + EPS) * gcol


def _rope(y, cos, sin):
    w = y.shape[-1]
    lane = lax.broadcasted_iota(jnp.int32, y.shape, 1)
    first = (lane % 32) < 16
    partner = jnp.where(first, pltpu.roll(y, w - 16, 1), pltpu.roll(y, 16, 1))
    return y * cos + partner * sin


def _dup_halves(k):
    lane = lax.broadcasted_iota(jnp.int32, k.shape, 1)
    sw = pltpu.roll(k, 64, 1)
    return jnp.where(lane < 64, k, sw), jnp.where(lane < 64, sw, k)


def _att_inproj_kernel(nlat, xl_ref, xc_ref, g_ref, sh_ref, sc_ref, w_ref, gcol_ref, cos_ref, sin_ref, seg_ref,
                       o_ref, h_scr):
    i = pl.program_id(0)
    x = jnp.where(i < nlat, xl_ref[...], xc_ref[...])
    h_scr[...] = _ada_norm(x, g_ref[...], sh_ref[...], sc_ref[...]).astype(BF16)
    seg = seg_ref[...]
    cos2 = jnp.concatenate([cos_ref[...], cos_ref[...]], axis=1)
    sin2 = jnp.concatenate([sin_ref[...], sin_ref[...]], axis=1)
    for c in range(w_ref.shape[1] // MXU_W):
        c0 = c * MXU_W
        cols = slice(c0, c0 + MXU_W)
        y = jnp.dot(h_scr[...], w_ref[:, cols], preferred_element_type=F32)
        gcol = gcol_ref[:, cols]
        if c in (0, 1, 2, 3):
            o_ref[:, cols] = _seg_norm(y, seg, gcol).astype(BF16)
        elif c in (4, 5):
            o_ref[:, cols] = y.astype(BF16)
        elif c in (6, 7):
            o_ref[:, cols] = _rope(_seg_norm(y, seg, gcol), cos2, sin2).astype(BF16)
        else:
            lane = lax.broadcasted_iota(jnp.int32, y.shape, 1)
            yk = jnp.where(lane < LANES, _seg_norm(y, seg, gcol), y)
            yr = jnp.where(lane < LANES, _rope(yk, cos2, sin2), yk)
            k0, k1 = _dup_halves(yr[:, :LANES])
            v0, v1 = _dup_halves(yr[:, LANES:])
            for t, dup in enumerate((k0, k1, v0, v1)):
                o_ref[:, c0 + t * LANES:c0 + (t + 1) * LANES] = dup.astype(BF16)


def _rope_tables(T, tm):
    t = np.arange(T)
    d = np.arange(HEAD_DIM)
    nf = HEAD_DIM // 4
    inv = jnp.asarray(ROPE_BASE, F32) ** (-jnp.arange(nf, dtype=F32) / nf)
    pos = np.where((d // 32 == 0)[None, :], (t // GRID_W)[:, None], (t % GRID_W)[:, None])
    ang = jnp.asarray(pos, F32) * inv[d % nf][None, :]
    sign = np.where((d % 32) < 16, -1.0, 1.0).astype(np.float32)
    cos = jnp.cos(ang)
    sin = jnp.sin(ang) * sign[None, :]
    cos = jnp.concatenate([cos, jnp.ones((tm, HEAD_DIM), F32)], axis=0)
    sin = jnp.concatenate([sin, jnp.zeros((tm, HEAD_DIM), F32)], axis=0)
    return jnp.tile(cos, (1, 2)), jnp.tile(sin, (1, 2))


def _att_inproj(rw, xl, xc, mods, norm_g, w_in, na_qn, na_kn, wa_qn, wa_kn):
    D, tm = rw.D, rw.tm
    scale = HEAD_DIM ** -0.5
    gcol = jnp.concatenate([jnp.tile(na_qn * scale, 8), jnp.tile(na_kn, 8), jnp.ones((512,), F32),
                            jnp.tile(wa_qn * scale, 8), jnp.tile(wa_kn, 2), jnp.ones((128,), F32)])[None, :]
    cos, sin = _rope_tables(rw.T, tm)
    segn = np.arange(256) // 64
    seg = jnp.asarray((segn[:, None] == segn[None, :]).astype(np.float32) / 64.0, BF16)
    nlat, tpb = rw.nlat, rw.tpb
    return pl.pallas_call(
        functools.partial(_att_inproj_kernel, nlat),
        grid=(rw.ntot,),
        in_specs=[pl.BlockSpec((tm, D), lambda i: (jnp.minimum(i, nlat - 1), 0)),
                  pl.BlockSpec((tm, D), lambda i: (jnp.maximum(i - nlat, 0), 0)),
                  pl.BlockSpec((1, D), lambda i: (0, 0)),
                  _mod_spec(rw, 0), _mod_spec(rw, 1),
                  pl.BlockSpec((D, 2304), lambda i: (0, 0)),
                  pl.BlockSpec((1, 2304), lambda i: (0, 0)),
                  pl.BlockSpec((tm, 128), lambda i: (jnp.where(i < nlat, i % tpb, tpb), 0)),
                  pl.BlockSpec((tm, 128), lambda i: (jnp.where(i < nlat, i % tpb, tpb), 0)),
                  pl.BlockSpec((256, 256), lambda i: (0, 0))],
        out_specs=pl.BlockSpec((tm, 2560), lambda i: (i, 0)),
        out_shape=jax.ShapeDtypeStruct((rw.rows, 2560), BF16),
        scratch_shapes=[pltpu.VMEM((tm, D), BF16)],
        compiler_params=_cparams(1),
        name="att_inproj",
    )(xl, xc, norm_g[None, :], mods, mods, w_in.astype(BF16), gcol, cos, sin, seg)


def _route(lg, lt, carry):
    lane = lax.broadcasted_iota(jnp.int32, lg.shape, 1).astype(F32)
    gm = lane < MOE_GROUPS
    mg = jnp.max(jnp.where(gm, lg, NEG_INF), axis=-1, keepdims=True)
    eg = jnp.where(gm, jnp.exp(jnp.where(gm, lg, NEG_INF) - mg), 0.0)
    pg = eg / jnp.sum(eg, axis=-1, keepdims=True)
    ptop = jnp.max(pg, axis=-1, keepdims=True)
    gidx = jnp.min(jnp.where(gm & (pg == ptop), lane, 1e9), axis=-1, keepdims=True)
    lo = MOE_GROUPS + MOE_EPG * gidx
    em = (lane >= lo) & (lane < lo + MOE_EPG)
    le = jnp.where(em, lg, NEG_INF)
    ee = jnp.where(em, jnp.exp(le - jnp.max(le, axis=-1, keepdims=True)), 0.0)
    pe = ee / jnp.sum(ee, axis=-1, keepdims=True)
    v1 = jnp.max(jnp.where(em, pe, -1.0), axis=-1, keepdims=True)
    i1 = jnp.min(jnp.where(em & (pe == v1), lane, 1e9), axis=-1, keepdims=True)
    em2 = em & (lane != i1)
    v2 = jnp.max(jnp.where(em2, pe, -1.0), axis=-1, keepdims=True)
    i2 = jnp.min(jnp.where(em2 & (pe == v2), lane, 1e9), axis=-1, keepdims=True)
    den = v1 + v2
    w1 = v1 / den * ptop
    w2 = v2 / den * ptop
    e1 = i1 - MOE_GROUPS
    e2 = i2 - MOE_GROUPS
    m1 = lane == e1
    m2 = lane == e2
    oh = jnp.where(m1 | m2, 1.0, 0.0)
    cnt = jnp.dot(lt, oh.astype(BF16), preferred_element_type=F32) + carry
    r1 = jnp.sum(jnp.where(m1, cnt, 0.0), axis=-1, keepdims=True)
    r2 = jnp.sum(jnp.where(m2, cnt, 0.0), axis=-1, keepdims=True)
    route = jnp.where(lane == 0, e1, jnp.where(lane == 1, e2, jnp.where(lane == 2, w1, jnp.where(
        lane == 3, w2, jnp.where(lane == 4, r1, jnp.where(lane == 5, r2, 0.0))))))
    return route, carry + jnp.sum(oh, axis=0, keepdims=True)


def _post_mixer(i, x, y, g1, gn, sh2, sc2, wr_ref, br_ref, lt_ref, xo_ref, h2_ref, rt_ref, cnt_ref, carry):
    xn = x + g1 * y
    xo_ref[...] = xn
    h2 = _ada_norm(xn, gn, sh2, sc2)
    hb = h2.astype(BF16)
    hbf = hb.astype(F32)
    half = h2.shape[1] // 2
    h2_ref[...] = pltpu.pack_elementwise([h2[:, :half], h2[:, half:]], packed_dtype=BF16)
    hl = (h2 - hbf).astype(BF16)
    lg = (jnp.dot(hb, wr_ref[0], preferred_element_type=F32)
          + (jnp.dot(hb, wr_ref[1], preferred_element_type=F32) + jnp.dot(hl, wr_ref[0], preferred_element_type=F32))
          + br_ref[...])

    @pl.when(i == 0)
    def _():
        carry[...] = jnp.zeros_like(carry)

    route, newc = _route(lg, lt_ref[...], carry[...])
    rt_ref[...] = route
    carry[...] = newc
    cnt_ref[...] = newc


def _att_outproj_kernel(nlat, na_ref, wa_ref, cx_ref, xl_ref, xc_ref, w_ref, g1_ref, gn_ref, sh2_ref, sc2_ref,
                        wr_ref, br_ref, lt_ref, xo_ref, h2_ref, rt_ref, cnt_ref, carry):
    i = pl.program_id(0)
    lat = i < nlat
    mix = jnp.where(lat, jnp.concatenate([na_ref[...], wa_ref[...]], axis=1), cx_ref[...])
    y = jnp.dot(mix, w_ref[...], preferred_element_type=F32)
    x = jnp.where(lat, xl_ref[...], xc_ref[...])
    _post_mixer(i, x, y, g1_ref[...], gn_ref[...], sh2_ref[...], sc2_ref[...], wr_ref, br_ref, lt_ref,
                xo_ref, h2_ref, rt_ref, cnt_ref, carry)


def _router_weights(w_group, b_group, w_expert, b_expert):
    D = w_group.shape[0]
    pad = LANES - MOE_GROUPS - MOE_EXPERTS
    wr = jnp.concatenate([w_group, w_expert, jnp.zeros((D, pad), F32)], axis=1)
    br = jnp.concatenate([b_group, b_expert, jnp.zeros((pad,), F32)])[None, :]
    hi = wr.astype(BF16)
    lo = (wr - hi.astype(F32)).astype(BF16)
    return jnp.stack([hi, lo]), br


def _lower_tri(tm):
    r = np.arange(tm)
    return jnp.asarray((r[None, :] < r[:, None]).astype(np.float32), BF16)


def _post_specs(rw):
    D, tm = rw.D, rw.tm
    return ([pl.BlockSpec((1, D), lambda i: (0, 0)), _mod_spec(rw, 3), _mod_spec(rw, 4),
             pl.BlockSpec((2, D, LANES), lambda i: (0, 0, 0)), pl.BlockSpec((1, LANES), lambda i: (0, 0)),
             pl.BlockSpec((tm, tm), lambda i: (0, 0))],
            [pl.BlockSpec((tm, D), lambda i: (i, 0)), pl.BlockSpec((tm, D // 2), lambda i: (i, 0)),
             pl.BlockSpec((tm, LANES), lambda i: (i, 0)), pl.BlockSpec((1, LANES), lambda i: (0, 0))])


def _post_shapes(nrows, D):
    return [jax.ShapeDtypeStruct((nrows, D), F32), jax.ShapeDtypeStruct((nrows, D // 2), jnp.uint32),
            jax.ShapeDtypeStruct((nrows, LANES), F32), jax.ShapeDtypeStruct((1, LANES), F32)]


def _att_outproj(rw, na, wa, cx, xl, xc, w_out, mods, norm_ffn, wr, br):
    D, tm, nlat = rw.D, rw.tm, rw.nlat
    post_in, post_out = _post_specs(rw)
    latmap = lambda i: (jnp.minimum(i, nlat - 1), 0)
    ctxmap = lambda i: (jnp.maximum(i - nlat, 0), 0)
    return pl.pallas_call(
        functools.partial(_att_outproj_kernel, nlat),
        grid=(rw.ntot,),
        in_specs=[pl.BlockSpec((tm, 512), latmap), pl.BlockSpec((tm, 512), latmap), pl.BlockSpec((tm, D), ctxmap),
                  pl.BlockSpec((tm, D), latmap), pl.BlockSpec((tm, D), ctxmap),
                  pl.BlockSpec((D, D), lambda i: (0, 0)), _mod_spec(rw, 2)] + post_in,
        out_specs=post_out,
        out_shape=_post_shapes(rw.rows, D),
        scratch_shapes=[pltpu.VMEM((1, LANES), F32)],
        compiler_params=_cparams(1),
        name="att_outproj_router",
    )(na, wa, cx, xl, xc, w_out.astype(BF16), mods, norm_ffn[None, :], mods, mods, wr, br, _lower_tri(tm))


def _moe_kernel(te_ref, nu_ref, src_ref, nsrc_ref, hp_ref, w13_ref, w2_ref, o_ref, w13b, w2b, xa, xb):
    i = pl.program_id(0)
    prev = te_ref[jnp.maximum(i - 1, 0)]
    changed = (i == 0) | (te_ref[i] != prev)
    tg = xa.shape[0]

    @pl.when(changed)
    def _():
        w13b[...] = w13_ref[...].astype(BF16)
        w2b[...] = w2_ref[...].astype(BF16)

    @pl.when(i == 0)
    def _():
        def fetch(j, carry):
            xa[pl.ds(j, 1), :] = hp_ref[pl.ds(src_ref[0, j], 1), :]
            return carry

        lax.fori_loop(0, tg, fetch, 0, unroll=8)

    def step(cur, nxt):
        for j in range(tg):
            nxt[pl.ds(j, 1), :] = hp_ref[pl.ds(nsrc_ref[0, j], 1), :]
        ff = w2b.shape[0]
        half = cur.shape[1]
        w = cur[...]
        unpack = functools.partial(pltpu.unpack_elementwise, packed_dtype=BF16, unpacked_dtype=F32)
        x_lo = unpack(w, index=0).astype(BF16)
        x_hi = unpack(w, index=1).astype(BF16)
        a13 = (jnp.dot(x_lo, w13b[:half, :], preferred_element_type=F32)
               + jnp.dot(x_hi, w13b[half:, :], preferred_element_type=F32))
        act = _silu(a13[:, :ff]) * a13[:, ff:]
        o_ref[...] = jnp.dot(act.astype(BF16), w2b[...], preferred_element_type=F32).astype(BF16)

    used = i < nu_ref[0]

    @pl.when(used & (i % 2 == 0))
    def _():
        step(xa, xb)

    @pl.when(used & (i % 2 == 1))
    def _():
        step(xb, xa)

    @pl.when(i >= nu_ref[0])
    def _():
        o_ref[...] = jnp.zeros_like(o_ref)


def _moe(h2p, route, counts, w13, w2, layer):
    N = h2p.shape[0]
    D = 2 * h2p.shape[1]
    _, E, _, F2 = w13.shape
    tg = MOE_TILE
    nt = (2 * N) // tg + E
    e = route[:, 0:2].astype(jnp.int32)
    rank = route[:, 4:6].astype(jnp.int32)
    cnt = counts[0, :E].astype(jnp.int32)
    ntile_e = (cnt + tg - 1) // tg
    tile_end = jnp.cumsum(ntile_e)
    offs = (tile_end - ntile_e) * tg
    onehot = (e[:, :, None] == jnp.arange(E, dtype=jnp.int32)).astype(jnp.int32)
    dest = jnp.sum(onehot * offs, axis=-1) + rank
    src = jnp.zeros((nt * tg,), jnp.int32).at[dest.reshape(-1)].set(jnp.repeat(jnp.arange(N, dtype=jnp.int32), 2))
    tile_id = jnp.arange(nt, dtype=jnp.int32)
    nu = tile_end[-1:].astype(jnp.int32)
    te = jnp.sum((tile_end[None, :] <= jnp.minimum(tile_id, nu[0] - 1)[:, None]).astype(jnp.int32), axis=1)
    te = jnp.minimum(te, E - 1)
    ys = pl.pallas_call(
        _moe_kernel,
        grid_spec=pltpu.PrefetchScalarGridSpec(
            num_scalar_prefetch=2,
            grid=(nt,),
            in_specs=[pl.BlockSpec((None, 1, tg), lambda i, te, nu: (i, 0, 0), memory_space=pltpu.SMEM),
                      pl.BlockSpec((None, 1, tg), lambda i, te, nu: (jnp.minimum(i + 1, nt - 1), 0, 0),
                                   memory_space=pltpu.SMEM),
                      pl.BlockSpec((N, D // 2), lambda i, te, nu: (0, 0), pipeline_mode=pl.Buffered(1)),
                      pl.BlockSpec((None, None, D, F2), lambda i, te, nu: (layer, te[i], 0, 0)),
                      pl.BlockSpec((None, None, F2 // 2, D), lambda i, te, nu: (layer, te[i], 0, 0))],
            out_specs=pl.BlockSpec((tg, D), lambda i, te, nu: (i, 0)),
            scratch_shapes=[pltpu.VMEM((D, F2), BF16), pltpu.VMEM((F2 // 2, D), BF16),
                            pltpu.VMEM((tg, D // 2), jnp.uint32), pltpu.VMEM((tg, D // 2), jnp.uint32)]),
        out_shape=jax.ShapeDtypeStruct((nt * tg, D), BF16),
        compiler_params=_cparams(1, vmem=MOE_VMEM_LIMIT),
        name="moe_experts",
    )(te, nu, src.reshape(nt, 1, tg), src.reshape(nt, 1, tg), h2p, w13, w2)
    y12 = ys.at[dest.reshape(-1)].get(mode="promise_in_bounds").reshape(N, 2 * D)
    return y12, y12


def _moe_residual(x_ref, y1_ref, y2_ref, rt_ref, g2_ref):
    rt = rt_ref[...]
    f = rt[:, 2:3] * y1_ref[...].astype(F32) + rt[:, 3:4] * y2_ref[...].astype(F32)
    return x_ref[...] + g2_ref[...] * f


def _combine_kernel(x_ref, y1_ref, y2_ref, rt_ref, g2_ref, o_ref):
    o_ref[...] = _moe_residual(x_ref, y1_ref, y2_ref, rt_ref, g2_ref)


def _combine(rw, ntiles, xall, y1, y2, route, mods):
    D, tm = rw.D, rw.tm
    row = lambda i: (i, 0)
    return pl.pallas_call(
        _combine_kernel,
        grid=(ntiles,),
        in_specs=[pl.BlockSpec((tm, D), row), pl.BlockSpec((tm, D), row), pl.BlockSpec((tm, D), lambda i: (i, 1)),
                  pl.BlockSpec((tm, LANES), row), _mod_spec(rw, 5)],
        out_specs=pl.BlockSpec((tm, D), row),
        out_shape=jax.ShapeDtypeStruct((ntiles * tm, D), F32),
        compiler_params=_cparams(1),
        name="moe_combine",
    )(xall, y1, y2, route, mods)


NA_QROWS = 8
NA_KROWS = 16


def _na_first_key_row(variant, a):
    return (max(a - 4, 0) + 4, a, min(a, 4))[variant]


def _na_key_lanes(row0):
    a = row0 // GRID_W
    starts = [_na_first_key_row(v, a) for v in range(3)]
    lo = (min(starts) * GRID_W) // LANES * LANES
    hi = -(-((max(starts) + NA_KH) * GRID_W) // LANES) * LANES
    return lo, hi


def _na_bias_tiles(rpb):
    H = rpb.shape[0]
    i = np.arange(GRID_W)
    c0 = np.clip(i - NA_KW // 2, 0, GRID_W - NA_KW)
    j = np.arange(GRID_W)
    colvalid = (j[None, :] >= c0[:, None]) & (j[None, :] < c0[:, None] + NA_KW)
    dc = np.clip(j[None, :] - i[:, None] + NA_KW - 1, 0, 2 * NA_KW - 2)
    onehot = ((dc[None] == np.arange(2 * NA_KW - 1)[:, None, None]) & colvalid[None]).astype(np.float32)
    tiles = jnp.einsum('hrc,cij->hrij', rpb.astype(F32), jnp.asarray(onehot), precision=lax.Precision.HIGHEST)
    tiles = tiles + jnp.asarray(np.where(colvalid, 0.0, NEG_INF).astype(np.float32))
    return tiles.transpose(0, 2, 1, 3).reshape(H, GRID_W, (2 * NA_KH - 1) * GRID_W)


def _na_fill_bias(variant, tiles_ref, bias_scr):
    for hh in range(2):
        for a in range(NA_QROWS):
            start = _na_first_key_row(variant, a)
            dr0 = start - a + 3
            rows = slice(a * GRID_W, (a + 1) * GRID_W)
            w0, w1 = start * GRID_W, (start + NA_KH) * GRID_W
            if w0 > 0:
                bias_scr[hh, rows, 0:w0] = jnp.full((GRID_W, w0), NEG_INF, F32)
            bias_scr[hh, rows, w0:w1] = tiles_ref[hh, :, dr0 * GRID_W:(dr0 + NA_KH) * GRID_W]
            if w1 < NA_KROWS * GRID_W:
                bias_scr[hh, rows, w1:] = jnp.full((GRID_W, NA_KROWS * GRID_W - w1), NEG_INF, F32)


def _softmax_pv(parts, extra=None, rc=64):
    m_rows = parts[0][0].shape[0]
    probs = [[] for _ in parts]
    inv_l = []
    for r0 in range(0, m_rows, rc):
        sc = []
        for s, _, bias_fn, lanes_fn in parts:
            l0, l1 = (0, s.shape[1]) if lanes_fn is None else lanes_fn(r0)
            c = s[r0:r0 + rc, l0:l1]
            if bias_fn is not None:
                c = c + bias_fn(r0, rc, slice(l0, l1))
            sc.append((c, l0, s.shape[1] - l1))
        mx = functools.reduce(jnp.maximum, [jnp.max(c, axis=-1, keepdims=True) for c, _, _ in sc])
        if extra is not None:
            mx = jnp.maximum(mx, extra[r0:r0 + rc])
        l = jnp.zeros_like(mx) if extra is None else jnp.exp(extra[r0:r0 + rc] - mx)
        for k, (c, before, after) in enumerate(sc):
            p = jnp.exp(c - mx)
            l = l + jnp.sum(p, axis=-1, keepdims=True)
            row = [jnp.zeros((rc, before), BF16)] * (before > 0) + [p.astype(BF16)] \
                + [jnp.zeros((rc, after), BF16)] * (after > 0)
            probs[k].append(row[0] if len(row) == 1 else jnp.concatenate(row, axis=1))
        inv_l.append(1.0 / l)
    o = None
    for k, (_, v, _, _) in enumerate(parts):
        pv = jnp.dot(jnp.concatenate(probs[k], axis=0), v, preferred_element_type=F32)
        o = pv if o is None else o + pv
    return o * jnp.concatenate(inv_l, axis=0)


def _nt(a, b):
    return lax.dot_general(a, b, (((1,), (1,)), ((), ())), preferred_element_type=F32)


NA_REFS_PER_BATCH = 11


def _na_kernel(n_rb, nb, *refs):
    tiles_ref, o_ref, bias_ref = refs[nb * NA_REFS_PER_BATCH:]
    rb = pl.program_id(1)
    for variant, at_rb in ((0, 0), (1, 1), (2, n_rb - 1)):
        @pl.when(rb == at_rb)
        def _(variant=variant):
            _na_fill_bias(variant, tiles_ref, bias_ref)

    for b in range(nb):
        q_ref, k0, k1, k2, k3, v0, v1, v2, v3, kc_ref, vc_ref = refs[b * NA_REFS_PER_BATCH:(b + 1) * NA_REFS_PER_BATCH]
        q2 = q_ref[...]
        kw = jnp.concatenate([k0[...], k1[...], k2[...], k3[...]], axis=0)
        vw = jnp.concatenate([v0[...], v1[...], v2[...], v3[...]], axis=0)
        kc = kc_ref[...]
        vc = vc_ref[...]
        lane = lax.broadcasted_iota(jnp.int32, q2.shape, 1)
        out = jnp.zeros(q2.shape, F32)
        for hh in range(2):
            m = (lane < HEAD_DIM) if hh == 0 else (lane >= HEAD_DIM)
            qm = jnp.where(m, q2, jnp.zeros_like(q2))
            o = _softmax_pv([(_nt(qm, kw), vw, lambda r0, rc, lanes, hh=hh: bias_ref[hh, r0:r0 + rc, lanes],
                              _na_key_lanes), (_nt(qm, kc), vc, None, None)], rc=32)
            out = jnp.where(m, o, out)
        o_ref[b] = out.astype(BF16)


def _na_attention(rw, qkv, rpb):
    B, T, C = rw.B, rw.T, rw.C
    tq = NA_QROWS * GRID_W
    tk = tq // 2
    n_rb = T // tq
    nkb = T // tk
    assert T % tq == 0 and n_rb >= 2 and (B * T) % C == 0
    tiles = _na_bias_tiles(rpb)
    ctxrow = (B * T) // C

    def batch_specs(b):
        kv = lambda j, col: pl.BlockSpec(
            (tk, LANES), lambda p, rb: (b * nkb + jnp.clip(2 * rb - 1 + j, 0, nkb - 1), col + p))
        return ([pl.BlockSpec((tq, LANES), lambda p, rb: (b * n_rb + rb, p))]
                + [kv(j, 4) for j in range(4)] + [kv(j, 8) for j in range(4)]
                + [pl.BlockSpec((C, LANES), lambda p, rb: (ctxrow + b, 4 + p)),
                   pl.BlockSpec((C, LANES), lambda p, rb: (ctxrow + b, 8 + p))])

    out = pl.pallas_call(
        functools.partial(_na_kernel, n_rb, B),
        grid=(4, n_rb),
        in_specs=sum([batch_specs(b) for b in range(B)], [])
        + [pl.BlockSpec((2,) + tiles.shape[1:], lambda p, rb: (p, 0, 0))],
        out_specs=pl.BlockSpec((B, tq, LANES), lambda p, rb: (0, rb, p)),
        out_shape=jax.ShapeDtypeStruct((B, T, 4 * LANES), BF16),
        scratch_shapes=[pltpu.VMEM((2, tq, 2 * tq), F32)],
        compiler_params=_cparams(2),
        name="neighbourhood_attention",
    )(*([qkv] * (B * NA_REFS_PER_BATCH)), tiles)
    return out.reshape(B * T, 4 * LANES)


WA_QBLOCKS = 2


def _wa_kernel(nb, sink_ref, q_ref, *refs):
    step = pl.program_id(1)
    blk = WA_BLOCK
    nkb = WA_QBLOCKS + 2
    kblocks, vblocks = refs[:nkb], refs[nkb:2 * nkb]
    kx_ref, vx_ref, o_ref = refs[2 * nkb:]
    lane = lax.broadcasted_iota(jnp.int32, (blk, LANES), 1)
    zero = jnp.zeros((blk, LANES), BF16)
    qi = lax.broadcasted_iota(jnp.int32, (blk, 3 * blk), 0)
    ks = lax.broadcasted_iota(jnp.int32, (blk, 3 * blk), 1)
    for qb in range(WA_QBLOCKS):
        n = step * WA_QBLOCKS + qb
        lo = jnp.where(n > 0, 0, blk)
        hi = jnp.where(n < nb - 1, 3 * blk, 2 * blk)
        valid = (ks >= qi) & (ks <= qi + 2 * blk) & (ks >= lo) & (ks < hi)
        band_mask = jnp.where(valid, 0.0, NEG_INF)
        rows = slice(qb * blk, (qb + 1) * blk)
        for kv in range(2):
            parts = []
            for pr in range(2):
                c0 = kv * 2 * LANES + pr * LANES
                qp = q_ref[rows, c0:c0 + LANES]
                parts += [jnp.where(lane < HEAD_DIM, qp, zero), jnp.where(lane >= HEAD_DIM, qp, zero)]
            qs = jnp.concatenate(parts, axis=0)
            cs = slice(kv * LANES, (kv + 1) * LANES)
            kb = jnp.concatenate([r[:, cs] for r in kblocks[qb:qb + 3]], axis=0)
            vb = jnp.concatenate([r[:, cs] for r in vblocks[qb:qb + 3]], axis=0)
            sink = jnp.concatenate([jnp.full((blk, 1), sink_ref[kv * 4 + g], F32) for g in range(4)], axis=0)
            o = _softmax_pv([(_nt(qs, kb), vb, lambda r0, rc, lanes, m=band_mask: m[r0 % blk:r0 % blk + rc, lanes], None),
                             (_nt(qs, kx_ref[:, cs]), vx_ref[:, cs], None, None)], extra=sink, rc=64)
            c0 = kv * 2 * LANES
            o_ref[rows, c0:c0 + LANES] = jnp.where(lane < HEAD_DIM, o[0:blk], o[blk:2 * blk]).astype(BF16)
            o_ref[rows, c0 + LANES:c0 + 2 * LANES] = jnp.where(
                lane < HEAD_DIM, o[2 * blk:3 * blk], o[3 * blk:4 * blk]).astype(BF16)


def _wa_attention(rw, qkv, sink):
    B, T, C = rw.B, rw.T, rw.C
    blk = WA_BLOCK
    nb = T // blk
    nq = WA_QBLOCKS
    assert nb % nq == 0
    ctxrow = (B * T) // C

    def kvspec(j, col):
        return pl.BlockSpec((blk, 2 * LANES), lambda b, s: (b * nb + jnp.clip(nq * s - 1 + j, 0, nb - 1), col))

    return pl.pallas_call(
        functools.partial(_wa_kernel, nb),
        grid=(B, nb // nq),
        in_specs=[pl.BlockSpec(memory_space=pltpu.SMEM),
                  pl.BlockSpec((nq * blk, 4 * LANES), lambda b, s: (b * (nb // nq) + s, 3))]
        + [kvspec(j, 8) for j in range(nq + 2)] + [kvspec(j, 9) for j in range(nq + 2)]
        + [pl.BlockSpec((C, 2 * LANES), lambda b, s: (ctxrow + b, 8)),
           pl.BlockSpec((C, 2 * LANES), lambda b, s: (ctxrow + b, 9))],
        out_specs=pl.BlockSpec((nq * blk, 4 * LANES), lambda b, s: (b * (nb // nq) + s, 0)),
        out_shape=jax.ShapeDtypeStruct((B * T, 4 * LANES), BF16),
        compiler_params=_cparams(2),
        name="window_attention",
    )(sink.astype(F32), qkv, *([qkv] * (2 * nq + 6)))


def _ctx_attn_kernel(sink_ref, t_ref, o_ref):
    C = t_ref.shape[0]
    lane = lax.broadcasted_iota(jnp.int32, (C, LANES), 1)
    zero = jnp.zeros((C, LANES), BF16)

    def pair(q2, k2, v2, sinks):
        out = jnp.zeros((C, LANES), F32)
        for hh in range(2):
            m = (lane < HEAD_DIM) if hh == 0 else (lane >= HEAD_DIM)
            extra = None if sinks is None else jnp.full((C, 1), sinks[hh], F32)
            o = _softmax_pv([(_nt(jnp.where(m, q2, zero), k2), v2, None, None)], extra=extra, rc=64)
            out = jnp.where(m, o, out)
        return out.astype(BF16)

    for p in range(4):
        c = p * LANES
        o_ref[:, c:c + LANES] = pair(t_ref[:, c:c + LANES], t_ref[:, 512 + c:640 + c], t_ref[:, 1024 + c:1152 + c], None)
    for kv in range(2):
        kd = t_ref[:, 2048 + kv * LANES:2176 + kv * LANES]
        vd = t_ref[:, 2304 + kv * LANES:2432 + kv * LANES]
        for pr in range(2):
            c = kv * 256 + pr * LANES
            h0 = kv * 4 + pr * 2
            o_ref[:, 512 + c:640 + c] = pair(t_ref[:, 1536 + c:1664 + c], kd, vd, (sink_ref[h0], sink_ref[h0 + 1]))


def _ctx_attention(rw, qkv, sink):
    B, T, C = rw.B, rw.T, rw.C
    ctxrow = (B * T) // C
    return pl.pallas_call(
        _ctx_attn_kernel,
        grid=(B,),
        in_specs=[pl.BlockSpec(memory_space=pltpu.SMEM),
                  pl.BlockSpec((C, qkv.shape[1]), lambda b: (ctxrow + b, 0))],
        out_specs=pl.BlockSpec((C, 8 * LANES), lambda b: (b, 0)),
        out_shape=jax.ShapeDtypeStruct((B * C, 8 * LANES), BF16),
        compiler_params=_cparams(1),
        name="context_attention",
    )(sink.astype(F32), qkv)


S5_Q = 16
CONV_TILE = 256
CONV_HALO = 16


def _ssm_inproj_kernel(x_ref, y1_ref, y2_ref, rt_ref, g2_ref, g_ref, sh_ref, sc_ref, w_ref,
                       xo_ref, z_ref, xbc_ref, u_ref, uj_ref, dt_ref, h_scr, u_scr):
    xn = _moe_residual(x_ref, y1_ref, y2_ref, rt_ref, g2_ref)
    xo_ref[...] = xn
    h_scr[...] = _ada_norm(xn, g_ref[...], sh_ref[...], sc_ref[...]).astype(BF16)

    def mm(c0, n):
        return jnp.dot(h_scr[...], w_ref[:, c0:c0 + n], preferred_element_type=F32)

    nz, nxbc, nu = z_ref.shape[1], xbc_ref.shape[1], u_ref.shape[1]
    for c0 in range(0, nz, MXU_W):
        z_ref[:, c0:c0 + MXU_W] = mm(c0, MXU_W).astype(BF16)
    for c0 in range(0, nxbc, MXU_W):
        xbc_ref[:, c0:c0 + MXU_W] = mm(nz + c0, MXU_W).astype(BF16)
    for c0 in range(0, nu, MXU_W):
        y = mm(nz + nxbc + c0, MXU_W)
        u_ref[:, c0:c0 + MXU_W] = y.astype(BF16)
        for t in range(MXU_W // LANES):
            u_scr[c0 // LANES + t] = y[:, t * LANES:(t + 1) * LANES]
    dt_ref[...] = mm(nz + nxbc + nu, LANES)
    nchunk = u_scr.shape[1] // S5_Q
    for j in range(S5_Q):
        for t in range(u_scr.shape[0]):
            uj_ref[j, :, t * LANES:(t + 1) * LANES] = u_scr[t, pl.ds(j, nchunk, stride=S5_Q), :].astype(BF16)


def _ssm_inproj(rw, xall, y1, y2, route, prev_mods, mods, norm_g, w_in):
    D, tm = rw.D, rw.tm
    w = jnp.concatenate([w_in[:, 0:2560], w_in[:, 2592:3104], w_in[:, 2560:2592], jnp.zeros((D, LANES - 32), F32)],
                        axis=1).astype(BF16)
    row = lambda i: (i, 0)
    return pl.pallas_call(
        _ssm_inproj_kernel,
        grid=(rw.ntot,),
        in_specs=[pl.BlockSpec((tm, D), row), pl.BlockSpec((tm, D), row), pl.BlockSpec((tm, D), lambda i: (i, 1)),
                  pl.BlockSpec((tm, LANES), row), _mod_spec(rw, 5), pl.BlockSpec((1, D), lambda i: (0, 0)),
                  _mod_spec(rw, 0), _mod_spec(rw, 1), pl.BlockSpec((D, 3200), lambda i: (0, 0))],
        out_specs=[pl.BlockSpec((tm, D), row),
                   pl.BlockSpec((tm, 1024), row), pl.BlockSpec((tm, 1536), row), pl.BlockSpec((tm, 512), row),
                   pl.BlockSpec((S5_Q, tm // S5_Q, 512), lambda i: (0, i, 0)), pl.BlockSpec((tm, LANES), row)],
        out_shape=[jax.ShapeDtypeStruct((rw.rows, D), F32),
                   jax.ShapeDtypeStruct((rw.rows, 1024), BF16), jax.ShapeDtypeStruct((rw.rows, 1536), BF16),
                   jax.ShapeDtypeStruct((rw.rows, 512), BF16),
                   jax.ShapeDtypeStruct((S5_Q, rw.rows // S5_Q, 512), BF16),
                   jax.ShapeDtypeStruct((rw.rows, LANES), F32)],
        scratch_shapes=[pltpu.VMEM((tm, D), BF16), pltpu.VMEM((512 // LANES, tm, LANES), F32)],
        compiler_params=_cparams(1),
        name="ssm_inproj",
    )(xall, y1, y2, route, prev_mods, norm_g[None, :], mods, mods, w)


def _softplus(x):
    return jnp.maximum(x, 0.0) + jnp.log(1.0 + jnp.exp(-jnp.abs(x)))


def _conv_kernel(lat_tiles, tpb, cpb, x_ref, pv_ref, nx_ref, w_ref, b_ref, dtr_ref, dtb_ref, act_ref, dt_ref):
    i = pl.program_id(0)
    is_lat = i < lat_tiles
    pos = jnp.where(is_lat, i % tpb, (i - lat_tiles) % cpb)
    last_pos = jnp.where(is_lat, tpb - 1, cpb - 1)
    x = x_ref[...].astype(F32)
    tc = x.shape[0]
    prev_row = jnp.where(pos == 0, 0.0, pv_ref[...].astype(F32)[CONV_HALO - 1:CONV_HALO, :])
    next_row = jnp.where(pos == last_pos, 0.0, nx_ref[...].astype(F32)[0:1, :])
    row = lax.broadcasted_iota(jnp.int32, x.shape, 0)
    xm1 = jnp.where(row == 0, prev_row, pltpu.roll(x, 1, 0))
    xp1 = jnp.where(row == tc - 1, next_row, pltpu.roll(x, tc - 1, 0))
    y = w_ref[0:1, :] * xm1 + w_ref[1:2, :] * x + w_ref[2:3, :] * xp1 + b_ref[...]
    act_ref[...] = _silu(y).astype(BF16)
    sp = _softplus(dtr_ref[...] + dtb_ref[...])
    dt_ref[0] = sp
    dt_ref[1] = pltpu.roll(sp, LANES - 16, 1)


def _ssm_conv(rw, xbc, dtr, conv_w, conv_b, dt_bias):
    B, T, C = rw.B, rw.T, rw.C
    tc = CONV_TILE
    assert T % tc == 0 and C % tc == 0
    lat_tiles, tpb, cpb = (B * T) // tc, T // tc, C // tc
    ntiles = rw.rows // tc
    hpt = tc // CONV_HALO
    nhalo = rw.rows // CONV_HALO
    W = xbc.shape[1]
    dtb = jnp.concatenate([dt_bias.reshape(-1), jnp.zeros((LANES - 32,), F32)])[None, :]
    row = lambda i: (i, 0)
    return pl.pallas_call(
        functools.partial(_conv_kernel, lat_tiles, tpb, cpb),
        grid=(ntiles,),
        in_specs=[pl.BlockSpec((tc, W), row),
                  pl.BlockSpec((CONV_HALO, W), lambda i: (jnp.maximum(i * hpt - 1, 0), 0)),
                  pl.BlockSpec((CONV_HALO, W), lambda i: (jnp.minimum((i + 1) * hpt, nhalo - 1), 0)),
                  pl.BlockSpec((3, W), lambda i: (0, 0)), pl.BlockSpec((1, W), lambda i: (0, 0)),
                  pl.BlockSpec((tc, LANES), row), pl.BlockSpec((1, LANES), lambda i: (0, 0))],
        out_specs=[pl.BlockSpec((tc, W), row), pl.BlockSpec((2, tc, LANES), lambda i: (0, i, 0))],
        out_shape=[jax.ShapeDtypeStruct((rw.rows, W), BF16), jax.ShapeDtypeStruct((2, rw.rows, LANES), F32)],
        compiler_params=_cparams(1),
        name="ssm_conv",
    )(xbc, xbc, xbc, conv_w, conv_b[None, :], dtr, dtb)


def _ssd_kernel(nb, *refs):
    acts, dts = refs[0:2 * nb], refs[2 * nb:4 * nb]
    tri_ref, a_ref, yf_ref, yb_ref, hst = refs[4 * nb:]

    @pl.when(pl.program_id(0) == 0)
    def _():
        hst[...] = jnp.zeros_like(hst)

    for d, y_ref in enumerate((yf_ref, yb_ref)):
        for b in range(nb):
            _ssd_chunk(acts[d * nb + b], dts[d * nb + b], tri_ref[d], a_ref[d], y_ref.at[b], hst.at[d, b])


def _ssd_chunk(act_ref, dt_ref, tri, avec, y_ref, hst):
    q = SSD_CHUNK
    dt = dt_ref[...]
    da = dt * avec
    acs = jnp.dot(tri, da, preferred_element_type=F32, precision=lax.Precision.HIGHEST)
    tot = jnp.sum(da, axis=0, keepdims=True)
    acs_t = acs.T
    dt_t = dt.T
    eacs = jnp.exp(acs)
    wend = jnp.exp(tot - acs) * dt
    etot = jnp.exp(tot)
    mask = tri > 0.5
    left = lax.broadcasted_iota(jnp.int32, (q, LANES), 1) < HEAD_DIM
    left1 = lax.broadcasted_iota(jnp.int32, (1, LANES), 1) < HEAD_DIM
    for g in range(2):
        bg = act_ref[:, 1024 + g * 128:1152 + g * 128]
        cg = act_ref[:, 1280 + g * 128:1408 + g * 128]
        cb = _nt(cg, bg)
        hin = hst[:, g * 512:(g + 1) * 512]
        yoff = jnp.dot(cg, hin.astype(BF16), preferred_element_type=F32)
        xw, dec = [], []
        for pr in range(4):
            h_a = g * 8 + pr * 2
            h_b = h_a + 1
            c0 = h_a * HEAD_DIM
            x2 = act_ref[:, c0:c0 + LANES]
            outs = []
            for h in (h_a, h_b):
                seg = acs[:, h:h + 1] - acs_t[h:h + 1, :]
                w = cb * jnp.exp(jnp.where(mask, seg, NEG_INF)) * dt_t[h:h + 1, :]
                outs.append(jnp.dot(w.astype(BF16), x2, preferred_element_type=F32))
            yd = jnp.where(left, outs[0], outs[1])
            sc = jnp.where(left, eacs[:, h_a:h_a + 1], eacs[:, h_b:h_b + 1])
            y_ref[:, c0:c0 + LANES] = (yd + yoff[:, pr * LANES:(pr + 1) * LANES] * sc).astype(BF16)
            wsc = jnp.where(left, wend[:, h_a:h_a + 1], wend[:, h_b:h_b + 1])
            xw.append((x2.astype(F32) * wsc).astype(BF16))
            dec.append(jnp.where(left1, etot[:, h_a:h_a + 1], etot[:, h_b:h_b + 1]))
        bg_t = bg.astype(F32).T.astype(BF16)
        snew = jnp.dot(bg_t, jnp.concatenate(xw, axis=1), preferred_element_type=F32)
        hst[:, g * 512:(g + 1) * 512] = hin * jnp.concatenate(dec, axis=1) + snew


def _ssd(rw, act, dt2, a_log):
    B, T, C = rw.B, rw.T, rw.C
    q = SSD_CHUNK
    nct, nlt = C // q, T // q
    ctx0 = (B * T) // q
    r = np.arange(q)
    tri = jnp.asarray(np.stack([r[None, :] <= r[:, None], r[None, :] >= r[:, None]]).astype(np.float32))
    avec = jnp.concatenate([-jnp.exp(a_log.astype(F32)), jnp.zeros((2, LANES - a_log.shape[1]), F32)], axis=1)[:, None, :]

    def lat(d, s):
        return jnp.clip(s - nct, 0, nlt - 1) if d == 0 else nlt - 1 - jnp.clip(s - nct, 0, nlt - 1)

    def blk(d, b, s):
        kc = s if d == 0 else nct - 1 - s
        return jnp.where(s < nct, ctx0 + b * nct + kc, b * nlt + lat(d, s))

    pairs = [(d, b) for d in range(2) for b in range(B)]
    aspec = lambda d, b: pl.BlockSpec((q, act.shape[1]), lambda s: (blk(d, b, s), 0))
    dspec = lambda d, b: pl.BlockSpec((None, q, LANES), lambda s: (d, blk(d, b, s), 0))
    yspec = lambda d: pl.BlockSpec((B, q, 1024), lambda s: (0, lat(d, s), 0))
    yf, yb = pl.pallas_call(
        functools.partial(_ssd_kernel, B),
        grid=(nct + nlt,),
        in_specs=[aspec(d, b) for d, b in pairs] + [dspec(d, b) for d, b in pairs]
        + [pl.BlockSpec((2, q, q), lambda s: (0, 0, 0)), pl.BlockSpec((2, 1, LANES), lambda s: (0, 0, 0))],
        out_specs=[yspec(0), yspec(1)],
        out_shape=[jax.ShapeDtypeStruct((B, T, 1024), BF16)] * 2,
        scratch_shapes=[pltpu.VMEM((2, B, q, 1024), F32)],
        compiler_params=_cparams(1),
        name="ssd_scan",
    )(*([act] * (2 * B)), *([dt2] * (2 * B)), tri, avec)
    return yf.reshape(B * T, 1024), yb.reshape(B * T, 1024)


def _cmul(ar, ai, br, bi):
    return ar * br - ai * bi, ar * bi + ai * br


def _s5_weight_kernel(lre_ref, lim_ref, ls_ref, bre_ref, bim_ref, cre_ref, cim_ref,
                      wsr_ref, wsi_ref, wor_ref, woi_ref, kt_ref, are_ref, aim_ref):
    lre, lim = lre_ref[...], lim_ref[...]
    step = jnp.exp(ls_ref[...])
    er, ei = lre * step, lim * step
    npow = 24
    p = lax.broadcasted_iota(jnp.int32, (1, npow, 1), 1).astype(F32)
    mag = jnp.exp(p * er)
    pre, pim = mag * jnp.cos(p * ei), mag * jnp.sin(p * ei)
    a_re, a_im = pre[:, 1:2, :], pim[:, 1:2, :]
    den = lre * lre + lim * lim
    q_re = ((a_re - 1.0) * lre + a_im * lim) / den
    q_im = (a_im * lre - (a_re - 1.0) * lim) / den
    bb_re, bb_im = _cmul(q_re, q_im, bre_ref[...], bim_ref[...])
    c_re, c_im = cre_ref[...], cim_ref[...]
    ws_r, ws_i, wo_r, wo_i, ca_r, ca_i = [], [], [], [], [], []
    for t in range(S5_Q):
        r, i = _cmul(bb_re, bb_im, pre[:, t:t + 1, :], pim[:, t:t + 1, :])
        ws_r.append(r)
        ws_i.append(i)
        r, i = _cmul(c_re, c_im, pre[:, t:t + 1, :], pim[:, t:t + 1, :])
        ca_r.append(r)
        ca_i.append(i)
        r, i = _cmul(c_re, c_im, pre[:, t + 1:t + 2, :], pim[:, t + 1:t + 2, :])
        wo_r.append(r)
        wo_i.append(-i)
    cat = lambda xs: jnp.concatenate(xs, axis=1)
    wsr_ref[...] = cat(ws_r)
    wsi_ref[...] = cat(ws_i)
    wor_ref[...] = cat(wo_r)
    woi_ref[...] = cat(wo_i)
    bdot = lambda a, b: lax.dot_general(a, b, (((2,), (2,)), ((0,), (0,))), preferred_element_type=F32,
                                        precision=lax.Precision.HIGHEST)
    kt_ref[...] = bdot(cat(ca_r), bb_re) - bdot(cat(ca_i), bb_im)
    are_ref[...] = pre[:, S5_Q:S5_Q + 1, :]
    aim_ref[...] = pim[:, S5_Q:S5_Q + 1, :]


def _s5_weights(lam_re, lam_im, log_step, b_re, b_im, c_re, c_im):
    nd, ng, ns = lam_re.shape
    G = nd * ng
    ch = S5_GROUP
    gb = 8
    qc = S5_Q * ch
    f = lambda a: a.astype(F32)
    args = (f(lam_re).reshape(G, 1, ns), f(lam_im).reshape(G, 1, ns), f(log_step).reshape(G, 1, 1),
            f(b_re).reshape(G, ns, ch).transpose(0, 2, 1), f(b_im).reshape(G, ns, ch).transpose(0, 2, 1),
            f(c_re).reshape(G, ch, ns), f(c_im).reshape(G, ch, ns))
    spec = lambda a: pl.BlockSpec((gb,) + a.shape[1:], lambda i: (i, 0, 0))
    oshape = [jax.ShapeDtypeStruct((G, qc, ns), F32)] * 4 + [jax.ShapeDtypeStruct((G, qc, ch), F32)] \
        + [jax.ShapeDtypeStruct((G, 1, ns), F32)] * 2
    wsr, wsi, wor, woi, kt, a_re, a_im = pl.pallas_call(
        _s5_weight_kernel,
        grid=(G // gb,),
        in_specs=[spec(a) for a in args],
        out_specs=[pl.BlockSpec((gb,) + s.shape[1:], lambda i: (i, 0, 0)) for s in oshape],
        out_shape=oshape,
        compiler_params=_cparams(1),
        name="s5_weights",
    )(*args)

    def by_dir(w, flip_dir):
        w = w.reshape(nd, ng, S5_Q, ch, ns)
        w = jnp.stack([jnp.flip(w[d], axis=1) if d == flip_dir else w[d] for d in range(nd)])
        return w.reshape(nd, ng, qc, ns)

    def pack(w):
        z = jnp.zeros_like(w)
        even = (np.arange(ng) % 2 == 0)[None, :, None, None]
        return jnp.where(even, jnp.concatenate([w, z], axis=-1), jnp.concatenate([z, w], axis=-1)).astype(BF16)

    ws_r, ws_i = pack(by_dir(wsr, 0)), pack(by_dir(wsi, 0))
    wo_r, wo_i = pack(by_dir(wor, 1)), pack(by_dir(woi, 1))
    k = kt.astype(BF16).reshape(nd, ng, S5_Q, ch, ch).transpose(0, 1, 4, 2, 3)
    kf = k.reshape(nd, ng, ch, qc)
    kb = jnp.flip(k, axis=3).reshape(nd, ng, ch, qc)
    rows_f, rows_b = [], []
    for j in range(S5_Q):
        z_f = jnp.zeros((ng, ch, j * ch), BF16)
        z_b = jnp.zeros((ng, ch, (S5_Q - 1 - j) * ch), BF16)
        rows_f.append(jnp.concatenate([z_f, kf[0, :, :, :(S5_Q - j) * ch]], axis=-1))
        rows_b.append(jnp.concatenate([kb[1, :, :, (S5_Q - 1 - j) * ch:], z_b], axis=-1))
    bt = jnp.stack([jnp.stack(rows_f, axis=1), jnp.stack(rows_b, axis=1)]).reshape(nd, ng, qc, qc)
    pair = lambda a: a.reshape(nd, ng // 2, 1, 2 * ns)
    return bt, ws_r, ws_i, wo_r, wo_i, pair(a_re), pair(a_im)


S5_GB = LANES // S5_GROUP


def _s5_kernel(B, nct, nlt, uj_ref, perm_ref, bt_ref, wsr_ref, wsi_ref, wor_ref, woi_ref, are_ref, aim_ref, yj_ref,
               x_scr, y_scr, s_re, s_im):
    gb, npair, qc = S5_GB, S5_GB // 2, S5_Q * S5_GROUP
    lhs = jnp.concatenate([uj_ref[j] for j in range(S5_Q)], axis=1)
    for m in range(gb):
        x_scr[:, m * qc:(m + 1) * qc] = jnp.dot(lhs, perm_ref[:, m * qc:(m + 1) * qc],
                                                preferred_element_type=F32).astype(BF16)
    xg = lambda g: x_scr[:, g * qc:(g + 1) * qc]
    for d in range(2):
        for pr in range(npair):
            for dst, w_ref in ((s_re, wsr_ref), (s_im, wsi_ref)):
                dst[d, pr] = (jnp.dot(xg(2 * pr), w_ref[d, 2 * pr], preferred_element_type=F32)
                              + jnp.dot(xg(2 * pr + 1), w_ref[d, 2 * pr + 1], preferred_element_type=F32))
    chains = [(d, pr, b) for d in range(2) for pr in range(npair) for b in range(B)]
    coef = {(d, pr): (are_ref[d, pr], aim_ref[d, pr]) for d in range(2) for pr in range(npair)}
    ctx0 = B * nlt

    def body(s, carry):
        in_ctx = s < nct
        rows = {}
        for d in range(2):
            kc = s if d == 0 else nct - 1 - s
            kl = s - nct if d == 0 else nlt - 1 - (s - nct)
            for b in range(B):
                rows[(d, b)] = pl.ds(jnp.where(in_ctx, ctx0 + b * nct + kc, b * nlt + kl), 1)
        contrib = [(s_re[d, pr, rows[(d, b)], :], s_im[d, pr, rows[(d, b)], :]) for d, pr, b in chains]
        new = []
        for (d, pr, b), (hr, hi), (sr, si) in zip(chains, carry, contrib):
            ar, ai = coef[(d, pr)]
            s_re[d, pr, rows[(d, b)], :] = hr
            s_im[d, pr, rows[(d, b)], :] = hi
            new.append((ar * hr - ai * hi + sr, ar * hi + ai * hr + si))
        return tuple(new)

    zero = jnp.zeros((1, LANES), F32)
    lax.fori_loop(0, nct + nlt, body, tuple((zero, zero) for _ in chains))
    for g in range(gb):
        acc = None
        for d in range(2):
            t = (jnp.dot(xg(g), bt_ref[d, g], preferred_element_type=F32)
                 + _nt(s_re[d, g // 2].astype(BF16), wor_ref[d, g])
                 + _nt(s_im[d, g // 2].astype(BF16), woi_ref[d, g]))
            acc = t if acc is None else acc + t
        y_scr[:, g * qc:(g + 1) * qc] = acc.astype(BF16)
    for i in range(S5_Q):
        yj_ref[i] = _nt(y_scr[...], perm_ref[i * LANES:(i + 1) * LANES, :]).astype(BF16)


def _s5(rw, uj, weights):
    B, T, C = rw.B, rw.T, rw.C
    bt, ws_r, ws_i, wo_r, wo_i, a_re, a_im = weights
    ng = bt.shape[1]
    q, gb = S5_Q, S5_GB
    nct, nlt = C // q, T // q
    nrow = uj.shape[1]
    qc = q * S5_GROUP
    k = gb * qc
    idx = np.arange(k)
    j, m, c = idx // LANES, (idx % LANES) // S5_GROUP, idx % S5_GROUP
    perm = np.zeros((k, k), np.float32)
    perm[idx, m * qc + j * S5_GROUP + c] = 1.0
    once = dict(pipeline_mode=pl.Buffered(1))
    wspec = lambda n: pl.BlockSpec((2, gb, qc, n), lambda i: (0, i, 0, 0), **once)
    aspec = pl.BlockSpec((2, gb // 2, 1, LANES), lambda i: (0, i, 0, 0))
    return pl.pallas_call(
        functools.partial(_s5_kernel, B, nct, nlt),
        grid=(ng // gb,),
        in_specs=[pl.BlockSpec((q, nrow, LANES), lambda i: (0, 0, i), **once),
                  pl.BlockSpec((k, k), lambda i: (0, 0), **once),
                  wspec(qc), wspec(LANES), wspec(LANES), wspec(LANES), wspec(LANES), aspec, aspec],
        out_specs=pl.BlockSpec((q, nrow, LANES), lambda i: (0, 0, i)),
        out_shape=jax.ShapeDtypeStruct(uj.shape, BF16),
        scratch_shapes=[pltpu.VMEM((nrow, k), BF16), pltpu.VMEM((nrow, k), BF16),
                        pltpu.VMEM((2, gb // 2, nrow, LANES), F32), pltpu.VMEM((2, gb // 2, nrow, LANES), F32)],
        compiler_params=_cparams(1),
        name="s5_scan",
    )(uj, jnp.asarray(perm, BF16), bt, ws_r, ws_i, wo_r, wo_i, a_re, a_im)


def _gelu_tanh(x):
    return 0.5 * x * (1.0 + jnp.tanh(math.sqrt(2.0 / math.pi) * (x + 0.044715 * (x * x * x))))


def _ssm_outproj_kernel(y0_ref, y1_ref, xs_ref, z_ref, v_ref, u_ref, dsk_ref, nw_ref, s5d_ref, gw_ref, gb_ref,
                        x_ref, w_ref, g1_ref, gn_ref, sh2_ref, sc2_ref, wr_ref, br_ref, lt_ref,
                        xo_ref, h2_ref, rt_ref, cnt_ref, carry, v_scr):
    i = pl.program_id(0)
    y = y0_ref[...].astype(F32) + y1_ref[...].astype(F32) + dsk_ref[...] * xs_ref[...].astype(F32)
    y = _rms(y * _silu(z_ref[...].astype(F32))) * nw_ref[...]
    ntile = v_scr.shape[0]
    nchunk = v_scr.shape[1] // S5_Q
    for j in range(S5_Q):
        for t in range(ntile):
            v_scr[t, pl.ds(j, nchunk, stride=S5_Q), :] = v_ref[j, :, t * LANES:(t + 1) * LANES].astype(F32)
    s5_y = jnp.concatenate([v_scr[t] for t in range(ntile)], axis=1)
    v = _gelu_tanh(s5_y + s5d_ref[...] * u_ref[...].astype(F32))
    v = v * _sigmoid(jnp.dot(v.astype(BF16), gw_ref[...], preferred_element_type=F32) + gb_ref[...])
    mix = jnp.concatenate([y, v], axis=1).astype(BF16)
    yo = jnp.dot(mix, w_ref[...], preferred_element_type=F32)
    _post_mixer(i, x_ref[...], yo, g1_ref[...], gn_ref[...], sh2_ref[...], sc2_ref[...], wr_ref, br_ref, lt_ref,
                xo_ref, h2_ref, rt_ref, cnt_ref, carry)


def _ssm_outproj(rw, ssd_y, act, z, s5_y, u, d_skip, norm_w, s5_d, glu_w, glu_b, xall, w_out, mods, norm_ffn, wr, br):
    D, tm = rw.D, rw.tm
    ntiles = rw.nlat
    post_in, post_out = _post_specs(rw)
    row = lambda i: (i, 0)
    vec = lambda n: pl.BlockSpec((1, n), lambda i: (0, 0))
    dsk = jnp.repeat(d_skip.astype(F32), HEAD_DIM)[None, :]
    return pl.pallas_call(
        _ssm_outproj_kernel,
        grid=(ntiles,),
        in_specs=[pl.BlockSpec((tm, 1024), row), pl.BlockSpec((tm, 1024), row),
                  pl.BlockSpec((tm, 1024), row), pl.BlockSpec((tm, 1024), row),
                  pl.BlockSpec((S5_Q, tm // S5_Q, 512), lambda i: (0, i, 0)),
                  pl.BlockSpec((tm, 512), row), vec(1024), vec(1024), vec(512),
                  pl.BlockSpec((512, 512), lambda i: (0, 0)), vec(512),
                  pl.BlockSpec((tm, D), row), pl.BlockSpec((1536, D), lambda i: (0, 0)), _mod_spec(rw, 2)] + post_in,
        out_specs=post_out,
        out_shape=_post_shapes(ntiles * tm, D),
        scratch_shapes=[pltpu.VMEM((1, LANES), F32), pltpu.VMEM((512 // LANES, tm, LANES), F32)],
        compiler_params=_cparams(1),
        name="ssm_outproj_router",
    )(ssd_y[0], ssd_y[1], act, z, s5_y, u, dsk, norm_w[None, :], s5_d[None, :], glu_w.astype(BF16), glu_b[None, :],
      xall, w_out.astype(BF16), mods, norm_ffn[None, :], mods, mods, wr, br, _lower_tri(tm))


def kernel(x, c, ctx, c_ctx, mod_w, mod_b, norm_mix, norm_ffn, att_w_in, att_w_out, na_q_norm, na_k_norm, na_rel_bias, wa_q_norm, wa_k_norm, wa_sink, ssm_w_in, ssm_w_out, ssd_conv_w, ssd_conv_b, ssd_dt_bias, ssd_a_log, ssd_d, ssd_norm, s5_lambda_re, s5_lambda_im, s5_log_step, s5_b_re, s5_b_im, s5_c_re, s5_c_im, s5_d, s5_glu_w, s5_glu_b, moe_w_group, moe_b_group, moe_w_expert, moe_b_expert, moe_w13, moe_w2):
    B, T, D = x.shape
    C = ctx.shape[1]
    rw = _Rows(B, T, C, D, ROW_TILE)
    xl = x.reshape(B * T, D)
    xc = ctx.reshape(B * C, D)
    cm = jnp.concatenate([c, c_ctx[None, :], jnp.zeros((8 - B - 1, D), F32)], axis=0)
    mods = _modulation(cm, mod_w, mod_b)
    mods = mods.reshape(mods.shape[0], 8, 1, 6 * D)

    m0 = mods[0]
    qkv = _att_inproj(rw, xl, xc, m0, norm_mix[0], att_w_in[0], na_q_norm[0], na_k_norm[0], wa_q_norm[0],
                      wa_k_norm[0])
    na = _na_attention(rw, qkv, na_rel_bias[0])
    wa = _wa_attention(rw, qkv, wa_sink[0])
    cx = _ctx_attention(rw, qkv, wa_sink[0])
    wr, br = _router_weights(moe_w_group[0], moe_b_group[0], moe_w_expert[0], moe_b_expert[0])
    xall, h2, route, counts = _att_outproj(rw, na, wa, cx, xl, xc, att_w_out[0], m0, norm_ffn[0], wr, br)
    y1, y2 = _moe(h2, route, counts, moe_w13, moe_w2, 0)

    m1 = mods[1]
    xall, z, xbc, u, uj, dtr = _ssm_inproj(rw, xall, y1, y2, route, m0, m1, norm_mix[1], ssm_w_in[0])
    act, dt2 = _ssm_conv(rw, xbc, dtr, ssd_conv_w[0], ssd_conv_b[0], ssd_dt_bias[0])
    ssd_y = _ssd(rw, act, dt2, ssd_a_log[0])
    s5_w = _s5_weights(s5_lambda_re[0], s5_lambda_im[0], s5_log_step[0], s5_b_re[0], s5_b_im[0], s5_c_re[0],
                       s5_c_im[0])
    s5_y = _s5(rw, uj, s5_w)
    wr, br = _router_weights(moe_w_group[1], moe_b_group[1], moe_w_expert[1], moe_b_expert[1])
    xlat, h2, route, counts = _ssm_outproj(rw, ssd_y, act, z, s5_y, u, ssd_d[0], ssd_norm[0], s5_d[0], s5_glu_w[0],
                                           s5_glu_b[0], xall, ssm_w_out[0], m1, norm_ffn[1], wr, br)
    y1, y2 = _moe(h2, route, counts, moe_w13, moe_w2, 1)
    out = _combine(rw, rw.nlat, xlat, y1, y2, route, m1)
    return out.reshape(B, T, D)
```

```python
import functools
import math

import jax
import jax.numpy as jnp
import numpy as np
from jax import lax
from jax.experimental import pallas as pl
from jax.experimental.pallas import tpu as pltpu

F32 = jnp.float32
BF16 = jnp.bfloat16

EPS = 1e-6
NEG_INF = -1e30
GRID_W = 64
HEAD_DIM = 64
NA_KH = 8
NA_KW = 16
WA_BLOCK = 128
ROPE_BASE = 10000.0
SSD_CHUNK = 128
S5_GROUP = 16
S5_STATE = 64
MOE_GROUPS = 4
MOE_EPG = 8
MOE_EXPERTS = MOE_GROUPS * MOE_EPG

LANES = 128
ROW_TILE = 512
MXU_W = 256
MOE_TILE = MXU_W
VMEM_LIMIT = 56 * 1024 * 1024
MOE_VMEM_LIMIT = 60 * 1024 * 1024


def _cparams(n_axes, vmem=VMEM_LIMIT):
    return pltpu.CompilerParams(dimension_semantics=("arbitrary",) * n_axes, vmem_limit_bytes=vmem)


def _sigmoid(x):
    return 1.0 / (1.0 + jnp.exp(-x))


def _silu(x):
    return x * _sigmoid(x)


def _rms(x, eps=EPS):
    return x * lax.rsqrt(jnp.mean(x * x, axis=-1, keepdims=True) + eps)


def _ada_norm(x, g, shift, scale):
    return (_rms(x) * g) * (1.0 + scale) + shift


def _mod_kernel(c_ref, w_ref, b_ref, o_ref):
    a = _silu(c_ref[...])
    o_ref[...] = jnp.dot(a, w_ref[...], preferred_element_type=F32, precision=lax.Precision.HIGHEST) + b_ref[...]


def _modulation(cm, mod_w, mod_b):
    depth, d, n6 = mod_w.shape
    tn = 1024
    return pl.pallas_call(
        _mod_kernel,
        grid=(depth, n6 // tn),
        in_specs=[pl.BlockSpec((8, d), lambda l, j: (0, 0)),
                  pl.BlockSpec((None, d, tn), lambda l, j: (l, 0, j)),
                  pl.BlockSpec((None, 1, tn), lambda l, j: (l, 0, j))],
        out_specs=pl.BlockSpec((None, 8, tn), lambda l, j: (l, 0, j)),
        out_shape=jax.ShapeDtypeStruct((depth, 8, n6), F32),
        compiler_params=_cparams(2),
        name="modulation",
    )(cm, mod_w, mod_b.reshape(depth, 1, n6))


class _Rows:
    def __init__(self, B, T, C, D, tm):
        assert T % tm == 0 and (B * C) % tm == 0
        self.B, self.T, self.C, self.D, self.tm = B, T, C, D, tm
        self.tpb = T // tm
        self.nlat = B * self.tpb
        self.nctx = (B * C) // tm
        self.ntot = self.nlat + self.nctx
        self.rows = B * (T + C)

    def group(self, i):
        return jnp.where(i < self.nlat, i // self.tpb, self.B)


def _mod_spec(rw, col):
    return pl.BlockSpec((None, 1, rw.D), lambda i, *_: (rw.group(i), 0, col))


def _seg_norm(y, seg, gcol):
    ss = jnp.dot((y * y).astype(BF16), seg, preferred_element_type=F32)
    return y * lax.rsqrt(ss + EPS) * gcol


def _rope(y, cos, sin):
    w = y.shape[-1]
    lane = lax.broadcasted_iota(jnp.int32, y.shape, 1)
    first = (lane % 32) < 16
    partner = jnp.where(first, pltpu.roll(y, w - 16, 1), pltpu.roll(y, 16, 1))
    return y * cos + partner * sin


def _dup_halves(k):
    lane = lax.broadcasted_iota(jnp.int32, k.shape, 1)
    sw = pltpu.roll(k, 64, 1)
    return jnp.where(lane < 64, k, sw), jnp.where(lane < 64, sw, k)


def _att_inproj_kernel(nlat, xl_ref, xc_ref, g_ref, sh_ref, sc_ref, w_ref, gcol_ref, cos_ref, sin_ref, seg_ref,
                       o_ref, h_scr):
    i = pl.program_id(0)
    x = jnp.where(i < nlat, xl_ref[...], xc_ref[...])
    h_scr[...] = _ada_norm(x, g_ref[...], sh_ref[...], sc_ref[...]).astype(BF16)
    seg = seg_ref[...]
    cos2 = jnp.concatenate([cos_ref[...], cos_ref[...]], axis=1)
    sin2 = jnp.concatenate([sin_ref[...], sin_ref[...]], axis=1)
    for c in range(w_ref.shape[1] // MXU_W):
        c0 = c * MXU_W
        cols = slice(c0, c0 + MXU_W)
        y = jnp.dot(h_scr[...], w_ref[:, cols], preferred_element_type=F32)
        gcol = gcol_ref[:, cols]
        if c in (0, 1, 2, 3):
            o_ref[:, cols] = _seg_norm(y, seg, gcol).astype(BF16)
        elif c in (4, 5):
            o_ref[:, cols] = y.astype(BF16)
        elif c in (6, 7):
            o_ref[:, cols] = _rope(_seg_norm(y, seg, gcol), cos2, sin2).astype(BF16)
        else:
            lane = lax.broadcasted_iota(jnp.int32, y.shape, 1)
            yk = jnp.where(lane < LANES, _seg_norm(y, seg, gcol), y)
            yr = jnp.where(lane < LANES, _rope(yk, cos2, sin2), yk)
            k0, k1 = _dup_halves(yr[:, :LANES])
            v0, v1 = _dup_halves(yr[:, LANES:])
            for t, dup in enumerate((k0, k1, v0, v1)):
                o_ref[:, c0 + t * LANES:c0 + (t + 1) * LANES] = dup.astype(BF16)


def _rope_tables(T, tm):
    t = np.arange(T)
    d = np.arange(HEAD_DIM)
    nf = HEAD_DIM // 4
    inv = jnp.asarray(ROPE_BASE, F32) ** (-jnp.arange(nf, dtype=F32) / nf)
    pos = np.where((d // 32 == 0)[None, :], (t // GRID_W)[:, None], (t % GRID_W)[:, None])
    ang = jnp.asarray(pos, F32) * inv[d % nf][None, :]
    sign = np.where((d % 32) < 16, -1.0, 1.0).astype(np.float32)
    cos = jnp.cos(ang)
    sin = jnp.sin(ang) * sign[None, :]
    cos = jnp.concatenate([cos, jnp.ones((tm, HEAD_DIM), F32)], axis=0)
    sin = jnp.concatenate([sin, jnp.zeros((tm, HEAD_DIM), F32)], axis=0)
    return jnp.tile(cos, (1, 2)), jnp.tile(sin, (1, 2))


def _att_inproj(rw, xl, xc, mods, norm_g, w_in, na_qn, na_kn, wa_qn, wa_kn):
    D, tm = rw.D, rw.tm
    scale = HEAD_DIM ** -0.5
    gcol = jnp.concatenate([jnp.tile(na_qn * scale, 8), jnp.tile(na_kn, 8), jnp.ones((512,), F32),
                            jnp.tile(wa_qn * scale, 8), jnp.tile(wa_kn, 2), jnp.ones((128,), F32)])[None, :]
    cos, sin = _rope_tables(rw.T, tm)
    segn = np.arange(256) // 64
    seg = jnp.asarray((segn[:, None] == segn[None, :]).astype(np.float32) / 64.0, BF16)
    nlat, tpb = rw.nlat, rw.tpb
    return pl.pallas_call(
        functools.partial(_att_inproj_kernel, nlat),
        grid=(rw.ntot,),
        in_specs=[pl.BlockSpec((tm, D), lambda i: (jnp.minimum(i, nlat - 1), 0)),
                  pl.BlockSpec((tm, D), lambda i: (jnp.maximum(i - nlat, 0), 0)),
                  pl.BlockSpec((1, D), lambda i: (0, 0)),
                  _mod_spec(rw, 0), _mod_spec(rw, 1),
                  pl.BlockSpec((D, 2304), lambda i: (0, 0)),
                  pl.BlockSpec((1, 2304), lambda i: (0, 0)),
                  pl.BlockSpec((tm, 128), lambda i: (jnp.where(i < nlat, i % tpb, tpb), 0)),
                  pl.BlockSpec((tm, 128), lambda i: (jnp.where(i < nlat, i % tpb, tpb), 0)),
                  pl.BlockSpec((256, 256), lambda i: (0, 0))],
        out_specs=pl.BlockSpec((tm, 2560), lambda i: (i, 0)),
        out_shape=jax.ShapeDtypeStruct((rw.rows, 2560), BF16),
        scratch_shapes=[pltpu.VMEM((tm, D), BF16)],
        compiler_params=_cparams(1),
        name="att_inproj",
    )(xl, xc, norm_g[None, :], mods, mods, w_in.astype(BF16), gcol, cos, sin, seg)


def _route(lg, lt, carry):
    lane = lax.broadcasted_iota(jnp.int32, lg.shape, 1).astype(F32)
    gm = lane < MOE_GROUPS
    mg = jnp.max(jnp.where(gm, lg, NEG_INF), axis=-1, keepdims=True)
    eg = jnp.where(gm, jnp.exp(jnp.where(gm, lg, NEG_INF) - mg), 0.0)
    pg = eg / jnp.sum(eg, axis=-1, keepdims=True)
    ptop = jnp.max(pg, axis=-1, keepdims=True)
    gidx = jnp.min(jnp.where(gm & (pg == ptop), lane, 1e9), axis=-1, keepdims=True)
    lo = MOE_GROUPS + MOE_EPG * gidx
    em = (lane >= lo) & (lane < lo + MOE_EPG)
    le = jnp.where(em, lg, NEG_INF)
    ee = jnp.where(em, jnp.exp(le - jnp.max(le, axis=-1, keepdims=True)), 0.0)
    pe = ee / jnp.sum(ee, axis=-1, keepdims=True)
    v1 = jnp.max(jnp.where(em, pe, -1.0), axis=-1, keepdims=True)
    i1 = jnp.min(jnp.where(em & (pe == v1), lane, 1e9), axis=-1, keepdims=True)
    em2 = em & (lane != i1)
    v2 = jnp.max(jnp.where(em2, pe, -1.0), axis=-1, keepdims=True)
    i2 = jnp.min(jnp.where(em2 & (pe == v2), lane, 1e9), axis=-1, keepdims=True)
    den = v1 + v2
    w1 = v1 / den * ptop
    w2 = v2 / den * ptop
    e1 = i1 - MOE_GROUPS
    e2 = i2 - MOE_GROUPS
    m1 = lane == e1
    m2 = lane == e2
    oh = jnp.where(m1 | m2, 1.0, 0.0)
    cnt = jnp.dot(lt, oh.astype(BF16), preferred_element_type=F32) + carry
    r1 = jnp.sum(jnp.where(m1, cnt, 0.0), axis=-1, keepdims=True)
    r2 = jnp.sum(jnp.where(m2, cnt, 0.0), axis=-1, keepdims=True)
    route = jnp.where(lane == 0, e1, jnp.where(lane == 1, e2, jnp.where(lane == 2, w1, jnp.where(
        lane == 3, w2, jnp.where(lane == 4, r1, jnp.where(lane == 5, r2, 0.0))))))
    return route, carry + jnp.sum(oh, axis=0, keepdims=True)


def _post_mixer(i, x, y, g1, gn, sh2, sc2, wr_ref, br_ref, lt_ref, xo_ref, h2_ref, rt_ref, cnt_ref, carry):
    xn = x + g1 * y
    xo_ref[...] = xn
    h2 = _ada_norm(xn, gn, sh2, sc2)
    hb = h2.astype(BF16)
    hbf = hb.astype(F32)
    half = h2.shape[1] // 2
    h2_ref[...] = pltpu.pack_elementwise([h2[:, :half], h2[:, half:]], packed_dtype=BF16)
    hl = (h2 - hbf).astype(BF16)
    lg = (jnp.dot(hb, wr_ref[0], preferred_element_type=F32)
          + (jnp.dot(hb, wr_ref[1], preferred_element_type=F32) + jnp.dot(hl, wr_ref[0], preferred_element_type=F32))
          + br_ref[...])

    @pl.when(i == 0)
    def _():
        carry[...] = jnp.zeros_like(carry)

    route, newc = _route(lg, lt_ref[...], carry[...])
    rt_ref[...] = route
    carry[...] = newc
    cnt_ref[...] = newc


def _att_outproj_kernel(nlat, na_ref, wa_ref, cx_ref, xl_ref, xc_ref, w_ref, g1_ref, gn_ref, sh2_ref, sc2_ref,
                        wr_ref, br_ref, lt_ref, xo_ref, h2_ref, rt_ref, cnt_ref, carry):
    i = pl.program_id(0)
    lat = i < nlat
    mix = jnp.where(lat, jnp.concatenate([na_ref[...], wa_ref[...]], axis=1), cx_ref[...])
    y = jnp.dot(mix, w_ref[...], preferred_element_type=F32)
    x = jnp.where(lat, xl_ref[...], xc_ref[...])
    _post_mixer(i, x, y, g1_ref[...], gn_ref[...], sh2_ref[...], sc2_ref[...], wr_ref, br_ref, lt_ref,
                xo_ref, h2_ref, rt_ref, cnt_ref, carry)


def _router_weights(w_group, b_group, w_expert, b_expert):
    D = w_group.shape[0]
    pad = LANES - MOE_GROUPS - MOE_EXPERTS
    wr = jnp.concatenate([w_group, w_expert, jnp.zeros((D, pad), F32)], axis=1)
    br = jnp.concatenate([b_group, b_expert, jnp.zeros((pad,), F32)])[None, :]
    hi = wr.astype(BF16)
    lo = (wr - hi.astype(F32)).astype(BF16)
    return jnp.stack([hi, lo]), br


def _lower_tri(tm):
    r = np.arange(tm)
    return jnp.asarray((r[None, :] < r[:, None]).astype(np.float32), BF16)


def _post_specs(rw):
    D, tm = rw.D, rw.tm
    return ([pl.BlockSpec((1, D), lambda i: (0, 0)), _mod_spec(rw, 3), _mod_spec(rw, 4),
             pl.BlockSpec((2, D, LANES), lambda i: (0, 0, 0)), pl.BlockSpec((1, LANES), lambda i: (0, 0)),
             pl.BlockSpec((tm, tm), lambda i: (0, 0))],
            [pl.BlockSpec((tm, D), lambda i: (i, 0)), pl.BlockSpec((tm, D // 2), lambda i: (i, 0)),
             pl.BlockSpec((tm, LANES), lambda i: (i, 0)), pl.BlockSpec((1, LANES), lambda i: (0, 0))])


def _post_shapes(nrows, D):
    return [jax.ShapeDtypeStruct((nrows, D), F32), jax.ShapeDtypeStruct((nrows, D // 2), jnp.uint32),
            jax.ShapeDtypeStruct((nrows, LANES), F32), jax.ShapeDtypeStruct((1, LANES), F32)]


def _att_outproj(rw, na, wa, cx, xl, xc, w_out, mods, norm_ffn, wr, br):
    D, tm, nlat = rw.D, rw.tm, rw.nlat
    post_in, post_out = _post_specs(rw)
    latmap = lambda i: (jnp.minimum(i, nlat - 1), 0)
    ctxmap = lambda i: (jnp.maximum(i - nlat, 0), 0)
    return pl.pallas_call(
        functools.partial(_att_outproj_kernel, nlat),
        grid=(rw.ntot,),
        in_specs=[pl.BlockSpec((tm, 512), latmap), pl.BlockSpec((tm, 512), latmap), pl.BlockSpec((tm, D), ctxmap),
                  pl.BlockSpec((tm, D), latmap), pl.BlockSpec((tm, D), ctxmap),
                  pl.BlockSpec((D, D), lambda i: (0, 0)), _mod_spec(rw, 2)] + post_in,
        out_specs=post_out,
        out_shape=_post_shapes(rw.rows, D),
        scratch_shapes=[pltpu.VMEM((1, LANES), F32)],
        compiler_params=_cparams(1),
        name="att_outproj_router",
    )(na, wa, cx, xl, xc, w_out.astype(BF16), mods, norm_ffn[None, :], mods, mods, wr, br, _lower_tri(tm))


def _moe_kernel(te_ref, nu_ref, src_ref, nsrc_ref, hp_ref, w13_ref, w2_ref, o_ref, w13b, w2b, xa, xb):
    i = pl.program_id(0)
    prev = te_ref[jnp.maximum(i - 1, 0)]
    changed = (i == 0) | (te_ref[i] != prev)
    tg = xa.shape[0]

    @pl.when(changed)
    def _():
        w13b[...] = w13_ref[...].astype(BF16)
        w2b[...] = w2_ref[...].astype(BF16)

    @pl.when(i == 0)
    def _():
        def fetch(j, carry):
            xa[pl.ds(j, 1), :] = hp_ref[pl.ds(src_ref[0, j], 1), :]
            return carry

        lax.fori_loop(0, tg, fetch, 0, unroll=8)

    def step(cur, nxt):
        for j in range(tg):
            nxt[pl.ds(j, 1), :] = hp_ref[pl.ds(nsrc_ref[0, j], 1), :]
        ff = w2b.shape[0]
        half = cur.shape[1]
        w = cur[...]
        unpack = functools.partial(pltpu.unpack_elementwise, packed_dtype=BF16, unpacked_dtype=F32)
        x_lo = unpack(w, index=0).astype(BF16)
        x_hi = unpack(w, index=1).astype(BF16)
        a13 = (jnp.dot(x_lo, w13b[:half, :], preferred_element_type=F32)
               + jnp.dot(x_hi, w13b[half:, :], preferred_element_type=F32))
        act = _silu(a13[:, :ff]) * a13[:, ff:]
        o_ref[...] = jnp.dot(act.astype(BF16), w2b[...], preferred_element_type=F32).astype(BF16)

    used = i < nu_ref[0]

    @pl.when(used & (i % 2 == 0))
    def _():
        step(xa, xb)

    @pl.when(used & (i % 2 == 1))
    def _():
        step(xb, xa)

    @pl.when(i >= nu_ref[0])
    def _():
        o_ref[...] = jnp.zeros_like(o_ref)


def _moe(h2p, route, counts, w13, w2, layer):
    N = h2p.shape[0]
    D = 2 * h2p.shape[1]
    _, E, _, F2 = w13.shape
    tg = MOE_TILE
    nt = (2 * N) // tg + E
    e = route[:, 0:2].astype(jnp.int32)
    rank = route[:, 4:6].astype(jnp.int32)
    cnt = counts[0, :E].astype(jnp.int32)
    ntile_e = (cnt + tg - 1) // tg
    tile_end = jnp.cumsum(ntile_e)
    offs = (tile_end - ntile_e) * tg
    onehot = (e[:, :, None] == jnp.arange(E, dtype=jnp.int32)).astype(jnp.int32)
    dest = jnp.sum(onehot * offs, axis=-1) + rank
    src = jnp.zeros((nt * tg,), jnp.int32).at[dest.reshape(-1)].set(jnp.repeat(jnp.arange(N, dtype=jnp.int32), 2))
    tile_id = jnp.arange(nt, dtype=jnp.int32)
    nu = tile_end[-1:].astype(jnp.int32)
    te = jnp.sum((tile_end[None, :] <= jnp.minimum(tile_id, nu[0] - 1)[:, None]).astype(jnp.int32), axis=1)
    te = jnp.minimum(te, E - 1)
    ys = pl.pallas_call(
        _moe_kernel,
        grid_spec=pltpu.PrefetchScalarGridSpec(
            num_scalar_prefetch=2,
            grid=(nt,),
            in_specs=[pl.BlockSpec((None, 1, tg), lambda i, te, nu: (i, 0, 0), memory_space=pltpu.SMEM),
                      pl.BlockSpec((None, 1, tg), lambda i, te, nu: (jnp.minimum(i + 1, nt - 1), 0, 0),
                                   memory_space=pltpu.SMEM),
                      pl.BlockSpec((N, D // 2), lambda i, te, nu: (0, 0), pipeline_mode=pl.Buffered(1)),
                      pl.BlockSpec((None, None, D, F2), lambda i, te, nu: (layer, te[i], 0, 0)),
                      pl.BlockSpec((None, None, F2 // 2, D), lambda i, te, nu: (layer, te[i], 0, 0))],
            out_specs=pl.BlockSpec((tg, D), lambda i, te, nu: (i, 0)),
            scratch_shapes=[pltpu.VMEM((D, F2), BF16), pltpu.VMEM((F2 // 2, D), BF16),
                            pltpu.VMEM((tg, D // 2), jnp.uint32), pltpu.VMEM((tg, D // 2), jnp.uint32)]),
        out_shape=jax.ShapeDtypeStruct((nt * tg, D), BF16),
        compiler_params=_cparams(1, vmem=MOE_VMEM_LIMIT),
        name="moe_experts",
    )(te, nu, src.reshape(nt, 1, tg), src.reshape(nt, 1, tg), h2p, w13, w2)
    pick = lambda k: ys.at[dest[:, k]].get(mode="promise_in_bounds")
    return pick(0), pick(1)


def _moe_residual(x_ref, y1_ref, y2_ref, rt_ref, g2_ref):
    rt = rt_ref[...]
    f = rt[:, 2:3] * y1_ref[...].astype(F32) + rt[:, 3:4] * y2_ref[...].astype(F32)
    return x_ref[...] + g2_ref[...] * f


def _combine_kernel(x_ref, y1_ref, y2_ref, rt_ref, g2_ref, o_ref):
    o_ref[...] = _moe_residual(x_ref, y1_ref, y2_ref, rt_ref, g2_ref)


def _combine(rw, ntiles, xall, y1, y2, route, mods):
    D, tm = rw.D, rw.tm
    row = lambda i: (i, 0)
    return pl.pallas_call(
        _combine_kernel,
        grid=(ntiles,),
        in_specs=[pl.BlockSpec((tm, D), row), pl.BlockSpec((tm, D), row), pl.BlockSpec((tm, D), row),
                  pl.BlockSpec((tm, LANES), row), _mod_spec(rw, 5)],
        out_specs=pl.BlockSpec((tm, D), row),
        out_shape=jax.ShapeDtypeStruct((ntiles * tm, D), F32),
        compiler_params=_cparams(1),
        name="moe_combine",
    )(xall, y1, y2, route, mods)


NA_QROWS = 8
NA_KROWS = 16


def _na_first_key_row(variant, a):
    return (max(a - 4, 0) + 4, a, min(a, 4))[variant]


def _na_key_lanes(row0):
    a = row0 // GRID_W
    starts = [_na_first_key_row(v, a) for v in range(3)]
    lo = (min(starts) * GRID_W) // LANES * LANES
    hi = -(-((max(starts) + NA_KH) * GRID_W) // LANES) * LANES
    return lo, hi


def _na_bias_tiles(rpb):
    H = rpb.shape[0]
    i = np.arange(GRID_W)
    c0 = np.clip(i - NA_KW // 2, 0, GRID_W - NA_KW)
    j = np.arange(GRID_W)
    colvalid = (j[None, :] >= c0[:, None]) & (j[None, :] < c0[:, None] + NA_KW)
    dc = np.clip(j[None, :] - i[:, None] + NA_KW - 1, 0, 2 * NA_KW - 2)
    onehot = ((dc[None] == np.arange(2 * NA_KW - 1)[:, None, None]) & colvalid[None]).astype(np.float32)
    tiles = jnp.einsum('hrc,cij->hrij', rpb.astype(F32), jnp.asarray(onehot), precision=lax.Precision.HIGHEST)
    tiles = tiles + jnp.asarray(np.where(colvalid, 0.0, NEG_INF).astype(np.float32))
    return tiles.transpose(0, 2, 1, 3).reshape(H, GRID_W, (2 * NA_KH - 1) * GRID_W)


def _na_fill_bias(variant, tiles_ref, bias_scr):
    for hh in range(2):
        for a in range(NA_QROWS):
            start = _na_first_key_row(variant, a)
            dr0 = start - a + 3
            rows = slice(a * GRID_W, (a + 1) * GRID_W)
            w0, w1 = start * GRID_W, (start + NA_KH) * GRID_W
            if w0 > 0:
                bias_scr[hh, rows, 0:w0] = jnp.full((GRID_W, w0), NEG_INF, F32)
            bias_scr[hh, rows, w0:w1] = tiles_ref[hh, :, dr0 * GRID_W:(dr0 + NA_KH) * GRID_W]
            if w1 < NA_KROWS * GRID_W:
                bias_scr[hh, rows, w1:] = jnp.full((GRID_W, NA_KROWS * GRID_W - w1), NEG_INF, F32)


def _softmax_pv(parts, extra=None, rc=64):
    m_rows = parts[0][0].shape[0]
    probs = [[] for _ in parts]
    inv_l = []
    for r0 in range(0, m_rows, rc):
        sc = []
        for s, _, bias_fn, lanes_fn in parts:
            l0, l1 = (0, s.shape[1]) if lanes_fn is None else lanes_fn(r0)
            c = s[r0:r0 + rc, l0:l1]
            if bias_fn is not None:
                c = c + bias_fn(r0, rc, slice(l0, l1))
            sc.append((c, l0, s.shape[1] - l1))
        mx = functools.reduce(jnp.maximum, [jnp.max(c, axis=-1, keepdims=True) for c, _, _ in sc])
        if extra is not None:
            mx = jnp.maximum(mx, extra[r0:r0 + rc])
        l = jnp.zeros_like(mx) if extra is None else jnp.exp(extra[r0:r0 + rc] - mx)
        for k, (c, before, after) in enumerate(sc):
            p = jnp.exp(c - mx)
            l = l + jnp.sum(p, axis=-1, keepdims=True)
            row = [jnp.zeros((rc, before), BF16)] * (before > 0) + [p.astype(BF16)] \
                + [jnp.zeros((rc, after), BF16)] * (after > 0)
            probs[k].append(row[0] if len(row) == 1 else jnp.concatenate(row, axis=1))
        inv_l.append(1.0 / l)
    o = None
    for k, (_, v, _, _) in enumerate(parts):
        pv = jnp.dot(jnp.concatenate(probs[k], axis=0), v, preferred_element_type=F32)
        o = pv if o is None else o + pv
    return o * jnp.concatenate(inv_l, axis=0)


def _nt(a, b):
    return lax.dot_general(a, b, (((1,), (1,)), ((), ())), preferred_element_type=F32)


NA_REFS_PER_BATCH = 11


def _na_kernel(n_rb, nb, *refs):
    tiles_ref, o_ref, bias_ref = refs[nb * NA_REFS_PER_BATCH:]
    rb = pl.program_id(1)
    for variant, at_rb in ((0, 0), (1, 1), (2, n_rb - 1)):
        @pl.when(rb == at_rb)
        def _(variant=variant):
            _na_fill_bias(variant, tiles_ref, bias_ref)

    for b in range(nb):
        q_ref, k0, k1, k2, k3, v0, v1, v2, v3, kc_ref, vc_ref = refs[b * NA_REFS_PER_BATCH:(b + 1) * NA_REFS_PER_BATCH]
        q2 = q_ref[...]
        kw = jnp.concatenate([k0[...], k1[...], k2[...], k3[...]], axis=0)
        vw = jnp.concatenate([v0[...], v1[...], v2[...], v3[...]], axis=0)
        kc = kc_ref[...]
        vc = vc_ref[...]
        lane = lax.broadcasted_iota(jnp.int32, q2.shape, 1)
        out = jnp.zeros(q2.shape, F32)
        for hh in range(2):
            m = (lane < HEAD_DIM) if hh == 0 else (lane >= HEAD_DIM)
            qm = jnp.where(m, q2, jnp.zeros_like(q2))
            o = _softmax_pv([(_nt(qm, kw), vw, lambda r0, rc, lanes, hh=hh: bias_ref[hh, r0:r0 + rc, lanes],
                              _na_key_lanes), (_nt(qm, kc), vc, None, None)], rc=32)
            out = jnp.where(m, o, out)
        o_ref[b] = out.astype(BF16)


def _na_attention(rw, qkv, rpb):
    B, T, C = rw.B, rw.T, rw.C
    tq = NA_QROWS * GRID_W
    tk = tq // 2
    n_rb = T // tq
    nkb = T // tk
    assert T % tq == 0 and n_rb >= 2 and (B * T) % C == 0
    tiles = _na_bias_tiles(rpb)
    ctxrow = (B * T) // C

    def batch_specs(b):
        kv = lambda j, col: pl.BlockSpec(
            (tk, LANES), lambda p, rb: (b * nkb + jnp.clip(2 * rb - 1 + j, 0, nkb - 1), col + p))
        return ([pl.BlockSpec((tq, LANES), lambda p, rb: (b * n_rb + rb, p))]
                + [kv(j, 4) for j in range(4)] + [kv(j, 8) for j in range(4)]
                + [pl.BlockSpec((C, LANES), lambda p, rb: (ctxrow + b, 4 + p)),
                   pl.BlockSpec((C, LANES), lambda p, rb: (ctxrow + b, 8 + p))])

    out = pl.pallas_call(
        functools.partial(_na_kernel, n_rb, B),
        grid=(4, n_rb),
        in_specs=sum([batch_specs(b) for b in range(B)], [])
        + [pl.BlockSpec((2,) + tiles.shape[1:], lambda p, rb: (p, 0, 0))],
        out_specs=pl.BlockSpec((B, tq, LANES), lambda p, rb: (0, rb, p)),
        out_shape=jax.ShapeDtypeStruct((B, T, 4 * LANES), BF16),
        scratch_shapes=[pltpu.VMEM((2, tq, 2 * tq), F32)],
        compiler_params=_cparams(2),
        name="neighbourhood_attention",
    )(*([qkv] * (B * NA_REFS_PER_BATCH)), tiles)
    return out.reshape(B * T, 4 * LANES)


WA_QBLOCKS = 2


def _wa_kernel(nb, sink_ref, q_ref, *refs):
    step = pl.program_id(1)
    blk = WA_BLOCK
    nkb = WA_QBLOCKS + 2
    kblocks, vblocks = refs[:nkb], refs[nkb:2 * nkb]
    kx_ref, vx_ref, o_ref = refs[2 * nkb:]
    lane = lax.broadcasted_iota(jnp.int32, (blk, LANES), 1)
    zero = jnp.zeros((blk, LANES), BF16)
    qi = lax.broadcasted_iota(jnp.int32, (blk, 3 * blk), 0)
    ks = lax.broadcasted_iota(jnp.int32, (blk, 3 * blk), 1)
    for qb in range(WA_QBLOCKS):
        n = step * WA_QBLOCKS + qb
        lo = jnp.where(n > 0, 0, blk)
        hi = jnp.where(n < nb - 1, 3 * blk, 2 * blk)
        valid = (ks >= qi) & (ks <= qi + 2 * blk) & (ks >= lo) & (ks < hi)
        band_mask = jnp.where(valid, 0.0, NEG_INF)
        rows = slice(qb * blk, (qb + 1) * blk)
        for kv in range(2):
            parts = []
            for pr in range(2):
                c0 = kv * 2 * LANES + pr * LANES
                qp = q_ref[rows, c0:c0 + LANES]
                parts += [jnp.where(lane < HEAD_DIM, qp, zero), jnp.where(lane >= HEAD_DIM, qp, zero)]
            qs = jnp.concatenate(parts, axis=0)
            cs = slice(kv * LANES, (kv + 1) * LANES)
            kb = jnp.concatenate([r[:, cs] for r in kblocks[qb:qb + 3]], axis=0)
            vb = jnp.concatenate([r[:, cs] for r in vblocks[qb:qb + 3]], axis=0)
            sink = jnp.concatenate([jnp.full((blk, 1), sink_ref[kv * 4 + g], F32) for g in range(4)], axis=0)
            o = _softmax_pv([(_nt(qs, kb), vb, lambda r0, rc, lanes, m=band_mask: m[r0 % blk:r0 % blk + rc, lanes], None),
                             (_nt(qs, kx_ref[:, cs]), vx_ref[:, cs], None, None)], extra=sink, rc=64)
            c0 = kv * 2 * LANES
            o_ref[rows, c0:c0 + LANES] = jnp.where(lane < HEAD_DIM, o[0:blk], o[blk:2 * blk]).astype(BF16)
            o_ref[rows, c0 + LANES:c0 + 2 * LANES] = jnp.where(
                lane < HEAD_DIM, o[2 * blk:3 * blk], o[3 * blk:4 * blk]).astype(BF16)


def _wa_attention(rw, qkv, sink):
    B, T, C = rw.B, rw.T, rw.C
    blk = WA_BLOCK
    nb = T // blk
    nq = WA_QBLOCKS
    assert nb % nq == 0
    ctxrow = (B * T) // C

    def kvspec(j, col):
        return pl.BlockSpec((blk, 2 * LANES), lambda b, s: (b * nb + jnp.clip(nq * s - 1 + j, 0, nb - 1), col))

    return pl.pallas_call(
        functools.partial(_wa_kernel, nb),
        grid=(B, nb // nq),
        in_specs=[pl.BlockSpec(memory_space=pltpu.SMEM),
                  pl.BlockSpec((nq * blk, 4 * LANES), lambda b, s: (b * (nb // nq) + s, 3))]
        + [kvspec(j, 8) for j in range(nq + 2)] + [kvspec(j, 9) for j in range(nq + 2)]
        + [pl.BlockSpec((C, 2 * LANES), lambda b, s: (ctxrow + b, 8)),
           pl.BlockSpec((C, 2 * LANES), lambda b, s: (ctxrow + b, 9))],
        out_specs=pl.BlockSpec((nq * blk, 4 * LANES), lambda b, s: (b * (nb // nq) + s, 0)),
        out_shape=jax.ShapeDtypeStruct((B * T, 4 * LANES), BF16),
        compiler_params=_cparams(2),
        name="window_attention",
    )(sink.astype(F32), qkv, *([qkv] * (2 * nq + 6)))


def _ctx_attn_kernel(sink_ref, t_ref, o_ref):
    C = t_ref.shape[0]
    lane = lax.broadcasted_iota(jnp.int32, (C, LANES), 1)
    zero = jnp.zeros((C, LANES), BF16)

    def pair(q2, k2, v2, sinks):
        out = jnp.zeros((C, LANES), F32)
        for hh in range(2):
            m = (lane < HEAD_DIM) if hh == 0 else (lane >= HEAD_DIM)
            extra = None if sinks is None else jnp.full((C, 1), sinks[hh], F32)
            o = _softmax_pv([(_nt(jnp.where(m, q2, zero), k2), v2, None, None)], extra=extra, rc=64)
            out = jnp.where(m, o, out)
        return out.astype(BF16)

    for p in range(4):
        c = p * LANES
        o_ref[:, c:c + LANES] = pair(t_ref[:, c:c + LANES], t_ref[:, 512 + c:640 + c], t_ref[:, 1024 + c:1152 + c], None)
    for kv in range(2):
        kd = t_ref[:, 2048 + kv * LANES:2176 + kv * LANES]
        vd = t_ref[:, 2304 + kv * LANES:2432 + kv * LANES]
        for pr in range(2):
            c = kv * 256 + pr * LANES
            h0 = kv * 4 + pr * 2
            o_ref[:, 512 + c:640 + c] = pair(t_ref[:, 1536 + c:1664 + c], kd, vd, (sink_ref[h0], sink_ref[h0 + 1]))


def _ctx_attention(rw, qkv, sink):
    B, T, C = rw.B, rw.T, rw.C
    ctxrow = (B * T) // C
    return pl.pallas_call(
        _ctx_attn_kernel,
        grid=(B,),
        in_specs=[pl.BlockSpec(memory_space=pltpu.SMEM),
                  pl.BlockSpec((C, qkv.shape[1]), lambda b: (ctxrow + b, 0))],
        out_specs=pl.BlockSpec((C, 8 * LANES), lambda b: (b, 0)),
        out_shape=jax.ShapeDtypeStruct((B * C, 8 * LANES), BF16),
        compiler_params=_cparams(1),
        name="context_attention",
    )(sink.astype(F32), qkv)


S5_Q = 16
CONV_TILE = 256
CONV_HALO = 16


def _ssm_inproj_kernel(x_ref, y1_ref, y2_ref, rt_ref, g2_ref, g_ref, sh_ref, sc_ref, w_ref,
                       xo_ref, z_ref, xbc_ref, u_ref, uj_ref, dt_ref, h_scr, u_scr):
    xn = _moe_residual(x_ref, y1_ref, y2_ref, rt_ref, g2_ref)
    xo_ref[...] = xn
    h_scr[...] = _ada_norm(xn, g_ref[...], sh_ref[...], sc_ref[...]).astype(BF16)

    def mm(c0, n):
        return jnp.dot(h_scr[...], w_ref[:, c0:c0 + n], preferred_element_type=F32)

    nz, nxbc, nu = z_ref.shape[1], xbc_ref.shape[1], u_ref.shape[1]
    for c0 in range(0, nz, MXU_W):
        z_ref[:, c0:c0 + MXU_W] = mm(c0, MXU_W).astype(BF16)
    for c0 in range(0, nxbc, MXU_W):
        xbc_ref[:, c0:c0 + MXU_W] = mm(nz + c0, MXU_W).astype(BF16)
    for c0 in range(0, nu, MXU_W):
        y = mm(nz + nxbc + c0, MXU_W)
        u_ref[:, c0:c0 + MXU_W] = y.astype(BF16)
        for t in range(MXU_W // LANES):
            u_scr[c0 // LANES + t] = y[:, t * LANES:(t + 1) * LANES]
    dt_ref[...] = mm(nz + nxbc + nu, LANES)
    nchunk = u_scr.shape[1] // S5_Q
    for j in range(S5_Q):
        for t in range(u_scr.shape[0]):
            uj_ref[j, :, t * LANES:(t + 1) * LANES] = u_scr[t, pl.ds(j, nchunk, stride=S5_Q), :].astype(BF16)


def _ssm_inproj(rw, xall, y1, y2, route, prev_mods, mods, norm_g, w_in):
    D, tm = rw.D, rw.tm
    w = jnp.concatenate([w_in[:, 0:2560], w_in[:, 2592:3104], w_in[:, 2560:2592], jnp.zeros((D, LANES - 32), F32)],
                        axis=1).astype(BF16)
    row = lambda i: (i, 0)
    return pl.pallas_call(
        _ssm_inproj_kernel,
        grid=(rw.ntot,),
        in_specs=[pl.BlockSpec((tm, D), row), pl.BlockSpec((tm, D), row), pl.BlockSpec((tm, D), row),
                  pl.BlockSpec((tm, LANES), row), _mod_spec(rw, 5), pl.BlockSpec((1, D), lambda i: (0, 0)),
                  _mod_spec(rw, 0), _mod_spec(rw, 1), pl.BlockSpec((D, 3200), lambda i: (0, 0))],
        out_specs=[pl.BlockSpec((tm, D), row),
                   pl.BlockSpec((tm, 1024), row), pl.BlockSpec((tm, 1536), row), pl.BlockSpec((tm, 512), row),
                   pl.BlockSpec((S5_Q, tm // S5_Q, 512), lambda i: (0, i, 0)), pl.BlockSpec((tm, LANES), row)],
        out_shape=[jax.ShapeDtypeStruct((rw.rows, D), F32),
                   jax.ShapeDtypeStruct((rw.rows, 1024), BF16), jax.ShapeDtypeStruct((rw.rows, 1536), BF16),
                   jax.ShapeDtypeStruct((rw.rows, 512), BF16),
                   jax.ShapeDtypeStruct((S5_Q, rw.rows // S5_Q, 512), BF16),
                   jax.ShapeDtypeStruct((rw.rows, LANES), F32)],
        scratch_shapes=[pltpu.VMEM((tm, D), BF16), pltpu.VMEM((512 // LANES, tm, LANES), F32)],
        compiler_params=_cparams(1),
        name="ssm_inproj",
    )(xall, y1, y2, route, prev_mods, norm_g[None, :], mods, mods, w)


def _softplus(x):
    return jnp.maximum(x, 0.0) + jnp.log(1.0 + jnp.exp(-jnp.abs(x)))


def _conv_kernel(lat_tiles, tpb, cpb, x_ref, pv_ref, nx_ref, w_ref, b_ref, dtr_ref, dtb_ref, act_ref, dt_ref):
    i = pl.program_id(0)
    is_lat = i < lat_tiles
    pos = jnp.where(is_lat, i % tpb, (i - lat_tiles) % cpb)
    last_pos = jnp.where(is_lat, tpb - 1, cpb - 1)
    x = x_ref[...].astype(F32)
    tc = x.shape[0]
    prev_row = jnp.where(pos == 0, 0.0, pv_ref[...].astype(F32)[CONV_HALO - 1:CONV_HALO, :])
    next_row = jnp.where(pos == last_pos, 0.0, nx_ref[...].astype(F32)[0:1, :])
    row = lax.broadcasted_iota(jnp.int32, x.shape, 0)
    xm1 = jnp.where(row == 0, prev_row, pltpu.roll(x, 1, 0))
    xp1 = jnp.where(row == tc - 1, next_row, pltpu.roll(x, tc - 1, 0))
    y = w_ref[0:1, :] * xm1 + w_ref[1:2, :] * x + w_ref[2:3, :] * xp1 + b_ref[...]
    act_ref[...] = _silu(y).astype(BF16)
    sp = _softplus(dtr_ref[...] + dtb_ref[...])
    dt_ref[0] = sp
    dt_ref[1] = pltpu.roll(sp, LANES - 16, 1)


def _ssm_conv(rw, xbc, dtr, conv_w, conv_b, dt_bias):
    B, T, C = rw.B, rw.T, rw.C
    tc = CONV_TILE
    assert T % tc == 0 and C % tc == 0
    lat_tiles, tpb, cpb = (B * T) // tc, T // tc, C // tc
    ntiles = rw.rows // tc
    hpt = tc // CONV_HALO
    nhalo = rw.rows // CONV_HALO
    W = xbc.shape[1]
    dtb = jnp.concatenate([dt_bias.reshape(-1), jnp.zeros((LANES - 32,), F32)])[None, :]
    row = lambda i: (i, 0)
    return pl.pallas_call(
        functools.partial(_conv_kernel, lat_tiles, tpb, cpb),
        grid=(ntiles,),
        in_specs=[pl.BlockSpec((tc, W), row),
                  pl.BlockSpec((CONV_HALO, W), lambda i: (jnp.maximum(i * hpt - 1, 0), 0)),
                  pl.BlockSpec((CONV_HALO, W), lambda i: (jnp.minimum((i + 1) * hpt, nhalo - 1), 0)),
                  pl.BlockSpec((3, W), lambda i: (0, 0)), pl.BlockSpec((1, W), lambda i: (0, 0)),
                  pl.BlockSpec((tc, LANES), row), pl.BlockSpec((1, LANES), lambda i: (0, 0))],
        out_specs=[pl.BlockSpec((tc, W), row), pl.BlockSpec((2, tc, LANES), lambda i: (0, i, 0))],
        out_shape=[jax.ShapeDtypeStruct((rw.rows, W), BF16), jax.ShapeDtypeStruct((2, rw.rows, LANES), F32)],
        compiler_params=_cparams(1),
        name="ssm_conv",
    )(xbc, xbc, xbc, conv_w, conv_b[None, :], dtr, dtb)


def _ssd_kernel(nb, *refs):
    acts, dts = refs[0:2 * nb], refs[2 * nb:4 * nb]
    tri_ref, a_ref, yf_ref, yb_ref, hst = refs[4 * nb:]

    @pl.when(pl.program_id(0) == 0)
    def _():
        hst[...] = jnp.zeros_like(hst)

    for d, y_ref in enumerate((yf_ref, yb_ref)):
        for b in range(nb):
            _ssd_chunk(acts[d * nb + b], dts[d * nb + b], tri_ref[d], a_ref[d], y_ref.at[b], hst.at[d, b])


def _ssd_chunk(act_ref, dt_ref, tri, avec, y_ref, hst):
    q = SSD_CHUNK
    dt = dt_ref[...]
    da = dt * avec
    acs = jnp.dot(tri, da, preferred_element_type=F32, precision=lax.Precision.HIGHEST)
    tot = jnp.sum(da, axis=0, keepdims=True)
    acs_t = acs.T
    dt_t = dt.T
    eacs = jnp.exp(acs)
    wend = jnp.exp(tot - acs) * dt
    etot = jnp.exp(tot)
    mask = tri > 0.5
    left = lax.broadcasted_iota(jnp.int32, (q, LANES), 1) < HEAD_DIM
    left1 = lax.broadcasted_iota(jnp.int32, (1, LANES), 1) < HEAD_DIM
    for g in range(2):
        bg = act_ref[:, 1024 + g * 128:1152 + g * 128]
        cg = act_ref[:, 1280 + g * 128:1408 + g * 128]
        cb = _nt(cg, bg)
        hin = hst[:, g * 512:(g + 1) * 512]
        yoff = jnp.dot(cg, hin.astype(BF16), preferred_element_type=F32)
        xw, dec = [], []
        for pr in range(4):
            h_a = g * 8 + pr * 2
            h_b = h_a + 1
            c0 = h_a * HEAD_DIM
            x2 = act_ref[:, c0:c0 + LANES]
            outs = []
            for h in (h_a, h_b):
                seg = acs[:, h:h + 1] - acs_t[h:h + 1, :]
                w = cb * jnp.exp(jnp.where(mask, seg, NEG_INF)) * dt_t[h:h + 1, :]
                outs.append(jnp.dot(w.astype(BF16), x2, preferred_element_type=F32))
            yd = jnp.where(left, outs[0], outs[1])
            sc = jnp.where(left, eacs[:, h_a:h_a + 1], eacs[:, h_b:h_b + 1])
            y_ref[:, c0:c0 + LANES] = (yd + yoff[:, pr * LANES:(pr + 1) * LANES] * sc).astype(BF16)
            wsc = jnp.where(left, wend[:, h_a:h_a + 1], wend[:, h_b:h_b + 1])
            xw.append((x2.astype(F32) * wsc).astype(BF16))
            dec.append(jnp.where(left1, etot[:, h_a:h_a + 1], etot[:, h_b:h_b + 1]))
        bg_t = bg.astype(F32).T.astype(BF16)
        snew = jnp.dot(bg_t, jnp.concatenate(xw, axis=1), preferred_element_type=F32)
        hst[:, g * 512:(g + 1) * 512] = hin * jnp.concatenate(dec, axis=1) + snew


def _ssd(rw, act, dt2, a_log):
    B, T, C = rw.B, rw.T, rw.C
    q = SSD_CHUNK
    nct, nlt = C // q, T // q
    ctx0 = (B * T) // q
    r = np.arange(q)
    tri = jnp.asarray(np.stack([r[None, :] <= r[:, None], r[None, :] >= r[:, None]]).astype(np.float32))
    avec = jnp.concatenate([-jnp.exp(a_log.astype(F32)), jnp.zeros((2, LANES - a_log.shape[1]), F32)], axis=1)[:, None, :]

    def lat(d, s):
        return jnp.clip(s - nct, 0, nlt - 1) if d == 0 else nlt - 1 - jnp.clip(s - nct, 0, nlt - 1)

    def blk(d, b, s):
        kc = s if d == 0 else nct - 1 - s
        return jnp.where(s < nct, ctx0 + b * nct + kc, b * nlt + lat(d, s))

    pairs = [(d, b) for d in range(2) for b in range(B)]
    aspec = lambda d, b: pl.BlockSpec((q, act.shape[1]), lambda s: (blk(d, b, s), 0))
    dspec = lambda d, b: pl.BlockSpec((None, q, LANES), lambda s: (d, blk(d, b, s), 0))
    yspec = lambda d: pl.BlockSpec((B, q, 1024), lambda s: (0, lat(d, s), 0))
    yf, yb = pl.pallas_call(
        functools.partial(_ssd_kernel, B),
        grid=(nct + nlt,),
        in_specs=[aspec(d, b) for d, b in pairs] + [dspec(d, b) for d, b in pairs]
        + [pl.BlockSpec((2, q, q), lambda s: (0, 0, 0)), pl.BlockSpec((2, 1, LANES), lambda s: (0, 0, 0))],
        out_specs=[yspec(0), yspec(1)],
        out_shape=[jax.ShapeDtypeStruct((B, T, 1024), BF16)] * 2,
        scratch_shapes=[pltpu.VMEM((2, B, q, 1024), F32)],
        compiler_params=_cparams(1),
        name="ssd_scan",
    )(*([act] * (2 * B)), *([dt2] * (2 * B)), tri, avec)
    return yf.reshape(B * T, 1024), yb.reshape(B * T, 1024)


def _cmul(ar, ai, br, bi):
    return ar * br - ai * bi, ar * bi + ai * br


def _s5_weight_kernel(lre_ref, lim_ref, ls_ref, bre_ref, bim_ref, cre_ref, cim_ref,
                      wsr_ref, wsi_ref, wor_ref, woi_ref, kt_ref, are_ref, aim_ref):
    lre, lim = lre_ref[...], lim_ref[...]
    step = jnp.exp(ls_ref[...])
    er, ei = lre * step, lim * step
    npow = 24
    p = lax.broadcasted_iota(jnp.int32, (1, npow, 1), 1).astype(F32)
    mag = jnp.exp(p * er)
    pre, pim = mag * jnp.cos(p * ei), mag * jnp.sin(p * ei)
    a_re, a_im = pre[:, 1:2, :], pim[:, 1:2, :]
    den = lre * lre + lim * lim
    q_re = ((a_re - 1.0) * lre + a_im * lim) / den
    q_im = (a_im * lre - (a_re - 1.0) * lim) / den
    bb_re, bb_im = _cmul(q_re, q_im, bre_ref[...], bim_ref[...])
    c_re, c_im = cre_ref[...], cim_ref[...]
    ws_r, ws_i, wo_r, wo_i, ca_r, ca_i = [], [], [], [], [], []
    for t in range(S5_Q):
        r, i = _cmul(bb_re, bb_im, pre[:, t:t + 1, :], pim[:, t:t + 1, :])
        ws_r.append(r)
        ws_i.append(i)
        r, i = _cmul(c_re, c_im, pre[:, t:t + 1, :], pim[:, t:t + 1, :])
        ca_r.append(r)
        ca_i.append(i)
        r, i = _cmul(c_re, c_im, pre[:, t + 1:t + 2, :], pim[:, t + 1:t + 2, :])
        wo_r.append(r)
        wo_i.append(-i)
    cat = lambda xs: jnp.concatenate(xs, axis=1)
    wsr_ref[...] = cat(ws_r)
    wsi_ref[...] = cat(ws_i)
    wor_ref[...] = cat(wo_r)
    woi_ref[...] = cat(wo_i)
    bdot = lambda a, b: lax.dot_general(a, b, (((2,), (2,)), ((0,), (0,))), preferred_element_type=F32,
                                        precision=lax.Precision.HIGHEST)
    kt_ref[...] = bdot(cat(ca_r), bb_re) - bdot(cat(ca_i), bb_im)
    are_ref[...] = pre[:, S5_Q:S5_Q + 1, :]
    aim_ref[...] = pim[:, S5_Q:S5_Q + 1, :]


def _s5_weights(lam_re, lam_im, log_step, b_re, b_im, c_re, c_im):
    nd, ng, ns = lam_re.shape
    G = nd * ng
    ch = S5_GROUP
    gb = 8
    qc = S5_Q * ch
    f = lambda a: a.astype(F32)
    args = (f(lam_re).reshape(G, 1, ns), f(lam_im).reshape(G, 1, ns), f(log_step).reshape(G, 1, 1),
            f(b_re).reshape(G, ns, ch).transpose(0, 2, 1), f(b_im).reshape(G, ns, ch).transpose(0, 2, 1),
            f(c_re).reshape(G, ch, ns), f(c_im).reshape(G, ch, ns))
    spec = lambda a: pl.BlockSpec((gb,) + a.shape[1:], lambda i: (i, 0, 0))
    oshape = [jax.ShapeDtypeStruct((G, qc, ns), F32)] * 4 + [jax.ShapeDtypeStruct((G, qc, ch), F32)] \
        + [jax.ShapeDtypeStruct((G, 1, ns), F32)] * 2
    wsr, wsi, wor, woi, kt, a_re, a_im = pl.pallas_call(
        _s5_weight_kernel,
        grid=(G // gb,),
        in_specs=[spec(a) for a in args],
        out_specs=[pl.BlockSpec((gb,) + s.shape[1:], lambda i: (i, 0, 0)) for s in oshape],
        out_shape=oshape,
        compiler_params=_cparams(1),
        name="s5_weights",
    )(*args)

    def by_dir(w, flip_dir):
        w = w.reshape(nd, ng, S5_Q, ch, ns)
        w = jnp.stack([jnp.flip(w[d], axis=1) if d == flip_dir else w[d] for d in range(nd)])
        return w.reshape(nd, ng, qc, ns)

    def pack(w):
        z = jnp.zeros_like(w)
        even = (np.arange(ng) % 2 == 0)[None, :, None, None]
        return jnp.where(even, jnp.concatenate([w, z], axis=-1), jnp.concatenate([z, w], axis=-1)).astype(BF16)

    ws_r, ws_i = pack(by_dir(wsr, 0)), pack(by_dir(wsi, 0))
    wo_r, wo_i = pack(by_dir(wor, 1)), pack(by_dir(woi, 1))
    k = kt.astype(BF16).reshape(nd, ng, S5_Q, ch, ch).transpose(0, 1, 4, 2, 3)
    kf = k[0].reshape(ng, ch, qc)
    kb = jnp.flip(k[1], axis=2).reshape(ng, ch, qc)
    zeros = jnp.zeros((ng, ch, qc), BF16)

    def toeplitz(padded, first):
        stride = 2 * qc - ch
        flat = jnp.tile(padded, (1, 1, S5_Q))
        rows = flat[:, :, first:first + S5_Q * stride].reshape(ng, ch, S5_Q, stride)[..., :qc]
        return rows.transpose(0, 2, 1, 3).reshape(ng, qc, qc)

    bt = jnp.stack([toeplitz(jnp.concatenate([zeros, kf], axis=-1), qc),
                    toeplitz(jnp.concatenate([kb, zeros], axis=-1), qc - ch)])
    pair = lambda a: a.reshape(nd, ng // 2, 1, 2 * ns)
    return bt, ws_r, ws_i, wo_r, wo_i, pair(a_re), pair(a_im)


S5_GB = LANES // S5_GROUP


def _s5_kernel(B, nct, nlt, uj_ref, perm_ref, bt_ref, wsr_ref, wsi_ref, wor_ref, woi_ref, are_ref, aim_ref, yj_ref,
               x_scr, y_scr, s_re, s_im):
    gb, npair, qc = S5_GB, S5_GB // 2, S5_Q * S5_GROUP
    lhs = jnp.concatenate([uj_ref[j] for j in range(S5_Q)], axis=1)
    for m in range(gb):
        x_scr[:, m * qc:(m + 1) * qc] = jnp.dot(lhs, perm_ref[:, m * qc:(m + 1) * qc],
                                                preferred_element_type=F32).astype(BF16)
    xg = lambda g: x_scr[:, g * qc:(g + 1) * qc]
    for d in range(2):
        for pr in range(npair):
            for dst, w_ref in ((s_re, wsr_ref), (s_im, wsi_ref)):
                dst[d, pr] = (jnp.dot(xg(2 * pr), w_ref[d, 2 * pr], preferred_element_type=F32)
                              + jnp.dot(xg(2 * pr + 1), w_ref[d, 2 * pr + 1], preferred_element_type=F32))
    chains = [(d, pr, b) for d in range(2) for pr in range(npair) for b in range(B)]
    coef = {(d, pr): (are_ref[d, pr], aim_ref[d, pr]) for d in range(2) for pr in range(npair)}
    ctx0 = B * nlt

    def body(s, carry):
        in_ctx = s < nct
        rows = {}
        for d in range(2):
            kc = s if d == 0 else nct - 1 - s
            kl = s - nct if d == 0 else nlt - 1 - (s - nct)
            for b in range(B):
                rows[(d, b)] = pl.ds(jnp.where(in_ctx, ctx0 + b * nct + kc, b * nlt + kl), 1)
        contrib = [(s_re[d, pr, rows[(d, b)], :], s_im[d, pr, rows[(d, b)], :]) for d, pr, b in chains]
        new = []
        for (d, pr, b), (hr, hi), (sr, si) in zip(chains, carry, contrib):
            ar, ai = coef[(d, pr)]
            s_re[d, pr, rows[(d, b)], :] = hr
            s_im[d, pr, rows[(d, b)], :] = hi
            new.append((ar * hr - ai * hi + sr, ar * hi + ai * hr + si))
        return tuple(new)

    zero = jnp.zeros((1, LANES), F32)
    lax.fori_loop(0, nct + nlt, body, tuple((zero, zero) for _ in chains))
    for g in range(gb):
        acc = None
        for d in range(2):
            t = (jnp.dot(xg(g), bt_ref[d, g], preferred_element_type=F32)
                 + _nt(s_re[d, g // 2].astype(BF16), wor_ref[d, g])
                 + _nt(s_im[d, g // 2].astype(BF16), woi_ref[d, g]))
            acc = t if acc is None else acc + t
        y_scr[:, g * qc:(g + 1) * qc] = acc.astype(BF16)
    for i in range(S5_Q):
        yj_ref[i] = _nt(y_scr[...], perm_ref[i * LANES:(i + 1) * LANES, :]).astype(BF16)


def _s5(rw, uj, weights):
    B, T, C = rw.B, rw.T, rw.C
    bt, ws_r, ws_i, wo_r, wo_i, a_re, a_im = weights
    ng = bt.shape[1]
    q, gb = S5_Q, S5_GB
    nct, nlt = C // q, T // q
    nrow = uj.shape[1]
    qc = q * S5_GROUP
    k = gb * qc
    idx = np.arange(k)
    j, m, c = idx // LANES, (idx % LANES) // S5_GROUP, idx % S5_GROUP
    perm = np.zeros((k, k), np.float32)
    perm[idx, m * qc + j * S5_GROUP + c] = 1.0
    once = dict(pipeline_mode=pl.Buffered(1))
    wspec = lambda n: pl.BlockSpec((2, gb, qc, n), lambda i: (0, i, 0, 0), **once)
    aspec = pl.BlockSpec((2, gb // 2, 1, LANES), lambda i: (0, i, 0, 0))
    return pl.pallas_call(
        functools.partial(_s5_kernel, B, nct, nlt),
        grid=(ng // gb,),
        in_specs=[pl.BlockSpec((q, nrow, LANES), lambda i: (0, 0, i), **once),
                  pl.BlockSpec((k, k), lambda i: (0, 0), **once),
                  wspec(qc), wspec(LANES), wspec(LANES), wspec(LANES), wspec(LANES), aspec, aspec],
        out_specs=pl.BlockSpec((q, nrow, LANES), lambda i: (0, 0, i)),
        out_shape=jax.ShapeDtypeStruct(uj.shape, BF16),
        scratch_shapes=[pltpu.VMEM((nrow, k), BF16), pltpu.VMEM((nrow, k), BF16),
                        pltpu.VMEM((2, gb // 2, nrow, LANES), F32), pltpu.VMEM((2, gb // 2, nrow, LANES), F32)],
        compiler_params=_cparams(1),
        name="s5_scan",
    )(uj, jnp.asarray(perm, BF16), bt, ws_r, ws_i, wo_r, wo_i, a_re, a_im)


def _gelu_tanh(x):
    return 0.5 * x * (1.0 + jnp.tanh(math.sqrt(2.0 / math.pi) * (x + 0.044715 * (x * x * x))))


def _ssm_outproj_kernel(y0_ref, y1_ref, xs_ref, z_ref, v_ref, u_ref, dsk_ref, nw_ref, s5d_ref, gw_ref, gb_ref,
                        x_ref, w_ref, g1_ref, gn_ref, sh2_ref, sc2_ref, wr_ref, br_ref, lt_ref,
                        xo_ref, h2_ref, rt_ref, cnt_ref, carry, v_scr):
    i = pl.program_id(0)
    y = y0_ref[...].astype(F32) + y1_ref[...].astype(F32) + dsk_ref[...] * xs_ref[...].astype(F32)
    y = _rms(y * _silu(z_ref[...].astype(F32))) * nw_ref[...]
    ntile = v_scr.shape[0]
    nchunk = v_scr.shape[1] // S5_Q
    for j in range(S5_Q):
        for t in range(ntile):
            v_scr[t, pl.ds(j, nchunk, stride=S5_Q), :] = v_ref[j, :, t * LANES:(t + 1) * LANES].astype(F32)
    s5_y = jnp.concatenate([v_scr[t] for t in range(ntile)], axis=1)
    v = _gelu_tanh(s5_y + s5d_ref[...] * u_ref[...].astype(F32))
    v = v * _sigmoid(jnp.dot(v.astype(BF16), gw_ref[...], preferred_element_type=F32) + gb_ref[...])
    mix = jnp.concatenate([y, v], axis=1).astype(BF16)
    yo = jnp.dot(mix, w_ref[...], preferred_element_type=F32)
    _post_mixer(i, x_ref[...], yo, g1_ref[...], gn_ref[...], sh2_ref[...], sc2_ref[...], wr_ref, br_ref, lt_ref,
                xo_ref, h2_ref, rt_ref, cnt_ref, carry)


def _ssm_outproj(rw, ssd_y, act, z, s5_y, u, d_skip, norm_w, s5_d, glu_w, glu_b, xall, w_out, mods, norm_ffn, wr, br):
    D, tm = rw.D, rw.tm
    ntiles = rw.nlat
    post_in, post_out = _post_specs(rw)
    row = lambda i: (i, 0)
    vec = lambda n: pl.BlockSpec((1, n), lambda i: (0, 0))
    dsk = jnp.repeat(d_skip.astype(F32), HEAD_DIM)[None, :]
    return pl.pallas_call(
        _ssm_outproj_kernel,
        grid=(ntiles,),
        in_specs=[pl.BlockSpec((tm, 1024), row), pl.BlockSpec((tm, 1024), row),
                  pl.BlockSpec((tm, 1024), row), pl.BlockSpec((tm, 1024), row),
                  pl.BlockSpec((S5_Q, tm // S5_Q, 512), lambda i: (0, i, 0)),
                  pl.BlockSpec((tm, 512), row), vec(1024), vec(1024), vec(512),
                  pl.BlockSpec((512, 512), lambda i: (0, 0)), vec(512),
                  pl.BlockSpec((tm, D), row), pl.BlockSpec((1536, D), lambda i: (0, 0)), _mod_spec(rw, 2)] + post_in,
        out_specs=post_out,
        out_shape=_post_shapes(ntiles * tm, D),
        scratch_shapes=[pltpu.VMEM((1, LANES), F32), pltpu.VMEM((512 // LANES, tm, LANES), F32)],
        compiler_params=_cparams(1),
        name="ssm_outproj_router",
    )(ssd_y[0], ssd_y[1], act, z, s5_y, u, dsk, norm_w[None, :], s5_d[None, :], glu_w.astype(BF16), glu_b[None, :],
      xall, w_out.astype(BF16), mods, norm_ffn[None, :], mods, mods, wr, br, _lower_tri(tm))


def kernel(x, c, ctx, c_ctx, mod_w, mod_b, norm_mix, norm_ffn, att_w_in, att_w_out, na_q_norm, na_k_norm, na_rel_bias, wa_q_norm, wa_k_norm, wa_sink, ssm_w_in, ssm_w_out, ssd_conv_w, ssd_conv_b, ssd_dt_bias, ssd_a_log, ssd_d, ssd_norm, s5_lambda_re, s5_lambda_im, s5_log_step, s5_b_re, s5_b_im, s5_c_re, s5_c_im, s5_d, s5_glu_w, s5_glu_b, moe_w_group, moe_b_group, moe_w_expert, moe_b_expert, moe_w13, moe_w2):
    B, T, D = x.shape
    C = ctx.shape[1]
    rw = _Rows(B, T, C, D, ROW_TILE)
    xl = x.reshape(B * T, D)
    xc = ctx.reshape(B * C, D)
    cm = jnp.concatenate([c, c_ctx[None, :], jnp.zeros((8 - B - 1, D), F32)], axis=0)
    mods = _modulation(cm, mod_w, mod_b)
    mods = mods.reshape(mods.shape[0], 8, 1, 6 * D)

    m0 = mods[0]
    qkv = _att_inproj(rw, xl, xc, m0, norm_mix[0], att_w_in[0], na_q_norm[0], na_k_norm[0], wa_q_norm[0],
                      wa_k_norm[0])
    na = _na_attention(rw, qkv, na_rel_bias[0])
    wa = _wa_attention(rw, qkv, wa_sink[0])
    cx = _ctx_attention(rw, qkv, wa_sink[0])
    wr, br = _router_weights(moe_w_group[0], moe_b_group[0], moe_w_expert[0], moe_b_expert[0])
    xall, h2, route, counts = _att_outproj(rw, na, wa, cx, xl, xc, att_w_out[0], m0, norm_ffn[0], wr, br)
    y1, y2 = _moe(h2, route, counts, moe_w13, moe_w2, 0)

    m1 = mods[1]
    xall, z, xbc, u, uj, dtr = _ssm_inproj(rw, xall, y1, y2, route, m0, m1, norm_mix[1], ssm_w_in[0])
    act, dt2 = _ssm_conv(rw, xbc, dtr, ssd_conv_w[0], ssd_conv_b[0], ssd_dt_bias[0])
    ssd_y = _ssd(rw, act, dt2, ssd_a_log[0])
    s5_w = _s5_weights(s5_lambda_re[0], s5_lambda_im[0], s5_log_step[0], s5_b_re[0], s5_b_im[0], s5_c_re[0],
                       s5_c_im[0])
    s5_y = _s5(rw, uj, s5_w)
    wr, br = _router_weights(moe_w_group[1], moe_b_group[1], moe_w_expert[1], moe_b_expert[1])
    xlat, h2, route, counts = _ssm_outproj(rw, ssd_y, act, z, s5_y, u, ssd_d[0], ssd_norm[0], s5_d[0], s5_glu_w[0],
                                           s5_glu_b[0], xall, ssm_w_out[0], m1, norm_ffn[1], wr, br)
    y1, y2 = _moe(h2, route, counts, moe_w13, moe_w2, 1)
    out = _combine(rw, rw.nlat, xlat, y1, y2, route, m1)
    return out.reshape(B, T, D)
```

```python
import functools
import math

import jax
import jax.numpy as jnp
import numpy as np
from jax import lax
from jax.experimental import pallas as pl
from jax.experimental.pallas import tpu as pltpu

F32 = jnp.float32
BF16 = jnp.bfloat16

EPS = 1e-6
NEG_INF = -1e30
GRID_W = 64
HEAD_DIM = 64
NA_KH = 8
NA_KW = 16
WA_BLOCK = 128
ROPE_BASE = 10000.0
SSD_CHUNK = 128
S5_GROUP = 16
S5_STATE = 64
MOE_GROUPS = 4
MOE_EPG = 8
MOE_EXPERTS = MOE_GROUPS * MOE_EPG

LANES = 128
ROW_TILE = 512
MXU_W = 256
MOE_TILE = MXU_W
VMEM_LIMIT = 56 * 1024 * 1024
MOE_VMEM_LIMIT = 60 * 1024 * 1024


def _cparams(n_axes, vmem=VMEM_LIMIT):
    return pltpu.CompilerParams(dimension_semantics=("arbitrary",) * n_axes, vmem_limit_bytes=vmem)


def _sigmoid(x):
    return 1.0 / (1.0 + jnp.exp(-x))


def _silu(x):
    return x * _sigmoid(x)


def _rms(x, eps=EPS):
    return x * lax.rsqrt(jnp.mean(x * x, axis=-1, keepdims=True) + eps)


def _ada_norm(x, g, shift, scale):
    return (_rms(x) * g) * (1.0 + scale) + shift


def _mod_kernel(c_ref, w_ref, b_ref, o_ref):
    a = _silu(c_ref[...])
    o_ref[...] = jnp.dot(a, w_ref[...], preferred_element_type=F32, precision=lax.Precision.HIGHEST) + b_ref[...]


def _modulation(cm, mod_w, mod_b):
    depth, d, n6 = mod_w.shape
    tn = 1024
    return pl.pallas_call(
        _mod_kernel,
        grid=(depth, n6 // tn),
        in_specs=[pl.BlockSpec((8, d), lambda l, j: (0, 0)),
                  pl.BlockSpec((None, d, tn), lambda l, j: (l, 0, j)),
                  pl.BlockSpec((None, 1, tn), lambda l, j: (l, 0, j))],
        out_specs=pl.BlockSpec((None, 8, tn), lambda l, j: (l, 0, j)),
        out_shape=jax.ShapeDtypeStruct((depth, 8, n6), F32),
        compiler_params=_cparams(2),
        name="modulation",
    )(cm, mod_w, mod_b.reshape(depth, 1, n6))


class _Rows:
    def __init__(self, B, T, C, D, tm):
        assert T % tm == 0 and (B * C) % tm == 0
        self.B, self.T, self.C, self.D, self.tm = B, T, C, D, tm
        self.tpb = T // tm
        self.nlat = B * self.tpb
        self.nctx = (B * C) // tm
        self.ntot = self.nlat + self.nctx
        self.rows = B * (T + C)

    def group(self, i):
        return jnp.where(i < self.nlat, i // self.tpb, self.B)


def _mod_spec(rw, col):
    return pl.BlockSpec((None, 1, rw.D), lambda i, *_: (rw.group(i), 0, col))


def _seg_norm(y, seg, gcol):
    ss = jnp.dot((y * y).astype(BF16), seg, preferred_element_type=F32)
    return y * lax.rsqrt(ss + EPS) * gcol


def _rope(y, cos, sin):
    w = y.shape[-1]
    lane = lax.broadcasted_iota(jnp.int32, y.shape, 1)
    first = (lane % 32) < 16
    partner = jnp.where(first, pltpu.roll(y, w - 16, 1), pltpu.roll(y, 16, 1))
    return y * cos + partner * sin


def _dup_halves(k):
    lane = lax.broadcasted_iota(jnp.int32, k.shape, 1)
    sw = pltpu.roll(k, 64, 1)
    return jnp.where(lane < 64, k, sw), jnp.where(lane < 64, sw, k)


def _att_inproj_kernel(nlat, xl_ref, xc_ref, g_ref, sh_ref, sc_ref, w_ref, gcol_ref, cos_ref, sin_ref, seg_ref,
                       o_ref, h_scr):
    i = pl.program_id(0)
    x = jnp.where(i < nlat, xl_ref[...], xc_ref[...])
    h_scr[...] = _ada_norm(x, g_ref[...], sh_ref[...], sc_ref[...]).astype(BF16)
    seg = seg_ref[...]
    cos2 = jnp.concatenate([cos_ref[...], cos_ref[...]], axis=1)
    sin2 = jnp.concatenate([sin_ref[...], sin_ref[...]], axis=1)
    for c in range(w_ref.shape[1] // MXU_W):
        c0 = c * MXU_W
        cols = slice(c0, c0 + MXU_W)
        y = jnp.dot(h_scr[...], w_ref[:, cols], preferred_element_type=F32)
        gcol = gcol_ref[:, cols]
        if c in (0, 1, 2, 3):
            o_ref[:, cols] = _seg_norm(y, seg, gcol).astype(BF16)
        elif c in (4, 5):
            o_ref[:, cols] = y.astype(BF16)
        elif c in (6, 7):
            o_ref[:, cols] = _rope(_seg_norm(y, seg, gcol), cos2, sin2).astype(BF16)
        else:
            lane = lax.broadcasted_iota(jnp.int32, y.shape, 1)
            yk = jnp.where(lane < LANES, _seg_norm(y, seg, gcol), y)
            yr = jnp.where(lane < LANES, _rope(yk, cos2, sin2), yk)
            k0, k1 = _dup_halves(yr[:, :LANES])
            v0, v1 = _dup_halves(yr[:, LANES:])
            for t, dup in enumerate((k0, k1, v0, v1)):
                o_ref[:, c0 + t * LANES:c0 + (t + 1) * LANES] = dup.astype(BF16)


def _rope_tables(T, tm):
    t = np.arange(T)
    d = np.arange(HEAD_DIM)
    nf = HEAD_DIM // 4
    inv = jnp.asarray(ROPE_BASE, F32) ** (-jnp.arange(nf, dtype=F32) / nf)
    pos = np.where((d // 32 == 0)[None, :], (t // GRID_W)[:, None], (t % GRID_W)[:, None])
    ang = jnp.asarray(pos, F32) * inv[d % nf][None, :]
    sign = np.where((d % 32) < 16, -1.0, 1.0).astype(np.float32)
    cos = jnp.cos(ang)
    sin = jnp.sin(ang) * sign[None, :]
    cos = jnp.concatenate([cos, jnp.ones((tm, HEAD_DIM), F32)], axis=0)
    sin = jnp.concatenate([sin, jnp.zeros((tm, HEAD_DIM), F32)], axis=0)
    return jnp.tile(cos, (1, 2)), jnp.tile(sin, (1, 2))


def _att_inproj(rw, xl, xc, mods, norm_g, w_in, na_qn, na_kn, wa_qn, wa_kn):
    D, tm = rw.D, rw.tm
    scale = HEAD_DIM ** -0.5
    gcol = jnp.concatenate([jnp.tile(na_qn * scale, 8), jnp.tile(na_kn, 8), jnp.ones((512,), F32),
                            jnp.tile(wa_qn * scale, 8), jnp.tile(wa_kn, 2), jnp.ones((128,), F32)])[None, :]
    cos, sin = _rope_tables(rw.T, tm)
    segn = np.arange(256) // 64
    seg = jnp.asarray((segn[:, None] == segn[None, :]).astype(np.float32) / 64.0, BF16)
    nlat, tpb = rw.nlat, rw.tpb
    return pl.pallas_call(
        functools.partial(_att_inproj_kernel, nlat),
        grid=(rw.ntot,),
        in_specs=[pl.BlockSpec((tm, D), lambda i: (jnp.minimum(i, nlat - 1), 0)),
                  pl.BlockSpec((tm, D), lambda i: (jnp.maximum(i - nlat, 0), 0)),
                  pl.BlockSpec((1, D), lambda i: (0, 0)),
                  _mod_spec(rw, 0), _mod_spec(rw, 1),
                  pl.BlockSpec((D, 2304), lambda i: (0, 0)),
                  pl.BlockSpec((1, 2304), lambda i: (0, 0)),
                  pl.BlockSpec((tm, 128), lambda i: (jnp.where(i < nlat, i % tpb, tpb), 0)),
                  pl.BlockSpec((tm, 128), lambda i: (jnp.where(i < nlat, i % tpb, tpb), 0)),
                  pl.BlockSpec((256, 256), lambda i: (0, 0))],
        out_specs=pl.BlockSpec((tm, 2560), lambda i: (i, 0)),
        out_shape=jax.ShapeDtypeStruct((rw.rows, 2560), BF16),
        scratch_shapes=[pltpu.VMEM((tm, D), BF16)],
        compiler_params=_cparams(1),
        name="att_inproj",
    )(xl, xc, norm_g[None, :], mods, mods, w_in.astype(BF16), gcol, cos, sin, seg)


def _route(lg, lt, carry):
    lane = lax.broadcasted_iota(jnp.int32, lg.shape, 1).astype(F32)
    gm = lane < MOE_GROUPS
    mg = jnp.max(jnp.where(gm, lg, NEG_INF), axis=-1, keepdims=True)
    eg = jnp.where(gm, jnp.exp(jnp.where(gm, lg, NEG_INF) - mg), 0.0)
    pg = eg / jnp.sum(eg, axis=-1, keepdims=True)
    ptop = jnp.max(pg, axis=-1, keepdims=True)
    gidx = jnp.min(jnp.where(gm & (pg == ptop), lane, 1e9), axis=-1, keepdims=True)
    lo = MOE_GROUPS + MOE_EPG * gidx
    em = (lane >= lo) & (lane < lo + MOE_EPG)
    le = jnp.where(em, lg, NEG_INF)
    ee = jnp.where(em, jnp.exp(le - jnp.max(le, axis=-1, keepdims=True)), 0.0)
    pe = ee / jnp.sum(ee, axis=-1, keepdims=True)
    v1 = jnp.max(jnp.where(em, pe, -1.0), axis=-1, keepdims=True)
    i1 = jnp.min(jnp.where(em & (pe == v1), lane, 1e9), axis=-1, keepdims=True)
    em2 = em & (lane != i1)
    v2 = jnp.max(jnp.where(em2, pe, -1.0), axis=-1, keepdims=True)
    i2 = jnp.min(jnp.where(em2 & (pe == v2), lane, 1e9), axis=-1, keepdims=True)
    den = v1 + v2
    w1 = v1 / den * ptop
    w2 = v2 / den * ptop
    e1 = i1 - MOE_GROUPS
    e2 = i2 - MOE_GROUPS
    m1 = lane == e1
    m2 = lane == e2
    oh = jnp.where(m1 | m2, 1.0, 0.0)
    cnt = jnp.dot(lt, oh.astype(BF16), preferred_element_type=F32) + carry
    r1 = jnp.sum(jnp.where(m1, cnt, 0.0), axis=-1, keepdims=True)
    r2 = jnp.sum(jnp.where(m2, cnt, 0.0), axis=-1, keepdims=True)
    route = jnp.where(lane == 0, e1, jnp.where(lane == 1, e2, jnp.where(lane == 2, w1, jnp.where(
        lane == 3, w2, jnp.where(lane == 4, r1, jnp.where(lane == 5, r2, 0.0))))))
    return route, carry + jnp.sum(oh, axis=0, keepdims=True)


def _post_mixer(i, x, y, g1, gn, sh2, sc2, wr_ref, br_ref, lt_ref, xo_ref, h2_ref, rt_ref, cnt_ref, carry):
    xn = x + g1 * y
    xo_ref[...] = xn
    h2 = _ada_norm(xn, gn, sh2, sc2)
    hb = h2.astype(BF16)
    hbf = hb.astype(F32)
    half = h2.shape[1] // 2
    h2_ref[...] = pltpu.pack_elementwise([h2[:, :half], h2[:, half:]], packed_dtype=BF16)
    hl = (h2 - hbf).astype(BF16)
    lg = (jnp.dot(hb, wr_ref[0], preferred_element_type=F32)
          + (jnp.dot(hb, wr_ref[1], preferred_element_type=F32) + jnp.dot(hl, wr_ref[0], preferred_element_type=F32))
          + br_ref[...])

    @pl.when(i == 0)
    def _():
        carry[...] = jnp.zeros_like(carry)

    route, newc = _route(lg, lt_ref[...], carry[...])
    rt_ref[...] = route
    carry[...] = newc
    cnt_ref[...] = newc


def _att_outproj_kernel(nlat, na_ref, wa_ref, cx_ref, xl_ref, xc_ref, w_ref, g1_ref, gn_ref, sh2_ref, sc2_ref,
                        wr_ref, br_ref, lt_ref, xo_ref, h2_ref, rt_ref, cnt_ref, carry):
    i = pl.program_id(0)
    lat = i < nlat
    mix = jnp.where(lat, jnp.concatenate([na_ref[...], wa_ref[...]], axis=1), cx_ref[...])
    y = jnp.dot(mix, w_ref[...], preferred_element_type=F32)
    x = jnp.where(lat, xl_ref[...], xc_ref[...])
    _post_mixer(i, x, y, g1_ref[...], gn_ref[...], sh2_ref[...], sc2_ref[...], wr_ref, br_ref, lt_ref,
                xo_ref, h2_ref, rt_ref, cnt_ref, carry)


def _router_weights(w_group, b_group, w_expert, b_expert):
    D = w_group.shape[0]
    pad = LANES - MOE_GROUPS - MOE_EXPERTS
    wr = jnp.concatenate([w_group, w_expert, jnp.zeros((D, pad), F32)], axis=1)
    br = jnp.concatenate([b_group, b_expert, jnp.zeros((pad,), F32)])[None, :]
    hi = wr.astype(BF16)
    lo = (wr - hi.astype(F32)).astype(BF16)
    return jnp.stack([hi, lo]), br


def _lower_tri(tm):
    r = np.arange(tm)
    return jnp.asarray((r[None, :] < r[:, None]).astype(np.float32), BF16)


def _post_specs(rw):
    D, tm = rw.D, rw.tm
    return ([pl.BlockSpec((1, D), lambda i: (0, 0)), _mod_spec(rw, 3), _mod_spec(rw, 4),
             pl.BlockSpec((2, D, LANES), lambda i: (0, 0, 0)), pl.BlockSpec((1, LANES), lambda i: (0, 0)),
             pl.BlockSpec((tm, tm), lambda i: (0, 0))],
            [pl.BlockSpec((tm, D), lambda i: (i, 0)), pl.BlockSpec((tm, D // 2), lambda i: (i, 0)),
             pl.BlockSpec((tm, LANES), lambda i: (i, 0)), pl.BlockSpec((1, LANES), lambda i: (0, 0))])


def _post_shapes(nrows, D):
    return [jax.ShapeDtypeStruct((nrows, D), F32), jax.ShapeDtypeStruct((nrows, D // 2), jnp.uint32),
            jax.ShapeDtypeStruct((nrows, LANES), F32), jax.ShapeDtypeStruct((1, LANES), F32)]


def _att_outproj(rw, na, wa, cx, xl, xc, w_out, mods, norm_ffn, wr, br):
    D, tm, nlat = rw.D, rw.tm, rw.nlat
    post_in, post_out = _post_specs(rw)
    latmap = lambda i: (jnp.minimum(i, nlat - 1), 0)
    ctxmap = lambda i: (jnp.maximum(i - nlat, 0), 0)
    return pl.pallas_call(
        functools.partial(_att_outproj_kernel, nlat),
        grid=(rw.ntot,),
        in_specs=[pl.BlockSpec((tm, 512), latmap), pl.BlockSpec((tm, 512), latmap), pl.BlockSpec((tm, D), ctxmap),
                  pl.BlockSpec((tm, D), latmap), pl.BlockSpec((tm, D), ctxmap),
                  pl.BlockSpec((D, D), lambda i: (0, 0)), _mod_spec(rw, 2)] + post_in,
        out_specs=post_out,
        out_shape=_post_shapes(rw.rows, D),
        scratch_shapes=[pltpu.VMEM((1, LANES), F32)],
        compiler_params=_cparams(1),
        name="att_outproj_router",
    )(na, wa, cx, xl, xc, w_out.astype(BF16), mods, norm_ffn[None, :], mods, mods, wr, br, _lower_tri(tm))


def _moe_kernel(te_ref, nu_ref, src_ref, nsrc_ref, hp_ref, w13_ref, w2_ref, o_ref, w13b, w2b, xa, xb):
    i = pl.program_id(0)
    prev = te_ref[jnp.maximum(i - 1, 0)]
    changed = (i == 0) | (te_ref[i] != prev)
    tg = xa.shape[0]

    @pl.when(changed)
    def _():
        w13b[...] = w13_ref[...].astype(BF16)
        w2b[...] = w2_ref[...].astype(BF16)

    @pl.when(i == 0)
    def _():
        def fetch(j, carry):
            xa[pl.ds(j, 1), :] = hp_ref[pl.ds(src_ref[0, j], 1), :]
            return carry

        lax.fori_loop(0, tg, fetch, 0, unroll=8)

    def step(cur, nxt):
        for j in range(tg):
            nxt[pl.ds(j, 1), :] = hp_ref[pl.ds(nsrc_ref[0, j], 1), :]
        ff = w2b.shape[0]
        half = cur.shape[1]
        w = cur[...]
        unpack = functools.partial(pltpu.unpack_elementwise, packed_dtype=BF16, unpacked_dtype=F32)
        x_lo = unpack(w, index=0).astype(BF16)
        x_hi = unpack(w, index=1).astype(BF16)
        a13 = (jnp.dot(x_lo, w13b[:half, :], preferred_element_type=F32)
               + jnp.dot(x_hi, w13b[half:, :], preferred_element_type=F32))
        act = _silu(a13[:, :ff]) * a13[:, ff:]
        o_ref[...] = jnp.dot(act.astype(BF16), w2b[...], preferred_element_type=F32).astype(BF16)

    used = i < nu_ref[0]

    @pl.when(used & (i % 2 == 0))
    def _():
        step(xa, xb)

    @pl.when(used & (i % 2 == 1))
    def _():
        step(xb, xa)

    @pl.when(i >= nu_ref[0])
    def _():
        o_ref[...] = jnp.zeros_like(o_ref)


def _moe(h2p, route, counts, w13, w2, layer):
    N = h2p.shape[0]
    D = 2 * h2p.shape[1]
    _, E, _, F2 = w13.shape
    tg = MOE_TILE
    nt = (2 * N) // tg + E
    e = route[:, 0:2].astype(jnp.int32)
    rank = route[:, 4:6].astype(jnp.int32)
    cnt = counts[0, :E].astype(jnp.int32)
    ntile_e = (cnt + tg - 1) // tg
    tile_end = jnp.cumsum(ntile_e)
    offs = (tile_end - ntile_e) * tg
    onehot = (e[:, :, None] == jnp.arange(E, dtype=jnp.int32)).astype(jnp.int32)
    dest = jnp.sum(onehot * offs, axis=-1) + rank
    src = jnp.zeros((nt * tg,), jnp.int32).at[dest.reshape(-1)].set(jnp.repeat(jnp.arange(N, dtype=jnp.int32), 2))
    tile_id = jnp.arange(nt, dtype=jnp.int32)
    nu = tile_end[-1:].astype(jnp.int32)
    te = jnp.sum((tile_end[None, :] <= jnp.minimum(tile_id, nu[0] - 1)[:, None]).astype(jnp.int32), axis=1)
    te = jnp.minimum(te, E - 1)
    ys = pl.pallas_call(
        _moe_kernel,
        grid_spec=pltpu.PrefetchScalarGridSpec(
            num_scalar_prefetch=2,
            grid=(nt,),
            in_specs=[pl.BlockSpec((None, 1, tg), lambda i, te, nu: (i, 0, 0), memory_space=pltpu.SMEM),
                      pl.BlockSpec((None, 1, tg), lambda i, te, nu: (jnp.minimum(i + 1, nt - 1), 0, 0),
                                   memory_space=pltpu.SMEM),
                      pl.BlockSpec((N, D // 2), lambda i, te, nu: (0, 0), pipeline_mode=pl.Buffered(1)),
                      pl.BlockSpec((None, None, D, F2), lambda i, te, nu: (layer, te[i], 0, 0)),
                      pl.BlockSpec((None, None, F2 // 2, D), lambda i, te, nu: (layer, te[i], 0, 0))],
            out_specs=pl.BlockSpec((tg, D), lambda i, te, nu: (i, 0)),
            scratch_shapes=[pltpu.VMEM((D, F2), BF16), pltpu.VMEM((F2 // 2, D), BF16),
                            pltpu.VMEM((tg, D // 2), jnp.uint32), pltpu.VMEM((tg, D // 2), jnp.uint32)]),
        out_shape=jax.ShapeDtypeStruct((nt * tg, D), BF16),
        compiler_params=_cparams(1, vmem=MOE_VMEM_LIMIT),
        name="moe_experts",
    )(te, nu, src.reshape(nt, 1, tg), src.reshape(nt, 1, tg), h2p, w13, w2)
    pick = lambda k: ys.at[dest[:, k]].get(mode="promise_in_bounds")
    return pick(0), pick(1)


def _moe_residual(x_ref, y1_ref, y2_ref, rt_ref, g2_ref):
    rt = rt_ref[...]
    f = rt[:, 2:3] * y1_ref[...].astype(F32) + rt[:, 3:4] * y2_ref[...].astype(F32)
    return x_ref[...] + g2_ref[...] * f


def _combine_kernel(x_ref, y1_ref, y2_ref, rt_ref, g2_ref, o_ref):
    o_ref[...] = _moe_residual(x_ref, y1_ref, y2_ref, rt_ref, g2_ref)


def _combine(rw, ntiles, xall, y1, y2, route, mods):
    D, tm = rw.D, rw.tm
    row = lambda i: (i, 0)
    return pl.pallas_call(
        _combine_kernel,
        grid=(ntiles,),
        in_specs=[pl.BlockSpec((tm, D), row), pl.BlockSpec((tm, D), row), pl.BlockSpec((tm, D), row),
                  pl.BlockSpec((tm, LANES), row), _mod_spec(rw, 5)],
        out_specs=pl.BlockSpec((tm, D), row),
        out_shape=jax.ShapeDtypeStruct((ntiles * tm, D), F32),
        compiler_params=_cparams(1),
        name="moe_combine",
    )(xall, y1, y2, route, mods)


NA_QROWS = 8
NA_KROWS = 16


def _na_first_key_row(variant, a):
    return (max(a - 4, 0) + 4, a, min(a, 4))[variant]


def _na_key_lanes(row0):
    a = row0 // GRID_W
    starts = [_na_first_key_row(v, a) for v in range(3)]
    lo = (min(starts) * GRID_W) // LANES * LANES
    hi = -(-((max(starts) + NA_KH) * GRID_W) // LANES) * LANES
    return lo, hi


def _na_bias_tiles(rpb):
    H = rpb.shape[0]
    i = np.arange(GRID_W)
    c0 = np.clip(i - NA_KW // 2, 0, GRID_W - NA_KW)
    j = np.arange(GRID_W)
    colvalid = (j[None, :] >= c0[:, None]) & (j[None, :] < c0[:, None] + NA_KW)
    dc = np.clip(j[None, :] - i[:, None] + NA_KW - 1, 0, 2 * NA_KW - 2)
    onehot = ((dc[None] == np.arange(2 * NA_KW - 1)[:, None, None]) & colvalid[None]).astype(np.float32)
    tiles = jnp.einsum('hrc,cij->hrij', rpb.astype(F32), jnp.asarray(onehot), precision=lax.Precision.HIGHEST)
    tiles = tiles + jnp.asarray(np.where(colvalid, 0.0, NEG_INF).astype(np.float32))
    return tiles.transpose(0, 2, 1, 3).reshape(H, GRID_W, (2 * NA_KH - 1) * GRID_W)


def _na_fill_bias(variant, tiles_ref, bias_scr):
    for hh in range(2):
        for a in range(NA_QROWS):
            start = _na_first_key_row(variant, a)
            dr0 = start - a + 3
            rows = slice(a * GRID_W, (a + 1) * GRID_W)
            w0, w1 = start * GRID_W, (start + NA_KH) * GRID_W
            if w0 > 0:
                bias_scr[hh, rows, 0:w0] = jnp.full((GRID_W, w0), NEG_INF, F32)
            bias_scr[hh, rows, w0:w1] = tiles_ref[hh, :, dr0 * GRID_W:(dr0 + NA_KH) * GRID_W]
            if w1 < NA_KROWS * GRID_W:
                bias_scr[hh, rows, w1:] = jnp.full((GRID_W, NA_KROWS * GRID_W - w1), NEG_INF, F32)


def _softmax_pv(parts, extra=None, rc=64):
    m_rows = parts[0][0].shape[0]
    probs = [[] for _ in parts]
    inv_l = []
    for r0 in range(0, m_rows, rc):
        sc = []
        for s, _, bias_fn, lanes_fn in parts:
            l0, l1 = (0, s.shape[1]) if lanes_fn is None else lanes_fn(r0)
            c = s[r0:r0 + rc, l0:l1]
            if bias_fn is not None:
                c = c + bias_fn(r0, rc, slice(l0, l1))
            sc.append((c, l0, s.shape[1] - l1))
        mx = functools.reduce(jnp.maximum, [jnp.max(c, axis=-1, keepdims=True) for c, _, _ in sc])
        if extra is not None:
            mx = jnp.maximum(mx, extra[r0:r0 + rc])
        l = jnp.zeros_like(mx) if extra is None else jnp.exp(extra[r0:r0 + rc] - mx)
        for k, (c, before, after) in enumerate(sc):
            p = jnp.exp(c - mx)
            l = l + jnp.sum(p, axis=-1, keepdims=True)
            row = [jnp.zeros((rc, before), BF16)] * (before > 0) + [p.astype(BF16)] \
                + [jnp.zeros((rc, after), BF16)] * (after > 0)
            probs[k].append(row[0] if len(row) == 1 else jnp.concatenate(row, axis=1))
        inv_l.append(1.0 / l)
    o = None
    for k, (_, v, _, _) in enumerate(parts):
        pv = jnp.dot(jnp.concatenate(probs[k], axis=0), v, preferred_element_type=F32)
        o = pv if o is None else o + pv
    return o * jnp.concatenate(inv_l, axis=0)


def _nt(a, b):
    return lax.dot_general(a, b, (((1,), (1,)), ((), ())), preferred_element_type=F32)


NA_REFS_PER_BATCH = 11


def _na_kernel(n_rb, nb, *refs):
    tiles_ref, o_ref, bias_ref = refs[nb * NA_REFS_PER_BATCH:]
    rb = pl.program_id(1)
    for variant, at_rb in ((0, 0), (1, 1), (2, n_rb - 1)):
        @pl.when(rb == at_rb)
        def _(variant=variant):
            _na_fill_bias(variant, tiles_ref, bias_ref)

    for b in range(nb):
        q_ref, k0, k1, k2, k3, v0, v1, v2, v3, kc_ref, vc_ref = refs[b * NA_REFS_PER_BATCH:(b + 1) * NA_REFS_PER_BATCH]
        q2 = q_ref[...]
        kw = jnp.concatenate([k0[...], k1[...], k2[...], k3[...]], axis=0)
        vw = jnp.concatenate([v0[...], v1[...], v2[...], v3[...]], axis=0)
        kc = kc_ref[...]
        vc = vc_ref[...]
        lane = lax.broadcasted_iota(jnp.int32, q2.shape, 1)
        out = jnp.zeros(q2.shape, F32)
        for hh in range(2):
            m = (lane < HEAD_DIM) if hh == 0 else (lane >= HEAD_DIM)
            qm = jnp.where(m, q2, jnp.zeros_like(q2))
            o = _softmax_pv([(_nt(qm, kw), vw, lambda r0, rc, lanes, hh=hh: bias_ref[hh, r0:r0 + rc, lanes],
                              _na_key_lanes), (_nt(qm, kc), vc, None, None)], rc=32)
            out = jnp.where(m, o, out)
        o_ref[b] = out.astype(BF16)


def _na_attention(rw, qkv, rpb):
    B, T, C = rw.B, rw.T, rw.C
    tq = NA_QROWS * GRID_W
    tk = tq // 2
    n_rb = T // tq
    nkb = T // tk
    assert T % tq == 0 and n_rb >= 2 and (B * T) % C == 0
    tiles = _na_bias_tiles(rpb)
    ctxrow = (B * T) // C

    def batch_specs(b):
        kv = lambda j, col: pl.BlockSpec(
            (tk, LANES), lambda p, rb: (b * nkb + jnp.clip(2 * rb - 1 + j, 0, nkb - 1), col + p))
        return ([pl.BlockSpec((tq, LANES), lambda p, rb: (b * n_rb + rb, p))]
                + [kv(j, 4) for j in range(4)] + [kv(j, 8) for j in range(4)]
                + [pl.BlockSpec((C, LANES), lambda p, rb: (ctxrow + b, 4 + p)),
                   pl.BlockSpec((C, LANES), lambda p, rb: (ctxrow + b, 8 + p))])

    out = pl.pallas_call(
        functools.partial(_na_kernel, n_rb, B),
        grid=(4, n_rb),
        in_specs=sum([batch_specs(b) for b in range(B)], [])
        + [pl.BlockSpec((2,) + tiles.shape[1:], lambda p, rb: (p, 0, 0))],
        out_specs=pl.BlockSpec((B, tq, LANES), lambda p, rb: (0, rb, p)),
        out_shape=jax.ShapeDtypeStruct((B, T, 4 * LANES), BF16),
        scratch_shapes=[pltpu.VMEM((2, tq, 2 * tq), F32)],
        compiler_params=_cparams(2),
        name="neighbourhood_attention",
    )(*([qkv] * (B * NA_REFS_PER_BATCH)), tiles)
    return out.reshape(B * T, 4 * LANES)


WA_QBLOCKS = 2


def _wa_kernel(nb, sink_ref, q_ref, *refs):
    step = pl.program_id(1)
    blk = WA_BLOCK
    nkb = WA_QBLOCKS + 2
    kblocks, vblocks = refs[:nkb], refs[nkb:2 * nkb]
    kx_ref, vx_ref, o_ref = refs[2 * nkb:]
    lane = lax.broadcasted_iota(jnp.int32, (blk, LANES), 1)
    zero = jnp.zeros((blk, LANES), BF16)
    qi = lax.broadcasted_iota(jnp.int32, (blk, 3 * blk), 0)
    ks = lax.broadcasted_iota(jnp.int32, (blk, 3 * blk), 1)
    for qb in range(WA_QBLOCKS):
        n = step * WA_QBLOCKS + qb
        lo = jnp.where(n > 0, 0, blk)
        hi = jnp.where(n < nb - 1, 3 * blk, 2 * blk)
        valid = (ks >= qi) & (ks <= qi + 2 * blk) & (ks >= lo) & (ks < hi)
        band_mask = jnp.where(valid, 0.0, NEG_INF)
        rows = slice(qb * blk, (qb + 1) * blk)
        for kv in range(2):
            parts = []
            for pr in range(2):
                c0 = kv * 2 * LANES + pr * LANES
                qp = q_ref[rows, c0:c0 + LANES]
                parts += [jnp.where(lane < HEAD_DIM, qp, zero), jnp.where(lane >= HEAD_DIM, qp, zero)]
            qs = jnp.concatenate(parts, axis=0)
            cs = slice(kv * LANES, (kv + 1) * LANES)
            kb = jnp.concatenate([r[:, cs] for r in kblocks[qb:qb + 3]], axis=0)
            vb = jnp.concatenate([r[:, cs] for r in vblocks[qb:qb + 3]], axis=0)
            sink = jnp.concatenate([jnp.full((blk, 1), sink_ref[kv * 4 + g], F32) for g in range(4)], axis=0)
            o = _softmax_pv([(_nt(qs, kb), vb, lambda r0, rc, lanes, m=band_mask: m[r0 % blk:r0 % blk + rc, lanes], None),
                             (_nt(qs, kx_ref[:, cs]), vx_ref[:, cs], None, None)], extra=sink, rc=64)
            c0 = kv * 2 * LANES
            o_ref[rows, c0:c0 + LANES] = jnp.where(lane < HEAD_DIM, o[0:blk], o[blk:2 * blk]).astype(BF16)
            o_ref[rows, c0 + LANES:c0 + 2 * LANES] = jnp.where(
                lane < HEAD_DIM, o[2 * blk:3 * blk], o[3 * blk:4 * blk]).astype(BF16)


def _wa_attention(rw, qkv, sink):
    B, T, C = rw.B, rw.T, rw.C
    blk = WA_BLOCK
    nb = T // blk
    nq = WA_QBLOCKS
    assert nb % nq == 0
    ctxrow = (B * T) // C

    def kvspec(j, col):
        return pl.BlockSpec((blk, 2 * LANES), lambda b, s: (b * nb + jnp.clip(nq * s - 1 + j, 0, nb - 1), col))

    return pl.pallas_call(
        functools.partial(_wa_kernel, nb),
        grid=(B, nb // nq),
        in_specs=[pl.BlockSpec(memory_space=pltpu.SMEM),
                  pl.BlockSpec((nq * blk, 4 * LANES), lambda b, s: (b * (nb // nq) + s, 3))]
        + [kvspec(j, 8) for j in range(nq + 2)] + [kvspec(j, 9) for j in range(nq + 2)]
        + [pl.BlockSpec((C, 2 * LANES), lambda b, s: (ctxrow + b, 8)),
           pl.BlockSpec((C, 2 * LANES), lambda b, s: (ctxrow + b, 9))],
        out_specs=pl.BlockSpec((nq * blk, 4 * LANES), lambda b, s: (b * (nb // nq) + s, 0)),
        out_shape=jax.ShapeDtypeStruct((B * T, 4 * LANES), BF16),
        compiler_params=_cparams(2),
        name="window_attention",
    )(sink.astype(F32), qkv, *([qkv] * (2 * nq + 6)))


def _ctx_attn_kernel(sink_ref, t_ref, o_ref):
    C = t_ref.shape[0]
    lane = lax.broadcasted_iota(jnp.int32, (C, LANES), 1)
    zero = jnp.zeros((C, LANES), BF16)

    def pair(q2, k2, v2, sinks):
        out = jnp.zeros((C, LANES), F32)
        for hh in range(2):
            m = (lane < HEAD_DIM) if hh == 0 else (lane >= HEAD_DIM)
            extra = None if sinks is None else jnp.full((C, 1), sinks[hh], F32)
            o = _softmax_pv([(_nt(jnp.where(m, q2, zero), k2), v2, None, None)], extra=extra, rc=64)
            out = jnp.where(m, o, out)
        return out.astype(BF16)

    for p in range(4):
        c = p * LANES
        o_ref[:, c:c + LANES] = pair(t_ref[:, c:c + LANES], t_ref[:, 512 + c:640 + c], t_ref[:, 1024 + c:1152 + c], None)
    for kv in range(2):
        kd = t_ref[:, 2048 + kv * LANES:2176 + kv * LANES]
        vd = t_ref[:, 2304 + kv * LANES:2432 + kv * LANES]
        for pr in range(2):
            c = kv * 256 + pr * LANES
            h0 = kv * 4 + pr * 2
            o_ref[:, 512 + c:640 + c] = pair(t_ref[:, 1536 + c:1664 + c], kd, vd, (sink_ref[h0], sink_ref[h0 + 1]))


def _ctx_attention(rw, qkv, sink):
    B, T, C = rw.B, rw.T, rw.C
    ctxrow = (B * T) // C
    return pl.pallas_call(
        _ctx_attn_kernel,
        grid=(B,),
        in_specs=[pl.BlockSpec(memory_space=pltpu.SMEM),
                  pl.BlockSpec((C, qkv.shape[1]), lambda b: (ctxrow + b, 0))],
        out_specs=pl.BlockSpec((C, 8 * LANES), lambda b: (b, 0)),
        out_shape=jax.ShapeDtypeStruct((B * C, 8 * LANES), BF16),
        compiler_params=_cparams(1),
        name="context_attention",
    )(sink.astype(F32), qkv)


S5_Q = 16
CONV_TILE = 256
CONV_HALO = 16


def _ssm_inproj_kernel(x_ref, y1_ref, y2_ref, rt_ref, g2_ref, g_ref, sh_ref, sc_ref, w_ref,
                       xo_ref, z_ref, xbc_ref, u_ref, uj_ref, dt_ref, h_scr, u_scr):
    xn = _moe_residual(x_ref, y1_ref, y2_ref, rt_ref, g2_ref)
    xo_ref[...] = xn
    h_scr[...] = _ada_norm(xn, g_ref[...], sh_ref[...], sc_ref[...]).astype(BF16)

    def mm(c0, n):
        return jnp.dot(h_scr[...], w_ref[:, c0:c0 + n], preferred_element_type=F32)

    nz, nxbc, nu = z_ref.shape[1], xbc_ref.shape[1], u_ref.shape[1]
    for c0 in range(0, nz, MXU_W):
        z_ref[:, c0:c0 + MXU_W] = mm(c0, MXU_W).astype(BF16)
    for c0 in range(0, nxbc, MXU_W):
        xbc_ref[:, c0:c0 + MXU_W] = mm(nz + c0, MXU_W).astype(BF16)
    for c0 in range(0, nu, MXU_W):
        y = mm(nz + nxbc + c0, MXU_W)
        u_ref[:, c0:c0 + MXU_W] = y.astype(BF16)
        for t in range(MXU_W // LANES):
            u_scr[c0 // LANES + t] = y[:, t * LANES:(t + 1) * LANES]
    dt_ref[...] = mm(nz + nxbc + nu, LANES)
    nchunk = u_scr.shape[1] // S5_Q
    for j in range(S5_Q):
        for t in range(u_scr.shape[0]):
            uj_ref[j, :, t * LANES:(t + 1) * LANES] = u_scr[t, pl.ds(j, nchunk, stride=S5_Q), :].astype(BF16)


def _ssm_inproj(rw, xall, y1, y2, route, prev_mods, mods, norm_g, w_in):
    D, tm = rw.D, rw.tm
    w = jnp.concatenate([w_in[:, 0:2560], w_in[:, 2592:3104], w_in[:, 2560:2592], jnp.zeros((D, LANES - 32), F32)],
                        axis=1).astype(BF16)
    row = lambda i: (i, 0)
    return pl.pallas_call(
        _ssm_inproj_kernel,
        grid=(rw.ntot,),
        in_specs=[pl.BlockSpec((tm, D), row), pl.BlockSpec((tm, D), row), pl.BlockSpec((tm, D), row),
                  pl.BlockSpec((tm, LANES), row), _mod_spec(rw, 5), pl.BlockSpec((1, D), lambda i: (0, 0)),
                  _mod_spec(rw, 0), _mod_spec(rw, 1), pl.BlockSpec((D, 3200), lambda i: (0, 0))],
        out_specs=[pl.BlockSpec((tm, D), row),
                   pl.BlockSpec((tm, 1024), row), pl.BlockSpec((tm, 1536), row), pl.BlockSpec((tm, 512), row),
                   pl.BlockSpec((S5_Q, tm // S5_Q, 512), lambda i: (0, i, 0)), pl.BlockSpec((tm, LANES), row)],
        out_shape=[jax.ShapeDtypeStruct((rw.rows, D), F32),
                   jax.ShapeDtypeStruct((rw.rows, 1024), BF16), jax.ShapeDtypeStruct((rw.rows, 1536), BF16),
                   jax.ShapeDtypeStruct((rw.rows, 512), BF16),
                   jax.ShapeDtypeStruct((S5_Q, rw.rows // S5_Q, 512), BF16),
                   jax.ShapeDtypeStruct((rw.rows, LANES), F32)],
        scratch_shapes=[pltpu.VMEM((tm, D), BF16), pltpu.VMEM((512 // LANES, tm, LANES), F32)],
        compiler_params=_cparams(1),
        name="ssm_inproj",
    )(xall, y1, y2, route, prev_mods, norm_g[None, :], mods, mods, w)


def _softplus(x):
    return jnp.maximum(x, 0.0) + jnp.log(1.0 + jnp.exp(-jnp.abs(x)))


def _conv_kernel(lat_tiles, tpb, cpb, x_ref, pv_ref, nx_ref, w_ref, b_ref, dtr_ref, dtb_ref, act_ref, dt_ref):
    i = pl.program_id(0)
    is_lat = i < lat_tiles
    pos = jnp.where(is_lat, i % tpb, (i - lat_tiles) % cpb)
    last_pos = jnp.where(is_lat, tpb - 1, cpb - 1)
    x = x_ref[...].astype(F32)
    tc = x.shape[0]
    prev_row = jnp.where(pos == 0, 0.0, pv_ref[...].astype(F32)[CONV_HALO - 1:CONV_HALO, :])
    next_row = jnp.where(pos == last_pos, 0.0, nx_ref[...].astype(F32)[0:1, :])
    row = lax.broadcasted_iota(jnp.int32, x.shape, 0)
    xm1 = jnp.where(row == 0, prev_row, pltpu.roll(x, 1, 0))
    xp1 = jnp.where(row == tc - 1, next_row, pltpu.roll(x, tc - 1, 0))
    y = w_ref[0:1, :] * xm1 + w_ref[1:2, :] * x + w_ref[2:3, :] * xp1 + b_ref[...]
    act_ref[...] = _silu(y).astype(BF16)
    sp = _softplus(dtr_ref[...] + dtb_ref[...])
    dt_ref[0] = sp
    dt_ref[1] = pltpu.roll(sp, LANES - 16, 1)


def _ssm_conv(rw, xbc, dtr, conv_w, conv_b, dt_bias):
    B, T, C = rw.B, rw.T, rw.C
    tc = CONV_TILE
    assert T % tc == 0 and C % tc == 0
    lat_tiles, tpb, cpb = (B * T) // tc, T // tc, C // tc
    ntiles = rw.rows // tc
    hpt = tc // CONV_HALO
    nhalo = rw.rows // CONV_HALO
    W = xbc.shape[1]
    dtb = jnp.concatenate([dt_bias.reshape(-1), jnp.zeros((LANES - 32,), F32)])[None, :]
    row = lambda i: (i, 0)
    return pl.pallas_call(
        functools.partial(_conv_kernel, lat_tiles, tpb, cpb),
        grid=(ntiles,),
        in_specs=[pl.BlockSpec((tc, W), row),
                  pl.BlockSpec((CONV_HALO, W), lambda i: (jnp.maximum(i * hpt - 1, 0), 0)),
                  pl.BlockSpec((CONV_HALO, W), lambda i: (jnp.minimum((i + 1) * hpt, nhalo - 1), 0)),
                  pl.BlockSpec((3, W), lambda i: (0, 0)), pl.BlockSpec((1, W), lambda i: (0, 0)),
                  pl.BlockSpec((tc, LANES), row), pl.BlockSpec((1, LANES), lambda i: (0, 0))],
        out_specs=[pl.BlockSpec((tc, W), row), pl.BlockSpec((2, tc, LANES), lambda i: (0, i, 0))],
        out_shape=[jax.ShapeDtypeStruct((rw.rows, W), BF16), jax.ShapeDtypeStruct((2, rw.rows, LANES), F32)],
        compiler_params=_cparams(1),
        name="ssm_conv",
    )(xbc, xbc, xbc, conv_w, conv_b[None, :], dtr, dtb)


def _ssd_kernel(nb, *refs):
    acts, dts = refs[0:2 * nb], refs[2 * nb:4 * nb]
    tri_ref, a_ref, yf_ref, yb_ref, hst = refs[4 * nb:]

    @pl.when(pl.program_id(0) == 0)
    def _():
        hst[...] = jnp.zeros_like(hst)

    for d, y_ref in enumerate((yf_ref, yb_ref)):
        for b in range(nb):
            _ssd_chunk(acts[d * nb + b], dts[d * nb + b], tri_ref[d], a_ref[d], y_ref.at[b], hst.at[d, b])


def _ssd_chunk(act_ref, dt_ref, tri, avec, y_ref, hst):
    q = SSD_CHUNK
    dt = dt_ref[...]
    da = dt * avec
    acs = jnp.dot(tri, da, preferred_element_type=F32, precision=lax.Precision.HIGHEST)
    tot = jnp.sum(da, axis=0, keepdims=True)
    acs_t = acs.T
    dt_t = dt.T
    eacs = jnp.exp(acs)
    wend = jnp.exp(tot - acs) * dt
    etot = jnp.exp(tot)
    mask = tri > 0.5
    left = lax.broadcasted_iota(jnp.int32, (q, LANES), 1) < HEAD_DIM
    left1 = lax.broadcasted_iota(jnp.int32, (1, LANES), 1) < HEAD_DIM
    for g in range(2):
        bg = act_ref[:, 1024 + g * 128:1152 + g * 128]
        cg = act_ref[:, 1280 + g * 128:1408 + g * 128]
        cb = _nt(cg, bg)
        hin = hst[:, g * 512:(g + 1) * 512]
        yoff = jnp.dot(cg, hin.astype(BF16), preferred_element_type=F32)
        xw, dec = [], []
        for pr in range(4):
            h_a = g * 8 + pr * 2
            h_b = h_a + 1
            c0 = h_a * HEAD_DIM
            x2 = act_ref[:, c0:c0 + LANES]
            outs = []
            for h in (h_a, h_b):
                seg = acs[:, h:h + 1] - acs_t[h:h + 1, :]
                w = cb * jnp.exp(jnp.where(mask, seg, NEG_INF)) * dt_t[h:h + 1, :]
                outs.append(jnp.dot(w.astype(BF16), x2, preferred_element_type=F32))
            yd = jnp.where(left, outs[0], outs[1])
            sc = jnp.where(left, eacs[:, h_a:h_a + 1], eacs[:, h_b:h_b + 1])
            y_ref[:, c0:c0 + LANES] = (yd + yoff[:, pr * LANES:(pr + 1) * LANES] * sc).astype(BF16)
            wsc = jnp.where(left, wend[:, h_a:h_a + 1], wend[:, h_b:h_b + 1])
            xw.append((x2.astype(F32) * wsc).astype(BF16))
            dec.append(jnp.where(left1, etot[:, h_a:h_a + 1], etot[:, h_b:h_b + 1]))
        bg_t = bg.astype(F32).T.astype(BF16)
        snew = jnp.dot(bg_t, jnp.concatenate(xw, axis=1), preferred_element_type=F32)
        hst[:, g * 512:(g + 1) * 512] = hin * jnp.concatenate(dec, axis=1) + snew


def _ssd(rw, act, dt2, a_log):
    B, T, C = rw.B, rw.T, rw.C
    q = SSD_CHUNK
    nct, nlt = C // q, T // q
    ctx0 = (B * T) // q
    r = np.arange(q)
    tri = jnp.asarray(np.stack([r[None, :] <= r[:, None], r[None, :] >= r[:, None]]).astype(np.float32))
    avec = jnp.concatenate([-jnp.exp(a_log.astype(F32)), jnp.zeros((2, LANES - a_log.shape[1]), F32)], axis=1)[:, None, :]

    def lat(d, s):
        return jnp.clip(s - nct, 0, nlt - 1) if d == 0 else nlt - 1 - jnp.clip(s - nct, 0, nlt - 1)

    def blk(d, b, s):
        kc = s if d == 0 else nct - 1 - s
        return jnp.where(s < nct, ctx0 + b * nct + kc, b * nlt + lat(d, s))

    pairs = [(d, b) for d in range(2) for b in range(B)]
    aspec = lambda d, b: pl.BlockSpec((q, act.shape[1]), lambda s: (blk(d, b, s), 0))
    dspec = lambda d, b: pl.BlockSpec((None, q, LANES), lambda s: (d, blk(d, b, s), 0))
    yspec = lambda d: pl.BlockSpec((B, q, 1024), lambda s: (0, lat(d, s), 0))
    yf, yb = pl.pallas_call(
        functools.partial(_ssd_kernel, B),
        grid=(nct + nlt,),
        in_specs=[aspec(d, b) for d, b in pairs] + [dspec(d, b) for d, b in pairs]
        + [pl.BlockSpec((2, q, q), lambda s: (0, 0, 0)), pl.BlockSpec((2, 1, LANES), lambda s: (0, 0, 0))],
        out_specs=[yspec(0), yspec(1)],
        out_shape=[jax.ShapeDtypeStruct((B, T, 1024), BF16)] * 2,
        scratch_shapes=[pltpu.VMEM((2, B, q, 1024), F32)],
        compiler_params=_cparams(1),
        name="ssd_scan",
    )(*([act] * (2 * B)), *([dt2] * (2 * B)), tri, avec)
    return yf.reshape(B * T, 1024), yb.reshape(B * T, 1024)


def _cmul(ar, ai, br, bi):
    return ar * br - ai * bi, ar * bi + ai * br


def _s5_weight_kernel(lre_ref, lim_ref, ls_ref, bre_ref, bim_ref, cre_ref, cim_ref,
                      wsr_ref, wsi_ref, wor_ref, woi_ref, bt_ref, are_ref, aim_ref):
    lre, lim = lre_ref[...], lim_ref[...]
    step = jnp.exp(ls_ref[...])
    er, ei = lre * step, lim * step
    npow = 24
    p = lax.broadcasted_iota(jnp.int32, (1, npow, 1), 1).astype(F32)
    mag = jnp.exp(p * er)
    pre, pim = mag * jnp.cos(p * ei), mag * jnp.sin(p * ei)
    a_re, a_im = pre[:, 1:2, :], pim[:, 1:2, :]
    den = lre * lre + lim * lim
    q_re = ((a_re - 1.0) * lre + a_im * lim) / den
    q_im = (a_im * lre - (a_re - 1.0) * lim) / den
    bb_re, bb_im = _cmul(q_re, q_im, bre_ref[...], bim_ref[...])
    c_re, c_im = cre_ref[...], cim_ref[...]
    ws_r, ws_i, wo_r, wo_i, ca_r, ca_i = [], [], [], [], [], []
    for t in range(S5_Q):
        r, i = _cmul(bb_re, bb_im, pre[:, t:t + 1, :], pim[:, t:t + 1, :])
        ws_r.append(r)
        ws_i.append(i)
        r, i = _cmul(c_re, c_im, pre[:, t:t + 1, :], pim[:, t:t + 1, :])
        ca_r.append(r)
        ca_i.append(i)
        r, i = _cmul(c_re, c_im, pre[:, t + 1:t + 2, :], pim[:, t + 1:t + 2, :])
        wo_r.append(r)
        wo_i.append(-i)
    cat = lambda xs: jnp.concatenate(xs, axis=1)
    is_fwd = pl.program_id(0) < pl.num_programs(0) // 2

    def packed(blocks, reverse_fwd):
        w = jnp.where(is_fwd if reverse_fwd else jnp.logical_not(is_fwd), cat(blocks[::-1]), cat(blocks))
        w2 = jnp.concatenate([w, w], axis=2)
        g = lax.broadcasted_iota(jnp.int32, w2.shape, 0)
        ln = lax.broadcasted_iota(jnp.int32, w2.shape, 2)
        return jnp.where((g % 2 == 0) == (ln < w.shape[2]), w2, 0.0).astype(BF16)

    wsr_ref[...] = packed(ws_r, True)
    wsi_ref[...] = packed(ws_i, True)
    wor_ref[...] = packed(wo_r, False)
    woi_ref[...] = packed(wo_i, False)
    bdot = lambda a, b: lax.dot_general(a, b, (((2,), (2,)), ((0,), (0,))), preferred_element_type=F32,
                                        precision=lax.Precision.HIGHEST)
    kin = jnp.where(is_fwd, bdot(bb_re, cat(ca_r)) - bdot(bb_im, cat(ca_i)),
                    bdot(bb_re, cat(ca_r[::-1])) - bdot(bb_im, cat(ca_i[::-1])))
    qc = kin.shape[2]
    lane = lax.broadcasted_iota(jnp.int32, kin.shape, 2)
    rows = []
    for j in range(S5_Q):
        fwd = jnp.where(lane >= j * S5_GROUP, pltpu.roll(kin, j * S5_GROUP, 2), 0.0)
        back = (S5_Q - 1 - j) * S5_GROUP
        bwd = jnp.where(lane < qc - back, pltpu.roll(kin, (qc - back) % qc, 2), 0.0)
        rows.append(jnp.where(is_fwd, fwd, bwd))
    bt_ref[...] = cat(rows).astype(BF16)
    are_ref[...] = pre[:, S5_Q:S5_Q + 1, :]
    aim_ref[...] = pim[:, S5_Q:S5_Q + 1, :]


def _s5_weights(lam_re, lam_im, log_step, b_re, b_im, c_re, c_im):
    nd, ng, ns = lam_re.shape
    G = nd * ng
    ch = S5_GROUP
    gb = 8
    qc = S5_Q * ch
    f = lambda a: a.astype(F32)
    args = (f(lam_re).reshape(G, 1, ns), f(lam_im).reshape(G, 1, ns), f(log_step).reshape(G, 1, 1),
            f(b_re).reshape(G, ns, ch).transpose(0, 2, 1), f(b_im).reshape(G, ns, ch).transpose(0, 2, 1),
            f(c_re).reshape(G, ch, ns), f(c_im).reshape(G, ch, ns))
    spec = lambda a: pl.BlockSpec((gb,) + a.shape[1:], lambda i: (i, 0, 0))
    assert nd == 2 and (G // gb) % 2 == 0
    oshape = [jax.ShapeDtypeStruct((G, qc, 2 * ns), BF16)] * 4 + [jax.ShapeDtypeStruct((G, qc, qc), BF16)] \
        + [jax.ShapeDtypeStruct((G, 1, ns), F32)] * 2
    ws_r, ws_i, wo_r, wo_i, bt, a_re, a_im = pl.pallas_call(
        _s5_weight_kernel,
        grid=(G // gb,),
        in_specs=[spec(a) for a in args],
        out_specs=[pl.BlockSpec((gb,) + s.shape[1:], lambda i: (i, 0, 0)) for s in oshape],
        out_shape=oshape,
        compiler_params=_cparams(1),
        name="s5_weights",
    )(*args)
    by_dir = lambda w: w.reshape((nd, ng) + w.shape[1:])
    pair = lambda a: a.reshape(nd, ng // 2, 1, 2 * ns)
    return by_dir(bt), by_dir(ws_r), by_dir(ws_i), by_dir(wo_r), by_dir(wo_i), pair(a_re), pair(a_im)


S5_GB = LANES // S5_GROUP


def _s5_kernel(B, nct, nlt, uj_ref, perm_ref, bt_ref, wsr_ref, wsi_ref, wor_ref, woi_ref, are_ref, aim_ref, yj_ref,
               x_scr, y_scr, s_re, s_im):
    gb, npair, qc = S5_GB, S5_GB // 2, S5_Q * S5_GROUP
    lhs = jnp.concatenate([uj_ref[j] for j in range(S5_Q)], axis=1)
    for m in range(gb):
        x_scr[:, m * qc:(m + 1) * qc] = jnp.dot(lhs, perm_ref[:, m * qc:(m + 1) * qc],
                                                preferred_element_type=F32).astype(BF16)
    xg = lambda g: x_scr[:, g * qc:(g + 1) * qc]
    for d in range(2):
        for pr in range(npair):
            for dst, w_ref in ((s_re, wsr_ref), (s_im, wsi_ref)):
                dst[d, pr] = (jnp.dot(xg(2 * pr), w_ref[d, 2 * pr], preferred_element_type=F32)
                              + jnp.dot(xg(2 * pr + 1), w_ref[d, 2 * pr + 1], preferred_element_type=F32))
    chains = [(d, pr, b) for d in range(2) for pr in range(npair) for b in range(B)]
    coef = {(d, pr): (are_ref[d, pr], aim_ref[d, pr]) for d in range(2) for pr in range(npair)}
    ctx0 = B * nlt

    def body(s, carry):
        in_ctx = s < nct
        rows = {}
        for d in range(2):
            kc = s if d == 0 else nct - 1 - s
            kl = s - nct if d == 0 else nlt - 1 - (s - nct)
            for b in range(B):
                rows[(d, b)] = pl.ds(jnp.where(in_ctx, ctx0 + b * nct + kc, b * nlt + kl), 1)
        contrib = [(s_re[d, pr, rows[(d, b)], :], s_im[d, pr, rows[(d, b)], :]) for d, pr, b in chains]
        new = []
        for (d, pr, b), (hr, hi), (sr, si) in zip(chains, carry, contrib):
            ar, ai = coef[(d, pr)]
            s_re[d, pr, rows[(d, b)], :] = hr
            s_im[d, pr, rows[(d, b)], :] = hi
            new.append((ar * hr - ai * hi + sr, ar * hi + ai * hr + si))
        return tuple(new)

    zero = jnp.zeros((1, LANES), F32)
    lax.fori_loop(0, nct + nlt, body, tuple((zero, zero) for _ in chains))
    for g in range(gb):
        acc = None
        for d in range(2):
            t = (jnp.dot(xg(g), bt_ref[d, g], preferred_element_type=F32)
                 + _nt(s_re[d, g // 2].astype(BF16), wor_ref[d, g])
                 + _nt(s_im[d, g // 2].astype(BF16), woi_ref[d, g]))
            acc = t if acc is None else acc + t
        y_scr[:, g * qc:(g + 1) * qc] = acc.astype(BF16)
    for i in range(S5_Q):
        yj_ref[i] = _nt(y_scr[...], perm_ref[i * LANES:(i + 1) * LANES, :]).astype(BF16)


def _s5(rw, uj, weights):
    B, T, C = rw.B, rw.T, rw.C
    bt, ws_r, ws_i, wo_r, wo_i, a_re, a_im = weights
    ng = bt.shape[1]
    q, gb = S5_Q, S5_GB
    nct, nlt = C // q, T // q
    nrow = uj.shape[1]
    qc = q * S5_GROUP
    k = gb * qc
    idx = np.arange(k)
    j, m, c = idx // LANES, (idx % LANES) // S5_GROUP, idx % S5_GROUP
    perm = np.zeros((k, k), np.float32)
    perm[idx, m * qc + j * S5_GROUP + c] = 1.0
    once = dict(pipeline_mode=pl.Buffered(1))
    wspec = lambda n: pl.BlockSpec((2, gb, qc, n), lambda i: (0, i, 0, 0), **once)
    aspec = pl.BlockSpec((2, gb // 2, 1, LANES), lambda i: (0, i, 0, 0))
    return pl.pallas_call(
        functools.partial(_s5_kernel, B, nct, nlt),
        grid=(ng // gb,),
        in_specs=[pl.BlockSpec((q, nrow, LANES), lambda i: (0, 0, i), **once),
                  pl.BlockSpec((k, k), lambda i: (0, 0), **once),
                  wspec(qc), wspec(LANES), wspec(LANES), wspec(LANES), wspec(LANES), aspec, aspec],
        out_specs=pl.BlockSpec((q, nrow, LANES), lambda i: (0, 0, i)),
        out_shape=jax.ShapeDtypeStruct(uj.shape, BF16),
        scratch_shapes=[pltpu.VMEM((nrow, k), BF16), pltpu.VMEM((nrow, k), BF16),
                        pltpu.VMEM((2, gb // 2, nrow, LANES), F32), pltpu.VMEM((2, gb // 2, nrow, LANES), F32)],
        compiler_params=_cparams(1),
        name="s5_scan",
    )(uj, jnp.asarray(perm, BF16), bt, ws_r, ws_i, wo_r, wo_i, a_re, a_im)


def _gelu_tanh(x):
    return 0.5 * x * (1.0 + jnp.tanh(math.sqrt(2.0 / math.pi) * (x + 0.044715 * (x * x * x))))


def _ssm_outproj_kernel(y0_ref, y1_ref, xs_ref, z_ref, v_ref, u_ref, dsk_ref, nw_ref, s5d_ref, gw_ref, gb_ref,
                        x_ref, w_ref, g1_ref, gn_ref, sh2_ref, sc2_ref, wr_ref, br_ref, lt_ref,
                        xo_ref, h2_ref, rt_ref, cnt_ref, carry, v_scr):
    i = pl.program_id(0)
    y = y0_ref[...].astype(F32) + y1_ref[...].astype(F32) + dsk_ref[...] * xs_ref[...].astype(F32)
    y = _rms(y * _silu(z_ref[...].astype(F32))) * nw_ref[...]
    ntile = v_scr.shape[0]
    nchunk = v_scr.shape[1] // S5_Q
    for j in range(S5_Q):
        for t in range(ntile):
            v_scr[t, pl.ds(j, nchunk, stride=S5_Q), :] = v_ref[j, :, t * LANES:(t + 1) * LANES].astype(F32)
    s5_y = jnp.concatenate([v_scr[t] for t in range(ntile)], axis=1)
    v = _gelu_tanh(s5_y + s5d_ref[...] * u_ref[...].astype(F32))
    v = v * _sigmoid(jnp.dot(v.astype(BF16), gw_ref[...], preferred_element_type=F32) + gb_ref[...])
    mix = jnp.concatenate([y, v], axis=1).astype(BF16)
    yo = jnp.dot(mix, w_ref[...], preferred_element_type=F32)
    _post_mixer(i, x_ref[...], yo, g1_ref[...], gn_ref[...], sh2_ref[...], sc2_ref[...], wr_ref, br_ref, lt_ref,
                xo_ref, h2_ref, rt_ref, cnt_ref, carry)


def _ssm_outproj(rw, ssd_y, act, z, s5_y, u, d_skip, norm_w, s5_d, glu_w, glu_b, xall, w_out, mods, norm_ffn, wr, br):
    D, tm = rw.D, rw.tm
    ntiles = rw.nlat
    post_in, post_out = _post_specs(rw)
    row = lambda i: (i, 0)
    vec = lambda n: pl.BlockSpec((1, n), lambda i: (0, 0))
    dsk = jnp.repeat(d_skip.astype(F32), HEAD_DIM)[None, :]
    return pl.pallas_call(
        _ssm_outproj_kernel,
        grid=(ntiles,),
        in_specs=[pl.BlockSpec((tm, 1024), row), pl.BlockSpec((tm, 1024), row),
                  pl.BlockSpec((tm, 1024), row), pl.BlockSpec((tm, 1024), row),
                  pl.BlockSpec((S5_Q, tm // S5_Q, 512), lambda i: (0, i, 0)),
                  pl.BlockSpec((tm, 512), row), vec(1024), vec(1024), vec(512),
                  pl.BlockSpec((512, 512), lambda i: (0, 0)), vec(512),
                  pl.BlockSpec((tm, D), row), pl.BlockSpec((1536, D), lambda i: (0, 0)), _mod_spec(rw, 2)] + post_in,
        out_specs=post_out,
        out_shape=_post_shapes(ntiles * tm, D),
        scratch_shapes=[pltpu.VMEM((1, LANES), F32), pltpu.VMEM((512 // LANES, tm, LANES), F32)],
        compiler_params=_cparams(1),
        name="ssm_outproj_router",
    )(ssd_y[0], ssd_y[1], act, z, s5_y, u, dsk, norm_w[None, :], s5_d[None, :], glu_w.astype(BF16), glu_b[None, :],
      xall, w_out.astype(BF16), mods, norm_ffn[None, :], mods, mods, wr, br, _lower_tri(tm))


def kernel(x, c, ctx, c_ctx, mod_w, mod_b, norm_mix, norm_ffn, att_w_in, att_w_out, na_q_norm, na_k_norm, na_rel_bias, wa_q_norm, wa_k_norm, wa_sink, ssm_w_in, ssm_w_out, ssd_conv_w, ssd_conv_b, ssd_dt_bias, ssd_a_log, ssd_d, ssd_norm, s5_lambda_re, s5_lambda_im, s5_log_step, s5_b_re, s5_b_im, s5_c_re, s5_c_im, s5_d, s5_glu_w, s5_glu_b, moe_w_group, moe_b_group, moe_w_expert, moe_b_expert, moe_w13, moe_w2):
    B, T, D = x.shape
    C = ctx.shape[1]
    rw = _Rows(B, T, C, D, ROW_TILE)
    xl = x.reshape(B * T, D)
    xc = ctx.reshape(B * C, D)
    cm = jnp.concatenate([c, c_ctx[None, :], jnp.zeros((8 - B - 1, D), F32)], axis=0)
    mods = _modulation(cm, mod_w, mod_b)
    mods = mods.reshape(mods.shape[0], 8, 1, 6 * D)

    m0 = mods[0]
    qkv = _att_inproj(rw, xl, xc, m0, norm_mix[0], att_w_in[0], na_q_norm[0], na_k_norm[0], wa_q_norm[0],
                      wa_k_norm[0])
    na = _na_attention(rw, qkv, na_rel_bias[0])
    wa = _wa_attention(rw, qkv, wa_sink[0])
    cx = _ctx_attention(rw, qkv, wa_sink[0])
    wr, br = _router_weights(moe_w_group[0], moe_b_group[0], moe_w_expert[0], moe_b_expert[0])
    xall, h2, route, counts = _att_outproj(rw, na, wa, cx, xl, xc, att_w_out[0], m0, norm_ffn[0], wr, br)
    y1, y2 = _moe(h2, route, counts, moe_w13, moe_w2, 0)

    m1 = mods[1]
    xall, z, xbc, u, uj, dtr = _ssm_inproj(rw, xall, y1, y2, route, m0, m1, norm_mix[1], ssm_w_in[0])
    act, dt2 = _ssm_conv(rw, xbc, dtr, ssd_conv_w[0], ssd_conv_b[0], ssd_dt_bias[0])
    ssd_y = _ssd(rw, act, dt2, ssd_a_log[0])
    s5_w = _s5_weights(s5_lambda_re[0], s5_lambda_im[0], s5_log_step[0], s5_b_re[0], s5_b_im[0], s5_c_re[0],
                       s5_c_im[0])
    s5_y = _s5(rw, uj, s5_w)
    wr, br = _router_weights(moe_w_group[1], moe_b_group[1], moe_w_expert[1], moe_b_expert[1])
    xlat, h2, route, counts = _ssm_outproj(rw, ssd_y, act, z, s5_y, u, ssd_d[0], ssd_norm[0], s5_d[0], s5_glu_w[0],
                                           s5_glu_b[0], xall, ssm_w_out[0], m1, norm_ffn[1], wr, br)
    y1, y2 = _moe(h2, route, counts, moe_w13, moe_w2, 1)
    out = _combine(rw, rw.nlat, xlat, y1, y2, route, m1)
    return out.reshape(B, T, D)
```

```python
import functools
import math

import jax
import jax.numpy as jnp
import numpy as np
from jax import lax
from jax.experimental import pallas as pl
from jax.experimental.pallas import tpu as pltpu

F32 = jnp.float32
BF16 = jnp.bfloat16

EPS = 1e-6
NEG_INF = -1e30
GRID_W = 64
HEAD_DIM = 64
NA_KH = 8
NA_KW = 16
WA_BLOCK = 128
ROPE_BASE = 10000.0
SSD_CHUNK = 128
S5_GROUP = 16
MOE_GROUPS = 4
MOE_EPG = 8
MOE_EXPERTS = MOE_GROUPS * MOE_EPG

LANES = 128
ROW_TILE = 512
MXU_W = 256
MOE_TILE = MXU_W
VMEM_LIMIT = 56 * 1024 * 1024
MOE_VMEM_LIMIT = 60 * 1024 * 1024


def _cparams(n_axes, vmem=VMEM_LIMIT):
    return pltpu.CompilerParams(dimension_semantics=("arbitrary",) * n_axes, vmem_limit_bytes=vmem)


def _sigmoid(x):
    return jax.nn.sigmoid(x)


def _silu(x):
    return x * _sigmoid(x)


def _rms(x, eps=EPS):
    return x * lax.rsqrt(jnp.mean(x * x, axis=-1, keepdims=True) + eps)


def _ada_norm(x, g, shift, scale):
    return (_rms(x) * g) * (1.0 + scale) + shift


def _mod_kernel(c_ref, w_ref, b_ref, o_ref):
    a = _silu(c_ref[...])
    o_ref[...] = jnp.dot(a, w_ref[...], preferred_element_type=F32, precision=lax.Precision.HIGHEST) + b_ref[...]


def _modulation(cm, mod_w, mod_b):
    depth, d, n6 = mod_w.shape
    tn = 1024
    return pl.pallas_call(
        _mod_kernel,
        grid=(depth, n6 // tn),
        in_specs=[pl.BlockSpec((8, d), lambda l, j: (0, 0)),
                  pl.BlockSpec((None, d, tn), lambda l, j: (l, 0, j)),
                  pl.BlockSpec((None, 1, tn), lambda l, j: (l, 0, j))],
        out_specs=pl.BlockSpec((None, 8, tn), lambda l, j: (l, 0, j)),
        out_shape=jax.ShapeDtypeStruct((depth, 8, n6), F32),
        compiler_params=_cparams(2),
        name="modulation",
    )(cm, mod_w, mod_b.reshape(depth, 1, n6))


class _Rows:
    def __init__(self, B, T, C, D, tm):
        assert T % tm == 0 and (B * C) % tm == 0
        self.B, self.T, self.C, self.D, self.tm = B, T, C, D, tm
        self.tpb = T // tm
        self.nlat = B * self.tpb
        self.nctx = (B * C) // tm
        self.ntot = self.nlat + self.nctx
        self.rows = B * (T + C)

    def group(self, i):
        return jnp.where(i < self.nlat, i // self.tpb, self.B)


def _mod_spec(rw, col):
    return pl.BlockSpec((None, 1, rw.D), lambda i, *_: (rw.group(i), 0, col))


def _seg_norm(y, seg, gcol):
    ss = jnp.dot((y * y).astype(BF16), seg, preferred_element_type=F32)
    return y * lax.rsqrt(ss + EPS) * gcol


def _rope(y, cos, sin):
    w = y.shape[-1]
    lane = lax.broadcasted_iota(jnp.int32, y.shape, 1)
    first = (lane % 32) < 16
    partner = jnp.where(first, pltpu.roll(y, w - 16, 1), pltpu.roll(y, 16, 1))
    return y * cos + partner * sin


def _dup_halves(k):
    lane = lax.broadcasted_iota(jnp.int32, k.shape, 1)
    sw = pltpu.roll(k, 64, 1)
    return jnp.where(lane < 64, k, sw), jnp.where(lane < 64, sw, k)


def _att_inproj_kernel(nlat, xl_ref, xc_ref, g_ref, sh_ref, sc_ref, w_ref, gcol_ref, cos_ref, sin_ref, seg_ref,
                       o_ref, h_scr):
    i = pl.program_id(0)
    x = jnp.where(i < nlat, xl_ref[...], xc_ref[...])
    h_scr[...] = _ada_norm(x, g_ref[...], sh_ref[...], sc_ref[...]).astype(BF16)
    seg = seg_ref[...]
    cos2 = jnp.concatenate([cos_ref[...], cos_ref[...]], axis=1)
    sin2 = jnp.concatenate([sin_ref[...], sin_ref[...]], axis=1)
    for c in range(w_ref.shape[1] // MXU_W):
        c0 = c * MXU_W
        cols = slice(c0, c0 + MXU_W)
        y = jnp.dot(h_scr[...], w_ref[:, cols], preferred_element_type=F32)
        gcol = gcol_ref[:, cols]
        if c in (0, 1, 2, 3):
            o_ref[:, cols] = _seg_norm(y, seg, gcol).astype(BF16)
        elif c in (4, 5):
            o_ref[:, cols] = y.astype(BF16)
        elif c in (6, 7):
            o_ref[:, cols] = _rope(_seg_norm(y, seg, gcol), cos2, sin2).astype(BF16)
        else:
            lane = lax.broadcasted_iota(jnp.int32, y.shape, 1)
            yk = jnp.where(lane < LANES, _seg_norm(y, seg, gcol), y)
            yr = jnp.where(lane < LANES, _rope(yk, cos2, sin2), yk)
            k0, k1 = _dup_halves(yr[:, :LANES])
            v0, v1 = _dup_halves(yr[:, LANES:])
            for t, dup in enumerate((k0, k1, v0, v1)):
                o_ref[:, c0 + t * LANES:c0 + (t + 1) * LANES] = dup.astype(BF16)


def _rope_tables(T, tm):
    t = np.arange(T)
    d = np.arange(HEAD_DIM)
    nf = HEAD_DIM // 4
    inv = jnp.asarray(ROPE_BASE, F32) ** (-jnp.arange(nf, dtype=F32) / nf)
    pos = np.where((d // 32 == 0)[None, :], (t // GRID_W)[:, None], (t % GRID_W)[:, None])
    ang = jnp.asarray(pos, F32) * inv[d % nf][None, :]
    sign = np.where((d % 32) < 16, -1.0, 1.0).astype(np.float32)
    cos = jnp.cos(ang)
    sin = jnp.sin(ang) * sign[None, :]
    cos = jnp.concatenate([cos, jnp.ones((tm, HEAD_DIM), F32)], axis=0)
    sin = jnp.concatenate([sin, jnp.zeros((tm, HEAD_DIM), F32)], axis=0)
    return jnp.tile(cos, (1, 2)), jnp.tile(sin, (1, 2))


def _att_inproj(rw, xl, xc, mods, norm_g, w_in, na_qn, na_kn, wa_qn, wa_kn):
    D, tm = rw.D, rw.tm
    scale = HEAD_DIM ** -0.5
    gcol = jnp.concatenate([jnp.tile(na_qn * scale, 8), jnp.tile(na_kn, 8), jnp.ones((512,), F32),
                            jnp.tile(wa_qn * scale, 8), jnp.tile(wa_kn, 2), jnp.ones((128,), F32)])[None, :]
    cos, sin = _rope_tables(rw.T, tm)
    segn = np.arange(256) // 64
    seg = jnp.asarray((segn[:, None] == segn[None, :]).astype(np.float32) / 64.0, BF16)
    nlat, tpb = rw.nlat, rw.tpb
    return pl.pallas_call(
        functools.partial(_att_inproj_kernel, nlat),
        grid=(rw.ntot,),
        in_specs=[pl.BlockSpec((tm, D), lambda i: (jnp.minimum(i, nlat - 1), 0)),
                  pl.BlockSpec((tm, D), lambda i: (jnp.maximum(i - nlat, 0), 0)),
                  pl.BlockSpec((1, D), lambda i: (0, 0)),
                  _mod_spec(rw, 0), _mod_spec(rw, 1),
                  pl.BlockSpec((D, 2304), lambda i: (0, 0)),
                  pl.BlockSpec((1, 2304), lambda i: (0, 0)),
                  pl.BlockSpec((tm, 128), lambda i: (jnp.where(i < nlat, i % tpb, tpb), 0)),
                  pl.BlockSpec((tm, 128), lambda i: (jnp.where(i < nlat, i % tpb, tpb), 0)),
                  pl.BlockSpec((256, 256), lambda i: (0, 0))],
        out_specs=pl.BlockSpec((tm, 2560), lambda i: (i, 0)),
        out_shape=jax.ShapeDtypeStruct((rw.rows, 2560), BF16),
        scratch_shapes=[pltpu.VMEM((tm, D), BF16)],
        compiler_params=_cparams(1),
        name="att_inproj",
    )(xl, xc, norm_g[None, :], mods, mods, w_in.astype(BF16), gcol, cos, sin, seg)


def _route(lg, lt, carry):
    lane = lax.broadcasted_iota(jnp.int32, lg.shape, 1).astype(F32)
    gm = lane < MOE_GROUPS
    mg = jnp.max(jnp.where(gm, lg, NEG_INF), axis=-1, keepdims=True)
    eg = jnp.where(gm, jnp.exp(jnp.where(gm, lg, NEG_INF) - mg), 0.0)
    pg = eg / jnp.sum(eg, axis=-1, keepdims=True)
    ptop = jnp.max(pg, axis=-1, keepdims=True)
    gidx = jnp.min(jnp.where(gm & (pg == ptop), lane, 1e9), axis=-1, keepdims=True)
    lo = MOE_GROUPS + MOE_EPG * gidx
    em = (lane >= lo) & (lane < lo + MOE_EPG)
    le = jnp.where(em, lg, NEG_INF)
    ee = jnp.where(em, jnp.exp(le - jnp.max(le, axis=-1, keepdims=True)), 0.0)
    pe = ee / jnp.sum(ee, axis=-1, keepdims=True)
    v1 = jnp.max(jnp.where(em, pe, -1.0), axis=-1, keepdims=True)
    i1 = jnp.min(jnp.where(em & (pe == v1), lane, 1e9), axis=-1, keepdims=True)
    em2 = em & (lane != i1)
    v2 = jnp.max(jnp.where(em2, pe, -1.0), axis=-1, keepdims=True)
    i2 = jnp.min(jnp.where(em2 & (pe == v2), lane, 1e9), axis=-1, keepdims=True)
    den = v1 + v2
    w1 = v1 / den * ptop
    w2 = v2 / den * ptop
    e1 = i1 - MOE_GROUPS
    e2 = i2 - MOE_GROUPS
    m1 = lane == e1
    m2 = lane == e2
    oh = jnp.where(m1 | m2, 1.0, 0.0)
    cnt = jnp.dot(lt, oh.astype(BF16), preferred_element_type=F32) + carry
    r1 = jnp.sum(jnp.where(m1, cnt, 0.0), axis=-1, keepdims=True)
    r2 = jnp.sum(jnp.where(m2, cnt, 0.0), axis=-1, keepdims=True)
    route = jnp.where(lane == 0, e1, jnp.where(lane == 1, e2, jnp.where(lane == 2, w1, jnp.where(
        lane == 3, w2, jnp.where(lane == 4, r1, jnp.where(lane == 5, r2, 0.0))))))
    return route, carry + jnp.sum(oh, axis=0, keepdims=True)


def _post_mixer(i, x, y, g1, gn, sh2, sc2, wr_ref, br_ref, lt_ref, xo_ref, h2_ref, rt_ref, cnt_ref, carry):
    xn = x + g1 * y
    xo_ref[...] = xn
    h2 = _ada_norm(xn, gn, sh2, sc2)
    hb = h2.astype(BF16)
    hbf = hb.astype(F32)
    half = h2.shape[1] // 2
    h2_ref[...] = pltpu.pack_elementwise([h2[:, :half], h2[:, half:]], packed_dtype=BF16)
    hl = (h2 - hbf).astype(BF16)
    lg = (jnp.dot(hb, wr_ref[0], preferred_element_type=F32)
          + (jnp.dot(hb, wr_ref[1], preferred_element_type=F32) + jnp.dot(hl, wr_ref[0], preferred_element_type=F32))
          + br_ref[...])

    @pl.when(i == 0)
    def _():
        carry[...] = jnp.zeros_like(carry)

    route, newc = _route(lg, lt_ref[...], carry[...])
    rt_ref[...] = route
    carry[...] = newc
    cnt_ref[...] = newc


def _att_outproj_kernel(nlat, na_ref, wa_ref, cx_ref, xl_ref, xc_ref, w_ref, g1_ref, gn_ref, sh2_ref, sc2_ref,
                        wr_ref, br_ref, lt_ref, xo_ref, h2_ref, rt_ref, cnt_ref, carry):
    i = pl.program_id(0)
    lat = i < nlat
    mix = jnp.where(lat, jnp.concatenate([na_ref[...], wa_ref[...]], axis=1), cx_ref[...])
    y = jnp.dot(mix, w_ref[...], preferred_element_type=F32)
    x = jnp.where(lat, xl_ref[...], xc_ref[...])
    _post_mixer(i, x, y, g1_ref[...], gn_ref[...], sh2_ref[...], sc2_ref[...], wr_ref, br_ref, lt_ref,
                xo_ref, h2_ref, rt_ref, cnt_ref, carry)


def _router_weights(w_group, b_group, w_expert, b_expert):
    D = w_group.shape[0]
    pad = LANES - MOE_GROUPS - MOE_EXPERTS
    wr = jnp.concatenate([w_group, w_expert, jnp.zeros((D, pad), F32)], axis=1)
    br = jnp.concatenate([b_group, b_expert, jnp.zeros((pad,), F32)])[None, :]
    hi = wr.astype(BF16)
    lo = (wr - hi.astype(F32)).astype(BF16)
    return jnp.stack([hi, lo]), br


def _lower_tri(tm):
    r = np.arange(tm)
    return jnp.asarray((r[None, :] < r[:, None]).astype(np.float32), BF16)


def _post_specs(rw):
    D, tm = rw.D, rw.tm
    return ([pl.BlockSpec((1, D), lambda i: (0, 0)), _mod_spec(rw, 3), _mod_spec(rw, 4),
             pl.BlockSpec((2, D, LANES), lambda i: (0, 0, 0)), pl.BlockSpec((1, LANES), lambda i: (0, 0)),
             pl.BlockSpec((tm, tm), lambda i: (0, 0))],
            [pl.BlockSpec((tm, D), lambda i: (i, 0)), pl.BlockSpec((tm, D // 2), lambda i: (i, 0)),
             pl.BlockSpec((tm, LANES), lambda i: (i, 0)), pl.BlockSpec((1, LANES), lambda i: (0, 0))])


def _post_shapes(nrows, D):
    return [jax.ShapeDtypeStruct((nrows, D), F32), jax.ShapeDtypeStruct((nrows, D // 2), jnp.uint32),
            jax.ShapeDtypeStruct((nrows, LANES), F32), jax.ShapeDtypeStruct((1, LANES), F32)]


def _att_outproj(rw, na, wa, cx, xl, xc, w_out, mods, norm_ffn, wr, br):
    D, tm, nlat = rw.D, rw.tm, rw.nlat
    post_in, post_out = _post_specs(rw)
    latmap = lambda i: (jnp.minimum(i, nlat - 1), 0)
    ctxmap = lambda i: (jnp.maximum(i - nlat, 0), 0)
    return pl.pallas_call(
        functools.partial(_att_outproj_kernel, nlat),
        grid=(rw.ntot,),
        in_specs=[pl.BlockSpec((tm, 512), latmap), pl.BlockSpec((tm, 512), latmap), pl.BlockSpec((tm, D), ctxmap),
                  pl.BlockSpec((tm, D), latmap), pl.BlockSpec((tm, D), ctxmap),
                  pl.BlockSpec((D, D), lambda i: (0, 0)), _mod_spec(rw, 2)] + post_in,
        out_specs=post_out,
        out_shape=_post_shapes(rw.rows, D),
        scratch_shapes=[pltpu.VMEM((1, LANES), F32)],
        compiler_params=_cparams(1),
        name="att_outproj_router",
    )(na, wa, cx, xl, xc, w_out.astype(BF16), mods, norm_ffn[None, :], mods, mods, wr, br, _lower_tri(tm))


def _moe_kernel(te_ref, nu_ref, src_ref, nsrc_ref, hp_ref, w13_ref, w2_ref, o_ref, w13b, w2b, xa, xb):
    i = pl.program_id(0)
    prev = te_ref[jnp.maximum(i - 1, 0)]
    changed = (i == 0) | (te_ref[i] != prev)
    tg = xa.shape[0]

    @pl.when(changed)
    def _():
        w13b[...] = w13_ref[...].astype(BF16)
        w2b[...] = w2_ref[...].astype(BF16)

    @pl.when(i == 0)
    def _():
        def fetch(j, carry):
            xa[pl.ds(j, 1), :] = hp_ref[pl.ds(src_ref[0, j], 1), :]
            return carry

        lax.fori_loop(0, tg, fetch, 0, unroll=8)

    def step(cur, nxt):
        for j in range(tg):
            nxt[pl.ds(j, 1), :] = hp_ref[pl.ds(nsrc_ref[0, j], 1), :]
        ff = w2b.shape[0]
        half = cur.shape[1]
        w = cur[...]
        unpack = functools.partial(pltpu.unpack_elementwise, packed_dtype=BF16, unpacked_dtype=F32)
        x_lo = unpack(w, index=0).astype(BF16)
        x_hi = unpack(w, index=1).astype(BF16)
        a13 = (jnp.dot(x_lo, w13b[:half, :], preferred_element_type=F32)
               + jnp.dot(x_hi, w13b[half:, :], preferred_element_type=F32))
        act = _silu(a13[:, :ff]) * a13[:, ff:]
        o_ref[...] = jnp.dot(act.astype(BF16), w2b[...], preferred_element_type=F32).astype(BF16)

    used = i < nu_ref[0]

    @pl.when(used & (i % 2 == 0))
    def _():
        step(xa, xb)

    @pl.when(used & (i % 2 == 1))
    def _():
        step(xb, xa)

    @pl.when(i >= nu_ref[0])
    def _():
        o_ref[...] = jnp.zeros_like(o_ref)


def _moe(h2p, route, counts, w13, w2, layer):
    N = h2p.shape[0]
    D = 2 * h2p.shape[1]
    _, E, _, F2 = w13.shape
    tg = MOE_TILE
    nt = (2 * N) // tg + E
    e = route[:, 0:2].astype(jnp.int32)
    rank = route[:, 4:6].astype(jnp.int32)
    cnt = counts[0, :E].astype(jnp.int32)
    ntile_e = (cnt + tg - 1) // tg
    tile_end = jnp.cumsum(ntile_e)
    offs = (tile_end - ntile_e) * tg
    onehot = (e[:, :, None] == jnp.arange(E, dtype=jnp.int32)).astype(jnp.int32)
    dest = jnp.sum(onehot * offs, axis=-1) + rank
    src = jnp.zeros((nt * tg,), jnp.int32).at[dest.reshape(-1)].set(jnp.repeat(jnp.arange(N, dtype=jnp.int32), 2))
    tile_id = jnp.arange(nt, dtype=jnp.int32)
    nu = tile_end[-1:].astype(jnp.int32)
    te = jnp.sum((tile_end[None, :] <= jnp.minimum(tile_id, nu[0] - 1)[:, None]).astype(jnp.int32), axis=1)
    te = jnp.minimum(te, E - 1)
    ys = pl.pallas_call(
        _moe_kernel,
        grid_spec=pltpu.PrefetchScalarGridSpec(
            num_scalar_prefetch=2,
            grid=(nt,),
            in_specs=[pl.BlockSpec((None, 1, tg), lambda i, te, nu: (i, 0, 0), memory_space=pltpu.SMEM),
                      pl.BlockSpec((None, 1, tg), lambda i, te, nu: (jnp.minimum(i + 1, nt - 1), 0, 0),
                                   memory_space=pltpu.SMEM),
                      pl.BlockSpec((N, D // 2), lambda i, te, nu: (0, 0), pipeline_mode=pl.Buffered(1)),
                      pl.BlockSpec((None, None, D, F2), lambda i, te, nu: (layer, te[i], 0, 0)),
                      pl.BlockSpec((None, None, F2 // 2, D), lambda i, te, nu: (layer, te[i], 0, 0))],
            out_specs=pl.BlockSpec((tg, D), lambda i, te, nu: (i, 0)),
            scratch_shapes=[pltpu.VMEM((D, F2), BF16), pltpu.VMEM((F2 // 2, D), BF16),
                            pltpu.VMEM((tg, D // 2), jnp.uint32), pltpu.VMEM((tg, D // 2), jnp.uint32)]),
        out_shape=jax.ShapeDtypeStruct((nt * tg, D), BF16),
        compiler_params=_cparams(1, vmem=MOE_VMEM_LIMIT),
        name="moe_experts",
    )(te, nu, src.reshape(nt, 1, tg), src.reshape(nt, 1, tg), h2p, w13, w2)
    pick = lambda k: ys.at[dest[:, k]].get(mode="promise_in_bounds")
    return pick(0), pick(1)


def _moe_residual(x_ref, y1_ref, y2_ref, rt_ref, g2_ref):
    rt = rt_ref[...]
    f = rt[:, 2:3] * y1_ref[...].astype(F32) + rt[:, 3:4] * y2_ref[...].astype(F32)
    return x_ref[...] + g2_ref[...] * f


def _combine_kernel(x_ref, y1_ref, y2_ref, rt_ref, g2_ref, o_ref):
    o_ref[...] = _moe_residual(x_ref, y1_ref, y2_ref, rt_ref, g2_ref)


def _combine(rw, ntiles, xall, y1, y2, route, mods):
    D, tm = rw.D, rw.tm
    row = lambda i: (i, 0)
    return pl.pallas_call(
        _combine_kernel,
        grid=(ntiles,),
        in_specs=[pl.BlockSpec((tm, D), row), pl.BlockSpec((tm, D), row), pl.BlockSpec((tm, D), row),
                  pl.BlockSpec((tm, LANES), row), _mod_spec(rw, 5)],
        out_specs=pl.BlockSpec((tm, D), row),
        out_shape=jax.ShapeDtypeStruct((ntiles * tm, D), F32),
        compiler_params=_cparams(1),
        name="moe_combine",
    )(xall, y1, y2, route, mods)


NA_QROWS = 8
NA_KROWS = 16


def _na_first_key_row(variant, a):
    return (max(a - 4, 0) + 4, a, min(a, 4))[variant]


def _na_key_lanes(row0):
    a = row0 // GRID_W
    starts = [_na_first_key_row(v, a) for v in range(3)]
    lo = (min(starts) * GRID_W) // LANES * LANES
    hi = -(-((max(starts) + NA_KH) * GRID_W) // LANES) * LANES
    return lo, hi


def _na_bias_tiles(rpb):
    H = rpb.shape[0]
    i = np.arange(GRID_W)
    c0 = np.clip(i - NA_KW // 2, 0, GRID_W - NA_KW)
    j = np.arange(GRID_W)
    colvalid = (j[None, :] >= c0[:, None]) & (j[None, :] < c0[:, None] + NA_KW)
    dc = np.clip(j[None, :] - i[:, None] + NA_KW - 1, 0, 2 * NA_KW - 2)
    onehot = ((dc[None] == np.arange(2 * NA_KW - 1)[:, None, None]) & colvalid[None]).astype(np.float32)
    tiles = jnp.einsum('hrc,cij->hrij', rpb.astype(F32), jnp.asarray(onehot), precision=lax.Precision.HIGHEST)
    tiles = tiles + jnp.asarray(np.where(colvalid, 0.0, NEG_INF).astype(np.float32))
    return tiles.transpose(0, 2, 1, 3).reshape(H, GRID_W, (2 * NA_KH - 1) * GRID_W)


def _na_fill_bias(variant, tiles_ref, bias_scr):
    for hh in range(2):
        for a in range(NA_QROWS):
            start = _na_first_key_row(variant, a)
            dr0 = start - a + 3
            rows = slice(a * GRID_W, (a + 1) * GRID_W)
            w0, w1 = start * GRID_W, (start + NA_KH) * GRID_W
            if w0 > 0:
                bias_scr[hh, rows, 0:w0] = jnp.full((GRID_W, w0), NEG_INF, F32)
            bias_scr[hh, rows, w0:w1] = tiles_ref[hh, :, dr0 * GRID_W:(dr0 + NA_KH) * GRID_W]
            if w1 < NA_KROWS * GRID_W:
                bias_scr[hh, rows, w1:] = jnp.full((GRID_W, NA_KROWS * GRID_W - w1), NEG_INF, F32)


def _softmax_pv(parts, extra=None, rc=64):
    m_rows = parts[0][0].shape[0]
    probs = [[] for _ in parts]
    inv_l = []
    for r0 in range(0, m_rows, rc):
        sc = []
        for s, _, bias_fn, lanes_fn in parts:
            l0, l1 = (0, s.shape[1]) if lanes_fn is None else lanes_fn(r0)
            c = s[r0:r0 + rc, l0:l1]
            if bias_fn is not None:
                c = c + bias_fn(r0, rc, slice(l0, l1))
            sc.append((c, l0, s.shape[1] - l1))
        mx = functools.reduce(jnp.maximum, [jnp.max(c, axis=-1, keepdims=True) for c, _, _ in sc])
        if extra is not None:
            mx = jnp.maximum(mx, extra[r0:r0 + rc])
        l = jnp.zeros_like(mx) if extra is None else jnp.exp(extra[r0:r0 + rc] - mx)
        for k, (c, before, after) in enumerate(sc):
            p = jnp.exp(c - mx)
            l = l + jnp.sum(p, axis=-1, keepdims=True)
            row = [jnp.zeros((rc, before), BF16)] * (before > 0) + [p.astype(BF16)] \
                + [jnp.zeros((rc, after), BF16)] * (after > 0)
            probs[k].append(row[0] if len(row) == 1 else jnp.concatenate(row, axis=1))
        inv_l.append(1.0 / l)
    o = None
    for k, (_, v, _, _) in enumerate(parts):
        pv = jnp.dot(jnp.concatenate(probs[k], axis=0), v, preferred_element_type=F32)
        o = pv if o is None else o + pv
    return o * jnp.concatenate(inv_l, axis=0)


def _nt(a, b):
    return lax.dot_general(a, b, (((1,), (1,)), ((), ())), preferred_element_type=F32)


NA_REFS_PER_BATCH = 11


def _na_kernel(n_rb, nb, *refs):
    tiles_ref, o_ref, bias_ref = refs[nb * NA_REFS_PER_BATCH:]
    rb = pl.program_id(1)
    for variant, at_rb in ((0, 0), (1, 1), (2, n_rb - 1)):
        @pl.when(rb == at_rb)
        def _(variant=variant):
            _na_fill_bias(variant, tiles_ref, bias_ref)

    for b in range(nb):
        q_ref, k0, k1, k2, k3, v0, v1, v2, v3, kc_ref, vc_ref = refs[b * NA_REFS_PER_BATCH:(b + 1) * NA_REFS_PER_BATCH]
        q2 = q_ref[...]
        kw = jnp.concatenate([k0[...], k1[...], k2[...], k3[...]], axis=0)
        vw = jnp.concatenate([v0[...], v1[...], v2[...], v3[...]], axis=0)
        kc = kc_ref[...]
        vc = vc_ref[...]
        lane = lax.broadcasted_iota(jnp.int32, q2.shape, 1)
        out = jnp.zeros(q2.shape, F32)
        for hh in range(2):
            m = (lane < HEAD_DIM) if hh == 0 else (lane >= HEAD_DIM)
            qm = jnp.where(m, q2, jnp.zeros_like(q2))
            o = _softmax_pv([(_nt(qm, kw), vw, lambda r0, rc, lanes, hh=hh: bias_ref[hh, r0:r0 + rc, lanes],
                              _na_key_lanes), (_nt(qm, kc), vc, None, None)], rc=32)
            out = jnp.where(m, o, out)
        o_ref[b] = out.astype(BF16)


def _na_attention(rw, qkv, rpb):
    B, T, C = rw.B, rw.T, rw.C
    tq = NA_QROWS * GRID_W
    tk = tq // 2
    n_rb = T // tq
    nkb = T // tk
    assert T % tq == 0 and n_rb >= 2 and (B * T) % C == 0
    tiles = _na_bias_tiles(rpb)
    ctxrow = (B * T) // C

    def batch_specs(b):
        kv = lambda j, col: pl.BlockSpec(
            (tk, LANES), lambda p, rb: (b * nkb + jnp.clip(2 * rb - 1 + j, 0, nkb - 1), col + p))
        return ([pl.BlockSpec((tq, LANES), lambda p, rb: (b * n_rb + rb, p))]
                + [kv(j, 4) for j in range(4)] + [kv(j, 8) for j in range(4)]
                + [pl.BlockSpec((C, LANES), lambda p, rb: (ctxrow + b, 4 + p)),
                   pl.BlockSpec((C, LANES), lambda p, rb: (ctxrow + b, 8 + p))])

    out = pl.pallas_call(
        functools.partial(_na_kernel, n_rb, B),
        grid=(4, n_rb),
        in_specs=sum([batch_specs(b) for b in range(B)], [])
        + [pl.BlockSpec((2,) + tiles.shape[1:], lambda p, rb: (p, 0, 0))],
        out_specs=pl.BlockSpec((B, tq, LANES), lambda p, rb: (0, rb, p)),
        out_shape=jax.ShapeDtypeStruct((B, T, 4 * LANES), BF16),
        scratch_shapes=[pltpu.VMEM((2, tq, 2 * tq), F32)],
        compiler_params=_cparams(2),
        name="neighbourhood_attention",
    )(*([qkv] * (B * NA_REFS_PER_BATCH)), tiles)
    return out.reshape(B * T, 4 * LANES)


WA_QBLOCKS = 2


def _wa_kernel(nb, sink_ref, q_ref, *refs):
    step = pl.program_id(1)
    blk = WA_BLOCK
    nkb = WA_QBLOCKS + 2
    kblocks, vblocks = refs[:nkb], refs[nkb:2 * nkb]
    kx_ref, vx_ref, o_ref = refs[2 * nkb:]
    lane = lax.broadcasted_iota(jnp.int32, (blk, LANES), 1)
    zero = jnp.zeros((blk, LANES), BF16)
    qi = lax.broadcasted_iota(jnp.int32, (blk, 3 * blk), 0)
    ks = lax.broadcasted_iota(jnp.int32, (blk, 3 * blk), 1)
    for qb in range(WA_QBLOCKS):
        n = step * WA_QBLOCKS + qb
        lo = jnp.where(n > 0, 0, blk)
        hi = jnp.where(n < nb - 1, 3 * blk, 2 * blk)
        valid = (ks >= qi) & (ks <= qi + 2 * blk) & (ks >= lo) & (ks < hi)
        band_mask = jnp.where(valid, 0.0, NEG_INF)
        rows = slice(qb * blk, (qb + 1) * blk)
        for kv in range(2):
            parts = []
            for pr in range(2):
                c0 = kv * 2 * LANES + pr * LANES
                qp = q_ref[rows, c0:c0 + LANES]
                parts += [jnp.where(lane < HEAD_DIM, qp, zero), jnp.where(lane >= HEAD_DIM, qp, zero)]
            qs = jnp.concatenate(parts, axis=0)
            cs = slice(kv * LANES, (kv + 1) * LANES)
            kb = jnp.concatenate([r[:, cs] for r in kblocks[qb:qb + 3]], axis=0)
            vb = jnp.concatenate([r[:, cs] for r in vblocks[qb:qb + 3]], axis=0)
            sink = jnp.concatenate([jnp.full((blk, 1), sink_ref[kv * 4 + g], F32) for g in range(4)], axis=0)
            o = _softmax_pv([(_nt(qs, kb), vb, lambda r0, rc, lanes, m=band_mask: m[r0 % blk:r0 % blk + rc, lanes], None),
                             (_nt(qs, kx_ref[:, cs]), vx_ref[:, cs], None, None)], extra=sink, rc=64)
            c0 = kv * 2 * LANES
            o_ref[rows, c0:c0 + LANES] = jnp.where(lane < HEAD_DIM, o[0:blk], o[blk:2 * blk]).astype(BF16)
            o_ref[rows, c0 + LANES:c0 + 2 * LANES] = jnp.where(
                lane < HEAD_DIM, o[2 * blk:3 * blk], o[3 * blk:4 * blk]).astype(BF16)


def _wa_attention(rw, qkv, sink):
    B, T, C = rw.B, rw.T, rw.C
    blk = WA_BLOCK
    nb = T // blk
    nq = WA_QBLOCKS
    assert nb % nq == 0
    ctxrow = (B * T) // C

    def kvspec(j, col):
        return pl.BlockSpec((blk, 2 * LANES), lambda b, s: (b * nb + jnp.clip(nq * s - 1 + j, 0, nb - 1), col))

    return pl.pallas_call(
        functools.partial(_wa_kernel, nb),
        grid=(B, nb // nq),
        in_specs=[pl.BlockSpec(memory_space=pltpu.SMEM),
                  pl.BlockSpec((nq * blk, 4 * LANES), lambda b, s: (b * (nb // nq) + s, 3))]
        + [kvspec(j, 8) for j in range(nq + 2)] + [kvspec(j, 9) for j in range(nq + 2)]
        + [pl.BlockSpec((C, 2 * LANES), lambda b, s: (ctxrow + b, 8)),
           pl.BlockSpec((C, 2 * LANES), lambda b, s: (ctxrow + b, 9))],
        out_specs=pl.BlockSpec((nq * blk, 4 * LANES), lambda b, s: (b * (nb // nq) + s, 0)),
        out_shape=jax.ShapeDtypeStruct((B * T, 4 * LANES), BF16),
        compiler_params=_cparams(2),
        name="window_attention",
    )(sink.astype(F32), qkv, *([qkv] * (2 * nq + 6)))


def _ctx_attn_kernel(sink_ref, t_ref, o_ref):
    C = t_ref.shape[0]
    lane = lax.broadcasted_iota(jnp.int32, (C, LANES), 1)
    zero = jnp.zeros((C, LANES), BF16)

    def pair(q2, k2, v2, sinks):
        out = jnp.zeros((C, LANES), F32)
        for hh in range(2):
            m = (lane < HEAD_DIM) if hh == 0 else (lane >= HEAD_DIM)
            extra = None if sinks is None else jnp.full((C, 1), sinks[hh], F32)
            o = _softmax_pv([(_nt(jnp.where(m, q2, zero), k2), v2, None, None)], extra=extra, rc=64)
            out = jnp.where(m, o, out)
        return out.astype(BF16)

    for p in range(4):
        c = p * LANES
        o_ref[:, c:c + LANES] = pair(t_ref[:, c:c + LANES], t_ref[:, 512 + c:640 + c], t_ref[:, 1024 + c:1152 + c], None)
    for kv in range(2):
        kd = t_ref[:, 2048 + kv * LANES:2176 + kv * LANES]
        vd = t_ref[:, 2304 + kv * LANES:2432 + kv * LANES]
        for pr in range(2):
            c = kv * 256 + pr * LANES
            h0 = kv * 4 + pr * 2
            o_ref[:, 512 + c:640 + c] = pair(t_ref[:, 1536 + c:1664 + c], kd, vd, (sink_ref[h0], sink_ref[h0 + 1]))


def _ctx_attention(rw, qkv, sink):
    B, T, C = rw.B, rw.T, rw.C
    ctxrow = (B * T) // C
    return pl.pallas_call(
        _ctx_attn_kernel,
        grid=(B,),
        in_specs=[pl.BlockSpec(memory_space=pltpu.SMEM),
                  pl.BlockSpec((C, qkv.shape[1]), lambda b: (ctxrow + b, 0))],
        out_specs=pl.BlockSpec((C, 8 * LANES), lambda b: (b, 0)),
        out_shape=jax.ShapeDtypeStruct((B * C, 8 * LANES), BF16),
        compiler_params=_cparams(1),
        name="context_attention",
    )(sink.astype(F32), qkv)


S5_Q = 16
CONV_TILE = 256
CONV_HALO = 16


def _ssm_inproj_kernel(x_ref, y1_ref, y2_ref, rt_ref, g2_ref, g_ref, sh_ref, sc_ref, w_ref,
                       xo_ref, z_ref, xbc_ref, u_ref, uj_ref, dt_ref, h_scr, u_scr):
    xn = _moe_residual(x_ref, y1_ref, y2_ref, rt_ref, g2_ref)
    xo_ref[...] = xn
    h_scr[...] = _ada_norm(xn, g_ref[...], sh_ref[...], sc_ref[...]).astype(BF16)

    def mm(c0, n):
        return jnp.dot(h_scr[...], w_ref[:, c0:c0 + n], preferred_element_type=F32)

    nz, nxbc, nu = z_ref.shape[1], xbc_ref.shape[1], u_ref.shape[1]
    for c0 in range(0, nz, MXU_W):
        z_ref[:, c0:c0 + MXU_W] = mm(c0, MXU_W).astype(BF16)
    for c0 in range(0, nxbc, MXU_W):
        xbc_ref[:, c0:c0 + MXU_W] = mm(nz + c0, MXU_W).astype(BF16)
    for c0 in range(0, nu, MXU_W):
        y = mm(nz + nxbc + c0, MXU_W)
        u_ref[:, c0:c0 + MXU_W] = y.astype(BF16)
        for t in range(MXU_W // LANES):
            u_scr[c0 // LANES + t] = y[:, t * LANES:(t + 1) * LANES]
    dt_ref[...] = mm(nz + nxbc + nu, LANES)
    nchunk = u_scr.shape[1] // S5_Q
    for j in range(S5_Q):
        for t in range(u_scr.shape[0]):
            uj_ref[j, :, t * LANES:(t + 1) * LANES] = u_scr[t, pl.ds(j, nchunk, stride=S5_Q), :].astype(BF16)


def _ssm_inproj(rw, xall, y1, y2, route, prev_mods, mods, norm_g, w_in):
    D, tm = rw.D, rw.tm
    w = jnp.concatenate([w_in[:, 0:2560], w_in[:, 2592:3104], w_in[:, 2560:2592], jnp.zeros((D, LANES - 32), F32)],
                        axis=1).astype(BF16)
    row = lambda i: (i, 0)
    return pl.pallas_call(
        _ssm_inproj_kernel,
        grid=(rw.ntot,),
        in_specs=[pl.BlockSpec((tm, D), row), pl.BlockSpec((tm, D), row), pl.BlockSpec((tm, D), row),
                  pl.BlockSpec((tm, LANES), row), _mod_spec(rw, 5), pl.BlockSpec((1, D), lambda i: (0, 0)),
                  _mod_spec(rw, 0), _mod_spec(rw, 1), pl.BlockSpec((D, 3200), lambda i: (0, 0))],
        out_specs=[pl.BlockSpec((tm, D), row),
                   pl.BlockSpec((tm, 1024), row), pl.BlockSpec((tm, 1536), row), pl.BlockSpec((tm, 512), row),
                   pl.BlockSpec((S5_Q, tm // S5_Q, 512), lambda i: (0, i, 0)), pl.BlockSpec((tm, LANES), row)],
        out_shape=[jax.ShapeDtypeStruct((rw.rows, D), F32),
                   jax.ShapeDtypeStruct((rw.rows, 1024), BF16), jax.ShapeDtypeStruct((rw.rows, 1536), BF16),
                   jax.ShapeDtypeStruct((rw.rows, 512), BF16),
                   jax.ShapeDtypeStruct((S5_Q, rw.rows // S5_Q, 512), BF16),
                   jax.ShapeDtypeStruct((rw.rows, LANES), F32)],
        scratch_shapes=[pltpu.VMEM((tm, D), BF16), pltpu.VMEM((512 // LANES, tm, LANES), F32)],
        compiler_params=_cparams(1),
        name="ssm_inproj",
    )(xall, y1, y2, route, prev_mods, norm_g[None, :], mods, mods, w)


def _softplus(x):
    return jnp.maximum(x, 0.0) + jnp.log(1.0 + jnp.exp(-jnp.abs(x)))


def _conv_kernel(lat_tiles, tpb, cpb, x_ref, pv_ref, nx_ref, w_ref, b_ref, dtr_ref, dtb_ref, act_ref, dt_ref):
    i = pl.program_id(0)
    is_lat = i < lat_tiles
    pos = jnp.where(is_lat, i % tpb, (i - lat_tiles) % cpb)
    last_pos = jnp.where(is_lat, tpb - 1, cpb - 1)
    x = x_ref[...].astype(F32)
    tc = x.shape[0]
    prev_row = jnp.where(pos == 0, 0.0, pv_ref[...].astype(F32)[CONV_HALO - 1:CONV_HALO, :])
    next_row = jnp.where(pos == last_pos, 0.0, nx_ref[...].astype(F32)[0:1, :])
    row = lax.broadcasted_iota(jnp.int32, x.shape, 0)
    xm1 = jnp.where(row == 0, prev_row, pltpu.roll(x, 1, 0))
    xp1 = jnp.where(row == tc - 1, next_row, pltpu.roll(x, tc - 1, 0))
    y = w_ref[0:1, :] * xm1 + w_ref[1:2, :] * x + w_ref[2:3, :] * xp1 + b_ref[...]
    act_ref[...] = _silu(y).astype(BF16)
    sp = _softplus(dtr_ref[...] + dtb_ref[...])
    dt_ref[0] = sp
    dt_ref[1] = pltpu.roll(sp, LANES - 16, 1)


def _ssm_conv(rw, xbc, dtr, conv_w, conv_b, dt_bias):
    B, T, C = rw.B, rw.T, rw.C
    tc = CONV_TILE
    assert T % tc == 0 and C % tc == 0
    lat_tiles, tpb, cpb = (B * T) // tc, T // tc, C // tc
    ntiles = rw.rows // tc
    hpt = tc // CONV_HALO
    nhalo = rw.rows // CONV_HALO
    W = xbc.shape[1]
    dtb = jnp.concatenate([dt_bias.reshape(-1), jnp.zeros((LANES - 32,), F32)])[None, :]
    row = lambda i: (i, 0)
    return pl.pallas_call(
        functools.partial(_conv_kernel, lat_tiles, tpb, cpb),
        grid=(ntiles,),
        in_specs=[pl.BlockSpec((tc, W), row),
                  pl.BlockSpec((CONV_HALO, W), lambda i: (jnp.maximum(i * hpt - 1, 0), 0)),
                  pl.BlockSpec((CONV_HALO, W), lambda i: (jnp.minimum((i + 1) * hpt, nhalo - 1), 0)),
                  pl.BlockSpec((3, W), lambda i: (0, 0)), pl.BlockSpec((1, W), lambda i: (0, 0)),
                  pl.BlockSpec((tc, LANES), row), pl.BlockSpec((1, LANES), lambda i: (0, 0))],
        out_specs=[pl.BlockSpec((tc, W), row), pl.BlockSpec((2, tc, LANES), lambda i: (0, i, 0))],
        out_shape=[jax.ShapeDtypeStruct((rw.rows, W), BF16), jax.ShapeDtypeStruct((2, rw.rows, LANES), F32)],
        compiler_params=_cparams(1),
        name="ssm_conv",
    )(xbc, xbc, xbc, conv_w, conv_b[None, :], dtr, dtb)


def _ssd_kernel(nb, *refs):
    acts, dts = refs[0:2 * nb], refs[2 * nb:4 * nb]
    tri_ref, a_ref, yf_ref, yb_ref, hst = refs[4 * nb:]

    @pl.when(pl.program_id(0) == 0)
    def _():
        hst[...] = jnp.zeros_like(hst)

    for d, y_ref in enumerate((yf_ref, yb_ref)):
        for b in range(nb):
            _ssd_chunk(acts[d * nb + b], dts[d * nb + b], tri_ref[d], a_ref[d], y_ref.at[b], hst.at[d, b])


def _ssd_chunk(act_ref, dt_ref, tri, avec, y_ref, hst):
    q = SSD_CHUNK
    dt = dt_ref[...]
    da = dt * avec
    acs = jnp.dot(tri, da, preferred_element_type=F32, precision=lax.Precision.HIGHEST)
    tot = jnp.sum(da, axis=0, keepdims=True)
    acs_t = acs.T
    dt_t = dt.T
    eacs = jnp.exp(acs)
    wend = jnp.exp(tot - acs) * dt
    etot = jnp.exp(tot)
    mask = tri > 0.5
    left = lax.broadcasted_iota(jnp.int32, (q, LANES), 1) < HEAD_DIM
    left1 = lax.broadcasted_iota(jnp.int32, (1, LANES), 1) < HEAD_DIM
    for g in range(2):
        bg = act_ref[:, 1024 + g * 128:1152 + g * 128]
        cg = act_ref[:, 1280 + g * 128:1408 + g * 128]
        cb = _nt(cg, bg)
        hin = hst[:, g * 512:(g + 1) * 512]
        yoff = jnp.dot(cg, hin.astype(BF16), preferred_element_type=F32)
        xw, dec = [], []
        for pr in range(4):
            h_a = g * 8 + pr * 2
            h_b = h_a + 1
            c0 = h_a * HEAD_DIM
            x2 = act_ref[:, c0:c0 + LANES]
            outs = []
            for h in (h_a, h_b):
                seg = acs[:, h:h + 1] - acs_t[h:h + 1, :]
                w = cb * jnp.exp(jnp.where(mask, seg, NEG_INF)) * dt_t[h:h + 1, :]
                outs.append(jnp.dot(w.astype(BF16), x2, preferred_element_type=F32))
            yd = jnp.where(left, outs[0], outs[1])
            sc = jnp.where(left, eacs[:, h_a:h_a + 1], eacs[:, h_b:h_b + 1])
            y_ref[:, c0:c0 + LANES] = (yd + yoff[:, pr * LANES:(pr + 1) * LANES] * sc).astype(BF16)
            wsc = jnp.where(left, wend[:, h_a:h_a + 1], wend[:, h_b:h_b + 1])
            xw.append((x2.astype(F32) * wsc).astype(BF16))
            dec.append(jnp.where(left1, etot[:, h_a:h_a + 1], etot[:, h_b:h_b + 1]))
        bg_t = bg.astype(F32).T.astype(BF16)
        snew = jnp.dot(bg_t, jnp.concatenate(xw, axis=1), preferred_element_type=F32)
        hst[:, g * 512:(g + 1) * 512] = hin * jnp.concatenate(dec, axis=1) + snew


def _ssd(rw, act, dt2, a_log):
    B, T, C = rw.B, rw.T, rw.C
    q = SSD_CHUNK
    nct, nlt = C // q, T // q
    ctx0 = (B * T) // q
    r = np.arange(q)
    tri = jnp.asarray(np.stack([r[None, :] <= r[:, None], r[None, :] >= r[:, None]]).astype(np.float32))
    avec = jnp.concatenate([-jnp.exp(a_log.astype(F32)), jnp.zeros((2, LANES - a_log.shape[1]), F32)], axis=1)[:, None, :]

    def lat(d, s):
        return jnp.clip(s - nct, 0, nlt - 1) if d == 0 else nlt - 1 - jnp.clip(s - nct, 0, nlt - 1)

    def blk(d, b, s):
        kc = s if d == 0 else nct - 1 - s
        return jnp.where(s < nct, ctx0 + b * nct + kc, b * nlt + lat(d, s))

    pairs = [(d, b) for d in range(2) for b in range(B)]
    aspec = lambda d, b: pl.BlockSpec((q, act.shape[1]), lambda s: (blk(d, b, s), 0))
    dspec = lambda d, b: pl.BlockSpec((None, q, LANES), lambda s: (d, blk(d, b, s), 0))
    yspec = lambda d: pl.BlockSpec((B, q, 1024), lambda s: (0, lat(d, s), 0))
    yf, yb = pl.pallas_call(
        functools.partial(_ssd_kernel, B),
        grid=(nct + nlt,),
        in_specs=[aspec(d, b) for d, b in pairs] + [dspec(d, b) for d, b in pairs]
        + [pl.BlockSpec((2, q, q), lambda s: (0, 0, 0)), pl.BlockSpec((2, 1, LANES), lambda s: (0, 0, 0))],
        out_specs=[yspec(0), yspec(1)],
        out_shape=[jax.ShapeDtypeStruct((B, T, 1024), BF16)] * 2,
        scratch_shapes=[pltpu.VMEM((2, B, q, 1024), F32)],
        compiler_params=_cparams(1),
        name="ssd_scan",
    )(*([act] * (2 * B)), *([dt2] * (2 * B)), tri, avec)
    return yf.reshape(B * T, 1024), yb.reshape(B * T, 1024)


def _cmul(ar, ai, br, bi):
    return ar * br - ai * bi, ar * bi + ai * br


def _s5_weight_kernel(lre_ref, lim_ref, ls_ref, bre_ref, bim_ref, cre_ref, cim_ref,
                      wsr_ref, wsi_ref, wor_ref, woi_ref, bt_ref, are_ref, aim_ref):
    lre, lim = lre_ref[...], lim_ref[...]
    step = jnp.exp(ls_ref[...])
    er, ei = lre * step, lim * step
    npow = 24
    p = lax.broadcasted_iota(jnp.int32, (1, npow, 1), 1).astype(F32)
    mag = jnp.exp(p * er)
    pre, pim = mag * jnp.cos(p * ei), mag * jnp.sin(p * ei)
    a_re, a_im = pre[:, 1:2, :], pim[:, 1:2, :]
    den = lre * lre + lim * lim
    q_re = ((a_re - 1.0) * lre + a_im * lim) / den
    q_im = (a_im * lre - (a_re - 1.0) * lim) / den
    bb_re, bb_im = _cmul(q_re, q_im, bre_ref[...], bim_ref[...])
    c_re, c_im = cre_ref[...], cim_ref[...]
    ws_r, ws_i, wo_r, wo_i, ca_r, ca_i = [], [], [], [], [], []
    for t in range(S5_Q):
        r, i = _cmul(bb_re, bb_im, pre[:, t:t + 1, :], pim[:, t:t + 1, :])
        ws_r.append(r)
        ws_i.append(i)
        r, i = _cmul(c_re, c_im, pre[:, t:t + 1, :], pim[:, t:t + 1, :])
        ca_r.append(r)
        ca_i.append(i)
        r, i = _cmul(c_re, c_im, pre[:, t + 1:t + 2, :], pim[:, t + 1:t + 2, :])
        wo_r.append(r)
        wo_i.append(-i)
    cat = lambda xs: jnp.concatenate(xs, axis=1)
    is_fwd = pl.program_id(0) < pl.num_programs(0) // 2

    def packed(blocks, reverse_fwd):
        w = jnp.where(is_fwd if reverse_fwd else jnp.logical_not(is_fwd), cat(blocks[::-1]), cat(blocks))
        w2 = jnp.concatenate([w, w], axis=2)
        g = lax.broadcasted_iota(jnp.int32, w2.shape, 0)
        ln = lax.broadcasted_iota(jnp.int32, w2.shape, 2)
        return jnp.where((g % 2 == 0) == (ln < w.shape[2]), w2, 0.0).astype(BF16)

    wsr_ref[...] = packed(ws_r, True)
    wsi_ref[...] = packed(ws_i, True)
    wor_ref[...] = packed(wo_r, False)
    woi_ref[...] = packed(wo_i, False)
    bdot = lambda a, b: lax.dot_general(a, b, (((2,), (2,)), ((0,), (0,))), preferred_element_type=F32,
                                        precision=lax.Precision.HIGHEST)
    kin = jnp.where(is_fwd, bdot(bb_re, cat(ca_r)) - bdot(bb_im, cat(ca_i)),
                    bdot(bb_re, cat(ca_r[::-1])) - bdot(bb_im, cat(ca_i[::-1])))
    qc = kin.shape[2]
    lane = lax.broadcasted_iota(jnp.int32, kin.shape, 2)
    rows = []
    for j in range(S5_Q):
        fwd = jnp.where(lane >= j * S5_GROUP, pltpu.roll(kin, j * S5_GROUP, 2), 0.0)
        back = (S5_Q - 1 - j) * S5_GROUP
        bwd = jnp.where(lane < qc - back, pltpu.roll(kin, (qc - back) % qc, 2), 0.0)
        rows.append(jnp.where(is_fwd, fwd, bwd))
    bt_ref[...] = cat(rows).astype(BF16)
    are_ref[...] = pre[:, S5_Q:S5_Q + 1, :]
    aim_ref[...] = pim[:, S5_Q:S5_Q + 1, :]


def _s5_weights(lam_re, lam_im, log_step, b_re, b_im, c_re, c_im):
    nd, ng, ns = lam_re.shape
    G = nd * ng
    ch = S5_GROUP
    gb = 8
    qc = S5_Q * ch
    f = lambda a: a.astype(F32)
    args = (f(lam_re).reshape(G, 1, ns), f(lam_im).reshape(G, 1, ns), f(log_step).reshape(G, 1, 1),
            f(b_re).reshape(G, ns, ch).transpose(0, 2, 1), f(b_im).reshape(G, ns, ch).transpose(0, 2, 1),
            f(c_re).reshape(G, ch, ns), f(c_im).reshape(G, ch, ns))
    spec = lambda a: pl.BlockSpec((gb,) + a.shape[1:], lambda i: (i, 0, 0))
    assert nd == 2 and (G // gb) % 2 == 0
    oshape = [jax.ShapeDtypeStruct((G, qc, 2 * ns), BF16)] * 4 + [jax.ShapeDtypeStruct((G, qc, qc), BF16)] \
        + [jax.ShapeDtypeStruct((G, 1, ns), F32)] * 2
    ws_r, ws_i, wo_r, wo_i, bt, a_re, a_im = pl.pallas_call(
        _s5_weight_kernel,
        grid=(G // gb,),
        in_specs=[spec(a) for a in args],
        out_specs=[pl.BlockSpec((gb,) + s.shape[1:], lambda i: (i, 0, 0)) for s in oshape],
        out_shape=oshape,
        compiler_params=_cparams(1),
        name="s5_weights",
    )(*args)
    by_dir = lambda w: w.reshape((nd, ng) + w.shape[1:])
    pair = lambda a: a.reshape(nd, ng // 2, 1, 2 * ns)
    return by_dir(bt), by_dir(ws_r), by_dir(ws_i), by_dir(wo_r), by_dir(wo_i), pair(a_re), pair(a_im)


S5_GB = LANES // S5_GROUP


def _s5_kernel(B, nct, nlt, uj_ref, perm_ref, bt_ref, wsr_ref, wsi_ref, wor_ref, woi_ref, are_ref, aim_ref, yj_ref,
               x_scr, y_scr, s_re, s_im):
    gb, npair, qc = S5_GB, S5_GB // 2, S5_Q * S5_GROUP
    lhs = jnp.concatenate([uj_ref[j] for j in range(S5_Q)], axis=1)
    for m in range(gb):
        x_scr[:, m * qc:(m + 1) * qc] = jnp.dot(lhs, perm_ref[:, m * qc:(m + 1) * qc],
                                                preferred_element_type=F32).astype(BF16)
    xg = lambda g: x_scr[:, g * qc:(g + 1) * qc]
    for d in range(2):
        for pr in range(npair):
            for dst, w_ref in ((s_re, wsr_ref), (s_im, wsi_ref)):
                dst[d, pr] = (jnp.dot(xg(2 * pr), w_ref[d, 2 * pr], preferred_element_type=F32)
                              + jnp.dot(xg(2 * pr + 1), w_ref[d, 2 * pr + 1], preferred_element_type=F32))
    chains = [(d, pr, b) for d in range(2) for pr in range(npair) for b in range(B)]
    coef = {(d, pr): (are_ref[d, pr], aim_ref[d, pr]) for d in range(2) for pr in range(npair)}
    ctx0 = B * nlt

    def body(s, carry):
        in_ctx = s < nct
        rows = {}
        for d in range(2):
            kc = s if d == 0 else nct - 1 - s
            kl = s - nct if d == 0 else nlt - 1 - (s - nct)
            for b in range(B):
                rows[(d, b)] = pl.ds(jnp.where(in_ctx, ctx0 + b * nct + kc, b * nlt + kl), 1)
        contrib = [(s_re[d, pr, rows[(d, b)], :], s_im[d, pr, rows[(d, b)], :]) for d, pr, b in chains]
        new = []
        for (d, pr, b), (hr, hi), (sr, si) in zip(chains, carry, contrib):
            ar, ai = coef[(d, pr)]
            s_re[d, pr, rows[(d, b)], :] = hr
            s_im[d, pr, rows[(d, b)], :] = hi
            new.append((ar * hr - ai * hi + sr, ar * hi + ai * hr + si))
        return tuple(new)

    zero = jnp.zeros((1, LANES), F32)
    lax.fori_loop(0, nct + nlt, body, tuple((zero, zero) for _ in chains))
    for g in range(gb):
        acc = None
        for d in range(2):
            t = (jnp.dot(xg(g), bt_ref[d, g], preferred_element_type=F32)
                 + _nt(s_re[d, g // 2].astype(BF16), wor_ref[d, g])
                 + _nt(s_im[d, g // 2].astype(BF16), woi_ref[d, g]))
            acc = t if acc is None else acc + t
        y_scr[:, g * qc:(g + 1) * qc] = acc.astype(BF16)
    for i in range(S5_Q):
        yj_ref[i] = _nt(y_scr[...], perm_ref[i * LANES:(i + 1) * LANES, :]).astype(BF16)


def _s5(rw, uj, weights):
    B, T, C = rw.B, rw.T, rw.C
    bt, ws_r, ws_i, wo_r, wo_i, a_re, a_im = weights
    ng = bt.shape[1]
    q, gb = S5_Q, S5_GB
    nct, nlt = C // q, T // q
    nrow = uj.shape[1]
    qc = q * S5_GROUP
    k = gb * qc
    idx = np.arange(k)
    j, m, c = idx // LANES, (idx % LANES) // S5_GROUP, idx % S5_GROUP
    perm = np.zeros((k, k), np.float32)
    perm[idx, m * qc + j * S5_GROUP + c] = 1.0
    once = dict(pipeline_mode=pl.Buffered(1))
    wspec = lambda n: pl.BlockSpec((2, gb, qc, n), lambda i: (0, i, 0, 0), **once)
    aspec = pl.BlockSpec((2, gb // 2, 1, LANES), lambda i: (0, i, 0, 0))
    return pl.pallas_call(
        functools.partial(_s5_kernel, B, nct, nlt),
        grid=(ng // gb,),
        in_specs=[pl.BlockSpec((q, nrow, LANES), lambda i: (0, 0, i), **once),
                  pl.BlockSpec((k, k), lambda i: (0, 0), **once),
                  wspec(qc), wspec(LANES), wspec(LANES), wspec(LANES), wspec(LANES), aspec, aspec],
        out_specs=pl.BlockSpec((q, nrow, LANES), lambda i: (0, 0, i)),
        out_shape=jax.ShapeDtypeStruct(uj.shape, BF16),
        scratch_shapes=[pltpu.VMEM((nrow, k), BF16), pltpu.VMEM((nrow, k), BF16),
                        pltpu.VMEM((2, gb // 2, nrow, LANES), F32), pltpu.VMEM((2, gb // 2, nrow, LANES), F32)],
        compiler_params=_cparams(1),
        name="s5_scan",
    )(uj, jnp.asarray(perm, BF16), bt, ws_r, ws_i, wo_r, wo_i, a_re, a_im)


def _gelu_tanh(x):
    return 0.5 * x * (1.0 + jnp.tanh(math.sqrt(2.0 / math.pi) * (x + 0.044715 * (x * x * x))))


def _ssm_outproj_kernel(y0_ref, y1_ref, xs_ref, z_ref, v_ref, u_ref, dsk_ref, nw_ref, s5d_ref, gw_ref, gb_ref,
                        x_ref, w_ref, g1_ref, gn_ref, sh2_ref, sc2_ref, wr_ref, br_ref, lt_ref,
                        xo_ref, h2_ref, rt_ref, cnt_ref, carry, v_scr):
    i = pl.program_id(0)
    y = y0_ref[...].astype(F32) + y1_ref[...].astype(F32) + dsk_ref[...] * xs_ref[...].astype(F32)
    y = _rms(y * _silu(z_ref[...].astype(F32))) * nw_ref[...]
    ntile = v_scr.shape[0]
    nchunk = v_scr.shape[1] // S5_Q
    for j in range(S5_Q):
        for t in range(ntile):
            v_scr[t, pl.ds(j, nchunk, stride=S5_Q), :] = v_ref[j, :, t * LANES:(t + 1) * LANES].astype(F32)
    s5_y = jnp.concatenate([v_scr[t] for t in range(ntile)], axis=1)
    v = _gelu_tanh(s5_y + s5d_ref[...] * u_ref[...].astype(F32))
    v = v * _sigmoid(jnp.dot(v.astype(BF16), gw_ref[...], preferred_element_type=F32) + gb_ref[...])
    mix = jnp.concatenate([y, v], axis=1).astype(BF16)
    yo = jnp.dot(mix, w_ref[...], preferred_element_type=F32)
    _post_mixer(i, x_ref[...], yo, g1_ref[...], gn_ref[...], sh2_ref[...], sc2_ref[...], wr_ref, br_ref, lt_ref,
                xo_ref, h2_ref, rt_ref, cnt_ref, carry)


def _ssm_outproj(rw, ssd_y, act, z, s5_y, u, d_skip, norm_w, s5_d, glu_w, glu_b, xall, w_out, mods, norm_ffn, wr, br):
    D, tm = rw.D, rw.tm
    ntiles = rw.nlat
    post_in, post_out = _post_specs(rw)
    row = lambda i: (i, 0)
    vec = lambda n: pl.BlockSpec((1, n), lambda i: (0, 0))
    dsk = jnp.repeat(d_skip.astype(F32), HEAD_DIM)[None, :]
    return pl.pallas_call(
        _ssm_outproj_kernel,
        grid=(ntiles,),
        in_specs=[pl.BlockSpec((tm, 1024), row), pl.BlockSpec((tm, 1024), row),
                  pl.BlockSpec((tm, 1024), row), pl.BlockSpec((tm, 1024), row),
                  pl.BlockSpec((S5_Q, tm // S5_Q, 512), lambda i: (0, i, 0)),
                  pl.BlockSpec((tm, 512), row), vec(1024), vec(1024), vec(512),
                  pl.BlockSpec((512, 512), lambda i: (0, 0)), vec(512),
                  pl.BlockSpec((tm, D), row), pl.BlockSpec((1536, D), lambda i: (0, 0)), _mod_spec(rw, 2)] + post_in,
        out_specs=post_out,
        out_shape=_post_shapes(ntiles * tm, D),
        scratch_shapes=[pltpu.VMEM((1, LANES), F32), pltpu.VMEM((512 // LANES, tm, LANES), F32)],
        compiler_params=_cparams(1),
        name="ssm_outproj_router",
    )(ssd_y[0], ssd_y[1], act, z, s5_y, u, dsk, norm_w[None, :], s5_d[None, :], glu_w.astype(BF16), glu_b[None, :],
      xall, w_out.astype(BF16), mods, norm_ffn[None, :], mods, mods, wr, br, _lower_tri(tm))


def kernel(x, c, ctx, c_ctx, mod_w, mod_b, norm_mix, norm_ffn, att_w_in, att_w_out, na_q_norm, na_k_norm, na_rel_bias, wa_q_norm, wa_k_norm, wa_sink, ssm_w_in, ssm_w_out, ssd_conv_w, ssd_conv_b, ssd_dt_bias, ssd_a_log, ssd_d, ssd_norm, s5_lambda_re, s5_lambda_im, s5_log_step, s5_b_re, s5_b_im, s5_c_re, s5_c_im, s5_d, s5_glu_w, s5_glu_b, moe_w_group, moe_b_group, moe_w_expert, moe_b_expert, moe_w13, moe_w2):
    B, T, D = x.shape
    C = ctx.shape[1]
    rw = _Rows(B, T, C, D, ROW_TILE)
    xl = x.reshape(B * T, D)
    xc = ctx.reshape(B * C, D)
    cm = jnp.concatenate([c, c_ctx[None, :], jnp.zeros((8 - B - 1, D), F32)], axis=0)
    mods = _modulation(cm, mod_w, mod_b)
    mods = mods.reshape(mods.shape[0], 8, 1, 6 * D)

    m0 = mods[0]
    qkv = _att_inproj(rw, xl, xc, m0, norm_mix[0], att_w_in[0], na_q_norm[0], na_k_norm[0], wa_q_norm[0],
                      wa_k_norm[0])
    na = _na_attention(rw, qkv, na_rel_bias[0])
    wa = _wa_attention(rw, qkv, wa_sink[0])
    cx = _ctx_attention(rw, qkv, wa_sink[0])
    wr, br = _router_weights(moe_w_group[0], moe_b_group[0], moe_w_expert[0], moe_b_expert[0])
    xall, h2, route, counts = _att_outproj(rw, na, wa, cx, xl, xc, att_w_out[0], m0, norm_ffn[0], wr, br)
    y1, y2 = _moe(h2, route, counts, moe_w13, moe_w2, 0)

    m1 = mods[1]
    xall, z, xbc, u, uj, dtr = _ssm_inproj(rw, xall, y1, y2, route, m0, m1, norm_mix[1], ssm_w_in[0])
    act, dt2 = _ssm_conv(rw, xbc, dtr, ssd_conv_w[0], ssd_conv_b[0], ssd_dt_bias[0])
    ssd_y = _ssd(rw, act, dt2, ssd_a_log[0])
    s5_w = _s5_weights(s5_lambda_re[0], s5_lambda_im[0], s5_log_step[0], s5_b_re[0], s5_b_im[0], s5_c_re[0],
                       s5_c_im[0])
    s5_y = _s5(rw, uj, s5_w)
    wr, br = _router_weights(moe_w_group[1], moe_b_group[1], moe_w_expert[1], moe_b_expert[1])
    xlat, h2, route, counts = _ssm_outproj(rw, ssd_y, act, z, s5_y, u, ssd_d[0], ssd_norm[0], s5_d[0], s5_glu_w[0],
                                           s5_glu_b[0], xall, ssm_w_out[0], m1, norm_ffn[1], wr, br)
    y1, y2 = _moe(h2, route, counts, moe_w13, moe_w2, 1)
    out = _combine(rw, rw.nlat, xlat, y1, y2, route, m1)
    return out.reshape(B, T, D)
```

```python
import functools
import math

import jax
import jax.numpy as jnp
import numpy as np
from jax import lax
from jax.experimental import pallas as pl
from jax.experimental.pallas import tpu as pltpu

F32 = jnp.float32
BF16 = jnp.bfloat16

EPS = 1e-6
NEG_INF = -1e30
GRID_W = 64
HEAD_DIM = 64
NA_KH = 8
NA_KW = 16
WA_BLOCK = 128
ROPE_BASE = 10000.0
SSD_CHUNK = 128
S5_GROUP = 16
MOE_GROUPS = 4
MOE_EPG = 8
MOE_EXPERTS = MOE_GROUPS * MOE_EPG

LANES = 128
ROW_TILE = 512
MXU_W = 256
MOE_TILE = MXU_W
VMEM_LIMIT = 56 * 1024 * 1024
MOE_VMEM_LIMIT = 60 * 1024 * 1024


def _cparams(n_axes, vmem=VMEM_LIMIT):
    return pltpu.CompilerParams(dimension_semantics=("arbitrary",) * n_axes, vmem_limit_bytes=vmem)


def _sigmoid(x):
    return jax.nn.sigmoid(x)


def _silu(x):
    return x * _sigmoid(x)


def _rms(x, eps=EPS):
    return x * lax.rsqrt(jnp.mean(x * x, axis=-1, keepdims=True) + eps)


def _ada_norm(x, g, shift, scale):
    return (_rms(x) * g) * (1.0 + scale) + shift


def _mod_kernel(c_ref, w_ref, b_ref, o_ref):
    a = _silu(c_ref[...])
    o_ref[...] = jnp.dot(a, w_ref[...], preferred_element_type=F32, precision=lax.Precision.HIGHEST) + b_ref[...]


def _modulation(cm, mod_w, mod_b):
    depth, d, n6 = mod_w.shape
    tn = 1024
    return pl.pallas_call(
        _mod_kernel,
        grid=(depth, n6 // tn),
        in_specs=[pl.BlockSpec((8, d), lambda l, j: (0, 0)),
                  pl.BlockSpec((None, d, tn), lambda l, j: (l, 0, j)),
                  pl.BlockSpec((None, 1, tn), lambda l, j: (l, 0, j))],
        out_specs=pl.BlockSpec((None, 8, tn), lambda l, j: (l, 0, j)),
        out_shape=jax.ShapeDtypeStruct((depth, 8, n6), F32),
        compiler_params=_cparams(2),
        name="modulation",
    )(cm, mod_w, mod_b.reshape(depth, 1, n6))


class _Rows:
    def __init__(self, B, T, C, D, tm):
        assert T % tm == 0 and (B * C) % tm == 0
        self.B, self.T, self.C, self.D, self.tm = B, T, C, D, tm
        self.tpb = T // tm
        self.nlat = B * self.tpb
        self.nctx = (B * C) // tm
        self.ntot = self.nlat + self.nctx
        self.rows = B * (T + C)

    def group(self, i):
        return jnp.where(i < self.nlat, i // self.tpb, self.B)


def _mod_spec(rw, col):
    return pl.BlockSpec((None, 1, rw.D), lambda i, *_: (rw.group(i), 0, col))


def _seg_norm(y, seg, gcol):
    ss = jnp.dot((y * y).astype(BF16), seg, preferred_element_type=F32)
    return y * lax.rsqrt(ss + EPS) * gcol


def _rope(y, cos, sin):
    w = y.shape[-1]
    lane = lax.broadcasted_iota(jnp.int32, y.shape, 1)
    first = (lane % 32) < 16
    partner = jnp.where(first, pltpu.roll(y, w - 16, 1), pltpu.roll(y, 16, 1))
    return y * cos + partner * sin


def _dup_halves(k):
    lane = lax.broadcasted_iota(jnp.int32, k.shape, 1)
    sw = pltpu.roll(k, 64, 1)
    return jnp.where(lane < 64, k, sw), jnp.where(lane < 64, sw, k)


def _att_inproj_kernel(nlat, xl_ref, xc_ref, g_ref, sh_ref, sc_ref, w_ref, gcol_ref, cos_ref, sin_ref, seg_ref,
                       o_ref, h_scr):
    i = pl.program_id(0)
    x = jnp.where(i < nlat, xl_ref[...], xc_ref[...])
    h_scr[...] = _ada_norm(x, g_ref[...], sh_ref[...], sc_ref[...]).astype(BF16)
    seg = seg_ref[...]
    cos2 = jnp.concatenate([cos_ref[...], cos_ref[...]], axis=1)
    sin2 = jnp.concatenate([sin_ref[...], sin_ref[...]], axis=1)
    for c in range(w_ref.shape[1] // MXU_W):
        c0 = c * MXU_W
        cols = slice(c0, c0 + MXU_W)
        y = jnp.dot(h_scr[...], w_ref[:, cols], preferred_element_type=F32)
        gcol = gcol_ref[:, cols]
        if c in (0, 1, 2, 3):
            o_ref[:, cols] = _seg_norm(y, seg, gcol).astype(BF16)
        elif c in (4, 5):
            o_ref[:, cols] = y.astype(BF16)
        elif c in (6, 7):
            o_ref[:, cols] = _rope(_seg_norm(y, seg, gcol), cos2, sin2).astype(BF16)
        else:
            lane = lax.broadcasted_iota(jnp.int32, y.shape, 1)
            yk = jnp.where(lane < LANES, _seg_norm(y, seg, gcol), y)
            yr = jnp.where(lane < LANES, _rope(yk, cos2, sin2), yk)
            k0, k1 = _dup_halves(yr[:, :LANES])
            v0, v1 = _dup_halves(yr[:, LANES:])
            for t, dup in enumerate((k0, k1, v0, v1)):
                o_ref[:, c0 + t * LANES:c0 + (t + 1) * LANES] = dup.astype(BF16)


def _rope_tables(T, tm):
    t = np.arange(T)
    d = np.arange(HEAD_DIM)
    nf = HEAD_DIM // 4
    inv = jnp.asarray(ROPE_BASE, F32) ** (-jnp.arange(nf, dtype=F32) / nf)
    pos = np.where((d // 32 == 0)[None, :], (t // GRID_W)[:, None], (t % GRID_W)[:, None])
    ang = jnp.asarray(pos, F32) * inv[d % nf][None, :]
    sign = np.where((d % 32) < 16, -1.0, 1.0).astype(np.float32)
    cos = jnp.cos(ang)
    sin = jnp.sin(ang) * sign[None, :]
    cos = jnp.concatenate([cos, jnp.ones((tm, HEAD_DIM), F32)], axis=0)
    sin = jnp.concatenate([sin, jnp.zeros((tm, HEAD_DIM), F32)], axis=0)
    return jnp.tile(cos, (1, 2)), jnp.tile(sin, (1, 2))


def _att_inproj(rw, xl, xc, mods, norm_g, w_in, na_qn, na_kn, wa_qn, wa_kn):
    D, tm = rw.D, rw.tm
    scale = HEAD_DIM ** -0.5
    gcol = jnp.concatenate([jnp.tile(na_qn * scale, 8), jnp.tile(na_kn, 8), jnp.ones((512,), F32),
                            jnp.tile(wa_qn * scale, 8), jnp.tile(wa_kn, 2), jnp.ones((128,), F32)])[None, :]
    cos, sin = _rope_tables(rw.T, tm)
    segn = np.arange(256) // 64
    seg = jnp.asarray((segn[:, None] == segn[None, :]).astype(np.float32) / 64.0, BF16)
    nlat, tpb = rw.nlat, rw.tpb
    return pl.pallas_call(
        functools.partial(_att_inproj_kernel, nlat),
        grid=(rw.ntot,),
        in_specs=[pl.BlockSpec((tm, D), lambda i: (jnp.minimum(i, nlat - 1), 0)),
                  pl.BlockSpec((tm, D), lambda i: (jnp.maximum(i - nlat, 0), 0)),
                  pl.BlockSpec((1, D), lambda i: (0, 0)),
                  _mod_spec(rw, 0), _mod_spec(rw, 1),
                  pl.BlockSpec((D, 2304), lambda i: (0, 0)),
                  pl.BlockSpec((1, 2304), lambda i: (0, 0)),
                  pl.BlockSpec((tm, 128), lambda i: (jnp.where(i < nlat, i % tpb, tpb), 0)),
                  pl.BlockSpec((tm, 128), lambda i: (jnp.where(i < nlat, i % tpb, tpb), 0)),
                  pl.BlockSpec((256, 256), lambda i: (0, 0))],
        out_specs=pl.BlockSpec((tm, 2560), lambda i: (i, 0)),
        out_shape=jax.ShapeDtypeStruct((rw.rows, 2560), BF16),
        scratch_shapes=[pltpu.VMEM((tm, D), BF16)],
        compiler_params=_cparams(1),
        name="att_inproj",
    )(xl, xc, norm_g[None, :], mods, mods, w_in.astype(BF16), gcol, cos, sin, seg)


def _route(lg, lt, carry):
    lane = lax.broadcasted_iota(jnp.int32, lg.shape, 1).astype(F32)
    gm = lane < MOE_GROUPS
    mg = jnp.max(jnp.where(gm, lg, NEG_INF), axis=-1, keepdims=True)
    eg = jnp.where(gm, jnp.exp(jnp.where(gm, lg, NEG_INF) - mg), 0.0)
    pg = eg / jnp.sum(eg, axis=-1, keepdims=True)
    ptop = jnp.max(pg, axis=-1, keepdims=True)
    gidx = jnp.min(jnp.where(gm & (pg == ptop), lane, 1e9), axis=-1, keepdims=True)
    lo = MOE_GROUPS + MOE_EPG * gidx
    em = (lane >= lo) & (lane < lo + MOE_EPG)
    le = jnp.where(em, lg, NEG_INF)
    ee = jnp.where(em, jnp.exp(le - jnp.max(le, axis=-1, keepdims=True)), 0.0)
    pe = ee / jnp.sum(ee, axis=-1, keepdims=True)
    v1 = jnp.max(jnp.where(em, pe, -1.0), axis=-1, keepdims=True)
    i1 = jnp.min(jnp.where(em & (pe == v1), lane, 1e9), axis=-1, keepdims=True)
    em2 = em & (lane != i1)
    v2 = jnp.max(jnp.where(em2, pe, -1.0), axis=-1, keepdims=True)
    i2 = jnp.min(jnp.where(em2 & (pe == v2), lane, 1e9), axis=-1, keepdims=True)
    den = v1 + v2
    w1 = v1 / den * ptop
    w2 = v2 / den * ptop
    e1 = i1 - MOE_GROUPS
    e2 = i2 - MOE_GROUPS
    m1 = lane == e1
    m2 = lane == e2
    oh = jnp.where(m1 | m2, 1.0, 0.0)
    cnt = jnp.dot(lt, oh.astype(BF16), preferred_element_type=F32) + carry
    r1 = jnp.sum(jnp.where(m1, cnt, 0.0), axis=-1, keepdims=True)
    r2 = jnp.sum(jnp.where(m2, cnt, 0.0), axis=-1, keepdims=True)
    route = jnp.where(lane == 0, e1, jnp.where(lane == 1, e2, jnp.where(lane == 2, w1, jnp.where(
        lane == 3, w2, jnp.where(lane == 4, r1, jnp.where(lane == 5, r2, 0.0))))))
    return route, carry + jnp.sum(oh, axis=0, keepdims=True)


def _post_mixer(i, x, y, g1, gn, sh2, sc2, wr_ref, br_ref, lt_ref, xo_ref, h2_ref, rt_ref, cnt_ref, carry):
    xn = x + g1 * y
    xo_ref[...] = xn
    h2 = _ada_norm(xn, gn, sh2, sc2)
    hb = h2.astype(BF16)
    hbf = hb.astype(F32)
    half = h2.shape[1] // 2
    h2_ref[...] = pltpu.pack_elementwise([h2[:, :half], h2[:, half:]], packed_dtype=BF16)
    hl = (h2 - hbf).astype(BF16)
    lg = (jnp.dot(hb, wr_ref[0], preferred_element_type=F32)
          + (jnp.dot(hb, wr_ref[1], preferred_element_type=F32) + jnp.dot(hl, wr_ref[0], preferred_element_type=F32))
          + br_ref[...])

    @pl.when(i == 0)
    def _():
        carry[...] = jnp.zeros_like(carry)

    route, newc = _route(lg, lt_ref[...], carry[...])
    rt_ref[...] = route
    carry[...] = newc
    cnt_ref[...] = newc


def _att_outproj_kernel(nlat, na_ref, wa_ref, cx_ref, xl_ref, xc_ref, w_ref, g1_ref, gn_ref, sh2_ref, sc2_ref,
                        wr_ref, br_ref, lt_ref, xo_ref, h2_ref, rt_ref, cnt_ref, carry):
    i = pl.program_id(0)
    lat = i < nlat
    mix = jnp.where(lat, jnp.concatenate([na_ref[...], wa_ref[...]], axis=1), cx_ref[...])
    y = jnp.dot(mix, w_ref[...], preferred_element_type=F32)
    x = jnp.where(lat, xl_ref[...], xc_ref[...])
    _post_mixer(i, x, y, g1_ref[...], gn_ref[...], sh2_ref[...], sc2_ref[...], wr_ref, br_ref, lt_ref,
                xo_ref, h2_ref, rt_ref, cnt_ref, carry)


def _router_weights(w_group, b_group, w_expert, b_expert):
    D = w_group.shape[0]
    pad = LANES - MOE_GROUPS - MOE_EXPERTS
    wr = jnp.concatenate([w_group, w_expert, jnp.zeros((D, pad), F32)], axis=1)
    br = jnp.concatenate([b_group, b_expert, jnp.zeros((pad,), F32)])[None, :]
    hi = wr.astype(BF16)
    lo = (wr - hi.astype(F32)).astype(BF16)
    return jnp.stack([hi, lo]), br


def _lower_tri(tm):
    r = np.arange(tm)
    return jnp.asarray((r[None, :] < r[:, None]).astype(np.float32), BF16)


def _post_specs(rw):
    D, tm = rw.D, rw.tm
    return ([pl.BlockSpec((1, D), lambda i: (0, 0)), _mod_spec(rw, 3), _mod_spec(rw, 4),
             pl.BlockSpec((2, D, LANES), lambda i: (0, 0, 0)), pl.BlockSpec((1, LANES), lambda i: (0, 0)),
             pl.BlockSpec((tm, tm), lambda i: (0, 0))],
            [pl.BlockSpec((tm, D), lambda i: (i, 0)), pl.BlockSpec((tm, D // 2), lambda i: (i, 0)),
             pl.BlockSpec((tm, LANES), lambda i: (i, 0)), pl.BlockSpec((1, LANES), lambda i: (0, 0))])


def _post_shapes(nrows, D):
    return [jax.ShapeDtypeStruct((nrows, D), F32), jax.ShapeDtypeStruct((nrows, D // 2), jnp.uint32),
            jax.ShapeDtypeStruct((nrows, LANES), F32), jax.ShapeDtypeStruct((1, LANES), F32)]


def _att_outproj(rw, na, wa, cx, xl, xc, w_out, mods, norm_ffn, wr, br):
    D, tm, nlat = rw.D, rw.tm, rw.nlat
    post_in, post_out = _post_specs(rw)
    latmap = lambda i: (jnp.minimum(i, nlat - 1), 0)
    ctxmap = lambda i: (jnp.maximum(i - nlat, 0), 0)
    return pl.pallas_call(
        functools.partial(_att_outproj_kernel, nlat),
        grid=(rw.ntot,),
        in_specs=[pl.BlockSpec((tm, 512), latmap), pl.BlockSpec((tm, 512), latmap), pl.BlockSpec((tm, D), ctxmap),
                  pl.BlockSpec((tm, D), latmap), pl.BlockSpec((tm, D), ctxmap),
                  pl.BlockSpec((D, D), lambda i: (0, 0)), _mod_spec(rw, 2)] + post_in,
        out_specs=post_out,
        out_shape=_post_shapes(rw.rows, D),
        scratch_shapes=[pltpu.VMEM((1, LANES), F32)],
        compiler_params=_cparams(1),
        name="att_outproj_router",
    )(na, wa, cx, xl, xc, w_out.astype(BF16), mods, norm_ffn[None, :], mods, mods, wr, br, _lower_tri(tm))


def _moe_kernel(te_ref, nu_ref, src_ref, nsrc_ref, hp_ref, w13_ref, w2_ref, o_ref, w13b, w2b, xa, xb):
    i = pl.program_id(0)
    prev = te_ref[jnp.maximum(i - 1, 0)]
    changed = (i == 0) | (te_ref[i] != prev)
    tg = xa.shape[0]

    @pl.when(changed)
    def _():
        w13b[...] = w13_ref[...].astype(BF16)
        w2b[...] = w2_ref[...].astype(BF16)

    @pl.when(i == 0)
    def _():
        def fetch(j, carry):
            xa[pl.ds(j, 1), :] = hp_ref[pl.ds(src_ref[0, j], 1), :]
            return carry

        lax.fori_loop(0, tg, fetch, 0, unroll=8)

    def step(cur, nxt):
        for j in range(tg):
            nxt[pl.ds(j, 1), :] = hp_ref[pl.ds(nsrc_ref[0, j], 1), :]
        ff = w2b.shape[0]
        half = cur.shape[1]
        w = cur[...]
        unpack = functools.partial(pltpu.unpack_elementwise, packed_dtype=BF16, unpacked_dtype=F32)
        x_lo = unpack(w, index=0).astype(BF16)
        x_hi = unpack(w, index=1).astype(BF16)
        a13 = (jnp.dot(x_lo, w13b[:half, :], preferred_element_type=F32)
               + jnp.dot(x_hi, w13b[half:, :], preferred_element_type=F32))
        act = _silu(a13[:, :ff]) * a13[:, ff:]
        o_ref[...] = jnp.dot(act.astype(BF16), w2b[...], preferred_element_type=F32).astype(BF16)

    used = i < nu_ref[0]

    @pl.when(used & (i % 2 == 0))
    def _():
        step(xa, xb)

    @pl.when(used & (i % 2 == 1))
    def _():
        step(xb, xa)

    @pl.when(i >= nu_ref[0])
    def _():
        o_ref[...] = jnp.zeros_like(o_ref)


def _moe(h2p, route, counts, w13, w2, layer):
    N = h2p.shape[0]
    D = 2 * h2p.shape[1]
    _, E, _, F2 = w13.shape
    tg = MOE_TILE
    nt = (2 * N) // tg + E
    e = route[:, 0:2].astype(jnp.int32)
    rank = route[:, 4:6].astype(jnp.int32)
    cnt = counts[0, :E].astype(jnp.int32)
    ntile_e = (cnt + tg - 1) // tg
    tile_end = jnp.cumsum(ntile_e)
    offs = (tile_end - ntile_e) * tg
    onehot = (e[:, :, None] == jnp.arange(E, dtype=jnp.int32)).astype(jnp.int32)
    dest = jnp.sum(onehot * offs, axis=-1) + rank
    src = jnp.zeros((nt * tg,), jnp.int32).at[dest.reshape(-1)].set(
        jnp.repeat(jnp.arange(N, dtype=jnp.int32), 2), unique_indices=True, mode="promise_in_bounds")
    tile_id = jnp.arange(nt, dtype=jnp.int32)
    nu = tile_end[-1:].astype(jnp.int32)
    te = jnp.sum((tile_end[None, :] <= jnp.minimum(tile_id, nu[0] - 1)[:, None]).astype(jnp.int32), axis=1)
    te = jnp.minimum(te, E - 1)
    ys = pl.pallas_call(
        _moe_kernel,
        grid_spec=pltpu.PrefetchScalarGridSpec(
            num_scalar_prefetch=2,
            grid=(nt,),
            in_specs=[pl.BlockSpec((None, 1, tg), lambda i, te, nu: (i, 0, 0), memory_space=pltpu.SMEM),
                      pl.BlockSpec((None, 1, tg), lambda i, te, nu: (jnp.minimum(i + 1, nt - 1), 0, 0),
                                   memory_space=pltpu.SMEM),
                      pl.BlockSpec((N, D // 2), lambda i, te, nu: (0, 0), pipeline_mode=pl.Buffered(1)),
                      pl.BlockSpec((None, None, D, F2), lambda i, te, nu: (layer, te[i], 0, 0)),
                      pl.BlockSpec((None, None, F2 // 2, D), lambda i, te, nu: (layer, te[i], 0, 0))],
            out_specs=pl.BlockSpec((tg, D), lambda i, te, nu: (i, 0)),
            scratch_shapes=[pltpu.VMEM((D, F2), BF16), pltpu.VMEM((F2 // 2, D), BF16),
                            pltpu.VMEM((tg, D // 2), jnp.uint32), pltpu.VMEM((tg, D // 2), jnp.uint32)]),
        out_shape=jax.ShapeDtypeStruct((nt * tg, D), BF16),
        compiler_params=_cparams(1, vmem=MOE_VMEM_LIMIT),
        name="moe_experts",
    )(te, nu, src.reshape(nt, 1, tg), src.reshape(nt, 1, tg), h2p, w13, w2)
    pick = lambda k: ys.at[dest[:, k]].get(mode="promise_in_bounds")
    return pick(0), pick(1)


def _moe_residual(x_ref, y1_ref, y2_ref, rt_ref, g2_ref):
    rt = rt_ref[...]
    f = rt[:, 2:3] * y1_ref[...].astype(F32) + rt[:, 3:4] * y2_ref[...].astype(F32)
    return x_ref[...] + g2_ref[...] * f


def _combine_kernel(x_ref, y1_ref, y2_ref, rt_ref, g2_ref, o_ref):
    o_ref[...] = _moe_residual(x_ref, y1_ref, y2_ref, rt_ref, g2_ref)


def _combine(rw, ntiles, xall, y1, y2, route, mods):
    D, tm = rw.D, rw.tm
    row = lambda i: (i, 0)
    return pl.pallas_call(
        _combine_kernel,
        grid=(ntiles,),
        in_specs=[pl.BlockSpec((tm, D), row), pl.BlockSpec((tm, D), row), pl.BlockSpec((tm, D), row),
                  pl.BlockSpec((tm, LANES), row), _mod_spec(rw, 5)],
        out_specs=pl.BlockSpec((tm, D), row),
        out_shape=jax.ShapeDtypeStruct((ntiles * tm, D), F32),
        compiler_params=_cparams(1),
        name="moe_combine",
    )(xall, y1, y2, route, mods)


NA_QROWS = 8
NA_KROWS = 16


def _na_first_key_row(variant, a):
    return (max(a - 4, 0) + 4, a, min(a, 4))[variant]


def _na_key_lanes(row0):
    a = row0 // GRID_W
    starts = [_na_first_key_row(v, a) for v in range(3)]
    lo = (min(starts) * GRID_W) // LANES * LANES
    hi = -(-((max(starts) + NA_KH) * GRID_W) // LANES) * LANES
    return lo, hi


def _na_bias_tiles(rpb):
    H = rpb.shape[0]
    i = np.arange(GRID_W)
    c0 = np.clip(i - NA_KW // 2, 0, GRID_W - NA_KW)
    j = np.arange(GRID_W)
    colvalid = (j[None, :] >= c0[:, None]) & (j[None, :] < c0[:, None] + NA_KW)
    dc = np.clip(j[None, :] - i[:, None] + NA_KW - 1, 0, 2 * NA_KW - 2)
    onehot = ((dc[None] == np.arange(2 * NA_KW - 1)[:, None, None]) & colvalid[None]).astype(np.float32)
    tiles = jnp.einsum('hrc,cij->hrij', rpb.astype(F32), jnp.asarray(onehot), precision=lax.Precision.HIGHEST)
    tiles = tiles + jnp.asarray(np.where(colvalid, 0.0, NEG_INF).astype(np.float32))
    return tiles.transpose(0, 2, 1, 3).reshape(H, GRID_W, (2 * NA_KH - 1) * GRID_W)


def _na_fill_bias(variant, tiles_ref, bias_scr):
    for hh in range(2):
        for a in range(NA_QROWS):
            start = _na_first_key_row(variant, a)
            dr0 = start - a + 3
            rows = slice(a * GRID_W, (a + 1) * GRID_W)
            w0, w1 = start * GRID_W, (start + NA_KH) * GRID_W
            if w0 > 0:
                bias_scr[hh, rows, 0:w0] = jnp.full((GRID_W, w0), NEG_INF, F32)
            bias_scr[hh, rows, w0:w1] = tiles_ref[hh, :, dr0 * GRID_W:(dr0 + NA_KH) * GRID_W]
            if w1 < NA_KROWS * GRID_W:
                bias_scr[hh, rows, w1:] = jnp.full((GRID_W, NA_KROWS * GRID_W - w1), NEG_INF, F32)


def _softmax_pv(parts, extra=None, rc=64):
    m_rows = parts[0][0].shape[0]
    probs = [[] for _ in parts]
    inv_l = []
    for r0 in range(0, m_rows, rc):
        sc = []
        for s, _, bias_fn, lanes_fn in parts:
            l0, l1 = (0, s.shape[1]) if lanes_fn is None else lanes_fn(r0)
            c = s[r0:r0 + rc, l0:l1]
            if bias_fn is not None:
                c = c + bias_fn(r0, rc, slice(l0, l1))
            sc.append((c, l0, s.shape[1] - l1))
        mx = functools.reduce(jnp.maximum, [jnp.max(c, axis=-1, keepdims=True) for c, _, _ in sc])
        if extra is not None:
            mx = jnp.maximum(mx, extra[r0:r0 + rc])
        l = jnp.zeros_like(mx) if extra is None else jnp.exp(extra[r0:r0 + rc] - mx)
        for k, (c, before, after) in enumerate(sc):
            p = jnp.exp(c - mx)
            l = l + jnp.sum(p, axis=-1, keepdims=True)
            row = [jnp.zeros((rc, before), BF16)] * (before > 0) + [p.astype(BF16)] \
                + [jnp.zeros((rc, after), BF16)] * (after > 0)
            probs[k].append(row[0] if len(row) == 1 else jnp.concatenate(row, axis=1))
        inv_l.append(1.0 / l)
    o = None
    for k, (_, v, _, _) in enumerate(parts):
        pv = jnp.dot(jnp.concatenate(probs[k], axis=0), v, preferred_element_type=F32)
        o = pv if o is None else o + pv
    return o * jnp.concatenate(inv_l, axis=0)


def _nt(a, b):
    return lax.dot_general(a, b, (((1,), (1,)), ((), ())), preferred_element_type=F32)


NA_REFS_PER_BATCH = 11


def _na_kernel(n_rb, nb, *refs):
    tiles_ref, o_ref, bias_ref = refs[nb * NA_REFS_PER_BATCH:]
    rb = pl.program_id(1)
    for variant, at_rb in ((0, 0), (1, 1), (2, n_rb - 1)):
        @pl.when(rb == at_rb)
        def _(variant=variant):
            _na_fill_bias(variant, tiles_ref, bias_ref)

    for b in range(nb):
        q_ref, k0, k1, k2, k3, v0, v1, v2, v3, kc_ref, vc_ref = refs[b * NA_REFS_PER_BATCH:(b + 1) * NA_REFS_PER_BATCH]
        q2 = q_ref[...]
        kw = jnp.concatenate([k0[...], k1[...], k2[...], k3[...]], axis=0)
        vw = jnp.concatenate([v0[...], v1[...], v2[...], v3[...]], axis=0)
        kc = kc_ref[...]
        vc = vc_ref[...]
        lane = lax.broadcasted_iota(jnp.int32, q2.shape, 1)
        out = jnp.zeros(q2.shape, F32)
        for hh in range(2):
            m = (lane < HEAD_DIM) if hh == 0 else (lane >= HEAD_DIM)
            qm = jnp.where(m, q2, jnp.zeros_like(q2))
            o = _softmax_pv([(_nt(qm, kw), vw, lambda r0, rc, lanes, hh=hh: bias_ref[hh, r0:r0 + rc, lanes],
                              _na_key_lanes), (_nt(qm, kc), vc, None, None)], rc=32)
            out = jnp.where(m, o, out)
        o_ref[b] = out.astype(BF16)


def _na_attention(rw, qkv, rpb):
    B, T, C = rw.B, rw.T, rw.C
    tq = NA_QROWS * GRID_W
    tk = tq // 2
    n_rb = T // tq
    nkb = T // tk
    assert T % tq == 0 and n_rb >= 2 and (B * T) % C == 0
    tiles = _na_bias_tiles(rpb)
    ctxrow = (B * T) // C

    def batch_specs(b):
        kv = lambda j, col: pl.BlockSpec(
            (tk, LANES), lambda p, rb: (b * nkb + jnp.clip(2 * rb - 1 + j, 0, nkb - 1), col + p))
        return ([pl.BlockSpec((tq, LANES), lambda p, rb: (b * n_rb + rb, p))]
                + [kv(j, 4) for j in range(4)] + [kv(j, 8) for j in range(4)]
                + [pl.BlockSpec((C, LANES), lambda p, rb: (ctxrow + b, 4 + p)),
                   pl.BlockSpec((C, LANES), lambda p, rb: (ctxrow + b, 8 + p))])

    out = pl.pallas_call(
        functools.partial(_na_kernel, n_rb, B),
        grid=(4, n_rb),
        in_specs=sum([batch_specs(b) for b in range(B)], [])
        + [pl.BlockSpec((2,) + tiles.shape[1:], lambda p, rb: (p, 0, 0))],
        out_specs=pl.BlockSpec((B, tq, LANES), lambda p, rb: (0, rb, p)),
        out_shape=jax.ShapeDtypeStruct((B, T, 4 * LANES), BF16),
        scratch_shapes=[pltpu.VMEM((2, tq, 2 * tq), F32)],
        compiler_params=_cparams(2),
        name="neighbourhood_attention",
    )(*([qkv] * (B * NA_REFS_PER_BATCH)), tiles)
    return out.reshape(B * T, 4 * LANES)


WA_QBLOCKS = 2


def _wa_kernel(nb, sink_ref, q_ref, *refs):
    step = pl.program_id(1)
    blk = WA_BLOCK
    nkb = WA_QBLOCKS + 2
    kblocks, vblocks = refs[:nkb], refs[nkb:2 * nkb]
    kx_ref, vx_ref, o_ref = refs[2 * nkb:]
    lane = lax.broadcasted_iota(jnp.int32, (blk, LANES), 1)
    zero = jnp.zeros((blk, LANES), BF16)
    qi = lax.broadcasted_iota(jnp.int32, (blk, 3 * blk), 0)
    ks = lax.broadcasted_iota(jnp.int32, (blk, 3 * blk), 1)
    for qb in range(WA_QBLOCKS):
        n = step * WA_QBLOCKS + qb
        lo = jnp.where(n > 0, 0, blk)
        hi = jnp.where(n < nb - 1, 3 * blk, 2 * blk)
        valid = (ks >= qi) & (ks <= qi + 2 * blk) & (ks >= lo) & (ks < hi)
        band_mask = jnp.where(valid, 0.0, NEG_INF)
        rows = slice(qb * blk, (qb + 1) * blk)
        for kv in range(2):
            parts = []
            for pr in range(2):
                c0 = kv * 2 * LANES + pr * LANES
                qp = q_ref[rows, c0:c0 + LANES]
                parts += [jnp.where(lane < HEAD_DIM, qp, zero), jnp.where(lane >= HEAD_DIM, qp, zero)]
            qs = jnp.concatenate(parts, axis=0)
            cs = slice(kv * LANES, (kv + 1) * LANES)
            kb = jnp.concatenate([r[:, cs] for r in kblocks[qb:qb + 3]], axis=0)
            vb = jnp.concatenate([r[:, cs] for r in vblocks[qb:qb + 3]], axis=0)
            sink = jnp.concatenate([jnp.full((blk, 1), sink_ref[kv * 4 + g], F32) for g in range(4)], axis=0)
            o = _softmax_pv([(_nt(qs, kb), vb, lambda r0, rc, lanes, m=band_mask: m[r0 % blk:r0 % blk + rc, lanes], None),
                             (_nt(qs, kx_ref[:, cs]), vx_ref[:, cs], None, None)], extra=sink, rc=64)
            c0 = kv * 2 * LANES
            o_ref[rows, c0:c0 + LANES] = jnp.where(lane < HEAD_DIM, o[0:blk], o[blk:2 * blk]).astype(BF16)
            o_ref[rows, c0 + LANES:c0 + 2 * LANES] = jnp.where(
                lane < HEAD_DIM, o[2 * blk:3 * blk], o[3 * blk:4 * blk]).astype(BF16)


def _wa_attention(rw, qkv, sink):
    B, T, C = rw.B, rw.T, rw.C
    blk = WA_BLOCK
    nb = T // blk
    nq = WA_QBLOCKS
    assert nb % nq == 0
    ctxrow = (B * T) // C

    def kvspec(j, col):
        return pl.BlockSpec((blk, 2 * LANES), lambda b, s: (b * nb + jnp.clip(nq * s - 1 + j, 0, nb - 1), col))

    return pl.pallas_call(
        functools.partial(_wa_kernel, nb),
        grid=(B, nb // nq),
        in_specs=[pl.BlockSpec(memory_space=pltpu.SMEM),
                  pl.BlockSpec((nq * blk, 4 * LANES), lambda b, s: (b * (nb // nq) + s, 3))]
        + [kvspec(j, 8) for j in range(nq + 2)] + [kvspec(j, 9) for j in range(nq + 2)]
        + [pl.BlockSpec((C, 2 * LANES), lambda b, s: (ctxrow + b, 8)),
           pl.BlockSpec((C, 2 * LANES), lambda b, s: (ctxrow + b, 9))],
        out_specs=pl.BlockSpec((nq * blk, 4 * LANES), lambda b, s: (b * (nb // nq) + s, 0)),
        out_shape=jax.ShapeDtypeStruct((B * T, 4 * LANES), BF16),
        compiler_params=_cparams(2),
        name="window_attention",
    )(sink.astype(F32), qkv, *([qkv] * (2 * nq + 6)))


def _ctx_attn_kernel(sink_ref, t_ref, o_ref):
    C = t_ref.shape[0]
    lane = lax.broadcasted_iota(jnp.int32, (C, LANES), 1)
    zero = jnp.zeros((C, LANES), BF16)

    def pair(q2, k2, v2, sinks):
        out = jnp.zeros((C, LANES), F32)
        for hh in range(2):
            m = (lane < HEAD_DIM) if hh == 0 else (lane >= HEAD_DIM)
            extra = None if sinks is None else jnp.full((C, 1), sinks[hh], F32)
            o = _softmax_pv([(_nt(jnp.where(m, q2, zero), k2), v2, None, None)], extra=extra, rc=64)
            out = jnp.where(m, o, out)
        return out.astype(BF16)

    for p in range(4):
        c = p * LANES
        o_ref[:, c:c + LANES] = pair(t_ref[:, c:c + LANES], t_ref[:, 512 + c:640 + c], t_ref[:, 1024 + c:1152 + c], None)
    for kv in range(2):
        kd = t_ref[:, 2048 + kv * LANES:2176 + kv * LANES]
        vd = t_ref[:, 2304 + kv * LANES:2432 + kv * LANES]
        for pr in range(2):
            c = kv * 256 + pr * LANES
            h0 = kv * 4 + pr * 2
            o_ref[:, 512 + c:640 + c] = pair(t_ref[:, 1536 + c:1664 + c], kd, vd, (sink_ref[h0], sink_ref[h0 + 1]))


def _ctx_attention(rw, qkv, sink):
    B, T, C = rw.B, rw.T, rw.C
    ctxrow = (B * T) // C
    return pl.pallas_call(
        _ctx_attn_kernel,
        grid=(B,),
        in_specs=[pl.BlockSpec(memory_space=pltpu.SMEM),
                  pl.BlockSpec((C, qkv.shape[1]), lambda b: (ctxrow + b, 0))],
        out_specs=pl.BlockSpec((C, 8 * LANES), lambda b: (b, 0)),
        out_shape=jax.ShapeDtypeStruct((B * C, 8 * LANES), BF16),
        compiler_params=_cparams(1),
        name="context_attention",
    )(sink.astype(F32), qkv)


S5_Q = 16
CONV_TILE = 256
CONV_HALO = 16


def _ssm_inproj_kernel(x_ref, y1_ref, y2_ref, rt_ref, g2_ref, g_ref, sh_ref, sc_ref, w_ref,
                       xo_ref, z_ref, xbc_ref, u_ref, uj_ref, dt_ref, h_scr, u_scr):
    xn = _moe_residual(x_ref, y1_ref, y2_ref, rt_ref, g2_ref)
    xo_ref[...] = xn
    h_scr[...] = _ada_norm(xn, g_ref[...], sh_ref[...], sc_ref[...]).astype(BF16)

    def mm(c0, n):
        return jnp.dot(h_scr[...], w_ref[:, c0:c0 + n], preferred_element_type=F32)

    nz, nxbc, nu = z_ref.shape[1], xbc_ref.shape[1], u_ref.shape[1]
    for c0 in range(0, nz, MXU_W):
        z_ref[:, c0:c0 + MXU_W] = mm(c0, MXU_W).astype(BF16)
    for c0 in range(0, nxbc, MXU_W):
        xbc_ref[:, c0:c0 + MXU_W] = mm(nz + c0, MXU_W).astype(BF16)
    for c0 in range(0, nu, MXU_W):
        y = mm(nz + nxbc + c0, MXU_W)
        u_ref[:, c0:c0 + MXU_W] = y.astype(BF16)
        for t in range(MXU_W // LANES):
            u_scr[c0 // LANES + t] = y[:, t * LANES:(t + 1) * LANES]
    dt_ref[...] = mm(nz + nxbc + nu, LANES)
    nchunk = u_scr.shape[1] // S5_Q
    for j in range(S5_Q):
        for t in range(u_scr.shape[0]):
            uj_ref[j, :, t * LANES:(t + 1) * LANES] = u_scr[t, pl.ds(j, nchunk, stride=S5_Q), :].astype(BF16)


def _ssm_inproj(rw, xall, y1, y2, route, prev_mods, mods, norm_g, w_in):
    D, tm = rw.D, rw.tm
    w = jnp.concatenate([w_in[:, 0:2560], w_in[:, 2592:3104], w_in[:, 2560:2592], jnp.zeros((D, LANES - 32), F32)],
                        axis=1).astype(BF16)
    row = lambda i: (i, 0)
    return pl.pallas_call(
        _ssm_inproj_kernel,
        grid=(rw.ntot,),
        in_specs=[pl.BlockSpec((tm, D), row), pl.BlockSpec((tm, D), row), pl.BlockSpec((tm, D), row),
                  pl.BlockSpec((tm, LANES), row), _mod_spec(rw, 5), pl.BlockSpec((1, D), lambda i: (0, 0)),
                  _mod_spec(rw, 0), _mod_spec(rw, 1), pl.BlockSpec((D, 3200), lambda i: (0, 0))],
        out_specs=[pl.BlockSpec((tm, D), row),
                   pl.BlockSpec((tm, 1024), row), pl.BlockSpec((tm, 1536), row), pl.BlockSpec((tm, 512), row),
                   pl.BlockSpec((S5_Q, tm // S5_Q, 512), lambda i: (0, i, 0)), pl.BlockSpec((tm, LANES), row)],
        out_shape=[jax.ShapeDtypeStruct((rw.rows, D), F32),
                   jax.ShapeDtypeStruct((rw.rows, 1024), BF16), jax.ShapeDtypeStruct((rw.rows, 1536), BF16),
                   jax.ShapeDtypeStruct((rw.rows, 512), BF16),
                   jax.ShapeDtypeStruct((S5_Q, rw.rows // S5_Q, 512), BF16),
                   jax.ShapeDtypeStruct((rw.rows, LANES), F32)],
        scratch_shapes=[pltpu.VMEM((tm, D), BF16), pltpu.VMEM((512 // LANES, tm, LANES), F32)],
        compiler_params=_cparams(1),
        name="ssm_inproj",
    )(xall, y1, y2, route, prev_mods, norm_g[None, :], mods, mods, w)


def _softplus(x):
    return jnp.maximum(x, 0.0) + jnp.log(1.0 + jnp.exp(-jnp.abs(x)))


def _conv_kernel(lat_tiles, tpb, cpb, x_ref, pv_ref, nx_ref, w_ref, b_ref, dtr_ref, dtb_ref, act_ref, dt_ref):
    i = pl.program_id(0)
    is_lat = i < lat_tiles
    pos = jnp.where(is_lat, i % tpb, (i - lat_tiles) % cpb)
    last_pos = jnp.where(is_lat, tpb - 1, cpb - 1)
    x = x_ref[...].astype(F32)
    tc = x.shape[0]
    prev_row = jnp.where(pos == 0, 0.0, pv_ref[...].astype(F32)[CONV_HALO - 1:CONV_HALO, :])
    next_row = jnp.where(pos == last_pos, 0.0, nx_ref[...].astype(F32)[0:1, :])
    row = lax.broadcasted_iota(jnp.int32, x.shape, 0)
    xm1 = jnp.where(row == 0, prev_row, pltpu.roll(x, 1, 0))
    xp1 = jnp.where(row == tc - 1, next_row, pltpu.roll(x, tc - 1, 0))
    y = w_ref[0:1, :] * xm1 + w_ref[1:2, :] * x + w_ref[2:3, :] * xp1 + b_ref[...]
    act_ref[...] = _silu(y).astype(BF16)
    sp = _softplus(dtr_ref[...] + dtb_ref[...])
    dt_ref[0] = sp
    dt_ref[1] = pltpu.roll(sp, LANES - 16, 1)


def _ssm_conv(rw, xbc, dtr, conv_w, conv_b, dt_bias):
    B, T, C = rw.B, rw.T, rw.C
    tc = CONV_TILE
    assert T % tc == 0 and C % tc == 0
    lat_tiles, tpb, cpb = (B * T) // tc, T // tc, C // tc
    ntiles = rw.rows // tc
    hpt = tc // CONV_HALO
    nhalo = rw.rows // CONV_HALO
    W = xbc.shape[1]
    dtb = jnp.concatenate([dt_bias.reshape(-1), jnp.zeros((LANES - 32,), F32)])[None, :]
    row = lambda i: (i, 0)
    return pl.pallas_call(
        functools.partial(_conv_kernel, lat_tiles, tpb, cpb),
        grid=(ntiles,),
        in_specs=[pl.BlockSpec((tc, W), row),
                  pl.BlockSpec((CONV_HALO, W), lambda i: (jnp.maximum(i * hpt - 1, 0), 0)),
                  pl.BlockSpec((CONV_HALO, W), lambda i: (jnp.minimum((i + 1) * hpt, nhalo - 1), 0)),
                  pl.BlockSpec((3, W), lambda i: (0, 0)), pl.BlockSpec((1, W), lambda i: (0, 0)),
                  pl.BlockSpec((tc, LANES), row), pl.BlockSpec((1, LANES), lambda i: (0, 0))],
        out_specs=[pl.BlockSpec((tc, W), row), pl.BlockSpec((2, tc, LANES), lambda i: (0, i, 0))],
        out_shape=[jax.ShapeDtypeStruct((rw.rows, W), BF16), jax.ShapeDtypeStruct((2, rw.rows, LANES), F32)],
        compiler_params=_cparams(1),
        name="ssm_conv",
    )(xbc, xbc, xbc, conv_w, conv_b[None, :], dtr, dtb)


def _ssd_kernel(nb, *refs):
    acts, dts = refs[0:2 * nb], refs[2 * nb:4 * nb]
    tri_ref, a_ref, yf_ref, yb_ref, hst = refs[4 * nb:]

    @pl.when(pl.program_id(0) == 0)
    def _():
        hst[...] = jnp.zeros_like(hst)

    for d, y_ref in enumerate((yf_ref, yb_ref)):
        for b in range(nb):
            _ssd_chunk(acts[d * nb + b], dts[d * nb + b], tri_ref[d], a_ref[d], y_ref.at[b], hst.at[d, b])


def _ssd_chunk(act_ref, dt_ref, tri, avec, y_ref, hst):
    q = SSD_CHUNK
    dt = dt_ref[...]
    da = dt * avec
    acs = jnp.dot(tri, da, preferred_element_type=F32, precision=lax.Precision.HIGHEST)
    tot = jnp.sum(da, axis=0, keepdims=True)
    acs_t = acs.T
    dt_t = dt.T
    eacs = jnp.exp(acs)
    wend = jnp.exp(tot - acs) * dt
    etot = jnp.exp(tot)
    mask = tri > 0.5
    left = lax.broadcasted_iota(jnp.int32, (q, LANES), 1) < HEAD_DIM
    left1 = lax.broadcasted_iota(jnp.int32, (1, LANES), 1) < HEAD_DIM
    for g in range(2):
        bg = act_ref[:, 1024 + g * 128:1152 + g * 128]
        cg = act_ref[:, 1280 + g * 128:1408 + g * 128]
        cb = _nt(cg, bg)
        hin = hst[:, g * 512:(g + 1) * 512]
        yoff = jnp.dot(cg, hin.astype(BF16), preferred_element_type=F32)
        xw, dec = [], []
        for pr in range(4):
            h_a = g * 8 + pr * 2
            h_b = h_a + 1
            c0 = h_a * HEAD_DIM
            x2 = act_ref[:, c0:c0 + LANES]
            outs = []
            for h in (h_a, h_b):
                seg = acs[:, h:h + 1] - acs_t[h:h + 1, :]
                w = cb * jnp.exp(jnp.where(mask, seg, NEG_INF)) * dt_t[h:h + 1, :]
                outs.append(jnp.dot(w.astype(BF16), x2, preferred_element_type=F32))
            yd = jnp.where(left, outs[0], outs[1])
            sc = jnp.where(left, eacs[:, h_a:h_a + 1], eacs[:, h_b:h_b + 1])
            y_ref[:, c0:c0 + LANES] = (yd + yoff[:, pr * LANES:(pr + 1) * LANES] * sc).astype(BF16)
            wsc = jnp.where(left, wend[:, h_a:h_a + 1], wend[:, h_b:h_b + 1])
            xw.append((x2.astype(F32) * wsc).astype(BF16))
            dec.append(jnp.where(left1, etot[:, h_a:h_a + 1], etot[:, h_b:h_b + 1]))
        bg_t = bg.astype(F32).T.astype(BF16)
        snew = jnp.dot(bg_t, jnp.concatenate(xw, axis=1), preferred_element_type=F32)
        hst[:, g * 512:(g + 1) * 512] = hin * jnp.concatenate(dec, axis=1) + snew


def _ssd(rw, act, dt2, a_log):
    B, T, C = rw.B, rw.T, rw.C
    q = SSD_CHUNK
    nct, nlt = C // q, T // q
    ctx0 = (B * T) // q
    r = np.arange(q)
    tri = jnp.asarray(np.stack([r[None, :] <= r[:, None], r[None, :] >= r[:, None]]).astype(np.float32))
    avec = jnp.concatenate([-jnp.exp(a_log.astype(F32)), jnp.zeros((2, LANES - a_log.shape[1]), F32)], axis=1)[:, None, :]

    def lat(d, s):
        return jnp.clip(s - nct, 0, nlt - 1) if d == 0 else nlt - 1 - jnp.clip(s - nct, 0, nlt - 1)

    def blk(d, b, s):
        kc = s if d == 0 else nct - 1 - s
        return jnp.where(s < nct, ctx0 + b * nct + kc, b * nlt + lat(d, s))

    pairs = [(d, b) for d in range(2) for b in range(B)]
    aspec = lambda d, b: pl.BlockSpec((q, act.shape[1]), lambda s: (blk(d, b, s), 0))
    dspec = lambda d, b: pl.BlockSpec((None, q, LANES), lambda s: (d, blk(d, b, s), 0))
    yspec = lambda d: pl.BlockSpec((B, q, 1024), lambda s: (0, lat(d, s), 0))
    yf, yb = pl.pallas_call(
        functools.partial(_ssd_kernel, B),
        grid=(nct + nlt,),
        in_specs=[aspec(d, b) for d, b in pairs] + [dspec(d, b) for d, b in pairs]
        + [pl.BlockSpec((2, q, q), lambda s: (0, 0, 0)), pl.BlockSpec((2, 1, LANES), lambda s: (0, 0, 0))],
        out_specs=[yspec(0), yspec(1)],
        out_shape=[jax.ShapeDtypeStruct((B, T, 1024), BF16)] * 2,
        scratch_shapes=[pltpu.VMEM((2, B, q, 1024), F32)],
        compiler_params=_cparams(1),
        name="ssd_scan",
    )(*([act] * (2 * B)), *([dt2] * (2 * B)), tri, avec)
    return yf.reshape(B * T, 1024), yb.reshape(B * T, 1024)


def _cmul(ar, ai, br, bi):
    return ar * br - ai * bi, ar * bi + ai * br


def _s5_weight_kernel(lre_ref, lim_ref, ls_ref, bre_ref, bim_ref, cre_ref, cim_ref,
                      wsr_ref, wsi_ref, wor_ref, woi_ref, bt_ref, are_ref, aim_ref):
    lre, lim = lre_ref[...], lim_ref[...]
    step = jnp.exp(ls_ref[...])
    er, ei = lre * step, lim * step
    npow = 24
    p = lax.broadcasted_iota(jnp.int32, (1, npow, 1), 1).astype(F32)
    mag = jnp.exp(p * er)
    pre, pim = mag * jnp.cos(p * ei), mag * jnp.sin(p * ei)
    a_re, a_im = pre[:, 1:2, :], pim[:, 1:2, :]
    den = lre * lre + lim * lim
    q_re = ((a_re - 1.0) * lre + a_im * lim) / den
    q_im = (a_im * lre - (a_re - 1.0) * lim) / den
    bb_re, bb_im = _cmul(q_re, q_im, bre_ref[...], bim_ref[...])
    c_re, c_im = cre_ref[...], cim_ref[...]
    ws_r, ws_i, wo_r, wo_i, ca_r, ca_i = [], [], [], [], [], []
    for t in range(S5_Q):
        r, i = _cmul(bb_re, bb_im, pre[:, t:t + 1, :], pim[:, t:t + 1, :])
        ws_r.append(r)
        ws_i.append(i)
        r, i = _cmul(c_re, c_im, pre[:, t:t + 1, :], pim[:, t:t + 1, :])
        ca_r.append(r)
        ca_i.append(i)
        r, i = _cmul(c_re, c_im, pre[:, t + 1:t + 2, :], pim[:, t + 1:t + 2, :])
        wo_r.append(r)
        wo_i.append(-i)
    cat = lambda xs: jnp.concatenate(xs, axis=1)
    is_fwd = pl.program_id(0) < pl.num_programs(0) // 2

    def packed(blocks, reverse_fwd):
        w = jnp.where(is_fwd if reverse_fwd else jnp.logical_not(is_fwd), cat(blocks[::-1]), cat(blocks))
        w2 = jnp.concatenate([w, w], axis=2)
        g = lax.broadcasted_iota(jnp.int32, w2.shape, 0)
        ln = lax.broadcasted_iota(jnp.int32, w2.shape, 2)
        return jnp.where((g % 2 == 0) == (ln < w.shape[2]), w2, 0.0).astype(BF16)

    wsr_ref[...] = packed(ws_r, True)
    wsi_ref[...] = packed(ws_i, True)
    wor_ref[...] = packed(wo_r, False)
    woi_ref[...] = packed(wo_i, False)
    bdot = lambda a, b: lax.dot_general(a, b, (((2,), (2,)), ((0,), (0,))), preferred_element_type=F32,
                                        precision=lax.Precision.HIGHEST)
    kin = jnp.where(is_fwd, bdot(bb_re, cat(ca_r)) - bdot(bb_im, cat(ca_i)),
                    bdot(bb_re, cat(ca_r[::-1])) - bdot(bb_im, cat(ca_i[::-1])))
    qc = kin.shape[2]
    lane = lax.broadcasted_iota(jnp.int32, kin.shape, 2)
    rows = []
    for j in range(S5_Q):
        fwd = jnp.where(lane >= j * S5_GROUP, pltpu.roll(kin, j * S5_GROUP, 2), 0.0)
        back = (S5_Q - 1 - j) * S5_GROUP
        bwd = jnp.where(lane < qc - back, pltpu.roll(kin, (qc - back) % qc, 2), 0.0)
        rows.append(jnp.where(is_fwd, fwd, bwd))
    bt_ref[...] = cat(rows).astype(BF16)
    are_ref[...] = pre[:, S5_Q:S5_Q + 1, :]
    aim_ref[...] = pim[:, S5_Q:S5_Q + 1, :]


def _s5_weights(lam_re, lam_im, log_step, b_re, b_im, c_re, c_im):
    nd, ng, ns = lam_re.shape
    G = nd * ng
    ch = S5_GROUP
    gb = 8
    qc = S5_Q * ch
    f = lambda a: a.astype(F32)
    args = (f(lam_re).reshape(G, 1, ns), f(lam_im).reshape(G, 1, ns), f(log_step).reshape(G, 1, 1),
            f(b_re).reshape(G, ns, ch).transpose(0, 2, 1), f(b_im).reshape(G, ns, ch).transpose(0, 2, 1),
            f(c_re).reshape(G, ch, ns), f(c_im).reshape(G, ch, ns))
    spec = lambda a: pl.BlockSpec((gb,) + a.shape[1:], lambda i: (i, 0, 0))
    assert nd == 2 and (G // gb) % 2 == 0
    oshape = [jax.ShapeDtypeStruct((G, qc, 2 * ns), BF16)] * 4 + [jax.ShapeDtypeStruct((G, qc, qc), BF16)] \
        + [jax.ShapeDtypeStruct((G, 1, ns), F32)] * 2
    ws_r, ws_i, wo_r, wo_i, bt, a_re, a_im = pl.pallas_call(
        _s5_weight_kernel,
        grid=(G // gb,),
        in_specs=[spec(a) for a in args],
        out_specs=[pl.BlockSpec((gb,) + s.shape[1:], lambda i: (i, 0, 0)) for s in oshape],
        out_shape=oshape,
        compiler_params=_cparams(1),
        name="s5_weights",
    )(*args)
    by_dir = lambda w: w.reshape((nd, ng) + w.shape[1:])
    pair = lambda a: a.reshape(nd, ng // 2, 1, 2 * ns)
    return by_dir(bt), by_dir(ws_r), by_dir(ws_i), by_dir(wo_r), by_dir(wo_i), pair(a_re), pair(a_im)


S5_GB = LANES // S5_GROUP


def _s5_kernel(B, nct, nlt, uj_ref, perm_ref, bt_ref, wsr_ref, wsi_ref, wor_ref, woi_ref, are_ref, aim_ref, yj_ref,
               x_scr, y_scr, s_re, s_im):
    gb, npair, qc = S5_GB, S5_GB // 2, S5_Q * S5_GROUP
    lhs = jnp.concatenate([uj_ref[j] for j in range(S5_Q)], axis=1)
    for m in range(gb):
        x_scr[:, m * qc:(m + 1) * qc] = jnp.dot(lhs, perm_ref[:, m * qc:(m + 1) * qc],
                                                preferred_element_type=F32).astype(BF16)
    xg = lambda g: x_scr[:, g * qc:(g + 1) * qc]
    for d in range(2):
        for pr in range(npair):
            for dst, w_ref in ((s_re, wsr_ref), (s_im, wsi_ref)):
                dst[d, pr] = (jnp.dot(xg(2 * pr), w_ref[d, 2 * pr], preferred_element_type=F32)
                              + jnp.dot(xg(2 * pr + 1), w_ref[d, 2 * pr + 1], preferred_element_type=F32))
    chains = [(d, pr, b) for d in range(2) for pr in range(npair) for b in range(B)]
    coef = {(d, pr): (are_ref[d, pr], aim_ref[d, pr]) for d in range(2) for pr in range(npair)}
    ctx0 = B * nlt

    def body(s, carry):
        in_ctx = s < nct
        rows = {}
        for d in range(2):
            kc = s if d == 0 else nct - 1 - s
            kl = s - nct if d == 0 else nlt - 1 - (s - nct)
            for b in range(B):
                rows[(d, b)] = pl.ds(jnp.where(in_ctx, ctx0 + b * nct + kc, b * nlt + kl), 1)
        contrib = [(s_re[d, pr, rows[(d, b)], :], s_im[d, pr, rows[(d, b)], :]) for d, pr, b in chains]
        new = []
        for (d, pr, b), (hr, hi), (sr, si) in zip(chains, carry, contrib):
            ar, ai = coef[(d, pr)]
            s_re[d, pr, rows[(d, b)], :] = hr
            s_im[d, pr, rows[(d, b)], :] = hi
            new.append((ar * hr - ai * hi + sr, ar * hi + ai * hr + si))
        return tuple(new)

    zero = jnp.zeros((1, LANES), F32)
    lax.fori_loop(0, nct + nlt, body, tuple((zero, zero) for _ in chains))
    for g in range(gb):
        acc = None
        for d in range(2):
            t = (jnp.dot(xg(g), bt_ref[d, g], preferred_element_type=F32)
                 + _nt(s_re[d, g // 2].astype(BF16), wor_ref[d, g])
                 + _nt(s_im[d, g // 2].astype(BF16), woi_ref[d, g]))
            acc = t if acc is None else acc + t
        y_scr[:, g * qc:(g + 1) * qc] = acc.astype(BF16)
    for i in range(S5_Q):
        yj_ref[i] = _nt(y_scr[...], perm_ref[i * LANES:(i + 1) * LANES, :]).astype(BF16)


def _s5(rw, uj, weights):
    B, T, C = rw.B, rw.T, rw.C
    bt, ws_r, ws_i, wo_r, wo_i, a_re, a_im = weights
    ng = bt.shape[1]
    q, gb = S5_Q, S5_GB
    nct, nlt = C // q, T // q
    nrow = uj.shape[1]
    qc = q * S5_GROUP
    k = gb * qc
    idx = np.arange(k)
    j, m, c = idx // LANES, (idx % LANES) // S5_GROUP, idx % S5_GROUP
    perm = np.zeros((k, k), np.float32)
    perm[idx, m * qc + j * S5_GROUP + c] = 1.0
    once = dict(pipeline_mode=pl.Buffered(1))
    wspec = lambda n: pl.BlockSpec((2, gb, qc, n), lambda i: (0, i, 0, 0), **once)
    aspec = pl.BlockSpec((2, gb // 2, 1, LANES), lambda i: (0, i, 0, 0))
    return pl.pallas_call(
        functools.partial(_s5_kernel, B, nct, nlt),
        grid=(ng // gb,),
        in_specs=[pl.BlockSpec((q, nrow, LANES), lambda i: (0, 0, i), **once),
                  pl.BlockSpec((k, k), lambda i: (0, 0), **once),
                  wspec(qc), wspec(LANES), wspec(LANES), wspec(LANES), wspec(LANES), aspec, aspec],
        out_specs=pl.BlockSpec((q, nrow, LANES), lambda i: (0, 0, i)),
        out_shape=jax.ShapeDtypeStruct(uj.shape, BF16),
        scratch_shapes=[pltpu.VMEM((nrow, k), BF16), pltpu.VMEM((nrow, k), BF16),
                        pltpu.VMEM((2, gb // 2, nrow, LANES), F32), pltpu.VMEM((2, gb // 2, nrow, LANES), F32)],
        compiler_params=_cparams(1),
        name="s5_scan",
    )(uj, jnp.asarray(perm, BF16), bt, ws_r, ws_i, wo_r, wo_i, a_re, a_im)


def _gelu_tanh(x):
    return 0.5 * x * (1.0 + jnp.tanh(math.sqrt(2.0 / math.pi) * (x + 0.044715 * (x * x * x))))


def _ssm_outproj_kernel(y0_ref, y1_ref, xs_ref, z_ref, v_ref, u_ref, dsk_ref, nw_ref, s5d_ref, gw_ref, gb_ref,
                        x_ref, w_ref, g1_ref, gn_ref, sh2_ref, sc2_ref, wr_ref, br_ref, lt_ref,
                        xo_ref, h2_ref, rt_ref, cnt_ref, carry, v_scr):
    i = pl.program_id(0)
    y = y0_ref[...].astype(F32) + y1_ref[...].astype(F32) + dsk_ref[...] * xs_ref[...].astype(F32)
    y = _rms(y * _silu(z_ref[...].astype(F32))) * nw_ref[...]
    ntile = v_scr.shape[0]
    nchunk = v_scr.shape[1] // S5_Q
    for j in range(S5_Q):
        for t in range(ntile):
            v_scr[t, pl.ds(j, nchunk, stride=S5_Q), :] = v_ref[j, :, t * LANES:(t + 1) * LANES].astype(F32)
    s5_y = jnp.concatenate([v_scr[t] for t in range(ntile)], axis=1)
    v = _gelu_tanh(s5_y + s5d_ref[...] * u_ref[...].astype(F32))
    v = v * _sigmoid(jnp.dot(v.astype(BF16), gw_ref[...], preferred_element_type=F32) + gb_ref[...])
    mix = jnp.concatenate([y, v], axis=1).astype(BF16)
    yo = jnp.dot(mix, w_ref[...], preferred_element_type=F32)
    _post_mixer(i, x_ref[...], yo, g1_ref[...], gn_ref[...], sh2_ref[...], sc2_ref[...], wr_ref, br_ref, lt_ref,
                xo_ref, h2_ref, rt_ref, cnt_ref, carry)


def _ssm_outproj(rw, ssd_y, act, z, s5_y, u, d_skip, norm_w, s5_d, glu_w, glu_b, xall, w_out, mods, norm_ffn, wr, br):
    D, tm = rw.D, rw.tm
    ntiles = rw.nlat
    post_in, post_out = _post_specs(rw)
    row = lambda i: (i, 0)
    vec = lambda n: pl.BlockSpec((1, n), lambda i: (0, 0))
    dsk = jnp.repeat(d_skip.astype(F32), HEAD_DIM)[None, :]
    return pl.pallas_call(
        _ssm_outproj_kernel,
        grid=(ntiles,),
        in_specs=[pl.BlockSpec((tm, 1024), row), pl.BlockSpec((tm, 1024), row),
                  pl.BlockSpec((tm, 1024), row), pl.BlockSpec((tm, 1024), row),
                  pl.BlockSpec((S5_Q, tm // S5_Q, 512), lambda i: (0, i, 0)),
                  pl.BlockSpec((tm, 512), row), vec(1024), vec(1024), vec(512),
                  pl.BlockSpec((512, 512), lambda i: (0, 0)), vec(512),
                  pl.BlockSpec((tm, D), row), pl.BlockSpec((1536, D), lambda i: (0, 0)), _mod_spec(rw, 2)] + post_in,
        out_specs=post_out,
        out_shape=_post_shapes(ntiles * tm, D),
        scratch_shapes=[pltpu.VMEM((1, LANES), F32), pltpu.VMEM((512 // LANES, tm, LANES), F32)],
        compiler_params=_cparams(1),
        name="ssm_outproj_router",
    )(ssd_y[0], ssd_y[1], act, z, s5_y, u, dsk, norm_w[None, :], s5_d[None, :], glu_w.astype(BF16), glu_b[None, :],
      xall, w_out.astype(BF16), mods, norm_ffn[None, :], mods, mods, wr, br, _lower_tri(tm))


def kernel(x, c, ctx, c_ctx, mod_w, mod_b, norm_mix, norm_ffn, att_w_in, att_w_out, na_q_norm, na_k_norm, na_rel_bias, wa_q_norm, wa_k_norm, wa_sink, ssm_w_in, ssm_w_out, ssd_conv_w, ssd_conv_b, ssd_dt_bias, ssd_a_log, ssd_d, ssd_norm, s5_lambda_re, s5_lambda_im, s5_log_step, s5_b_re, s5_b_im, s5_c_re, s5_c_im, s5_d, s5_glu_w, s5_glu_b, moe_w_group, moe_b_group, moe_w_expert, moe_b_expert, moe_w13, moe_w2):
    B, T, D = x.shape
    C = ctx.shape[1]
    rw = _Rows(B, T, C, D, ROW_TILE)
    xl = x.reshape(B * T, D)
    xc = ctx.reshape(B * C, D)
    cm = jnp.concatenate([c, c_ctx[None, :], jnp.zeros((8 - B - 1, D), F32)], axis=0)
    mods = _modulation(cm, mod_w, mod_b)
    mods = mods.reshape(mods.shape[0], 8, 1, 6 * D)

    m0 = mods[0]
    qkv = _att_inproj(rw, xl, xc, m0, norm_mix[0], att_w_in[0], na_q_norm[0], na_k_norm[0], wa_q_norm[0],
                      wa_k_norm[0])
    na = _na_attention(rw, qkv, na_rel_bias[0])
    wa = _wa_attention(rw, qkv, wa_sink[0])
    cx = _ctx_attention(rw, qkv, wa_sink[0])
    wr, br = _router_weights(moe_w_group[0], moe_b_group[0], moe_w_expert[0], moe_b_expert[0])
    xall, h2, route, counts = _att_outproj(rw, na, wa, cx, xl, xc, att_w_out[0], m0, norm_ffn[0], wr, br)
    y1, y2 = _moe(h2, route, counts, moe_w13, moe_w2, 0)

    m1 = mods[1]
    xall, z, xbc, u, uj, dtr = _ssm_inproj(rw, xall, y1, y2, route, m0, m1, norm_mix[1], ssm_w_in[0])
    act, dt2 = _ssm_conv(rw, xbc, dtr, ssd_conv_w[0], ssd_conv_b[0], ssd_dt_bias[0])
    ssd_y = _ssd(rw, act, dt2, ssd_a_log[0])
    s5_w = _s5_weights(s5_lambda_re[0], s5_lambda_im[0], s5_log_step[0], s5_b_re[0], s5_b_im[0], s5_c_re[0],
                       s5_c_im[0])
    s5_y = _s5(rw, uj, s5_w)
    wr, br = _router_weights(moe_w_group[1], moe_b_group[1], moe_w_expert[1], moe_b_expert[1])
    xlat, h2, route, counts = _ssm_outproj(rw, ssd_y, act, z, s5_y, u, ssd_d[0], ssd_norm[0], s5_d[0], s5_glu_w[0],
                                           s5_glu_b[0], xall, ssm_w_out[0], m1, norm_ffn[1], wr, br)
    y1, y2 = _moe(h2, route, counts, moe_w13, moe_w2, 1)
    out = _combine(rw, rw.nlat, xlat, y1, y2, route, m1)
    return out.reshape(B, T, D)
```

```python
import functools
import math

import jax
import jax.numpy as jnp
import numpy as np
from jax import lax
from jax.experimental import pallas as pl
from jax.experimental.pallas import tpu as pltpu

F32 = jnp.float32
BF16 = jnp.bfloat16

EPS = 1e-6
NEG_INF = -1e30
GRID_W = 64
HEAD_DIM = 64
NA_KH = 8
NA_KW = 16
WA_BLOCK = 128
ROPE_BASE = 10000.0
SSD_CHUNK = 128
S5_GROUP = 16
MOE_GROUPS = 4
MOE_EPG = 8
MOE_EXPERTS = MOE_GROUPS * MOE_EPG

LANES = 128
ROW_TILE = 512
MXU_W = 256
MOE_TILE = MXU_W
VMEM_LIMIT = 56 * 1024 * 1024
MOE_VMEM_LIMIT = 60 * 1024 * 1024


def _cparams(n_axes, vmem=VMEM_LIMIT):
    return pltpu.CompilerParams(dimension_semantics=("arbitrary",) * n_axes, vmem_limit_bytes=vmem)


def _sigmoid(x):
    return jax.nn.sigmoid(x)


def _silu(x):
    return x * _sigmoid(x)


def _rms(x, eps=EPS):
    return x * lax.rsqrt(jnp.mean(x * x, axis=-1, keepdims=True) + eps)


def _ada_norm(x, g, shift, scale):
    return (_rms(x) * g) * (1.0 + scale) + shift


def _mod_kernel(c_ref, w_ref, b_ref, o_ref):
    a = _silu(c_ref[...])
    o_ref[...] = jnp.dot(a, w_ref[...], preferred_element_type=F32, precision=lax.Precision.HIGHEST) + b_ref[...]


def _modulation(cm, mod_w, mod_b):
    depth, d, n6 = mod_w.shape
    tn = 1024
    return pl.pallas_call(
        _mod_kernel,
        grid=(depth, n6 // tn),
        in_specs=[pl.BlockSpec((8, d), lambda l, j: (0, 0)),
                  pl.BlockSpec((None, d, tn), lambda l, j: (l, 0, j)),
                  pl.BlockSpec((None, 1, tn), lambda l, j: (l, 0, j))],
        out_specs=pl.BlockSpec((None, 8, tn), lambda l, j: (l, 0, j)),
        out_shape=jax.ShapeDtypeStruct((depth, 8, n6), F32),
        compiler_params=_cparams(2),
        name="modulation",
    )(cm, mod_w, mod_b.reshape(depth, 1, n6))


class _Rows:
    def __init__(self, B, T, C, D, tm):
        assert T % tm == 0 and (B * C) % tm == 0
        self.B, self.T, self.C, self.D, self.tm = B, T, C, D, tm
        self.tpb = T // tm
        self.nlat = B * self.tpb
        self.nctx = (B * C) // tm
        self.ntot = self.nlat + self.nctx
        self.rows = B * (T + C)

    def group(self, i):
        return jnp.where(i < self.nlat, i // self.tpb, self.B)


def _mod_spec(rw, col):
    return pl.BlockSpec((None, 1, rw.D), lambda i, *_: (rw.group(i), 0, col))


def _seg_norm(y, seg, gcol):
    ss = jnp.dot((y * y).astype(BF16), seg, preferred_element_type=F32)
    return y * lax.rsqrt(ss + EPS) * gcol


def _rope(y, cos, sin):
    w = y.shape[-1]
    lane = lax.broadcasted_iota(jnp.int32, y.shape, 1)
    first = (lane % 32) < 16
    partner = jnp.where(first, pltpu.roll(y, w - 16, 1), pltpu.roll(y, 16, 1))
    return y * cos + partner * sin


def _dup_halves(k):
    lane = lax.broadcasted_iota(jnp.int32, k.shape, 1)
    sw = pltpu.roll(k, 64, 1)
    return jnp.where(lane < 64, k, sw), jnp.where(lane < 64, sw, k)


def _att_inproj_kernel(nlat, xl_ref, xc_ref, g_ref, sh_ref, sc_ref, w_ref, gcol_ref, cos_ref, sin_ref, seg_ref,
                       o_ref, h_scr):
    i = pl.program_id(0)
    x = jnp.where(i < nlat, xl_ref[...], xc_ref[...])
    h_scr[...] = _ada_norm(x, g_ref[...], sh_ref[...], sc_ref[...]).astype(BF16)
    seg = seg_ref[...]
    cos2 = jnp.concatenate([cos_ref[...], cos_ref[...]], axis=1)
    sin2 = jnp.concatenate([sin_ref[...], sin_ref[...]], axis=1)
    for c in range(w_ref.shape[1] // MXU_W):
        c0 = c * MXU_W
        cols = slice(c0, c0 + MXU_W)
        y = jnp.dot(h_scr[...], w_ref[:, cols], preferred_element_type=F32)
        gcol = gcol_ref[:, cols]
        if c in (0, 1, 2, 3):
            o_ref[:, cols] = _seg_norm(y, seg, gcol).astype(BF16)
        elif c in (4, 5):
            o_ref[:, cols] = y.astype(BF16)
        elif c in (6, 7):
            o_ref[:, cols] = _rope(_seg_norm(y, seg, gcol), cos2, sin2).astype(BF16)
        else:
            lane = lax.broadcasted_iota(jnp.int32, y.shape, 1)
            yk = jnp.where(lane < LANES, _seg_norm(y, seg, gcol), y)
            yr = jnp.where(lane < LANES, _rope(yk, cos2, sin2), yk)
            k0, k1 = _dup_halves(yr[:, :LANES])
            v0, v1 = _dup_halves(yr[:, LANES:])
            for t, dup in enumerate((k0, k1, v0, v1)):
                o_ref[:, c0 + t * LANES:c0 + (t + 1) * LANES] = dup.astype(BF16)


def _rope_tables(T, tm):
    t = np.arange(T)
    d = np.arange(HEAD_DIM)
    nf = HEAD_DIM // 4
    inv = jnp.asarray(ROPE_BASE, F32) ** (-jnp.arange(nf, dtype=F32) / nf)
    pos = np.where((d // 32 == 0)[None, :], (t // GRID_W)[:, None], (t % GRID_W)[:, None])
    ang = jnp.asarray(pos, F32) * inv[d % nf][None, :]
    sign = np.where((d % 32) < 16, -1.0, 1.0).astype(np.float32)
    cos = jnp.cos(ang)
    sin = jnp.sin(ang) * sign[None, :]
    cos = jnp.concatenate([cos, jnp.ones((tm, HEAD_DIM), F32)], axis=0)
    sin = jnp.concatenate([sin, jnp.zeros((tm, HEAD_DIM), F32)], axis=0)
    return jnp.tile(cos, (1, 2)), jnp.tile(sin, (1, 2))


def _att_inproj(rw, xl, xc, mods, norm_g, w_in, na_qn, na_kn, wa_qn, wa_kn):
    D, tm = rw.D, rw.tm
    scale = HEAD_DIM ** -0.5
    gcol = jnp.concatenate([jnp.tile(na_qn * scale, 8), jnp.tile(na_kn, 8), jnp.ones((512,), F32),
                            jnp.tile(wa_qn * scale, 8), jnp.tile(wa_kn, 2), jnp.ones((128,), F32)])[None, :]
    cos, sin = _rope_tables(rw.T, tm)
    segn = np.arange(256) // 64
    seg = jnp.asarray((segn[:, None] == segn[None, :]).astype(np.float32) / 64.0, BF16)
    nlat, tpb = rw.nlat, rw.tpb
    return pl.pallas_call(
        functools.partial(_att_inproj_kernel, nlat),
        grid=(rw.ntot,),
        in_specs=[pl.BlockSpec((tm, D), lambda i: (jnp.minimum(i, nlat - 1), 0)),
                  pl.BlockSpec((tm, D), lambda i: (jnp.maximum(i - nlat, 0), 0)),
                  pl.BlockSpec((1, D), lambda i: (0, 0)),
                  _mod_spec(rw, 0), _mod_spec(rw, 1),
                  pl.BlockSpec((D, 2304), lambda i: (0, 0)),
                  pl.BlockSpec((1, 2304), lambda i: (0, 0)),
                  pl.BlockSpec((tm, 128), lambda i: (jnp.where(i < nlat, i % tpb, tpb), 0)),
                  pl.BlockSpec((tm, 128), lambda i: (jnp.where(i < nlat, i % tpb, tpb), 0)),
                  pl.BlockSpec((256, 256), lambda i: (0, 0))],
        out_specs=pl.BlockSpec((tm, 2560), lambda i: (i, 0)),
        out_shape=jax.ShapeDtypeStruct((rw.rows, 2560), BF16),
        scratch_shapes=[pltpu.VMEM((tm, D), BF16)],
        compiler_params=_cparams(1),
        name="att_inproj",
    )(xl, xc, norm_g[None, :], mods, mods, w_in.astype(BF16), gcol, cos, sin, seg)


def _route(lg, lt, carry):
    lane = lax.broadcasted_iota(jnp.int32, lg.shape, 1).astype(F32)
    gm = lane < MOE_GROUPS
    mg = jnp.max(jnp.where(gm, lg, NEG_INF), axis=-1, keepdims=True)
    eg = jnp.where(gm, jnp.exp(jnp.where(gm, lg, NEG_INF) - mg), 0.0)
    pg = eg / jnp.sum(eg, axis=-1, keepdims=True)
    ptop = jnp.max(pg, axis=-1, keepdims=True)
    gidx = jnp.min(jnp.where(gm & (pg == ptop), lane, 1e9), axis=-1, keepdims=True)
    lo = MOE_GROUPS + MOE_EPG * gidx
    em = (lane >= lo) & (lane < lo + MOE_EPG)
    le = jnp.where(em, lg, NEG_INF)
    ee = jnp.where(em, jnp.exp(le - jnp.max(le, axis=-1, keepdims=True)), 0.0)
    pe = ee / jnp.sum(ee, axis=-1, keepdims=True)
    v1 = jnp.max(jnp.where(em, pe, -1.0), axis=-1, keepdims=True)
    i1 = jnp.min(jnp.where(em & (pe == v1), lane, 1e9), axis=-1, keepdims=True)
    em2 = em & (lane != i1)
    v2 = jnp.max(jnp.where(em2, pe, -1.0), axis=-1, keepdims=True)
    i2 = jnp.min(jnp.where(em2 & (pe == v2), lane, 1e9), axis=-1, keepdims=True)
    den = v1 + v2
    w1 = v1 / den * ptop
    w2 = v2 / den * ptop
    e1 = i1 - MOE_GROUPS
    e2 = i2 - MOE_GROUPS
    m1 = lane == e1
    m2 = lane == e2
    oh = jnp.where(m1 | m2, 1.0, 0.0)
    cnt = jnp.dot(lt, oh.astype(BF16), preferred_element_type=F32) + carry
    r1 = jnp.sum(jnp.where(m1, cnt, 0.0), axis=-1, keepdims=True)
    r2 = jnp.sum(jnp.where(m2, cnt, 0.0), axis=-1, keepdims=True)
    route = jnp.where(lane == 0, e1, jnp.where(lane == 1, e2, jnp.where(lane == 2, w1, jnp.where(
        lane == 3, w2, jnp.where(lane == 4, r1, jnp.where(lane == 5, r2, 0.0))))))
    return route, carry + jnp.sum(oh, axis=0, keepdims=True)


def _post_mixer(i, x, y, g1, gn, sh2, sc2, wr_ref, br_ref, lt_ref, xo_ref, h2_ref, rt_ref, cnt_ref, carry):
    xn = x + g1 * y
    xo_ref[...] = xn
    h2 = _ada_norm(xn, gn, sh2, sc2)
    hb = h2.astype(BF16)
    hbf = hb.astype(F32)
    half = h2.shape[1] // 2
    h2_ref[...] = pltpu.pack_elementwise([h2[:, :half], h2[:, half:]], packed_dtype=BF16)
    hl = (h2 - hbf).astype(BF16)
    lg = (jnp.dot(hb, wr_ref[0], preferred_element_type=F32)
          + (jnp.dot(hb, wr_ref[1], preferred_element_type=F32) + jnp.dot(hl, wr_ref[0], preferred_element_type=F32))
          + br_ref[...])

    @pl.when(i == 0)
    def _():
        carry[...] = jnp.zeros_like(carry)

    route, newc = _route(lg, lt_ref[...], carry[...])
    rt_ref[...] = route
    carry[...] = newc
    cnt_ref[...] = newc


def _att_outproj_kernel(nlat, na_ref, wa_ref, cx_ref, xl_ref, xc_ref, w_ref, g1_ref, gn_ref, sh2_ref, sc2_ref,
                        wr_ref, br_ref, lt_ref, xo_ref, h2_ref, rt_ref, cnt_ref, carry):
    i = pl.program_id(0)
    lat = i < nlat
    mix = jnp.where(lat, jnp.concatenate([na_ref[...], wa_ref[...]], axis=1), cx_ref[...])
    y = jnp.dot(mix, w_ref[...], preferred_element_type=F32)
    x = jnp.where(lat, xl_ref[...], xc_ref[...])
    _post_mixer(i, x, y, g1_ref[...], gn_ref[...], sh2_ref[...], sc2_ref[...], wr_ref, br_ref, lt_ref,
                xo_ref, h2_ref, rt_ref, cnt_ref, carry)


def _router_weights(w_group, b_group, w_expert, b_expert):
    D = w_group.shape[0]
    pad = LANES - MOE_GROUPS - MOE_EXPERTS
    wr = jnp.concatenate([w_group, w_expert, jnp.zeros((D, pad), F32)], axis=1)
    br = jnp.concatenate([b_group, b_expert, jnp.zeros((pad,), F32)])[None, :]
    hi = wr.astype(BF16)
    lo = (wr - hi.astype(F32)).astype(BF16)
    return jnp.stack([hi, lo]), br


def _lower_tri(tm):
    r = np.arange(tm)
    return jnp.asarray((r[None, :] < r[:, None]).astype(np.float32), BF16)


def _post_specs(rw):
    D, tm = rw.D, rw.tm
    return ([pl.BlockSpec((1, D), lambda i: (0, 0)), _mod_spec(rw, 3), _mod_spec(rw, 4),
             pl.BlockSpec((2, D, LANES), lambda i: (0, 0, 0)), pl.BlockSpec((1, LANES), lambda i: (0, 0)),
             pl.BlockSpec((tm, tm), lambda i: (0, 0))],
            [pl.BlockSpec((tm, D), lambda i: (i, 0)), pl.BlockSpec((tm, D // 2), lambda i: (i, 0)),
             pl.BlockSpec((tm, LANES), lambda i: (i, 0)), pl.BlockSpec((1, LANES), lambda i: (0, 0))])


def _post_shapes(nrows, D):
    return [jax.ShapeDtypeStruct((nrows, D), F32), jax.ShapeDtypeStruct((nrows, D // 2), jnp.uint32),
            jax.ShapeDtypeStruct((nrows, LANES), F32), jax.ShapeDtypeStruct((1, LANES), F32)]


def _att_outproj(rw, na, wa, cx, xl, xc, w_out, mods, norm_ffn, wr, br):
    D, tm, nlat = rw.D, rw.tm, rw.nlat
    post_in, post_out = _post_specs(rw)
    latmap = lambda i: (jnp.minimum(i, nlat - 1), 0)
    ctxmap = lambda i: (jnp.maximum(i - nlat, 0), 0)
    return pl.pallas_call(
        functools.partial(_att_outproj_kernel, nlat),
        grid=(rw.ntot,),
        in_specs=[pl.BlockSpec((tm, 512), latmap), pl.BlockSpec((tm, 512), latmap), pl.BlockSpec((tm, D), ctxmap),
                  pl.BlockSpec((tm, D), latmap), pl.BlockSpec((tm, D), ctxmap),
                  pl.BlockSpec((D, D), lambda i: (0, 0)), _mod_spec(rw, 2)] + post_in,
        out_specs=post_out,
        out_shape=_post_shapes(rw.rows, D),
        scratch_shapes=[pltpu.VMEM((1, LANES), F32)],
        compiler_params=_cparams(1),
        name="att_outproj_router",
    )(na, wa, cx, xl, xc, w_out.astype(BF16), mods, norm_ffn[None, :], mods, mods, wr, br, _lower_tri(tm))


def _moe_kernel(te_ref, nu_ref, src_ref, nsrc_ref, hp_ref, w13_ref, w2_ref, o_ref, w13b, w2b, xa, xb):
    i = pl.program_id(0)
    prev = te_ref[jnp.maximum(i - 1, 0)]
    changed = (i == 0) | (te_ref[i] != prev)
    tg = xa.shape[0]

    @pl.when(changed)
    def _():
        w13b[...] = w13_ref[...].astype(BF16)
        w2b[...] = w2_ref[...].astype(BF16)

    @pl.when(i == 0)
    def _():
        def fetch(j, carry):
            xa[pl.ds(j, 1), :] = hp_ref[pl.ds(src_ref[0, j], 1), :]
            return carry

        lax.fori_loop(0, tg, fetch, 0, unroll=8)

    def step(cur, nxt):
        for j in range(tg):
            nxt[pl.ds(j, 1), :] = hp_ref[pl.ds(nsrc_ref[0, j], 1), :]
        ff = w2b.shape[0]
        half = cur.shape[1]
        w = cur[...]
        unpack = functools.partial(pltpu.unpack_elementwise, packed_dtype=BF16, unpacked_dtype=F32)
        x_lo = unpack(w, index=0).astype(BF16)
        x_hi = unpack(w, index=1).astype(BF16)
        a13 = (jnp.dot(x_lo, w13b[:half, :], preferred_element_type=F32)
               + jnp.dot(x_hi, w13b[half:, :], preferred_element_type=F32))
        act = _silu(a13[:, :ff]) * a13[:, ff:]
        o_ref[...] = jnp.dot(act.astype(BF16), w2b[...], preferred_element_type=F32).astype(BF16)

    used = i < nu_ref[0]

    @pl.when(used & (i % 2 == 0))
    def _():
        step(xa, xb)

    @pl.when(used & (i % 2 == 1))
    def _():
        step(xb, xa)

    @pl.when(i >= nu_ref[0])
    def _():
        o_ref[...] = jnp.zeros_like(o_ref)


def _moe(h2p, route, counts, w13, w2, layer):
    N = h2p.shape[0]
    D = 2 * h2p.shape[1]
    _, E, _, F2 = w13.shape
    tg = MOE_TILE
    nt = (2 * N) // tg + E
    e = route[:, 0:2].astype(jnp.int32)
    rank = route[:, 4:6].astype(jnp.int32)
    cnt = counts[0, :E].astype(jnp.int32)
    ntile_e = (cnt + tg - 1) // tg
    tile_end = jnp.cumsum(ntile_e)
    offs = (tile_end - ntile_e) * tg
    onehot = (e[:, :, None] == jnp.arange(E, dtype=jnp.int32)).astype(jnp.int32)
    dest = jnp.sum(onehot * offs, axis=-1) + rank
    src = jnp.zeros((nt * tg,), jnp.int32).at[dest.reshape(-1)].set(jnp.repeat(jnp.arange(N, dtype=jnp.int32), 2))
    tile_id = jnp.arange(nt, dtype=jnp.int32)
    nu = tile_end[-1:].astype(jnp.int32)
    te = jnp.sum((tile_end[None, :] <= jnp.minimum(tile_id, nu[0] - 1)[:, None]).astype(jnp.int32), axis=1)
    te = jnp.minimum(te, E - 1)
    ys = pl.pallas_call(
        _moe_kernel,
        grid_spec=pltpu.PrefetchScalarGridSpec(
            num_scalar_prefetch=2,
            grid=(nt,),
            in_specs=[pl.BlockSpec((None, 1, tg), lambda i, te, nu: (i, 0, 0), memory_space=pltpu.SMEM),
                      pl.BlockSpec((None, 1, tg), lambda i, te, nu: (jnp.minimum(i + 1, nt - 1), 0, 0),
                                   memory_space=pltpu.SMEM),
                      pl.BlockSpec((N, D // 2), lambda i, te, nu: (0, 0), pipeline_mode=pl.Buffered(1)),
                      pl.BlockSpec((None, None, D, F2), lambda i, te, nu: (layer, te[i], 0, 0)),
                      pl.BlockSpec((None, None, F2 // 2, D), lambda i, te, nu: (layer, te[i], 0, 0))],
            out_specs=pl.BlockSpec((tg, D), lambda i, te, nu: (i, 0)),
            scratch_shapes=[pltpu.VMEM((D, F2), BF16), pltpu.VMEM((F2 // 2, D), BF16),
                            pltpu.VMEM((tg, D // 2), jnp.uint32), pltpu.VMEM((tg, D // 2), jnp.uint32)]),
        out_shape=jax.ShapeDtypeStruct((nt * tg, D), BF16),
        compiler_params=_cparams(1, vmem=MOE_VMEM_LIMIT),
        name="moe_experts",
    )(te, nu, src.reshape(nt, 1, tg), src.reshape(nt, 1, tg), h2p, w13, w2)
    pick = lambda k: ys.at[dest[:, k]].get(mode="promise_in_bounds")
    return pick(0), pick(1)


def _moe_residual(x_ref, y1_ref, y2_ref, rt_ref, g2_ref):
    rt = rt_ref[...]
    f = rt[:, 2:3] * y1_ref[...].astype(F32) + rt[:, 3:4] * y2_ref[...].astype(F32)
    return x_ref[...] + g2_ref[...] * f


def _combine_kernel(x_ref, y1_ref, y2_ref, rt_ref, g2_ref, o_ref):
    o_ref[...] = _moe_residual(x_ref, y1_ref, y2_ref, rt_ref, g2_ref)


def _combine(rw, ntiles, xall, y1, y2, route, mods):
    D, tm = rw.D, rw.tm
    row = lambda i: (i, 0)
    return pl.pallas_call(
        _combine_kernel,
        grid=(ntiles,),
        in_specs=[pl.BlockSpec((tm, D), row), pl.BlockSpec((tm, D), row), pl.BlockSpec((tm, D), row),
                  pl.BlockSpec((tm, LANES), row), _mod_spec(rw, 5)],
        out_specs=pl.BlockSpec((tm, D), row),
        out_shape=jax.ShapeDtypeStruct((ntiles * tm, D), F32),
        compiler_params=_cparams(1),
        name="moe_combine",
    )(xall, y1, y2, route, mods)


NA_QROWS = 8
NA_KROWS = 16


def _na_first_key_row(variant, a):
    return (max(a - 4, 0) + 4, a, min(a, 4))[variant]


def _na_key_lanes(row0):
    a = row0 // GRID_W
    starts = [_na_first_key_row(v, a) for v in range(3)]
    lo = (min(starts) * GRID_W) // LANES * LANES
    hi = -(-((max(starts) + NA_KH) * GRID_W) // LANES) * LANES
    return lo, hi


def _na_bias_tiles(rpb):
    H = rpb.shape[0]
    i = np.arange(GRID_W)
    c0 = np.clip(i - NA_KW // 2, 0, GRID_W - NA_KW)
    j = np.arange(GRID_W)
    colvalid = (j[None, :] >= c0[:, None]) & (j[None, :] < c0[:, None] + NA_KW)
    dc = np.clip(j[None, :] - i[:, None] + NA_KW - 1, 0, 2 * NA_KW - 2)
    onehot = ((dc[None] == np.arange(2 * NA_KW - 1)[:, None, None]) & colvalid[None]).astype(np.float32)
    tiles = jnp.einsum('hrc,cij->hrij', rpb.astype(F32), jnp.asarray(onehot), precision=lax.Precision.HIGHEST)
    tiles = tiles + jnp.asarray(np.where(colvalid, 0.0, NEG_INF).astype(np.float32))
    return tiles.transpose(0, 2, 1, 3).reshape(H, GRID_W, (2 * NA_KH - 1) * GRID_W)


def _na_fill_bias(variant, tiles_ref, bias_scr):
    for hh in range(2):
        for a in range(NA_QROWS):
            start = _na_first_key_row(variant, a)
            dr0 = start - a + 3
            rows = slice(a * GRID_W, (a + 1) * GRID_W)
            w0, w1 = start * GRID_W, (start + NA_KH) * GRID_W
            if w0 > 0:
                bias_scr[hh, rows, 0:w0] = jnp.full((GRID_W, w0), NEG_INF, F32)
            bias_scr[hh, rows, w0:w1] = tiles_ref[hh, :, dr0 * GRID_W:(dr0 + NA_KH) * GRID_W]
            if w1 < NA_KROWS * GRID_W:
                bias_scr[hh, rows, w1:] = jnp.full((GRID_W, NA_KROWS * GRID_W - w1), NEG_INF, F32)


def _softmax_pv(parts, extra=None, rc=64):
    m_rows = parts[0][0].shape[0]
    probs = [[] for _ in parts]
    inv_l = []
    for r0 in range(0, m_rows, rc):
        sc = []
        for s, _, bias_fn, lanes_fn in parts:
            l0, l1 = (0, s.shape[1]) if lanes_fn is None else lanes_fn(r0)
            c = s[r0:r0 + rc, l0:l1]
            if bias_fn is not None:
                c = c + bias_fn(r0, rc, slice(l0, l1))
            sc.append((c, l0, s.shape[1] - l1))
        mx = functools.reduce(jnp.maximum, [jnp.max(c, axis=-1, keepdims=True) for c, _, _ in sc])
        if extra is not None:
            mx = jnp.maximum(mx, extra[r0:r0 + rc])
        l = jnp.zeros_like(mx) if extra is None else jnp.exp(extra[r0:r0 + rc] - mx)
        for k, (c, before, after) in enumerate(sc):
            p = jnp.exp(c - mx)
            l = l + jnp.sum(p, axis=-1, keepdims=True)
            row = [jnp.zeros((rc, before), BF16)] * (before > 0) + [p.astype(BF16)] \
                + [jnp.zeros((rc, after), BF16)] * (after > 0)
            probs[k].append(row[0] if len(row) == 1 else jnp.concatenate(row, axis=1))
        inv_l.append(1.0 / l)
    o = None
    for k, (_, v, _, _) in enumerate(parts):
        pv = jnp.dot(jnp.concatenate(probs[k], axis=0), v, preferred_element_type=F32)
        o = pv if o is None else o + pv
    return o * jnp.concatenate(inv_l, axis=0)


def _nt(a, b):
    return lax.dot_general(a, b, (((1,), (1,)), ((), ())), preferred_element_type=F32)


NA_REFS_PER_BATCH = 11


def _na_kernel(n_rb, nb, *refs):
    tiles_ref, o_ref, bias_ref = refs[nb * NA_REFS_PER_BATCH:]
    rb = pl.program_id(1)
    for variant, at_rb in ((0, 0), (1, 1), (2, n_rb - 1)):
        @pl.when(rb == at_rb)
        def _(variant=variant):
            _na_fill_bias(variant, tiles_ref, bias_ref)

    for b in range(nb):
        q_ref, k0, k1, k2, k3, v0, v1, v2, v3, kc_ref, vc_ref = refs[b * NA_REFS_PER_BATCH:(b + 1) * NA_REFS_PER_BATCH]
        q2 = q_ref[...]
        kw = jnp.concatenate([k0[...], k1[...], k2[...], k3[...]], axis=0)
        vw = jnp.concatenate([v0[...], v1[...], v2[...], v3[...]], axis=0)
        kc = kc_ref[...]
        vc = vc_ref[...]
        lane = lax.broadcasted_iota(jnp.int32, q2.shape, 1)
        out = jnp.zeros(q2.shape, F32)
        for hh in range(2):
            m = (lane < HEAD_DIM) if hh == 0 else (lane >= HEAD_DIM)
            qm = jnp.where(m, q2, jnp.zeros_like(q2))
            o = _softmax_pv([(_nt(qm, kw), vw, lambda r0, rc, lanes, hh=hh: bias_ref[hh, r0:r0 + rc, lanes],
                              _na_key_lanes), (_nt(qm, kc), vc, None, None)], rc=32)
            out = jnp.where(m, o, out)
        o_ref[b] = out.astype(BF16)


def _na_attention(rw, qkv, rpb):
    B, T, C = rw.B, rw.T, rw.C
    tq = NA_QROWS * GRID_W
    tk = tq // 2
    n_rb = T // tq
    nkb = T // tk
    assert T % tq == 0 and n_rb >= 2 and (B * T) % C == 0
    tiles = _na_bias_tiles(rpb)
    ctxrow = (B * T) // C

    def batch_specs(b):
        kv = lambda j, col: pl.BlockSpec(
            (tk, LANES), lambda p, rb: (b * nkb + jnp.clip(2 * rb - 1 + j, 0, nkb - 1), col + p))
        return ([pl.BlockSpec((tq, LANES), lambda p, rb: (b * n_rb + rb, p))]
                + [kv(j, 4) for j in range(4)] + [kv(j, 8) for j in range(4)]
                + [pl.BlockSpec((C, LANES), lambda p, rb: (ctxrow + b, 4 + p)),
                   pl.BlockSpec((C, LANES), lambda p, rb: (ctxrow + b, 8 + p))])

    out = pl.pallas_call(
        functools.partial(_na_kernel, n_rb, B),
        grid=(4, n_rb),
        in_specs=sum([batch_specs(b) for b in range(B)], [])
        + [pl.BlockSpec((2,) + tiles.shape[1:], lambda p, rb: (p, 0, 0))],
        out_specs=pl.BlockSpec((B, tq, LANES), lambda p, rb: (0, rb, p)),
        out_shape=jax.ShapeDtypeStruct((B, T, 4 * LANES), BF16),
        scratch_shapes=[pltpu.VMEM((2, tq, 2 * tq), F32)],
        compiler_params=_cparams(2),
        name="neighbourhood_attention",
    )(*([qkv] * (B * NA_REFS_PER_BATCH)), tiles)
    return out.reshape(B * T, 4 * LANES)


WA_QBLOCKS = 2


def _wa_kernel(nb, sink_ref, q_ref, *refs):
    step = pl.program_id(1)
    blk = WA_BLOCK
    nkb = WA_QBLOCKS + 2
    kvblocks = refs[:nkb]
    kvx_ref, o_ref = refs[nkb:]
    lane = lax.broadcasted_iota(jnp.int32, (blk, LANES), 1)
    zero = jnp.zeros((blk, LANES), BF16)
    qi = lax.broadcasted_iota(jnp.int32, (blk, 3 * blk), 0)
    ks = lax.broadcasted_iota(jnp.int32, (blk, 3 * blk), 1)
    for qb in range(WA_QBLOCKS):
        n = step * WA_QBLOCKS + qb
        lo = jnp.where(n > 0, 0, blk)
        hi = jnp.where(n < nb - 1, 3 * blk, 2 * blk)
        valid = (ks >= qi) & (ks <= qi + 2 * blk) & (ks >= lo) & (ks < hi)
        band_mask = jnp.where(valid, 0.0, NEG_INF)
        rows = slice(qb * blk, (qb + 1) * blk)
        for kv in range(2):
            parts = []
            for pr in range(2):
                c0 = kv * 2 * LANES + pr * LANES
                qp = q_ref[rows, c0:c0 + LANES]
                parts += [jnp.where(lane < HEAD_DIM, qp, zero), jnp.where(lane >= HEAD_DIM, qp, zero)]
            qs = jnp.concatenate(parts, axis=0)
            ks_ = slice(kv * LANES, (kv + 1) * LANES)
            vs_ = slice((2 + kv) * LANES, (3 + kv) * LANES)
            kb = jnp.concatenate([r[:, ks_] for r in kvblocks[qb:qb + 3]], axis=0)
            vb = jnp.concatenate([r[:, vs_] for r in kvblocks[qb:qb + 3]], axis=0)
            sink = jnp.concatenate([jnp.full((blk, 1), sink_ref[kv * 4 + g], F32) for g in range(4)], axis=0)
            o = _softmax_pv([(_nt(qs, kb), vb, lambda r0, rc, lanes, m=band_mask: m[r0 % blk:r0 % blk + rc, lanes], None),
                             (_nt(qs, kvx_ref[:, ks_]), kvx_ref[:, vs_], None, None)], extra=sink, rc=64)
            c0 = kv * 2 * LANES
            o_ref[rows, c0:c0 + LANES] = jnp.where(lane < HEAD_DIM, o[0:blk], o[blk:2 * blk]).astype(BF16)
            o_ref[rows, c0 + LANES:c0 + 2 * LANES] = jnp.where(
                lane < HEAD_DIM, o[2 * blk:3 * blk], o[3 * blk:4 * blk]).astype(BF16)


def _wa_attention(rw, qkv, sink):
    B, T, C = rw.B, rw.T, rw.C
    blk = WA_BLOCK
    nb = T // blk
    nq = WA_QBLOCKS
    assert nb % nq == 0
    ctxrow = (B * T) // C

    def kvspec(j):
        return pl.BlockSpec((blk, 4 * LANES), lambda b, s: (b * nb + jnp.clip(nq * s - 1 + j, 0, nb - 1), 4))

    return pl.pallas_call(
        functools.partial(_wa_kernel, nb),
        grid=(B, nb // nq),
        in_specs=[pl.BlockSpec(memory_space=pltpu.SMEM),
                  pl.BlockSpec((nq * blk, 4 * LANES), lambda b, s: (b * (nb // nq) + s, 3))]
        + [kvspec(j) for j in range(nq + 2)]
        + [pl.BlockSpec((C, 4 * LANES), lambda b, s: (ctxrow + b, 4))],
        out_specs=pl.BlockSpec((nq * blk, 4 * LANES), lambda b, s: (b * (nb // nq) + s, 0)),
        out_shape=jax.ShapeDtypeStruct((B * T, 4 * LANES), BF16),
        compiler_params=_cparams(2),
        name="window_attention",
    )(sink.astype(F32), qkv, *([qkv] * (nq + 3)))


def _ctx_attn_kernel(sink_ref, t_ref, o_ref):
    C = t_ref.shape[0]
    lane = lax.broadcasted_iota(jnp.int32, (C, LANES), 1)
    zero = jnp.zeros((C, LANES), BF16)

    def pair(q2, k2, v2, sinks):
        out = jnp.zeros((C, LANES), F32)
        for hh in range(2):
            m = (lane < HEAD_DIM) if hh == 0 else (lane >= HEAD_DIM)
            extra = None if sinks is None else jnp.full((C, 1), sinks[hh], F32)
            o = _softmax_pv([(_nt(jnp.where(m, q2, zero), k2), v2, None, None)], extra=extra, rc=64)
            out = jnp.where(m, o, out)
        return out.astype(BF16)

    for p in range(4):
        c = p * LANES
        o_ref[:, c:c + LANES] = pair(t_ref[:, c:c + LANES], t_ref[:, 512 + c:640 + c], t_ref[:, 1024 + c:1152 + c], None)
    for kv in range(2):
        kd = t_ref[:, 2048 + kv * LANES:2176 + kv * LANES]
        vd = t_ref[:, 2304 + kv * LANES:2432 + kv * LANES]
        for pr in range(2):
            c = kv * 256 + pr * LANES
            h0 = kv * 4 + pr * 2
            o_ref[:, 512 + c:640 + c] = pair(t_ref[:, 1536 + c:1664 + c], kd, vd, (sink_ref[h0], sink_ref[h0 + 1]))


def _ctx_attention(rw, qkv, sink):
    B, T, C = rw.B, rw.T, rw.C
    ctxrow = (B * T) // C
    return pl.pallas_call(
        _ctx_attn_kernel,
        grid=(B,),
        in_specs=[pl.BlockSpec(memory_space=pltpu.SMEM),
                  pl.BlockSpec((C, qkv.shape[1]), lambda b: (ctxrow + b, 0))],
        out_specs=pl.BlockSpec((C, 8 * LANES), lambda b: (b, 0)),
        out_shape=jax.ShapeDtypeStruct((B * C, 8 * LANES), BF16),
        compiler_params=_cparams(1),
        name="context_attention",
    )(sink.astype(F32), qkv)


S5_Q = 16
CONV_TILE = 256
CONV_HALO = 16


def _ssm_inproj_kernel(x_ref, y1_ref, y2_ref, rt_ref, g2_ref, g_ref, sh_ref, sc_ref, w_ref,
                       xo_ref, z_ref, xbc_ref, u_ref, uj_ref, dt_ref, h_scr, u_scr):
    xn = _moe_residual(x_ref, y1_ref, y2_ref, rt_ref, g2_ref)
    xo_ref[...] = xn
    h_scr[...] = _ada_norm(xn, g_ref[...], sh_ref[...], sc_ref[...]).astype(BF16)

    def mm(c0, n):
        return jnp.dot(h_scr[...], w_ref[:, c0:c0 + n], preferred_element_type=F32)

    nz, nxbc, nu = z_ref.shape[1], xbc_ref.shape[1], u_ref.shape[1]
    for c0 in range(0, nz, MXU_W):
        z_ref[:, c0:c0 + MXU_W] = mm(c0, MXU_W).astype(BF16)
    for c0 in range(0, nxbc, MXU_W):
        xbc_ref[:, c0:c0 + MXU_W] = mm(nz + c0, MXU_W).astype(BF16)
    for c0 in range(0, nu, MXU_W):
        y = mm(nz + nxbc + c0, MXU_W)
        u_ref[:, c0:c0 + MXU_W] = y.astype(BF16)
        for t in range(MXU_W // LANES):
            u_scr[c0 // LANES + t] = y[:, t * LANES:(t + 1) * LANES]
    dt_ref[...] = mm(nz + nxbc + nu, LANES)
    nchunk = u_scr.shape[1] // S5_Q
    for j in range(S5_Q):
        for t in range(u_scr.shape[0]):
            uj_ref[j, :, t * LANES:(t + 1) * LANES] = u_scr[t, pl.ds(j, nchunk, stride=S5_Q), :].astype(BF16)


def _ssm_inproj(rw, xall, y1, y2, route, prev_mods, mods, norm_g, w_in):
    D, tm = rw.D, rw.tm
    w = jnp.concatenate([w_in[:, 0:2560], w_in[:, 2592:3104], w_in[:, 2560:2592], jnp.zeros((D, LANES - 32), F32)],
                        axis=1).astype(BF16)
    row = lambda i: (i, 0)
    return pl.pallas_call(
        _ssm_inproj_kernel,
        grid=(rw.ntot,),
        in_specs=[pl.BlockSpec((tm, D), row), pl.BlockSpec((tm, D), row), pl.BlockSpec((tm, D), row),
                  pl.BlockSpec((tm, LANES), row), _mod_spec(rw, 5), pl.BlockSpec((1, D), lambda i: (0, 0)),
                  _mod_spec(rw, 0), _mod_spec(rw, 1), pl.BlockSpec((D, 3200), lambda i: (0, 0))],
        out_specs=[pl.BlockSpec((tm, D), row),
                   pl.BlockSpec((tm, 1024), row), pl.BlockSpec((tm, 1536), row), pl.BlockSpec((tm, 512), row),
                   pl.BlockSpec((S5_Q, tm // S5_Q, 512), lambda i: (0, i, 0)), pl.BlockSpec((tm, LANES), row)],
        out_shape=[jax.ShapeDtypeStruct((rw.rows, D), F32),
                   jax.ShapeDtypeStruct((rw.rows, 1024), BF16), jax.ShapeDtypeStruct((rw.rows, 1536), BF16),
                   jax.ShapeDtypeStruct((rw.rows, 512), BF16),
                   jax.ShapeDtypeStruct((S5_Q, rw.rows // S5_Q, 512), BF16),
                   jax.ShapeDtypeStruct((rw.rows, LANES), F32)],
        scratch_shapes=[pltpu.VMEM((tm, D), BF16), pltpu.VMEM((512 // LANES, tm, LANES), F32)],
        compiler_params=_cparams(1),
        name="ssm_inproj",
    )(xall, y1, y2, route, prev_mods, norm_g[None, :], mods, mods, w)


def _softplus(x):
    return jnp.maximum(x, 0.0) + jnp.log(1.0 + jnp.exp(-jnp.abs(x)))


def _conv_kernel(lat_tiles, tpb, cpb, x_ref, pv_ref, nx_ref, w_ref, b_ref, dtr_ref, dtb_ref, act_ref, dt_ref):
    i = pl.program_id(0)
    is_lat = i < lat_tiles
    pos = jnp.where(is_lat, i % tpb, (i - lat_tiles) % cpb)
    last_pos = jnp.where(is_lat, tpb - 1, cpb - 1)
    x = x_ref[...].astype(F32)
    tc = x.shape[0]
    prev_row = jnp.where(pos == 0, 0.0, pv_ref[...].astype(F32)[CONV_HALO - 1:CONV_HALO, :])
    next_row = jnp.where(pos == last_pos, 0.0, nx_ref[...].astype(F32)[0:1, :])
    row = lax.broadcasted_iota(jnp.int32, x.shape, 0)
    xm1 = jnp.where(row == 0, prev_row, pltpu.roll(x, 1, 0))
    xp1 = jnp.where(row == tc - 1, next_row, pltpu.roll(x, tc - 1, 0))
    y = w_ref[0:1, :] * xm1 + w_ref[1:2, :] * x + w_ref[2:3, :] * xp1 + b_ref[...]
    act_ref[...] = _silu(y).astype(BF16)
    sp = _softplus(dtr_ref[...] + dtb_ref[...])
    dt_ref[0] = sp
    dt_ref[1] = pltpu.roll(sp, LANES - 16, 1)


def _ssm_conv(rw, xbc, dtr, conv_w, conv_b, dt_bias):
    B, T, C = rw.B, rw.T, rw.C
    tc = CONV_TILE
    assert T % tc == 0 and C % tc == 0
    lat_tiles, tpb, cpb = (B * T) // tc, T // tc, C // tc
    ntiles = rw.rows // tc
    hpt = tc // CONV_HALO
    nhalo = rw.rows // CONV_HALO
    W = xbc.shape[1]
    dtb = jnp.concatenate([dt_bias.reshape(-1), jnp.zeros((LANES - 32,), F32)])[None, :]
    row = lambda i: (i, 0)
    return pl.pallas_call(
        functools.partial(_conv_kernel, lat_tiles, tpb, cpb),
        grid=(ntiles,),
        in_specs=[pl.BlockSpec((tc, W), row),
                  pl.BlockSpec((CONV_HALO, W), lambda i: (jnp.maximum(i * hpt - 1, 0), 0)),
                  pl.BlockSpec((CONV_HALO, W), lambda i: (jnp.minimum((i + 1) * hpt, nhalo - 1), 0)),
                  pl.BlockSpec((3, W), lambda i: (0, 0)), pl.BlockSpec((1, W), lambda i: (0, 0)),
                  pl.BlockSpec((tc, LANES), row), pl.BlockSpec((1, LANES), lambda i: (0, 0))],
        out_specs=[pl.BlockSpec((tc, W), row), pl.BlockSpec((2, tc, LANES), lambda i: (0, i, 0))],
        out_shape=[jax.ShapeDtypeStruct((rw.rows, W), BF16), jax.ShapeDtypeStruct((2, rw.rows, LANES), F32)],
        compiler_params=_cparams(1),
        name="ssm_conv",
    )(xbc, xbc, xbc, conv_w, conv_b[None, :], dtr, dtb)


def _ssd_kernel(nb, *refs):
    acts, dts = refs[0:2 * nb], refs[2 * nb:4 * nb]
    tri_ref, a_ref, yf_ref, yb_ref, hst = refs[4 * nb:]

    @pl.when(pl.program_id(0) == 0)
    def _():
        hst[...] = jnp.zeros_like(hst)

    for d, y_ref in enumerate((yf_ref, yb_ref)):
        for b in range(nb):
            _ssd_chunk(acts[d * nb + b], dts[d * nb + b], tri_ref[d], a_ref[d], y_ref.at[b], hst.at[d, b])


def _ssd_chunk(act_ref, dt_ref, tri, avec, y_ref, hst):
    q = SSD_CHUNK
    dt = dt_ref[...]
    da = dt * avec
    acs = jnp.dot(tri, da, preferred_element_type=F32, precision=lax.Precision.HIGHEST)
    tot = jnp.sum(da, axis=0, keepdims=True)
    acs_t = acs.T
    dt_t = dt.T
    eacs = jnp.exp(acs)
    wend = jnp.exp(tot - acs) * dt
    etot = jnp.exp(tot)
    mask = tri > 0.5
    left = lax.broadcasted_iota(jnp.int32, (q, LANES), 1) < HEAD_DIM
    left1 = lax.broadcasted_iota(jnp.int32, (1, LANES), 1) < HEAD_DIM
    for g in range(2):
        bg = act_ref[:, 1024 + g * 128:1152 + g * 128]
        cg = act_ref[:, 1280 + g * 128:1408 + g * 128]
        cb = _nt(cg, bg)
        hin = hst[:, g * 512:(g + 1) * 512]
        yoff = jnp.dot(cg, hin.astype(BF16), preferred_element_type=F32)
        xw, dec = [], []
        for pr in range(4):
            h_a = g * 8 + pr * 2
            h_b = h_a + 1
            c0 = h_a * HEAD_DIM
            x2 = act_ref[:, c0:c0 + LANES]
            outs = []
            for h in (h_a, h_b):
                seg = acs[:, h:h + 1] - acs_t[h:h + 1, :]
                w = cb * jnp.exp(jnp.where(mask, seg, NEG_INF)) * dt_t[h:h + 1, :]
                outs.append(jnp.dot(w.astype(BF16), x2, preferred_element_type=F32))
            yd = jnp.where(left, outs[0], outs[1])
            sc = jnp.where(left, eacs[:, h_a:h_a + 1], eacs[:, h_b:h_b + 1])
            y_ref[:, c0:c0 + LANES] = (yd + yoff[:, pr * LANES:(pr + 1) * LANES] * sc).astype(BF16)
            wsc = jnp.where(left, wend[:, h_a:h_a + 1], wend[:, h_b:h_b + 1])
            xw.append((x2.astype(F32) * wsc).astype(BF16))
            dec.append(jnp.where(left1, etot[:, h_a:h_a + 1], etot[:, h_b:h_b + 1]))
        bg_t = bg.astype(F32).T.astype(BF16)
        snew = jnp.dot(bg_t, jnp.concatenate(xw, axis=1), preferred_element_type=F32)
        hst[:, g * 512:(g + 1) * 512] = hin * jnp.concatenate(dec, axis=1) + snew


def _ssd(rw, act, dt2, a_log):
    B, T, C = rw.B, rw.T, rw.C
    q = SSD_CHUNK
    nct, nlt = C // q, T // q
    ctx0 = (B * T) // q
    r = np.arange(q)
    tri = jnp.asarray(np.stack([r[None, :] <= r[:, None], r[None, :] >= r[:, None]]).astype(np.float32))
    avec = jnp.concatenate([-jnp.exp(a_log.astype(F32)), jnp.zeros((2, LANES - a_log.shape[1]), F32)], axis=1)[:, None, :]

    def lat(d, s):
        return jnp.clip(s - nct, 0, nlt - 1) if d == 0 else nlt - 1 - jnp.clip(s - nct, 0, nlt - 1)

    def blk(d, b, s):
        kc = s if d == 0 else nct - 1 - s
        return jnp.where(s < nct, ctx0 + b * nct + kc, b * nlt + lat(d, s))

    pairs = [(d, b) for d in range(2) for b in range(B)]
    aspec = lambda d, b: pl.BlockSpec((q, act.shape[1]), lambda s: (blk(d, b, s), 0))
    dspec = lambda d, b: pl.BlockSpec((None, q, LANES), lambda s: (d, blk(d, b, s), 0))
    yspec = lambda d: pl.BlockSpec((B, q, 1024), lambda s: (0, lat(d, s), 0))
    yf, yb = pl.pallas_call(
        functools.partial(_ssd_kernel, B),
        grid=(nct + nlt,),
        in_specs=[aspec(d, b) for d, b in pairs] + [dspec(d, b) for d, b in pairs]
        + [pl.BlockSpec((2, q, q), lambda s: (0, 0, 0)), pl.BlockSpec((2, 1, LANES), lambda s: (0, 0, 0))],
        out_specs=[yspec(0), yspec(1)],
        out_shape=[jax.ShapeDtypeStruct((B, T, 1024), BF16)] * 2,
        scratch_shapes=[pltpu.VMEM((2, B, q, 1024), F32)],
        compiler_params=_cparams(1),
        name="ssd_scan",
    )(*([act] * (2 * B)), *([dt2] * (2 * B)), tri, avec)
    return yf.reshape(B * T, 1024), yb.reshape(B * T, 1024)


def _cmul(ar, ai, br, bi):
    return ar * br - ai * bi, ar * bi + ai * br


def _s5_weight_kernel(lre_ref, lim_ref, ls_ref, bre_ref, bim_ref, cre_ref, cim_ref,
                      wsr_ref, wsi_ref, wor_ref, woi_ref, bt_ref, are_ref, aim_ref):
    lre, lim = lre_ref[...], lim_ref[...]
    step = jnp.exp(ls_ref[...])
    er, ei = lre * step, lim * step
    npow = 24
    p = lax.broadcasted_iota(jnp.int32, (1, npow, 1), 1).astype(F32)
    mag = jnp.exp(p * er)
    pre, pim = mag * jnp.cos(p * ei), mag * jnp.sin(p * ei)
    a_re, a_im = pre[:, 1:2, :], pim[:, 1:2, :]
    den = lre * lre + lim * lim
    q_re = ((a_re - 1.0) * lre + a_im * lim) / den
    q_im = (a_im * lre - (a_re - 1.0) * lim) / den
    bb_re, bb_im = _cmul(q_re, q_im, bre_ref[...], bim_ref[...])
    c_re, c_im = cre_ref[...], cim_ref[...]
    ws_r, ws_i, wo_r, wo_i, ca_r, ca_i = [], [], [], [], [], []
    for t in range(S5_Q):
        r, i = _cmul(bb_re, bb_im, pre[:, t:t + 1, :], pim[:, t:t + 1, :])
        ws_r.append(r)
        ws_i.append(i)
        r, i = _cmul(c_re, c_im, pre[:, t:t + 1, :], pim[:, t:t + 1, :])
        ca_r.append(r)
        ca_i.append(i)
        r, i = _cmul(c_re, c_im, pre[:, t + 1:t + 2, :], pim[:, t + 1:t + 2, :])
        wo_r.append(r)
        wo_i.append(-i)
    cat = lambda xs: jnp.concatenate(xs, axis=1)
    is_fwd = pl.program_id(0) < pl.num_programs(0) // 2

    def packed(blocks, reverse_fwd):
        w = jnp.where(is_fwd if reverse_fwd else jnp.logical_not(is_fwd), cat(blocks[::-1]), cat(blocks))
        w2 = jnp.concatenate([w, w], axis=2)
        g = lax.broadcasted_iota(jnp.int32, w2.shape, 0)
        ln = lax.broadcasted_iota(jnp.int32, w2.shape, 2)
        return jnp.where((g % 2 == 0) == (ln < w.shape[2]), w2, 0.0).astype(BF16)

    wsr_ref[...] = packed(ws_r, True)
    wsi_ref[...] = packed(ws_i, True)
    wor_ref[...] = packed(wo_r, False)
    woi_ref[...] = packed(wo_i, False)
    bdot = lambda a, b: lax.dot_general(a, b, (((2,), (2,)), ((0,), (0,))), preferred_element_type=F32,
                                        precision=lax.Precision.HIGHEST)
    kin = jnp.where(is_fwd, bdot(bb_re, cat(ca_r)) - bdot(bb_im, cat(ca_i)),
                    bdot(bb_re, cat(ca_r[::-1])) - bdot(bb_im, cat(ca_i[::-1])))
    qc = kin.shape[2]
    lane = lax.broadcasted_iota(jnp.int32, kin.shape, 2)
    rows = []
    for j in range(S5_Q):
        fwd = jnp.where(lane >= j * S5_GROUP, pltpu.roll(kin, j * S5_GROUP, 2), 0.0)
        back = (S5_Q - 1 - j) * S5_GROUP
        bwd = jnp.where(lane < qc - back, pltpu.roll(kin, (qc - back) % qc, 2), 0.0)
        rows.append(jnp.where(is_fwd, fwd, bwd))
    bt_ref[...] = cat(rows).astype(BF16)
    are_ref[...] = pre[:, S5_Q:S5_Q + 1, :]
    aim_ref[...] = pim[:, S5_Q:S5_Q + 1, :]


def _s5_weights(lam_re, lam_im, log_step, b_re, b_im, c_re, c_im):
    nd, ng, ns = lam_re.shape
    G = nd * ng
    ch = S5_GROUP
    gb = 8
    qc = S5_Q * ch
    f = lambda a: a.astype(F32)
    args = (f(lam_re).reshape(G, 1, ns), f(lam_im).reshape(G, 1, ns), f(log_step).reshape(G, 1, 1),
            f(b_re).reshape(G, ns, ch).transpose(0, 2, 1), f(b_im).reshape(G, ns, ch).transpose(0, 2, 1),
            f(c_re).reshape(G, ch, ns), f(c_im).reshape(G, ch, ns))
    spec = lambda a: pl.BlockSpec((gb,) + a.shape[1:], lambda i: (i, 0, 0))
    assert nd == 2 and (G // gb) % 2 == 0
    oshape = [jax.ShapeDtypeStruct((G, qc, 2 * ns), BF16)] * 4 + [jax.ShapeDtypeStruct((G, qc, qc), BF16)] \
        + [jax.ShapeDtypeStruct((G, 1, ns), F32)] * 2
    ws_r, ws_i, wo_r, wo_i, bt, a_re, a_im = pl.pallas_call(
        _s5_weight_kernel,
        grid=(G // gb,),
        in_specs=[spec(a) for a in args],
        out_specs=[pl.BlockSpec((gb,) + s.shape[1:], lambda i: (i, 0, 0)) for s in oshape],
        out_shape=oshape,
        compiler_params=_cparams(1),
        name="s5_weights",
    )(*args)
    by_dir = lambda w: w.reshape((nd, ng) + w.shape[1:])
    pair = lambda a: a.reshape(nd, ng // 2, 1, 2 * ns)
    return by_dir(bt), by_dir(ws_r), by_dir(ws_i), by_dir(wo_r), by_dir(wo_i), pair(a_re), pair(a_im)


S5_GB = LANES // S5_GROUP


def _s5_kernel(B, nct, nlt, uj_ref, perm_ref, bt_ref, wsr_ref, wsi_ref, wor_ref, woi_ref, are_ref, aim_ref, yj_ref,
               x_scr, y_scr, s_re, s_im):
    gb, npair, qc = S5_GB, S5_GB // 2, S5_Q * S5_GROUP
    lhs = jnp.concatenate([uj_ref[j] for j in range(S5_Q)], axis=1)
    for m in range(gb):
        x_scr[:, m * qc:(m + 1) * qc] = jnp.dot(lhs, perm_ref[:, m * qc:(m + 1) * qc],
                                                preferred_element_type=F32).astype(BF16)
    xg = lambda g: x_scr[:, g * qc:(g + 1) * qc]
    for d in range(2):
        for pr in range(npair):
            for dst, w_ref in ((s_re, wsr_ref), (s_im, wsi_ref)):
                dst[d, pr] = (jnp.dot(xg(2 * pr), w_ref[d, 2 * pr], preferred_element_type=F32)
                              + jnp.dot(xg(2 * pr + 1), w_ref[d, 2 * pr + 1], preferred_element_type=F32))
    chains = [(d, pr, b) for d in range(2) for pr in range(npair) for b in range(B)]
    coef = {(d, pr): (are_ref[d, pr], aim_ref[d, pr]) for d in range(2) for pr in range(npair)}
    ctx0 = B * nlt

    def body(s, carry):
        in_ctx = s < nct
        rows = {}
        for d in range(2):
            kc = s if d == 0 else nct - 1 - s
            kl = s - nct if d == 0 else nlt - 1 - (s - nct)
            for b in range(B):
                rows[(d, b)] = pl.ds(jnp.where(in_ctx, ctx0 + b * nct + kc, b * nlt + kl), 1)
        contrib = [(s_re[d, pr, rows[(d, b)], :], s_im[d, pr, rows[(d, b)], :]) for d, pr, b in chains]
        new = []
        for (d, pr, b), (hr, hi), (sr, si) in zip(chains, carry, contrib):
            ar, ai = coef[(d, pr)]
            s_re[d, pr, rows[(d, b)], :] = hr
            s_im[d, pr, rows[(d, b)], :] = hi
            new.append((ar * hr - ai * hi + sr, ar * hi + ai * hr + si))
        return tuple(new)

    zero = jnp.zeros((1, LANES), F32)
    lax.fori_loop(0, nct + nlt, body, tuple((zero, zero) for _ in chains))
    for g in range(gb):
        acc = None
        for d in range(2):
            t = (jnp.dot(xg(g), bt_ref[d, g], preferred_element_type=F32)
                 + _nt(s_re[d, g // 2].astype(BF16), wor_ref[d, g])
                 + _nt(s_im[d, g // 2].astype(BF16), woi_ref[d, g]))
            acc = t if acc is None else acc + t
        y_scr[:, g * qc:(g + 1) * qc] = acc.astype(BF16)
    for i in range(S5_Q):
        yj_ref[i] = _nt(y_scr[...], perm_ref[i * LANES:(i + 1) * LANES, :]).astype(BF16)


def _s5(rw, uj, weights):
    B, T, C = rw.B, rw.T, rw.C
    bt, ws_r, ws_i, wo_r, wo_i, a_re, a_im = weights
    ng = bt.shape[1]
    q, gb = S5_Q, S5_GB
    nct, nlt = C // q, T // q
    nrow = uj.shape[1]
    qc = q * S5_GROUP
    k = gb * qc
    idx = np.arange(k)
    j, m, c = idx // LANES, (idx % LANES) // S5_GROUP, idx % S5_GROUP
    perm = np.zeros((k, k), np.float32)
    perm[idx, m * qc + j * S5_GROUP + c] = 1.0
    once = dict(pipeline_mode=pl.Buffered(1))
    wspec = lambda n: pl.BlockSpec((2, gb, qc, n), lambda i: (0, i, 0, 0), **once)
    aspec = pl.BlockSpec((2, gb // 2, 1, LANES), lambda i: (0, i, 0, 0))
    return pl.pallas_call(
        functools.partial(_s5_kernel, B, nct, nlt),
        grid=(ng // gb,),
        in_specs=[pl.BlockSpec((q, nrow, LANES), lambda i: (0, 0, i), **once),
                  pl.BlockSpec((k, k), lambda i: (0, 0), **once),
                  wspec(qc), wspec(LANES), wspec(LANES), wspec(LANES), wspec(LANES), aspec, aspec],
        out_specs=pl.BlockSpec((q, nrow, LANES), lambda i: (0, 0, i)),
        out_shape=jax.ShapeDtypeStruct(uj.shape, BF16),
        scratch_shapes=[pltpu.VMEM((nrow, k), BF16), pltpu.VMEM((nrow, k), BF16),
                        pltpu.VMEM((2, gb // 2, nrow, LANES), F32), pltpu.VMEM((2, gb // 2, nrow, LANES), F32)],
        compiler_params=_cparams(1),
        name="s5_scan",
    )(uj, jnp.asarray(perm, BF16), bt, ws_r, ws_i, wo_r, wo_i, a_re, a_im)


def _gelu_tanh(x):
    return 0.5 * x * (1.0 + jnp.tanh(math.sqrt(2.0 / math.pi) * (x + 0.044715 * (x * x * x))))


def _ssm_outproj_kernel(y0_ref, y1_ref, xs_ref, z_ref, v_ref, u_ref, dsk_ref, nw_ref, s5d_ref, gw_ref, gb_ref,
                        x_ref, w_ref, g1_ref, gn_ref, sh2_ref, sc2_ref, wr_ref, br_ref, lt_ref,
                        xo_ref, h2_ref, rt_ref, cnt_ref, carry, v_scr):
    i = pl.program_id(0)
    y = y0_ref[...].astype(F32) + y1_ref[...].astype(F32) + dsk_ref[...] * xs_ref[...].astype(F32)
    y = _rms(y * _silu(z_ref[...].astype(F32))) * nw_ref[...]
    ntile = v_scr.shape[0]
    nchunk = v_scr.shape[1] // S5_Q
    for j in range(S5_Q):
        for t in range(ntile):
            v_scr[t, pl.ds(j, nchunk, stride=S5_Q), :] = v_ref[j, :, t * LANES:(t + 1) * LANES].astype(F32)
    s5_y = jnp.concatenate([v_scr[t] for t in range(ntile)], axis=1)
    v = _gelu_tanh(s5_y + s5d_ref[...] * u_ref[...].astype(F32))
    v = v * _sigmoid(jnp.dot(v.astype(BF16), gw_ref[...], preferred_element_type=F32) + gb_ref[...])
    mix = jnp.concatenate([y, v], axis=1).astype(BF16)
    yo = jnp.dot(mix, w_ref[...], preferred_element_type=F32)
    _post_mixer(i, x_ref[...], yo, g1_ref[...], gn_ref[...], sh2_ref[...], sc2_ref[...], wr_ref, br_ref, lt_ref,
                xo_ref, h2_ref, rt_ref, cnt_ref, carry)


def _ssm_outproj(rw, ssd_y, act, z, s5_y, u, d_skip, norm_w, s5_d, glu_w, glu_b, xall, w_out, mods, norm_ffn, wr, br):
    D, tm = rw.D, rw.tm
    ntiles = rw.nlat
    post_in, post_out = _post_specs(rw)
    row = lambda i: (i, 0)
    vec = lambda n: pl.BlockSpec((1, n), lambda i: (0, 0))
    dsk = jnp.repeat(d_skip.astype(F32), HEAD_DIM)[None, :]
    return pl.pallas_call(
        _ssm_outproj_kernel,
        grid=(ntiles,),
        in_specs=[pl.BlockSpec((tm, 1024), row), pl.BlockSpec((tm, 1024), row),
                  pl.BlockSpec((tm, 1024), row), pl.BlockSpec((tm, 1024), row),
                  pl.BlockSpec((S5_Q, tm // S5_Q, 512), lambda i: (0, i, 0)),
                  pl.BlockSpec((tm, 512), row), vec(1024), vec(1024), vec(512),
                  pl.BlockSpec((512, 512), lambda i: (0, 0)), vec(512),
                  pl.BlockSpec((tm, D), row), pl.BlockSpec((1536, D), lambda i: (0, 0)), _mod_spec(rw, 2)] + post_in,
        out_specs=post_out,
        out_shape=_post_shapes(ntiles * tm, D),
        scratch_shapes=[pltpu.VMEM((1, LANES), F32), pltpu.VMEM((512 // LANES, tm, LANES), F32)],
        compiler_params=_cparams(1),
        name="ssm_outproj_router",
    )(ssd_y[0], ssd_y[1], act, z, s5_y, u, dsk, norm_w[None, :], s5_d[None, :], glu_w.astype(BF16), glu_b[None, :],
      xall, w_out.astype(BF16), mods, norm_ffn[None, :], mods, mods, wr, br, _lower_tri(tm))


def kernel(x, c, ctx, c_ctx, mod_w, mod_b, norm_mix, norm_ffn, att_w_in, att_w_out, na_q_norm, na_k_norm, na_rel_bias, wa_q_norm, wa_k_norm, wa_sink, ssm_w_in, ssm_w_out, ssd_conv_w, ssd_conv_b, ssd_dt_bias, ssd_a_log, ssd_d, ssd_norm, s5_lambda_re, s5_lambda_im, s5_log_step, s5_b_re, s5_b_im, s5_c_re, s5_c_im, s5_d, s5_glu_w, s5_glu_b, moe_w_group, moe_b_group, moe_w_expert, moe_b_expert, moe_w13, moe_w2):
    B, T, D = x.shape
    C = ctx.shape[1]
    rw = _Rows(B, T, C, D, ROW_TILE)
    xl = x.reshape(B * T, D)
    xc = ctx.reshape(B * C, D)
    cm = jnp.concatenate([c, c_ctx[None, :], jnp.zeros((8 - B - 1, D), F32)], axis=0)
    mods = _modulation(cm, mod_w, mod_b)
    mods = mods.reshape(mods.shape[0], 8, 1, 6 * D)

    m0 = mods[0]
    qkv = _att_inproj(rw, xl, xc, m0, norm_mix[0], att_w_in[0], na_q_norm[0], na_k_norm[0], wa_q_norm[0],
                      wa_k_norm[0])
    na = _na_attention(rw, qkv, na_rel_bias[0])
    wa = _wa_attention(rw, qkv, wa_sink[0])
    cx = _ctx_attention(rw, qkv, wa_sink[0])
    wr, br = _router_weights(moe_w_group[0], moe_b_group[0], moe_w_expert[0], moe_b_expert[0])
    xall, h2, route, counts = _att_outproj(rw, na, wa, cx, xl, xc, att_w_out[0], m0, norm_ffn[0], wr, br)
    y1, y2 = _moe(h2, route, counts, moe_w13, moe_w2, 0)

    m1 = mods[1]
    xall, z, xbc, u, uj, dtr = _ssm_inproj(rw, xall, y1, y2, route, m0, m1, norm_mix[1], ssm_w_in[0])
    act, dt2 = _ssm_conv(rw, xbc, dtr, ssd_conv_w[0], ssd_conv_b[0], ssd_dt_bias[0])
    ssd_y = _ssd(rw, act, dt2, ssd_a_log[0])
    s5_w = _s5_weights(s5_lambda_re[0], s5_lambda_im[0], s5_log_step[0], s5_b_re[0], s5_b_im[0], s5_c_re[0],
                       s5_c_im[0])
    s5_y = _s5(rw, uj, s5_w)
    wr, br = _router_weights(moe_w_group[1], moe_b_group[1], moe_w_expert[1], moe_b_expert[1])
    xlat, h2, route, counts = _ssm_outproj(rw, ssd_y, act, z, s5_y, u, ssd_d[0], ssd_norm[0], s5_d[0], s5_glu_w[0],
                                           s5_glu_b[0], xall, ssm_w_out[0], m1, norm_ffn[1], wr, br)
    y1, y2 = _moe(h2, route, counts, moe_w13, moe_w2, 1)
    out = _combine(rw, rw.nlat, xlat, y1, y2, route, m1)
    return out.reshape(B, T, D)
```

```python
import functools
import math

import jax
import jax.numpy as jnp
import numpy as np
from jax import lax
from jax.experimental import pallas as pl
from jax.experimental.pallas import tpu as pltpu

F32 = jnp.float32
BF16 = jnp.bfloat16

EPS = 1e-6
NEG_INF = -1e30
GRID_W = 64
HEAD_DIM = 64
NA_KH = 8
NA_KW = 16
WA_BLOCK = 128
ROPE_BASE = 10000.0
SSD_CHUNK = 128
S5_GROUP = 16
MOE_GROUPS = 4
MOE_EPG = 8
MOE_EXPERTS = MOE_GROUPS * MOE_EPG

LANES = 128
ROW_TILE = 512
MXU_W = 256
MOE_TILE = MXU_W
VMEM_LIMIT = 56 * 1024 * 1024
MOE_VMEM_LIMIT = 60 * 1024 * 1024


def _cparams(n_axes, vmem=VMEM_LIMIT):
    return pltpu.CompilerParams(dimension_semantics=("arbitrary",) * n_axes, vmem_limit_bytes=vmem)


def _sigmoid(x):
    return jax.nn.sigmoid(x)


def _silu(x):
    return x * _sigmoid(x)


def _rms(x, eps=EPS):
    return x * lax.rsqrt(jnp.mean(x * x, axis=-1, keepdims=True) + eps)


def _ada_norm(x, g, shift, scale):
    return (_rms(x) * g) * (1.0 + scale) + shift


def _mod_kernel(c_ref, w_ref, b_ref, o_ref):
    a = _silu(c_ref[...])
    o_ref[...] = jnp.dot(a, w_ref[...], preferred_element_type=F32, precision=lax.Precision.HIGHEST) + b_ref[...]


def _modulation(cm, mod_w, mod_b):
    depth, d, n6 = mod_w.shape
    tn = 1024
    return pl.pallas_call(
        _mod_kernel,
        grid=(depth, n6 // tn),
        in_specs=[pl.BlockSpec((8, d), lambda l, j: (0, 0)),
                  pl.BlockSpec((None, d, tn), lambda l, j: (l, 0, j)),
                  pl.BlockSpec((None, 1, tn), lambda l, j: (l, 0, j))],
        out_specs=pl.BlockSpec((None, 8, tn), lambda l, j: (l, 0, j)),
        out_shape=jax.ShapeDtypeStruct((depth, 8, n6), F32),
        compiler_params=_cparams(2),
        name="modulation",
    )(cm, mod_w, mod_b.reshape(depth, 1, n6))


class _Rows:
    def __init__(self, B, T, C, D, tm):
        assert T % tm == 0 and (B * C) % tm == 0
        self.B, self.T, self.C, self.D, self.tm = B, T, C, D, tm
        self.tpb = T // tm
        self.nlat = B * self.tpb
        self.nctx = (B * C) // tm
        self.ntot = self.nlat + self.nctx
        self.rows = B * (T + C)

    def group(self, i):
        return jnp.where(i < self.nlat, i // self.tpb, self.B)


def _mod_spec(rw, col):
    return pl.BlockSpec((None, 1, rw.D), lambda i, *_: (rw.group(i), 0, col))


def _seg_norm(y, seg, gcol):
    ss = jnp.dot((y * y).astype(BF16), seg, preferred_element_type=F32)
    return y * lax.rsqrt(ss + EPS) * gcol


def _rope(y, cos, sin):
    w = y.shape[-1]
    lane = lax.broadcasted_iota(jnp.int32, y.shape, 1)
    first = (lane % 32) < 16
    partner = jnp.where(first, pltpu.roll(y, w - 16, 1), pltpu.roll(y, 16, 1))
    return y * cos + partner * sin


def _dup_halves(k):
    lane = lax.broadcasted_iota(jnp.int32, k.shape, 1)
    sw = pltpu.roll(k, 64, 1)
    return jnp.where(lane < 64, k, sw), jnp.where(lane < 64, sw, k)


def _att_inproj_kernel(nlat, xl_ref, xc_ref, g_ref, sh_ref, sc_ref, w_ref, gcol_ref, cos_ref, sin_ref, seg_ref,
                       o_ref, h_scr):
    i = pl.program_id(0)
    x = jnp.where(i < nlat, xl_ref[...], xc_ref[...])
    h_scr[...] = _ada_norm(x, g_ref[...], sh_ref[...], sc_ref[...]).astype(BF16)
    seg = seg_ref[...]
    cos2 = jnp.concatenate([cos_ref[...], cos_ref[...]], axis=1)
    sin2 = jnp.concatenate([sin_ref[...], sin_ref[...]], axis=1)
    for c in range(w_ref.shape[1] // MXU_W):
        c0 = c * MXU_W
        cols = slice(c0, c0 + MXU_W)
        y = jnp.dot(h_scr[...], w_ref[:, cols], preferred_element_type=F32)
        gcol = gcol_ref[:, cols]
        if c in (0, 1, 2, 3):
            o_ref[:, cols] = _seg_norm(y, seg, gcol).astype(BF16)
        elif c in (4, 5):
            o_ref[:, cols] = y.astype(BF16)
        elif c in (6, 7):
            o_ref[:, cols] = _rope(_seg_norm(y, seg, gcol), cos2, sin2).astype(BF16)
        else:
            lane = lax.broadcasted_iota(jnp.int32, y.shape, 1)
            yk = jnp.where(lane < LANES, _seg_norm(y, seg, gcol), y)
            yr = jnp.where(lane < LANES, _rope(yk, cos2, sin2), yk)
            k0, k1 = _dup_halves(yr[:, :LANES])
            v0, v1 = _dup_halves(yr[:, LANES:])
            for t, dup in enumerate((k0, k1, v0, v1)):
                o_ref[:, c0 + t * LANES:c0 + (t + 1) * LANES] = dup.astype(BF16)


def _rope_tables(T, tm):
    t = np.arange(T)
    d = np.arange(HEAD_DIM)
    nf = HEAD_DIM // 4
    inv = jnp.asarray(ROPE_BASE, F32) ** (-jnp.arange(nf, dtype=F32) / nf)
    pos = np.where((d // 32 == 0)[None, :], (t // GRID_W)[:, None], (t % GRID_W)[:, None])
    ang = jnp.asarray(pos, F32) * inv[d % nf][None, :]
    sign = np.where((d % 32) < 16, -1.0, 1.0).astype(np.float32)
    cos = jnp.cos(ang)
    sin = jnp.sin(ang) * sign[None, :]
    cos = jnp.concatenate([cos, jnp.ones((tm, HEAD_DIM), F32)], axis=0)
    sin = jnp.concatenate([sin, jnp.zeros((tm, HEAD_DIM), F32)], axis=0)
    return jnp.tile(cos, (1, 2)), jnp.tile(sin, (1, 2))


def _att_inproj(rw, xl, xc, mods, norm_g, w_in, na_qn, na_kn, wa_qn, wa_kn):
    D, tm = rw.D, rw.tm
    scale = HEAD_DIM ** -0.5
    gcol = jnp.concatenate([jnp.tile(na_qn * scale, 8), jnp.tile(na_kn, 8), jnp.ones((512,), F32),
                            jnp.tile(wa_qn * scale, 8), jnp.tile(wa_kn, 2), jnp.ones((128,), F32)])[None, :]
    cos, sin = _rope_tables(rw.T, tm)
    segn = np.arange(256) // 64
    seg = jnp.asarray((segn[:, None] == segn[None, :]).astype(np.float32) / 64.0, BF16)
    nlat, tpb = rw.nlat, rw.tpb
    return pl.pallas_call(
        functools.partial(_att_inproj_kernel, nlat),
        grid=(rw.ntot,),
        in_specs=[pl.BlockSpec((tm, D), lambda i: (jnp.minimum(i, nlat - 1), 0)),
                  pl.BlockSpec((tm, D), lambda i: (jnp.maximum(i - nlat, 0), 0)),
                  pl.BlockSpec((1, D), lambda i: (0, 0)),
                  _mod_spec(rw, 0), _mod_spec(rw, 1),
                  pl.BlockSpec((D, 2304), lambda i: (0, 0)),
                  pl.BlockSpec((1, 2304), lambda i: (0, 0)),
                  pl.BlockSpec((tm, 128), lambda i: (jnp.where(i < nlat, i % tpb, tpb), 0)),
                  pl.BlockSpec((tm, 128), lambda i: (jnp.where(i < nlat, i % tpb, tpb), 0)),
                  pl.BlockSpec((256, 256), lambda i: (0, 0))],
        out_specs=pl.BlockSpec((tm, 2560), lambda i: (i, 0)),
        out_shape=jax.ShapeDtypeStruct((rw.rows, 2560), BF16),
        scratch_shapes=[pltpu.VMEM((tm, D), BF16)],
        compiler_params=_cparams(1),
        name="att_inproj",
    )(xl, xc, norm_g[None, :], mods, mods, w_in.astype(BF16), gcol, cos, sin, seg)


def _route(lg, lt, carry):
    lane = lax.broadcasted_iota(jnp.int32, lg.shape, 1).astype(F32)
    gm = lane < MOE_GROUPS
    mg = jnp.max(jnp.where(gm, lg, NEG_INF), axis=-1, keepdims=True)
    eg = jnp.where(gm, jnp.exp(jnp.where(gm, lg, NEG_INF) - mg), 0.0)
    pg = eg / jnp.sum(eg, axis=-1, keepdims=True)
    ptop = jnp.max(pg, axis=-1, keepdims=True)
    gidx = jnp.min(jnp.where(gm & (pg == ptop), lane, 1e9), axis=-1, keepdims=True)
    lo = MOE_GROUPS + MOE_EPG * gidx
    em = (lane >= lo) & (lane < lo + MOE_EPG)
    le = jnp.where(em, lg, NEG_INF)
    ee = jnp.where(em, jnp.exp(le - jnp.max(le, axis=-1, keepdims=True)), 0.0)
    pe = ee / jnp.sum(ee, axis=-1, keepdims=True)
    v1 = jnp.max(jnp.where(em, pe, -1.0), axis=-1, keepdims=True)
    i1 = jnp.min(jnp.where(em & (pe == v1), lane, 1e9), axis=-1, keepdims=True)
    em2 = em & (lane != i1)
    v2 = jnp.max(jnp.where(em2, pe, -1.0), axis=-1, keepdims=True)
    i2 = jnp.min(jnp.where(em2 & (pe == v2), lane, 1e9), axis=-1, keepdims=True)
    den = v1 + v2
    w1 = v1 / den * ptop
    w2 = v2 / den * ptop
    e1 = i1 - MOE_GROUPS
    e2 = i2 - MOE_GROUPS
    m1 = lane == e1
    m2 = lane == e2
    oh = jnp.where(m1 | m2, 1.0, 0.0)
    cnt = jnp.dot(lt, oh.astype(BF16), preferred_element_type=F32) + carry
    r1 = jnp.sum(jnp.where(m1, cnt, 0.0), axis=-1, keepdims=True)
    r2 = jnp.sum(jnp.where(m2, cnt, 0.0), axis=-1, keepdims=True)
    route = jnp.where(lane == 0, e1, jnp.where(lane == 1, e2, jnp.where(lane == 2, w1, jnp.where(
        lane == 3, w2, jnp.where(lane == 4, r1, jnp.where(lane == 5, r2, 0.0))))))
    return route, carry + jnp.sum(oh, axis=0, keepdims=True)


def _post_mixer(i, x, y, g1, gn, sh2, sc2, wr_ref, br_ref, lt_ref, xo_ref, h2_ref, rt_ref, cnt_ref, carry):
    xn = x + g1 * y
    xo_ref[...] = xn
    h2 = _ada_norm(xn, gn, sh2, sc2)
    hb = h2.astype(BF16)
    hbf = hb.astype(F32)
    half = h2.shape[1] // 2
    h2_ref[...] = pltpu.pack_elementwise([h2[:, :half], h2[:, half:]], packed_dtype=BF16)
    hl = (h2 - hbf).astype(BF16)
    lg = (jnp.dot(hb, wr_ref[0], preferred_element_type=F32)
          + (jnp.dot(hb, wr_ref[1], preferred_element_type=F32) + jnp.dot(hl, wr_ref[0], preferred_element_type=F32))
          + br_ref[...])

    @pl.when(i == 0)
    def _():
        carry[...] = jnp.zeros_like(carry)

    route, newc = _route(lg, lt_ref[...], carry[...])
    rt_ref[...] = route
    carry[...] = newc
    cnt_ref[...] = newc


def _att_outproj_kernel(nlat, na_ref, wa_ref, cx_ref, xl_ref, xc_ref, w_ref, g1_ref, gn_ref, sh2_ref, sc2_ref,
                        wr_ref, br_ref, lt_ref, xo_ref, h2_ref, rt_ref, cnt_ref, carry):
    i = pl.program_id(0)
    lat = i < nlat
    mix = jnp.where(lat, jnp.concatenate([na_ref[...], wa_ref[...]], axis=1), cx_ref[...])
    y = jnp.dot(mix, w_ref[...], preferred_element_type=F32)
    x = jnp.where(lat, xl_ref[...], xc_ref[...])
    _post_mixer(i, x, y, g1_ref[...], gn_ref[...], sh2_ref[...], sc2_ref[...], wr_ref, br_ref, lt_ref,
                xo_ref, h2_ref, rt_ref, cnt_ref, carry)


def _router_weights(w_group, b_group, w_expert, b_expert):
    D = w_group.shape[0]
    pad = LANES - MOE_GROUPS - MOE_EXPERTS
    wr = jnp.concatenate([w_group, w_expert, jnp.zeros((D, pad), F32)], axis=1)
    br = jnp.concatenate([b_group, b_expert, jnp.zeros((pad,), F32)])[None, :]
    hi = wr.astype(BF16)
    lo = (wr - hi.astype(F32)).astype(BF16)
    return jnp.stack([hi, lo]), br


def _lower_tri(tm):
    r = np.arange(tm)
    return jnp.asarray((r[None, :] < r[:, None]).astype(np.float32), BF16)


def _post_specs(rw):
    D, tm = rw.D, rw.tm
    return ([pl.BlockSpec((1, D), lambda i: (0, 0)), _mod_spec(rw, 3), _mod_spec(rw, 4),
             pl.BlockSpec((2, D, LANES), lambda i: (0, 0, 0)), pl.BlockSpec((1, LANES), lambda i: (0, 0)),
             pl.BlockSpec((tm, tm), lambda i: (0, 0))],
            [pl.BlockSpec((tm, D), lambda i: (i, 0)), pl.BlockSpec((tm, D // 2), lambda i: (i, 0)),
             pl.BlockSpec((tm, LANES), lambda i: (i, 0)), pl.BlockSpec((1, LANES), lambda i: (0, 0))])


def _post_shapes(nrows, D):
    return [jax.ShapeDtypeStruct((nrows, D), F32), jax.ShapeDtypeStruct((nrows, D // 2), jnp.uint32),
            jax.ShapeDtypeStruct((nrows, LANES), F32), jax.ShapeDtypeStruct((1, LANES), F32)]


def _att_outproj(rw, na, wa, cx, xl, xc, w_out, mods, norm_ffn, wr, br):
    D, tm, nlat = rw.D, rw.tm, rw.nlat
    post_in, post_out = _post_specs(rw)
    latmap = lambda i: (jnp.minimum(i, nlat - 1), 0)
    ctxmap = lambda i: (jnp.maximum(i - nlat, 0), 0)
    return pl.pallas_call(
        functools.partial(_att_outproj_kernel, nlat),
        grid=(rw.ntot,),
        in_specs=[pl.BlockSpec((tm, 512), latmap), pl.BlockSpec((tm, 512), latmap), pl.BlockSpec((tm, D), ctxmap),
                  pl.BlockSpec((tm, D), latmap), pl.BlockSpec((tm, D), ctxmap),
                  pl.BlockSpec((D, D), lambda i: (0, 0)), _mod_spec(rw, 2)] + post_in,
        out_specs=post_out,
        out_shape=_post_shapes(rw.rows, D),
        scratch_shapes=[pltpu.VMEM((1, LANES), F32)],
        compiler_params=_cparams(1),
        name="att_outproj_router",
    )(na, wa, cx, xl, xc, w_out.astype(BF16), mods, norm_ffn[None, :], mods, mods, wr, br, _lower_tri(tm))


def _moe_kernel(te_ref, nu_ref, src_ref, nsrc_ref, hp_ref, w13_ref, w2_ref, o_ref, w13b, w2b, xa, xb):
    i = pl.program_id(0)
    prev = te_ref[jnp.maximum(i - 1, 0)]
    changed = (i == 0) | (te_ref[i] != prev)
    tg = xa.shape[0]

    @pl.when(changed)
    def _():
        w13b[...] = w13_ref[...].astype(BF16)
        w2b[...] = w2_ref[...].astype(BF16)

    @pl.when(i == 0)
    def _():
        def fetch(j, carry):
            xa[pl.ds(j, 1), :] = hp_ref[pl.ds(src_ref[0, j], 1), :]
            return carry

        lax.fori_loop(0, tg, fetch, 0, unroll=8)

    def step(cur, nxt):
        for j in range(tg):
            nxt[pl.ds(j, 1), :] = hp_ref[pl.ds(nsrc_ref[0, j], 1), :]
        ff = w2b.shape[0]
        half = cur.shape[1]
        w = cur[...]
        unpack = functools.partial(pltpu.unpack_elementwise, packed_dtype=BF16, unpacked_dtype=F32)
        x_lo = unpack(w, index=0).astype(BF16)
        x_hi = unpack(w, index=1).astype(BF16)
        a13 = (jnp.dot(x_lo, w13b[:half, :], preferred_element_type=F32)
               + jnp.dot(x_hi, w13b[half:, :], preferred_element_type=F32))
        act = _silu(a13[:, :ff]) * a13[:, ff:]
        o_ref[...] = jnp.dot(act.astype(BF16), w2b[...], preferred_element_type=F32).astype(BF16)

    used = i < nu_ref[0]

    @pl.when(used & (i % 2 == 0))
    def _():
        step(xa, xb)

    @pl.when(used & (i % 2 == 1))
    def _():
        step(xb, xa)

    @pl.when(i >= nu_ref[0])
    def _():
        o_ref[...] = jnp.zeros_like(o_ref)


def _moe(h2p, route, counts, w13, w2, layer):
    N = h2p.shape[0]
    D = 2 * h2p.shape[1]
    _, E, _, F2 = w13.shape
    tg = MOE_TILE
    nt = (2 * N) // tg + E
    e = route[:, 0:2].astype(jnp.int32)
    rank = route[:, 4:6].astype(jnp.int32)
    cnt = counts[0, :E].astype(jnp.int32)
    ntile_e = (cnt + tg - 1) // tg
    tile_end = jnp.cumsum(ntile_e)
    offs = (tile_end - ntile_e) * tg
    onehot = (e[:, :, None] == jnp.arange(E, dtype=jnp.int32)).astype(jnp.int32)
    dest = jnp.sum(onehot * offs, axis=-1) + rank
    src = jnp.zeros((nt * tg,), jnp.int32).at[dest.reshape(-1)].set(jnp.repeat(jnp.arange(N, dtype=jnp.int32), 2))
    tile_id = jnp.arange(nt, dtype=jnp.int32)
    nu = tile_end[-1:].astype(jnp.int32)
    te = jnp.sum((tile_end[None, :] <= jnp.minimum(tile_id, nu[0] - 1)[:, None]).astype(jnp.int32), axis=1)
    te = jnp.minimum(te, E - 1)
    ys = pl.pallas_call(
        _moe_kernel,
        grid_spec=pltpu.PrefetchScalarGridSpec(
            num_scalar_prefetch=2,
            grid=(nt,),
            in_specs=[pl.BlockSpec((None, 1, tg), lambda i, te, nu: (i, 0, 0), memory_space=pltpu.SMEM),
                      pl.BlockSpec((None, 1, tg), lambda i, te, nu: (jnp.minimum(i + 1, nt - 1), 0, 0),
                                   memory_space=pltpu.SMEM),
                      pl.BlockSpec((N, D // 2), lambda i, te, nu: (0, 0), pipeline_mode=pl.Buffered(1)),
                      pl.BlockSpec((None, None, D, F2), lambda i, te, nu: (layer, te[i], 0, 0)),
                      pl.BlockSpec((None, None, F2 // 2, D), lambda i, te, nu: (layer, te[i], 0, 0))],
            out_specs=pl.BlockSpec((tg, D), lambda i, te, nu: (i, 0)),
            scratch_shapes=[pltpu.VMEM((D, F2), BF16), pltpu.VMEM((F2 // 2, D), BF16),
                            pltpu.VMEM((tg, D // 2), jnp.uint32), pltpu.VMEM((tg, D // 2), jnp.uint32)]),
        out_shape=jax.ShapeDtypeStruct((nt * tg, D), BF16),
        compiler_params=_cparams(1, vmem=MOE_VMEM_LIMIT),
        name="moe_experts",
    )(te, nu, src.reshape(nt, 1, tg), src.reshape(nt, 1, tg), h2p, w13, w2)
    pick = lambda k: ys.at[dest[:, k]].get(mode="promise_in_bounds")
    return pick(0), pick(1)


def _moe_residual(x_ref, y1_ref, y2_ref, rt_ref, g2_ref):
    rt = rt_ref[...]
    f = rt[:, 2:3] * y1_ref[...].astype(F32) + rt[:, 3:4] * y2_ref[...].astype(F32)
    return x_ref[...] + g2_ref[...] * f


def _combine_kernel(x_ref, y1_ref, y2_ref, rt_ref, g2_ref, o_ref):
    o_ref[...] = _moe_residual(x_ref, y1_ref, y2_ref, rt_ref, g2_ref)


def _combine(rw, ntiles, xall, y1, y2, route, mods):
    D, tm = rw.D, rw.tm
    row = lambda i: (i, 0)
    return pl.pallas_call(
        _combine_kernel,
        grid=(ntiles,),
        in_specs=[pl.BlockSpec((tm, D), row), pl.BlockSpec((tm, D), row), pl.BlockSpec((tm, D), row),
                  pl.BlockSpec((tm, LANES), row), _mod_spec(rw, 5)],
        out_specs=pl.BlockSpec((tm, D), row),
        out_shape=jax.ShapeDtypeStruct((ntiles * tm, D), F32),
        compiler_params=_cparams(1),
        name="moe_combine",
    )(xall, y1, y2, route, mods)


NA_QROWS = 8
NA_KROWS = 16


def _na_first_key_row(variant, a):
    return (max(a - 4, 0) + 4, a, min(a, 4))[variant]


def _na_key_lanes(row0):
    a = row0 // GRID_W
    starts = [_na_first_key_row(v, a) for v in range(3)]
    lo = (min(starts) * GRID_W) // LANES * LANES
    hi = -(-((max(starts) + NA_KH) * GRID_W) // LANES) * LANES
    return lo, hi


def _na_bias_tiles(rpb):
    H = rpb.shape[0]
    i = np.arange(GRID_W)
    c0 = np.clip(i - NA_KW // 2, 0, GRID_W - NA_KW)
    j = np.arange(GRID_W)
    colvalid = (j[None, :] >= c0[:, None]) & (j[None, :] < c0[:, None] + NA_KW)
    dc = np.clip(j[None, :] - i[:, None] + NA_KW - 1, 0, 2 * NA_KW - 2)
    onehot = ((dc[None] == np.arange(2 * NA_KW - 1)[:, None, None]) & colvalid[None]).astype(np.float32)
    tiles = jnp.einsum('hrc,cij->hrij', rpb.astype(F32), jnp.asarray(onehot), precision=lax.Precision.HIGHEST)
    tiles = tiles + jnp.asarray(np.where(colvalid, 0.0, NEG_INF).astype(np.float32))
    return tiles.transpose(0, 2, 1, 3).reshape(H, GRID_W, (2 * NA_KH - 1) * GRID_W)


def _na_fill_bias(variant, tiles_ref, bias_scr):
    for hh in range(2):
        for a in range(NA_QROWS):
            start = _na_first_key_row(variant, a)
            dr0 = start - a + 3
            rows = slice(a * GRID_W, (a + 1) * GRID_W)
            w0, w1 = start * GRID_W, (start + NA_KH) * GRID_W
            if w0 > 0:
                bias_scr[hh, rows, 0:w0] = jnp.full((GRID_W, w0), NEG_INF, F32)
            bias_scr[hh, rows, w0:w1] = tiles_ref[hh, :, dr0 * GRID_W:(dr0 + NA_KH) * GRID_W]
            if w1 < NA_KROWS * GRID_W:
                bias_scr[hh, rows, w1:] = jnp.full((GRID_W, NA_KROWS * GRID_W - w1), NEG_INF, F32)


def _softmax_pv(parts, extra=None, rc=64):
    m_rows = parts[0][0].shape[0]
    probs = [[] for _ in parts]
    inv_l = []
    for r0 in range(0, m_rows, rc):
        sc = []
        for s, _, bias_fn, lanes_fn in parts:
            l0, l1 = (0, s.shape[1]) if lanes_fn is None else lanes_fn(r0)
            c = s[r0:r0 + rc, l0:l1]
            if bias_fn is not None:
                c = c + bias_fn(r0, rc, slice(l0, l1))
            sc.append((c, l0, s.shape[1] - l1))
        mx = functools.reduce(jnp.maximum, [jnp.max(c, axis=-1, keepdims=True) for c, _, _ in sc])
        if extra is not None:
            mx = jnp.maximum(mx, extra[r0:r0 + rc])
        l = jnp.zeros_like(mx) if extra is None else jnp.exp(extra[r0:r0 + rc] - mx)
        for k, (c, before, after) in enumerate(sc):
            p = jnp.exp(c - mx)
            l = l + jnp.sum(p, axis=-1, keepdims=True)
            row = [jnp.zeros((rc, before), BF16)] * (before > 0) + [p.astype(BF16)] \
                + [jnp.zeros((rc, after), BF16)] * (after > 0)
            probs[k].append(row[0] if len(row) == 1 else jnp.concatenate(row, axis=1))
        inv_l.append(1.0 / l)
    o = None
    for k, (_, v, _, _) in enumerate(parts):
        pv = jnp.dot(jnp.concatenate(probs[k], axis=0), v, preferred_element_type=F32)
        o = pv if o is None else o + pv
    return o * jnp.concatenate(inv_l, axis=0)


def _nt(a, b):
    return lax.dot_general(a, b, (((1,), (1,)), ((), ())), preferred_element_type=F32)


NA_REFS_PER_BATCH = 11


def _na_kernel(n_rb, nb, *refs):
    tiles_ref, o_ref, bias_ref = refs[nb * NA_REFS_PER_BATCH:]
    rb = pl.program_id(1)
    for variant, at_rb in ((0, 0), (1, 1), (2, n_rb - 1)):
        @pl.when(rb == at_rb)
        def _(variant=variant):
            _na_fill_bias(variant, tiles_ref, bias_ref)

    for b in range(nb):
        q_ref, k0, k1, k2, k3, v0, v1, v2, v3, kc_ref, vc_ref = refs[b * NA_REFS_PER_BATCH:(b + 1) * NA_REFS_PER_BATCH]
        q2 = q_ref[...]
        kw = jnp.concatenate([k0[...], k1[...], k2[...], k3[...]], axis=0)
        vw = jnp.concatenate([v0[...], v1[...], v2[...], v3[...]], axis=0)
        kc = kc_ref[...]
        vc = vc_ref[...]
        lane = lax.broadcasted_iota(jnp.int32, q2.shape, 1)
        out = jnp.zeros(q2.shape, F32)
        for hh in range(2):
            m = (lane < HEAD_DIM) if hh == 0 else (lane >= HEAD_DIM)
            qm = jnp.where(m, q2, jnp.zeros_like(q2))
            o = _softmax_pv([(_nt(qm, kw), vw, lambda r0, rc, lanes, hh=hh: bias_ref[hh, r0:r0 + rc, lanes],
                              _na_key_lanes), (_nt(qm, kc), vc, None, None)], rc=32)
            out = jnp.where(m, o, out)
        o_ref[b] = out.astype(BF16)


def _na_attention(rw, qkv, rpb):
    B, T, C = rw.B, rw.T, rw.C
    tq = NA_QROWS * GRID_W
    tk = tq // 2
    n_rb = T // tq
    nkb = T // tk
    assert T % tq == 0 and n_rb >= 2 and (B * T) % C == 0
    tiles = _na_bias_tiles(rpb)
    ctxrow = (B * T) // C

    def batch_specs(b):
        kv = lambda j, col: pl.BlockSpec(
            (tk, LANES), lambda p, rb: (b * nkb + jnp.clip(2 * rb - 1 + j, 0, nkb - 1), col + p))
        return ([pl.BlockSpec((tq, LANES), lambda p, rb: (b * n_rb + rb, p))]
                + [kv(j, 4) for j in range(4)] + [kv(j, 8) for j in range(4)]
                + [pl.BlockSpec((C, LANES), lambda p, rb: (ctxrow + b, 4 + p)),
                   pl.BlockSpec((C, LANES), lambda p, rb: (ctxrow + b, 8 + p))])

    out = pl.pallas_call(
        functools.partial(_na_kernel, n_rb, B),
        grid=(4, n_rb),
        in_specs=sum([batch_specs(b) for b in range(B)], [])
        + [pl.BlockSpec((2,) + tiles.shape[1:], lambda p, rb: (p, 0, 0))],
        out_specs=pl.BlockSpec((B, tq, LANES), lambda p, rb: (0, rb, p)),
        out_shape=jax.ShapeDtypeStruct((B, T, 4 * LANES), BF16),
        scratch_shapes=[pltpu.VMEM((2, tq, 2 * tq), F32)],
        compiler_params=_cparams(2),
        name="neighbourhood_attention",
    )(*([qkv] * (B * NA_REFS_PER_BATCH)), tiles)
    return out.reshape(B * T, 4 * LANES)


WA_QBLOCKS = 2


def _wa_kernel(nb, sink_ref, q_ref, *refs):
    step = pl.program_id(1)
    blk = WA_BLOCK
    nkb = WA_QBLOCKS + 2
    kblocks, vblocks = refs[:nkb], refs[nkb:2 * nkb]
    kx_ref, vx_ref, o_ref = refs[2 * nkb:]
    lane = lax.broadcasted_iota(jnp.int32, (blk, LANES), 1)
    zero = jnp.zeros((blk, LANES), BF16)
    qi = lax.broadcasted_iota(jnp.int32, (blk, 3 * blk), 0)
    ks = lax.broadcasted_iota(jnp.int32, (blk, 3 * blk), 1)
    for qb in range(WA_QBLOCKS):
        n = step * WA_QBLOCKS + qb
        lo = jnp.where(n > 0, 0, blk)
        hi = jnp.where(n < nb - 1, 3 * blk, 2 * blk)
        valid = (ks >= qi) & (ks <= qi + 2 * blk) & (ks >= lo) & (ks < hi)
        band_mask = jnp.where(valid, 0.0, NEG_INF)
        rows = slice(qb * blk, (qb + 1) * blk)
        for kv in range(2):
            parts = []
            for pr in range(2):
                c0 = kv * 2 * LANES + pr * LANES
                qp = q_ref[rows, c0:c0 + LANES]
                parts += [jnp.where(lane < HEAD_DIM, qp, zero), jnp.where(lane >= HEAD_DIM, qp, zero)]
            qs = jnp.concatenate(parts, axis=0)
            cs = slice(kv * LANES, (kv + 1) * LANES)
            kb = jnp.concatenate([r[:, cs] for r in kblocks[qb:qb + 3]], axis=0)
            vb = jnp.concatenate([r[:, cs] for r in vblocks[qb:qb + 3]], axis=0)
            sink = jnp.concatenate([jnp.full((blk, 1), sink_ref[kv * 4 + g], F32) for g in range(4)], axis=0)
            o = _softmax_pv([(_nt(qs, kb), vb, lambda r0, rc, lanes, m=band_mask: m[r0 % blk:r0 % blk + rc, lanes], None),
                             (_nt(qs, kx_ref[:, cs]), vx_ref[:, cs], None, None)], extra=sink, rc=64)
            c0 = kv * 2 * LANES
            o_ref[rows, c0:c0 + LANES] = jnp.where(lane < HEAD_DIM, o[0:blk], o[blk:2 * blk]).astype(BF16)
            o_ref[rows, c0 + LANES:c0 + 2 * LANES] = jnp.where(
                lane < HEAD_DIM, o[2 * blk:3 * blk], o[3 * blk:4 * blk]).astype(BF16)


def _wa_attention(rw, qkv, sink):
    B, T, C = rw.B, rw.T, rw.C
    blk = WA_BLOCK
    nb = T // blk
    nq = WA_QBLOCKS
    assert nb % nq == 0
    ctxrow = (B * T) // C

    def kvspec(j, col):
        return pl.BlockSpec((blk, 2 * LANES), lambda b, s: (b * nb + jnp.clip(nq * s - 1 + j, 0, nb - 1), col))

    return pl.pallas_call(
        functools.partial(_wa_kernel, nb),
        grid=(B, nb // nq),
        in_specs=[pl.BlockSpec(memory_space=pltpu.SMEM),
                  pl.BlockSpec((nq * blk, 4 * LANES), lambda b, s: (b * (nb // nq) + s, 3))]
        + [kvspec(j, 8) for j in range(nq + 2)] + [kvspec(j, 9) for j in range(nq + 2)]
        + [pl.BlockSpec((C, 2 * LANES), lambda b, s: (ctxrow + b, 8)),
           pl.BlockSpec((C, 2 * LANES), lambda b, s: (ctxrow + b, 9))],
        out_specs=pl.BlockSpec((nq * blk, 4 * LANES), lambda b, s: (b * (nb // nq) + s, 0)),
        out_shape=jax.ShapeDtypeStruct((B * T, 4 * LANES), BF16),
        compiler_params=_cparams(2),
        name="window_attention",
    )(sink.astype(F32), qkv, *([qkv] * (2 * nq + 6)))


def _ctx_attn_kernel(sink_ref, t_ref, o_ref):
    C = t_ref.shape[0]
    lane = lax.broadcasted_iota(jnp.int32, (C, LANES), 1)
    zero = jnp.zeros((C, LANES), BF16)

    def pair(q2, k2, v2, sinks):
        out = jnp.zeros((C, LANES), F32)
        for hh in range(2):
            m = (lane < HEAD_DIM) if hh == 0 else (lane >= HEAD_DIM)
            extra = None if sinks is None else jnp.full((C, 1), sinks[hh], F32)
            o = _softmax_pv([(_nt(jnp.where(m, q2, zero), k2), v2, None, None)], extra=extra, rc=64)
            out = jnp.where(m, o, out)
        return out.astype(BF16)

    for p in range(4):
        c = p * LANES
        o_ref[:, c:c + LANES] = pair(t_ref[:, c:c + LANES], t_ref[:, 512 + c:640 + c], t_ref[:, 1024 + c:1152 + c], None)
    for kv in range(2):
        kd = t_ref[:, 2048 + kv * LANES:2176 + kv * LANES]
        vd = t_ref[:, 2304 + kv * LANES:2432 + kv * LANES]
        for pr in range(2):
            c = kv * 256 + pr * LANES
            h0 = kv * 4 + pr * 2
            o_ref[:, 512 + c:640 + c] = pair(t_ref[:, 1536 + c:1664 + c], kd, vd, (sink_ref[h0], sink_ref[h0 + 1]))


def _ctx_attention(rw, qkv, sink):
    B, T, C = rw.B, rw.T, rw.C
    ctxrow = (B * T) // C
    return pl.pallas_call(
        _ctx_attn_kernel,
        grid=(B,),
        in_specs=[pl.BlockSpec(memory_space=pltpu.SMEM),
                  pl.BlockSpec((C, qkv.shape[1]), lambda b: (ctxrow + b, 0))],
        out_specs=pl.BlockSpec((C, 8 * LANES), lambda b: (b, 0)),
        out_shape=jax.ShapeDtypeStruct((B * C, 8 * LANES), BF16),
        compiler_params=_cparams(1),
        name="context_attention",
    )(sink.astype(F32), qkv)


S5_Q = 16
CONV_TILE = 256
CONV_HALO = 16


def _ssm_inproj_kernel(x_ref, y1_ref, y2_ref, rt_ref, g2_ref, g_ref, sh_ref, sc_ref, w_ref,
                       xo_ref, z_ref, xbc_ref, u_ref, uj_ref, dt_ref, h_scr, u_scr):
    xn = _moe_residual(x_ref, y1_ref, y2_ref, rt_ref, g2_ref)
    xo_ref[...] = xn
    h_scr[...] = _ada_norm(xn, g_ref[...], sh_ref[...], sc_ref[...]).astype(BF16)

    def mm(c0, n):
        return jnp.dot(h_scr[...], w_ref[:, c0:c0 + n], preferred_element_type=F32)

    nz, nxbc, nu = z_ref.shape[1], xbc_ref.shape[1], u_ref.shape[1]
    for c0 in range(0, nz, MXU_W):
        z_ref[:, c0:c0 + MXU_W] = mm(c0, MXU_W).astype(BF16)
    for c0 in range(0, nxbc, MXU_W):
        xbc_ref[:, c0:c0 + MXU_W] = mm(nz + c0, MXU_W).astype(BF16)
    for c0 in range(0, nu, MXU_W):
        y = mm(nz + nxbc + c0, MXU_W)
        u_ref[:, c0:c0 + MXU_W] = y.astype(BF16)
        for t in range(MXU_W // LANES):
            u_scr[c0 // LANES + t] = y[:, t * LANES:(t + 1) * LANES]
    dt_ref[...] = mm(nz + nxbc + nu, LANES)
    nchunk = u_scr.shape[1] // S5_Q
    for j in range(S5_Q):
        for t in range(u_scr.shape[0]):
            uj_ref[j, :, t * LANES:(t + 1) * LANES] = u_scr[t, pl.ds(j, nchunk, stride=S5_Q), :].astype(BF16)


def _ssm_inproj(rw, xall, y1, y2, route, prev_mods, mods, norm_g, w_in):
    D, tm = rw.D, rw.tm
    w = jnp.concatenate([w_in[:, 0:2560], w_in[:, 2592:3104], w_in[:, 2560:2592], jnp.zeros((D, LANES - 32), F32)],
                        axis=1).astype(BF16)
    row = lambda i: (i, 0)
    return pl.pallas_call(
        _ssm_inproj_kernel,
        grid=(rw.ntot,),
        in_specs=[pl.BlockSpec((tm, D), row), pl.BlockSpec((tm, D), row), pl.BlockSpec((tm, D), row),
                  pl.BlockSpec((tm, LANES), row), _mod_spec(rw, 5), pl.BlockSpec((1, D), lambda i: (0, 0)),
                  _mod_spec(rw, 0), _mod_spec(rw, 1), pl.BlockSpec((D, 3200), lambda i: (0, 0))],
        out_specs=[pl.BlockSpec((tm, D), row),
                   pl.BlockSpec((tm, 1024), row), pl.BlockSpec((tm, 1536), row), pl.BlockSpec((tm, 512), row),
                   pl.BlockSpec((S5_Q, tm // S5_Q, 512), lambda i: (0, i, 0)), pl.BlockSpec((tm, LANES), row)],
        out_shape=[jax.ShapeDtypeStruct((rw.rows, D), F32),
                   jax.ShapeDtypeStruct((rw.rows, 1024), BF16), jax.ShapeDtypeStruct((rw.rows, 1536), BF16),
                   jax.ShapeDtypeStruct((rw.rows, 512), BF16),
                   jax.ShapeDtypeStruct((S5_Q, rw.rows // S5_Q, 512), BF16),
                   jax.ShapeDtypeStruct((rw.rows, LANES), F32)],
        scratch_shapes=[pltpu.VMEM((tm, D), BF16), pltpu.VMEM((512 // LANES, tm, LANES), F32)],
        compiler_params=_cparams(1),
        name="ssm_inproj",
    )(xall, y1, y2, route, prev_mods, norm_g[None, :], mods, mods, w)


def _softplus(x):
    return jnp.maximum(x, 0.0) + jnp.log(1.0 + jnp.exp(-jnp.abs(x)))


def _conv_kernel(lat_tiles, tpb, cpb, x_ref, pv_ref, nx_ref, w_ref, b_ref, dtr_ref, dtb_ref, act_ref, dt_ref):
    i = pl.program_id(0)
    is_lat = i < lat_tiles
    pos = jnp.where(is_lat, i % tpb, (i - lat_tiles) % cpb)
    last_pos = jnp.where(is_lat, tpb - 1, cpb - 1)
    x = x_ref[...].astype(F32)
    tc = x.shape[0]
    prev_row = jnp.where(pos == 0, 0.0, pv_ref[...].astype(F32)[CONV_HALO - 1:CONV_HALO, :])
    next_row = jnp.where(pos == last_pos, 0.0, nx_ref[...].astype(F32)[0:1, :])
    row = lax.broadcasted_iota(jnp.int32, x.shape, 0)
    xm1 = jnp.where(row == 0, prev_row, pltpu.roll(x, 1, 0))
    xp1 = jnp.where(row == tc - 1, next_row, pltpu.roll(x, tc - 1, 0))
    y = w_ref[0:1, :] * xm1 + w_ref[1:2, :] * x + w_ref[2:3, :] * xp1 + b_ref[...]
    act_ref[...] = _silu(y).astype(BF16)
    sp = _softplus(dtr_ref[...] + dtb_ref[...])
    dt_ref[0] = sp
    dt_ref[1] = pltpu.roll(sp, LANES - 16, 1)


def _ssm_conv(rw, xbc, dtr, conv_w, conv_b, dt_bias):
    B, T, C = rw.B, rw.T, rw.C
    tc = CONV_TILE
    assert T % tc == 0 and C % tc == 0
    lat_tiles, tpb, cpb = (B * T) // tc, T // tc, C // tc
    ntiles = rw.rows // tc
    hpt = tc // CONV_HALO
    nhalo = rw.rows // CONV_HALO
    W = xbc.shape[1]
    dtb = jnp.concatenate([dt_bias.reshape(-1), jnp.zeros((LANES - 32,), F32)])[None, :]
    row = lambda i: (i, 0)
    return pl.pallas_call(
        functools.partial(_conv_kernel, lat_tiles, tpb, cpb),
        grid=(ntiles,),
        in_specs=[pl.BlockSpec((tc, W), row),
                  pl.BlockSpec((CONV_HALO, W), lambda i: (jnp.maximum(i * hpt - 1, 0), 0)),
                  pl.BlockSpec((CONV_HALO, W), lambda i: (jnp.minimum((i + 1) * hpt, nhalo - 1), 0)),
                  pl.BlockSpec((3, W), lambda i: (0, 0)), pl.BlockSpec((1, W), lambda i: (0, 0)),
                  pl.BlockSpec((tc, LANES), row), pl.BlockSpec((1, LANES), lambda i: (0, 0))],
        out_specs=[pl.BlockSpec((tc, W), row), pl.BlockSpec((2, tc, LANES), lambda i: (0, i, 0))],
        out_shape=[jax.ShapeDtypeStruct((rw.rows, W), BF16), jax.ShapeDtypeStruct((2, rw.rows, LANES), F32)],
        compiler_params=_cparams(1),
        name="ssm_conv",
    )(xbc, xbc, xbc, conv_w, conv_b[None, :], dtr, dtb)


def _ssd_kernel(nb, *refs):
    acts, dts = refs[0:2 * nb], refs[2 * nb:4 * nb]
    tri_ref, a_ref, yf_ref, yb_ref, hst = refs[4 * nb:]

    @pl.when(pl.program_id(0) == 0)
    def _():
        hst[...] = jnp.zeros_like(hst)

    for d, y_ref in enumerate((yf_ref, yb_ref)):
        for b in range(nb):
            _ssd_chunk(acts[d * nb + b], dts[d * nb + b], tri_ref[d], a_ref[d], y_ref.at[b], hst.at[d, b])


def _ssd_chunk(act_ref, dt_ref, tri, avec, y_ref, hst):
    q = SSD_CHUNK
    dt = dt_ref[...]
    da = dt * avec
    acs = jnp.dot(tri, da, preferred_element_type=F32, precision=lax.Precision.HIGHEST)
    tot = jnp.sum(da, axis=0, keepdims=True)
    acs_t = acs.T
    dt_t = dt.T
    eacs = jnp.exp(acs)
    wend = jnp.exp(tot - acs) * dt
    etot = jnp.exp(tot)
    mask = tri > 0.5
    left = lax.broadcasted_iota(jnp.int32, (q, LANES), 1) < HEAD_DIM
    left1 = lax.broadcasted_iota(jnp.int32, (1, LANES), 1) < HEAD_DIM
    for g in range(2):
        bg = act_ref[:, 1024 + g * 128:1152 + g * 128]
        cg = act_ref[:, 1280 + g * 128:1408 + g * 128]
        cb = _nt(cg, bg)
        hin = hst[:, g * 512:(g + 1) * 512]
        yoff = jnp.dot(cg, hin.astype(BF16), preferred_element_type=F32)
        xw, dec = [], []
        for pr in range(4):
            h_a = g * 8 + pr * 2
            h_b = h_a + 1
            c0 = h_a * HEAD_DIM
            x2 = act_ref[:, c0:c0 + LANES]
            outs = []
            for h in (h_a, h_b):
                seg = acs[:, h:h + 1] - acs_t[h:h + 1, :]
                w = cb * jnp.exp(jnp.where(mask, seg, NEG_INF)) * dt_t[h:h + 1, :]
                outs.append(jnp.dot(w.astype(BF16), x2, preferred_element_type=F32))
            yd = jnp.where(left, outs[0], outs[1])
            sc = jnp.where(left, eacs[:, h_a:h_a + 1], eacs[:, h_b:h_b + 1])
            y_ref[:, c0:c0 + LANES] = (yd + yoff[:, pr * LANES:(pr + 1) * LANES] * sc).astype(BF16)
            wsc = jnp.where(left, wend[:, h_a:h_a + 1], wend[:, h_b:h_b + 1])
            xw.append((x2.astype(F32) * wsc).astype(BF16))
            dec.append(jnp.where(left1, etot[:, h_a:h_a + 1], etot[:, h_b:h_b + 1]))
        snew = lax.dot_general(bg, jnp.concatenate(xw, axis=1), (((0,), (0,)), ((), ())),
                               preferred_element_type=F32)
        hst[:, g * 512:(g + 1) * 512] = hin * jnp.concatenate(dec, axis=1) + snew


def _ssd(rw, act, dt2, a_log):
    B, T, C = rw.B, rw.T, rw.C
    q = SSD_CHUNK
    nct, nlt = C // q, T // q
    ctx0 = (B * T) // q
    r = np.arange(q)
    tri = jnp.asarray(np.stack([r[None, :] <= r[:, None], r[None, :] >= r[:, None]]).astype(np.float32))
    avec = jnp.concatenate([-jnp.exp(a_log.astype(F32)), jnp.zeros((2, LANES - a_log.shape[1]), F32)], axis=1)[:, None, :]

    def lat(d, s):
        return jnp.clip(s - nct, 0, nlt - 1) if d == 0 else nlt - 1 - jnp.clip(s - nct, 0, nlt - 1)

    def blk(d, b, s):
        kc = s if d == 0 else nct - 1 - s
        return jnp.where(s < nct, ctx0 + b * nct + kc, b * nlt + lat(d, s))

    pairs = [(d, b) for d in range(2) for b in range(B)]
    aspec = lambda d, b: pl.BlockSpec((q, act.shape[1]), lambda s: (blk(d, b, s), 0))
    dspec = lambda d, b: pl.BlockSpec((None, q, LANES), lambda s: (d, blk(d, b, s), 0))
    yspec = lambda d: pl.BlockSpec((B, q, 1024), lambda s: (0, lat(d, s), 0))
    yf, yb = pl.pallas_call(
        functools.partial(_ssd_kernel, B),
        grid=(nct + nlt,),
        in_specs=[aspec(d, b) for d, b in pairs] + [dspec(d, b) for d, b in pairs]
        + [pl.BlockSpec((2, q, q), lambda s: (0, 0, 0)), pl.BlockSpec((2, 1, LANES), lambda s: (0, 0, 0))],
        out_specs=[yspec(0), yspec(1)],
        out_shape=[jax.ShapeDtypeStruct((B, T, 1024), BF16)] * 2,
        scratch_shapes=[pltpu.VMEM((2, B, q, 1024), F32)],
        compiler_params=_cparams(1),
        name="ssd_scan",
    )(*([act] * (2 * B)), *([dt2] * (2 * B)), tri, avec)
    return yf.reshape(B * T, 1024), yb.reshape(B * T, 1024)


def _cmul(ar, ai, br, bi):
    return ar * br - ai * bi, ar * bi + ai * br


def _s5_weight_kernel(lre_ref, lim_ref, ls_ref, bre_ref, bim_ref, cre_ref, cim_ref,
                      wsr_ref, wsi_ref, wor_ref, woi_ref, bt_ref, are_ref, aim_ref):
    lre, lim = lre_ref[...], lim_ref[...]
    step = jnp.exp(ls_ref[...])
    er, ei = lre * step, lim * step
    npow = 24
    p = lax.broadcasted_iota(jnp.int32, (1, npow, 1), 1).astype(F32)
    mag = jnp.exp(p * er)
    pre, pim = mag * jnp.cos(p * ei), mag * jnp.sin(p * ei)
    a_re, a_im = pre[:, 1:2, :], pim[:, 1:2, :]
    den = lre * lre + lim * lim
    q_re = ((a_re - 1.0) * lre + a_im * lim) / den
    q_im = (a_im * lre - (a_re - 1.0) * lim) / den
    bb_re, bb_im = _cmul(q_re, q_im, bre_ref[...], bim_ref[...])
    c_re, c_im = cre_ref[...], cim_ref[...]
    ws_r, ws_i, wo_r, wo_i, ca_r, ca_i = [], [], [], [], [], []
    for t in range(S5_Q):
        r, i = _cmul(bb_re, bb_im, pre[:, t:t + 1, :], pim[:, t:t + 1, :])
        ws_r.append(r)
        ws_i.append(i)
        r, i = _cmul(c_re, c_im, pre[:, t:t + 1, :], pim[:, t:t + 1, :])
        ca_r.append(r)
        ca_i.append(i)
        r, i = _cmul(c_re, c_im, pre[:, t + 1:t + 2, :], pim[:, t + 1:t + 2, :])
        wo_r.append(r)
        wo_i.append(-i)
    cat = lambda xs: jnp.concatenate(xs, axis=1)
    is_fwd = pl.program_id(0) < pl.num_programs(0) // 2

    def packed(blocks, reverse_fwd):
        w = jnp.where(is_fwd if reverse_fwd else jnp.logical_not(is_fwd), cat(blocks[::-1]), cat(blocks))
        w2 = jnp.concatenate([w, w], axis=2)
        g = lax.broadcasted_iota(jnp.int32, w2.shape, 0)
        ln = lax.broadcasted_iota(jnp.int32, w2.shape, 2)
        return jnp.where((g % 2 == 0) == (ln < w.shape[2]), w2, 0.0).astype(BF16)

    wsr_ref[...] = packed(ws_r, True)
    wsi_ref[...] = packed(ws_i, True)
    wor_ref[...] = packed(wo_r, False)
    woi_ref[...] = packed(wo_i, False)
    bdot = lambda a, b: lax.dot_general(a, b, (((2,), (2,)), ((0,), (0,))), preferred_element_type=F32,
                                        precision=lax.Precision.HIGHEST)
    kin = jnp.where(is_fwd, bdot(bb_re, cat(ca_r)) - bdot(bb_im, cat(ca_i)),
                    bdot(bb_re, cat(ca_r[::-1])) - bdot(bb_im, cat(ca_i[::-1])))
    qc = kin.shape[2]
    lane = lax.broadcasted_iota(jnp.int32, kin.shape, 2)
    rows = []
    for j in range(S5_Q):
        fwd = jnp.where(lane >= j * S5_GROUP, pltpu.roll(kin, j * S5_GROUP, 2), 0.0)
        back = (S5_Q - 1 - j) * S5_GROUP
        bwd = jnp.where(lane < qc - back, pltpu.roll(kin, (qc - back) % qc, 2), 0.0)
        rows.append(jnp.where(is_fwd, fwd, bwd))
    bt_ref[...] = cat(rows).astype(BF16)
    are_ref[...] = pre[:, S5_Q:S5_Q + 1, :]
    aim_ref[...] = pim[:, S5_Q:S5_Q + 1, :]


def _s5_weights(lam_re, lam_im, log_step, b_re, b_im, c_re, c_im):
    nd, ng, ns = lam_re.shape
    G = nd * ng
    ch = S5_GROUP
    gb = 8
    qc = S5_Q * ch
    f = lambda a: a.astype(F32)
    args = (f(lam_re).reshape(G, 1, ns), f(lam_im).reshape(G, 1, ns), f(log_step).reshape(G, 1, 1),
            f(b_re).reshape(G, ns, ch).transpose(0, 2, 1), f(b_im).reshape(G, ns, ch).transpose(0, 2, 1),
            f(c_re).reshape(G, ch, ns), f(c_im).reshape(G, ch, ns))
    spec = lambda a: pl.BlockSpec((gb,) + a.shape[1:], lambda i: (i, 0, 0))
    assert nd == 2 and (G // gb) % 2 == 0
    oshape = [jax.ShapeDtypeStruct((G, qc, 2 * ns), BF16)] * 4 + [jax.ShapeDtypeStruct((G, qc, qc), BF16)] \
        + [jax.ShapeDtypeStruct((G, 1, ns), F32)] * 2
    ws_r, ws_i, wo_r, wo_i, bt, a_re, a_im = pl.pallas_call(
        _s5_weight_kernel,
        grid=(G // gb,),
        in_specs=[spec(a) for a in args],
        out_specs=[pl.BlockSpec((gb,) + s.shape[1:], lambda i: (i, 0, 0)) for s in oshape],
        out_shape=oshape,
        compiler_params=_cparams(1),
        name="s5_weights",
    )(*args)
    by_dir = lambda w: w.reshape((nd, ng) + w.shape[1:])
    pair = lambda a: a.reshape(nd, ng // 2, 1, 2 * ns)
    return by_dir(bt), by_dir(ws_r), by_dir(ws_i), by_dir(wo_r), by_dir(wo_i), pair(a_re), pair(a_im)


S5_GB = LANES // S5_GROUP


def _s5_kernel(B, nct, nlt, uj_ref, perm_ref, bt_ref, wsr_ref, wsi_ref, wor_ref, woi_ref, are_ref, aim_ref, yj_ref,
               x_scr, y_scr, s_re, s_im):
    gb, npair, qc = S5_GB, S5_GB // 2, S5_Q * S5_GROUP
    lhs = jnp.concatenate([uj_ref[j] for j in range(S5_Q)], axis=1)
    for m in range(gb):
        x_scr[:, m * qc:(m + 1) * qc] = jnp.dot(lhs, perm_ref[:, m * qc:(m + 1) * qc],
                                                preferred_element_type=F32).astype(BF16)
    xg = lambda g: x_scr[:, g * qc:(g + 1) * qc]
    for d in range(2):
        for pr in range(npair):
            for dst, w_ref in ((s_re, wsr_ref), (s_im, wsi_ref)):
                dst[d, pr] = (jnp.dot(xg(2 * pr), w_ref[d, 2 * pr], preferred_element_type=F32)
                              + jnp.dot(xg(2 * pr + 1), w_ref[d, 2 * pr + 1], preferred_element_type=F32))
    chains = [(d, pr, b) for d in range(2) for pr in range(npair) for b in range(B)]
    coef = {(d, pr): (are_ref[d, pr], aim_ref[d, pr]) for d in range(2) for pr in range(npair)}
    ctx0 = B * nlt

    def body(s, carry):
        in_ctx = s < nct
        rows = {}
        for d in range(2):
            kc = s if d == 0 else nct - 1 - s
            kl = s - nct if d == 0 else nlt - 1 - (s - nct)
            for b in range(B):
                rows[(d, b)] = pl.ds(jnp.where(in_ctx, ctx0 + b * nct + kc, b * nlt + kl), 1)
        contrib = [(s_re[d, pr, rows[(d, b)], :], s_im[d, pr, rows[(d, b)], :]) for d, pr, b in chains]
        new = []
        for (d, pr, b), (hr, hi), (sr, si) in zip(chains, carry, contrib):
            ar, ai = coef[(d, pr)]
            s_re[d, pr, rows[(d, b)], :] = hr
            s_im[d, pr, rows[(d, b)], :] = hi
            new.append((ar * hr - ai * hi + sr, ar * hi + ai * hr + si))
        return tuple(new)

    zero = jnp.zeros((1, LANES), F32)
    lax.fori_loop(0, nct + nlt, body, tuple((zero, zero) for _ in chains))
    for g in range(gb):
        acc = jnp.dot(xg(g), bt_ref[0, g] + bt_ref[1, g], preferred_element_type=F32)
        for d in range(2):
            acc = (acc + _nt(s_re[d, g // 2].astype(BF16), wor_ref[d, g])
                   + _nt(s_im[d, g // 2].astype(BF16), woi_ref[d, g]))
        y_scr[:, g * qc:(g + 1) * qc] = acc.astype(BF16)
    for i in range(S5_Q):
        yj_ref[i] = _nt(y_scr[...], perm_ref[i * LANES:(i + 1) * LANES, :]).astype(BF16)


def _s5(rw, uj, weights):
    B, T, C = rw.B, rw.T, rw.C
    bt, ws_r, ws_i, wo_r, wo_i, a_re, a_im = weights
    ng = bt.shape[1]
    q, gb = S5_Q, S5_GB
    nct, nlt = C // q, T // q
    nrow = uj.shape[1]
    qc = q * S5_GROUP
    k = gb * qc
    idx = np.arange(k)
    j, m, c = idx // LANES, (idx % LANES) // S5_GROUP, idx % S5_GROUP
    perm = np.zeros((k, k), np.float32)
    perm[idx, m * qc + j * S5_GROUP + c] = 1.0
    once = dict(pipeline_mode=pl.Buffered(1))
    wspec = lambda n: pl.BlockSpec((2, gb, qc, n), lambda i: (0, i, 0, 0), **once)
    aspec = pl.BlockSpec((2, gb // 2, 1, LANES), lambda i: (0, i, 0, 0))
    return pl.pallas_call(
        functools.partial(_s5_kernel, B, nct, nlt),
        grid=(ng // gb,),
        in_specs=[pl.BlockSpec((q, nrow, LANES), lambda i: (0, 0, i), **once),
                  pl.BlockSpec((k, k), lambda i: (0, 0), **once),
                  wspec(qc), wspec(LANES), wspec(LANES), wspec(LANES), wspec(LANES), aspec, aspec],
        out_specs=pl.BlockSpec((q, nrow, LANES), lambda i: (0, 0, i)),
        out_shape=jax.ShapeDtypeStruct(uj.shape, BF16),
        scratch_shapes=[pltpu.VMEM((nrow, k), BF16), pltpu.VMEM((nrow, k), BF16),
                        pltpu.VMEM((2, gb // 2, nrow, LANES), F32), pltpu.VMEM((2, gb // 2, nrow, LANES), F32)],
        compiler_params=_cparams(1),
        name="s5_scan",
    )(uj, jnp.asarray(perm, BF16), bt, ws_r, ws_i, wo_r, wo_i, a_re, a_im)


def _gelu_tanh(x):
    return 0.5 * x * (1.0 + jnp.tanh(math.sqrt(2.0 / math.pi) * (x + 0.044715 * (x * x * x))))


def _ssm_outproj_kernel(y0_ref, y1_ref, xs_ref, z_ref, v_ref, u_ref, dsk_ref, nw_ref, s5d_ref, gw_ref, gb_ref,
                        x_ref, w_ref, g1_ref, gn_ref, sh2_ref, sc2_ref, wr_ref, br_ref, lt_ref,
                        xo_ref, h2_ref, rt_ref, cnt_ref, carry, v_scr):
    i = pl.program_id(0)
    y = y0_ref[...].astype(F32) + y1_ref[...].astype(F32) + dsk_ref[...] * xs_ref[...].astype(F32)
    y = _rms(y * _silu(z_ref[...].astype(F32))) * nw_ref[...]
    ntile = v_scr.shape[0]
    nchunk = v_scr.shape[1] // S5_Q
    for j in range(S5_Q):
        for t in range(ntile):
            v_scr[t, pl.ds(j, nchunk, stride=S5_Q), :] = v_ref[j, :, t * LANES:(t + 1) * LANES].astype(F32)
    s5_y = jnp.concatenate([v_scr[t] for t in range(ntile)], axis=1)
    v = _gelu_tanh(s5_y + s5d_ref[...] * u_ref[...].astype(F32))
    v = v * _sigmoid(jnp.dot(v.astype(BF16), gw_ref[...], preferred_element_type=F32) + gb_ref[...])
    mix = jnp.concatenate([y, v], axis=1).astype(BF16)
    yo = jnp.dot(mix, w_ref[...], preferred_element_type=F32)
    _post_mixer(i, x_ref[...], yo, g1_ref[...], gn_ref[...], sh2_ref[...], sc2_ref[...], wr_ref, br_ref, lt_ref,
                xo_ref, h2_ref, rt_ref, cnt_ref, carry)


def _ssm_outproj(rw, ssd_y, act, z, s5_y, u, d_skip, norm_w, s5_d, glu_w, glu_b, xall, w_out, mods, norm_ffn, wr, br):
    D, tm = rw.D, rw.tm
    ntiles = rw.nlat
    post_in, post_out = _post_specs(rw)
    row = lambda i: (i, 0)
    vec = lambda n: pl.BlockSpec((1, n), lambda i: (0, 0))
    dsk = jnp.repeat(d_skip.astype(F32), HEAD_DIM)[None, :]
    return pl.pallas_call(
        _ssm_outproj_kernel,
        grid=(ntiles,),
        in_specs=[pl.BlockSpec((tm, 1024), row), pl.BlockSpec((tm, 1024), row),
                  pl.BlockSpec((tm, 1024), row), pl.BlockSpec((tm, 1024), row),
                  pl.BlockSpec((S5_Q, tm // S5_Q, 512), lambda i: (0, i, 0)),
                  pl.BlockSpec((tm, 512), row), vec(1024), vec(1024), vec(512),
                  pl.BlockSpec((512, 512), lambda i: (0, 0)), vec(512),
                  pl.BlockSpec((tm, D), row), pl.BlockSpec((1536, D), lambda i: (0, 0)), _mod_spec(rw, 2)] + post_in,
        out_specs=post_out,
        out_shape=_post_shapes(ntiles * tm, D),
        scratch_shapes=[pltpu.VMEM((1, LANES), F32), pltpu.VMEM((512 // LANES, tm, LANES), F32)],
        compiler_params=_cparams(1),
        name="ssm_outproj_router",
    )(ssd_y[0], ssd_y[1], act, z, s5_y, u, dsk, norm_w[None, :], s5_d[None, :], glu_w.astype(BF16), glu_b[None, :],
      xall, w_out.astype(BF16), mods, norm_ffn[None, :], mods, mods, wr, br, _lower_tri(tm))


def kernel(x, c, ctx, c_ctx, mod_w, mod_b, norm_mix, norm_ffn, att_w_in, att_w_out, na_q_norm, na_k_norm, na_rel_bias, wa_q_norm, wa_k_norm, wa_sink, ssm_w_in, ssm_w_out, ssd_conv_w, ssd_conv_b, ssd_dt_bias, ssd_a_log, ssd_d, ssd_norm, s5_lambda_re, s5_lambda_im, s5_log_step, s5_b_re, s5_b_im, s5_c_re, s5_c_im, s5_d, s5_glu_w, s5_glu_b, moe_w_group, moe_b_group, moe_w_expert, moe_b_expert, moe_w13, moe_w2):
    B, T, D = x.shape
    C = ctx.shape[1]
    rw = _Rows(B, T, C, D, ROW_TILE)
    xl = x.reshape(B * T, D)
    xc = ctx.reshape(B * C, D)
    cm = jnp.concatenate([c, c_ctx[None, :], jnp.zeros((8 - B - 1, D), F32)], axis=0)
    mods = _modulation(cm, mod_w, mod_b)
    mods = mods.reshape(mods.shape[0], 8, 1, 6 * D)

    m0 = mods[0]
    qkv = _att_inproj(rw, xl, xc, m0, norm_mix[0], att_w_in[0], na_q_norm[0], na_k_norm[0], wa_q_norm[0],
                      wa_k_norm[0])
    na = _na_attention(rw, qkv, na_rel_bias[0])
    wa = _wa_attention(rw, qkv, wa_sink[0])
    cx = _ctx_attention(rw, qkv, wa_sink[0])
    wr, br = _router_weights(moe_w_group[0], moe_b_group[0], moe_w_expert[0], moe_b_expert[0])
    xall, h2, route, counts = _att_outproj(rw, na, wa, cx, xl, xc, att_w_out[0], m0, norm_ffn[0], wr, br)
    y1, y2 = _moe(h2, route, counts, moe_w13, moe_w2, 0)

    m1 = mods[1]
    xall, z, xbc, u, uj, dtr = _ssm_inproj(rw, xall, y1, y2, route, m0, m1, norm_mix[1], ssm_w_in[0])
    act, dt2 = _ssm_conv(rw, xbc, dtr, ssd_conv_w[0], ssd_conv_b[0], ssd_dt_bias[0])
    ssd_y = _ssd(rw, act, dt2, ssd_a_log[0])
    s5_w = _s5_weights(s5_lambda_re[0], s5_lambda_im[0], s5_log_step[0], s5_b_re[0], s5_b_im[0], s5_c_re[0],
                       s5_c_im[0])
    s5_y = _s5(rw, uj, s5_w)
    wr, br = _router_weights(moe_w_group[1], moe_b_group[1], moe_w_expert[1], moe_b_expert[1])
    xlat, h2, route, counts = _ssm_outproj(rw, ssd_y, act, z, s5_y, u, ssd_d[0], ssd_norm[0], s5_d[0], s5_glu_w[0],
                                           s5_glu_b[0], xall, ssm_w_out[0], m1, norm_ffn[1], wr, br)
    y1, y2 = _moe(h2, route, counts, moe_w13, moe_w2, 1)
    out = _combine(rw, rw.nlat, xlat, y1, y2, route, m1)
    return out.reshape(B, T, D)
```
